```python
import math
import jax
import jax.numpy as jnp
from jax import lax
import numpy as np

D_MODEL = 2048
BATCH = 8
SEQ = 4096
DEPTH = 4

N_MIXERS = 4
GROUP_WIDTH = D_MODEL // N_MIXERS
HEAD_DIM = 128
N_HEADS = GROUP_WIDTH // HEAD_DIM
MIX_WIDTH = N_MIXERS * GROUP_WIDTH
Q_LORA = 512
KV_LORA = 512
QK_NOPE = 128
QK_ROPE = 64
V_HEAD = HEAD_DIM
MLA_QK_DIM = QK_NOPE + QK_ROPE
DILATED_PAIRS = ((128, 1), (512, 4), (2048, 16))
FORGET_BIAS_INIT = 2.0
BLOCK = 128
ROPE_THETA = 10000.0
FFN_HIDDEN = -(-8 * D_MODEL // (3 * 256)) * 256
EPS = 1e-6
NEG_INF = -1e30
IN_SPLITS = ((Q_LORA, KV_LORA, QK_ROPE)
             + (GROUP_WIDTH,) * 3
             + (GROUP_WIDTH,) * 3 + (N_HEADS,)
             + (GROUP_WIDTH,) * 3)
IN_WIDTH = sum(IN_SPLITS)

kernel_name = "hybrid_parallel_heads_mla_dilated_fox_stickbreak"


def rms_norm(x, gain):
    xf = x.astype(jnp.float32)
    y = xf * lax.rsqrt(jnp.mean(jnp.square(xf), axis=-1, keepdims=True) + EPS)
    return (y * gain.astype(jnp.float32)).astype(x.dtype)


def rope_tables(seq, dim):
    pos = jnp.arange(seq, dtype=jnp.float32)
    inv_freq = ROPE_THETA ** (-jnp.arange(0, dim, 2, dtype=jnp.float32) / dim)
    ang = pos[:, None] * inv_freq[None, :]
    return jnp.cos(ang), jnp.sin(ang)


def apply_rope(x, cos, sin):
    xf = x.astype(jnp.float32)
    x1, x2 = jnp.split(xf, 2, axis=-1)
    c, s = cos[:, None, :], sin[:, None, :]
    return jnp.concatenate([x1 * c - x2 * s, x1 * s + x2 * c], axis=-1).astype(x.dtype)


def split_heads(t):
    b, s, _ = t.shape
    return t.reshape(b, s, N_HEADS, -1)


def to_query_blocks(t):
    b, h, s = t.shape[:3]
    t = t.reshape((b, h, s // BLOCK, BLOCK) + t.shape[3:])
    return jnp.moveaxis(t, 2, 0)


def from_query_blocks(o):
    nb, b, h, _, d = o.shape
    return jnp.moveaxis(o, 0, 2).reshape(b, h, nb * BLOCK, d).transpose(0, 2, 1, 3)


def causal_softmax_attention(q, k, v, scale, cum_log_forget=None):
    s_len = q.shape[1]
    qh, kh, vh = (t.transpose(0, 2, 1, 3) for t in (q, k, v))
    key_pos = jnp.arange(s_len)
    block_idx = jnp.arange(s_len // BLOCK)
    if cum_log_forget is None:
        xs = (to_query_blocks(qh), block_idx)
    else:
        xs = (to_query_blocks(qh), block_idx, to_query_blocks(cum_log_forget))

    def body(args):
        qi, i = args[0], args[1]
        s = jnp.einsum('bhqd,bhkd->bhqk', qi, kh).astype(jnp.float32) * scale
        if cum_log_forget is not None:
            s = s + args[2][..., None] - cum_log_forget[:, :, None, :]
        q_pos = i * BLOCK + jnp.arange(BLOCK)
        s = jnp.where(key_pos[None, :] <= q_pos[:, None], s, NEG_INF)
        p = jax.nn.softmax(s, axis=-1).astype(vh.dtype)
        return jnp.einsum('bhqk,bhkd->bhqd', p, vh)

    return from_query_blocks(lax.map(body, xs))


def mla_attention(q_lat, kv_lat, k_rope, q_norm, w_uq, kv_norm, w_ukv, cos, sin):
    b, s, _ = q_lat.shape
    q = (rms_norm(q_lat, q_norm) @ w_uq).reshape(b, s, N_HEADS, MLA_QK_DIM)
    q = jnp.concatenate([q[..., :QK_NOPE], apply_rope(q[..., QK_NOPE:], cos, sin)], axis=-1)
    kv = (rms_norm(kv_lat, kv_norm) @ w_ukv).reshape(b, s, N_HEADS, QK_NOPE + V_HEAD)
    k_nope, v = kv[..., :QK_NOPE], kv[..., QK_NOPE:]
    k_pe = apply_rope(k_rope[:, :, None, :], cos, sin)
    k = jnp.concatenate([k_nope, jnp.broadcast_to(k_pe, (b, s, N_HEADS, QK_ROPE))], axis=-1)
    return causal_softmax_attention(q, k, v, MLA_QK_DIM ** -0.5)


def banded_window_attention(q, k, v, steps, scale):
    n, h, l, d = q.shape
    pad = (-l) % BLOCK
    cfg = ((0, 0), (0, 0), (0, pad), (0, 0))
    q, k, v = jnp.pad(q, cfg), jnp.pad(k, cfg), jnp.pad(v, cfg)
    nb = (l + pad) // BLOCK
    qb = q.reshape(n, h, nb, BLOCK, d)

    def with_prev(t):
        tb = t.reshape(n, h, nb, BLOCK, d)
        prev = jnp.concatenate([jnp.zeros_like(tb[:, :, :1]), tb[:, :, :-1]], axis=2)
        return jnp.concatenate([prev, tb], axis=3)

    kb, vb = with_prev(k), with_prev(v)
    s = jnp.einsum('nhbqd,nhbkd->nhbqk', qb, kb).astype(jnp.float32) * scale
    q_idx = jnp.arange(BLOCK)
    k_idx = jnp.arange(2 * BLOCK)
    dist = BLOCK + q_idx[:, None] - k_idx[None, :]
    key_pos = (jnp.arange(nb)[:, None] - 1) * BLOCK + k_idx[None, :]
    valid = ((dist >= 0) & (dist <= steps))[None] & (key_pos >= 0)[:, None, :]
    s = jnp.where(valid, s, NEG_INF)
    lse = jax.nn.logsumexp(s, axis=-1)
    p = jnp.exp(s - lse[..., None]).astype(v.dtype)
    out = jnp.einsum('nhbqk,nhbkd->nhbqd', p, vb).reshape(n, h, nb * BLOCK, d)[:, :, :l]
    return out, lse.reshape(n, h, nb * BLOCK)[:, :, :l]


def dilated_window_attention(q, k, v):
    b, s, h, d = q.shape
    scale = d ** -0.5
    outs, lses = [], []
    for window, dilation in DILATED_PAIRS:
        l = s // dilation

        def by_residue(t):
            return t.reshape(b, l, dilation, h, d).transpose(0, 2, 3, 1, 4).reshape(b * dilation, h, l, d)

        o, lse = banded_window_attention(by_residue(q), by_residue(k), by_residue(v),
                                         window // dilation, scale)
        outs.append(o.reshape(b, dilation, h, l, d).transpose(0, 3, 1, 2, 4).reshape(b, s, h, d))
        lses.append(lse.reshape(b, dilation, h, l).transpose(0, 3, 1, 2).reshape(b, s, h))
    weights = jax.nn.softmax(jnp.stack(lses), axis=0).astype(q.dtype)
    return jnp.einsum('gbsh,gbshd->bshd', weights, jnp.stack(outs))


def forgetting_attention(q, k, v, f_logit, f_bias):
    log_f = jax.nn.log_sigmoid((f_logit + f_bias).astype(jnp.float32))
    cum = jnp.cumsum(log_f, axis=1).transpose(0, 2, 1)
    return causal_softmax_attention(q, k, v, HEAD_DIM ** -0.5, cum)


def stick_breaking_attention(q, k, v):
    s_len, d = q.shape[1], q.shape[-1]
    scale = d ** -0.5
    qh, kh, vh = (t.transpose(0, 2, 1, 3) for t in (q, k, v))
    key_pos = jnp.arange(s_len)

    def body(args):
        qi, i = args
        z = jnp.einsum('bhqd,bhkd->bhqk', qi, kh).astype(jnp.float32) * scale
        q_pos = i * BLOCK + jnp.arange(BLOCK)
        past = key_pos[None, :] < q_pos[:, None]
        log_keep = jnp.where(past, jax.nn.log_sigmoid(-z), 0.0)
        between = lax.cumsum(log_keep, axis=3, reverse=True) - log_keep
        a = jnp.where(past, jnp.exp(jax.nn.log_sigmoid(z) + between), 0.0)
        return jnp.einsum('bhqk,bhkd->bhqd', a.astype(vh.dtype), vh)

    xs = (to_query_blocks(qh), jnp.arange(s_len // BLOCK))
    return from_query_blocks(lax.map(body, xs))


def hybrid_mixer(h, w_in, mla_q_norm, w_uq, mla_kv_norm, w_ukv, fox_forget_bias,
                 group_norm, w_out, rope_full, rope_mla):
    b, s, _ = h.shape
    proj = h @ w_in
    offsets = np.cumsum(IN_SPLITS)[:-1].tolist()
    (q_lat, kv_lat, k_rope, q_b, k_b, v_b, q_c, k_c, v_c, f_c, q_d, k_d, v_d) = \
        jnp.split(proj, offsets, axis=-1)
    cos, sin = rope_full
    out_a = mla_attention(q_lat, kv_lat, k_rope, mla_q_norm, w_uq, mla_kv_norm, w_ukv, *rope_mla)
    out_b = dilated_window_attention(apply_rope(split_heads(q_b), cos, sin),
                                     apply_rope(split_heads(k_b), cos, sin), split_heads(v_b))
    out_c = forgetting_attention(split_heads(q_c), split_heads(k_c), split_heads(v_c),
                                 f_c, fox_forget_bias)
    out_d = stick_breaking_attention(split_heads(q_d), split_heads(k_d), split_heads(v_d))
    groups = jnp.stack([o.reshape(b, s, GROUP_WIDTH) for o in (out_a, out_b, out_c, out_d)],
                       axis=2)
    groups = rms_norm(groups, group_norm.reshape(N_MIXERS, GROUP_WIDTH))
    return groups.reshape(b, s, MIX_WIDTH) @ w_out


def _fwd_setup_inputs(seed: int = 0) -> dict:
    key = jax.random.key(seed)
    ks = jax.random.split(key, 16)

    def w(k, shape, fan_in):
        return jax.random.normal(k, shape, jnp.float32) * fan_in ** -0.5

    def gain(k, shape):
        return 1.0 + 0.05 * jax.random.normal(k, shape, jnp.float32)

    return {
        "x": jax.random.normal(ks[0], (BATCH, SEQ, D_MODEL), jnp.float32),
        "attn_norm": gain(ks[1], (DEPTH, D_MODEL)),
        "w_in": w(ks[2], (DEPTH, D_MODEL, IN_WIDTH), D_MODEL),
        "mla_q_norm": gain(ks[3], (DEPTH, Q_LORA)),
        "w_uq": w(ks[4], (DEPTH, Q_LORA, N_HEADS * MLA_QK_DIM), Q_LORA),
        "mla_kv_norm": gain(ks[5], (DEPTH, KV_LORA)),
        "w_ukv": w(ks[6], (DEPTH, KV_LORA, N_HEADS * (QK_NOPE + V_HEAD)), KV_LORA),
        "fox_forget_bias": FORGET_BIAS_INIT + 0.1 * jax.random.normal(ks[7], (DEPTH, N_HEADS), jnp.float32),
        "group_norm": gain(ks[8], (DEPTH, MIX_WIDTH)),
        "w_out": w(ks[9], (DEPTH, MIX_WIDTH, D_MODEL), MIX_WIDTH),
        "ffn_norm": gain(ks[10], (DEPTH, D_MODEL)),
        "w_gate": w(ks[11], (DEPTH, D_MODEL, FFN_HIDDEN), D_MODEL),
        "w_up": w(ks[12], (DEPTH, D_MODEL, FFN_HIDDEN), D_MODEL),
        "w_down": w(ks[13], (DEPTH, FFN_HIDDEN, D_MODEL), FFN_HIDDEN),
        "final_norm": gain(ks[14], (D_MODEL,)),
    }


def _fwd_reference(x, attn_norm, w_in, mla_q_norm, w_uq, mla_kv_norm, w_ukv, fox_forget_bias,
              group_norm, w_out, ffn_norm, w_gate, w_up, w_down, final_norm):
    s_len = x.shape[1]
    rope_full = rope_tables(s_len, HEAD_DIM)
    rope_mla = rope_tables(s_len, QK_ROPE)
    for l in range(DEPTH):
        x = x + hybrid_mixer(rms_norm(x, attn_norm[l]), w_in[l], mla_q_norm[l], w_uq[l],
                             mla_kv_norm[l], w_ukv[l], fox_forget_bias[l], group_norm[l],
                             w_out[l], rope_full, rope_mla)
        h = rms_norm(x, ffn_norm[l])
        x = x + (jax.nn.silu(h @ w_gate[l]) * (h @ w_up[l])) @ w_down[l]
    return rms_norm(x, final_norm)


import jax as _jax
import jax.numpy as _jnp

TWIN_FORMAT = 'train_step'
FWD_PARAMS = ['x', 'attn_norm', 'w_in', 'mla_q_norm', 'w_uq', 'mla_kv_norm', 'w_ukv', 'fox_forget_bias', 'group_norm', 'w_out', 'ffn_norm', 'w_gate', 'w_up', 'w_down', 'final_norm']
TWIN_WEIGHTS = ['attn_norm', 'w_in', 'mla_q_norm', 'w_uq', 'mla_kv_norm', 'w_ukv', 'fox_forget_bias', 'group_norm', 'w_out', 'ffn_norm', 'w_gate', 'w_up', 'w_down', 'final_norm']
TWIN_DIFF_INPUT = 'x'
TWIN_INPUTS = ['x', 'attn_norm', 'w_in', 'mla_q_norm', 'w_uq', 'mla_kv_norm', 'w_ukv', 'fox_forget_bias', 'group_norm', 'w_out', 'ffn_norm', 'w_gate', 'w_up', 'w_down', 'final_norm', 'loss_target', 'm_attn_norm', 'm_w_in', 'm_mla_q_norm', 'm_w_uq', 'm_mla_kv_norm', 'm_w_ukv', 'm_fox_forget_bias', 'm_group_norm', 'm_w_out', 'm_ffn_norm', 'm_w_gate', 'm_w_up', 'm_w_down', 'm_final_norm', 'v_attn_norm', 'v_w_in', 'v_mla_q_norm', 'v_w_uq', 'v_mla_kv_norm', 'v_w_ukv', 'v_fox_forget_bias', 'v_group_norm', 'v_w_out', 'v_ffn_norm', 'v_w_gate', 'v_w_up', 'v_w_down', 'v_final_norm']
TWIN_OUTPUTS = ['loss', 'grad_x', 'grad_attn_norm', 'grad_w_in', 'grad_mla_q_norm', 'grad_w_uq', 'grad_mla_kv_norm', 'grad_w_ukv', 'grad_fox_forget_bias', 'grad_group_norm', 'grad_w_out', 'grad_ffn_norm', 'grad_w_gate', 'grad_w_up', 'grad_w_down', 'grad_final_norm', 'delta_attn_norm', 'delta_w_in', 'delta_mla_q_norm', 'delta_w_uq', 'delta_mla_kv_norm', 'delta_w_ukv', 'delta_fox_forget_bias', 'delta_group_norm', 'delta_w_out', 'delta_ffn_norm', 'delta_w_gate', 'delta_w_up', 'delta_w_down', 'delta_final_norm', 'new_m_attn_norm', 'new_m_w_in', 'new_m_mla_q_norm', 'new_m_w_uq', 'new_m_mla_kv_norm', 'new_m_w_ukv', 'new_m_fox_forget_bias', 'new_m_group_norm', 'new_m_w_out', 'new_m_ffn_norm', 'new_m_w_gate', 'new_m_w_up', 'new_m_w_down', 'new_m_final_norm', 'new_v_attn_norm', 'new_v_w_in', 'new_v_mla_q_norm', 'new_v_w_uq', 'new_v_mla_kv_norm', 'new_v_w_ukv', 'new_v_fox_forget_bias', 'new_v_group_norm', 'new_v_w_out', 'new_v_ffn_norm', 'new_v_w_gate', 'new_v_w_up', 'new_v_w_down', 'new_v_final_norm']
TWIN_LEAF_KINDS = {'loss': 'loss', 'grad_x': 'grad_x', 'grad_attn_norm': 'grad_w', 'grad_w_in': 'grad_w', 'grad_mla_q_norm': 'grad_w', 'grad_w_uq': 'grad_w', 'grad_mla_kv_norm': 'grad_w', 'grad_w_ukv': 'grad_w', 'grad_fox_forget_bias': 'grad_w', 'grad_group_norm': 'grad_w', 'grad_w_out': 'grad_w', 'grad_ffn_norm': 'grad_w', 'grad_w_gate': 'grad_w', 'grad_w_up': 'grad_w', 'grad_w_down': 'grad_w', 'grad_final_norm': 'grad_w', 'delta_attn_norm': 'delta_w', 'delta_w_in': 'delta_w', 'delta_mla_q_norm': 'delta_w', 'delta_w_uq': 'delta_w', 'delta_mla_kv_norm': 'delta_w', 'delta_w_ukv': 'delta_w', 'delta_fox_forget_bias': 'delta_w', 'delta_group_norm': 'delta_w', 'delta_w_out': 'delta_w', 'delta_ffn_norm': 'delta_w', 'delta_w_gate': 'delta_w', 'delta_w_up': 'delta_w', 'delta_w_down': 'delta_w', 'delta_final_norm': 'delta_w', 'new_m_attn_norm': 'new_m', 'new_m_w_in': 'new_m', 'new_m_mla_q_norm': 'new_m', 'new_m_w_uq': 'new_m', 'new_m_mla_kv_norm': 'new_m', 'new_m_w_ukv': 'new_m', 'new_m_fox_forget_bias': 'new_m', 'new_m_group_norm': 'new_m', 'new_m_w_out': 'new_m', 'new_m_ffn_norm': 'new_m', 'new_m_w_gate': 'new_m', 'new_m_w_up': 'new_m', 'new_m_w_down': 'new_m', 'new_m_final_norm': 'new_m', 'new_v_attn_norm': 'new_v', 'new_v_w_in': 'new_v', 'new_v_mla_q_norm': 'new_v', 'new_v_w_uq': 'new_v', 'new_v_mla_kv_norm': 'new_v', 'new_v_w_ukv': 'new_v', 'new_v_fox_forget_bias': 'new_v', 'new_v_group_norm': 'new_v', 'new_v_w_out': 'new_v', 'new_v_ffn_norm': 'new_v', 'new_v_w_gate': 'new_v', 'new_v_w_up': 'new_v', 'new_v_w_down': 'new_v', 'new_v_final_norm': 'new_v'}


def _forward(args):
    return _fwd_reference(*[args[k] for k in FWD_PARAMS])


def _output_shape():
    out = _jax.eval_shape(lambda: _forward(_fwd_setup_inputs(0)))
    return out.shape, out.dtype

N_MICROBATCH = 1
ADAM_LR = 0.001
ADAM_B1 = 0.9
ADAM_B2 = 0.999
ADAM_EPS = 1e-08
ADAM_WD = 0.01
ADAM_STEP = 10
PER_EXAMPLE_BATCH_AXIS = {'x': 0, 'loss_target': 0}
SHARED_INPUTS = []
_WEIGHT_DTYPES = {'attn_norm': _jnp.float32, 'w_in': _jnp.float32, 'mla_q_norm': _jnp.float32, 'w_uq': _jnp.float32, 'mla_kv_norm': _jnp.float32, 'w_ukv': _jnp.float32, 'fox_forget_bias': _jnp.float32, 'group_norm': _jnp.float32, 'w_out': _jnp.float32, 'ffn_norm': _jnp.float32, 'w_gate': _jnp.float32, 'w_up': _jnp.float32, 'w_down': _jnp.float32, 'final_norm': _jnp.float32}
MOMENT_SCALE = {'attn_norm': 1.188910e-01, 'w_in': 7.232073e-02, 'mla_q_norm': 4.295294e-02, 'w_uq': 2.994680e-02, 'mla_kv_norm': 1.785435e-01, 'w_ukv': 1.273177e-01, 'fox_forget_bias': 2.964069e-01, 'group_norm': 1.149787e-01, 'w_out': 1.186552e-01, 'ffn_norm': 4.322756e-02, 'w_gate': 1.811349e-02, 'w_up': 1.916485e-02, 'w_down': 3.167648e-02, 'final_norm': 1.616302e+01}


def _to_microbatches(a, axis):
    t = _jnp.moveaxis(a, axis, 0)
    t = t.reshape((N_MICROBATCH, t.shape[0] // N_MICROBATCH) + t.shape[1:])
    return _jnp.moveaxis(t, 1, axis + 1)


def setup_inputs(seed: int = 0) -> dict:
    inp = _fwd_setup_inputs(seed)
    key = _jax.random.fold_in(_jax.random.key(seed), 7919)
    shape, _ = _output_shape()
    out = dict(inp)
    out["loss_target"] = _jax.random.normal(_jax.random.fold_in(key, 0), shape, _jnp.float32)
    for i, name in enumerate(TWIN_WEIGHTS):
        w = inp[name].astype(_jnp.float32)
        if MOMENT_SCALE is None:
            s = _jnp.sqrt(_jnp.mean(_jnp.square(w)) + 1e-30)
        else:
            s = MOMENT_SCALE[name]
        km, kv = _jax.random.split(_jax.random.fold_in(key, i + 1))
        out[name] = w
        out["m_" + name] = s * _jax.random.normal(km, w.shape, _jnp.float32)
        out["v_" + name] = (s * s) * _jax.random.uniform(kv, w.shape, _jnp.float32, 0.5, 1.5)
    if N_MICROBATCH > 1:
        for name, axis in PER_EXAMPLE_BATCH_AXIS.items():
            out[name] = _to_microbatches(out[name], axis)
    return {'x': out['x'], 'attn_norm': out['attn_norm'], 'w_in': out['w_in'], 'mla_q_norm': out['mla_q_norm'], 'w_uq': out['w_uq'], 'mla_kv_norm': out['mla_kv_norm'], 'w_ukv': out['w_ukv'], 'fox_forget_bias': out['fox_forget_bias'], 'group_norm': out['group_norm'], 'w_out': out['w_out'], 'ffn_norm': out['ffn_norm'], 'w_gate': out['w_gate'], 'w_up': out['w_up'], 'w_down': out['w_down'], 'final_norm': out['final_norm'], 'loss_target': out['loss_target'], 'm_attn_norm': out['m_attn_norm'], 'm_w_in': out['m_w_in'], 'm_mla_q_norm': out['m_mla_q_norm'], 'm_w_uq': out['m_w_uq'], 'm_mla_kv_norm': out['m_mla_kv_norm'], 'm_w_ukv': out['m_w_ukv'], 'm_fox_forget_bias': out['m_fox_forget_bias'], 'm_group_norm': out['m_group_norm'], 'm_w_out': out['m_w_out'], 'm_ffn_norm': out['m_ffn_norm'], 'm_w_gate': out['m_w_gate'], 'm_w_up': out['m_w_up'], 'm_w_down': out['m_w_down'], 'm_final_norm': out['m_final_norm'], 'v_attn_norm': out['v_attn_norm'], 'v_w_in': out['v_w_in'], 'v_mla_q_norm': out['v_mla_q_norm'], 'v_w_uq': out['v_w_uq'], 'v_mla_kv_norm': out['v_mla_kv_norm'], 'v_w_ukv': out['v_w_ukv'], 'v_fox_forget_bias': out['v_fox_forget_bias'], 'v_group_norm': out['v_group_norm'], 'v_w_out': out['v_w_out'], 'v_ffn_norm': out['v_ffn_norm'], 'v_w_gate': out['v_w_gate'], 'v_w_up': out['v_w_up'], 'v_w_down': out['v_w_down'], 'v_final_norm': out['v_final_norm']}


def _loss(weights, diff, rest, loss_target):
    with _jax.named_scope("forward"):
        args = {**rest, TWIN_DIFF_INPUT: diff, **{k: w.astype(_WEIGHT_DTYPES[k]) for k, w in weights.items()}}
        y = _forward(args)
    with _jax.named_scope("loss_head"):
        err = _jnp.square(y.astype(_jnp.float32) - loss_target)
        return 0.5 * _jnp.sum(_jnp.mean(err, axis=-1)) if err.ndim else 0.5 * err


def _adamw(w, g, m, v):
    m = ADAM_B1 * m + (1.0 - ADAM_B1) * g
    v = ADAM_B2 * v + (1.0 - ADAM_B2) * _jnp.square(g)
    m_hat = m / (1.0 - ADAM_B1 ** ADAM_STEP)
    v_hat = v / (1.0 - ADAM_B2 ** ADAM_STEP)
    delta = -ADAM_LR * (m_hat / (_jnp.sqrt(v_hat) + ADAM_EPS) + ADAM_WD * w)
    return delta, m, v


def reference(x, attn_norm, w_in, mla_q_norm, w_uq, mla_kv_norm, w_ukv, fox_forget_bias, group_norm, w_out, ffn_norm, w_gate, w_up, w_down, final_norm, loss_target, m_attn_norm, m_w_in, m_mla_q_norm, m_w_uq, m_mla_kv_norm, m_w_ukv, m_fox_forget_bias, m_group_norm, m_w_out, m_ffn_norm, m_w_gate, m_w_up, m_w_down, m_final_norm, v_attn_norm, v_w_in, v_mla_q_norm, v_w_uq, v_mla_kv_norm, v_w_ukv, v_fox_forget_bias, v_group_norm, v_w_out, v_ffn_norm, v_w_gate, v_w_up, v_w_down, v_final_norm):
    given = dict(x=x, attn_norm=attn_norm, w_in=w_in, mla_q_norm=mla_q_norm, w_uq=w_uq, mla_kv_norm=mla_kv_norm, w_ukv=w_ukv, fox_forget_bias=fox_forget_bias, group_norm=group_norm, w_out=w_out, ffn_norm=ffn_norm, w_gate=w_gate, w_up=w_up, w_down=w_down, final_norm=final_norm, loss_target=loss_target, m_attn_norm=m_attn_norm, m_w_in=m_w_in, m_mla_q_norm=m_mla_q_norm, m_w_uq=m_w_uq, m_mla_kv_norm=m_mla_kv_norm, m_w_ukv=m_w_ukv, m_fox_forget_bias=m_fox_forget_bias, m_group_norm=m_group_norm, m_w_out=m_w_out, m_ffn_norm=m_ffn_norm, m_w_gate=m_w_gate, m_w_up=m_w_up, m_w_down=m_w_down, m_final_norm=m_final_norm, v_attn_norm=v_attn_norm, v_w_in=v_w_in, v_mla_q_norm=v_mla_q_norm, v_w_uq=v_w_uq, v_mla_kv_norm=v_mla_kv_norm, v_w_ukv=v_w_ukv, v_fox_forget_bias=v_fox_forget_bias, v_group_norm=v_group_norm, v_w_out=v_w_out, v_ffn_norm=v_ffn_norm, v_w_gate=v_w_gate, v_w_up=v_w_up, v_w_down=v_w_down, v_final_norm=v_final_norm)
    weights = {n: given[n] for n in TWIN_WEIGHTS}
    shared = {n: given[n] for n in SHARED_INPUTS}
    per_example = {n: given[n] for n in ['x']}
    grad_fn = _jax.value_and_grad(_loss, argnums=(0, 1))

    def one_microbatch(ex, loss_target):
        ex = dict(ex)
        diff = ex.pop(TWIN_DIFF_INPUT)
        return grad_fn(weights, diff, {**shared, **ex}, loss_target)

    if N_MICROBATCH == 1:
        loss, (grad_w, grad_x) = one_microbatch(per_example, given["loss_target"])
    else:
        def body(carry, xs):
            loss_sum, grad_sum = carry
            l_k, (gw_k, gx_k) = one_microbatch(xs[0], xs[1])
            with _jax.named_scope("update"):
                return (loss_sum + l_k, _jax.tree.map(_jnp.add, grad_sum, gw_k)), gx_k

        init = (_jnp.zeros((), _jnp.float32), _jax.tree.map(_jnp.zeros_like, weights))
        (loss, grad_w), grad_x = _jax.lax.scan(body, init, (per_example, given["loss_target"]))
    with _jax.named_scope("update"):
        delta_w, new_m, new_v = {}, {}, {}
        for n in TWIN_WEIGHTS:
            delta_w[n], new_m[n], new_v[n] = _adamw(weights[n], grad_w[n], given["m_" + n], given["v_" + n])
    return (loss, grad_x, *[grad_w[n] for n in TWIN_WEIGHTS], *[delta_w[n] for n in TWIN_WEIGHTS],
            *[new_m[n] for n in TWIN_WEIGHTS], *[new_v[n] for n in TWIN_WEIGHTS])
```

```python
import functools
import math

import numpy as np
import jax
import jax.numpy as jnp
from jax import lax
from jax.experimental import pallas as pl
from jax.experimental.pallas import tpu as pltpu

F32 = jnp.float32
BF16 = jnp.bfloat16
NDEV = 8
LANE = 128
HEAD = 128
Q_LORA = 512
KV_LORA = 512
QK_ROPE = 64
DILATED_PAIRS = ((128, 1), (512, 4), (2048, 16))
ROPE_THETA = 10000.0
EPS = 1e-6
NEG = -1e30
TQ = 256
TK = 128
VMEM_LIMIT = 48 * 1024 * 1024
ADAM_LR, ADAM_B1, ADAM_B2, ADAM_EPS, ADAM_WD, ADAM_STEP = 0.001, 0.9, 0.999, 1e-08, 0.01, 10
MESH = pl.DeviceIdType.MESH
ANY = pl.BlockSpec(memory_space=pl.ANY)


def _cp(*sem):
    return pltpu.CompilerParams(dimension_semantics=sem, vmem_limit_bytes=VMEM_LIMIT)


def _dot(a, b, ca, cb):
    return lax.dot_general(a, b, (((ca,), (cb,)), ((), ())), preferred_element_type=F32)


def _dot_nn(a, b):
    return _dot(a, b, 1, 0)


def _dot_nt(a, b):
    return _dot(a, b, 1, 1)


def _dot_tn(a, b):
    return _dot(a, b, 0, 0)


def _tile(n, t):
    if n <= t:
        return n
    t -= t % LANE
    while n % t:
        t -= LANE
    return t


def _all_gather(shards, name):
    n = len(shards)

    def body(*refs):
        ins, outs = refs[:n], refs[n:2 * n]
        send_sems, recv_sems, local_sems = refs[2 * n:]
        x, y, c = lax.axis_index("x"), lax.axis_index("y"), lax.axis_index("c")
        me, sibling = (x, y, c), (x, y, 1 - c)
        chips = [(1 - x, y), (x, 1 - y), (1 - x, 1 - y)]

        def copy(a, k, block, to, src=None):
            px, py, pc = block
            rows = outs[a].at[4 * px + 2 * py + pc]
            return pltpu.make_async_remote_copy(
                src_ref=rows if src is None else src, dst_ref=rows,
                send_sem=send_sems.at[a, k], recv_sem=recv_sems.at[a, k],
                device_id=to, device_id_type=MESH)

        started = []
        for a in range(n):
            mine = pltpu.make_async_copy(ins[a], outs[a].at[4 * x + 2 * y + c], local_sems.at[a])
            mine.start()
            started.append(mine)
        sends = []
        for a in range(n):
            first = [copy(a, 0, me, sibling, src=ins[a])]
            first += [copy(a, 1 + j, me, (*chip, c), src=ins[a]) for j, chip in enumerate(chips)]
            for cp in first:
                cp.start()
            sends += first
        for j, chip in enumerate(chips):
            for a in range(n):
                copy(a, 1 + j, (*chip, c), me).wait_recv()
                passed = copy(a, 4 + j, (*chip, c), sibling)
                passed.start()
                sends.append(passed)
        for a in range(n):
            copy(a, 0, sibling, me).wait_recv()
            for j, chip in enumerate(chips):
                copy(a, 4 + j, (*chip, 1 - c), me).wait_recv()
        for cp in sends:
            cp.wait_send()
        for mine in started:
            mine.wait()

    return pl.pallas_call(
        body, name=name,
        out_shape=[jax.ShapeDtypeStruct((NDEV,) + s.shape, s.dtype) for s in shards],
        in_specs=[ANY] * n, out_specs=[ANY] * n,
        scratch_shapes=[pltpu.SemaphoreType.DMA((n, 7)), pltpu.SemaphoreType.DMA((n, 7)),
                        pltpu.SemaphoreType.DMA((n,))],
    )(*shards)


def _all_to_all(arrs, name):
    n = len(arrs)

    def body(*refs):
        ins, outs = refs[:n], refs[n:2 * n]
        send_sems, recv_sems, local_sems = refs[2 * n:]
        x, y, c = lax.axis_index("x"), lax.axis_index("y"), lax.axis_index("c")
        my_id = 4 * x + 2 * y + c
        peers = []
        for k in range(1, NDEV):
            px = 1 - x if k & 4 else x
            py = 1 - y if k & 2 else y
            pc = 1 - c if k & 1 else c
            peers.append((px, py, pc))
        local = []
        for a in range(n):
            cp = pltpu.make_async_copy(ins[a].at[my_id], outs[a].at[my_id], local_sems.at[a])
            cp.start()
            local.append(cp)

        def send(a, k):
            px, py, pc = peers[k]
            return pltpu.make_async_remote_copy(
                src_ref=ins[a].at[4 * px + 2 * py + pc], dst_ref=outs[a].at[my_id],
                send_sem=send_sems.at[a, k], recv_sem=recv_sems.at[a, k],
                device_id=peers[k], device_id_type=MESH)

        def recv(a, k):
            px, py, pc = peers[k]
            return pltpu.make_async_remote_copy(
                src_ref=ins[a].at[my_id], dst_ref=outs[a].at[4 * px + 2 * py + pc],
                send_sem=send_sems.at[a, k], recv_sem=recv_sems.at[a, k],
                device_id=peers[k], device_id_type=MESH)

        sends = [send(a, k) for a in range(n) for k in range(NDEV - 1)]
        for cp in sends:
            cp.start()
        for a in range(n):
            for k in range(NDEV - 1):
                recv(a, k).wait_recv()
        for cp in sends:
            cp.wait_send()
        for cp in local:
            cp.wait()

    return pl.pallas_call(
        body, name=name,
        out_shape=[jax.ShapeDtypeStruct(s.shape, s.dtype) for s in arrs],
        in_specs=[ANY] * n, out_specs=[ANY] * n,
        scratch_shapes=[pltpu.SemaphoreType.DMA((n, 7)), pltpu.SemaphoreType.DMA((n, 7)),
                        pltpu.SemaphoreType.DMA((n,))],
    )(*arrs)


def _mm_call(pairs, grid, a_spec, b_spec, o_spec, out_shape, acc_shape, nk, ca, cb, name,
             res=None, res_spec=None):
    npairs = len(pairs)

    def body(*refs):
        ab = refs[:2 * npairs]
        r_ref = refs[2 * npairs] if res is not None else None
        o_ref, acc = refs[-2], refs[-1]
        k = pl.program_id(2)

        @pl.when(k == 0)
        def _():
            acc[...] = jnp.zeros_like(acc)

        tot = None
        for p in range(npairs):
            d = _dot(ab[2 * p][...].astype(BF16), ab[2 * p + 1][...].astype(BF16), ca, cb)
            tot = d if tot is None else tot + d
        acc[...] += tot

        @pl.when(k == nk - 1)
        def _():
            r = acc[...]
            if r_ref is not None:
                r = r + r_ref[...]
            o_ref[...] = r.astype(o_ref.dtype)

    ops, specs = [], []
    for a, b in pairs:
        ops += [a, b]
        specs += [a_spec, b_spec]
    if res is not None:
        ops.append(res)
        specs.append(res_spec)
    return pl.pallas_call(
        body, name=name, grid=grid, in_specs=specs, out_specs=o_spec, out_shape=out_shape,
        scratch_shapes=[pltpu.VMEM(acc_shape, F32)],
        compiler_params=_cp("parallel", "parallel", "arbitrary"),
    )(*ops)


def _mm(a, b, name, ta=False, tb=False, out_dtype=F32, res=None, tm=1024, tn=1024, tk=512):
    M, K = (a.shape[1], a.shape[0]) if ta else a.shape
    N = b.shape[0] if tb else b.shape[1]
    tm, tn, tk = _tile(M, tm), _tile(N, tn), _tile(K, tk)
    a_spec = pl.BlockSpec((tk, tm), lambda i, j, k: (k, i)) if ta else pl.BlockSpec((tm, tk), lambda i, j, k: (i, k))
    b_spec = pl.BlockSpec((tn, tk), lambda i, j, k: (j, k)) if tb else pl.BlockSpec((tk, tn), lambda i, j, k: (k, j))
    o_spec = pl.BlockSpec((tm, tn), lambda i, j, k: (i, j))
    return _mm_call([(a, b)], (M // tm, N // tn, K // tk), a_spec, b_spec, o_spec,
                    jax.ShapeDtypeStruct((M, N), out_dtype), (tm, tn), K // tk,
                    0 if ta else 1, 1 if tb else 0, name, res=res, res_spec=o_spec)


def _mm_down(act, wd, res, name, tm=1024, tn=1024):
    _, S, FB = act.shape
    D = wd.shape[1]
    tm, tn = _tile(S, tm), _tile(D, tn)
    o_spec = pl.BlockSpec((tm, tn), lambda i, j, k: (i, j))
    return _mm_call([(act, wd)], (S // tm, D // tn, NDEV),
                    pl.BlockSpec((None, tm, FB), lambda i, j, k: (k, i, 0)),
                    pl.BlockSpec((FB, tn), lambda i, j, k: (k, j)), o_spec,
                    jax.ShapeDtypeStruct((S, D), F32), (tm, tn), NDEV, 1, 0, name, res=res, res_spec=o_spec)


def _mm_dwdown(act, dy, name, tn=1024, tk=512):
    _, S, FB = act.shape
    D = dy.shape[1]
    tn, tk = _tile(D, tn), _tile(S, tk)
    return _mm_call([(act, dy)], (NDEV, D // tn, S // tk),
                    pl.BlockSpec((None, tk, FB), lambda i, j, k: (i, k, 0)),
                    pl.BlockSpec((tk, tn), lambda i, j, k: (k, j)),
                    pl.BlockSpec((FB, tn), lambda i, j, k: (i, j)),
                    jax.ShapeDtypeStruct((NDEV * FB, D), BF16), (FB, tn), S // tk, 0, 0, name)


def _mm_dh2(dg, wg, du, wu, name, tm=1024, tn=1024):
    _, S, FB = dg.shape
    D = wg.shape[1]
    tm, tn = _tile(S, tm), _tile(D, tn)
    return _mm_call([(dg, wg), (du, wu)], (S // tm, D // tn, NDEV),
                    pl.BlockSpec((None, tm, FB), lambda i, j, k: (k, i, 0)),
                    pl.BlockSpec((None, tn, FB), lambda i, j, k: (k, j, 0)),
                    pl.BlockSpec((tm, tn), lambda i, j, k: (i, j)),
                    jax.ShapeDtypeStruct((S, D), F32), (tm, tn), NDEV, 1, 1, name)


def _mm_dwgate(h2, dg, name, tm=1024, tk=512):
    _, S, FB = dg.shape
    D = h2.shape[1]
    tm, tk = _tile(D, tm), _tile(S, tk)
    return _mm_call([(h2, dg)], (NDEV, D // tm, S // tk),
                    pl.BlockSpec((tk, tm), lambda p, i, k: (k, i)),
                    pl.BlockSpec((None, tk, FB), lambda p, i, k: (p, k, 0)),
                    pl.BlockSpec((None, tm, FB), lambda p, i, k: (p, i, 0)),
                    jax.ShapeDtypeStruct((NDEV, D, FB), BF16), (tm, FB), S // tk, 0, 0, name)


def _ffn_up(h2, wg, wu, name, tm=512):
    S, D = h2.shape
    FB = wg.shape[2]
    tm = _tile(S, tm)

    def body(h_ref, wg_ref, wu_ref, g_ref, u_ref, act_ref):
        h = h_ref[...]
        g = _dot_nn(h, wg_ref[...])
        u = _dot_nn(h, wu_ref[...])
        g_ref[...] = g
        u_ref[...] = u
        act_ref[...] = (g / (1.0 + jnp.exp(-g)) * u).astype(BF16)

    w_spec = pl.BlockSpec((None, D, FB), lambda p, i: (p, 0, 0))
    o_spec = pl.BlockSpec((None, tm, FB), lambda p, i: (p, i, 0))
    shp = (NDEV, S, FB)
    return pl.pallas_call(
        body, name=name, grid=(NDEV, S // tm),
        in_specs=[pl.BlockSpec((tm, D), lambda p, i: (i, 0)), w_spec, w_spec],
        out_specs=[o_spec, o_spec, o_spec],
        out_shape=[jax.ShapeDtypeStruct(shp, F32), jax.ShapeDtypeStruct(shp, F32), jax.ShapeDtypeStruct(shp, BF16)],
        compiler_params=_cp("parallel", "parallel"),
    )(h2, wg, wu)


def _ffn_dact(dy, wd, g, u, name, tm=512):
    S, D = dy.shape
    FB = g.shape[2]
    tm = _tile(S, tm)

    def body(dy_ref, wd_ref, g_ref, u_ref, dg_ref, du_ref):
        dact = _dot_nt(dy_ref[...].astype(BF16), wd_ref[...])
        gv = g_ref[...]
        sg = 1.0 / (1.0 + jnp.exp(-gv))
        dg_ref[...] = (dact * u_ref[...] * (sg * (1.0 + gv * (1.0 - sg)))).astype(BF16)
        du_ref[...] = (dact * (gv * sg)).astype(BF16)

    t_spec = pl.BlockSpec((None, tm, FB), lambda p, i: (p, i, 0))
    shp = jax.ShapeDtypeStruct((NDEV, S, FB), BF16)
    return pl.pallas_call(
        body, name=name, grid=(NDEV, S // tm),
        in_specs=[pl.BlockSpec((tm, D), lambda p, i: (i, 0)), pl.BlockSpec((FB, D), lambda p, i: (p, 0)),
                  t_spec, t_spec],
        out_specs=[t_spec, t_spec], out_shape=[shp, shp],
        compiler_params=_cp("parallel", "parallel"),
    )(dy, wd, g, u)


def _rms_fwd(x, gain, width, cb, out_dtype, name, ts=512):
    S = x.shape[0]
    ts = _tile(S, ts)

    def body(x_ref, g_ref, o_ref):
        xv = x_ref[...]
        r = lax.rsqrt(jnp.mean(xv * xv, axis=1, keepdims=True) + EPS)
        o_ref[...] = (xv * r * g_ref[...]).astype(o_ref.dtype)

    return pl.pallas_call(
        body, name=name, grid=(S // ts,),
        in_specs=[pl.BlockSpec((ts, width), lambda i: (i, cb)), pl.BlockSpec((1, width), lambda i: (0, 0))],
        out_specs=pl.BlockSpec((ts, width), lambda i: (i, 0)),
        out_shape=jax.ShapeDtypeStruct((S, width), out_dtype),
        compiler_params=_cp("parallel"),
    )(x, gain)


def _rms_bwd(x, gain, dy, width, cb, name, res=None, ts=256):
    S = x.shape[0]
    ts = _tile(S, ts)
    has_res = res is not None

    def body(*refs):
        x_ref, g_ref, dy_ref = refs[:3]
        r_ref = refs[3] if has_res else None
        dx_ref, dg_ref = refs[-2], refs[-1]

        @pl.when(pl.program_id(0) == 0)
        def _():
            dg_ref[...] = jnp.zeros_like(dg_ref)

        xv = x_ref[...]
        r = lax.rsqrt(jnp.mean(xv * xv, axis=1, keepdims=True) + EPS)
        xh = xv * r
        dyv = dy_ref[...]
        dyg = dyv * g_ref[...]
        dx = r * (dyg - xh * jnp.mean(dyg * xh, axis=1, keepdims=True))
        if has_res:
            dx = dx + r_ref[...]
        dx_ref[...] = dx
        dg_ref[...] += jnp.sum(dyv * xh, axis=0, keepdims=True)

    row = pl.BlockSpec((ts, width), lambda i: (i, 0))
    vec = pl.BlockSpec((1, width), lambda i: (0, 0))
    ops = [x, gain, dy] + ([res] if has_res else [])
    specs = [pl.BlockSpec((ts, width), lambda i: (i, cb)), vec, row] + ([row] if has_res else [])
    return pl.pallas_call(
        body, name=name, grid=(S // ts,), in_specs=specs, out_specs=[row, vec],
        out_shape=[jax.ShapeDtypeStruct((S, width), F32), jax.ShapeDtypeStruct((1, width), F32)],
        compiler_params=_cp("arbitrary"),
    )(*ops)


def _gn_fwd(outs, gain, name, ts=512):
    S, GW = outs[0].shape
    ts = _tile(S, ts)

    def body(a_ref, b_ref, c_ref, d_ref, g_ref, o_ref):
        for g, r_ref in enumerate((a_ref, b_ref, c_ref, d_ref)):
            xv = r_ref[...]
            r = lax.rsqrt(jnp.mean(xv * xv, axis=1, keepdims=True) + EPS)
            o_ref[:, g * GW:(g + 1) * GW] = (xv * r * g_ref[:, g * GW:(g + 1) * GW]).astype(BF16)

    row = pl.BlockSpec((ts, GW), lambda i: (i, 0))
    return pl.pallas_call(
        body, name=name, grid=(S // ts,),
        in_specs=[row] * 4 + [pl.BlockSpec((1, 4 * GW), lambda i: (0, 0))],
        out_specs=pl.BlockSpec((ts, 4 * GW), lambda i: (i, 0)),
        out_shape=jax.ShapeDtypeStruct((S, 4 * GW), BF16),
        compiler_params=_cp("parallel"),
    )(*outs, gain)


def _gn_bwd(outs, gain, dmix, name, ts=256):
    S, GW = outs[0].shape
    ts = _tile(S, ts)

    def body(a_ref, b_ref, c_ref, d_ref, g_ref, dm_ref, da_ref, db_ref, dc_ref, dd_ref, dg_ref):
        @pl.when(pl.program_id(0) == 0)
        def _():
            dg_ref[...] = jnp.zeros_like(dg_ref)

        for g, (r_ref, o_ref) in enumerate(zip((a_ref, b_ref, c_ref, d_ref), (da_ref, db_ref, dc_ref, dd_ref))):
            sl = slice(g * GW, (g + 1) * GW)
            xv = r_ref[...]
            r = lax.rsqrt(jnp.mean(xv * xv, axis=1, keepdims=True) + EPS)
            xh = xv * r
            dyv = dm_ref[:, sl]
            dyg = dyv * g_ref[:, sl]
            o_ref[...] = (r * (dyg - xh * jnp.mean(dyg * xh, axis=1, keepdims=True))).astype(BF16)
            dg_ref[:, sl] += jnp.sum(dyv * xh, axis=0, keepdims=True)

    row = pl.BlockSpec((ts, GW), lambda i: (i, 0))
    vec = pl.BlockSpec((1, 4 * GW), lambda i: (0, 0))
    return pl.pallas_call(
        body, name=name, grid=(S // ts,),
        in_specs=[row] * 4 + [vec, pl.BlockSpec((ts, 4 * GW), lambda i: (i, 0))],
        out_specs=[row] * 4 + [vec],
        out_shape=[jax.ShapeDtypeStruct((S, GW), BF16)] * 4 + [jax.ShapeDtypeStruct((1, 4 * GW), F32)],
        compiler_params=_cp("arbitrary"),
    )(*outs, gain, dmix)


def _rope(x, cbs, cos, sin, half, out_dtype, name, ts=512):
    S = x.shape[0]
    cb0, nb, stride = cbs
    ts = _tile(S, ts)

    def body(x_ref, c_ref, s_ref, o_ref):
        xv = x_ref[...].astype(F32)
        if half:
            lane = lax.broadcasted_iota(jnp.int32, xv.shape, 1)
            partner = jnp.where(lane % 64 < 32, pltpu.roll(xv, LANE - 32, 1), pltpu.roll(xv, 32, 1))
        else:
            partner = pltpu.roll(xv, 64, 1)
        o_ref[...] = (xv * c_ref[...] + partner * s_ref[...]).astype(o_ref.dtype)

    tab = pl.BlockSpec((ts, LANE), lambda i, j: (i, 0))
    return pl.pallas_call(
        body, name=name, grid=(S // ts, nb),
        in_specs=[pl.BlockSpec((ts, LANE), lambda i, j: (i, cb0 + stride * j)), tab, tab],
        out_specs=pl.BlockSpec((ts, LANE), lambda i, j: (i, j)),
        out_shape=jax.ShapeDtypeStruct((S, nb * LANE), out_dtype),
        compiler_params=_cp("parallel", "parallel"),
    )(x, cos, sin)


def _final_loss(x, gain, target, name, ts=256):
    S, D = x.shape
    ts = _tile(S, ts)

    def body(x_ref, g_ref, t_ref, dy_ref, l_ref):
        @pl.when(pl.program_id(0) == 0)
        def _():
            l_ref[...] = jnp.zeros_like(l_ref)

        xv = x_ref[...]
        r = lax.rsqrt(jnp.mean(xv * xv, axis=1, keepdims=True) + EPS)
        err = xv * r * g_ref[...] - t_ref[...]
        dy_ref[...] = err * (1.0 / D)
        part = jnp.sum(jnp.mean(err * err, axis=1, keepdims=True), axis=0, keepdims=True)
        l_ref[...] += jnp.broadcast_to(0.5 * part, (1, LANE))

    row = pl.BlockSpec((ts, D), lambda i: (i, 0))
    return pl.pallas_call(
        body, name=name, grid=(S // ts,),
        in_specs=[row, pl.BlockSpec((1, D), lambda i: (0, 0)), row],
        out_specs=[row, pl.BlockSpec((1, LANE), lambda i: (0, 0))],
        out_shape=[jax.ShapeDtypeStruct((S, D), F32), jax.ShapeDtypeStruct((1, LANE), F32)],
        compiler_params=_cp("arbitrary"),
    )(x, gain, target)


def _colspec(rows, f):
    return pl.BlockSpec((rows, LANE), f)


def _attn_fwd(S, H, q1, q1cb, k1, k1cb, v, vcb, scale, name, q2=None, q2cb=None, k2=None, k2cb=None,
              tab=None, win=None, ccol=None, crow=None):
    tq, tk = _tile(S, TQ), TK
    r = tq // tk
    has2, hastab, hasc = q2 is not None, tab is not None, ccol is not None

    def body(*refs):
        it = iter(refs)
        q1r, k1r, vr = next(it), next(it), next(it)
        q2r, k2r = (next(it), next(it)) if has2 else (None, None)
        tabr = next(it) if hastab else None
        ccolr, crowr = (next(it), next(it)) if hasc else (None, None)
        o_ref, lse_ref = next(it), next(it)
        i = pl.program_id(1)
        q = q1r[...].astype(BF16)
        qb2 = q2r[...].astype(BF16) if has2 else None
        cq = ccolr[:, 0:1] if hasc else None
        qpos = i * tq + lax.broadcasted_iota(jnp.int32, (tq, tk), 0)
        kio = lax.broadcasted_iota(jnp.int32, (tq, tk), 1)
        j_hi = i * r + r
        j_lo = jnp.maximum(i * r - win // tk, 0) if win else 0

        def step(j, carry):
            m, l, acc = carry
            off = pl.multiple_of(j * tk, tk)
            s = _dot_nt(q, k1r[pl.ds(off, tk), :].astype(BF16))
            if has2:
                s = s + _dot_nt(qb2, k2r[pl.ds(off, tk), :].astype(BF16))
            s = s * scale
            if hastab:
                s = s + tabr[i * r - j + (r - 1)]
            else:
                if hasc:
                    s = s + (cq - crowr[j])
                s = jnp.where(kio + j * tk <= qpos, s, NEG)
            mn = jnp.maximum(m, jnp.max(s, axis=1, keepdims=True))
            p = jnp.exp(s - mn)
            al = jnp.exp(m - mn)
            l = al * l + jnp.sum(p, axis=1, keepdims=True)
            acc = al * acc + _dot_nn(p.astype(BF16), vr[pl.ds(off, tk), :].astype(BF16))
            return mn, l, acc

        init = (jnp.full((tq, 1), NEG, F32), jnp.zeros((tq, 1), F32), jnp.zeros((tq, LANE), F32))
        m, l, acc = lax.fori_loop(j_lo, j_hi, step, init)
        o_ref[...] = acc / l
        lse_ref[...] = jnp.broadcast_to(m + jnp.log(l), (tq, LANE))

    ops = [q1, k1, v]
    specs = [_colspec(tq, lambda h, i: (i, q1cb(h))), _colspec(S, lambda h, i: (0, k1cb(h))),
             _colspec(S, lambda h, i: (0, vcb(h)))]
    if has2:
        ops += [q2, k2]
        specs += [_colspec(tq, lambda h, i: (i, q2cb(h))), _colspec(S, lambda h, i: (0, k2cb(h)))]
    if hastab:
        ops.append(tab)
        specs.append(pl.BlockSpec(tab.shape, lambda h, i: (0, 0, 0)))
    if hasc:
        ops += [ccol, crow]
        specs += [_colspec(tq, lambda h, i: (i, h)),
                  pl.BlockSpec((None, S // tk, 1, tk), lambda h, i: (h, 0, 0, 0))]
    o_spec = _colspec(tq, lambda h, i: (i, h))
    shp = jax.ShapeDtypeStruct((S, H * LANE), F32)
    return pl.pallas_call(
        body, name=name, grid=(H, S // tq), in_specs=specs, out_specs=[o_spec, o_spec], out_shape=[shp, shp],
        compiler_params=_cp("parallel", "arbitrary"),
    )(*ops)


def _attn_bwd(S, H, q1, q1cb, k1, k1cb, v, vcb, o, do, lse, scale, name, q2=None, q2cb=None, k2=None, k2cb=None,
              tab=None, win=None, ccol=None, crow=None):
    tq, tk = _tile(S, TQ), TK
    r = tq // tk
    has2, hastab, hasc = q2 is not None, tab is not None, ccol is not None

    def body(*refs):
        it = iter(refs)
        q1r, k1r, vr, o_r, do_r, lse_r = (next(it) for _ in range(6))
        q2r, k2r = (next(it), next(it)) if has2 else (None, None)
        tabr = next(it) if hastab else None
        ccolr, crowr = (next(it), next(it)) if hasc else (None, None)
        dq1_r, dk1_r, dv_r = next(it), next(it), next(it)
        dq2_r, dk2_r = (next(it), next(it)) if has2 else (None, None)
        dcr_r = next(it) if hasc else None
        i = pl.program_id(1)

        @pl.when(i == 0)
        def _():
            dk1_r[...] = jnp.zeros_like(dk1_r)
            dv_r[...] = jnp.zeros_like(dv_r)
            if has2:
                dk2_r[...] = jnp.zeros_like(dk2_r)
            if hasc:
                dcr_r[...] = jnp.zeros_like(dcr_r)

        q = q1r[...].astype(BF16)
        qb2 = q2r[...].astype(BF16) if has2 else None
        dob = do_r[...].astype(BF16)
        delta = jnp.sum(dob.astype(F32) * o_r[...], axis=1, keepdims=True)
        lse_c = lse_r[:, 0:1]
        cq = ccolr[:, 0:1] if hasc else None
        qpos = i * tq + lax.broadcasted_iota(jnp.int32, (tq, tk), 0)
        kio = lax.broadcasted_iota(jnp.int32, (tq, tk), 1)
        j_hi = i * r + r
        j_lo = jnp.maximum(i * r - win // tk, 0) if win else 0

        def probs(j):
            off = pl.multiple_of(j * tk, tk)
            kb = k1r[pl.ds(off, tk), :].astype(BF16)
            s = _dot_nt(q, kb)
            kb2 = None
            if has2:
                kb2 = k2r[pl.ds(off, tk), :].astype(BF16)
                s = s + _dot_nt(qb2, kb2)
            s = s * scale
            if hastab:
                s = s + tabr[i * r - j + (r - 1)]
            else:
                if hasc:
                    s = s + (cq - crowr[j])
                s = jnp.where(kio + j * tk <= qpos, s, NEG)
            p = jnp.exp(s - lse_c)
            dp = _dot_nt(dob, vr[pl.ds(off, tk), :].astype(BF16))
            return off, kb, kb2, p, dp

        if hasc:
            def dstep(j, acc):
                _, _, _, p, dp = probs(j)
                return acc + jnp.sum(p * dp, axis=1, keepdims=True)

            delta = lax.fori_loop(j_lo, j_hi, dstep, jnp.zeros((tq, 1), F32))

        def step(j, carry):
            dq, dq2 = carry
            off, kb, kb2, p, dp = probs(j)
            ds = p * (dp - delta)
            dsb = ds.astype(BF16)
            dq = dq + _dot_nn(dsb, kb)
            dk1_r[pl.ds(off, tk), :] += _dot_tn(dsb, q) * scale
            dv_r[pl.ds(off, tk), :] += _dot_tn(p.astype(BF16), dob)
            if has2:
                dq2 = dq2 + _dot_nn(dsb, kb2)
                dk2_r[pl.ds(off, tk), :] += _dot_tn(dsb, qb2) * scale
            if hasc:
                dcr_r[j] += -jnp.sum(ds, axis=0, keepdims=True)
            return dq, dq2

        z = jnp.zeros((tq, LANE), F32)
        dq, dq2 = lax.fori_loop(j_lo, j_hi, step, (z, z))
        dq1_r[...] = dq * scale
        if has2:
            dq2_r[...] = dq2 * scale

    qspec = _colspec(tq, lambda h, i: (i, h))
    kspec = _colspec(S, lambda h, i: (0, h))
    ops = [q1, k1, v, o, do, lse]
    specs = [_colspec(tq, lambda h, i: (i, q1cb(h))), _colspec(S, lambda h, i: (0, k1cb(h))),
             _colspec(S, lambda h, i: (0, vcb(h))), qspec, qspec, qspec]
    if has2:
        ops += [q2, k2]
        specs += [_colspec(tq, lambda h, i: (i, q2cb(h))), _colspec(S, lambda h, i: (0, k2cb(h)))]
    if hastab:
        ops.append(tab)
        specs.append(pl.BlockSpec(tab.shape, lambda h, i: (0, 0, 0)))
    if hasc:
        ops += [ccol, crow]
        specs += [qspec, pl.BlockSpec((None, S // tk, 1, tk), lambda h, i: (h, 0, 0, 0))]
    out_specs = [qspec, kspec, kspec] + ([qspec, kspec] if has2 else [])
    shp = jax.ShapeDtypeStruct((S, H * LANE), F32)
    out_shape = [shp] * len(out_specs)
    if hasc:
        out_specs.append(pl.BlockSpec((None, S // tk, 1, tk), lambda h, i: (h, 0, 0, 0)))
        out_shape.append(jax.ShapeDtypeStruct((H, S // tk, 1, tk), F32))
    return pl.pallas_call(
        body, name=name, grid=(H, S // tq), in_specs=specs, out_specs=out_specs, out_shape=out_shape,
        compiler_params=_cp("parallel", "arbitrary"),
    )(*ops)


def _scan_lanes(x, lane, reverse):
    n = x.shape[1]
    sh = 1
    while sh < n:
        if reverse:
            x = x + jnp.where(lane + sh < n, pltpu.roll(x, n - sh, 1), 0.0)
        else:
            x = x + jnp.where(lane >= sh, pltpu.roll(x, sh, 1), 0.0)
        sh *= 2
    return x


def _stick_terms(z):
    e = jnp.exp(-jnp.abs(z))
    sp = jnp.log(1.0 + e)
    inv = 1.0 / (1.0 + e)
    pos = z >= 0
    return -jnp.maximum(z, 0.0) - sp, jnp.minimum(z, 0.0) - sp, jnp.where(pos, inv, e * inv), jnp.where(pos, e * inv, inv)


def _stick_fwd(S, H, x, qcb, kcb, vcb, scale, name):
    tq, tk = _tile(S, TQ), TK
    r = tq // tk

    def body(q_r, k_r, v_r, o_ref, t_ref):
        i = pl.program_id(1)
        q = q_r[...].astype(BF16)
        qpos = i * tq + lax.broadcasted_iota(jnp.int32, (tq, tk), 0)
        lane = lax.broadcasted_iota(jnp.int32, (tq, tk), 1)
        n = i * r + r

        def step(jj, carry):
            c, acc = carry
            j = n - 1 - jj
            off = pl.multiple_of(j * tk, tk)
            z = _dot_nt(q, k_r[pl.ds(off, tk), :].astype(BF16)) * scale
            past = lane + j * tk < qpos
            lk, lsz, _, _ = _stick_terms(z)
            lk = jnp.where(past, lk, 0.0)
            suf = _scan_lanes(lk, lane, True)
            a = jnp.where(past, jnp.exp(lsz + (suf - lk + c)), 0.0)
            acc = acc + _dot_nn(a.astype(BF16), v_r[pl.ds(off, tk), :].astype(BF16))
            return c + suf[:, 0:1], acc

        c, acc = lax.fori_loop(0, n, step, (jnp.zeros((tq, 1), F32), jnp.zeros((tq, LANE), F32)))
        o_ref[...] = acc
        t_ref[...] = jnp.broadcast_to(c, (tq, LANE))

    o_spec = _colspec(tq, lambda h, i: (i, h))
    shp = jax.ShapeDtypeStruct((S, H * LANE), F32)
    return pl.pallas_call(
        body, name=name, grid=(H, S // tq),
        in_specs=[_colspec(tq, lambda h, i: (i, qcb(h))), _colspec(S, lambda h, i: (0, kcb(h))),
                  _colspec(S, lambda h, i: (0, vcb(h)))],
        out_specs=[o_spec, o_spec], out_shape=[shp, shp],
        compiler_params=_cp("parallel", "arbitrary"),
    )(x, x, x)


def _stick_bwd(S, H, x, qcb, kcb, vcb, do, tot, scale, name):
    tq, tk = _tile(S, TQ), TK
    r = tq // tk

    def body(q_r, k_r, v_r, do_r, t_r, dq_r, dk_r, dv_r):
        i = pl.program_id(1)

        @pl.when(i == 0)
        def _():
            dk_r[...] = jnp.zeros_like(dk_r)
            dv_r[...] = jnp.zeros_like(dv_r)

        q = q_r[...].astype(BF16)
        dob = do_r[...].astype(BF16)
        total = t_r[:, 0:1]
        qpos = i * tq + lax.broadcasted_iota(jnp.int32, (tq, tk), 0)
        lane = lax.broadcasted_iota(jnp.int32, (tq, tk), 1)

        def step(j, carry):
            cl, cg, dq = carry
            off = pl.multiple_of(j * tk, tk)
            kb = k_r[pl.ds(off, tk), :].astype(BF16)
            z = _dot_nt(q, kb) * scale
            past = lane + j * tk < qpos
            lk, lsz, sig, nsig = _stick_terms(z)
            lk = jnp.where(past, lk, 0.0)
            pre = _scan_lanes(lk, lane, False)
            a = jnp.where(past, jnp.exp(lsz + (total - cl - pre)), 0.0)
            g = _dot_nt(dob, v_r[pl.ds(off, tk), :].astype(BF16)) * a
            gpre = _scan_lanes(g, lane, False)
            dz = jnp.where(past, g * nsig - sig * (cg + gpre - g), 0.0)
            dzb = dz.astype(BF16)
            dk_r[pl.ds(off, tk), :] += _dot_tn(dzb, q) * scale
            dv_r[pl.ds(off, tk), :] += _dot_tn(a.astype(BF16), dob)
            return cl + pre[:, tk - 1:tk], cg + gpre[:, tk - 1:tk], dq + _dot_nn(dzb, kb)

        z1 = jnp.zeros((tq, 1), F32)
        _, _, dq = lax.fori_loop(0, i * r + r, step, (z1, z1, jnp.zeros((tq, LANE), F32)))
        dq_r[...] = dq * scale

    qspec = _colspec(tq, lambda h, i: (i, h))
    kspec = _colspec(S, lambda h, i: (0, h))
    shp = jax.ShapeDtypeStruct((S, H * LANE), F32)
    return pl.pallas_call(
        body, name=name, grid=(H, S // tq),
        in_specs=[_colspec(tq, lambda h, i: (i, qcb(h))), _colspec(S, lambda h, i: (0, kcb(h))),
                  _colspec(S, lambda h, i: (0, vcb(h))), qspec, qspec],
        out_specs=[qspec, kspec, kspec], out_shape=[shp] * 3,
        compiler_params=_cp("parallel", "arbitrary"),
    )(x, x, x, do, tot)


def _scan8(x, rows, reverse):
    for sh in (1, 2, 4):
        if reverse:
            x = x + jnp.where(rows + sh < 8, pltpu.roll(x, 8 - sh, 0), 0.0)
        else:
            x = x + jnp.where(rows >= sh, pltpu.roll(x, sh, 0), 0.0)
    return x


def _fox_prep(S, H, proj, fcb, bias, name):
    tk = TK

    def body(f_ref, b_ref, ccol_ref, crow_ref, scr):
        rows = lax.broadcasted_iota(jnp.int32, (8, LANE), 0)

        def step(t, carry):
            off = pl.multiple_of(t * 8, 8)
            xb = f_ref[pl.ds(off, 8), :] + b_ref[...]
            lf = jnp.minimum(xb, 0.0) - jnp.log(1.0 + jnp.exp(-jnp.abs(xb)))
            lf = _scan8(lf, rows, False) + carry
            scr[pl.ds(off, 8), :] = lf
            return lf[7:8, :]

        lax.fori_loop(0, S // 8, step, jnp.zeros((1, LANE), F32))
        for h in range(H):
            ccol_ref[:, h * LANE:(h + 1) * LANE] = jnp.broadcast_to(scr[:, h:h + 1], (S, LANE))

            def tr(t, _):
                off = pl.multiple_of(t * tk, tk)
                blk = ccol_ref[pl.ds(off, tk), h * LANE:(h + 1) * LANE]
                crow_ref[h, t] = blk.T[0:1, :]
                return 0

            lax.fori_loop(0, S // tk, tr, 0)

    return pl.pallas_call(
        body, name=name, grid=(1,),
        in_specs=[_colspec(S, lambda i: (0, fcb)), pl.BlockSpec((1, LANE), lambda i: (0, 0))],
        out_specs=[pl.BlockSpec((S, H * LANE), lambda i: (0, 0)),
                   pl.BlockSpec((H, S // tk, 1, tk), lambda i: (0, 0, 0, 0))],
        out_shape=[jax.ShapeDtypeStruct((S, H * LANE), F32), jax.ShapeDtypeStruct((H, S // tk, 1, tk), F32)],
        scratch_shapes=[pltpu.VMEM((S, LANE), F32)],
        compiler_params=_cp("arbitrary"),
    )(proj, bias)


def _fox_bwd(S, H, proj, fcb, bias, dcr, name):
    tk = TK

    def body(f_ref, b_ref, dcr_ref, df_ref, db_ref, scr):
        rows = lax.broadcasted_iota(jnp.int32, (8, LANE), 0)
        lane_t = lax.broadcasted_iota(jnp.int32, (tk, LANE), 1)
        nb = S // 8

        def tr(t, _):
            off = pl.multiple_of(t * tk, tk)
            d = jnp.zeros((tk, LANE), F32)
            for h in range(H):
                d = d + jnp.where(lane_t == h, jnp.broadcast_to(dcr_ref[h, t], (LANE, tk)).T, 0.0)
            scr[pl.ds(off, tk), :] = d
            return 0

        lax.fori_loop(0, S // tk, tr, 0)

        def step(tt, carry):
            suffix, db = carry
            off = pl.multiple_of((nb - 1 - tt) * 8, 8)
            d = _scan8(scr[pl.ds(off, 8), :], rows, True) + suffix
            xb = f_ref[pl.ds(off, 8), :] + b_ref[...]
            e = jnp.exp(-jnp.abs(xb))
            dx = d * jnp.where(xb >= 0, e, 1.0) / (1.0 + e)
            df_ref[pl.ds(off, 8), :] = dx
            return d[0:1, :], db + jnp.sum(dx, axis=0, keepdims=True)

        z = jnp.zeros((1, LANE), F32)
        _, db = lax.fori_loop(0, nb, step, (z, z))
        db_ref[...] = db

    return pl.pallas_call(
        body, name=name, grid=(1,),
        in_specs=[_colspec(S, lambda i: (0, fcb)), pl.BlockSpec((1, LANE), lambda i: (0, 0)),
                  pl.BlockSpec((H, S // tk, 1, tk), lambda i: (0, 0, 0, 0))],
        out_specs=[pl.BlockSpec((S, LANE), lambda i: (0, 0)), pl.BlockSpec((1, LANE), lambda i: (0, 0))],
        out_shape=[jax.ShapeDtypeStruct((S, LANE), F32), jax.ShapeDtypeStruct((1, LANE), F32)],
        scratch_shapes=[pltpu.VMEM((S, LANE), F32)],
        compiler_params=_cp("arbitrary"),
    )(proj, bias, dcr)


def _adamw(w, slots, m, v, name, block_bytes=1 << 20):
    R, C = w.shape
    tr = R
    while tr * C * 4 > block_bytes and tr % 16 == 0:
        tr //= 2
    c1 = 1.0 - ADAM_B1 ** ADAM_STEP
    c2 = 1.0 - ADAM_B2 ** ADAM_STEP

    def body(w_ref, s_ref, m_ref, v_ref, g_ref, d_ref, nm_ref, nv_ref):
        g = s_ref[0].astype(F32)
        for s in range(1, NDEV):
            g = g + s_ref[s].astype(F32)
        mn = ADAM_B1 * m_ref[...] + (1.0 - ADAM_B1) * g
        vn = ADAM_B2 * v_ref[...] + (1.0 - ADAM_B2) * (g * g)
        g_ref[...] = g
        nm_ref[...] = mn
        nv_ref[...] = vn
        d_ref[...] = -ADAM_LR * ((mn / c1) / (jnp.sqrt(vn / c2) + ADAM_EPS) + ADAM_WD * w_ref[...])

    row = pl.BlockSpec((tr, C), lambda i: (i, 0))
    return pl.pallas_call(
        body, name=name, grid=(R // tr,),
        in_specs=[row, pl.BlockSpec((NDEV, tr, C), lambda i: (0, i, 0)), row, row],
        out_specs=[row] * 4, out_shape=[jax.ShapeDtypeStruct((R, C), F32)] * 4,
        compiler_params=_cp("parallel"),
    )(w, slots, m, v)


class _Layout:
    def __init__(self, D):
        self.GW = GW = D // 4
        self.H = H = GW // HEAD
        self.QL, self.KVL = 0, Q_LORA
        base = Q_LORA + KV_LORA
        (self.QB, self.KB, self.VB, self.QC, self.KC, self.VC, self.QD, self.KD, self.VD) = (
            base + k * GW for k in range(9))
        self.KR = base + 9 * GW
        self.FC = self.KR + LANE
        self.PW = -(-(self.FC + LANE) // 512) * 512
        self.o_kr = base
        self.o_bc = base + QK_ROPE
        self.o_fc = self.o_bc + 6 * GW
        self.o_d = self.o_fc + H
        self.IN = self.o_d + 3 * GW

    def pad(self, w):
        z = lambda n: jnp.zeros(w.shape[:-1] + (n,), w.dtype)
        return jnp.concatenate([
            w[..., :self.o_kr], w[..., self.o_bc:self.o_fc], w[..., self.o_d:self.IN],
            w[..., self.o_kr:self.o_bc], z(LANE - QK_ROPE), w[..., self.o_fc:self.o_d], z(LANE - self.H),
            z(self.PW - self.FC - LANE)], axis=-1)

    def unpad(self, g):
        return jnp.concatenate([
            g[..., :self.KR - 9 * self.GW], g[..., self.KR:self.KR + QK_ROPE], g[..., self.QB:self.QD],
            g[..., self.FC:self.FC + self.H], g[..., self.QD:self.KR]], axis=-1)


def _rope_tables(S):
    pos = jnp.arange(S, dtype=F32)

    def cs(dim):
        inv = ROPE_THETA ** (-jnp.arange(0, dim, 2, dtype=F32) / dim)
        ang = pos[:, None] * inv[None, :]
        return jnp.cos(ang), jnp.sin(ang)

    c, s = cs(HEAD)
    full = (jnp.concatenate([c, c], 1), jnp.concatenate([-s, s], 1))
    c, s = cs(QK_ROPE)
    z = jnp.zeros((S, LANE - QK_ROPE), F32)
    half = (jnp.concatenate([c, c, z], 1), jnp.concatenate([-s, s, z], 1))
    return full, half


def _dilated_table(tq, tk):
    r = tq // tk
    win = max(w for w, _ in DILATED_PAIRS)
    nd = win // tk + r
    d = (np.arange(nd)[:, None, None] - (r - 1)) * tk + np.arange(tq)[None, :, None] - np.arange(tk)[None, None, :]
    mult = np.zeros(d.shape, np.float64)
    for w, dil in DILATED_PAIRS:
        mult += (d >= 0) & (d <= w) & (d % dil == 0)
    return jnp.asarray(np.where(mult > 0, np.log(np.maximum(mult, 1.0)), NEG), F32), win


def _pack(arrs):
    rows = []
    for a in arrs:
        f = a.reshape(-1).astype(F32)
        f = jnp.pad(f, (0, (-f.shape[0]) % LANE))
        rows.append(f.reshape(-1, LANE))
    p = jnp.concatenate(rows, 0)
    return jnp.pad(p, ((0, (-p.shape[0]) % 8), (0, 0)))


def _unpack(p, shapes):
    out, r = [], 0
    for shp in shapes:
        n = int(np.prod(shp))
        nr = -(-n // LANE)
        out.append(p[r:r + nr].reshape(-1)[:n].reshape(shp))
        r += nr
    return out


def kernel(x, attn_norm, w_in, mla_q_norm, w_uq, mla_kv_norm, w_ukv, fox_forget_bias, group_norm, w_out, ffn_norm, w_gate, w_up, w_down, final_norm, loss_target, m_attn_norm, m_w_in, m_mla_q_norm, m_w_uq, m_mla_kv_norm, m_w_ukv, m_fox_forget_bias, m_group_norm, m_w_out, m_ffn_norm, m_w_gate, m_w_up, m_w_down, m_final_norm, v_attn_norm, v_w_in, v_mla_q_norm, v_w_uq, v_mla_kv_norm, v_w_ukv, v_fox_forget_bias, v_group_norm, v_w_out, v_ffn_norm, v_w_gate, v_w_up, v_w_down, v_final_norm):
    _, S, D = x.shape
    L = attn_norm.shape[0]
    lay = _Layout(D)
    H, GW, PW = lay.H, lay.GW, lay.PW
    FB = w_gate.shape[2]
    QKA = HEAD + QK_ROPE
    x = x[0]
    target = loss_target[0]
    rope_full, rope_half = _rope_tables(S)
    neg = lambda t: (t[0], -t[1])
    tab, win = _dilated_table(_tile(S, TQ), TK)
    cb = lambda col: col // LANE

    win_sh = lay.pad(w_in).astype(BF16)
    sh = [w.astype(BF16) for w in (w_uq, w_ukv, w_out, w_gate, w_up, w_down)]

    def gather(l):
        g = _all_gather([win_sh[l]] + [s[l] for s in sh], name="gather_weights")
        wuq = jnp.transpose(g[1], (1, 0, 2)).reshape(Q_LORA, H, QKA)
        wuq = jnp.pad(wuq, ((0, 0), (0, 0), (0, 2 * LANE - QKA))).reshape(Q_LORA, H * 2 * LANE)
        return dict(win=g[0].reshape(D, PW), wuq=wuq,
                    wukv=jnp.transpose(g[2], (1, 0, 2)).reshape(KV_LORA, H * 2 * LANE),
                    wout=g[3].reshape(4 * GW, D), wg=g[4], wu=g[5], wd=g[6].reshape(NDEV * FB, D))

    def row(a):
        return a.reshape(1, -1)

    def forward(l, x0, W):
        A = dict(x0=x0)
        A["bias"] = jnp.pad(row(fox_forget_bias[l]), ((0, 0), (0, LANE - H)))
        h1 = A["h1"] = _rms_fwd(x0, row(attn_norm[l]), D, 0, BF16, "attn_norm")
        proj = A["proj"] = _mm(h1, W["win"], "in_proj")
        qln = A["qln"] = _rms_fwd(proj, row(mla_q_norm[l]), Q_LORA, cb(lay.QL) // 4, BF16, "q_norm")
        kvln = A["kvln"] = _rms_fwd(proj, row(mla_kv_norm[l]), KV_LORA, cb(lay.KVL) // 4, BF16, "kv_norm")
        qa = A["qa"] = _mm(qln, W["wuq"], "q_up")
        kv = A["kv"] = _mm(kvln, W["wukv"], "kv_up")
        q_pe = A["q_pe"] = _rope(qa, (1, H, 2), *rope_half, True, BF16, "rope_q_mla")
        k_pe = A["k_pe"] = _rope(proj, (cb(lay.KR), 1, 1), *rope_half, True, BF16, "rope_k_mla")
        A["o_a"], A["lse_a"] = _attn_fwd(
            S, H, qa, lambda h: 2 * h, kv, lambda h: 2 * h, kv, lambda h: 2 * h + 1, QKA ** -0.5, "mla_fwd",
            q2=q_pe, q2cb=lambda h: h, k2=k_pe, k2cb=lambda h: 0)
        qk_b = A["qk_b"] = _rope(proj, (cb(lay.QB), 2 * H, 1), *rope_full, False, BF16, "rope_qk_dil")
        A["o_b"], A["lse_b"] = _attn_fwd(
            S, H, qk_b, lambda h: h, qk_b, lambda h: H + h, proj, lambda h: cb(lay.VB) + h, HEAD ** -0.5,
            "dilated_fwd", tab=tab, win=win)
        ccol, crow = A["ccol"], A["crow"] = _fox_prep(S, H, proj, cb(lay.FC), A["bias"], "fox_prep")
        A["o_c"], A["lse_c"] = _attn_fwd(
            S, H, proj, lambda h: cb(lay.QC) + h, proj, lambda h: cb(lay.KC) + h, proj, lambda h: cb(lay.VC) + h,
            HEAD ** -0.5, "fox_fwd", ccol=ccol, crow=crow)
        A["o_d"], A["tot_d"] = _stick_fwd(
            S, H, proj, lambda h: cb(lay.QD) + h, lambda h: cb(lay.KD) + h, lambda h: cb(lay.VD) + h,
            HEAD ** -0.5, "stick_fwd")
        mix = A["mix"] = _gn_fwd([A["o_a"], A["o_b"], A["o_c"], A["o_d"]], row(group_norm[l]), "group_norm")
        x1 = A["x1"] = _mm(mix, W["wout"], "out_proj", res=x0)
        h2 = A["h2"] = _rms_fwd(x1, row(ffn_norm[l]), D, 0, BF16, "ffn_norm")
        A["g"], A["u"], A["act"] = _ffn_up(h2, W["wg"], W["wu"], "ffn_up")
        return _mm_down(A["act"], W["wd"], x1, "ffn_down"), A

    def backward(l, dx2, W, A):
        proj = A["proj"]
        G, small = {}, {}
        dgate, dup = _ffn_dact(dx2, W["wd"], A["g"], A["u"], "ffn_dact")
        G["w_down"] = _mm_dwdown(A["act"], dx2, "dw_down").reshape(NDEV, FB, D)
        dh2 = _mm_dh2(dgate, W["wg"], dup, W["wu"], "ffn_dh")
        G["w_gate"] = _mm_dwgate(A["h2"], dgate, "dw_gate")
        G["w_up"] = _mm_dwgate(A["h2"], dup, "dw_up")
        dx1, small["ffn_norm"] = _rms_bwd(A["x1"], row(ffn_norm[l]), dh2, D, 0, "ffn_norm_bwd", res=dx2)
        dmix = _mm(dx1, W["wout"], "out_proj_dx", tb=True)
        G["w_out"] = _mm(A["mix"], dx1, "dw_out", ta=True, out_dtype=BF16).reshape(NDEV, 4 * GW // NDEV, D)
        do_a, do_b, do_c, do_d, small["group_norm"] = _gn_bwd(
            [A["o_a"], A["o_b"], A["o_c"], A["o_d"]], row(group_norm[l]), dmix, "group_norm_bwd")
        dq_d, dk_d, dv_d = _stick_bwd(
            S, H, proj, lambda h: cb(lay.QD) + h, lambda h: cb(lay.KD) + h, lambda h: cb(lay.VD) + h,
            do_d, A["tot_d"], HEAD ** -0.5, "stick_bwd")
        dq_c, dk_c, dv_c, dcc = _attn_bwd(
            S, H, proj, lambda h: cb(lay.QC) + h, proj, lambda h: cb(lay.KC) + h, proj, lambda h: cb(lay.VC) + h,
            A["o_c"], do_c, A["lse_c"], HEAD ** -0.5, "fox_bwd", ccol=A["ccol"], crow=A["crow"])
        dfc, dbias = _fox_bwd(S, H, proj, cb(lay.FC), A["bias"], dcc, "fox_gate_bwd")
        small["fox_forget_bias"] = dbias[0, :H]
        qk_b = A["qk_b"]
        dq_b, dk_b, dv_b = _attn_bwd(
            S, H, qk_b, lambda h: h, qk_b, lambda h: H + h, proj, lambda h: cb(lay.VB) + h,
            A["o_b"], do_b, A["lse_b"], HEAD ** -0.5, "dilated_bwd", tab=tab, win=win)
        dqk_b = _rope(jnp.concatenate([dq_b, dk_b], 1), (0, 2 * H, 1), *neg(rope_full), False, BF16, "rope_qk_dil_bwd")
        qa, kv = A["qa"], A["kv"]
        dq1, dk1, dv_a, dq2, dk2 = _attn_bwd(
            S, H, qa, lambda h: 2 * h, kv, lambda h: 2 * h, kv, lambda h: 2 * h + 1,
            A["o_a"], do_a, A["lse_a"], QKA ** -0.5, "mla_bwd",
            q2=A["q_pe"], q2cb=lambda h: h, k2=A["k_pe"], k2cb=lambda h: 0)
        dq2 = _rope(dq2, (0, H, 1), *neg(rope_half), True, F32, "rope_q_mla_bwd")
        dk_pe = _rope(dk2.reshape(S, H, LANE).sum(1), (0, 1, 1), *neg(rope_half), True, BF16, "rope_k_mla_bwd")
        dqa = jnp.stack([dq1.reshape(S, H, LANE), dq2.reshape(S, H, LANE)], 2).reshape(S, H * 2 * LANE).astype(BF16)
        dkv = jnp.stack([dk1.reshape(S, H, LANE), dv_a.reshape(S, H, LANE)], 2).reshape(S, H * 2 * LANE).astype(BF16)
        dwuq = _mm(A["qln"], dqa, "dw_uq", ta=True, out_dtype=BF16)
        dwuq = dwuq.reshape(Q_LORA, H, 2 * LANE)[:, :, :QKA].reshape(Q_LORA, NDEV, H * QKA // NDEV)
        G["w_uq"] = jnp.transpose(dwuq, (1, 0, 2))
        dwukv = _mm(A["kvln"], dkv, "dw_ukv", ta=True, out_dtype=BF16).reshape(KV_LORA, NDEV, H * 2 * LANE // NDEV)
        G["w_ukv"] = jnp.transpose(dwukv, (1, 0, 2))
        dqln = _mm(dqa, W["wuq"], "q_up_dx", tb=True)
        dkvln = _mm(dkv, W["wukv"], "kv_up_dx", tb=True)
        dql, small["mla_q_norm"] = _rms_bwd(proj, row(mla_q_norm[l]), dqln, Q_LORA, cb(lay.QL) // 4, "q_norm_bwd")
        dkvl, small["mla_kv_norm"] = _rms_bwd(proj, row(mla_kv_norm[l]), dkvln, KV_LORA, cb(lay.KVL) // 4, "kv_norm_bwd")
        bf = lambda t: t.astype(BF16)
        dproj = jnp.concatenate([
            bf(dql), bf(dkvl), dqk_b, bf(dv_b), bf(dq_c), bf(dk_c), bf(dv_c), bf(dq_d), bf(dk_d), bf(dv_d),
            dk_pe, bf(dfc), jnp.zeros((S, PW - lay.FC - LANE), BF16)], axis=1)
        G["w_in"] = _mm(A["h1"], dproj, "dw_in", ta=True, out_dtype=BF16).reshape(NDEV, D // NDEV, PW)
        dh1 = _mm(dproj, W["win"], "in_proj_dx", tb=True)
        dx0, small["attn_norm"] = _rms_bwd(A["x0"], row(attn_norm[l]), dh1, D, 0, "attn_norm_bwd", res=dx1)
        return dx0, G, small

    big = ["w_in", "w_uq", "w_ukv", "w_out", "w_gate", "w_up", "w_down"]
    Ws, As = [], []
    xc = x
    for l in range(L):
        W = gather(l)
        xc, A = forward(l, xc, W)
        Ws.append(W)
        As.append(A)
    dx, loss_part = _final_loss(xc, row(final_norm), target, "final_loss")
    dx, dfinal = _rms_bwd(xc, row(final_norm), dx, D, 0, "final_norm_bwd")
    slots = [None] * L
    smalls = [None] * L
    for l in reversed(range(L)):
        dx, G, smalls[l] = backward(l, dx, Ws[l], As[l])
        slots[l] = _all_to_all([G[n] for n in big], name="exchange_grads")

    names_small = ["attn_norm", "mla_q_norm", "mla_kv_norm", "fox_forget_bias", "group_norm", "ffn_norm"]
    params = dict(attn_norm=attn_norm, mla_q_norm=mla_q_norm, mla_kv_norm=mla_kv_norm, fox_forget_bias=fox_forget_bias,
                  group_norm=group_norm, ffn_norm=ffn_norm, final_norm=final_norm, w_in=w_in, w_uq=w_uq, w_ukv=w_ukv,
                  w_out=w_out, w_gate=w_gate, w_up=w_up, w_down=w_down)
    moms = dict(attn_norm=(m_attn_norm, v_attn_norm), mla_q_norm=(m_mla_q_norm, v_mla_q_norm),
                mla_kv_norm=(m_mla_kv_norm, v_mla_kv_norm), fox_forget_bias=(m_fox_forget_bias, v_fox_forget_bias),
                group_norm=(m_group_norm, v_group_norm), ffn_norm=(m_ffn_norm, v_ffn_norm),
                final_norm=(m_final_norm, v_final_norm), w_in=(m_w_in, v_w_in), w_uq=(m_w_uq, v_w_uq),
                w_ukv=(m_w_ukv, v_w_ukv), w_out=(m_w_out, v_w_out), w_gate=(m_w_gate, v_w_gate),
                w_up=(m_w_up, v_w_up), w_down=(m_w_down, v_w_down))
    small_list = names_small + ["final_norm"]
    small_grads = [jnp.stack([smalls[l][n].reshape(params[n].shape[1:]) for l in range(L)]) for n in names_small]
    small_grads.append(dfinal.reshape(final_norm.shape))
    shapes = [params[n].shape for n in small_list] + [(LANE,)]
    packed_g = _all_gather([_pack(small_grads + [loss_part.reshape(LANE)])], name="gather_small")[0]
    zero = jnp.zeros((LANE,), F32)
    res_small = _adamw(_pack([params[n] for n in small_list] + [zero]), packed_g,
                       _pack([moms[n][0] for n in small_list] + [zero]),
                       _pack([moms[n][1] for n in small_list] + [zero]), "adamw_small")
    unp = [_unpack(r, shapes) for r in res_small]
    out = {n: tuple(unp[k][i] for k in range(4)) for i, n in enumerate(small_list)}
    loss = unp[0][-1][0]

    for i, n in enumerate(big):
        st = jnp.stack([slots[l][i] for l in range(L)], axis=1)
        if n == "w_in":
            st = lay.unpad(st)
        C = st.shape[-1]
        st = st.reshape(NDEV, -1, C)
        w2 = params[n].reshape(-1, C)
        res = _adamw(w2, st, moms[n][0].reshape(-1, C), moms[n][1].reshape(-1, C), "adamw_" + n)
        out[n] = tuple(r.reshape(params[n].shape) for r in res)

    order = ["attn_norm", "w_in", "mla_q_norm", "w_uq", "mla_kv_norm", "w_ukv", "fox_forget_bias", "group_norm",
             "w_out", "ffn_norm", "w_gate", "w_up", "w_down", "final_norm"]
    return (loss, dx[None], *[out[n][0] for n in order], *[out[n][1] for n in order],
            *[out[n][2] for n in order], *[out[n][3] for n in order])
```

```python
import functools
import math

import numpy as np
import jax
import jax.numpy as jnp
from jax import lax
from jax.experimental import pallas as pl
from jax.experimental.pallas import tpu as pltpu

F32 = jnp.float32
BF16 = jnp.bfloat16
NDEV = 8
LANE = 128
HEAD = 128
Q_LORA = 512
KV_LORA = 512
QK_ROPE = 64
DILATED_PAIRS = ((128, 1), (512, 4), (2048, 16))
ROPE_THETA = 10000.0
EPS = 1e-6
NEG = -1e30
TQ = 256
TK = 128
TKS = 512
VMEM_LIMIT = 48 * 1024 * 1024
ADAM_LR, ADAM_B1, ADAM_B2, ADAM_EPS, ADAM_WD, ADAM_STEP = 0.001, 0.9, 0.999, 1e-08, 0.01, 10
MESH = pl.DeviceIdType.MESH
ANY = pl.BlockSpec(memory_space=pl.ANY)


def _cp(*sem):
    return pltpu.CompilerParams(dimension_semantics=sem, vmem_limit_bytes=VMEM_LIMIT)


def _dot(a, b, ca, cb):
    return lax.dot_general(a, b, (((ca,), (cb,)), ((), ())), preferred_element_type=F32)


def _dot_nn(a, b):
    return _dot(a, b, 1, 0)


def _dot_nt(a, b):
    return _dot(a, b, 1, 1)


def _dot_tn(a, b):
    return _dot(a, b, 0, 0)


def _tile(n, t):
    if n <= t:
        return n
    t -= t % LANE
    while n % t:
        t -= LANE
    return t


def _all_gather(shards, name):
    n = len(shards)

    def body(*refs):
        ins, outs = refs[:n], refs[n:2 * n]
        send_sems, recv_sems, local_sems = refs[2 * n:]
        x, y, c = lax.axis_index("x"), lax.axis_index("y"), lax.axis_index("c")
        me, sibling = (x, y, c), (x, y, 1 - c)
        chips = [(1 - x, y), (x, 1 - y), (1 - x, 1 - y)]

        def copy(a, k, block, to, src=None):
            px, py, pc = block
            rows = outs[a].at[4 * px + 2 * py + pc]
            return pltpu.make_async_remote_copy(
                src_ref=rows if src is None else src, dst_ref=rows,
                send_sem=send_sems.at[a, k], recv_sem=recv_sems.at[a, k],
                device_id=to, device_id_type=MESH)

        started = []
        for a in range(n):
            mine = pltpu.make_async_copy(ins[a], outs[a].at[4 * x + 2 * y + c], local_sems.at[a])
            mine.start()
            started.append(mine)
        sends = []
        for a in range(n):
            first = [copy(a, 0, me, sibling, src=ins[a])]
            first += [copy(a, 1 + j, me, (*chip, c), src=ins[a]) for j, chip in enumerate(chips)]
            for cp in first:
                cp.start()
            sends += first
        for j, chip in enumerate(chips):
            for a in range(n):
                copy(a, 1 + j, (*chip, c), me).wait_recv()
                passed = copy(a, 4 + j, (*chip, c), sibling)
                passed.start()
                sends.append(passed)
        for a in range(n):
            copy(a, 0, sibling, me).wait_recv()
            for j, chip in enumerate(chips):
                copy(a, 4 + j, (*chip, 1 - c), me).wait_recv()
        for cp in sends:
            cp.wait_send()
        for mine in started:
            mine.wait()

    return pl.pallas_call(
        body, name=name,
        out_shape=[jax.ShapeDtypeStruct((NDEV,) + s.shape, s.dtype) for s in shards],
        in_specs=[ANY] * n, out_specs=[ANY] * n,
        scratch_shapes=[pltpu.SemaphoreType.DMA((n, 7)), pltpu.SemaphoreType.DMA((n, 7)),
                        pltpu.SemaphoreType.DMA((n,))],
    )(*shards)


def _all_to_all(arrs, name):
    n = len(arrs)

    def body(*refs):
        ins, outs = refs[:n], refs[n:2 * n]
        send_sems, recv_sems, local_sems = refs[2 * n:]
        x, y, c = lax.axis_index("x"), lax.axis_index("y"), lax.axis_index("c")
        my_id = 4 * x + 2 * y + c
        peers = []
        for k in range(1, NDEV):
            px = 1 - x if k & 4 else x
            py = 1 - y if k & 2 else y
            pc = 1 - c if k & 1 else c
            peers.append((px, py, pc))
        local = []
        for a in range(n):
            cp = pltpu.make_async_copy(ins[a].at[my_id], outs[a].at[my_id], local_sems.at[a])
            cp.start()
            local.append(cp)

        def send(a, k):
            px, py, pc = peers[k]
            return pltpu.make_async_remote_copy(
                src_ref=ins[a].at[4 * px + 2 * py + pc], dst_ref=outs[a].at[my_id],
                send_sem=send_sems.at[a, k], recv_sem=recv_sems.at[a, k],
                device_id=peers[k], device_id_type=MESH)

        def recv(a, k):
            px, py, pc = peers[k]
            return pltpu.make_async_remote_copy(
                src_ref=ins[a].at[my_id], dst_ref=outs[a].at[4 * px + 2 * py + pc],
                send_sem=send_sems.at[a, k], recv_sem=recv_sems.at[a, k],
                device_id=peers[k], device_id_type=MESH)

        sends = [send(a, k) for a in range(n) for k in range(NDEV - 1)]
        for cp in sends:
            cp.start()
        for a in range(n):
            for k in range(NDEV - 1):
                recv(a, k).wait_recv()
        for cp in sends:
            cp.wait_send()
        for cp in local:
            cp.wait()

    return pl.pallas_call(
        body, name=name,
        out_shape=[jax.ShapeDtypeStruct(s.shape, s.dtype) for s in arrs],
        in_specs=[ANY] * n, out_specs=[ANY] * n,
        scratch_shapes=[pltpu.SemaphoreType.DMA((n, 7)), pltpu.SemaphoreType.DMA((n, 7)),
                        pltpu.SemaphoreType.DMA((n,))],
    )(*arrs)


def _mm_call(pairs, grid, a_spec, b_spec, o_spec, out_shape, acc_shape, nk, ca, cb, name,
             res=None, res_spec=None):
    npairs = len(pairs)
    multi = isinstance(out_shape, (list, tuple))
    nout = len(out_shape) if multi else 1

    def body(*refs):
        ab = refs[:2 * npairs]
        r_ref = refs[2 * npairs] if res is not None else None
        o_refs, acc = refs[-1 - nout:-1], refs[-1]
        k = pl.program_id(2)

        @pl.when(k == 0)
        def _():
            acc[...] = jnp.zeros_like(acc)

        tot = None
        for p in range(npairs):
            d = _dot(ab[2 * p][...].astype(BF16), ab[2 * p + 1][...].astype(BF16), ca, cb)
            tot = d if tot is None else tot + d
        acc[...] += tot

        @pl.when(k == nk - 1)
        def _():
            r = acc[...]
            if r_ref is not None:
                r = r + r_ref[...]
            for o_ref in o_refs:
                o_ref[...] = r.astype(o_ref.dtype)

    ops, specs = [], []
    for a, b in pairs:
        ops += [a, b]
        specs += [a_spec, b_spec]
    if res is not None:
        ops.append(res)
        specs.append(res_spec)
    return pl.pallas_call(
        body, name=name, grid=grid, in_specs=specs, out_specs=[o_spec] * nout if multi else o_spec,
        out_shape=out_shape,
        scratch_shapes=[pltpu.VMEM(acc_shape, F32)],
        compiler_params=_cp("parallel", "parallel", "arbitrary"),
    )(*ops)


def _mm(a, b, name, ta=False, tb=False, out_dtype=F32, res=None, tm=1024, tn=1024, tk=512):
    M, K = (a.shape[1], a.shape[0]) if ta else a.shape
    N = b.shape[0] if tb else b.shape[1]
    tm, tn, tk = _tile(M, tm), _tile(N, tn), _tile(K, tk)
    a_spec = pl.BlockSpec((tk, tm), lambda i, j, k: (k, i)) if ta else pl.BlockSpec((tm, tk), lambda i, j, k: (i, k))
    b_spec = pl.BlockSpec((tn, tk), lambda i, j, k: (j, k)) if tb else pl.BlockSpec((tk, tn), lambda i, j, k: (k, j))
    o_spec = pl.BlockSpec((tm, tn), lambda i, j, k: (i, j))
    if isinstance(out_dtype, tuple):
        out_shape = [jax.ShapeDtypeStruct((M, N), d) for d in out_dtype]
    else:
        out_shape = jax.ShapeDtypeStruct((M, N), out_dtype)
    return _mm_call([(a, b)], (M // tm, N // tn, K // tk), a_spec, b_spec, o_spec,
                    out_shape, (tm, tn), K // tk,
                    0 if ta else 1, 1 if tb else 0, name, res=res, res_spec=o_spec)


def _mm_down(act, wd, res, name, tm=1024, tn=1024):
    _, S, FB = act.shape
    D = wd.shape[1]
    tm, tn = _tile(S, tm), _tile(D, tn)
    o_spec = pl.BlockSpec((tm, tn), lambda i, j, k: (i, j))
    return _mm_call([(act, wd)], (S // tm, D // tn, NDEV),
                    pl.BlockSpec((None, tm, FB), lambda i, j, k: (k, i, 0)),
                    pl.BlockSpec((FB, tn), lambda i, j, k: (k, j)), o_spec,
                    jax.ShapeDtypeStruct((S, D), F32), (tm, tn), NDEV, 1, 0, name, res=res, res_spec=o_spec)


def _mm_dwdown(act, dy, name, tn=1024, tk=512):
    _, S, FB = act.shape
    D = dy.shape[1]
    tn, tk = _tile(D, tn), _tile(S, tk)
    return _mm_call([(act, dy)], (NDEV, D // tn, S // tk),
                    pl.BlockSpec((None, tk, FB), lambda i, j, k: (i, k, 0)),
                    pl.BlockSpec((tk, tn), lambda i, j, k: (k, j)),
                    pl.BlockSpec((FB, tn), lambda i, j, k: (i, j)),
                    jax.ShapeDtypeStruct((NDEV * FB, D), BF16), (FB, tn), S // tk, 0, 0, name)


def _mm_dh2(dg, wg, du, wu, name, tm=1024, tn=1024):
    _, S, FB = dg.shape
    D = wg.shape[1]
    tm, tn = _tile(S, tm), _tile(D, tn)
    return _mm_call([(dg, wg), (du, wu)], (S // tm, D // tn, NDEV),
                    pl.BlockSpec((None, tm, FB), lambda i, j, k: (k, i, 0)),
                    pl.BlockSpec((None, tn, FB), lambda i, j, k: (k, j, 0)),
                    pl.BlockSpec((tm, tn), lambda i, j, k: (i, j)),
                    jax.ShapeDtypeStruct((S, D), F32), (tm, tn), NDEV, 1, 1, name)


def _mm_dwgate(h2, dg, name, tm=1024, tk=512):
    _, S, FB = dg.shape
    D = h2.shape[1]
    tm, tk = _tile(D, tm), _tile(S, tk)
    return _mm_call([(h2, dg)], (NDEV, D // tm, S // tk),
                    pl.BlockSpec((tk, tm), lambda p, i, k: (k, i)),
                    pl.BlockSpec((None, tk, FB), lambda p, i, k: (p, k, 0)),
                    pl.BlockSpec((None, tm, FB), lambda p, i, k: (p, i, 0)),
                    jax.ShapeDtypeStruct((NDEV, D, FB), BF16), (tm, FB), S // tk, 0, 0, name)


def _ffn_up(h2, wg, wu, name, tm=512):
    S, D = h2.shape
    FB = wg.shape[2]
    tm = _tile(S, tm)

    def body(h_ref, wg_ref, wu_ref, g_ref, u_ref, act_ref):
        h = h_ref[...]
        g = _dot_nn(h, wg_ref[...])
        u = _dot_nn(h, wu_ref[...])
        g_ref[...] = g
        u_ref[...] = u
        act_ref[...] = (g / (1.0 + jnp.exp(-g)) * u).astype(BF16)

    w_spec = pl.BlockSpec((None, D, FB), lambda p, i: (p, 0, 0))
    o_spec = pl.BlockSpec((None, tm, FB), lambda p, i: (p, i, 0))
    shp = (NDEV, S, FB)
    return pl.pallas_call(
        body, name=name, grid=(NDEV, S // tm),
        in_specs=[pl.BlockSpec((tm, D), lambda p, i: (i, 0)), w_spec, w_spec],
        out_specs=[o_spec, o_spec, o_spec],
        out_shape=[jax.ShapeDtypeStruct(shp, F32), jax.ShapeDtypeStruct(shp, F32), jax.ShapeDtypeStruct(shp, BF16)],
        compiler_params=_cp("parallel", "parallel"),
    )(h2, wg, wu)


def _ffn_dact(dy, wd, g, u, name, tm=512):
    S, D = dy.shape
    FB = g.shape[2]
    tm = _tile(S, tm)

    def body(dy_ref, wd_ref, g_ref, u_ref, dg_ref, du_ref):
        dact = _dot_nt(dy_ref[...].astype(BF16), wd_ref[...])
        gv = g_ref[...]
        sg = 1.0 / (1.0 + jnp.exp(-gv))
        dg_ref[...] = (dact * u_ref[...] * (sg * (1.0 + gv * (1.0 - sg)))).astype(BF16)
        du_ref[...] = (dact * (gv * sg)).astype(BF16)

    t_spec = pl.BlockSpec((None, tm, FB), lambda p, i: (p, i, 0))
    shp = jax.ShapeDtypeStruct((NDEV, S, FB), BF16)
    return pl.pallas_call(
        body, name=name, grid=(NDEV, S // tm),
        in_specs=[pl.BlockSpec((tm, D), lambda p, i: (i, 0)), pl.BlockSpec((FB, D), lambda p, i: (p, 0)),
                  t_spec, t_spec],
        out_specs=[t_spec, t_spec], out_shape=[shp, shp],
        compiler_params=_cp("parallel", "parallel"),
    )(dy, wd, g, u)


def _rms_fwd(x, gain, width, cb, out_dtype, name, ts=512):
    S = x.shape[0]
    ts = _tile(S, ts)

    def body(x_ref, g_ref, o_ref):
        xv = x_ref[...]
        r = lax.rsqrt(jnp.mean(xv * xv, axis=1, keepdims=True) + EPS)
        o_ref[...] = (xv * r * g_ref[...]).astype(o_ref.dtype)

    return pl.pallas_call(
        body, name=name, grid=(S // ts,),
        in_specs=[pl.BlockSpec((ts, width), lambda i: (i, cb)), pl.BlockSpec((1, width), lambda i: (0, 0))],
        out_specs=pl.BlockSpec((ts, width), lambda i: (i, 0)),
        out_shape=jax.ShapeDtypeStruct((S, width), out_dtype),
        compiler_params=_cp("parallel"),
    )(x, gain)


def _rms_bwd(x, gain, dy, width, cb, name, res=None, ts=256):
    S = x.shape[0]
    ts = _tile(S, ts)
    has_res = res is not None

    def body(*refs):
        x_ref, g_ref, dy_ref = refs[:3]
        r_ref = refs[3] if has_res else None
        dx_ref, dg_ref = refs[-2], refs[-1]

        @pl.when(pl.program_id(0) == 0)
        def _():
            dg_ref[...] = jnp.zeros_like(dg_ref)

        xv = x_ref[...]
        r = lax.rsqrt(jnp.mean(xv * xv, axis=1, keepdims=True) + EPS)
        xh = xv * r
        dyv = dy_ref[...]
        dyg = dyv * g_ref[...]
        dx = r * (dyg - xh * jnp.mean(dyg * xh, axis=1, keepdims=True))
        if has_res:
            dx = dx + r_ref[...]
        dx_ref[...] = dx
        dg_ref[...] += jnp.sum(dyv * xh, axis=0, keepdims=True)

    row = pl.BlockSpec((ts, width), lambda i: (i, 0))
    vec = pl.BlockSpec((1, width), lambda i: (0, 0))
    ops = [x, gain, dy] + ([res] if has_res else [])
    specs = [pl.BlockSpec((ts, width), lambda i: (i, cb)), vec, row] + ([row] if has_res else [])
    return pl.pallas_call(
        body, name=name, grid=(S // ts,), in_specs=specs, out_specs=[row, vec],
        out_shape=[jax.ShapeDtypeStruct((S, width), F32), jax.ShapeDtypeStruct((1, width), F32)],
        compiler_params=_cp("arbitrary"),
    )(*ops)


def _gn_fwd(outs, gain, name, ts=512):
    S, GW = outs[0].shape
    ts = _tile(S, ts)

    def body(a_ref, b_ref, c_ref, d_ref, g_ref, o_ref):
        for g, r_ref in enumerate((a_ref, b_ref, c_ref, d_ref)):
            xv = r_ref[...]
            r = lax.rsqrt(jnp.mean(xv * xv, axis=1, keepdims=True) + EPS)
            o_ref[:, g * GW:(g + 1) * GW] = (xv * r * g_ref[:, g * GW:(g + 1) * GW]).astype(BF16)

    row = pl.BlockSpec((ts, GW), lambda i: (i, 0))
    return pl.pallas_call(
        body, name=name, grid=(S // ts,),
        in_specs=[row] * 4 + [pl.BlockSpec((1, 4 * GW), lambda i: (0, 0))],
        out_specs=pl.BlockSpec((ts, 4 * GW), lambda i: (i, 0)),
        out_shape=jax.ShapeDtypeStruct((S, 4 * GW), BF16),
        compiler_params=_cp("parallel"),
    )(*outs, gain)


def _gn_bwd(outs, gain, dmix, name, ts=256):
    S, GW = outs[0].shape
    ts = _tile(S, ts)

    def body(a_ref, b_ref, c_ref, d_ref, g_ref, dm_ref, da_ref, db_ref, dc_ref, dd_ref, dg_ref):
        @pl.when(pl.program_id(0) == 0)
        def _():
            dg_ref[...] = jnp.zeros_like(dg_ref)

        for g, (r_ref, o_ref) in enumerate(zip((a_ref, b_ref, c_ref, d_ref), (da_ref, db_ref, dc_ref, dd_ref))):
            sl = slice(g * GW, (g + 1) * GW)
            xv = r_ref[...]
            r = lax.rsqrt(jnp.mean(xv * xv, axis=1, keepdims=True) + EPS)
            xh = xv * r
            dyv = dm_ref[:, sl]
            dyg = dyv * g_ref[:, sl]
            o_ref[...] = (r * (dyg - xh * jnp.mean(dyg * xh, axis=1, keepdims=True))).astype(BF16)
            dg_ref[:, sl] += jnp.sum(dyv * xh, axis=0, keepdims=True)

    row = pl.BlockSpec((ts, GW), lambda i: (i, 0))
    vec = pl.BlockSpec((1, 4 * GW), lambda i: (0, 0))
    return pl.pallas_call(
        body, name=name, grid=(S // ts,),
        in_specs=[row] * 4 + [vec, pl.BlockSpec((ts, 4 * GW), lambda i: (i, 0))],
        out_specs=[row] * 4 + [vec],
        out_shape=[jax.ShapeDtypeStruct((S, GW), BF16)] * 4 + [jax.ShapeDtypeStruct((1, 4 * GW), F32)],
        compiler_params=_cp("arbitrary"),
    )(*outs, gain, dmix)


def _rope(x, cbs, cos, sin, half, out_dtype, name, ts=512):
    S = x.shape[0]
    cb0, nb, stride = cbs
    ts = _tile(S, ts)

    def body(x_ref, c_ref, s_ref, o_ref):
        xv = x_ref[...].astype(F32)
        if half:
            lane = lax.broadcasted_iota(jnp.int32, xv.shape, 1)
            partner = jnp.where(lane % 64 < 32, pltpu.roll(xv, LANE - 32, 1), pltpu.roll(xv, 32, 1))
        else:
            partner = pltpu.roll(xv, 64, 1)
        o_ref[...] = (xv * c_ref[...] + partner * s_ref[...]).astype(o_ref.dtype)

    tab = pl.BlockSpec((ts, LANE), lambda i, j: (i, 0))
    return pl.pallas_call(
        body, name=name, grid=(S // ts, nb),
        in_specs=[pl.BlockSpec((ts, LANE), lambda i, j: (i, cb0 + stride * j)), tab, tab],
        out_specs=pl.BlockSpec((ts, LANE), lambda i, j: (i, j)),
        out_shape=jax.ShapeDtypeStruct((S, nb * LANE), out_dtype),
        compiler_params=_cp("parallel", "parallel"),
    )(x, cos, sin)


def _final_loss(x, gain, target, name, ts=256):
    S, D = x.shape
    ts = _tile(S, ts)

    def body(x_ref, g_ref, t_ref, dy_ref, l_ref):
        @pl.when(pl.program_id(0) == 0)
        def _():
            l_ref[...] = jnp.zeros_like(l_ref)

        xv = x_ref[...]
        r = lax.rsqrt(jnp.mean(xv * xv, axis=1, keepdims=True) + EPS)
        err = xv * r * g_ref[...] - t_ref[...]
        dy_ref[...] = err * (1.0 / D)
        part = jnp.sum(jnp.mean(err * err, axis=1, keepdims=True), axis=0, keepdims=True)
        l_ref[...] += jnp.broadcast_to(0.5 * part, (1, LANE))

    row = pl.BlockSpec((ts, D), lambda i: (i, 0))
    return pl.pallas_call(
        body, name=name, grid=(S // ts,),
        in_specs=[row, pl.BlockSpec((1, D), lambda i: (0, 0)), row],
        out_specs=[row, pl.BlockSpec((1, LANE), lambda i: (0, 0))],
        out_shape=[jax.ShapeDtypeStruct((S, D), F32), jax.ShapeDtypeStruct((1, LANE), F32)],
        compiler_params=_cp("arbitrary"),
    )(x, gain, target)


def _colspec(rows, f):
    return pl.BlockSpec((rows, LANE), f)


def _soft_tiles(S):
    tq = _tile(S, TQ)
    tk = _tile(S, TKS)
    assert tk % tq == 0
    return tq, tk


def _key_row(crow_ref, j, tk):
    n = tk // TK
    return jnp.concatenate([crow_ref[j * n + c] for c in range(n)], axis=1)


def _attn_fwd(S, H, q1, q1cb, k1, k1cb, v, vcb, scale, name, q2=None, q2cb=None, k2=None, k2cb=None,
              tab=None, win=None, ccol=None, crow=None):
    tq, tk = _soft_tiles(S)
    has2, hastab, hasc = q2 is not None, tab is not None, ccol is not None

    def body(*refs):
        it = iter(refs)
        q1r, k1r, vr = next(it), next(it), next(it)
        q2r, k2r = (next(it), next(it)) if has2 else (None, None)
        tabr = next(it) if hastab else None
        ccolr, crowr = (next(it), next(it)) if hasc else (None, None)
        o_ref, lse_ref = next(it), next(it)
        i = pl.program_id(1)
        q = q1r[...]
        qb2 = q2r[...] if has2 else None
        cq = ccolr[:, 0:1] if hasc else None
        qpos = i * tq + lax.broadcasted_iota(jnp.int32, (tq, tk), 0)
        kio = lax.broadcasted_iota(jnp.int32, (tq, tk), 1)
        j_diag = (i * tq) // tk
        j_lo = jnp.maximum((i * tq - win) // tk, 0) if win else 0

        def step(j, carry, masked):
            m, l, acc = carry
            off = pl.multiple_of(j * tk, tk)
            s = _dot_nt(q, k1r[pl.ds(off, tk), :])
            if has2:
                s = s + _dot_nt(qb2, k2r[pl.ds(off, tk), :])
            s = s * scale
            if hastab:
                s = s + tabr[i - j * (tk // tq)]
            else:
                if hasc:
                    s = s + (cq - _key_row(crowr, j, tk))
                if masked:
                    s = jnp.where(kio + j * tk <= qpos, s, NEG)
            mn = jnp.maximum(m, jnp.max(s, axis=1, keepdims=True))
            p = jnp.exp(s - mn)
            al = jnp.exp(m - mn)
            l = al * l + jnp.sum(p, axis=1, keepdims=True)
            acc = al * acc + _dot_nn(p.astype(BF16), vr[pl.ds(off, tk), :])
            return mn, l, acc

        carry = (jnp.full((tq, 1), NEG, F32), jnp.zeros((tq, 1), F32), jnp.zeros((tq, LANE), F32))
        if hastab:
            carry = lax.fori_loop(j_lo, j_diag + 1, functools.partial(step, masked=False), carry)
        else:
            carry = lax.fori_loop(j_lo, j_diag, functools.partial(step, masked=False), carry)
            carry = step(j_diag, carry, True)
        m, l, acc = carry
        o_ref[...] = acc / l
        lse_ref[...] = jnp.broadcast_to(m + jnp.log(l), (tq, LANE))

    ops = [q1, k1, v]
    specs = [_colspec(tq, lambda h, i: (i, q1cb(h))), _colspec(S, lambda h, i: (0, k1cb(h))),
             _colspec(S, lambda h, i: (0, vcb(h)))]
    if has2:
        ops += [q2, k2]
        specs += [_colspec(tq, lambda h, i: (i, q2cb(h))), _colspec(S, lambda h, i: (0, k2cb(h)))]
    if hastab:
        ops.append(tab)
        specs.append(pl.BlockSpec(tab.shape, lambda h, i: (0, 0, 0)))
    if hasc:
        ops += [ccol, crow]
        specs += [_colspec(tq, lambda h, i: (i, h)),
                  pl.BlockSpec((None, S // TK, 1, TK), lambda h, i: (h, 0, 0, 0))]
    o_spec = _colspec(tq, lambda h, i: (i, h))
    shp = jax.ShapeDtypeStruct((S, H * LANE), F32)
    return pl.pallas_call(
        body, name=name, grid=(H, S // tq), in_specs=specs, out_specs=[o_spec, o_spec], out_shape=[shp, shp],
        compiler_params=_cp("parallel", "arbitrary"),
    )(*ops)


def _attn_bwd(S, H, q1, q1cb, k1, k1cb, v, vcb, o, do, lse, scale, name, q2=None, q2cb=None, k2=None, k2cb=None,
              tab=None, win=None, ccol=None, crow=None):
    tq, tk = _soft_tiles(S)
    has2, hastab, hasc = q2 is not None, tab is not None, ccol is not None

    def body(*refs):
        it = iter(refs)
        q1r, k1r, vr, o_r, do_r, lse_r = (next(it) for _ in range(6))
        q2r, k2r = (next(it), next(it)) if has2 else (None, None)
        tabr = next(it) if hastab else None
        ccolr, crowr = (next(it), next(it)) if hasc else (None, None)
        dq1_r, dk1_r, dv_r = next(it), next(it), next(it)
        dq2_r, dk2_r = (next(it), next(it)) if has2 else (None, None)
        dcr_r = next(it) if hasc else None
        i = pl.program_id(1)

        @pl.when(i == 0)
        def _():
            dk1_r[...] = jnp.zeros_like(dk1_r)
            dv_r[...] = jnp.zeros_like(dv_r)
            if has2:
                dk2_r[...] = jnp.zeros_like(dk2_r)
            if hasc:
                dcr_r[...] = jnp.zeros_like(dcr_r)

        q = q1r[...]
        qb2 = q2r[...] if has2 else None
        dob = do_r[...]
        delta = jnp.sum(dob.astype(F32) * o_r[...], axis=1, keepdims=True)
        lse_c = lse_r[:, 0:1]
        cq = ccolr[:, 0:1] if hasc else None
        qpos = i * tq + lax.broadcasted_iota(jnp.int32, (tq, tk), 0)
        kio = lax.broadcasted_iota(jnp.int32, (tq, tk), 1)
        j_diag = (i * tq) // tk
        j_lo = jnp.maximum((i * tq - win) // tk, 0) if win else 0

        def probs(j, masked):
            off = pl.multiple_of(j * tk, tk)
            kb = k1r[pl.ds(off, tk), :]
            s = _dot_nt(q, kb)
            kb2 = None
            if has2:
                kb2 = k2r[pl.ds(off, tk), :]
                s = s + _dot_nt(qb2, kb2)
            s = s * scale
            if hastab:
                s = s + tabr[i - j * (tk // tq)]
            else:
                if hasc:
                    s = s + (cq - _key_row(crowr, j, tk))
                if masked:
                    s = jnp.where(kio + j * tk <= qpos, s, NEG)
            p = jnp.exp(s - lse_c)
            dp = _dot_nt(dob, vr[pl.ds(off, tk), :])
            return off, kb, kb2, p, dp

        def sweep(fn, carry):
            if hastab:
                return lax.fori_loop(j_lo, j_diag + 1, functools.partial(fn, masked=False), carry)
            carry = lax.fori_loop(j_lo, j_diag, functools.partial(fn, masked=False), carry)
            return fn(j_diag, carry, True)

        if hasc:
            def dstep(j, acc, masked):
                _, _, _, p, dp = probs(j, masked)
                return acc + jnp.sum(p * dp, axis=1, keepdims=True)

            delta = sweep(dstep, jnp.zeros((tq, 1), F32))

        def step(j, carry, masked):
            dq, dq2 = carry
            off, kb, kb2, p, dp = probs(j, masked)
            ds = p * (dp - delta)
            dsb = ds.astype(BF16)
            dq = dq + _dot_nn(dsb, kb)
            dk1_r[pl.ds(off, tk), :] += _dot_tn(dsb, q) * scale
            dv_r[pl.ds(off, tk), :] += _dot_tn(p.astype(BF16), dob)
            if has2:
                dq2 = dq2 + _dot_nn(dsb, kb2)
                dk2_r[pl.ds(off, tk), :] += _dot_tn(dsb, qb2) * scale
            if hasc:
                cs = -jnp.sum(ds, axis=0, keepdims=True)
                for c in range(tk // TK):
                    dcr_r[j * (tk // TK) + c] += cs[:, c * TK:(c + 1) * TK]
            return dq, dq2

        z = jnp.zeros((tq, LANE), F32)
        dq, dq2 = sweep(step, (z, z))
        dq1_r[...] = dq * scale
        if has2:
            dq2_r[...] = dq2 * scale

    qspec = _colspec(tq, lambda h, i: (i, h))
    kspec = _colspec(S, lambda h, i: (0, h))
    ops = [q1, k1, v, o, do, lse]
    specs = [_colspec(tq, lambda h, i: (i, q1cb(h))), _colspec(S, lambda h, i: (0, k1cb(h))),
             _colspec(S, lambda h, i: (0, vcb(h))), qspec, qspec, qspec]
    if has2:
        ops += [q2, k2]
        specs += [_colspec(tq, lambda h, i: (i, q2cb(h))), _colspec(S, lambda h, i: (0, k2cb(h)))]
    if hastab:
        ops.append(tab)
        specs.append(pl.BlockSpec(tab.shape, lambda h, i: (0, 0, 0)))
    if hasc:
        ops += [ccol, crow]
        specs += [qspec, pl.BlockSpec((None, S // TK, 1, TK), lambda h, i: (h, 0, 0, 0))]
    assert all(t.dtype == BF16 for t in ops[:3] + [do] + ([q2, k2] if has2 else []))
    out_specs = [qspec, kspec, kspec] + ([qspec, kspec] if has2 else [])
    shp = jax.ShapeDtypeStruct((S, H * LANE), F32)
    out_shape = [shp] * len(out_specs)
    if hasc:
        out_specs.append(pl.BlockSpec((None, S // TK, 1, TK), lambda h, i: (h, 0, 0, 0)))
        out_shape.append(jax.ShapeDtypeStruct((H, S // TK, 1, TK), F32))
    return pl.pallas_call(
        body, name=name, grid=(H, S // tq), in_specs=specs, out_specs=out_specs, out_shape=out_shape,
        compiler_params=_cp("parallel", "arbitrary"),
    )(*ops)


def _scan_matrix(kind):
    j = np.arange(TK)[:, None]
    s = np.arange(TK)[None, :]
    tri = {"suffix_ex": j > s, "prefix_in": j <= s, "prefix_ex": j < s}[kind].astype(np.float32)
    half = np.concatenate([tri, np.ones((TK, TK), np.float32)], axis=1)
    return jnp.asarray(np.concatenate([half, half], axis=0), BF16)


def _scan_mxu(x, mat):
    hi = x.astype(BF16)
    lo = (x - hi.astype(F32)).astype(BF16)
    r = _dot_nn(jnp.concatenate([hi, lo], axis=1), mat)
    return r[:, :TK], r[:, TK:]


def _stick_logs(z):
    e = jnp.exp(-jnp.abs(z))
    return e, -jnp.maximum(z, 0.0) - jnp.log(1.0 + e)


def _stick_fwd(S, H, x, qcb, kcb, vcb, scale, name):
    tq, tk = _tile(S, TQ), TK
    r = tq // tk
    assert x.dtype == BF16

    def body(q_r, k_r, v_r, mat_r, o_ref, t_ref):
        i = pl.program_id(1)
        q = q_r[...]
        qpos = i * tq + lax.broadcasted_iota(jnp.int32, (tq, tk), 0)
        lane = lax.broadcasted_iota(jnp.int32, (tq, tk), 1)

        def step(j, carry, masked):
            c, acc = carry
            off = pl.multiple_of(j * tk, tk)
            z = _dot_nt(q, k_r[pl.ds(off, tk), :]) * scale
            _, lk = _stick_logs(z)
            if masked:
                past = lane + j * tk < qpos
                lk = jnp.where(past, lk, 0.0)
            suf, tot = _scan_mxu(lk, mat_r[...])
            a = jnp.exp(z + lk + suf + c)
            if masked:
                a = jnp.where(past, a, 0.0)
            acc = acc + _dot_nn(a.astype(BF16), v_r[pl.ds(off, tk), :])
            return c + tot, acc

        carry = (jnp.zeros((tq, tk), F32), jnp.zeros((tq, LANE), F32))
        for d in reversed(range(r)):
            carry = step(i * r + d, carry, True)
        c, acc = lax.fori_loop(0, i * r, lambda jj, cr: step(i * r - 1 - jj, cr, False), carry)
        o_ref[...] = acc
        t_ref[...] = c

    o_spec = _colspec(tq, lambda h, i: (i, h))
    shp = jax.ShapeDtypeStruct((S, H * LANE), F32)
    mat = _scan_matrix("suffix_ex")
    return pl.pallas_call(
        body, name=name, grid=(H, S // tq),
        in_specs=[_colspec(tq, lambda h, i: (i, qcb(h))), _colspec(S, lambda h, i: (0, kcb(h))),
                  _colspec(S, lambda h, i: (0, vcb(h))), pl.BlockSpec(mat.shape, lambda h, i: (0, 0))],
        out_specs=[o_spec, o_spec], out_shape=[shp, shp],
        compiler_params=_cp("parallel", "arbitrary"),
    )(x, x, x, mat)


def _stick_bwd(S, H, x, qcb, kcb, vcb, do, tot, scale, name):
    tq, tk = _tile(S, TQ), TK
    r = tq // tk
    assert x.dtype == BF16 and do.dtype == BF16

    def body(q_r, k_r, v_r, do_r, t_r, pin_r, pex_r, dq_r, dk_r, dv_r):
        i = pl.program_id(1)

        @pl.when(i == 0)
        def _():
            dk_r[...] = jnp.zeros_like(dk_r)
            dv_r[...] = jnp.zeros_like(dv_r)

        q = q_r[...]
        dob = do_r[...]
        total = t_r[...]
        qpos = i * tq + lax.broadcasted_iota(jnp.int32, (tq, tk), 0)
        lane = lax.broadcasted_iota(jnp.int32, (tq, tk), 1)

        def step(j, carry, masked):
            cl, cg, dq = carry
            off = pl.multiple_of(j * tk, tk)
            kb = k_r[pl.ds(off, tk), :]
            z = _dot_nt(q, kb) * scale
            e, lk = _stick_logs(z)
            if masked:
                past = lane + j * tk < qpos
                lk = jnp.where(past, lk, 0.0)
            pre, tl = _scan_mxu(lk, pin_r[...])
            a = jnp.exp(z + lk + (total - cl - pre))
            if masked:
                a = jnp.where(past, a, 0.0)
            g = _dot_nt(dob, v_r[pl.ds(off, tk), :]) * a
            gpre, tg = _scan_mxu(g, pex_r[...])
            inv = 1.0 / (1.0 + e)
            small = e * inv
            pos = z >= 0
            dz = g * jnp.where(pos, small, inv) - jnp.where(pos, inv, small) * (cg + gpre)
            if masked:
                dz = jnp.where(past, dz, 0.0)
            dzb = dz.astype(BF16)
            dk_r[pl.ds(off, tk), :] += _dot_tn(dzb, q) * scale
            dv_r[pl.ds(off, tk), :] += _dot_tn(a.astype(BF16), dob)
            return cl + tl, cg + tg, dq + _dot_nn(dzb, kb)

        zt = jnp.zeros((tq, tk), F32)
        carry = lax.fori_loop(0, i * r, functools.partial(step, masked=False), (zt, zt, jnp.zeros((tq, LANE), F32)))
        for d in range(r):
            carry = step(i * r + d, carry, True)
        dq_r[...] = carry[2] * scale

    qspec = _colspec(tq, lambda h, i: (i, h))
    kspec = _colspec(S, lambda h, i: (0, h))
    shp = jax.ShapeDtypeStruct((S, H * LANE), F32)
    pin, pex = _scan_matrix("prefix_in"), _scan_matrix("prefix_ex")
    mspec = pl.BlockSpec(pin.shape, lambda h, i: (0, 0))
    return pl.pallas_call(
        body, name=name, grid=(H, S // tq),
        in_specs=[_colspec(tq, lambda h, i: (i, qcb(h))), _colspec(S, lambda h, i: (0, kcb(h))),
                  _colspec(S, lambda h, i: (0, vcb(h))), qspec, qspec, mspec, mspec],
        out_specs=[qspec, kspec, kspec], out_shape=[shp] * 3,
        compiler_params=_cp("parallel", "arbitrary"),
    )(x, x, x, do, tot, pin, pex)


def _scan8(x, rows, reverse):
    for sh in (1, 2, 4):
        if reverse:
            x = x + jnp.where(rows + sh < 8, pltpu.roll(x, 8 - sh, 0), 0.0)
        else:
            x = x + jnp.where(rows >= sh, pltpu.roll(x, sh, 0), 0.0)
    return x


def _fox_prep(S, H, proj, fcb, bias, name):
    tk = TK

    def body(f_ref, b_ref, ccol_ref, crow_ref, scr):
        rows = lax.broadcasted_iota(jnp.int32, (8, LANE), 0)

        def step(t, carry):
            off = pl.multiple_of(t * 8, 8)
            xb = f_ref[pl.ds(off, 8), :] + b_ref[...]
            lf = jnp.minimum(xb, 0.0) - jnp.log(1.0 + jnp.exp(-jnp.abs(xb)))
            lf = _scan8(lf, rows, False) + carry
            scr[pl.ds(off, 8), :] = lf
            return lf[7:8, :]

        lax.fori_loop(0, S // 8, step, jnp.zeros((1, LANE), F32))
        for h in range(H):
            ccol_ref[:, h * LANE:(h + 1) * LANE] = jnp.broadcast_to(scr[:, h:h + 1], (S, LANE))

            def tr(t, _):
                off = pl.multiple_of(t * tk, tk)
                blk = ccol_ref[pl.ds(off, tk), h * LANE:(h + 1) * LANE]
                crow_ref[h, t] = blk.T[0:1, :]
                return 0

            lax.fori_loop(0, S // tk, tr, 0)

    return pl.pallas_call(
        body, name=name, grid=(1,),
        in_specs=[_colspec(S, lambda i: (0, fcb)), pl.BlockSpec((1, LANE), lambda i: (0, 0))],
        out_specs=[pl.BlockSpec((S, H * LANE), lambda i: (0, 0)),
                   pl.BlockSpec((H, S // tk, 1, tk), lambda i: (0, 0, 0, 0))],
        out_shape=[jax.ShapeDtypeStruct((S, H * LANE), F32), jax.ShapeDtypeStruct((H, S // tk, 1, tk), F32)],
        scratch_shapes=[pltpu.VMEM((S, LANE), F32)],
        compiler_params=_cp("arbitrary"),
    )(proj, bias)


def _fox_bwd(S, H, proj, fcb, bias, dcr, name):
    tk = TK

    def body(f_ref, b_ref, dcr_ref, df_ref, db_ref, scr):
        rows = lax.broadcasted_iota(jnp.int32, (8, LANE), 0)
        lane_t = lax.broadcasted_iota(jnp.int32, (tk, LANE), 1)
        nb = S // 8

        def tr(t, _):
            off = pl.multiple_of(t * tk, tk)
            d = jnp.zeros((tk, LANE), F32)
            for h in range(H):
                d = d + jnp.where(lane_t == h, jnp.broadcast_to(dcr_ref[h, t], (LANE, tk)).T, 0.0)
            scr[pl.ds(off, tk), :] = d
            return 0

        lax.fori_loop(0, S // tk, tr, 0)

        def step(tt, carry):
            suffix, db = carry
            off = pl.multiple_of((nb - 1 - tt) * 8, 8)
            d = _scan8(scr[pl.ds(off, 8), :], rows, True) + suffix
            xb = f_ref[pl.ds(off, 8), :] + b_ref[...]
            e = jnp.exp(-jnp.abs(xb))
            dx = d * jnp.where(xb >= 0, e, 1.0) / (1.0 + e)
            df_ref[pl.ds(off, 8), :] = dx
            return d[0:1, :], db + jnp.sum(dx, axis=0, keepdims=True)

        z = jnp.zeros((1, LANE), F32)
        _, db = lax.fori_loop(0, nb, step, (z, z))
        db_ref[...] = db

    return pl.pallas_call(
        body, name=name, grid=(1,),
        in_specs=[_colspec(S, lambda i: (0, fcb)), pl.BlockSpec((1, LANE), lambda i: (0, 0)),
                  pl.BlockSpec((H, S // tk, 1, tk), lambda i: (0, 0, 0, 0))],
        out_specs=[pl.BlockSpec((S, LANE), lambda i: (0, 0)), pl.BlockSpec((1, LANE), lambda i: (0, 0))],
        out_shape=[jax.ShapeDtypeStruct((S, LANE), F32), jax.ShapeDtypeStruct((1, LANE), F32)],
        scratch_shapes=[pltpu.VMEM((S, LANE), F32)],
        compiler_params=_cp("arbitrary"),
    )(proj, bias, dcr)


def _adamw(w, slots, m, v, name, block_bytes=1 << 20):
    R, C = w.shape
    tr = R
    while tr * C * 4 > block_bytes and tr % 16 == 0:
        tr //= 2
    c1 = 1.0 - ADAM_B1 ** ADAM_STEP
    c2 = 1.0 - ADAM_B2 ** ADAM_STEP

    def body(w_ref, s_ref, m_ref, v_ref, g_ref, d_ref, nm_ref, nv_ref):
        g = s_ref[0].astype(F32)
        for s in range(1, NDEV):
            g = g + s_ref[s].astype(F32)
        mn = ADAM_B1 * m_ref[...] + (1.0 - ADAM_B1) * g
        vn = ADAM_B2 * v_ref[...] + (1.0 - ADAM_B2) * (g * g)
        g_ref[...] = g
        nm_ref[...] = mn
        nv_ref[...] = vn
        d_ref[...] = -ADAM_LR * ((mn / c1) / (jnp.sqrt(vn / c2) + ADAM_EPS) + ADAM_WD * w_ref[...])

    row = pl.BlockSpec((tr, C), lambda i: (i, 0))
    return pl.pallas_call(
        body, name=name, grid=(R // tr,),
        in_specs=[row, pl.BlockSpec((NDEV, tr, C), lambda i: (0, i, 0)), row, row],
        out_specs=[row] * 4, out_shape=[jax.ShapeDtypeStruct((R, C), F32)] * 4,
        compiler_params=_cp("parallel"),
    )(w, slots, m, v)


class _Layout:
    def __init__(self, D):
        self.GW = GW = D // 4
        self.H = H = GW // HEAD
        self.QL, self.KVL = 0, Q_LORA
        base = Q_LORA + KV_LORA
        (self.QB, self.KB, self.VB, self.QC, self.KC, self.VC, self.QD, self.KD, self.VD) = (
            base + k * GW for k in range(9))
        self.KR = base + 9 * GW
        self.FC = self.KR + LANE
        self.PW = -(-(self.FC + LANE) // 512) * 512
        self.o_kr = base
        self.o_bc = base + QK_ROPE
        self.o_fc = self.o_bc + 6 * GW
        self.o_d = self.o_fc + H
        self.IN = self.o_d + 3 * GW

    def pad(self, w):
        z = lambda n: jnp.zeros(w.shape[:-1] + (n,), w.dtype)
        return jnp.concatenate([
            w[..., :self.o_kr], w[..., self.o_bc:self.o_fc], w[..., self.o_d:self.IN],
            w[..., self.o_kr:self.o_bc], z(LANE - QK_ROPE), w[..., self.o_fc:self.o_d], z(LANE - self.H),
            z(self.PW - self.FC - LANE)], axis=-1)

    def unpad(self, g):
        return jnp.concatenate([
            g[..., :self.KR - 9 * self.GW], g[..., self.KR:self.KR + QK_ROPE], g[..., self.QB:self.QD],
            g[..., self.FC:self.FC + self.H], g[..., self.QD:self.KR]], axis=-1)


def _rope_tables(S):
    pos = jnp.arange(S, dtype=F32)

    def cs(dim):
        inv = ROPE_THETA ** (-jnp.arange(0, dim, 2, dtype=F32) / dim)
        ang = pos[:, None] * inv[None, :]
        return jnp.cos(ang), jnp.sin(ang)

    c, s = cs(HEAD)
    full = (jnp.concatenate([c, c], 1), jnp.concatenate([-s, s], 1))
    c, s = cs(QK_ROPE)
    z = jnp.zeros((S, LANE - QK_ROPE), F32)
    half = (jnp.concatenate([c, c, z], 1), jnp.concatenate([-s, s, z], 1))
    return full, half


def _dilated_table(tq, tk):
    win = max(w for w, _ in DILATED_PAIRS)
    nd = (win + tk) // tq + 1
    d = np.arange(nd)[:, None, None] * tq + np.arange(tq)[None, :, None] - np.arange(tk)[None, None, :]
    mult = np.zeros(d.shape, np.float64)
    for w, dil in DILATED_PAIRS:
        mult += (d >= 0) & (d <= w) & (d % dil == 0)
    return jnp.asarray(np.where(mult > 0, np.log(np.maximum(mult, 1.0)), NEG), F32), win


def _pack(arrs):
    rows = []
    for a in arrs:
        f = a.reshape(-1).astype(F32)
        f = jnp.pad(f, (0, (-f.shape[0]) % LANE))
        rows.append(f.reshape(-1, LANE))
    p = jnp.concatenate(rows, 0)
    return jnp.pad(p, ((0, (-p.shape[0]) % 8), (0, 0)))


def _unpack(p, shapes):
    out, r = [], 0
    for shp in shapes:
        n = int(np.prod(shp))
        nr = -(-n // LANE)
        out.append(p[r:r + nr].reshape(-1)[:n].reshape(shp))
        r += nr
    return out


def kernel(x, attn_norm, w_in, mla_q_norm, w_uq, mla_kv_norm, w_ukv, fox_forget_bias, group_norm, w_out, ffn_norm, w_gate, w_up, w_down, final_norm, loss_target, m_attn_norm, m_w_in, m_mla_q_norm, m_w_uq, m_mla_kv_norm, m_w_ukv, m_fox_forget_bias, m_group_norm, m_w_out, m_ffn_norm, m_w_gate, m_w_up, m_w_down, m_final_norm, v_attn_norm, v_w_in, v_mla_q_norm, v_w_uq, v_mla_kv_norm, v_w_ukv, v_fox_forget_bias, v_group_norm, v_w_out, v_ffn_norm, v_w_gate, v_w_up, v_w_down, v_final_norm):
    _, S, D = x.shape
    L = attn_norm.shape[0]
    lay = _Layout(D)
    H, GW, PW = lay.H, lay.GW, lay.PW
    FB = w_gate.shape[2]
    QKA = HEAD + QK_ROPE
    x = x[0]
    target = loss_target[0]
    rope_full, rope_half = _rope_tables(S)
    neg = lambda t: (t[0], -t[1])
    tab, win = _dilated_table(*_soft_tiles(S))
    cb = lambda col: col // LANE

    win_sh = lay.pad(w_in).astype(BF16)
    sh = [w.astype(BF16) for w in (w_uq, w_ukv, w_out, w_gate, w_up, w_down)]

    def gather(l):
        g = _all_gather([win_sh[l]] + [s[l] for s in sh], name="gather_weights")
        wuq = jnp.transpose(g[1], (1, 0, 2)).reshape(Q_LORA, H, QKA)
        wuq = jnp.pad(wuq, ((0, 0), (0, 0), (0, 2 * LANE - QKA))).reshape(Q_LORA, H * 2 * LANE)
        return dict(win=g[0].reshape(D, PW), wuq=wuq,
                    wukv=jnp.transpose(g[2], (1, 0, 2)).reshape(KV_LORA, H * 2 * LANE),
                    wout=g[3].reshape(4 * GW, D), wg=g[4], wu=g[5], wd=g[6].reshape(NDEV * FB, D))

    def row(a):
        return a.reshape(1, -1)

    def forward(l, x0, W):
        A = dict(x0=x0)
        A["bias"] = jnp.pad(row(fox_forget_bias[l]), ((0, 0), (0, LANE - H)))
        h1 = A["h1"] = _rms_fwd(x0, row(attn_norm[l]), D, 0, BF16, "attn_norm")
        proj, pb = A["proj"], A["pb"] = _mm(h1, W["win"], "in_proj", out_dtype=(F32, BF16))
        qln = A["qln"] = _rms_fwd(proj, row(mla_q_norm[l]), Q_LORA, cb(lay.QL) // 4, BF16, "q_norm")
        kvln = A["kvln"] = _rms_fwd(proj, row(mla_kv_norm[l]), KV_LORA, cb(lay.KVL) // 4, BF16, "kv_norm")
        qa, qab = _mm(qln, W["wuq"], "q_up", out_dtype=(F32, BF16))
        A["qab"] = qab
        kv = A["kv"] = _mm(kvln, W["wukv"], "kv_up", out_dtype=BF16)
        q_pe = A["q_pe"] = _rope(qa, (1, H, 2), *rope_half, True, BF16, "rope_q_mla")
        k_pe = A["k_pe"] = _rope(proj, (cb(lay.KR), 1, 1), *rope_half, True, BF16, "rope_k_mla")
        A["o_a"], A["lse_a"] = _attn_fwd(
            S, H, qab, lambda h: 2 * h, kv, lambda h: 2 * h, kv, lambda h: 2 * h + 1, QKA ** -0.5, "mla_fwd",
            q2=q_pe, q2cb=lambda h: h, k2=k_pe, k2cb=lambda h: 0)
        qk_b = A["qk_b"] = _rope(proj, (cb(lay.QB), 2 * H, 1), *rope_full, False, BF16, "rope_qk_dil")
        A["o_b"], A["lse_b"] = _attn_fwd(
            S, H, qk_b, lambda h: h, qk_b, lambda h: H + h, pb, lambda h: cb(lay.VB) + h, HEAD ** -0.5,
            "dilated_fwd", tab=tab, win=win)
        ccol, crow = A["ccol"], A["crow"] = _fox_prep(S, H, proj, cb(lay.FC), A["bias"], "fox_prep")
        A["o_c"], A["lse_c"] = _attn_fwd(
            S, H, pb, lambda h: cb(lay.QC) + h, pb, lambda h: cb(lay.KC) + h, pb, lambda h: cb(lay.VC) + h,
            HEAD ** -0.5, "fox_fwd", ccol=ccol, crow=crow)
        A["o_d"], A["tot_d"] = _stick_fwd(
            S, H, pb, lambda h: cb(lay.QD) + h, lambda h: cb(lay.KD) + h, lambda h: cb(lay.VD) + h,
            HEAD ** -0.5, "stick_fwd")
        mix = A["mix"] = _gn_fwd([A["o_a"], A["o_b"], A["o_c"], A["o_d"]], row(group_norm[l]), "group_norm")
        x1 = A["x1"] = _mm(mix, W["wout"], "out_proj", res=x0)
        h2 = A["h2"] = _rms_fwd(x1, row(ffn_norm[l]), D, 0, BF16, "ffn_norm")
        A["g"], A["u"], A["act"] = _ffn_up(h2, W["wg"], W["wu"], "ffn_up")
        return _mm_down(A["act"], W["wd"], x1, "ffn_down"), A

    def backward(l, dx2, W, A):
        proj, pb = A["proj"], A["pb"]
        G, small = {}, {}
        dgate, dup = _ffn_dact(dx2, W["wd"], A["g"], A["u"], "ffn_dact")
        G["w_down"] = _mm_dwdown(A["act"], dx2, "dw_down").reshape(NDEV, FB, D)
        dh2 = _mm_dh2(dgate, W["wg"], dup, W["wu"], "ffn_dh")
        G["w_gate"] = _mm_dwgate(A["h2"], dgate, "dw_gate")
        G["w_up"] = _mm_dwgate(A["h2"], dup, "dw_up")
        dx1, small["ffn_norm"] = _rms_bwd(A["x1"], row(ffn_norm[l]), dh2, D, 0, "ffn_norm_bwd", res=dx2)
        dmix = _mm(dx1, W["wout"], "out_proj_dx", tb=True)
        G["w_out"] = _mm(A["mix"], dx1, "dw_out", ta=True, out_dtype=BF16).reshape(NDEV, 4 * GW // NDEV, D)
        do_a, do_b, do_c, do_d, small["group_norm"] = _gn_bwd(
            [A["o_a"], A["o_b"], A["o_c"], A["o_d"]], row(group_norm[l]), dmix, "group_norm_bwd")
        dq_d, dk_d, dv_d = _stick_bwd(
            S, H, pb, lambda h: cb(lay.QD) + h, lambda h: cb(lay.KD) + h, lambda h: cb(lay.VD) + h,
            do_d, A["tot_d"], HEAD ** -0.5, "stick_bwd")
        dq_c, dk_c, dv_c, dcc = _attn_bwd(
            S, H, pb, lambda h: cb(lay.QC) + h, pb, lambda h: cb(lay.KC) + h, pb, lambda h: cb(lay.VC) + h,
            A["o_c"], do_c, A["lse_c"], HEAD ** -0.5, "fox_bwd", ccol=A["ccol"], crow=A["crow"])
        dfc, dbias = _fox_bwd(S, H, proj, cb(lay.FC), A["bias"], dcc, "fox_gate_bwd")
        small["fox_forget_bias"] = dbias[0, :H]
        qk_b = A["qk_b"]
        dq_b, dk_b, dv_b = _attn_bwd(
            S, H, qk_b, lambda h: h, qk_b, lambda h: H + h, pb, lambda h: cb(lay.VB) + h,
            A["o_b"], do_b, A["lse_b"], HEAD ** -0.5, "dilated_bwd", tab=tab, win=win)
        dqk_b = _rope(jnp.concatenate([dq_b, dk_b], 1), (0, 2 * H, 1), *neg(rope_full), False, BF16, "rope_qk_dil_bwd")
        qab, kv = A["qab"], A["kv"]
        dq1, dk1, dv_a, dq2, dk2 = _attn_bwd(
            S, H, qab, lambda h: 2 * h, kv, lambda h: 2 * h, kv, lambda h: 2 * h + 1,
            A["o_a"], do_a, A["lse_a"], QKA ** -0.5, "mla_bwd",
            q2=A["q_pe"], q2cb=lambda h: h, k2=A["k_pe"], k2cb=lambda h: 0)
        dq2 = _rope(dq2, (0, H, 1), *neg(rope_half), True, F32, "rope_q_mla_bwd")
        dk_pe = _rope(dk2.reshape(S, H, LANE).sum(1), (0, 1, 1), *neg(rope_half), True, BF16, "rope_k_mla_bwd")
        dqa = jnp.stack([dq1.reshape(S, H, LANE), dq2.reshape(S, H, LANE)], 2).reshape(S, H * 2 * LANE).astype(BF16)
        dkv = jnp.stack([dk1.reshape(S, H, LANE), dv_a.reshape(S, H, LANE)], 2).reshape(S, H * 2 * LANE).astype(BF16)
        dwuq = _mm(A["qln"], dqa, "dw_uq", ta=True, out_dtype=BF16)
        dwuq = dwuq.reshape(Q_LORA, H, 2 * LANE)[:, :, :QKA].reshape(Q_LORA, NDEV, H * QKA // NDEV)
        G["w_uq"] = jnp.transpose(dwuq, (1, 0, 2))
        dwukv = _mm(A["kvln"], dkv, "dw_ukv", ta=True, out_dtype=BF16).reshape(KV_LORA, NDEV, H * 2 * LANE // NDEV)
        G["w_ukv"] = jnp.transpose(dwukv, (1, 0, 2))
        dqln = _mm(dqa, W["wuq"], "q_up_dx", tb=True)
        dkvln = _mm(dkv, W["wukv"], "kv_up_dx", tb=True)
        dql, small["mla_q_norm"] = _rms_bwd(proj, row(mla_q_norm[l]), dqln, Q_LORA, cb(lay.QL) // 4, "q_norm_bwd")
        dkvl, small["mla_kv_norm"] = _rms_bwd(proj, row(mla_kv_norm[l]), dkvln, KV_LORA, cb(lay.KVL) // 4, "kv_norm_bwd")
        bf = lambda t: t.astype(BF16)
        dproj = jnp.concatenate([
            bf(dql), bf(dkvl), dqk_b, bf(dv_b), bf(dq_c), bf(dk_c), bf(dv_c), bf(dq_d), bf(dk_d), bf(dv_d),
            dk_pe, bf(dfc), jnp.zeros((S, PW - lay.FC - LANE), BF16)], axis=1)
        G["w_in"] = _mm(A["h1"], dproj, "dw_in", ta=True, out_dtype=BF16).reshape(NDEV, D // NDEV, PW)
        dh1 = _mm(dproj, W["win"], "in_proj_dx", tb=True)
        dx0, small["attn_norm"] = _rms_bwd(A["x0"], row(attn_norm[l]), dh1, D, 0, "attn_norm_bwd", res=dx1)
        return dx0, G, small

    big = ["w_in", "w_uq", "w_ukv", "w_out", "w_gate", "w_up", "w_down"]
    Ws, As = [], []
    xc = x
    for l in range(L):
        W = gather(l)
        xc, A = forward(l, xc, W)
        Ws.append(W)
        As.append(A)
    dx, loss_part = _final_loss(xc, row(final_norm), target, "final_loss")
    dx, dfinal = _rms_bwd(xc, row(final_norm), dx, D, 0, "final_norm_bwd")
    slots = [None] * L
    smalls = [None] * L
    for l in reversed(range(L)):
        dx, G, smalls[l] = backward(l, dx, Ws[l], As[l])
        slots[l] = _all_to_all([G[n] for n in big], name="exchange_grads")

    names_small = ["attn_norm", "mla_q_norm", "mla_kv_norm", "fox_forget_bias", "group_norm", "ffn_norm"]
    params = dict(attn_norm=attn_norm, mla_q_norm=mla_q_norm, mla_kv_norm=mla_kv_norm, fox_forget_bias=fox_forget_bias,
                  group_norm=group_norm, ffn_norm=ffn_norm, final_norm=final_norm, w_in=w_in, w_uq=w_uq, w_ukv=w_ukv,
                  w_out=w_out, w_gate=w_gate, w_up=w_up, w_down=w_down)
    moms = dict(attn_norm=(m_attn_norm, v_attn_norm), mla_q_norm=(m_mla_q_norm, v_mla_q_norm),
                mla_kv_norm=(m_mla_kv_norm, v_mla_kv_norm), fox_forget_bias=(m_fox_forget_bias, v_fox_forget_bias),
                group_norm=(m_group_norm, v_group_norm), ffn_norm=(m_ffn_norm, v_ffn_norm),
                final_norm=(m_final_norm, v_final_norm), w_in=(m_w_in, v_w_in), w_uq=(m_w_uq, v_w_uq),
                w_ukv=(m_w_ukv, v_w_ukv), w_out=(m_w_out, v_w_out), w_gate=(m_w_gate, v_w_gate),
                w_up=(m_w_up, v_w_up), w_down=(m_w_down, v_w_down))
    small_list = names_small + ["final_norm"]
    small_grads = [jnp.stack([smalls[l][n].reshape(params[n].shape[1:]) for l in range(L)]) for n in names_small]
    small_grads.append(dfinal.reshape(final_norm.shape))
    shapes = [params[n].shape for n in small_list] + [(LANE,)]
    packed_g = _all_gather([_pack(small_grads + [loss_part.reshape(LANE)])], name="gather_small")[0]
    zero = jnp.zeros((LANE,), F32)
    res_small = _adamw(_pack([params[n] for n in small_list] + [zero]), packed_g,
                       _pack([moms[n][0] for n in small_list] + [zero]),
                       _pack([moms[n][1] for n in small_list] + [zero]), "adamw_small")
    unp = [_unpack(r, shapes) for r in res_small]
    out = {n: tuple(unp[k][i] for k in range(4)) for i, n in enumerate(small_list)}
    loss = unp[0][-1][0]

    for i, n in enumerate(big):
        st = jnp.stack([slots[l][i] for l in range(L)], axis=1)
        if n == "w_in":
            st = lay.unpad(st)
        C = st.shape[-1]
        st = st.reshape(NDEV, -1, C)
        w2 = params[n].reshape(-1, C)
        res = _adamw(w2, st, moms[n][0].reshape(-1, C), moms[n][1].reshape(-1, C), "adamw_" + n)
        out[n] = tuple(r.reshape(params[n].shape) for r in res)

    order = ["attn_norm", "w_in", "mla_q_norm", "w_uq", "mla_kv_norm", "w_ukv", "fox_forget_bias", "group_norm",
             "w_out", "ffn_norm", "w_gate", "w_up", "w_down", "final_norm"]
    return (loss, dx[None], *[out[n][0] for n in order], *[out[n][1] for n in order],
            *[out[n][2] for n in order], *[out[n][3] for n in order])
```

```python
import functools
import math

import numpy as np
import jax
import jax.numpy as jnp
from jax import lax
from jax.experimental import pallas as pl
from jax.experimental.pallas import tpu as pltpu

F32 = jnp.float32
BF16 = jnp.bfloat16
NDEV = 8
LANE = 128
HEAD = 128
Q_LORA = 512
KV_LORA = 512
QK_ROPE = 64
DILATED_PAIRS = ((128, 1), (512, 4), (2048, 16))
ROPE_THETA = 10000.0
EPS = 1e-6
NEG = -1e30
TQ = 256
TK = 128
TKS = 512
VMEM_LIMIT = 48 * 1024 * 1024
ADAM_LR, ADAM_B1, ADAM_B2, ADAM_EPS, ADAM_WD, ADAM_STEP = 0.001, 0.9, 0.999, 1e-08, 0.01, 10
MESH = pl.DeviceIdType.MESH
ANY = pl.BlockSpec(memory_space=pl.ANY)


def _cp(*sem):
    return pltpu.CompilerParams(dimension_semantics=sem, vmem_limit_bytes=VMEM_LIMIT)


def _dot(a, b, ca, cb):
    return lax.dot_general(a, b, (((ca,), (cb,)), ((), ())), preferred_element_type=F32)


def _dot_nn(a, b):
    return _dot(a, b, 1, 0)


def _dot_nt(a, b):
    return _dot(a, b, 1, 1)


def _dot_tn(a, b):
    return _dot(a, b, 0, 0)


def _tile(n, t):
    if n <= t:
        return n
    t -= t % LANE
    while n % t:
        t -= LANE
    return t


def _all_gather(shards, name):
    n = len(shards)

    def body(*refs):
        ins, outs = refs[:n], refs[n:2 * n]
        send_sems, recv_sems, local_sems = refs[2 * n:]
        x, y, c = lax.axis_index("x"), lax.axis_index("y"), lax.axis_index("c")
        me, sibling = (x, y, c), (x, y, 1 - c)
        chips = [(1 - x, y), (x, 1 - y), (1 - x, 1 - y)]

        def copy(a, k, block, to, src=None):
            px, py, pc = block
            rows = outs[a].at[4 * px + 2 * py + pc]
            return pltpu.make_async_remote_copy(
                src_ref=rows if src is None else src, dst_ref=rows,
                send_sem=send_sems.at[a, k], recv_sem=recv_sems.at[a, k],
                device_id=to, device_id_type=MESH)

        started = []
        for a in range(n):
            mine = pltpu.make_async_copy(ins[a], outs[a].at[4 * x + 2 * y + c], local_sems.at[a])
            mine.start()
            started.append(mine)
        sends = []
        for a in range(n):
            first = [copy(a, 0, me, sibling, src=ins[a])]
            first += [copy(a, 1 + j, me, (*chip, c), src=ins[a]) for j, chip in enumerate(chips)]
            for cp in first:
                cp.start()
            sends += first
        for j, chip in enumerate(chips):
            for a in range(n):
                copy(a, 1 + j, (*chip, c), me).wait_recv()
                passed = copy(a, 4 + j, (*chip, c), sibling)
                passed.start()
                sends.append(passed)
        for a in range(n):
            copy(a, 0, sibling, me).wait_recv()
            for j, chip in enumerate(chips):
                copy(a, 4 + j, (*chip, 1 - c), me).wait_recv()
        for cp in sends:
            cp.wait_send()
        for mine in started:
            mine.wait()

    return pl.pallas_call(
        body, name=name,
        out_shape=[jax.ShapeDtypeStruct((NDEV,) + s.shape, s.dtype) for s in shards],
        in_specs=[ANY] * n, out_specs=[ANY] * n,
        scratch_shapes=[pltpu.SemaphoreType.DMA((n, 7)), pltpu.SemaphoreType.DMA((n, 7)),
                        pltpu.SemaphoreType.DMA((n,))],
    )(*shards)


def _direct_copies(kind, ins, outs, send_sems, recv_sems, local_sems, want_recvs=True):
    x, y, c = lax.axis_index("x"), lax.axis_index("y"), lax.axis_index("c")
    my_id = 4 * x + 2 * y + c
    local, sends, recvs = [], [], []
    for a in range(len(ins)):
        mine = ins[a] if kind == "gather" else ins[a].at[my_id]
        local.append(pltpu.make_async_copy(mine, outs[a].at[my_id], local_sems.at[a]))
        for k in range(1, NDEV):
            peer = (1 - x if k & 4 else x, 1 - y if k & 2 else y, 1 - c if k & 1 else c)
            pid = 4 * peer[0] + 2 * peer[1] + peer[2]
            sems = dict(send_sem=send_sems.at[a, k - 1], recv_sem=recv_sems.at[a, k - 1],
                        device_id=peer, device_id_type=MESH)
            sends.append(pltpu.make_async_remote_copy(
                src_ref=ins[a] if kind == "gather" else ins[a].at[pid], dst_ref=outs[a].at[my_id], **sems))
            if want_recvs:
                recvs.append(pltpu.make_async_remote_copy(src_ref=mine, dst_ref=outs[a].at[pid], **sems))
    return local, sends, recvs


def _comm_shapes(kind, arrs):
    out_shape = [jax.ShapeDtypeStruct(((NDEV,) if kind == "gather" else ()) + a.shape, a.dtype) for a in arrs]
    n = len(arrs)
    sems = [pltpu.SemaphoreType.DMA((n, 7)), pltpu.SemaphoreType.DMA((n, 7)), pltpu.SemaphoreType.DMA((n,))]
    return out_shape, sems


def _all_to_all(arrs, name):
    n = len(arrs)

    def body(*refs):
        local, sends, recvs = _direct_copies("exchange", refs[:n], refs[n:2 * n], *refs[2 * n:])
        for cp in local + sends:
            cp.start()
        for cp in recvs:
            cp.wait_recv()
        for cp in sends:
            cp.wait_send()
        for cp in local:
            cp.wait()

    out_shape, sems = _comm_shapes("exchange", arrs)
    return pl.pallas_call(body, name=name, out_shape=out_shape, in_specs=[ANY] * n, out_specs=[ANY] * n,
                          scratch_shapes=sems)(*arrs)


def _pcall(body, *, name, grid, in_specs, out_specs, out_shape, operands, sem, scratch_shapes=(), comm=None):
    in_specs, out_specs, out_shape = list(in_specs), list(out_specs), list(out_shape)
    scratch_shapes = list(scratch_shapes)
    if comm is None:
        res = pl.pallas_call(body, name=name, grid=grid, in_specs=in_specs, out_specs=out_specs, out_shape=out_shape,
                             scratch_shapes=scratch_shapes, compiler_params=_cp(*sem))(*operands)
        return list(res), []
    kind, arrs = comm
    nc, n_in, n_out, n_scr = len(arrs), len(operands), len(out_shape), len(scratch_shapes)
    c_shape, c_sems = _comm_shapes(kind, arrs)

    def carrier(*refs):
        ins, cin = refs[:n_in], refs[n_in:n_in + nc]
        outs = refs[n_in + nc:n_in + nc + n_out]
        cout = refs[n_in + nc + n_out:n_in + 2 * nc + n_out]
        scr = refs[n_in + 2 * nc + n_out:n_in + 2 * nc + n_out + n_scr]
        sems = refs[n_in + 2 * nc + n_out + n_scr:]
        pids = [pl.program_id(d) for d in range(len(grid))]
        first = functools.reduce(jnp.logical_and, [p == 0 for p in pids])
        last = functools.reduce(jnp.logical_and, [p == g - 1 for p, g in zip(pids, grid)])

        @pl.when(first)
        def _():
            local, sends, _ = _direct_copies(kind, cin, cout, *sems, want_recvs=False)
            for cp in local + sends:
                cp.start()

        body(*ins, *outs, *scr)

        @pl.when(last)
        def _():
            local, sends, recvs = _direct_copies(kind, cin, cout, *sems)
            for cp in recvs:
                cp.wait_recv()
            for cp in sends:
                cp.wait_send()
            for cp in local:
                cp.wait()

    res = pl.pallas_call(
        carrier, name=name, grid=grid, in_specs=in_specs + [ANY] * nc, out_specs=out_specs + [ANY] * nc,
        out_shape=out_shape + c_shape, scratch_shapes=scratch_shapes + c_sems,
        compiler_params=_cp(*["arbitrary"] * len(grid)))(*operands, *arrs)
    return list(res[:n_out]), list(res[n_out:])


def _mm_call(pairs, grid, a_spec, b_spec, o_spec, out_shape, acc_shape, nk, ca, cb, name,
             res=None, res_spec=None, comm=None):
    npairs = len(pairs)
    multi = isinstance(out_shape, (list, tuple))
    nout = len(out_shape) if multi else 1

    def body(*refs):
        ab = refs[:2 * npairs]
        r_ref = refs[2 * npairs] if res is not None else None
        o_refs, acc = refs[-1 - nout:-1], refs[-1]
        k = pl.program_id(2)

        @pl.when(k == 0)
        def _():
            acc[...] = jnp.zeros_like(acc)

        tot = None
        for p in range(npairs):
            d = _dot(ab[2 * p][...].astype(BF16), ab[2 * p + 1][...].astype(BF16), ca, cb)
            tot = d if tot is None else tot + d
        acc[...] += tot

        @pl.when(k == nk - 1)
        def _():
            r = acc[...]
            if r_ref is not None:
                r = r + r_ref[...]
            for o_ref in o_refs:
                o_ref[...] = r.astype(o_ref.dtype)

    ops, specs = [], []
    for a, b in pairs:
        ops += [a, b]
        specs += [a_spec, b_spec]
    if res is not None:
        ops.append(res)
        specs.append(res_spec)
    outs, moved = _pcall(
        body, name=name, grid=grid, in_specs=specs, out_specs=[o_spec] * nout,
        out_shape=out_shape if multi else [out_shape], operands=ops,
        scratch_shapes=[pltpu.VMEM(acc_shape, F32)], sem=("parallel", "parallel", "arbitrary"), comm=comm)
    outs = outs if multi else outs[0]
    return outs if comm is None else (outs, moved)


def _mm(a, b, name, ta=False, tb=False, out_dtype=F32, res=None, tm=1024, tn=1024, tk=512):
    M, K = (a.shape[1], a.shape[0]) if ta else a.shape
    N = b.shape[0] if tb else b.shape[1]
    tm, tn, tk = _tile(M, tm), _tile(N, tn), _tile(K, tk)
    a_spec = pl.BlockSpec((tk, tm), lambda i, j, k: (k, i)) if ta else pl.BlockSpec((tm, tk), lambda i, j, k: (i, k))
    b_spec = pl.BlockSpec((tn, tk), lambda i, j, k: (j, k)) if tb else pl.BlockSpec((tk, tn), lambda i, j, k: (k, j))
    o_spec = pl.BlockSpec((tm, tn), lambda i, j, k: (i, j))
    if isinstance(out_dtype, tuple):
        out_shape = [jax.ShapeDtypeStruct((M, N), d) for d in out_dtype]
    else:
        out_shape = jax.ShapeDtypeStruct((M, N), out_dtype)
    return _mm_call([(a, b)], (M // tm, N // tn, K // tk), a_spec, b_spec, o_spec,
                    out_shape, (tm, tn), K // tk,
                    0 if ta else 1, 1 if tb else 0, name, res=res, res_spec=o_spec)


def _mm_down(act, wd, res, name, tm=1024, tn=1024):
    _, S, FB = act.shape
    D = wd.shape[1]
    tm, tn = _tile(S, tm), _tile(D, tn)
    o_spec = pl.BlockSpec((tm, tn), lambda i, j, k: (i, j))
    return _mm_call([(act, wd)], (S // tm, D // tn, NDEV),
                    pl.BlockSpec((None, tm, FB), lambda i, j, k: (k, i, 0)),
                    pl.BlockSpec((FB, tn), lambda i, j, k: (k, j)), o_spec,
                    jax.ShapeDtypeStruct((S, D), F32), (tm, tn), NDEV, 1, 0, name, res=res, res_spec=o_spec)


def _mm_dwdown(act, dy, name, tn=1024, tk=512):
    _, S, FB = act.shape
    D = dy.shape[1]
    tn, tk = _tile(D, tn), _tile(S, tk)
    return _mm_call([(act, dy)], (NDEV, D // tn, S // tk),
                    pl.BlockSpec((None, tk, FB), lambda i, j, k: (i, k, 0)),
                    pl.BlockSpec((tk, tn), lambda i, j, k: (k, j)),
                    pl.BlockSpec((FB, tn), lambda i, j, k: (i, j)),
                    jax.ShapeDtypeStruct((NDEV * FB, D), BF16), (FB, tn), S // tk, 0, 0, name)


def _mm_dh2(dg, wg, du, wu, name, tm=1024, tn=1024, comm=None):
    _, S, FB = dg.shape
    D = wg.shape[1]
    tm, tn = _tile(S, tm), _tile(D, tn)
    return _mm_call([(dg, wg), (du, wu)], (S // tm, D // tn, NDEV),
                    pl.BlockSpec((None, tm, FB), lambda i, j, k: (k, i, 0)),
                    pl.BlockSpec((None, tn, FB), lambda i, j, k: (k, j, 0)),
                    pl.BlockSpec((tm, tn), lambda i, j, k: (i, j)),
                    jax.ShapeDtypeStruct((S, D), F32), (tm, tn), NDEV, 1, 1, name, comm=comm)


def _mm_dwgate(h2, dg, name, tm=1024, tk=512):
    _, S, FB = dg.shape
    D = h2.shape[1]
    tm, tk = _tile(D, tm), _tile(S, tk)
    return _mm_call([(h2, dg)], (NDEV, D // tm, S // tk),
                    pl.BlockSpec((tk, tm), lambda p, i, k: (k, i)),
                    pl.BlockSpec((None, tk, FB), lambda p, i, k: (p, k, 0)),
                    pl.BlockSpec((None, tm, FB), lambda p, i, k: (p, i, 0)),
                    jax.ShapeDtypeStruct((NDEV, D, FB), BF16), (tm, FB), S // tk, 0, 0, name)


def _ffn_up(h2, wg, wu, name, tm=512, comm=None):
    S, D = h2.shape
    FB = wg.shape[2]
    tm = _tile(S, tm)

    def body(h_ref, wg_ref, wu_ref, g_ref, u_ref, act_ref):
        h = h_ref[...]
        g = _dot_nn(h, wg_ref[...])
        u = _dot_nn(h, wu_ref[...])
        g_ref[...] = g
        u_ref[...] = u
        act_ref[...] = (g / (1.0 + jnp.exp(-g)) * u).astype(BF16)

    w_spec = pl.BlockSpec((None, D, FB), lambda p, i: (p, 0, 0))
    o_spec = pl.BlockSpec((None, tm, FB), lambda p, i: (p, i, 0))
    shp = (NDEV, S, FB)
    outs, moved = _pcall(
        body, name=name, grid=(NDEV, S // tm),
        in_specs=[pl.BlockSpec((tm, D), lambda p, i: (i, 0)), w_spec, w_spec],
        out_specs=[o_spec, o_spec, o_spec],
        out_shape=[jax.ShapeDtypeStruct(shp, F32), jax.ShapeDtypeStruct(shp, F32), jax.ShapeDtypeStruct(shp, BF16)],
        operands=[h2, wg, wu], sem=("parallel", "parallel"), comm=comm)
    return outs if comm is None else (outs, moved)


def _ffn_dact(dy, wd, g, u, name, tm=512):
    S, D = dy.shape
    FB = g.shape[2]
    tm = _tile(S, tm)

    def body(dy_ref, wd_ref, g_ref, u_ref, dg_ref, du_ref):
        dact = _dot_nt(dy_ref[...].astype(BF16), wd_ref[...])
        gv = g_ref[...]
        sg = 1.0 / (1.0 + jnp.exp(-gv))
        dg_ref[...] = (dact * u_ref[...] * (sg * (1.0 + gv * (1.0 - sg)))).astype(BF16)
        du_ref[...] = (dact * (gv * sg)).astype(BF16)

    t_spec = pl.BlockSpec((None, tm, FB), lambda p, i: (p, i, 0))
    shp = jax.ShapeDtypeStruct((NDEV, S, FB), BF16)
    return pl.pallas_call(
        body, name=name, grid=(NDEV, S // tm),
        in_specs=[pl.BlockSpec((tm, D), lambda p, i: (i, 0)), pl.BlockSpec((FB, D), lambda p, i: (p, 0)),
                  t_spec, t_spec],
        out_specs=[t_spec, t_spec], out_shape=[shp, shp],
        compiler_params=_cp("parallel", "parallel"),
    )(dy, wd, g, u)


def _rms_fwd(x, gain, width, cb, out_dtype, name, ts=512):
    S = x.shape[0]
    ts = _tile(S, ts)

    def body(x_ref, g_ref, o_ref):
        xv = x_ref[...]
        r = lax.rsqrt(jnp.mean(xv * xv, axis=1, keepdims=True) + EPS)
        o_ref[...] = (xv * r * g_ref[...]).astype(o_ref.dtype)

    return pl.pallas_call(
        body, name=name, grid=(S // ts,),
        in_specs=[pl.BlockSpec((ts, width), lambda i: (i, cb)), pl.BlockSpec((1, width), lambda i: (0, 0))],
        out_specs=pl.BlockSpec((ts, width), lambda i: (i, 0)),
        out_shape=jax.ShapeDtypeStruct((S, width), out_dtype),
        compiler_params=_cp("parallel"),
    )(x, gain)


def _rms_bwd(x, gain, dy, width, cb, name, res=None, ts=256):
    S = x.shape[0]
    ts = _tile(S, ts)
    has_res = res is not None

    def body(*refs):
        x_ref, g_ref, dy_ref = refs[:3]
        r_ref = refs[3] if has_res else None
        dx_ref, dg_ref = refs[-2], refs[-1]

        @pl.when(pl.program_id(0) == 0)
        def _():
            dg_ref[...] = jnp.zeros_like(dg_ref)

        xv = x_ref[...]
        r = lax.rsqrt(jnp.mean(xv * xv, axis=1, keepdims=True) + EPS)
        xh = xv * r
        dyv = dy_ref[...]
        dyg = dyv * g_ref[...]
        dx = r * (dyg - xh * jnp.mean(dyg * xh, axis=1, keepdims=True))
        if has_res:
            dx = dx + r_ref[...]
        dx_ref[...] = dx
        dg_ref[...] += jnp.sum(dyv * xh, axis=0, keepdims=True)

    row = pl.BlockSpec((ts, width), lambda i: (i, 0))
    vec = pl.BlockSpec((1, width), lambda i: (0, 0))
    ops = [x, gain, dy] + ([res] if has_res else [])
    specs = [pl.BlockSpec((ts, width), lambda i: (i, cb)), vec, row] + ([row] if has_res else [])
    return pl.pallas_call(
        body, name=name, grid=(S // ts,), in_specs=specs, out_specs=[row, vec],
        out_shape=[jax.ShapeDtypeStruct((S, width), F32), jax.ShapeDtypeStruct((1, width), F32)],
        compiler_params=_cp("arbitrary"),
    )(*ops)


def _gn_fwd(outs, gain, name, ts=512):
    S, GW = outs[0].shape
    ts = _tile(S, ts)

    def body(a_ref, b_ref, c_ref, d_ref, g_ref, o_ref):
        for g, r_ref in enumerate((a_ref, b_ref, c_ref, d_ref)):
            xv = r_ref[...]
            r = lax.rsqrt(jnp.mean(xv * xv, axis=1, keepdims=True) + EPS)
            o_ref[:, g * GW:(g + 1) * GW] = (xv * r * g_ref[:, g * GW:(g + 1) * GW]).astype(BF16)

    row = pl.BlockSpec((ts, GW), lambda i: (i, 0))
    return pl.pallas_call(
        body, name=name, grid=(S // ts,),
        in_specs=[row] * 4 + [pl.BlockSpec((1, 4 * GW), lambda i: (0, 0))],
        out_specs=pl.BlockSpec((ts, 4 * GW), lambda i: (i, 0)),
        out_shape=jax.ShapeDtypeStruct((S, 4 * GW), BF16),
        compiler_params=_cp("parallel"),
    )(*outs, gain)


def _gn_bwd(outs, gain, dmix, name, ts=256):
    S, GW = outs[0].shape
    ts = _tile(S, ts)

    def body(a_ref, b_ref, c_ref, d_ref, g_ref, dm_ref, da_ref, db_ref, dc_ref, dd_ref, dg_ref):
        @pl.when(pl.program_id(0) == 0)
        def _():
            dg_ref[...] = jnp.zeros_like(dg_ref)

        for g, (r_ref, o_ref) in enumerate(zip((a_ref, b_ref, c_ref, d_ref), (da_ref, db_ref, dc_ref, dd_ref))):
            sl = slice(g * GW, (g + 1) * GW)
            xv = r_ref[...]
            r = lax.rsqrt(jnp.mean(xv * xv, axis=1, keepdims=True) + EPS)
            xh = xv * r
            dyv = dm_ref[:, sl]
            dyg = dyv * g_ref[:, sl]
            o_ref[...] = (r * (dyg - xh * jnp.mean(dyg * xh, axis=1, keepdims=True))).astype(BF16)
            dg_ref[:, sl] += jnp.sum(dyv * xh, axis=0, keepdims=True)

    row = pl.BlockSpec((ts, GW), lambda i: (i, 0))
    vec = pl.BlockSpec((1, 4 * GW), lambda i: (0, 0))
    return pl.pallas_call(
        body, name=name, grid=(S // ts,),
        in_specs=[row] * 4 + [vec, pl.BlockSpec((ts, 4 * GW), lambda i: (i, 0))],
        out_specs=[row] * 4 + [vec],
        out_shape=[jax.ShapeDtypeStruct((S, GW), BF16)] * 4 + [jax.ShapeDtypeStruct((1, 4 * GW), F32)],
        compiler_params=_cp("arbitrary"),
    )(*outs, gain, dmix)


def _rope(x, cbs, cos, sin, half, out_dtype, name, ts=512):
    S = x.shape[0]
    cb0, nb, stride = cbs
    ts = _tile(S, ts)

    def body(x_ref, c_ref, s_ref, o_ref):
        xv = x_ref[...].astype(F32)
        if half:
            lane = lax.broadcasted_iota(jnp.int32, xv.shape, 1)
            partner = jnp.where(lane % 64 < 32, pltpu.roll(xv, LANE - 32, 1), pltpu.roll(xv, 32, 1))
        else:
            partner = pltpu.roll(xv, 64, 1)
        o_ref[...] = (xv * c_ref[...] + partner * s_ref[...]).astype(o_ref.dtype)

    tab = pl.BlockSpec((ts, LANE), lambda i, j: (i, 0))
    return pl.pallas_call(
        body, name=name, grid=(S // ts, nb),
        in_specs=[pl.BlockSpec((ts, LANE), lambda i, j: (i, cb0 + stride * j)), tab, tab],
        out_specs=pl.BlockSpec((ts, LANE), lambda i, j: (i, j)),
        out_shape=jax.ShapeDtypeStruct((S, nb * LANE), out_dtype),
        compiler_params=_cp("parallel", "parallel"),
    )(x, cos, sin)


def _final_loss(x, gain, target, name, ts=256):
    S, D = x.shape
    ts = _tile(S, ts)

    def body(x_ref, g_ref, t_ref, dy_ref, l_ref):
        @pl.when(pl.program_id(0) == 0)
        def _():
            l_ref[...] = jnp.zeros_like(l_ref)

        xv = x_ref[...]
        r = lax.rsqrt(jnp.mean(xv * xv, axis=1, keepdims=True) + EPS)
        err = xv * r * g_ref[...] - t_ref[...]
        dy_ref[...] = err * (1.0 / D)
        part = jnp.sum(jnp.mean(err * err, axis=1, keepdims=True), axis=0, keepdims=True)
        l_ref[...] += jnp.broadcast_to(0.5 * part, (1, LANE))

    row = pl.BlockSpec((ts, D), lambda i: (i, 0))
    return pl.pallas_call(
        body, name=name, grid=(S // ts,),
        in_specs=[row, pl.BlockSpec((1, D), lambda i: (0, 0)), row],
        out_specs=[row, pl.BlockSpec((1, LANE), lambda i: (0, 0))],
        out_shape=[jax.ShapeDtypeStruct((S, D), F32), jax.ShapeDtypeStruct((1, LANE), F32)],
        compiler_params=_cp("arbitrary"),
    )(x, gain, target)


def _colspec(rows, f):
    return pl.BlockSpec((rows, LANE), f)


def _soft_tiles(S):
    tq = _tile(S, TQ)
    tk = _tile(S, TKS)
    assert tk % tq == 0
    return tq, tk


def _key_row(crow_ref, j, tk):
    n = tk // TK
    return jnp.concatenate([crow_ref[j * n + c] for c in range(n)], axis=1)


def _attn_fwd(S, H, q1, q1cb, k1, k1cb, v, vcb, scale, name, q2=None, q2cb=None, k2=None, k2cb=None,
              tab=None, win=None, ccol=None, crow=None, comm=None):
    tq, tk = _soft_tiles(S)
    has2, hastab, hasc = q2 is not None, tab is not None, ccol is not None

    def body(*refs):
        it = iter(refs)
        q1r, k1r, vr = next(it), next(it), next(it)
        q2r, k2r = (next(it), next(it)) if has2 else (None, None)
        tabr = next(it) if hastab else None
        ccolr, crowr = (next(it), next(it)) if hasc else (None, None)
        o_ref, lse_ref = next(it), next(it)
        i = pl.program_id(1)
        q = q1r[...]
        qb2 = q2r[...] if has2 else None
        cq = ccolr[:, 0:1] if hasc else None
        qpos = i * tq + lax.broadcasted_iota(jnp.int32, (tq, tk), 0)
        kio = lax.broadcasted_iota(jnp.int32, (tq, tk), 1)
        j_diag = (i * tq) // tk
        j_lo = jnp.maximum((i * tq - win) // tk, 0) if win else 0

        def step(j, carry, masked):
            m, l, acc = carry
            off = pl.multiple_of(j * tk, tk)
            s = _dot_nt(q, k1r[pl.ds(off, tk), :])
            if has2:
                s = s + _dot_nt(qb2, k2r[pl.ds(off, tk), :])
            s = s * scale
            if hastab:
                s = s + tabr[i - j * (tk // tq)]
            else:
                if hasc:
                    s = s + (cq - _key_row(crowr, j, tk))
                if masked:
                    s = jnp.where(kio + j * tk <= qpos, s, NEG)
            mn = jnp.maximum(m, jnp.max(s, axis=1, keepdims=True))
            p = jnp.exp(s - mn)
            al = jnp.exp(m - mn)
            l = al * l + jnp.sum(p, axis=1, keepdims=True)
            acc = al * acc + _dot_nn(p.astype(BF16), vr[pl.ds(off, tk), :])
            return mn, l, acc

        carry = (jnp.full((tq, 1), NEG, F32), jnp.zeros((tq, 1), F32), jnp.zeros((tq, LANE), F32))
        if hastab:
            carry = lax.fori_loop(j_lo, j_diag + 1, functools.partial(step, masked=False), carry)
        else:
            carry = lax.fori_loop(j_lo, j_diag, functools.partial(step, masked=False), carry)
            carry = step(j_diag, carry, True)
        m, l, acc = carry
        o_ref[...] = acc / l
        lse_ref[...] = jnp.broadcast_to(m + jnp.log(l), (tq, LANE))

    ops = [q1, k1, v]
    specs = [_colspec(tq, lambda h, i: (i, q1cb(h))), _colspec(S, lambda h, i: (0, k1cb(h))),
             _colspec(S, lambda h, i: (0, vcb(h)))]
    if has2:
        ops += [q2, k2]
        specs += [_colspec(tq, lambda h, i: (i, q2cb(h))), _colspec(S, lambda h, i: (0, k2cb(h)))]
    if hastab:
        ops.append(tab)
        specs.append(pl.BlockSpec(tab.shape, lambda h, i: (0, 0, 0)))
    if hasc:
        ops += [ccol, crow]
        specs += [_colspec(tq, lambda h, i: (i, h)),
                  pl.BlockSpec((None, S // TK, 1, TK), lambda h, i: (h, 0, 0, 0))]
    o_spec = _colspec(tq, lambda h, i: (i, h))
    shp = jax.ShapeDtypeStruct((S, H * LANE), F32)
    outs, moved = _pcall(
        body, name=name, grid=(H, S // tq), in_specs=specs, out_specs=[o_spec, o_spec], out_shape=[shp, shp],
        operands=ops, sem=("parallel", "arbitrary"), comm=comm)
    return outs if comm is None else (outs, moved)


def _attn_bwd(S, H, q1, q1cb, k1, k1cb, v, vcb, o, do, lse, scale, name, q2=None, q2cb=None, k2=None, k2cb=None,
              tab=None, win=None, ccol=None, crow=None, comm=None):
    tq, tk = _soft_tiles(S)
    has2, hastab, hasc = q2 is not None, tab is not None, ccol is not None

    def body(*refs):
        it = iter(refs)
        q1r, k1r, vr, o_r, do_r, lse_r = (next(it) for _ in range(6))
        q2r, k2r = (next(it), next(it)) if has2 else (None, None)
        tabr = next(it) if hastab else None
        ccolr, crowr = (next(it), next(it)) if hasc else (None, None)
        dq1_r, dk1_r, dv_r = next(it), next(it), next(it)
        dq2_r, dk2_r = (next(it), next(it)) if has2 else (None, None)
        dcr_r = next(it) if hasc else None
        i = pl.program_id(1)

        @pl.when(i == 0)
        def _():
            dk1_r[...] = jnp.zeros_like(dk1_r)
            dv_r[...] = jnp.zeros_like(dv_r)
            if has2:
                dk2_r[...] = jnp.zeros_like(dk2_r)
            if hasc:
                dcr_r[...] = jnp.zeros_like(dcr_r)

        q = q1r[...]
        qb2 = q2r[...] if has2 else None
        dob = do_r[...]
        delta = jnp.sum(dob.astype(F32) * o_r[...], axis=1, keepdims=True)
        lse_c = lse_r[:, 0:1]
        cq = ccolr[:, 0:1] if hasc else None
        qpos = i * tq + lax.broadcasted_iota(jnp.int32, (tq, tk), 0)
        kio = lax.broadcasted_iota(jnp.int32, (tq, tk), 1)
        j_diag = (i * tq) // tk
        j_lo = jnp.maximum((i * tq - win) // tk, 0) if win else 0

        def probs(j, masked):
            off = pl.multiple_of(j * tk, tk)
            kb = k1r[pl.ds(off, tk), :]
            s = _dot_nt(q, kb)
            kb2 = None
            if has2:
                kb2 = k2r[pl.ds(off, tk), :]
                s = s + _dot_nt(qb2, kb2)
            s = s * scale
            if hastab:
                s = s + tabr[i - j * (tk // tq)]
            else:
                if hasc:
                    s = s + (cq - _key_row(crowr, j, tk))
                if masked:
                    s = jnp.where(kio + j * tk <= qpos, s, NEG)
            p = jnp.exp(s - lse_c)
            dp = _dot_nt(dob, vr[pl.ds(off, tk), :])
            return off, kb, kb2, p, dp

        def sweep(fn, carry):
            if hastab:
                return lax.fori_loop(j_lo, j_diag + 1, functools.partial(fn, masked=False), carry)
            carry = lax.fori_loop(j_lo, j_diag, functools.partial(fn, masked=False), carry)
            return fn(j_diag, carry, True)

        if hasc:
            def dstep(j, acc, masked):
                _, _, _, p, dp = probs(j, masked)
                return acc + jnp.sum(p * dp, axis=1, keepdims=True)

            delta = sweep(dstep, jnp.zeros((tq, 1), F32))

        def step(j, carry, masked):
            dq, dq2 = carry
            off, kb, kb2, p, dp = probs(j, masked)
            ds = p * (dp - delta)
            dsb = ds.astype(BF16)
            dq = dq + _dot_nn(dsb, kb)
            dk1_r[pl.ds(off, tk), :] += _dot_tn(dsb, q) * scale
            dv_r[pl.ds(off, tk), :] += _dot_tn(p.astype(BF16), dob)
            if has2:
                dq2 = dq2 + _dot_nn(dsb, kb2)
                dk2_r[pl.ds(off, tk), :] += _dot_tn(dsb, qb2) * scale
            if hasc:
                cs = -jnp.sum(ds, axis=0, keepdims=True)
                for c in range(tk // TK):
                    dcr_r[j * (tk // TK) + c] += cs[:, c * TK:(c + 1) * TK]
            return dq, dq2

        z = jnp.zeros((tq, LANE), F32)
        dq, dq2 = sweep(step, (z, z))
        dq1_r[...] = dq * scale
        if has2:
            dq2_r[...] = dq2 * scale

    qspec = _colspec(tq, lambda h, i: (i, h))
    kspec = _colspec(S, lambda h, i: (0, h))
    ops = [q1, k1, v, o, do, lse]
    specs = [_colspec(tq, lambda h, i: (i, q1cb(h))), _colspec(S, lambda h, i: (0, k1cb(h))),
             _colspec(S, lambda h, i: (0, vcb(h))), qspec, qspec, qspec]
    if has2:
        ops += [q2, k2]
        specs += [_colspec(tq, lambda h, i: (i, q2cb(h))), _colspec(S, lambda h, i: (0, k2cb(h)))]
    if hastab:
        ops.append(tab)
        specs.append(pl.BlockSpec(tab.shape, lambda h, i: (0, 0, 0)))
    if hasc:
        ops += [ccol, crow]
        specs += [qspec, pl.BlockSpec((None, S // TK, 1, TK), lambda h, i: (h, 0, 0, 0))]
    assert all(t.dtype == BF16 for t in ops[:3] + [do] + ([q2, k2] if has2 else []))
    out_specs = [qspec, kspec, kspec] + ([qspec, kspec] if has2 else [])
    shp = jax.ShapeDtypeStruct((S, H * LANE), F32)
    out_shape = [shp] * len(out_specs)
    if hasc:
        out_specs.append(pl.BlockSpec((None, S // TK, 1, TK), lambda h, i: (h, 0, 0, 0)))
        out_shape.append(jax.ShapeDtypeStruct((H, S // TK, 1, TK), F32))
    outs, moved = _pcall(
        body, name=name, grid=(H, S // tq), in_specs=specs, out_specs=out_specs, out_shape=out_shape,
        operands=ops, sem=("parallel", "arbitrary"), comm=comm)
    return outs if comm is None else (outs, moved)


def _scan_matrix(kind):
    j = np.arange(TK)[:, None]
    s = np.arange(TK)[None, :]
    tri = {"suffix_ex": j > s, "prefix_in": j <= s, "prefix_ex": j < s}[kind].astype(np.float32)
    half = np.concatenate([tri, np.ones((TK, TK), np.float32)], axis=1)
    return jnp.asarray(np.concatenate([half, half], axis=0), BF16)


def _scan_mxu(x, mat):
    hi = x.astype(BF16)
    lo = (x - hi.astype(F32)).astype(BF16)
    r = _dot_nn(jnp.concatenate([hi, lo], axis=1), mat)
    return r[:, :TK], r[:, TK:]


def _stick_logs(z):
    e = jnp.exp(-jnp.abs(z))
    return e, -jnp.maximum(z, 0.0) - jnp.log(1.0 + e)


def _stick_fwd(S, H, x, qcb, kcb, vcb, scale, name, comm=None):
    tq, tk = _tile(S, TQ), TK
    r = tq // tk
    assert x.dtype == BF16

    def body(q_r, k_r, v_r, mat_r, o_ref, t_ref):
        i = pl.program_id(1)
        q = q_r[...]
        qpos = i * tq + lax.broadcasted_iota(jnp.int32, (tq, tk), 0)
        lane = lax.broadcasted_iota(jnp.int32, (tq, tk), 1)

        def step(j, carry, masked):
            c, acc = carry
            off = pl.multiple_of(j * tk, tk)
            z = _dot_nt(q, k_r[pl.ds(off, tk), :]) * scale
            _, lk = _stick_logs(z)
            if masked:
                past = lane + j * tk < qpos
                lk = jnp.where(past, lk, 0.0)
            suf, tot = _scan_mxu(lk, mat_r[...])
            a = jnp.exp(z + lk + suf + c)
            if masked:
                a = jnp.where(past, a, 0.0)
            acc = acc + _dot_nn(a.astype(BF16), v_r[pl.ds(off, tk), :])
            return c + tot, acc

        carry = (jnp.zeros((tq, tk), F32), jnp.zeros((tq, LANE), F32))
        for d in reversed(range(r)):
            carry = step(i * r + d, carry, True)
        c, acc = lax.fori_loop(0, i * r, lambda jj, cr: step(i * r - 1 - jj, cr, False), carry)
        o_ref[...] = acc
        t_ref[...] = c

    o_spec = _colspec(tq, lambda h, i: (i, h))
    shp = jax.ShapeDtypeStruct((S, H * LANE), F32)
    mat = _scan_matrix("suffix_ex")
    outs, moved = _pcall(
        body, name=name, grid=(H, S // tq),
        in_specs=[_colspec(tq, lambda h, i: (i, qcb(h))), _colspec(S, lambda h, i: (0, kcb(h))),
                  _colspec(S, lambda h, i: (0, vcb(h))), pl.BlockSpec(mat.shape, lambda h, i: (0, 0))],
        out_specs=[o_spec, o_spec], out_shape=[shp, shp],
        operands=[x, x, x, mat], sem=("parallel", "arbitrary"), comm=comm)
    return outs if comm is None else (outs, moved)


def _stick_bwd(S, H, x, qcb, kcb, vcb, do, tot, scale, name, comm=None):
    tq, tk = _tile(S, TQ), TK
    r = tq // tk
    assert x.dtype == BF16 and do.dtype == BF16

    def body(q_r, k_r, v_r, do_r, t_r, pin_r, pex_r, dq_r, dk_r, dv_r):
        i = pl.program_id(1)

        @pl.when(i == 0)
        def _():
            dk_r[...] = jnp.zeros_like(dk_r)
            dv_r[...] = jnp.zeros_like(dv_r)

        q = q_r[...]
        dob = do_r[...]
        total = t_r[...]
        qpos = i * tq + lax.broadcasted_iota(jnp.int32, (tq, tk), 0)
        lane = lax.broadcasted_iota(jnp.int32, (tq, tk), 1)

        def step(j, carry, masked):
            cl, cg, dq = carry
            off = pl.multiple_of(j * tk, tk)
            kb = k_r[pl.ds(off, tk), :]
            z = _dot_nt(q, kb) * scale
            e, lk = _stick_logs(z)
            if masked:
                past = lane + j * tk < qpos
                lk = jnp.where(past, lk, 0.0)
            pre, tl = _scan_mxu(lk, pin_r[...])
            a = jnp.exp(z + lk + (total - cl - pre))
            if masked:
                a = jnp.where(past, a, 0.0)
            g = _dot_nt(dob, v_r[pl.ds(off, tk), :]) * a
            gpre, tg = _scan_mxu(g, pex_r[...])
            inv = 1.0 / (1.0 + e)
            small = e * inv
            pos = z >= 0
            dz = g * jnp.where(pos, small, inv) - jnp.where(pos, inv, small) * (cg + gpre)
            if masked:
                dz = jnp.where(past, dz, 0.0)
            dzb = dz.astype(BF16)
            dk_r[pl.ds(off, tk), :] += _dot_tn(dzb, q) * scale
            dv_r[pl.ds(off, tk), :] += _dot_tn(a.astype(BF16), dob)
            return cl + tl, cg + tg, dq + _dot_nn(dzb, kb)

        zt = jnp.zeros((tq, tk), F32)
        carry = lax.fori_loop(0, i * r, functools.partial(step, masked=False), (zt, zt, jnp.zeros((tq, LANE), F32)))
        for d in range(r):
            carry = step(i * r + d, carry, True)
        dq_r[...] = carry[2] * scale

    qspec = _colspec(tq, lambda h, i: (i, h))
    kspec = _colspec(S, lambda h, i: (0, h))
    shp = jax.ShapeDtypeStruct((S, H * LANE), F32)
    pin, pex = _scan_matrix("prefix_in"), _scan_matrix("prefix_ex")
    mspec = pl.BlockSpec(pin.shape, lambda h, i: (0, 0))
    outs, moved = _pcall(
        body, name=name, grid=(H, S // tq),
        in_specs=[_colspec(tq, lambda h, i: (i, qcb(h))), _colspec(S, lambda h, i: (0, kcb(h))),
                  _colspec(S, lambda h, i: (0, vcb(h))), qspec, qspec, mspec, mspec],
        out_specs=[qspec, kspec, kspec], out_shape=[shp] * 3,
        operands=[x, x, x, do, tot, pin, pex], sem=("parallel", "arbitrary"), comm=comm)
    return outs if comm is None else (outs, moved)


def _scan8(x, rows, reverse):
    for sh in (1, 2, 4):
        if reverse:
            x = x + jnp.where(rows + sh < 8, pltpu.roll(x, 8 - sh, 0), 0.0)
        else:
            x = x + jnp.where(rows >= sh, pltpu.roll(x, sh, 0), 0.0)
    return x


def _fox_prep(S, H, proj, fcb, bias, name):
    tk = TK

    def body(f_ref, b_ref, ccol_ref, crow_ref, scr):
        rows = lax.broadcasted_iota(jnp.int32, (8, LANE), 0)

        def step(t, carry):
            off = pl.multiple_of(t * 8, 8)
            xb = f_ref[pl.ds(off, 8), :] + b_ref[...]
            lf = jnp.minimum(xb, 0.0) - jnp.log(1.0 + jnp.exp(-jnp.abs(xb)))
            lf = _scan8(lf, rows, False) + carry
            scr[pl.ds(off, 8), :] = lf
            return lf[7:8, :]

        lax.fori_loop(0, S // 8, step, jnp.zeros((1, LANE), F32))
        for h in range(H):
            ccol_ref[:, h * LANE:(h + 1) * LANE] = jnp.broadcast_to(scr[:, h:h + 1], (S, LANE))

            def tr(t, _):
                off = pl.multiple_of(t * tk, tk)
                blk = ccol_ref[pl.ds(off, tk), h * LANE:(h + 1) * LANE]
                crow_ref[h, t] = blk.T[0:1, :]
                return 0

            lax.fori_loop(0, S // tk, tr, 0)

    return pl.pallas_call(
        body, name=name, grid=(1,),
        in_specs=[_colspec(S, lambda i: (0, fcb)), pl.BlockSpec((1, LANE), lambda i: (0, 0))],
        out_specs=[pl.BlockSpec((S, H * LANE), lambda i: (0, 0)),
                   pl.BlockSpec((H, S // tk, 1, tk), lambda i: (0, 0, 0, 0))],
        out_shape=[jax.ShapeDtypeStruct((S, H * LANE), F32), jax.ShapeDtypeStruct((H, S // tk, 1, tk), F32)],
        scratch_shapes=[pltpu.VMEM((S, LANE), F32)],
        compiler_params=_cp("arbitrary"),
    )(proj, bias)


def _fox_bwd(S, H, proj, fcb, bias, dcr, name):
    tk = TK

    def body(f_ref, b_ref, dcr_ref, df_ref, db_ref, scr):
        rows = lax.broadcasted_iota(jnp.int32, (8, LANE), 0)
        lane_t = lax.broadcasted_iota(jnp.int32, (tk, LANE), 1)
        nb = S // 8

        def tr(t, _):
            off = pl.multiple_of(t * tk, tk)
            d = jnp.zeros((tk, LANE), F32)
            for h in range(H):
                d = d + jnp.where(lane_t == h, jnp.broadcast_to(dcr_ref[h, t], (LANE, tk)).T, 0.0)
            scr[pl.ds(off, tk), :] = d
            return 0

        lax.fori_loop(0, S // tk, tr, 0)

        def step(tt, carry):
            suffix, db = carry
            off = pl.multiple_of((nb - 1 - tt) * 8, 8)
            d = _scan8(scr[pl.ds(off, 8), :], rows, True) + suffix
            xb = f_ref[pl.ds(off, 8), :] + b_ref[...]
            e = jnp.exp(-jnp.abs(xb))
            dx = d * jnp.where(xb >= 0, e, 1.0) / (1.0 + e)
            df_ref[pl.ds(off, 8), :] = dx
            return d[0:1, :], db + jnp.sum(dx, axis=0, keepdims=True)

        z = jnp.zeros((1, LANE), F32)
        _, db = lax.fori_loop(0, nb, step, (z, z))
        db_ref[...] = db

    return pl.pallas_call(
        body, name=name, grid=(1,),
        in_specs=[_colspec(S, lambda i: (0, fcb)), pl.BlockSpec((1, LANE), lambda i: (0, 0)),
                  pl.BlockSpec((H, S // tk, 1, tk), lambda i: (0, 0, 0, 0))],
        out_specs=[pl.BlockSpec((S, LANE), lambda i: (0, 0)), pl.BlockSpec((1, LANE), lambda i: (0, 0))],
        out_shape=[jax.ShapeDtypeStruct((S, LANE), F32), jax.ShapeDtypeStruct((1, LANE), F32)],
        scratch_shapes=[pltpu.VMEM((S, LANE), F32)],
        compiler_params=_cp("arbitrary"),
    )(proj, bias, dcr)


def _adamw(w, slots, m, v, name, block_bytes=1 << 20):
    R, C = w.shape
    tr = R
    while tr * C * 4 > block_bytes and tr % 16 == 0:
        tr //= 2
    c1 = 1.0 - ADAM_B1 ** ADAM_STEP
    c2 = 1.0 - ADAM_B2 ** ADAM_STEP

    def body(w_ref, s_ref, m_ref, v_ref, g_ref, d_ref, nm_ref, nv_ref):
        g = s_ref[0].astype(F32)
        for s in range(1, NDEV):
            g = g + s_ref[s].astype(F32)
        mn = ADAM_B1 * m_ref[...] + (1.0 - ADAM_B1) * g
        vn = ADAM_B2 * v_ref[...] + (1.0 - ADAM_B2) * (g * g)
        g_ref[...] = g
        nm_ref[...] = mn
        nv_ref[...] = vn
        d_ref[...] = -ADAM_LR * ((mn / c1) / (jnp.sqrt(vn / c2) + ADAM_EPS) + ADAM_WD * w_ref[...])

    row = pl.BlockSpec((tr, C), lambda i: (i, 0))
    return pl.pallas_call(
        body, name=name, grid=(R // tr,),
        in_specs=[row, pl.BlockSpec((NDEV, tr, C), lambda i: (0, i, 0)), row, row],
        out_specs=[row] * 4, out_shape=[jax.ShapeDtypeStruct((R, C), F32)] * 4,
        compiler_params=_cp("parallel"),
    )(w, slots, m, v)


class _Layout:
    def __init__(self, D):
        self.GW = GW = D // 4
        self.H = H = GW // HEAD
        self.QL, self.KVL = 0, Q_LORA
        base = Q_LORA + KV_LORA
        (self.QB, self.KB, self.VB, self.QC, self.KC, self.VC, self.QD, self.KD, self.VD) = (
            base + k * GW for k in range(9))
        self.KR = base + 9 * GW
        self.FC = self.KR + LANE
        self.PW = -(-(self.FC + LANE) // 512) * 512
        self.o_kr = base
        self.o_bc = base + QK_ROPE
        self.o_fc = self.o_bc + 6 * GW
        self.o_d = self.o_fc + H
        self.IN = self.o_d + 3 * GW

    def pad(self, w):
        z = lambda n: jnp.zeros(w.shape[:-1] + (n,), w.dtype)
        return jnp.concatenate([
            w[..., :self.o_kr], w[..., self.o_bc:self.o_fc], w[..., self.o_d:self.IN],
            w[..., self.o_kr:self.o_bc], z(LANE - QK_ROPE), w[..., self.o_fc:self.o_d], z(LANE - self.H),
            z(self.PW - self.FC - LANE)], axis=-1)

    def unpad(self, g):
        return jnp.concatenate([
            g[..., :self.KR - 9 * self.GW], g[..., self.KR:self.KR + QK_ROPE], g[..., self.QB:self.QD],
            g[..., self.FC:self.FC + self.H], g[..., self.QD:self.KR]], axis=-1)


def _rope_tables(S):
    pos = jnp.arange(S, dtype=F32)

    def cs(dim):
        inv = ROPE_THETA ** (-jnp.arange(0, dim, 2, dtype=F32) / dim)
        ang = pos[:, None] * inv[None, :]
        return jnp.cos(ang), jnp.sin(ang)

    c, s = cs(HEAD)
    full = (jnp.concatenate([c, c], 1), jnp.concatenate([-s, s], 1))
    c, s = cs(QK_ROPE)
    z = jnp.zeros((S, LANE - QK_ROPE), F32)
    half = (jnp.concatenate([c, c, z], 1), jnp.concatenate([-s, s, z], 1))
    return full, half


def _dilated_table(tq, tk):
    win = max(w for w, _ in DILATED_PAIRS)
    nd = (win + tk) // tq + 1
    d = np.arange(nd)[:, None, None] * tq + np.arange(tq)[None, :, None] - np.arange(tk)[None, None, :]
    mult = np.zeros(d.shape, np.float64)
    for w, dil in DILATED_PAIRS:
        mult += (d >= 0) & (d <= w) & (d % dil == 0)
    return jnp.asarray(np.where(mult > 0, np.log(np.maximum(mult, 1.0)), NEG), F32), win


def _pack(arrs):
    rows = []
    for a in arrs:
        f = a.reshape(-1).astype(F32)
        f = jnp.pad(f, (0, (-f.shape[0]) % LANE))
        rows.append(f.reshape(-1, LANE))
    p = jnp.concatenate(rows, 0)
    return jnp.pad(p, ((0, (-p.shape[0]) % 8), (0, 0)))


def _unpack(p, shapes):
    out, r = [], 0
    for shp in shapes:
        n = int(np.prod(shp))
        nr = -(-n // LANE)
        out.append(p[r:r + nr].reshape(-1)[:n].reshape(shp))
        r += nr
    return out


def kernel(x, attn_norm, w_in, mla_q_norm, w_uq, mla_kv_norm, w_ukv, fox_forget_bias, group_norm, w_out, ffn_norm, w_gate, w_up, w_down, final_norm, loss_target, m_attn_norm, m_w_in, m_mla_q_norm, m_w_uq, m_mla_kv_norm, m_w_ukv, m_fox_forget_bias, m_group_norm, m_w_out, m_ffn_norm, m_w_gate, m_w_up, m_w_down, m_final_norm, v_attn_norm, v_w_in, v_mla_q_norm, v_w_uq, v_mla_kv_norm, v_w_ukv, v_fox_forget_bias, v_group_norm, v_w_out, v_ffn_norm, v_w_gate, v_w_up, v_w_down, v_final_norm):
    _, S, D = x.shape
    L = attn_norm.shape[0]
    lay = _Layout(D)
    H, GW, PW = lay.H, lay.GW, lay.PW
    FB = w_gate.shape[2]
    QKA = HEAD + QK_ROPE
    x = x[0]
    target = loss_target[0]
    rope_full, rope_half = _rope_tables(S)
    neg = lambda t: (t[0], -t[1])
    tab, win = _dilated_table(*_soft_tiles(S))
    cb = lambda col: col // LANE

    sh = dict(w_in=lay.pad(w_in).astype(BF16),
              **{n: w.astype(BF16) for n, w in (("w_uq", w_uq), ("w_ukv", w_ukv), ("w_out", w_out),
                                                  ("w_gate", w_gate), ("w_up", w_up), ("w_down", w_down))})
    first3 = ["w_in", "w_uq", "w_ukv"]

    def first_weights(g):
        wuq = jnp.transpose(g[1], (1, 0, 2)).reshape(Q_LORA, H, QKA)
        wuq = jnp.pad(wuq, ((0, 0), (0, 0), (0, 2 * LANE - QKA))).reshape(Q_LORA, H * 2 * LANE)
        return dict(win=g[0].reshape(D, PW), wuq=wuq,
                    wukv=jnp.transpose(g[2], (1, 0, 2)).reshape(KV_LORA, H * 2 * LANE))

    def row(a):
        return a.reshape(1, -1)

    def forward(l, x0, W):
        A = dict(x0=x0)
        A["bias"] = jnp.pad(row(fox_forget_bias[l]), ((0, 0), (0, LANE - H)))
        h1 = A["h1"] = _rms_fwd(x0, row(attn_norm[l]), D, 0, BF16, "attn_norm")
        proj, pb = A["proj"], A["pb"] = _mm(h1, W["win"], "in_proj", out_dtype=(F32, BF16))
        qln = A["qln"] = _rms_fwd(proj, row(mla_q_norm[l]), Q_LORA, cb(lay.QL) // 4, BF16, "q_norm")
        kvln = A["kvln"] = _rms_fwd(proj, row(mla_kv_norm[l]), KV_LORA, cb(lay.KVL) // 4, BF16, "kv_norm")
        qa, qab = _mm(qln, W["wuq"], "q_up", out_dtype=(F32, BF16))
        A["qab"] = qab
        kv = A["kv"] = _mm(kvln, W["wukv"], "kv_up", out_dtype=BF16)
        q_pe = A["q_pe"] = _rope(qa, (1, H, 2), *rope_half, True, BF16, "rope_q_mla")
        k_pe = A["k_pe"] = _rope(proj, (cb(lay.KR), 1, 1), *rope_half, True, BF16, "rope_k_mla")
        (A["o_a"], A["lse_a"]), (g_down,) = _attn_fwd(
            S, H, qab, lambda h: 2 * h, kv, lambda h: 2 * h, kv, lambda h: 2 * h + 1, QKA ** -0.5, "mla_fwd",
            q2=q_pe, q2cb=lambda h: h, k2=k_pe, k2cb=lambda h: 0, comm=("gather", [sh["w_down"][l]]))
        W["wd"] = g_down.reshape(NDEV * FB, D)
        qk_b = A["qk_b"] = _rope(proj, (cb(lay.QB), 2 * H, 1), *rope_full, False, BF16, "rope_qk_dil")
        A["o_b"], A["lse_b"] = _attn_fwd(
            S, H, qk_b, lambda h: h, qk_b, lambda h: H + h, pb, lambda h: cb(lay.VB) + h, HEAD ** -0.5,
            "dilated_fwd", tab=tab, win=win)
        ccol, crow = A["ccol"], A["crow"] = _fox_prep(S, H, proj, cb(lay.FC), A["bias"], "fox_prep")
        (A["o_c"], A["lse_c"]), (g_out,) = _attn_fwd(
            S, H, pb, lambda h: cb(lay.QC) + h, pb, lambda h: cb(lay.KC) + h, pb, lambda h: cb(lay.VC) + h,
            HEAD ** -0.5, "fox_fwd", ccol=ccol, crow=crow, comm=("gather", [sh["w_out"][l]]))
        W["wout"] = g_out.reshape(4 * GW, D)
        (A["o_d"], A["tot_d"]), (W["wg"], W["wu"]) = _stick_fwd(
            S, H, pb, lambda h: cb(lay.QD) + h, lambda h: cb(lay.KD) + h, lambda h: cb(lay.VD) + h,
            HEAD ** -0.5, "stick_fwd", comm=("gather", [sh["w_gate"][l], sh["w_up"][l]]))
        mix = A["mix"] = _gn_fwd([A["o_a"], A["o_b"], A["o_c"], A["o_d"]], row(group_norm[l]), "group_norm")
        x1 = A["x1"] = _mm(mix, W["wout"], "out_proj", res=x0)
        h2 = A["h2"] = _rms_fwd(x1, row(ffn_norm[l]), D, 0, BF16, "ffn_norm")
        nxt = None
        if l + 1 < L:
            (A["g"], A["u"], A["act"]), nxt = _ffn_up(
                h2, W["wg"], W["wu"], "ffn_up", comm=("gather", [sh[n][l + 1] for n in first3]))
        else:
            A["g"], A["u"], A["act"] = _ffn_up(h2, W["wg"], W["wu"], "ffn_up_last")
        return _mm_down(A["act"], W["wd"], x1, "ffn_down"), A, nxt

    def backward(l, dx2, W, A, late):
        proj, pb = A["proj"], A["pb"]
        G, small, got = {}, {}, {}
        dgate, dup = _ffn_dact(dx2, W["wd"], A["g"], A["u"], "ffn_dact")
        G["w_down"] = _mm_dwdown(A["act"], dx2, "dw_down").reshape(NDEV, FB, D)
        if late is None:
            dh2 = _mm_dh2(dgate, W["wg"], dup, W["wu"], "ffn_dh_first")
        else:
            dh2, moved = _mm_dh2(dgate, W["wg"], dup, W["wu"], "ffn_dh", comm=("exchange", late))
            got.update({(l + 1, n): s for n, s in zip(first3, moved)})
        G["w_gate"] = _mm_dwgate(A["h2"], dgate, "dw_gate")
        G["w_up"] = _mm_dwgate(A["h2"], dup, "dw_up")
        dx1, small["ffn_norm"] = _rms_bwd(A["x1"], row(ffn_norm[l]), dh2, D, 0, "ffn_norm_bwd", res=dx2)
        dmix = _mm(dx1, W["wout"], "out_proj_dx", tb=True)
        G["w_out"] = _mm(A["mix"], dx1, "dw_out", ta=True, out_dtype=BF16).reshape(NDEV, 4 * GW // NDEV, D)
        do_a, do_b, do_c, do_d, small["group_norm"] = _gn_bwd(
            [A["o_a"], A["o_b"], A["o_c"], A["o_d"]], row(group_norm[l]), dmix, "group_norm_bwd")
        (dq_d, dk_d, dv_d), (got[l, "w_gate"], got[l, "w_up"]) = _stick_bwd(
            S, H, pb, lambda h: cb(lay.QD) + h, lambda h: cb(lay.KD) + h, lambda h: cb(lay.VD) + h,
            do_d, A["tot_d"], HEAD ** -0.5, "stick_bwd", comm=("exchange", [G["w_gate"], G["w_up"]]))
        (dq_c, dk_c, dv_c, dcc), (got[l, "w_down"],) = _attn_bwd(
            S, H, pb, lambda h: cb(lay.QC) + h, pb, lambda h: cb(lay.KC) + h, pb, lambda h: cb(lay.VC) + h,
            A["o_c"], do_c, A["lse_c"], HEAD ** -0.5, "fox_bwd", ccol=A["ccol"], crow=A["crow"],
            comm=("exchange", [G["w_down"]]))
        dfc, dbias = _fox_bwd(S, H, proj, cb(lay.FC), A["bias"], dcc, "fox_gate_bwd")
        small["fox_forget_bias"] = dbias[0, :H]
        qk_b = A["qk_b"]
        dq_b, dk_b, dv_b = _attn_bwd(
            S, H, qk_b, lambda h: h, qk_b, lambda h: H + h, pb, lambda h: cb(lay.VB) + h,
            A["o_b"], do_b, A["lse_b"], HEAD ** -0.5, "dilated_bwd", tab=tab, win=win)
        dqk_b = _rope(jnp.concatenate([dq_b, dk_b], 1), (0, 2 * H, 1), *neg(rope_full), False, BF16, "rope_qk_dil_bwd")
        qab, kv = A["qab"], A["kv"]
        (dq1, dk1, dv_a, dq2, dk2), (got[l, "w_out"],) = _attn_bwd(
            S, H, qab, lambda h: 2 * h, kv, lambda h: 2 * h, kv, lambda h: 2 * h + 1,
            A["o_a"], do_a, A["lse_a"], QKA ** -0.5, "mla_bwd",
            q2=A["q_pe"], q2cb=lambda h: h, k2=A["k_pe"], k2cb=lambda h: 0, comm=("exchange", [G["w_out"]]))
        dq2 = _rope(dq2, (0, H, 1), *neg(rope_half), True, F32, "rope_q_mla_bwd")
        dk_pe = _rope(dk2.reshape(S, H, LANE).sum(1), (0, 1, 1), *neg(rope_half), True, BF16, "rope_k_mla_bwd")
        dqa = jnp.stack([dq1.reshape(S, H, LANE), dq2.reshape(S, H, LANE)], 2).reshape(S, H * 2 * LANE).astype(BF16)
        dkv = jnp.stack([dk1.reshape(S, H, LANE), dv_a.reshape(S, H, LANE)], 2).reshape(S, H * 2 * LANE).astype(BF16)
        dwuq = _mm(A["qln"], dqa, "dw_uq", ta=True, out_dtype=BF16)
        dwuq = dwuq.reshape(Q_LORA, H, 2 * LANE)[:, :, :QKA].reshape(Q_LORA, NDEV, H * QKA // NDEV)
        G["w_uq"] = jnp.transpose(dwuq, (1, 0, 2))
        dwukv = _mm(A["kvln"], dkv, "dw_ukv", ta=True, out_dtype=BF16).reshape(KV_LORA, NDEV, H * 2 * LANE // NDEV)
        G["w_ukv"] = jnp.transpose(dwukv, (1, 0, 2))
        dqln = _mm(dqa, W["wuq"], "q_up_dx", tb=True)
        dkvln = _mm(dkv, W["wukv"], "kv_up_dx", tb=True)
        dql, small["mla_q_norm"] = _rms_bwd(proj, row(mla_q_norm[l]), dqln, Q_LORA, cb(lay.QL) // 4, "q_norm_bwd")
        dkvl, small["mla_kv_norm"] = _rms_bwd(proj, row(mla_kv_norm[l]), dkvln, KV_LORA, cb(lay.KVL) // 4, "kv_norm_bwd")
        bf = lambda t: t.astype(BF16)
        dproj = jnp.concatenate([
            bf(dql), bf(dkvl), dqk_b, bf(dv_b), bf(dq_c), bf(dk_c), bf(dv_c), bf(dq_d), bf(dk_d), bf(dv_d),
            dk_pe, bf(dfc), jnp.zeros((S, PW - lay.FC - LANE), BF16)], axis=1)
        G["w_in"] = _mm(A["h1"], dproj, "dw_in", ta=True, out_dtype=BF16).reshape(NDEV, D // NDEV, PW)
        dh1 = _mm(dproj, W["win"], "in_proj_dx", tb=True)
        dx0, small["attn_norm"] = _rms_bwd(A["x0"], row(attn_norm[l]), dh1, D, 0, "attn_norm_bwd", res=dx1)
        return dx0, [G[n] for n in first3], got, small

    big = first3 + ["w_out", "w_gate", "w_up", "w_down"]
    Ws, As = [], []
    xc = x
    nxt = _all_gather([sh[n][0] for n in first3], name="gather_first")
    for l in range(L):
        W = first_weights(nxt)
        xc, A, nxt = forward(l, xc, W)
        Ws.append(W)
        As.append(A)
    dx, loss_part = _final_loss(xc, row(final_norm), target, "final_loss")
    dx, dfinal = _rms_bwd(xc, row(final_norm), dx, D, 0, "final_norm_bwd")
    slots = {}
    smalls = [None] * L
    late = None
    for l in reversed(range(L)):
        dx, late, got, smalls[l] = backward(l, dx, Ws[l], As[l], late)
        slots.update(got)
    slots.update({(0, n): s for n, s in zip(first3, _all_to_all(late, name="exchange_last"))})

    names_small = ["attn_norm", "mla_q_norm", "mla_kv_norm", "fox_forget_bias", "group_norm", "ffn_norm"]
    params = dict(attn_norm=attn_norm, mla_q_norm=mla_q_norm, mla_kv_norm=mla_kv_norm, fox_forget_bias=fox_forget_bias,
                  group_norm=group_norm, ffn_norm=ffn_norm, final_norm=final_norm, w_in=w_in, w_uq=w_uq, w_ukv=w_ukv,
                  w_out=w_out, w_gate=w_gate, w_up=w_up, w_down=w_down)
    moms = dict(attn_norm=(m_attn_norm, v_attn_norm), mla_q_norm=(m_mla_q_norm, v_mla_q_norm),
                mla_kv_norm=(m_mla_kv_norm, v_mla_kv_norm), fox_forget_bias=(m_fox_forget_bias, v_fox_forget_bias),
                group_norm=(m_group_norm, v_group_norm), ffn_norm=(m_ffn_norm, v_ffn_norm),
                final_norm=(m_final_norm, v_final_norm), w_in=(m_w_in, v_w_in), w_uq=(m_w_uq, v_w_uq),
                w_ukv=(m_w_ukv, v_w_ukv), w_out=(m_w_out, v_w_out), w_gate=(m_w_gate, v_w_gate),
                w_up=(m_w_up, v_w_up), w_down=(m_w_down, v_w_down))
    small_list = names_small + ["final_norm"]
    small_grads = [jnp.stack([smalls[l][n].reshape(params[n].shape[1:]) for l in range(L)]) for n in names_small]
    small_grads.append(dfinal.reshape(final_norm.shape))
    shapes = [params[n].shape for n in small_list] + [(LANE,)]
    packed_g = _all_gather([_pack(small_grads + [loss_part.reshape(LANE)])], name="gather_small")[0]
    zero = jnp.zeros((LANE,), F32)
    res_small = _adamw(_pack([params[n] for n in small_list] + [zero]), packed_g,
                       _pack([moms[n][0] for n in small_list] + [zero]),
                       _pack([moms[n][1] for n in small_list] + [zero]), "adamw_small")
    unp = [_unpack(r, shapes) for r in res_small]
    out = {n: tuple(unp[k][i] for k in range(4)) for i, n in enumerate(small_list)}
    loss = unp[0][-1][0]

    for i, n in enumerate(big):
        st = jnp.stack([slots[l, n] for l in range(L)], axis=1)
        if n == "w_in":
            st = lay.unpad(st)
        C = st.shape[-1]
        st = st.reshape(NDEV, -1, C)
        w2 = params[n].reshape(-1, C)
        res = _adamw(w2, st, moms[n][0].reshape(-1, C), moms[n][1].reshape(-1, C), "adamw_" + n)
        out[n] = tuple(r.reshape(params[n].shape) for r in res)

    order = ["attn_norm", "w_in", "mla_q_norm", "w_uq", "mla_kv_norm", "w_ukv", "fox_forget_bias", "group_norm",
             "w_out", "ffn_norm", "w_gate", "w_up", "w_down", "final_norm"]
    return (loss, dx[None], *[out[n][0] for n in order], *[out[n][1] for n in order],
            *[out[n][2] for n in order], *[out[n][3] for n in order])
```

```python
import functools
import math

import numpy as np
import jax
import jax.numpy as jnp
from jax import lax
from jax.experimental import pallas as pl
from jax.experimental.pallas import tpu as pltpu

F32 = jnp.float32
BF16 = jnp.bfloat16
NDEV = 8
LANE = 128
HEAD = 128
Q_LORA = 512
KV_LORA = 512
QK_ROPE = 64
DILATED_PAIRS = ((128, 1), (512, 4), (2048, 16))
ROPE_THETA = 10000.0
EPS = 1e-6
NEG = -1e30
TQ = 256
TK = 128
TKS = 512
VMEM_LIMIT = 48 * 1024 * 1024
ADAM_LR, ADAM_B1, ADAM_B2, ADAM_EPS, ADAM_WD, ADAM_STEP = 0.001, 0.9, 0.999, 1e-08, 0.01, 10
MESH = pl.DeviceIdType.MESH
ANY = pl.BlockSpec(memory_space=pl.ANY)


def _cp(*sem):
    return pltpu.CompilerParams(dimension_semantics=sem, vmem_limit_bytes=VMEM_LIMIT)


def _dot(a, b, ca, cb):
    return lax.dot_general(a, b, (((ca,), (cb,)), ((), ())), preferred_element_type=F32)


def _dot_nn(a, b):
    return _dot(a, b, 1, 0)


def _dot_nt(a, b):
    return _dot(a, b, 1, 1)


def _dot_tn(a, b):
    return _dot(a, b, 0, 0)


def _tile(n, t):
    if n <= t:
        return n
    t -= t % LANE
    while n % t:
        t -= LANE
    return t


def _all_gather(shards, name):
    n = len(shards)

    def body(*refs):
        ins, outs = refs[:n], refs[n:2 * n]
        send_sems, recv_sems, local_sems = refs[2 * n:]
        x, y, c = lax.axis_index("x"), lax.axis_index("y"), lax.axis_index("c")
        me, sibling = (x, y, c), (x, y, 1 - c)
        chips = [(1 - x, y), (x, 1 - y), (1 - x, 1 - y)]

        def copy(a, k, block, to, src=None):
            px, py, pc = block
            rows = outs[a].at[4 * px + 2 * py + pc]
            return pltpu.make_async_remote_copy(
                src_ref=rows if src is None else src, dst_ref=rows,
                send_sem=send_sems.at[a, k], recv_sem=recv_sems.at[a, k],
                device_id=to, device_id_type=MESH)

        started = []
        for a in range(n):
            mine = pltpu.make_async_copy(ins[a], outs[a].at[4 * x + 2 * y + c], local_sems.at[a])
            mine.start()
            started.append(mine)
        sends = []
        for a in range(n):
            first = [copy(a, 0, me, sibling, src=ins[a])]
            first += [copy(a, 1 + j, me, (*chip, c), src=ins[a]) for j, chip in enumerate(chips)]
            for cp in first:
                cp.start()
            sends += first
        for j, chip in enumerate(chips):
            for a in range(n):
                copy(a, 1 + j, (*chip, c), me).wait_recv()
                passed = copy(a, 4 + j, (*chip, c), sibling)
                passed.start()
                sends.append(passed)
        for a in range(n):
            copy(a, 0, sibling, me).wait_recv()
            for j, chip in enumerate(chips):
                copy(a, 4 + j, (*chip, 1 - c), me).wait_recv()
        for cp in sends:
            cp.wait_send()
        for mine in started:
            mine.wait()

    return pl.pallas_call(
        body, name=name,
        out_shape=[jax.ShapeDtypeStruct((NDEV,) + s.shape, s.dtype) for s in shards],
        in_specs=[ANY] * n, out_specs=[ANY] * n,
        scratch_shapes=[pltpu.SemaphoreType.DMA((n, 7)), pltpu.SemaphoreType.DMA((n, 7)),
                        pltpu.SemaphoreType.DMA((n,))],
    )(*shards)


def _direct_copies(kind, ins, outs, send_sems, recv_sems, local_sems, want_recvs=True):
    x, y, c = lax.axis_index("x"), lax.axis_index("y"), lax.axis_index("c")
    my_id = 4 * x + 2 * y + c
    local, sends, recvs = [], [], []
    for a in range(len(ins)):
        mine = ins[a] if kind == "gather" else ins[a].at[my_id]
        local.append(pltpu.make_async_copy(mine, outs[a].at[my_id], local_sems.at[a]))
        for k in range(1, NDEV):
            peer = (1 - x if k & 4 else x, 1 - y if k & 2 else y, 1 - c if k & 1 else c)
            pid = 4 * peer[0] + 2 * peer[1] + peer[2]
            sems = dict(send_sem=send_sems.at[a, k - 1], recv_sem=recv_sems.at[a, k - 1],
                        device_id=peer, device_id_type=MESH)
            sends.append(pltpu.make_async_remote_copy(
                src_ref=ins[a] if kind == "gather" else ins[a].at[pid], dst_ref=outs[a].at[my_id], **sems))
            if want_recvs:
                recvs.append(pltpu.make_async_remote_copy(src_ref=mine, dst_ref=outs[a].at[pid], **sems))
    return local, sends, recvs


def _comm_shapes(kind, arrs):
    out_shape = [jax.ShapeDtypeStruct(((NDEV,) if kind == "gather" else ()) + a.shape, a.dtype) for a in arrs]
    n = len(arrs)
    sems = [pltpu.SemaphoreType.DMA((n, 7)), pltpu.SemaphoreType.DMA((n, 7)), pltpu.SemaphoreType.DMA((n,))]
    return out_shape, sems


def _all_to_all(arrs, name):
    n = len(arrs)

    def body(*refs):
        local, sends, recvs = _direct_copies("exchange", refs[:n], refs[n:2 * n], *refs[2 * n:])
        for cp in local + sends:
            cp.start()
        for cp in recvs:
            cp.wait_recv()
        for cp in sends:
            cp.wait_send()
        for cp in local:
            cp.wait()

    out_shape, sems = _comm_shapes("exchange", arrs)
    return pl.pallas_call(body, name=name, out_shape=out_shape, in_specs=[ANY] * n, out_specs=[ANY] * n,
                          scratch_shapes=sems)(*arrs)


def _pcall(body, *, name, grid, in_specs, out_specs, out_shape, operands, sem, scratch_shapes=(), comm=None):
    in_specs, out_specs, out_shape = list(in_specs), list(out_specs), list(out_shape)
    scratch_shapes = list(scratch_shapes)
    if comm is None:
        res = pl.pallas_call(body, name=name, grid=grid, in_specs=in_specs, out_specs=out_specs, out_shape=out_shape,
                             scratch_shapes=scratch_shapes, compiler_params=_cp(*sem))(*operands)
        return list(res), []
    kind, arrs = comm
    nc, n_in, n_out, n_scr = len(arrs), len(operands), len(out_shape), len(scratch_shapes)
    c_shape, c_sems = _comm_shapes(kind, arrs)

    def carrier(*refs):
        ins, cin = refs[:n_in], refs[n_in:n_in + nc]
        outs = refs[n_in + nc:n_in + nc + n_out]
        cout = refs[n_in + nc + n_out:n_in + 2 * nc + n_out]
        scr = refs[n_in + 2 * nc + n_out:n_in + 2 * nc + n_out + n_scr]
        sems = refs[n_in + 2 * nc + n_out + n_scr:]
        pids = [pl.program_id(d) for d in range(len(grid))]
        first = functools.reduce(jnp.logical_and, [p == 0 for p in pids])
        last = functools.reduce(jnp.logical_and, [p == g - 1 for p, g in zip(pids, grid)])

        @pl.when(first)
        def _():
            local, sends, _ = _direct_copies(kind, cin, cout, *sems, want_recvs=False)
            for cp in local + sends:
                cp.start()

        body(*ins, *outs, *scr)

        @pl.when(last)
        def _():
            local, sends, recvs = _direct_copies(kind, cin, cout, *sems)
            for cp in recvs:
                cp.wait_recv()
            for cp in sends:
                cp.wait_send()
            for cp in local:
                cp.wait()

    res = pl.pallas_call(
        carrier, name=name, grid=grid, in_specs=in_specs + [ANY] * nc, out_specs=out_specs + [ANY] * nc,
        out_shape=out_shape + c_shape, scratch_shapes=scratch_shapes + c_sems,
        compiler_params=_cp(*["arbitrary"] * len(grid)))(*operands, *arrs)
    return list(res[:n_out]), list(res[n_out:])


def _mm_call(pairs, grid, a_spec, b_spec, o_spec, out_shape, acc_shape, nk, ca, cb, name,
             res=None, res_spec=None, comm=None):
    npairs = len(pairs)
    multi = isinstance(out_shape, (list, tuple))
    nout = len(out_shape) if multi else 1

    def body(*refs):
        ab = refs[:2 * npairs]
        r_ref = refs[2 * npairs] if res is not None else None
        o_refs, acc = refs[-1 - nout:-1], refs[-1]
        k = pl.program_id(2)

        @pl.when(k == 0)
        def _():
            acc[...] = jnp.zeros_like(acc)

        tot = None
        for p in range(npairs):
            d = _dot(ab[2 * p][...].astype(BF16), ab[2 * p + 1][...].astype(BF16), ca, cb)
            tot = d if tot is None else tot + d
        acc[...] += tot

        @pl.when(k == nk - 1)
        def _():
            r = acc[...]
            if r_ref is not None:
                r = r + r_ref[...]
            for o_ref in o_refs:
                o_ref[...] = r.astype(o_ref.dtype)

    ops, specs = [], []
    for a, b in pairs:
        ops += [a, b]
        specs += [a_spec, b_spec]
    if res is not None:
        ops.append(res)
        specs.append(res_spec)
    outs, moved = _pcall(
        body, name=name, grid=grid, in_specs=specs, out_specs=[o_spec] * nout,
        out_shape=out_shape if multi else [out_shape], operands=ops,
        scratch_shapes=[pltpu.VMEM(acc_shape, F32)], sem=("parallel", "parallel", "arbitrary"), comm=comm)
    outs = outs if multi else outs[0]
    return outs if comm is None else (outs, moved)


def _mm(a, b, name, ta=False, tb=False, out_dtype=F32, res=None, tm=1024, tn=1024, tk=512):
    M, K = (a.shape[1], a.shape[0]) if ta else a.shape
    N = b.shape[0] if tb else b.shape[1]
    tm, tn, tk = _tile(M, tm), _tile(N, tn), _tile(K, tk)
    a_spec = pl.BlockSpec((tk, tm), lambda i, j, k: (k, i)) if ta else pl.BlockSpec((tm, tk), lambda i, j, k: (i, k))
    b_spec = pl.BlockSpec((tn, tk), lambda i, j, k: (j, k)) if tb else pl.BlockSpec((tk, tn), lambda i, j, k: (k, j))
    o_spec = pl.BlockSpec((tm, tn), lambda i, j, k: (i, j))
    if isinstance(out_dtype, tuple):
        out_shape = [jax.ShapeDtypeStruct((M, N), d) for d in out_dtype]
    else:
        out_shape = jax.ShapeDtypeStruct((M, N), out_dtype)
    return _mm_call([(a, b)], (M // tm, N // tn, K // tk), a_spec, b_spec, o_spec,
                    out_shape, (tm, tn), K // tk,
                    0 if ta else 1, 1 if tb else 0, name, res=res, res_spec=o_spec)


def _mm_down(act, wd, res, name, tm=1024, tn=1024):
    _, S, FB = act.shape
    D = wd.shape[1]
    tm, tn = _tile(S, tm), _tile(D, tn)
    o_spec = pl.BlockSpec((tm, tn), lambda i, j, k: (i, j))
    return _mm_call([(act, wd)], (S // tm, D // tn, NDEV),
                    pl.BlockSpec((None, tm, FB), lambda i, j, k: (k, i, 0)),
                    pl.BlockSpec((FB, tn), lambda i, j, k: (k, j)), o_spec,
                    jax.ShapeDtypeStruct((S, D), F32), (tm, tn), NDEV, 1, 0, name, res=res, res_spec=o_spec)


def _mm_dwdown(act, dy, name, tn=1024, tk=512):
    _, S, FB = act.shape
    D = dy.shape[1]
    tn, tk = _tile(D, tn), _tile(S, tk)
    return _mm_call([(act, dy)], (NDEV, D // tn, S // tk),
                    pl.BlockSpec((None, tk, FB), lambda i, j, k: (i, k, 0)),
                    pl.BlockSpec((tk, tn), lambda i, j, k: (k, j)),
                    pl.BlockSpec((FB, tn), lambda i, j, k: (i, j)),
                    jax.ShapeDtypeStruct((NDEV * FB, D), BF16), (FB, tn), S // tk, 0, 0, name)


def _mm_dh2(dg, wg, du, wu, name, tm=1024, tn=1024, comm=None):
    _, S, FB = dg.shape
    D = wg.shape[1]
    tm, tn = _tile(S, tm), _tile(D, tn)
    return _mm_call([(dg, wg), (du, wu)], (S // tm, D // tn, NDEV),
                    pl.BlockSpec((None, tm, FB), lambda i, j, k: (k, i, 0)),
                    pl.BlockSpec((None, tn, FB), lambda i, j, k: (k, j, 0)),
                    pl.BlockSpec((tm, tn), lambda i, j, k: (i, j)),
                    jax.ShapeDtypeStruct((S, D), F32), (tm, tn), NDEV, 1, 1, name, comm=comm)


def _mm_dwgate(h2, dg, name, tm=1024, tk=512):
    _, S, FB = dg.shape
    D = h2.shape[1]
    tm, tk = _tile(D, tm), _tile(S, tk)
    return _mm_call([(h2, dg)], (NDEV, D // tm, S // tk),
                    pl.BlockSpec((tk, tm), lambda p, i, k: (k, i)),
                    pl.BlockSpec((None, tk, FB), lambda p, i, k: (p, k, 0)),
                    pl.BlockSpec((None, tm, FB), lambda p, i, k: (p, i, 0)),
                    jax.ShapeDtypeStruct((NDEV, D, FB), BF16), (tm, FB), S // tk, 0, 0, name)


def _ffn_up(h2, wg, wu, name, tm=512, comm=None):
    S, D = h2.shape
    FB = wg.shape[2]
    tm = _tile(S, tm)

    def body(h_ref, wg_ref, wu_ref, g_ref, u_ref, act_ref):
        h = h_ref[...]
        g = _dot_nn(h, wg_ref[...])
        u = _dot_nn(h, wu_ref[...])
        g_ref[...] = g
        u_ref[...] = u
        act_ref[...] = (g / (1.0 + jnp.exp(-g)) * u).astype(BF16)

    w_spec = pl.BlockSpec((None, D, FB), lambda p, i: (p, 0, 0))
    o_spec = pl.BlockSpec((None, tm, FB), lambda p, i: (p, i, 0))
    shp = (NDEV, S, FB)
    outs, moved = _pcall(
        body, name=name, grid=(NDEV, S // tm),
        in_specs=[pl.BlockSpec((tm, D), lambda p, i: (i, 0)), w_spec, w_spec],
        out_specs=[o_spec, o_spec, o_spec],
        out_shape=[jax.ShapeDtypeStruct(shp, F32), jax.ShapeDtypeStruct(shp, F32), jax.ShapeDtypeStruct(shp, BF16)],
        operands=[h2, wg, wu], sem=("parallel", "parallel"), comm=comm)
    return outs if comm is None else (outs, moved)


def _ffn_dact(dy, wd, g, u, name, tm=512):
    S, D = dy.shape
    FB = g.shape[2]
    tm = _tile(S, tm)

    def body(dy_ref, wd_ref, g_ref, u_ref, dg_ref, du_ref):
        dact = _dot_nt(dy_ref[...].astype(BF16), wd_ref[...])
        gv = g_ref[...]
        sg = 1.0 / (1.0 + jnp.exp(-gv))
        dg_ref[...] = (dact * u_ref[...] * (sg * (1.0 + gv * (1.0 - sg)))).astype(BF16)
        du_ref[...] = (dact * (gv * sg)).astype(BF16)

    t_spec = pl.BlockSpec((None, tm, FB), lambda p, i: (p, i, 0))
    shp = jax.ShapeDtypeStruct((NDEV, S, FB), BF16)
    return pl.pallas_call(
        body, name=name, grid=(NDEV, S // tm),
        in_specs=[pl.BlockSpec((tm, D), lambda p, i: (i, 0)), pl.BlockSpec((FB, D), lambda p, i: (p, 0)),
                  t_spec, t_spec],
        out_specs=[t_spec, t_spec], out_shape=[shp, shp],
        compiler_params=_cp("parallel", "parallel"),
    )(dy, wd, g, u)


def _rms_fwd(x, gain, width, cb, out_dtype, name, ts=512):
    S = x.shape[0]
    ts = _tile(S, ts)

    def body(x_ref, g_ref, o_ref):
        xv = x_ref[...]
        r = lax.rsqrt(jnp.mean(xv * xv, axis=1, keepdims=True) + EPS)
        o_ref[...] = (xv * r * g_ref[...]).astype(o_ref.dtype)

    return pl.pallas_call(
        body, name=name, grid=(S // ts,),
        in_specs=[pl.BlockSpec((ts, width), lambda i: (i, cb)), pl.BlockSpec((1, width), lambda i: (0, 0))],
        out_specs=pl.BlockSpec((ts, width), lambda i: (i, 0)),
        out_shape=jax.ShapeDtypeStruct((S, width), out_dtype),
        compiler_params=_cp("parallel"),
    )(x, gain)


def _rms_bwd(x, gain, dy, width, cb, name, res=None, ts=256):
    S = x.shape[0]
    ts = _tile(S, ts)
    has_res = res is not None

    def body(*refs):
        x_ref, g_ref, dy_ref = refs[:3]
        r_ref = refs[3] if has_res else None
        dx_ref, dg_ref = refs[-2], refs[-1]

        @pl.when(pl.program_id(0) == 0)
        def _():
            dg_ref[...] = jnp.zeros_like(dg_ref)

        xv = x_ref[...]
        r = lax.rsqrt(jnp.mean(xv * xv, axis=1, keepdims=True) + EPS)
        xh = xv * r
        dyv = dy_ref[...]
        dyg = dyv * g_ref[...]
        dx = r * (dyg - xh * jnp.mean(dyg * xh, axis=1, keepdims=True))
        if has_res:
            dx = dx + r_ref[...]
        dx_ref[...] = dx
        dg_ref[...] += jnp.sum(dyv * xh, axis=0, keepdims=True)

    row = pl.BlockSpec((ts, width), lambda i: (i, 0))
    vec = pl.BlockSpec((1, width), lambda i: (0, 0))
    ops = [x, gain, dy] + ([res] if has_res else [])
    specs = [pl.BlockSpec((ts, width), lambda i: (i, cb)), vec, row] + ([row] if has_res else [])
    return pl.pallas_call(
        body, name=name, grid=(S // ts,), in_specs=specs, out_specs=[row, vec],
        out_shape=[jax.ShapeDtypeStruct((S, width), F32), jax.ShapeDtypeStruct((1, width), F32)],
        compiler_params=_cp("arbitrary"),
    )(*ops)


def _gn_fwd(outs, gain, name, ts=512):
    S, GW = outs[0].shape
    ts = _tile(S, ts)

    def body(a_ref, b_ref, c_ref, d_ref, g_ref, o_ref):
        for g, r_ref in enumerate((a_ref, b_ref, c_ref, d_ref)):
            xv = r_ref[...]
            r = lax.rsqrt(jnp.mean(xv * xv, axis=1, keepdims=True) + EPS)
            o_ref[:, g * GW:(g + 1) * GW] = (xv * r * g_ref[:, g * GW:(g + 1) * GW]).astype(BF16)

    row = pl.BlockSpec((ts, GW), lambda i: (i, 0))
    return pl.pallas_call(
        body, name=name, grid=(S // ts,),
        in_specs=[row] * 4 + [pl.BlockSpec((1, 4 * GW), lambda i: (0, 0))],
        out_specs=pl.BlockSpec((ts, 4 * GW), lambda i: (i, 0)),
        out_shape=jax.ShapeDtypeStruct((S, 4 * GW), BF16),
        compiler_params=_cp("parallel"),
    )(*outs, gain)


def _gn_bwd(outs, gain, dmix, name, ts=256):
    S, GW = outs[0].shape
    ts = _tile(S, ts)

    def body(a_ref, b_ref, c_ref, d_ref, g_ref, dm_ref, da_ref, db_ref, dc_ref, dd_ref, dg_ref):
        @pl.when(pl.program_id(0) == 0)
        def _():
            dg_ref[...] = jnp.zeros_like(dg_ref)

        for g, (r_ref, o_ref) in enumerate(zip((a_ref, b_ref, c_ref, d_ref), (da_ref, db_ref, dc_ref, dd_ref))):
            sl = slice(g * GW, (g + 1) * GW)
            xv = r_ref[...]
            r = lax.rsqrt(jnp.mean(xv * xv, axis=1, keepdims=True) + EPS)
            xh = xv * r
            dyv = dm_ref[:, sl]
            dyg = dyv * g_ref[:, sl]
            o_ref[...] = (r * (dyg - xh * jnp.mean(dyg * xh, axis=1, keepdims=True))).astype(BF16)
            dg_ref[:, sl] += jnp.sum(dyv * xh, axis=0, keepdims=True)

    row = pl.BlockSpec((ts, GW), lambda i: (i, 0))
    vec = pl.BlockSpec((1, 4 * GW), lambda i: (0, 0))
    return pl.pallas_call(
        body, name=name, grid=(S // ts,),
        in_specs=[row] * 4 + [vec, pl.BlockSpec((ts, 4 * GW), lambda i: (i, 0))],
        out_specs=[row] * 4 + [vec],
        out_shape=[jax.ShapeDtypeStruct((S, GW), BF16)] * 4 + [jax.ShapeDtypeStruct((1, 4 * GW), F32)],
        compiler_params=_cp("arbitrary"),
    )(*outs, gain, dmix)


def _rope(x, cbs, cos, sin, half, out_dtype, name, ts=512):
    S = x.shape[0]
    cb0, nb, stride = cbs
    ts = _tile(S, ts)

    def body(x_ref, c_ref, s_ref, o_ref):
        xv = x_ref[...].astype(F32)
        if half:
            lane = lax.broadcasted_iota(jnp.int32, xv.shape, 1)
            partner = jnp.where(lane % 64 < 32, pltpu.roll(xv, LANE - 32, 1), pltpu.roll(xv, 32, 1))
        else:
            partner = pltpu.roll(xv, 64, 1)
        o_ref[...] = (xv * c_ref[...] + partner * s_ref[...]).astype(o_ref.dtype)

    tab = pl.BlockSpec((ts, LANE), lambda i, j: (i, 0))
    return pl.pallas_call(
        body, name=name, grid=(S // ts, nb),
        in_specs=[pl.BlockSpec((ts, LANE), lambda i, j: (i, cb0 + stride * j)), tab, tab],
        out_specs=pl.BlockSpec((ts, LANE), lambda i, j: (i, j)),
        out_shape=jax.ShapeDtypeStruct((S, nb * LANE), out_dtype),
        compiler_params=_cp("parallel", "parallel"),
    )(x, cos, sin)


def _final_loss(x, gain, target, name, ts=256):
    S, D = x.shape
    ts = _tile(S, ts)

    def body(x_ref, g_ref, t_ref, dy_ref, l_ref):
        @pl.when(pl.program_id(0) == 0)
        def _():
            l_ref[...] = jnp.zeros_like(l_ref)

        xv = x_ref[...]
        r = lax.rsqrt(jnp.mean(xv * xv, axis=1, keepdims=True) + EPS)
        err = xv * r * g_ref[...] - t_ref[...]
        dy_ref[...] = err * (1.0 / D)
        part = jnp.sum(jnp.mean(err * err, axis=1, keepdims=True), axis=0, keepdims=True)
        l_ref[...] += jnp.broadcast_to(0.5 * part, (1, LANE))

    row = pl.BlockSpec((ts, D), lambda i: (i, 0))
    return pl.pallas_call(
        body, name=name, grid=(S // ts,),
        in_specs=[row, pl.BlockSpec((1, D), lambda i: (0, 0)), row],
        out_specs=[row, pl.BlockSpec((1, LANE), lambda i: (0, 0))],
        out_shape=[jax.ShapeDtypeStruct((S, D), F32), jax.ShapeDtypeStruct((1, LANE), F32)],
        compiler_params=_cp("arbitrary"),
    )(x, gain, target)


def _colspec(rows, f):
    return pl.BlockSpec((rows, LANE), f)


def _soft_tiles(S):
    tq = _tile(S, TQ)
    tk = _tile(S, TKS)
    assert tk % tq == 0
    return tq, tk


def _key_row(crow_ref, j, tk):
    n = tk // TK
    return jnp.concatenate([crow_ref[j * n + c] for c in range(n)], axis=1)


def _attn_fwd(S, H, q1, q1cb, k1, k1cb, v, vcb, scale, name, q2=None, q2cb=None, k2=None, k2cb=None,
              tab=None, win=None, ccol=None, crow=None, comm=None):
    tq, tk = _soft_tiles(S)
    has2, hastab, hasc = q2 is not None, tab is not None, ccol is not None

    def body(*refs):
        it = iter(refs)
        q1r, k1r, vr = next(it), next(it), next(it)
        q2r, k2r = (next(it), next(it)) if has2 else (None, None)
        tabr = next(it) if hastab else None
        ccolr, crowr = (next(it), next(it)) if hasc else (None, None)
        o_ref, lse_ref = next(it), next(it)
        i = pl.program_id(1)
        q = q1r[...]
        qb2 = q2r[...] if has2 else None
        cq = ccolr[:, 0:1] if hasc else None
        qpos = i * tq + lax.broadcasted_iota(jnp.int32, (tq, tk), 0)
        kio = lax.broadcasted_iota(jnp.int32, (tq, tk), 1)
        j_diag = (i * tq) // tk
        j_lo = jnp.maximum((i * tq - win) // tk, 0) if win else 0

        def step(j, carry, masked):
            m, l, acc = carry
            off = pl.multiple_of(j * tk, tk)
            s = _dot_nt(q, k1r[pl.ds(off, tk), :])
            if has2:
                s = s + _dot_nt(qb2, k2r[pl.ds(off, tk), :])
            s = s * scale
            if hastab:
                s = s + tabr[i - j * (tk // tq)]
            else:
                if hasc:
                    s = s + (cq - _key_row(crowr, j, tk))
                if masked:
                    s = jnp.where(kio + j * tk <= qpos, s, NEG)
            mn = jnp.maximum(m, jnp.max(s, axis=1, keepdims=True))
            p = jnp.exp(s - mn)
            al = jnp.exp(m - mn)
            l = al * l + jnp.sum(p, axis=1, keepdims=True)
            acc = al * acc + _dot_nn(p.astype(BF16), vr[pl.ds(off, tk), :])
            return mn, l, acc

        carry = (jnp.full((tq, 1), NEG, F32), jnp.zeros((tq, 1), F32), jnp.zeros((tq, LANE), F32))
        if hastab:
            carry = lax.fori_loop(j_lo, j_diag + 1, functools.partial(step, masked=False), carry)
        else:
            carry = lax.fori_loop(j_lo, j_diag, functools.partial(step, masked=False), carry)
            carry = step(j_diag, carry, True)
        m, l, acc = carry
        o_ref[...] = acc / l
        lse_ref[...] = jnp.broadcast_to(m + jnp.log(l), (tq, LANE))

    ops = [q1, k1, v]
    specs = [_colspec(tq, lambda h, i: (i, q1cb(h))), _colspec(S, lambda h, i: (0, k1cb(h))),
             _colspec(S, lambda h, i: (0, vcb(h)))]
    if has2:
        ops += [q2, k2]
        specs += [_colspec(tq, lambda h, i: (i, q2cb(h))), _colspec(S, lambda h, i: (0, k2cb(h)))]
    if hastab:
        ops.append(tab)
        specs.append(pl.BlockSpec(tab.shape, lambda h, i: (0, 0, 0)))
    if hasc:
        ops += [ccol, crow]
        specs += [_colspec(tq, lambda h, i: (i, h)),
                  pl.BlockSpec((None, S // TK, 1, TK), lambda h, i: (h, 0, 0, 0))]
    o_spec = _colspec(tq, lambda h, i: (i, h))
    shp = jax.ShapeDtypeStruct((S, H * LANE), F32)
    outs, moved = _pcall(
        body, name=name, grid=(H, S // tq), in_specs=specs, out_specs=[o_spec, o_spec], out_shape=[shp, shp],
        operands=ops, sem=("parallel", "arbitrary"), comm=comm)
    return outs if comm is None else (outs, moved)


def _attn_bwd(S, H, q1, q1cb, k1, k1cb, v, vcb, o, do, lse, scale, name, q2=None, q2cb=None, k2=None, k2cb=None,
              tab=None, win=None, ccol=None, crow=None, comm=None):
    tq, tk = _soft_tiles(S)
    has2, hastab, hasc = q2 is not None, tab is not None, ccol is not None

    def body(*refs):
        it = iter(refs)
        q1r, k1r, vr, o_r, do_r, lse_r = (next(it) for _ in range(6))
        q2r, k2r = (next(it), next(it)) if has2 else (None, None)
        tabr = next(it) if hastab else None
        ccolr, crowr = (next(it), next(it)) if hasc else (None, None)
        dq1_r, dk1_r, dv_r = next(it), next(it), next(it)
        dq2_r, dk2_r = (next(it), next(it)) if has2 else (None, None)
        dcr_r = next(it) if hasc else None
        i = pl.program_id(1)

        @pl.when(i == 0)
        def _():
            dk1_r[...] = jnp.zeros_like(dk1_r)
            dv_r[...] = jnp.zeros_like(dv_r)
            if has2:
                dk2_r[...] = jnp.zeros_like(dk2_r)
            if hasc:
                dcr_r[...] = jnp.zeros_like(dcr_r)

        q = q1r[...]
        qb2 = q2r[...] if has2 else None
        dob = do_r[...]
        delta = jnp.sum(dob.astype(F32) * o_r[...], axis=1, keepdims=True)
        lse_c = lse_r[:, 0:1]
        cq = ccolr[:, 0:1] if hasc else None
        qpos = i * tq + lax.broadcasted_iota(jnp.int32, (tq, tk), 0)
        kio = lax.broadcasted_iota(jnp.int32, (tq, tk), 1)
        j_diag = (i * tq) // tk
        j_lo = jnp.maximum((i * tq - win) // tk, 0) if win else 0

        def probs(j, masked):
            off = pl.multiple_of(j * tk, tk)
            kb = k1r[pl.ds(off, tk), :]
            s = _dot_nt(q, kb)
            kb2 = None
            if has2:
                kb2 = k2r[pl.ds(off, tk), :]
                s = s + _dot_nt(qb2, kb2)
            s = s * scale
            if hastab:
                s = s + tabr[i - j * (tk // tq)]
            else:
                if hasc:
                    s = s + (cq - _key_row(crowr, j, tk))
                if masked:
                    s = jnp.where(kio + j * tk <= qpos, s, NEG)
            p = jnp.exp(s - lse_c)
            dp = _dot_nt(dob, vr[pl.ds(off, tk), :])
            return off, kb, kb2, p, dp

        def sweep(fn, carry):
            if hastab:
                return lax.fori_loop(j_lo, j_diag + 1, functools.partial(fn, masked=False), carry)
            carry = lax.fori_loop(j_lo, j_diag, functools.partial(fn, masked=False), carry)
            return fn(j_diag, carry, True)

        if hasc:
            def dstep(j, acc, masked):
                _, _, _, p, dp = probs(j, masked)
                return acc + jnp.sum(p * dp, axis=1, keepdims=True)

            delta = sweep(dstep, jnp.zeros((tq, 1), F32))

        def step(j, carry, masked):
            dq, dq2 = carry
            off, kb, kb2, p, dp = probs(j, masked)
            ds = p * (dp - delta)
            dsb = ds.astype(BF16)
            dq = dq + _dot_nn(dsb, kb)
            dk1_r[pl.ds(off, tk), :] += _dot_tn(dsb, q) * scale
            dv_r[pl.ds(off, tk), :] += _dot_tn(p.astype(BF16), dob)
            if has2:
                dq2 = dq2 + _dot_nn(dsb, kb2)
                dk2_r[pl.ds(off, tk), :] += _dot_tn(dsb, qb2) * scale
            if hasc:
                cs = -jnp.sum(ds, axis=0, keepdims=True)
                for c in range(tk // TK):
                    dcr_r[j * (tk // TK) + c] += cs[:, c * TK:(c + 1) * TK]
            return dq, dq2

        z = jnp.zeros((tq, LANE), F32)
        dq, dq2 = sweep(step, (z, z))
        dq1_r[...] = dq * scale
        if has2:
            dq2_r[...] = dq2 * scale

    qspec = _colspec(tq, lambda h, i: (i, h))
    kspec = _colspec(S, lambda h, i: (0, h))
    ops = [q1, k1, v, o, do, lse]
    specs = [_colspec(tq, lambda h, i: (i, q1cb(h))), _colspec(S, lambda h, i: (0, k1cb(h))),
             _colspec(S, lambda h, i: (0, vcb(h))), qspec, qspec, qspec]
    if has2:
        ops += [q2, k2]
        specs += [_colspec(tq, lambda h, i: (i, q2cb(h))), _colspec(S, lambda h, i: (0, k2cb(h)))]
    if hastab:
        ops.append(tab)
        specs.append(pl.BlockSpec(tab.shape, lambda h, i: (0, 0, 0)))
    if hasc:
        ops += [ccol, crow]
        specs += [qspec, pl.BlockSpec((None, S // TK, 1, TK), lambda h, i: (h, 0, 0, 0))]
    assert all(t.dtype == BF16 for t in ops[:3] + [do] + ([q2, k2] if has2 else []))
    out_specs = [qspec, kspec, kspec] + ([qspec, kspec] if has2 else [])
    shp = jax.ShapeDtypeStruct((S, H * LANE), F32)
    out_shape = [shp] * len(out_specs)
    if hasc:
        out_specs.append(pl.BlockSpec((None, S // TK, 1, TK), lambda h, i: (h, 0, 0, 0)))
        out_shape.append(jax.ShapeDtypeStruct((H, S // TK, 1, TK), F32))
    outs, moved = _pcall(
        body, name=name, grid=(H, S // tq), in_specs=specs, out_specs=out_specs, out_shape=out_shape,
        operands=ops, sem=("parallel", "arbitrary"), comm=comm)
    return outs if comm is None else (outs, moved)


def _scan_matrix(kind):
    j = np.arange(TK)[:, None]
    s = np.arange(TK)[None, :]
    tri = {"suffix_ex": j > s, "prefix_in": j <= s, "prefix_ex": j < s}[kind].astype(np.float32)
    half = np.concatenate([tri, np.ones((TK, TK), np.float32)], axis=1)
    return jnp.asarray(np.concatenate([half, half], axis=0), BF16)


def _scan_mxu(x, mat, carry, reverse):
    n = x.shape[1] // TK
    hi = x.astype(BF16)
    lo = (x - hi.astype(F32)).astype(BF16)
    parts = [None] * n
    for b in (reversed(range(n)) if reverse else range(n)):
        sl = slice(b * TK, (b + 1) * TK)
        r = _dot_nn(jnp.concatenate([hi[:, sl], lo[:, sl]], axis=1), mat)
        parts[b] = r[:, :TK] + carry
        carry = carry + r[:, TK:]
    return jnp.concatenate(parts, axis=1), carry


def _stick_logs(z):
    e = jnp.exp(-jnp.abs(z))
    return e, -jnp.maximum(z, 0.0) - jnp.log(1.0 + e)


def _stick_fwd(S, H, x, qcb, kcb, vcb, scale, name, comm=None):
    tq, tk = _soft_tiles(S)
    assert x.dtype == BF16

    def body(q_r, k_r, v_r, mat_r, o_ref, t_ref):
        i = pl.program_id(1)
        q = q_r[...]
        qpos = i * tq + lax.broadcasted_iota(jnp.int32, (tq, tk), 0)
        lane = lax.broadcasted_iota(jnp.int32, (tq, tk), 1)
        j_diag = (i * tq) // tk

        def step(j, carry, masked):
            c, acc = carry
            off = pl.multiple_of(j * tk, tk)
            z = _dot_nt(q, k_r[pl.ds(off, tk), :]) * scale
            _, lk = _stick_logs(z)
            if masked:
                past = lane + j * tk < qpos
                lk = jnp.where(past, lk, 0.0)
            suf, c = _scan_mxu(lk, mat_r[...], c, True)
            a = jnp.exp(z + lk + suf)
            if masked:
                a = jnp.where(past, a, 0.0)
            acc = acc + _dot_nn(a.astype(BF16), v_r[pl.ds(off, tk), :])
            return c, acc

        carry = step(j_diag, (jnp.zeros((tq, TK), F32), jnp.zeros((tq, LANE), F32)), True)
        c, acc = lax.fori_loop(0, j_diag, lambda jj, cr: step(j_diag - 1 - jj, cr, False), carry)
        o_ref[...] = acc
        t_ref[...] = c

    o_spec = _colspec(tq, lambda h, i: (i, h))
    shp = jax.ShapeDtypeStruct((S, H * LANE), F32)
    mat = _scan_matrix("suffix_ex")
    outs, moved = _pcall(
        body, name=name, grid=(H, S // tq),
        in_specs=[_colspec(tq, lambda h, i: (i, qcb(h))), _colspec(S, lambda h, i: (0, kcb(h))),
                  _colspec(S, lambda h, i: (0, vcb(h))), pl.BlockSpec(mat.shape, lambda h, i: (0, 0))],
        out_specs=[o_spec, o_spec], out_shape=[shp, shp],
        operands=[x, x, x, mat], sem=("parallel", "arbitrary"), comm=comm)
    return outs if comm is None else (outs, moved)


def _stick_bwd(S, H, x, qcb, kcb, vcb, do, tot, scale, name, comm=None):
    tq, tk = _soft_tiles(S)
    assert x.dtype == BF16 and do.dtype == BF16

    def body(q_r, k_r, v_r, do_r, t_r, pin_r, pex_r, dq_r, dk_r, dv_r):
        i = pl.program_id(1)

        @pl.when(i == 0)
        def _():
            dk_r[...] = jnp.zeros_like(dk_r)
            dv_r[...] = jnp.zeros_like(dv_r)

        q = q_r[...]
        dob = do_r[...]
        total = jnp.concatenate([t_r[...]] * (tk // TK), axis=1)
        qpos = i * tq + lax.broadcasted_iota(jnp.int32, (tq, tk), 0)
        lane = lax.broadcasted_iota(jnp.int32, (tq, tk), 1)
        j_diag = (i * tq) // tk

        def step(j, carry, masked):
            cl, cg, dq = carry
            off = pl.multiple_of(j * tk, tk)
            kb = k_r[pl.ds(off, tk), :]
            z = _dot_nt(q, kb) * scale
            e, lk = _stick_logs(z)
            if masked:
                past = lane + j * tk < qpos
                lk = jnp.where(past, lk, 0.0)
            pre, cl = _scan_mxu(lk, pin_r[...], cl, False)
            a = jnp.exp(z + lk + (total - pre))
            if masked:
                a = jnp.where(past, a, 0.0)
            g = _dot_nt(dob, v_r[pl.ds(off, tk), :]) * a
            gpre, cg = _scan_mxu(g, pex_r[...], cg, False)
            inv = 1.0 / (1.0 + e)
            small = e * inv
            pos = z >= 0
            dz = g * jnp.where(pos, small, inv) - jnp.where(pos, inv, small) * gpre
            if masked:
                dz = jnp.where(past, dz, 0.0)
            dzb = dz.astype(BF16)
            dk_r[pl.ds(off, tk), :] += _dot_tn(dzb, q) * scale
            dv_r[pl.ds(off, tk), :] += _dot_tn(a.astype(BF16), dob)
            return cl, cg, dq + _dot_nn(dzb, kb)

        zt = jnp.zeros((tq, TK), F32)
        carry = lax.fori_loop(0, j_diag, functools.partial(step, masked=False), (zt, zt, jnp.zeros((tq, LANE), F32)))
        dq_r[...] = step(j_diag, carry, True)[2] * scale

    qspec = _colspec(tq, lambda h, i: (i, h))
    kspec = _colspec(S, lambda h, i: (0, h))
    shp = jax.ShapeDtypeStruct((S, H * LANE), F32)
    pin, pex = _scan_matrix("prefix_in"), _scan_matrix("prefix_ex")
    mspec = pl.BlockSpec(pin.shape, lambda h, i: (0, 0))
    outs, moved = _pcall(
        body, name=name, grid=(H, S // tq),
        in_specs=[_colspec(tq, lambda h, i: (i, qcb(h))), _colspec(S, lambda h, i: (0, kcb(h))),
                  _colspec(S, lambda h, i: (0, vcb(h))), qspec, qspec, mspec, mspec],
        out_specs=[qspec, kspec, kspec], out_shape=[shp] * 3,
        operands=[x, x, x, do, tot, pin, pex], sem=("parallel", "arbitrary"), comm=comm)
    return outs if comm is None else (outs, moved)


def _scan8(x, rows, reverse):
    for sh in (1, 2, 4):
        if reverse:
            x = x + jnp.where(rows + sh < 8, pltpu.roll(x, 8 - sh, 0), 0.0)
        else:
            x = x + jnp.where(rows >= sh, pltpu.roll(x, sh, 0), 0.0)
    return x


def _fox_prep(S, H, proj, fcb, bias, name):
    tk = TK

    def body(f_ref, b_ref, ccol_ref, crow_ref, scr):
        rows = lax.broadcasted_iota(jnp.int32, (8, LANE), 0)

        def step(t, carry):
            off = pl.multiple_of(t * 8, 8)
            xb = f_ref[pl.ds(off, 8), :] + b_ref[...]
            lf = jnp.minimum(xb, 0.0) - jnp.log(1.0 + jnp.exp(-jnp.abs(xb)))
            lf = _scan8(lf, rows, False) + carry
            scr[pl.ds(off, 8), :] = lf
            return lf[7:8, :]

        lax.fori_loop(0, S // 8, step, jnp.zeros((1, LANE), F32))
        for h in range(H):
            ccol_ref[:, h * LANE:(h + 1) * LANE] = jnp.broadcast_to(scr[:, h:h + 1], (S, LANE))

            def tr(t, _):
                off = pl.multiple_of(t * tk, tk)
                blk = ccol_ref[pl.ds(off, tk), h * LANE:(h + 1) * LANE]
                crow_ref[h, t] = blk.T[0:1, :]
                return 0

            lax.fori_loop(0, S // tk, tr, 0)

    return pl.pallas_call(
        body, name=name, grid=(1,),
        in_specs=[_colspec(S, lambda i: (0, fcb)), pl.BlockSpec((1, LANE), lambda i: (0, 0))],
        out_specs=[pl.BlockSpec((S, H * LANE), lambda i: (0, 0)),
                   pl.BlockSpec((H, S // tk, 1, tk), lambda i: (0, 0, 0, 0))],
        out_shape=[jax.ShapeDtypeStruct((S, H * LANE), F32), jax.ShapeDtypeStruct((H, S // tk, 1, tk), F32)],
        scratch_shapes=[pltpu.VMEM((S, LANE), F32)],
        compiler_params=_cp("arbitrary"),
    )(proj, bias)


def _fox_bwd(S, H, proj, fcb, bias, dcr, name):
    tk = TK

    def body(f_ref, b_ref, dcr_ref, df_ref, db_ref, scr):
        rows = lax.broadcasted_iota(jnp.int32, (8, LANE), 0)
        lane_t = lax.broadcasted_iota(jnp.int32, (tk, LANE), 1)
        nb = S // 8

        def tr(t, _):
            off = pl.multiple_of(t * tk, tk)
            d = jnp.zeros((tk, LANE), F32)
            for h in range(H):
                d = d + jnp.where(lane_t == h, jnp.broadcast_to(dcr_ref[h, t], (LANE, tk)).T, 0.0)
            scr[pl.ds(off, tk), :] = d
            return 0

        lax.fori_loop(0, S // tk, tr, 0)

        def step(tt, carry):
            suffix, db = carry
            off = pl.multiple_of((nb - 1 - tt) * 8, 8)
            d = _scan8(scr[pl.ds(off, 8), :], rows, True) + suffix
            xb = f_ref[pl.ds(off, 8), :] + b_ref[...]
            e = jnp.exp(-jnp.abs(xb))
            dx = d * jnp.where(xb >= 0, e, 1.0) / (1.0 + e)
            df_ref[pl.ds(off, 8), :] = dx
            return d[0:1, :], db + jnp.sum(dx, axis=0, keepdims=True)

        z = jnp.zeros((1, LANE), F32)
        _, db = lax.fori_loop(0, nb, step, (z, z))
        db_ref[...] = db

    return pl.pallas_call(
        body, name=name, grid=(1,),
        in_specs=[_colspec(S, lambda i: (0, fcb)), pl.BlockSpec((1, LANE), lambda i: (0, 0)),
                  pl.BlockSpec((H, S // tk, 1, tk), lambda i: (0, 0, 0, 0))],
        out_specs=[pl.BlockSpec((S, LANE), lambda i: (0, 0)), pl.BlockSpec((1, LANE), lambda i: (0, 0))],
        out_shape=[jax.ShapeDtypeStruct((S, LANE), F32), jax.ShapeDtypeStruct((1, LANE), F32)],
        scratch_shapes=[pltpu.VMEM((S, LANE), F32)],
        compiler_params=_cp("arbitrary"),
    )(proj, bias, dcr)


def _adamw(w, slots, m, v, name, block_bytes=1 << 20):
    R, C = w.shape
    tr = R
    while tr * C * 4 > block_bytes and tr % 16 == 0:
        tr //= 2
    c1 = 1.0 - ADAM_B1 ** ADAM_STEP
    c2 = 1.0 - ADAM_B2 ** ADAM_STEP

    def body(w_ref, s_ref, m_ref, v_ref, g_ref, d_ref, nm_ref, nv_ref):
        g = s_ref[0].astype(F32)
        for s in range(1, NDEV):
            g = g + s_ref[s].astype(F32)
        mn = ADAM_B1 * m_ref[...] + (1.0 - ADAM_B1) * g
        vn = ADAM_B2 * v_ref[...] + (1.0 - ADAM_B2) * (g * g)
        g_ref[...] = g
        nm_ref[...] = mn
        nv_ref[...] = vn
        d_ref[...] = -ADAM_LR * ((mn / c1) / (jnp.sqrt(vn / c2) + ADAM_EPS) + ADAM_WD * w_ref[...])

    row = pl.BlockSpec((tr, C), lambda i: (i, 0))
    return pl.pallas_call(
        body, name=name, grid=(R // tr,),
        in_specs=[row, pl.BlockSpec((NDEV, tr, C), lambda i: (0, i, 0)), row, row],
        out_specs=[row] * 4, out_shape=[jax.ShapeDtypeStruct((R, C), F32)] * 4,
        compiler_params=_cp("parallel"),
    )(w, slots, m, v)


class _Layout:
    def __init__(self, D):
        self.GW = GW = D // 4
        self.H = H = GW // HEAD
        self.QL, self.KVL = 0, Q_LORA
        base = Q_LORA + KV_LORA
        (self.QB, self.KB, self.VB, self.QC, self.KC, self.VC, self.QD, self.KD, self.VD) = (
            base + k * GW for k in range(9))
        self.KR = base + 9 * GW
        self.FC = self.KR + LANE
        self.PW = -(-(self.FC + LANE) // 512) * 512
        self.o_kr = base
        self.o_bc = base + QK_ROPE
        self.o_fc = self.o_bc + 6 * GW
        self.o_d = self.o_fc + H
        self.IN = self.o_d + 3 * GW

    def pad(self, w):
        z = lambda n: jnp.zeros(w.shape[:-1] + (n,), w.dtype)
        return jnp.concatenate([
            w[..., :self.o_kr], w[..., self.o_bc:self.o_fc], w[..., self.o_d:self.IN],
            w[..., self.o_kr:self.o_bc], z(LANE - QK_ROPE), w[..., self.o_fc:self.o_d], z(LANE - self.H),
            z(self.PW - self.FC - LANE)], axis=-1)

    def unpad(self, g):
        return jnp.concatenate([
            g[..., :self.KR - 9 * self.GW], g[..., self.KR:self.KR + QK_ROPE], g[..., self.QB:self.QD],
            g[..., self.FC:self.FC + self.H], g[..., self.QD:self.KR]], axis=-1)


def _rope_tables(S):
    pos = jnp.arange(S, dtype=F32)

    def cs(dim):
        inv = ROPE_THETA ** (-jnp.arange(0, dim, 2, dtype=F32) / dim)
        ang = pos[:, None] * inv[None, :]
        return jnp.cos(ang), jnp.sin(ang)

    c, s = cs(HEAD)
    full = (jnp.concatenate([c, c], 1), jnp.concatenate([-s, s], 1))
    c, s = cs(QK_ROPE)
    z = jnp.zeros((S, LANE - QK_ROPE), F32)
    half = (jnp.concatenate([c, c, z], 1), jnp.concatenate([-s, s, z], 1))
    return full, half


def _dilated_table(tq, tk):
    win = max(w for w, _ in DILATED_PAIRS)
    nd = (win + tk) // tq + 1
    d = np.arange(nd)[:, None, None] * tq + np.arange(tq)[None, :, None] - np.arange(tk)[None, None, :]
    mult = np.zeros(d.shape, np.float64)
    for w, dil in DILATED_PAIRS:
        mult += (d >= 0) & (d <= w) & (d % dil == 0)
    return jnp.asarray(np.where(mult > 0, np.log(np.maximum(mult, 1.0)), NEG), F32), win


def _pack(arrs):
    rows = []
    for a in arrs:
        f = a.reshape(-1).astype(F32)
        f = jnp.pad(f, (0, (-f.shape[0]) % LANE))
        rows.append(f.reshape(-1, LANE))
    p = jnp.concatenate(rows, 0)
    return jnp.pad(p, ((0, (-p.shape[0]) % 8), (0, 0)))


def _unpack(p, shapes):
    out, r = [], 0
    for shp in shapes:
        n = int(np.prod(shp))
        nr = -(-n // LANE)
        out.append(p[r:r + nr].reshape(-1)[:n].reshape(shp))
        r += nr
    return out


def kernel(x, attn_norm, w_in, mla_q_norm, w_uq, mla_kv_norm, w_ukv, fox_forget_bias, group_norm, w_out, ffn_norm, w_gate, w_up, w_down, final_norm, loss_target, m_attn_norm, m_w_in, m_mla_q_norm, m_w_uq, m_mla_kv_norm, m_w_ukv, m_fox_forget_bias, m_group_norm, m_w_out, m_ffn_norm, m_w_gate, m_w_up, m_w_down, m_final_norm, v_attn_norm, v_w_in, v_mla_q_norm, v_w_uq, v_mla_kv_norm, v_w_ukv, v_fox_forget_bias, v_group_norm, v_w_out, v_ffn_norm, v_w_gate, v_w_up, v_w_down, v_final_norm):
    _, S, D = x.shape
    L = attn_norm.shape[0]
    lay = _Layout(D)
    H, GW, PW = lay.H, lay.GW, lay.PW
    FB = w_gate.shape[2]
    QKA = HEAD + QK_ROPE
    x = x[0]
    target = loss_target[0]
    rope_full, rope_half = _rope_tables(S)
    neg = lambda t: (t[0], -t[1])
    tab, win = _dilated_table(*_soft_tiles(S))
    cb = lambda col: col // LANE

    sh = dict(w_in=lay.pad(w_in).astype(BF16),
              **{n: w.astype(BF16) for n, w in (("w_uq", w_uq), ("w_ukv", w_ukv), ("w_out", w_out),
                                                  ("w_gate", w_gate), ("w_up", w_up), ("w_down", w_down))})
    first3 = ["w_in", "w_uq", "w_ukv"]

    def first_weights(g):
        wuq = jnp.transpose(g[1], (1, 0, 2)).reshape(Q_LORA, H, QKA)
        wuq = jnp.pad(wuq, ((0, 0), (0, 0), (0, 2 * LANE - QKA))).reshape(Q_LORA, H * 2 * LANE)
        return dict(win=g[0].reshape(D, PW), wuq=wuq,
                    wukv=jnp.transpose(g[2], (1, 0, 2)).reshape(KV_LORA, H * 2 * LANE))

    def row(a):
        return a.reshape(1, -1)

    def forward(l, x0, W):
        A = dict(x0=x0)
        A["bias"] = jnp.pad(row(fox_forget_bias[l]), ((0, 0), (0, LANE - H)))
        h1 = A["h1"] = _rms_fwd(x0, row(attn_norm[l]), D, 0, BF16, "attn_norm")
        proj, pb = A["proj"], A["pb"] = _mm(h1, W["win"], "in_proj", out_dtype=(F32, BF16))
        qln = A["qln"] = _rms_fwd(proj, row(mla_q_norm[l]), Q_LORA, cb(lay.QL) // 4, BF16, "q_norm")
        kvln = A["kvln"] = _rms_fwd(proj, row(mla_kv_norm[l]), KV_LORA, cb(lay.KVL) // 4, BF16, "kv_norm")
        qa, qab = _mm(qln, W["wuq"], "q_up", out_dtype=(F32, BF16))
        A["qab"] = qab
        kv = A["kv"] = _mm(kvln, W["wukv"], "kv_up", out_dtype=BF16)
        q_pe = A["q_pe"] = _rope(qa, (1, H, 2), *rope_half, True, BF16, "rope_q_mla")
        k_pe = A["k_pe"] = _rope(proj, (cb(lay.KR), 1, 1), *rope_half, True, BF16, "rope_k_mla")
        (A["o_a"], A["lse_a"]), (g_down,) = _attn_fwd(
            S, H, qab, lambda h: 2 * h, kv, lambda h: 2 * h, kv, lambda h: 2 * h + 1, QKA ** -0.5, "mla_fwd",
            q2=q_pe, q2cb=lambda h: h, k2=k_pe, k2cb=lambda h: 0, comm=("gather", [sh["w_down"][l]]))
        W["wd"] = g_down.reshape(NDEV * FB, D)
        qk_b = A["qk_b"] = _rope(proj, (cb(lay.QB), 2 * H, 1), *rope_full, False, BF16, "rope_qk_dil")
        A["o_b"], A["lse_b"] = _attn_fwd(
            S, H, qk_b, lambda h: h, qk_b, lambda h: H + h, pb, lambda h: cb(lay.VB) + h, HEAD ** -0.5,
            "dilated_fwd", tab=tab, win=win)
        ccol, crow = A["ccol"], A["crow"] = _fox_prep(S, H, proj, cb(lay.FC), A["bias"], "fox_prep")
        (A["o_c"], A["lse_c"]), (g_out,) = _attn_fwd(
            S, H, pb, lambda h: cb(lay.QC) + h, pb, lambda h: cb(lay.KC) + h, pb, lambda h: cb(lay.VC) + h,
            HEAD ** -0.5, "fox_fwd", ccol=ccol, crow=crow, comm=("gather", [sh["w_out"][l]]))
        W["wout"] = g_out.reshape(4 * GW, D)
        (A["o_d"], A["tot_d"]), (W["wg"], W["wu"]) = _stick_fwd(
            S, H, pb, lambda h: cb(lay.QD) + h, lambda h: cb(lay.KD) + h, lambda h: cb(lay.VD) + h,
            HEAD ** -0.5, "stick_fwd", comm=("gather", [sh["w_gate"][l], sh["w_up"][l]]))
        mix = A["mix"] = _gn_fwd([A["o_a"], A["o_b"], A["o_c"], A["o_d"]], row(group_norm[l]), "group_norm")
        x1 = A["x1"] = _mm(mix, W["wout"], "out_proj", res=x0)
        h2 = A["h2"] = _rms_fwd(x1, row(ffn_norm[l]), D, 0, BF16, "ffn_norm")
        nxt = None
        if l + 1 < L:
            (A["g"], A["u"], A["act"]), nxt = _ffn_up(
                h2, W["wg"], W["wu"], "ffn_up", comm=("gather", [sh[n][l + 1] for n in first3]))
        else:
            A["g"], A["u"], A["act"] = _ffn_up(h2, W["wg"], W["wu"], "ffn_up_last")
        return _mm_down(A["act"], W["wd"], x1, "ffn_down"), A, nxt

    def backward(l, dx2, W, A, late):
        proj, pb = A["proj"], A["pb"]
        G, small, got = {}, {}, {}
        dgate, dup = _ffn_dact(dx2, W["wd"], A["g"], A["u"], "ffn_dact")
        G["w_down"] = _mm_dwdown(A["act"], dx2, "dw_down").reshape(NDEV, FB, D)
        if late is None:
            dh2 = _mm_dh2(dgate, W["wg"], dup, W["wu"], "ffn_dh_first")
        else:
            dh2, moved = _mm_dh2(dgate, W["wg"], dup, W["wu"], "ffn_dh", comm=("exchange", late))
            got.update({(l + 1, n): s for n, s in zip(first3, moved)})
        G["w_gate"] = _mm_dwgate(A["h2"], dgate, "dw_gate")
        G["w_up"] = _mm_dwgate(A["h2"], dup, "dw_up")
        dx1, small["ffn_norm"] = _rms_bwd(A["x1"], row(ffn_norm[l]), dh2, D, 0, "ffn_norm_bwd", res=dx2)
        dmix = _mm(dx1, W["wout"], "out_proj_dx", tb=True)
        G["w_out"] = _mm(A["mix"], dx1, "dw_out", ta=True, out_dtype=BF16).reshape(NDEV, 4 * GW // NDEV, D)
        do_a, do_b, do_c, do_d, small["group_norm"] = _gn_bwd(
            [A["o_a"], A["o_b"], A["o_c"], A["o_d"]], row(group_norm[l]), dmix, "group_norm_bwd")
        (dq_d, dk_d, dv_d), (got[l, "w_gate"], got[l, "w_up"]) = _stick_bwd(
            S, H, pb, lambda h: cb(lay.QD) + h, lambda h: cb(lay.KD) + h, lambda h: cb(lay.VD) + h,
            do_d, A["tot_d"], HEAD ** -0.5, "stick_bwd", comm=("exchange", [G["w_gate"], G["w_up"]]))
        (dq_c, dk_c, dv_c, dcc), (got[l, "w_down"],) = _attn_bwd(
            S, H, pb, lambda h: cb(lay.QC) + h, pb, lambda h: cb(lay.KC) + h, pb, lambda h: cb(lay.VC) + h,
            A["o_c"], do_c, A["lse_c"], HEAD ** -0.5, "fox_bwd", ccol=A["ccol"], crow=A["crow"],
            comm=("exchange", [G["w_down"]]))
        dfc, dbias = _fox_bwd(S, H, proj, cb(lay.FC), A["bias"], dcc, "fox_gate_bwd")
        small["fox_forget_bias"] = dbias[0, :H]
        qk_b = A["qk_b"]
        dq_b, dk_b, dv_b = _attn_bwd(
            S, H, qk_b, lambda h: h, qk_b, lambda h: H + h, pb, lambda h: cb(lay.VB) + h,
            A["o_b"], do_b, A["lse_b"], HEAD ** -0.5, "dilated_bwd", tab=tab, win=win)
        dqk_b = _rope(jnp.concatenate([dq_b, dk_b], 1), (0, 2 * H, 1), *neg(rope_full), False, BF16, "rope_qk_dil_bwd")
        qab, kv = A["qab"], A["kv"]
        (dq1, dk1, dv_a, dq2, dk2), (got[l, "w_out"],) = _attn_bwd(
            S, H, qab, lambda h: 2 * h, kv, lambda h: 2 * h, kv, lambda h: 2 * h + 1,
            A["o_a"], do_a, A["lse_a"], QKA ** -0.5, "mla_bwd",
            q2=A["q_pe"], q2cb=lambda h: h, k2=A["k_pe"], k2cb=lambda h: 0, comm=("exchange", [G["w_out"]]))
        dq2 = _rope(dq2, (0, H, 1), *neg(rope_half), True, F32, "rope_q_mla_bwd")
        dk_pe = _rope(dk2.reshape(S, H, LANE).sum(1), (0, 1, 1), *neg(rope_half), True, BF16, "rope_k_mla_bwd")
        dqa = jnp.stack([dq1.reshape(S, H, LANE), dq2.reshape(S, H, LANE)], 2).reshape(S, H * 2 * LANE).astype(BF16)
        dkv = jnp.stack([dk1.reshape(S, H, LANE), dv_a.reshape(S, H, LANE)], 2).reshape(S, H * 2 * LANE).astype(BF16)
        dwuq = _mm(A["qln"], dqa, "dw_uq", ta=True, out_dtype=BF16)
        dwuq = dwuq.reshape(Q_LORA, H, 2 * LANE)[:, :, :QKA].reshape(Q_LORA, NDEV, H * QKA // NDEV)
        G["w_uq"] = jnp.transpose(dwuq, (1, 0, 2))
        dwukv = _mm(A["kvln"], dkv, "dw_ukv", ta=True, out_dtype=BF16).reshape(KV_LORA, NDEV, H * 2 * LANE // NDEV)
        G["w_ukv"] = jnp.transpose(dwukv, (1, 0, 2))
        dqln = _mm(dqa, W["wuq"], "q_up_dx", tb=True)
        dkvln = _mm(dkv, W["wukv"], "kv_up_dx", tb=True)
        dql, small["mla_q_norm"] = _rms_bwd(proj, row(mla_q_norm[l]), dqln, Q_LORA, cb(lay.QL) // 4, "q_norm_bwd")
        dkvl, small["mla_kv_norm"] = _rms_bwd(proj, row(mla_kv_norm[l]), dkvln, KV_LORA, cb(lay.KVL) // 4, "kv_norm_bwd")
        bf = lambda t: t.astype(BF16)
        dproj = jnp.concatenate([
            bf(dql), bf(dkvl), dqk_b, bf(dv_b), bf(dq_c), bf(dk_c), bf(dv_c), bf(dq_d), bf(dk_d), bf(dv_d),
            dk_pe, bf(dfc), jnp.zeros((S, PW - lay.FC - LANE), BF16)], axis=1)
        G["w_in"] = _mm(A["h1"], dproj, "dw_in", ta=True, out_dtype=BF16).reshape(NDEV, D // NDEV, PW)
        dh1 = _mm(dproj, W["win"], "in_proj_dx", tb=True)
        dx0, small["attn_norm"] = _rms_bwd(A["x0"], row(attn_norm[l]), dh1, D, 0, "attn_norm_bwd", res=dx1)
        return dx0, [G[n] for n in first3], got, small

    big = first3 + ["w_out", "w_gate", "w_up", "w_down"]
    Ws, As = [], []
    xc = x
    nxt = _all_gather([sh[n][0] for n in first3], name="gather_first")
    for l in range(L):
        W = first_weights(nxt)
        xc, A, nxt = forward(l, xc, W)
        Ws.append(W)
        As.append(A)
    dx, loss_part = _final_loss(xc, row(final_norm), target, "final_loss")
    dx, dfinal = _rms_bwd(xc, row(final_norm), dx, D, 0, "final_norm_bwd")
    slots = {}
    smalls = [None] * L
    late = None
    for l in reversed(range(L)):
        dx, late, got, smalls[l] = backward(l, dx, Ws[l], As[l], late)
        slots.update(got)
    slots.update({(0, n): s for n, s in zip(first3, _all_to_all(late, name="exchange_last"))})

    names_small = ["attn_norm", "mla_q_norm", "mla_kv_norm", "fox_forget_bias", "group_norm", "ffn_norm"]
    params = dict(attn_norm=attn_norm, mla_q_norm=mla_q_norm, mla_kv_norm=mla_kv_norm, fox_forget_bias=fox_forget_bias,
                  group_norm=group_norm, ffn_norm=ffn_norm, final_norm=final_norm, w_in=w_in, w_uq=w_uq, w_ukv=w_ukv,
                  w_out=w_out, w_gate=w_gate, w_up=w_up, w_down=w_down)
    moms = dict(attn_norm=(m_attn_norm, v_attn_norm), mla_q_norm=(m_mla_q_norm, v_mla_q_norm),
                mla_kv_norm=(m_mla_kv_norm, v_mla_kv_norm), fox_forget_bias=(m_fox_forget_bias, v_fox_forget_bias),
                group_norm=(m_group_norm, v_group_norm), ffn_norm=(m_ffn_norm, v_ffn_norm),
                final_norm=(m_final_norm, v_final_norm), w_in=(m_w_in, v_w_in), w_uq=(m_w_uq, v_w_uq),
                w_ukv=(m_w_ukv, v_w_ukv), w_out=(m_w_out, v_w_out), w_gate=(m_w_gate, v_w_gate),
                w_up=(m_w_up, v_w_up), w_down=(m_w_down, v_w_down))
    small_list = names_small + ["final_norm"]
    small_grads = [jnp.stack([smalls[l][n].reshape(params[n].shape[1:]) for l in range(L)]) for n in names_small]
    small_grads.append(dfinal.reshape(final_norm.shape))
    shapes = [params[n].shape for n in small_list] + [(LANE,)]
    packed_g = _all_gather([_pack(small_grads + [loss_part.reshape(LANE)])], name="gather_small")[0]
    zero = jnp.zeros((LANE,), F32)
    res_small = _adamw(_pack([params[n] for n in small_list] + [zero]), packed_g,
                       _pack([moms[n][0] for n in small_list] + [zero]),
                       _pack([moms[n][1] for n in small_list] + [zero]), "adamw_small")
    unp = [_unpack(r, shapes) for r in res_small]
    out = {n: tuple(unp[k][i] for k in range(4)) for i, n in enumerate(small_list)}
    loss = unp[0][-1][0]

    for i, n in enumerate(big):
        st = jnp.stack([slots[l, n] for l in range(L)], axis=1)
        if n == "w_in":
            st = lay.unpad(st)
        C = st.shape[-1]
        st = st.reshape(NDEV, -1, C)
        w2 = params[n].reshape(-1, C)
        res = _adamw(w2, st, moms[n][0].reshape(-1, C), moms[n][1].reshape(-1, C), "adamw_" + n)
        out[n] = tuple(r.reshape(params[n].shape) for r in res)

    order = ["attn_norm", "w_in", "mla_q_norm", "w_uq", "mla_kv_norm", "w_ukv", "fox_forget_bias", "group_norm",
             "w_out", "ffn_norm", "w_gate", "w_up", "w_down", "final_norm"]
    return (loss, dx[None], *[out[n][0] for n in order], *[out[n][1] for n in order],
            *[out[n][2] for n in order], *[out[n][3] for n in order])
```

```python
import functools
import math

import numpy as np
import jax
import jax.numpy as jnp
from jax import lax
from jax.experimental import pallas as pl
from jax.experimental.pallas import tpu as pltpu

F32 = jnp.float32
BF16 = jnp.bfloat16
NDEV = 8
LANE = 128
HEAD = 128
Q_LORA = 512
KV_LORA = 512
QK_ROPE = 64
DILATED_PAIRS = ((128, 1), (512, 4), (2048, 16))
ROPE_THETA = 10000.0
EPS = 1e-6
NEG = -1e30
TQ = 256
TK = 128
TKS = 512
VMEM_LIMIT = 48 * 1024 * 1024
ADAM_LR, ADAM_B1, ADAM_B2, ADAM_EPS, ADAM_WD, ADAM_STEP = 0.001, 0.9, 0.999, 1e-08, 0.01, 10
MESH = pl.DeviceIdType.MESH
ANY = pl.BlockSpec(memory_space=pl.ANY)


def _cp(*sem):
    return pltpu.CompilerParams(dimension_semantics=sem, vmem_limit_bytes=VMEM_LIMIT)


def _dot(a, b, ca, cb):
    return lax.dot_general(a, b, (((ca,), (cb,)), ((), ())), preferred_element_type=F32)


def _dot_nn(a, b):
    return _dot(a, b, 1, 0)


def _dot_nt(a, b):
    return _dot(a, b, 1, 1)


def _dot_tn(a, b):
    return _dot(a, b, 0, 0)


def _tile(n, t):
    if n <= t:
        return n
    t -= t % LANE
    while n % t:
        t -= LANE
    return t


def _direct_copies(ins, outs, send_sems, recv_sems, local_sems, want_recvs=True):
    x, y, c = lax.axis_index("x"), lax.axis_index("y"), lax.axis_index("c")
    my_id = 4 * x + 2 * y + c
    local, sends, recvs = [], [], []
    for a in range(len(ins)):
        mine = ins[a].at[my_id]
        local.append(pltpu.make_async_copy(mine, outs[a].at[my_id], local_sems.at[a]))
        for k in range(1, NDEV):
            peer = (1 - x if k & 4 else x, 1 - y if k & 2 else y, 1 - c if k & 1 else c)
            pid = 4 * peer[0] + 2 * peer[1] + peer[2]
            sems = dict(send_sem=send_sems.at[a, k - 1], recv_sem=recv_sems.at[a, k - 1],
                        device_id=peer, device_id_type=MESH)
            sends.append(pltpu.make_async_remote_copy(src_ref=ins[a].at[pid], dst_ref=outs[a].at[my_id], **sems))
            if want_recvs:
                recvs.append(pltpu.make_async_remote_copy(src_ref=mine, dst_ref=outs[a].at[pid], **sems))
    return local, sends, recvs


def _comm_start(kind, ins, outs, send_sems, recv_sems, local_sems):
    if kind == "exchange":
        local, sends, _ = _direct_copies(ins, outs, send_sems, recv_sems, local_sems, want_recvs=False)
        for cp in local + sends:
            cp.start()
        return
    x, y, c = lax.axis_index("x"), lax.axis_index("y"), lax.axis_index("c")
    for a in range(len(ins)):
        mine = outs[a].at[4 * x + 2 * y + c]
        pltpu.make_async_copy(ins[a], mine, local_sems.at[a]).start()
        for k, to in enumerate([(x, y, 1 - c), (1 - x, y, c), (x, 1 - y, c), (1 - x, 1 - y, c)]):
            pltpu.make_async_remote_copy(src_ref=ins[a], dst_ref=mine, send_sem=send_sems.at[a, k],
                                         recv_sem=recv_sems.at[a, k], device_id=to, device_id_type=MESH).start()


def _comm_finish(kind, ins, outs, send_sems, recv_sems, local_sems):
    if kind == "exchange":
        local, sends, recvs = _direct_copies(ins, outs, send_sems, recv_sems, local_sems)
        for cp in recvs:
            cp.wait_recv()
        for cp in sends:
            cp.wait_send()
        for cp in local:
            cp.wait()
        return
    x, y, c = lax.axis_index("x"), lax.axis_index("y"), lax.axis_index("c")
    sibling = (x, y, 1 - c)
    chips = [(1 - x, y), (x, 1 - y), (1 - x, 1 - y)]
    for a in range(len(ins)):
        def copy(k, block, to):
            rows = outs[a].at[4 * block[0] + 2 * block[1] + block[2]]
            return pltpu.make_async_remote_copy(src_ref=rows, dst_ref=rows, send_sem=send_sems.at[a, k],
                                                recv_sem=recv_sems.at[a, k], device_id=to, device_id_type=MESH)

        passed = []
        for j, chip in enumerate(chips):
            copy(1 + j, (*chip, c), (x, y, c)).wait_recv()
            passed.append(copy(4 + j, (*chip, c), sibling))
            passed[-1].start()
        copy(0, sibling, (x, y, c)).wait_recv()
        for j, chip in enumerate(chips):
            copy(4 + j, (*chip, 1 - c), (x, y, c)).wait_recv()
        for k in range(4):
            copy(k, (x, y, c), sibling).wait_send()
        for cp in passed:
            cp.wait_send()
        pltpu.make_async_copy(ins[a], outs[a].at[4 * x + 2 * y + c], local_sems.at[a]).wait()


def _comm_shapes(kind, arrs):
    out_shape = [jax.ShapeDtypeStruct(((NDEV,) if kind == "gather" else ()) + a.shape, a.dtype) for a in arrs]
    n = len(arrs)
    sems = [pltpu.SemaphoreType.DMA((n, 7)), pltpu.SemaphoreType.DMA((n, 7)), pltpu.SemaphoreType.DMA((n,))]
    return out_shape, sems


def _comm_alone(kind, arrs, name):
    n = len(arrs)

    def body(*refs):
        _comm_start(kind, refs[:n], refs[n:2 * n], *refs[2 * n:])
        _comm_finish(kind, refs[:n], refs[n:2 * n], *refs[2 * n:])

    out_shape, sems = _comm_shapes(kind, arrs)
    return pl.pallas_call(body, name=name, out_shape=out_shape, in_specs=[ANY] * n, out_specs=[ANY] * n,
                          scratch_shapes=sems)(*arrs)


def _pcall(body, *, name, grid, in_specs, out_specs, out_shape, operands, sem, scratch_shapes=(), comm=None):
    in_specs, out_specs, out_shape = list(in_specs), list(out_specs), list(out_shape)
    scratch_shapes = list(scratch_shapes)
    if comm is None:
        res = pl.pallas_call(body, name=name, grid=grid, in_specs=in_specs, out_specs=out_specs, out_shape=out_shape,
                             scratch_shapes=scratch_shapes, compiler_params=_cp(*sem))(*operands)
        return list(res), []
    kind, arrs = comm
    nc, n_in, n_out, n_scr = len(arrs), len(operands), len(out_shape), len(scratch_shapes)
    c_shape, c_sems = _comm_shapes(kind, arrs)

    def carrier(*refs):
        ins, cin = refs[:n_in], refs[n_in:n_in + nc]
        outs = refs[n_in + nc:n_in + nc + n_out]
        cout = refs[n_in + nc + n_out:n_in + 2 * nc + n_out]
        scr = refs[n_in + 2 * nc + n_out:n_in + 2 * nc + n_out + n_scr]
        sems = refs[n_in + 2 * nc + n_out + n_scr:]
        pids = [pl.program_id(d) for d in range(len(grid))]
        first = functools.reduce(jnp.logical_and, [p == 0 for p in pids])
        last = functools.reduce(jnp.logical_and, [p == g - 1 for p, g in zip(pids, grid)])

        @pl.when(first)
        def _():
            _comm_start(kind, cin, cout, *sems)

        body(*ins, *outs, *scr)

        @pl.when(last)
        def _():
            _comm_finish(kind, cin, cout, *sems)

    res = pl.pallas_call(
        carrier, name=name, grid=grid, in_specs=in_specs + [ANY] * nc, out_specs=out_specs + [ANY] * nc,
        out_shape=out_shape + c_shape, scratch_shapes=scratch_shapes + c_sems,
        compiler_params=_cp(*["arbitrary"] * len(grid)))(*operands, *arrs)
    return list(res[:n_out]), list(res[n_out:])


def _mm_call(pairs, grid, a_spec, b_spec, o_spec, out_shape, acc_shape, nk, ca, cb, name,
             res=None, res_spec=None, comm=None):
    npairs = len(pairs)
    multi = isinstance(out_shape, (list, tuple))
    nout = len(out_shape) if multi else 1

    def body(*refs):
        ab = refs[:2 * npairs]
        r_ref = refs[2 * npairs] if res is not None else None
        o_refs, acc = refs[-1 - nout:-1], refs[-1]
        k = pl.program_id(2)

        @pl.when(k == 0)
        def _():
            acc[...] = jnp.zeros_like(acc)

        tot = None
        for p in range(npairs):
            d = _dot(ab[2 * p][...].astype(BF16), ab[2 * p + 1][...].astype(BF16), ca, cb)
            tot = d if tot is None else tot + d
        acc[...] += tot

        @pl.when(k == nk - 1)
        def _():
            r = acc[...]
            if r_ref is not None:
                r = r + r_ref[...]
            for o_ref in o_refs:
                o_ref[...] = r.astype(o_ref.dtype)

    ops, specs = [], []
    for a, b in pairs:
        ops += [a, b]
        specs += [a_spec, b_spec]
    if res is not None:
        ops.append(res)
        specs.append(res_spec)
    outs, moved = _pcall(
        body, name=name, grid=grid, in_specs=specs, out_specs=[o_spec] * nout,
        out_shape=out_shape if multi else [out_shape], operands=ops,
        scratch_shapes=[pltpu.VMEM(acc_shape, F32)], sem=("parallel", "parallel", "arbitrary"), comm=comm)
    outs = outs if multi else outs[0]
    return outs if comm is None else (outs, moved)


def _mm(a, b, name, ta=False, tb=False, out_dtype=F32, res=None, tm=1024, tn=1024, tk=512, comm=None):
    M, K = (a.shape[1], a.shape[0]) if ta else a.shape
    N = b.shape[0] if tb else b.shape[1]
    tm, tn, tk = _tile(M, tm), _tile(N, tn), _tile(K, tk)
    a_spec = pl.BlockSpec((tk, tm), lambda i, j, k: (k, i)) if ta else pl.BlockSpec((tm, tk), lambda i, j, k: (i, k))
    b_spec = pl.BlockSpec((tn, tk), lambda i, j, k: (j, k)) if tb else pl.BlockSpec((tk, tn), lambda i, j, k: (k, j))
    o_spec = pl.BlockSpec((tm, tn), lambda i, j, k: (i, j))
    if isinstance(out_dtype, tuple):
        out_shape = [jax.ShapeDtypeStruct((M, N), d) for d in out_dtype]
    else:
        out_shape = jax.ShapeDtypeStruct((M, N), out_dtype)
    return _mm_call([(a, b)], (M // tm, N // tn, K // tk), a_spec, b_spec, o_spec,
                    out_shape, (tm, tn), K // tk,
                    0 if ta else 1, 1 if tb else 0, name, res=res, res_spec=o_spec, comm=comm)


def _mm_down(act, wd, res, name, tm=1024, tn=1024):
    _, S, FB = act.shape
    D = wd.shape[1]
    tm, tn = _tile(S, tm), _tile(D, tn)
    o_spec = pl.BlockSpec((tm, tn), lambda i, j, k: (i, j))
    return _mm_call([(act, wd)], (S // tm, D // tn, NDEV),
                    pl.BlockSpec((None, tm, FB), lambda i, j, k: (k, i, 0)),
                    pl.BlockSpec((FB, tn), lambda i, j, k: (k, j)), o_spec,
                    jax.ShapeDtypeStruct((S, D), F32), (tm, tn), NDEV, 1, 0, name, res=res, res_spec=o_spec)


def _mm_dwdown(act, dy, name, tn=1024, tk=512):
    _, S, FB = act.shape
    D = dy.shape[1]
    tn, tk = _tile(D, tn), _tile(S, tk)
    return _mm_call([(act, dy)], (NDEV, D // tn, S // tk),
                    pl.BlockSpec((None, tk, FB), lambda i, j, k: (i, k, 0)),
                    pl.BlockSpec((tk, tn), lambda i, j, k: (k, j)),
                    pl.BlockSpec((FB, tn), lambda i, j, k: (i, j)),
                    jax.ShapeDtypeStruct((NDEV * FB, D), BF16), (FB, tn), S // tk, 0, 0, name)


def _mm_dh2(dg, wg, du, wu, name, tm=1024, tn=1024, comm=None):
    _, S, FB = dg.shape
    D = wg.shape[1]
    tm, tn = _tile(S, tm), _tile(D, tn)
    return _mm_call([(dg, wg), (du, wu)], (S // tm, D // tn, NDEV),
                    pl.BlockSpec((None, tm, FB), lambda i, j, k: (k, i, 0)),
                    pl.BlockSpec((None, tn, FB), lambda i, j, k: (k, j, 0)),
                    pl.BlockSpec((tm, tn), lambda i, j, k: (i, j)),
                    jax.ShapeDtypeStruct((S, D), F32), (tm, tn), NDEV, 1, 1, name, comm=comm)


def _mm_dwgate(h2, dg, name, tm=1024, tk=512):
    _, S, FB = dg.shape
    D = h2.shape[1]
    tm, tk = _tile(D, tm), _tile(S, tk)
    return _mm_call([(h2, dg)], (NDEV, D // tm, S // tk),
                    pl.BlockSpec((tk, tm), lambda p, i, k: (k, i)),
                    pl.BlockSpec((None, tk, FB), lambda p, i, k: (p, k, 0)),
                    pl.BlockSpec((None, tm, FB), lambda p, i, k: (p, i, 0)),
                    jax.ShapeDtypeStruct((NDEV, D, FB), BF16), (tm, FB), S // tk, 0, 0, name)


def _ffn_up(h2, wg, wu, name, tm=512, comm=None):
    S, D = h2.shape
    FB = wg.shape[2]
    tm = _tile(S, tm)

    def body(h_ref, wg_ref, wu_ref, g_ref, u_ref, act_ref):
        h = h_ref[...]
        g = _dot_nn(h, wg_ref[...])
        u = _dot_nn(h, wu_ref[...])
        g_ref[...] = g
        u_ref[...] = u
        act_ref[...] = (g / (1.0 + jnp.exp(-g)) * u).astype(BF16)

    w_spec = pl.BlockSpec((None, D, FB), lambda p, i: (p, 0, 0))
    o_spec = pl.BlockSpec((None, tm, FB), lambda p, i: (p, i, 0))
    shp = (NDEV, S, FB)
    outs, moved = _pcall(
        body, name=name, grid=(NDEV, S // tm),
        in_specs=[pl.BlockSpec((tm, D), lambda p, i: (i, 0)), w_spec, w_spec],
        out_specs=[o_spec, o_spec, o_spec],
        out_shape=[jax.ShapeDtypeStruct(shp, F32), jax.ShapeDtypeStruct(shp, F32), jax.ShapeDtypeStruct(shp, BF16)],
        operands=[h2, wg, wu], sem=("parallel", "parallel"), comm=comm)
    return outs if comm is None else (outs, moved)


def _ffn_dact(dy, wd, g, u, name, tm=512):
    S, D = dy.shape
    FB = g.shape[2]
    tm = _tile(S, tm)

    def body(dy_ref, wd_ref, g_ref, u_ref, dg_ref, du_ref):
        dact = _dot_nt(dy_ref[...].astype(BF16), wd_ref[...])
        gv = g_ref[...]
        sg = 1.0 / (1.0 + jnp.exp(-gv))
        dg_ref[...] = (dact * u_ref[...] * (sg * (1.0 + gv * (1.0 - sg)))).astype(BF16)
        du_ref[...] = (dact * (gv * sg)).astype(BF16)

    t_spec = pl.BlockSpec((None, tm, FB), lambda p, i: (p, i, 0))
    shp = jax.ShapeDtypeStruct((NDEV, S, FB), BF16)
    return pl.pallas_call(
        body, name=name, grid=(NDEV, S // tm),
        in_specs=[pl.BlockSpec((tm, D), lambda p, i: (i, 0)), pl.BlockSpec((FB, D), lambda p, i: (p, 0)),
                  t_spec, t_spec],
        out_specs=[t_spec, t_spec], out_shape=[shp, shp],
        compiler_params=_cp("parallel", "parallel"),
    )(dy, wd, g, u)


def _rms_fwd(x, gain, width, cb, out_dtype, name, ts=512):
    S = x.shape[0]
    ts = _tile(S, ts)

    def body(x_ref, g_ref, o_ref):
        xv = x_ref[...]
        r = lax.rsqrt(jnp.mean(xv * xv, axis=1, keepdims=True) + EPS)
        o_ref[...] = (xv * r * g_ref[...]).astype(o_ref.dtype)

    return pl.pallas_call(
        body, name=name, grid=(S // ts,),
        in_specs=[pl.BlockSpec((ts, width), lambda i: (i, cb)), pl.BlockSpec((1, width), lambda i: (0, 0))],
        out_specs=pl.BlockSpec((ts, width), lambda i: (i, 0)),
        out_shape=jax.ShapeDtypeStruct((S, width), out_dtype),
        compiler_params=_cp("parallel"),
    )(x, gain)


def _rms_bwd(x, gain, dy, width, cb, name, res=None, ts=256):
    S = x.shape[0]
    ts = _tile(S, ts)
    has_res = res is not None

    def body(*refs):
        x_ref, g_ref, dy_ref = refs[:3]
        r_ref = refs[3] if has_res else None
        dx_ref, dg_ref = refs[-2], refs[-1]

        @pl.when(pl.program_id(0) == 0)
        def _():
            dg_ref[...] = jnp.zeros_like(dg_ref)

        xv = x_ref[...]
        r = lax.rsqrt(jnp.mean(xv * xv, axis=1, keepdims=True) + EPS)
        xh = xv * r
        dyv = dy_ref[...]
        dyg = dyv * g_ref[...]
        dx = r * (dyg - xh * jnp.mean(dyg * xh, axis=1, keepdims=True))
        if has_res:
            dx = dx + r_ref[...]
        dx_ref[...] = dx
        dg_ref[...] += jnp.sum(dyv * xh, axis=0, keepdims=True)

    row = pl.BlockSpec((ts, width), lambda i: (i, 0))
    vec = pl.BlockSpec((1, width), lambda i: (0, 0))
    ops = [x, gain, dy] + ([res] if has_res else [])
    specs = [pl.BlockSpec((ts, width), lambda i: (i, cb)), vec, row] + ([row] if has_res else [])
    return pl.pallas_call(
        body, name=name, grid=(S // ts,), in_specs=specs, out_specs=[row, vec],
        out_shape=[jax.ShapeDtypeStruct((S, width), F32), jax.ShapeDtypeStruct((1, width), F32)],
        compiler_params=_cp("arbitrary"),
    )(*ops)


def _gn_fwd(outs, gain, name, ts=512):
    S, GW = outs[0].shape
    ts = _tile(S, ts)

    def body(a_ref, b_ref, c_ref, d_ref, g_ref, o_ref):
        for g, r_ref in enumerate((a_ref, b_ref, c_ref, d_ref)):
            xv = r_ref[...]
            r = lax.rsqrt(jnp.mean(xv * xv, axis=1, keepdims=True) + EPS)
            o_ref[:, g * GW:(g + 1) * GW] = (xv * r * g_ref[:, g * GW:(g + 1) * GW]).astype(BF16)

    row = pl.BlockSpec((ts, GW), lambda i: (i, 0))
    return pl.pallas_call(
        body, name=name, grid=(S // ts,),
        in_specs=[row] * 4 + [pl.BlockSpec((1, 4 * GW), lambda i: (0, 0))],
        out_specs=pl.BlockSpec((ts, 4 * GW), lambda i: (i, 0)),
        out_shape=jax.ShapeDtypeStruct((S, 4 * GW), BF16),
        compiler_params=_cp("parallel"),
    )(*outs, gain)


def _gn_bwd(outs, gain, dmix, name, ts=256):
    S, GW = outs[0].shape
    ts = _tile(S, ts)

    def body(a_ref, b_ref, c_ref, d_ref, g_ref, dm_ref, da_ref, db_ref, dc_ref, dd_ref, dg_ref):
        @pl.when(pl.program_id(0) == 0)
        def _():
            dg_ref[...] = jnp.zeros_like(dg_ref)

        for g, (r_ref, o_ref) in enumerate(zip((a_ref, b_ref, c_ref, d_ref), (da_ref, db_ref, dc_ref, dd_ref))):
            sl = slice(g * GW, (g + 1) * GW)
            xv = r_ref[...]
            r = lax.rsqrt(jnp.mean(xv * xv, axis=1, keepdims=True) + EPS)
            xh = xv * r
            dyv = dm_ref[:, sl]
            dyg = dyv * g_ref[:, sl]
            o_ref[...] = (r * (dyg - xh * jnp.mean(dyg * xh, axis=1, keepdims=True))).astype(BF16)
            dg_ref[:, sl] += jnp.sum(dyv * xh, axis=0, keepdims=True)

    row = pl.BlockSpec((ts, GW), lambda i: (i, 0))
    vec = pl.BlockSpec((1, 4 * GW), lambda i: (0, 0))
    return pl.pallas_call(
        body, name=name, grid=(S // ts,),
        in_specs=[row] * 4 + [vec, pl.BlockSpec((ts, 4 * GW), lambda i: (i, 0))],
        out_specs=[row] * 4 + [vec],
        out_shape=[jax.ShapeDtypeStruct((S, GW), BF16)] * 4 + [jax.ShapeDtypeStruct((1, 4 * GW), F32)],
        compiler_params=_cp("arbitrary"),
    )(*outs, gain, dmix)


def _rope(x, cbs, cos, sin, half, out_dtype, name, ts=512):
    S = x.shape[0]
    cb0, nb, stride = cbs
    ts = _tile(S, ts)

    def body(x_ref, c_ref, s_ref, o_ref):
        xv = x_ref[...].astype(F32)
        if half:
            lane = lax.broadcasted_iota(jnp.int32, xv.shape, 1)
            partner = jnp.where(lane % 64 < 32, pltpu.roll(xv, LANE - 32, 1), pltpu.roll(xv, 32, 1))
        else:
            partner = pltpu.roll(xv, 64, 1)
        o_ref[...] = (xv * c_ref[...] + partner * s_ref[...]).astype(o_ref.dtype)

    tab = pl.BlockSpec((ts, LANE), lambda i, j: (i, 0))
    return pl.pallas_call(
        body, name=name, grid=(S // ts, nb),
        in_specs=[pl.BlockSpec((ts, LANE), lambda i, j: (i, cb0 + stride * j)), tab, tab],
        out_specs=pl.BlockSpec((ts, LANE), lambda i, j: (i, j)),
        out_shape=jax.ShapeDtypeStruct((S, nb * LANE), out_dtype),
        compiler_params=_cp("parallel", "parallel"),
    )(x, cos, sin)


def _final_loss(x, gain, target, name, ts=256):
    S, D = x.shape
    ts = _tile(S, ts)

    def body(x_ref, g_ref, t_ref, dy_ref, l_ref):
        @pl.when(pl.program_id(0) == 0)
        def _():
            l_ref[...] = jnp.zeros_like(l_ref)

        xv = x_ref[...]
        r = lax.rsqrt(jnp.mean(xv * xv, axis=1, keepdims=True) + EPS)
        err = xv * r * g_ref[...] - t_ref[...]
        dy_ref[...] = err * (1.0 / D)
        part = jnp.sum(jnp.mean(err * err, axis=1, keepdims=True), axis=0, keepdims=True)
        l_ref[...] += jnp.broadcast_to(0.5 * part, (1, LANE))

    row = pl.BlockSpec((ts, D), lambda i: (i, 0))
    return pl.pallas_call(
        body, name=name, grid=(S // ts,),
        in_specs=[row, pl.BlockSpec((1, D), lambda i: (0, 0)), row],
        out_specs=[row, pl.BlockSpec((1, LANE), lambda i: (0, 0))],
        out_shape=[jax.ShapeDtypeStruct((S, D), F32), jax.ShapeDtypeStruct((1, LANE), F32)],
        compiler_params=_cp("arbitrary"),
    )(x, gain, target)


def _colspec(rows, f):
    return pl.BlockSpec((rows, LANE), f)


def _soft_tiles(S):
    tq = _tile(S, TQ)
    tk = _tile(S, TKS)
    assert tk % tq == 0
    return tq, tk


def _key_row(crow_ref, j, tk):
    n = tk // TK
    return jnp.concatenate([crow_ref[j * n + c] for c in range(n)], axis=1)


def _attn_fwd(S, H, q1, q1cb, k1, k1cb, v, vcb, scale, name, q2=None, q2cb=None, k2=None, k2cb=None,
              tab=None, win=None, ccol=None, crow=None, comm=None):
    tq, tk = _soft_tiles(S)
    has2, hastab, hasc = q2 is not None, tab is not None, ccol is not None

    def body(*refs):
        it = iter(refs)
        q1r, k1r, vr = next(it), next(it), next(it)
        q2r, k2r = (next(it), next(it)) if has2 else (None, None)
        tabr = next(it) if hastab else None
        ccolr, crowr = (next(it), next(it)) if hasc else (None, None)
        o_ref, lse_ref = next(it), next(it)
        i = pl.program_id(1)
        q = q1r[...]
        qb2 = q2r[...] if has2 else None
        cq = ccolr[:, 0:1] if hasc else None
        qpos = i * tq + lax.broadcasted_iota(jnp.int32, (tq, tk), 0)
        kio = lax.broadcasted_iota(jnp.int32, (tq, tk), 1)
        j_diag = (i * tq) // tk
        j_lo = jnp.maximum((i * tq - win) // tk, 0) if win else 0

        def step(j, carry, masked):
            m, l, acc = carry
            off = pl.multiple_of(j * tk, tk)
            s = _dot_nt(q, k1r[pl.ds(off, tk), :])
            if has2:
                s = s + _dot_nt(qb2, k2r[pl.ds(off, tk), :])
            s = s * scale
            if hastab:
                s = s + tabr[i - j * (tk // tq)]
            else:
                if hasc:
                    s = s + (cq - _key_row(crowr, j, tk))
                if masked:
                    s = jnp.where(kio + j * tk <= qpos, s, NEG)
            mn = jnp.maximum(m, jnp.max(s, axis=1, keepdims=True))
            p = jnp.exp(s - mn)
            al = jnp.exp(m - mn)
            l = al * l + jnp.sum(p, axis=1, keepdims=True)
            acc = al * acc + _dot_nn(p.astype(BF16), vr[pl.ds(off, tk), :])
            return mn, l, acc

        carry = (jnp.full((tq, 1), NEG, F32), jnp.zeros((tq, 1), F32), jnp.zeros((tq, LANE), F32))
        if hastab:
            carry = lax.fori_loop(j_lo, j_diag + 1, functools.partial(step, masked=False), carry)
        else:
            carry = lax.fori_loop(j_lo, j_diag, functools.partial(step, masked=False), carry)
            carry = step(j_diag, carry, True)
        m, l, acc = carry
        o_ref[...] = acc / l
        lse_ref[...] = jnp.broadcast_to(m + jnp.log(l), (tq, LANE))

    ops = [q1, k1, v]
    specs = [_colspec(tq, lambda h, i: (i, q1cb(h))), _colspec(S, lambda h, i: (0, k1cb(h))),
             _colspec(S, lambda h, i: (0, vcb(h)))]
    if has2:
        ops += [q2, k2]
        specs += [_colspec(tq, lambda h, i: (i, q2cb(h))), _colspec(S, lambda h, i: (0, k2cb(h)))]
    if hastab:
        ops.append(tab)
        specs.append(pl.BlockSpec(tab.shape, lambda h, i: (0, 0, 0)))
    if hasc:
        ops += [ccol, crow]
        specs += [_colspec(tq, lambda h, i: (i, h)),
                  pl.BlockSpec((None, S // TK, 1, TK), lambda h, i: (h, 0, 0, 0))]
    o_spec = _colspec(tq, lambda h, i: (i, h))
    shp = jax.ShapeDtypeStruct((S, H * LANE), F32)
    outs, moved = _pcall(
        body, name=name, grid=(H, S // tq), in_specs=specs, out_specs=[o_spec, o_spec], out_shape=[shp, shp],
        operands=ops, sem=("parallel", "arbitrary"), comm=comm)
    return outs if comm is None else (outs, moved)


def _attn_bwd(S, H, q1, q1cb, k1, k1cb, v, vcb, o, do, lse, scale, name, q2=None, q2cb=None, k2=None, k2cb=None,
              tab=None, win=None, ccol=None, crow=None, comm=None):
    tq, tk = _soft_tiles(S)
    has2, hastab, hasc = q2 is not None, tab is not None, ccol is not None

    def body(*refs):
        it = iter(refs)
        q1r, k1r, vr, o_r, do_r, lse_r = (next(it) for _ in range(6))
        q2r, k2r = (next(it), next(it)) if has2 else (None, None)
        tabr = next(it) if hastab else None
        ccolr, crowr = (next(it), next(it)) if hasc else (None, None)
        dq1_r, dk1_r, dv_r = next(it), next(it), next(it)
        dq2_r, dk2_r = (next(it), next(it)) if has2 else (None, None)
        dcr_r = next(it) if hasc else None
        i = pl.program_id(1)

        @pl.when(i == 0)
        def _():
            dk1_r[...] = jnp.zeros_like(dk1_r)
            dv_r[...] = jnp.zeros_like(dv_r)
            if has2:
                dk2_r[...] = jnp.zeros_like(dk2_r)
            if hasc:
                dcr_r[...] = jnp.zeros_like(dcr_r)

        q = q1r[...]
        qb2 = q2r[...] if has2 else None
        dob = do_r[...]
        delta = jnp.sum(dob.astype(F32) * o_r[...], axis=1, keepdims=True)
        lse_c = lse_r[:, 0:1]
        cq = ccolr[:, 0:1] if hasc else None
        qpos = i * tq + lax.broadcasted_iota(jnp.int32, (tq, tk), 0)
        kio = lax.broadcasted_iota(jnp.int32, (tq, tk), 1)
        j_diag = (i * tq) // tk
        j_lo = jnp.maximum((i * tq - win) // tk, 0) if win else 0

        def probs(j, masked):
            off = pl.multiple_of(j * tk, tk)
            kb = k1r[pl.ds(off, tk), :]
            s = _dot_nt(q, kb)
            kb2 = None
            if has2:
                kb2 = k2r[pl.ds(off, tk), :]
                s = s + _dot_nt(qb2, kb2)
            s = s * scale
            if hastab:
                s = s + tabr[i - j * (tk // tq)]
            else:
                if hasc:
                    s = s + (cq - _key_row(crowr, j, tk))
                if masked:
                    s = jnp.where(kio + j * tk <= qpos, s, NEG)
            p = jnp.exp(s - lse_c)
            dp = _dot_nt(dob, vr[pl.ds(off, tk), :])
            return off, kb, kb2, p, dp

        def sweep(fn, carry):
            if hastab:
                return lax.fori_loop(j_lo, j_diag + 1, functools.partial(fn, masked=False), carry)
            carry = lax.fori_loop(j_lo, j_diag, functools.partial(fn, masked=False), carry)
            return fn(j_diag, carry, True)

        if hasc:
            def dstep(j, acc, masked):
                _, _, _, p, dp = probs(j, masked)
                return acc + jnp.sum(p * dp, axis=1, keepdims=True)

            delta = sweep(dstep, jnp.zeros((tq, 1), F32))

        def step(j, carry, masked):
            dq, dq2 = carry
            off, kb, kb2, p, dp = probs(j, masked)
            ds = p * (dp - delta)
            dsb = ds.astype(BF16)
            dq = dq + _dot_nn(dsb, kb)
            dk1_r[pl.ds(off, tk), :] += _dot_tn(dsb, q) * scale
            dv_r[pl.ds(off, tk), :] += _dot_tn(p.astype(BF16), dob)
            if has2:
                dq2 = dq2 + _dot_nn(dsb, kb2)
                dk2_r[pl.ds(off, tk), :] += _dot_tn(dsb, qb2) * scale
            if hasc:
                cs = -jnp.sum(ds, axis=0, keepdims=True)
                for c in range(tk // TK):
                    dcr_r[j * (tk // TK) + c] += cs[:, c * TK:(c + 1) * TK]
            return dq, dq2

        z = jnp.zeros((tq, LANE), F32)
        dq, dq2 = sweep(step, (z, z))
        dq1_r[...] = dq * scale
        if has2:
            dq2_r[...] = dq2 * scale

    qspec = _colspec(tq, lambda h, i: (i, h))
    kspec = _colspec(S, lambda h, i: (0, h))
    ops = [q1, k1, v, o, do, lse]
    specs = [_colspec(tq, lambda h, i: (i, q1cb(h))), _colspec(S, lambda h, i: (0, k1cb(h))),
             _colspec(S, lambda h, i: (0, vcb(h))), qspec, qspec, qspec]
    if has2:
        ops += [q2, k2]
        specs += [_colspec(tq, lambda h, i: (i, q2cb(h))), _colspec(S, lambda h, i: (0, k2cb(h)))]
    if hastab:
        ops.append(tab)
        specs.append(pl.BlockSpec(tab.shape, lambda h, i: (0, 0, 0)))
    if hasc:
        ops += [ccol, crow]
        specs += [qspec, pl.BlockSpec((None, S // TK, 1, TK), lambda h, i: (h, 0, 0, 0))]
    assert all(t.dtype == BF16 for t in ops[:3] + [do] + ([q2, k2] if has2 else []))
    out_specs = [qspec, kspec, kspec] + ([qspec, kspec] if has2 else [])
    shp = jax.ShapeDtypeStruct((S, H * LANE), F32)
    out_shape = [shp] * len(out_specs)
    if hasc:
        out_specs.append(pl.BlockSpec((None, S // TK, 1, TK), lambda h, i: (h, 0, 0, 0)))
        out_shape.append(jax.ShapeDtypeStruct((H, S // TK, 1, TK), F32))
    outs, moved = _pcall(
        body, name=name, grid=(H, S // tq), in_specs=specs, out_specs=out_specs, out_shape=out_shape,
        operands=ops, sem=("parallel", "arbitrary"), comm=comm)
    return outs if comm is None else (outs, moved)


def _scan_matrix(kind):
    j = np.arange(TK)[:, None]
    s = np.arange(TK)[None, :]
    tri = {"suffix_ex": j > s, "prefix_in": j <= s, "prefix_ex": j < s}[kind].astype(np.float32)
    half = np.concatenate([tri, np.ones((TK, TK), np.float32)], axis=1)
    return jnp.asarray(np.concatenate([half, half], axis=0), BF16)


def _scan_mxu(x, mat, carry, reverse):
    n = x.shape[1] // TK
    hi = x.astype(BF16)
    lo = (x - hi.astype(F32)).astype(BF16)
    parts = [None] * n
    for b in (reversed(range(n)) if reverse else range(n)):
        sl = slice(b * TK, (b + 1) * TK)
        r = _dot_nn(jnp.concatenate([hi[:, sl], lo[:, sl]], axis=1), mat)
        parts[b] = r[:, :TK] + carry
        carry = carry + r[:, TK:]
    return jnp.concatenate(parts, axis=1), carry


def _stick_logs(z):
    e = jnp.exp(-jnp.abs(z))
    return e, -jnp.maximum(z, 0.0) - jnp.log(1.0 + e)


def _stick_fwd(S, H, x, qcb, kcb, vcb, scale, name, comm=None):
    tq, tk = _soft_tiles(S)
    assert x.dtype == BF16

    def body(q_r, k_r, v_r, mat_r, o_ref, t_ref):
        i = pl.program_id(1)
        q = q_r[...]
        qpos = i * tq + lax.broadcasted_iota(jnp.int32, (tq, tk), 0)
        lane = lax.broadcasted_iota(jnp.int32, (tq, tk), 1)
        j_diag = (i * tq) // tk

        def step(j, carry, masked):
            c, acc = carry
            off = pl.multiple_of(j * tk, tk)
            z = _dot_nt(q, k_r[pl.ds(off, tk), :]) * scale
            _, lk = _stick_logs(z)
            if masked:
                past = lane + j * tk < qpos
                lk = jnp.where(past, lk, 0.0)
            suf, c = _scan_mxu(lk, mat_r[...], c, True)
            a = jnp.exp(z + lk + suf)
            if masked:
                a = jnp.where(past, a, 0.0)
            acc = acc + _dot_nn(a.astype(BF16), v_r[pl.ds(off, tk), :])
            return c, acc

        carry = step(j_diag, (jnp.zeros((tq, TK), F32), jnp.zeros((tq, LANE), F32)), True)
        c, acc = lax.fori_loop(0, j_diag, lambda jj, cr: step(j_diag - 1 - jj, cr, False), carry)
        o_ref[...] = acc
        t_ref[...] = c

    o_spec = _colspec(tq, lambda h, i: (i, h))
    shp = jax.ShapeDtypeStruct((S, H * LANE), F32)
    mat = _scan_matrix("suffix_ex")
    outs, moved = _pcall(
        body, name=name, grid=(H, S // tq),
        in_specs=[_colspec(tq, lambda h, i: (i, qcb(h))), _colspec(S, lambda h, i: (0, kcb(h))),
                  _colspec(S, lambda h, i: (0, vcb(h))), pl.BlockSpec(mat.shape, lambda h, i: (0, 0))],
        out_specs=[o_spec, o_spec], out_shape=[shp, shp],
        operands=[x, x, x, mat], sem=("parallel", "arbitrary"), comm=comm)
    return outs if comm is None else (outs, moved)


def _stick_bwd(S, H, x, qcb, kcb, vcb, do, tot, scale, name, comm=None):
    tq, tk = _soft_tiles(S)
    assert x.dtype == BF16 and do.dtype == BF16

    def body(q_r, k_r, v_r, do_r, t_r, pin_r, pex_r, dq_r, dk_r, dv_r):
        i = pl.program_id(1)

        @pl.when(i == 0)
        def _():
            dk_r[...] = jnp.zeros_like(dk_r)
            dv_r[...] = jnp.zeros_like(dv_r)

        q = q_r[...]
        dob = do_r[...]
        total = jnp.concatenate([t_r[...]] * (tk // TK), axis=1)
        qpos = i * tq + lax.broadcasted_iota(jnp.int32, (tq, tk), 0)
        lane = lax.broadcasted_iota(jnp.int32, (tq, tk), 1)
        j_diag = (i * tq) // tk

        def step(j, carry, masked):
            cl, cg, dq = carry
            off = pl.multiple_of(j * tk, tk)
            kb = k_r[pl.ds(off, tk), :]
            z = _dot_nt(q, kb) * scale
            e, lk = _stick_logs(z)
            if masked:
                past = lane + j * tk < qpos
                lk = jnp.where(past, lk, 0.0)
            pre, cl = _scan_mxu(lk, pin_r[...], cl, False)
            a = jnp.exp(z + lk + (total - pre))
            if masked:
                a = jnp.where(past, a, 0.0)
            g = _dot_nt(dob, v_r[pl.ds(off, tk), :]) * a
            gpre, cg = _scan_mxu(g, pex_r[...], cg, False)
            inv = 1.0 / (1.0 + e)
            small = e * inv
            pos = z >= 0
            dz = g * jnp.where(pos, small, inv) - jnp.where(pos, inv, small) * gpre
            if masked:
                dz = jnp.where(past, dz, 0.0)
            dzb = dz.astype(BF16)
            dk_r[pl.ds(off, tk), :] += _dot_tn(dzb, q) * scale
            dv_r[pl.ds(off, tk), :] += _dot_tn(a.astype(BF16), dob)
            return cl, cg, dq + _dot_nn(dzb, kb)

        zt = jnp.zeros((tq, TK), F32)
        carry = lax.fori_loop(0, j_diag, functools.partial(step, masked=False), (zt, zt, jnp.zeros((tq, LANE), F32)))
        dq_r[...] = step(j_diag, carry, True)[2] * scale

    qspec = _colspec(tq, lambda h, i: (i, h))
    kspec = _colspec(S, lambda h, i: (0, h))
    shp = jax.ShapeDtypeStruct((S, H * LANE), F32)
    pin, pex = _scan_matrix("prefix_in"), _scan_matrix("prefix_ex")
    mspec = pl.BlockSpec(pin.shape, lambda h, i: (0, 0))
    outs, moved = _pcall(
        body, name=name, grid=(H, S // tq),
        in_specs=[_colspec(tq, lambda h, i: (i, qcb(h))), _colspec(S, lambda h, i: (0, kcb(h))),
                  _colspec(S, lambda h, i: (0, vcb(h))), qspec, qspec, mspec, mspec],
        out_specs=[qspec, kspec, kspec], out_shape=[shp] * 3,
        operands=[x, x, x, do, tot, pin, pex], sem=("parallel", "arbitrary"), comm=comm)
    return outs if comm is None else (outs, moved)


def _scan8(x, rows, reverse):
    for sh in (1, 2, 4):
        if reverse:
            x = x + jnp.where(rows + sh < 8, pltpu.roll(x, 8 - sh, 0), 0.0)
        else:
            x = x + jnp.where(rows >= sh, pltpu.roll(x, sh, 0), 0.0)
    return x


def _fox_prep(S, H, proj, fcb, bias, name):
    tk = TK

    def body(f_ref, b_ref, ccol_ref, crow_ref, scr):
        rows = lax.broadcasted_iota(jnp.int32, (8, LANE), 0)

        def step(t, carry):
            off = pl.multiple_of(t * 8, 8)
            xb = f_ref[pl.ds(off, 8), :] + b_ref[...]
            lf = jnp.minimum(xb, 0.0) - jnp.log(1.0 + jnp.exp(-jnp.abs(xb)))
            lf = _scan8(lf, rows, False) + carry
            scr[pl.ds(off, 8), :] = lf
            return lf[7:8, :]

        lax.fori_loop(0, S // 8, step, jnp.zeros((1, LANE), F32))
        for h in range(H):
            ccol_ref[:, h * LANE:(h + 1) * LANE] = jnp.broadcast_to(scr[:, h:h + 1], (S, LANE))

            def tr(t, _):
                off = pl.multiple_of(t * tk, tk)
                blk = ccol_ref[pl.ds(off, tk), h * LANE:(h + 1) * LANE]
                crow_ref[h, t] = blk.T[0:1, :]
                return 0

            lax.fori_loop(0, S // tk, tr, 0)

    return pl.pallas_call(
        body, name=name, grid=(1,),
        in_specs=[_colspec(S, lambda i: (0, fcb)), pl.BlockSpec((1, LANE), lambda i: (0, 0))],
        out_specs=[pl.BlockSpec((S, H * LANE), lambda i: (0, 0)),
                   pl.BlockSpec((H, S // tk, 1, tk), lambda i: (0, 0, 0, 0))],
        out_shape=[jax.ShapeDtypeStruct((S, H * LANE), F32), jax.ShapeDtypeStruct((H, S // tk, 1, tk), F32)],
        scratch_shapes=[pltpu.VMEM((S, LANE), F32)],
        compiler_params=_cp("arbitrary"),
    )(proj, bias)


def _fox_bwd(S, H, proj, fcb, bias, dcr, name):
    tk = TK

    def body(f_ref, b_ref, dcr_ref, df_ref, db_ref, scr):
        rows = lax.broadcasted_iota(jnp.int32, (8, LANE), 0)
        lane_t = lax.broadcasted_iota(jnp.int32, (tk, LANE), 1)
        nb = S // 8

        def tr(t, _):
            off = pl.multiple_of(t * tk, tk)
            d = jnp.zeros((tk, LANE), F32)
            for h in range(H):
                d = d + jnp.where(lane_t == h, jnp.broadcast_to(dcr_ref[h, t], (LANE, tk)).T, 0.0)
            scr[pl.ds(off, tk), :] = d
            return 0

        lax.fori_loop(0, S // tk, tr, 0)

        def step(tt, carry):
            suffix, db = carry
            off = pl.multiple_of((nb - 1 - tt) * 8, 8)
            d = _scan8(scr[pl.ds(off, 8), :], rows, True) + suffix
            xb = f_ref[pl.ds(off, 8), :] + b_ref[...]
            e = jnp.exp(-jnp.abs(xb))
            dx = d * jnp.where(xb >= 0, e, 1.0) / (1.0 + e)
            df_ref[pl.ds(off, 8), :] = dx
            return d[0:1, :], db + jnp.sum(dx, axis=0, keepdims=True)

        z = jnp.zeros((1, LANE), F32)
        _, db = lax.fori_loop(0, nb, step, (z, z))
        db_ref[...] = db

    return pl.pallas_call(
        body, name=name, grid=(1,),
        in_specs=[_colspec(S, lambda i: (0, fcb)), pl.BlockSpec((1, LANE), lambda i: (0, 0)),
                  pl.BlockSpec((H, S // tk, 1, tk), lambda i: (0, 0, 0, 0))],
        out_specs=[pl.BlockSpec((S, LANE), lambda i: (0, 0)), pl.BlockSpec((1, LANE), lambda i: (0, 0))],
        out_shape=[jax.ShapeDtypeStruct((S, LANE), F32), jax.ShapeDtypeStruct((1, LANE), F32)],
        scratch_shapes=[pltpu.VMEM((S, LANE), F32)],
        compiler_params=_cp("arbitrary"),
    )(proj, bias, dcr)


def _adamw(w, slots, m, v, name, block_bytes=1 << 20):
    R, C = w.shape
    tr = R
    while tr * C * 4 > block_bytes and tr % 16 == 0:
        tr //= 2
    c1 = 1.0 - ADAM_B1 ** ADAM_STEP
    c2 = 1.0 - ADAM_B2 ** ADAM_STEP

    def body(w_ref, s_ref, m_ref, v_ref, g_ref, d_ref, nm_ref, nv_ref):
        g = s_ref[0].astype(F32)
        for s in range(1, NDEV):
            g = g + s_ref[s].astype(F32)
        mn = ADAM_B1 * m_ref[...] + (1.0 - ADAM_B1) * g
        vn = ADAM_B2 * v_ref[...] + (1.0 - ADAM_B2) * (g * g)
        g_ref[...] = g
        nm_ref[...] = mn
        nv_ref[...] = vn
        d_ref[...] = -ADAM_LR * ((mn / c1) / (jnp.sqrt(vn / c2) + ADAM_EPS) + ADAM_WD * w_ref[...])

    row = pl.BlockSpec((tr, C), lambda i: (i, 0))
    return pl.pallas_call(
        body, name=name, grid=(R // tr,),
        in_specs=[row, pl.BlockSpec((NDEV, tr, C), lambda i: (0, i, 0)), row, row],
        out_specs=[row] * 4, out_shape=[jax.ShapeDtypeStruct((R, C), F32)] * 4,
        compiler_params=_cp("parallel"),
    )(w, slots, m, v)


class _Layout:
    def __init__(self, D):
        self.GW = GW = D // 4
        self.H = H = GW // HEAD
        self.QL, self.KVL = 0, Q_LORA
        base = Q_LORA + KV_LORA
        (self.QB, self.KB, self.VB, self.QC, self.KC, self.VC, self.QD, self.KD, self.VD) = (
            base + k * GW for k in range(9))
        self.KR = base + 9 * GW
        self.FC = self.KR + LANE
        self.PW = -(-(self.FC + LANE) // 512) * 512
        self.o_kr = base
        self.o_bc = base + QK_ROPE
        self.o_fc = self.o_bc + 6 * GW
        self.o_d = self.o_fc + H
        self.IN = self.o_d + 3 * GW

    def pad(self, w):
        z = lambda n: jnp.zeros(w.shape[:-1] + (n,), w.dtype)
        return jnp.concatenate([
            w[..., :self.o_kr], w[..., self.o_bc:self.o_fc], w[..., self.o_d:self.IN],
            w[..., self.o_kr:self.o_bc], z(LANE - QK_ROPE), w[..., self.o_fc:self.o_d], z(LANE - self.H),
            z(self.PW - self.FC - LANE)], axis=-1)

    def unpad(self, g):
        return jnp.concatenate([
            g[..., :self.KR - 9 * self.GW], g[..., self.KR:self.KR + QK_ROPE], g[..., self.QB:self.QD],
            g[..., self.FC:self.FC + self.H], g[..., self.QD:self.KR]], axis=-1)


def _rope_tables(S):
    pos = jnp.arange(S, dtype=F32)

    def cs(dim):
        inv = ROPE_THETA ** (-jnp.arange(0, dim, 2, dtype=F32) / dim)
        ang = pos[:, None] * inv[None, :]
        return jnp.cos(ang), jnp.sin(ang)

    c, s = cs(HEAD)
    full = (jnp.concatenate([c, c], 1), jnp.concatenate([-s, s], 1))
    c, s = cs(QK_ROPE)
    z = jnp.zeros((S, LANE - QK_ROPE), F32)
    half = (jnp.concatenate([c, c, z], 1), jnp.concatenate([-s, s, z], 1))
    return full, half


def _dilated_table(tq, tk):
    win = max(w for w, _ in DILATED_PAIRS)
    nd = (win + tk) // tq + 1
    d = np.arange(nd)[:, None, None] * tq + np.arange(tq)[None, :, None] - np.arange(tk)[None, None, :]
    mult = np.zeros(d.shape, np.float64)
    for w, dil in DILATED_PAIRS:
        mult += (d >= 0) & (d <= w) & (d % dil == 0)
    return jnp.asarray(np.where(mult > 0, np.log(np.maximum(mult, 1.0)), NEG), F32), win


def _pack(arrs):
    rows = []
    for a in arrs:
        f = a.reshape(-1).astype(F32)
        f = jnp.pad(f, (0, (-f.shape[0]) % LANE))
        rows.append(f.reshape(-1, LANE))
    p = jnp.concatenate(rows, 0)
    return jnp.pad(p, ((0, (-p.shape[0]) % 8), (0, 0)))


def _unpack(p, shapes):
    out, r = [], 0
    for shp in shapes:
        n = int(np.prod(shp))
        nr = -(-n // LANE)
        out.append(p[r:r + nr].reshape(-1)[:n].reshape(shp))
        r += nr
    return out


def kernel(x, attn_norm, w_in, mla_q_norm, w_uq, mla_kv_norm, w_ukv, fox_forget_bias, group_norm, w_out, ffn_norm, w_gate, w_up, w_down, final_norm, loss_target, m_attn_norm, m_w_in, m_mla_q_norm, m_w_uq, m_mla_kv_norm, m_w_ukv, m_fox_forget_bias, m_group_norm, m_w_out, m_ffn_norm, m_w_gate, m_w_up, m_w_down, m_final_norm, v_attn_norm, v_w_in, v_mla_q_norm, v_w_uq, v_mla_kv_norm, v_w_ukv, v_fox_forget_bias, v_group_norm, v_w_out, v_ffn_norm, v_w_gate, v_w_up, v_w_down, v_final_norm):
    _, S, D = x.shape
    L = attn_norm.shape[0]
    lay = _Layout(D)
    H, GW, PW = lay.H, lay.GW, lay.PW
    FB = w_gate.shape[2]
    QKA = HEAD + QK_ROPE
    x = x[0]
    target = loss_target[0]
    rope_full, rope_half = _rope_tables(S)
    neg = lambda t: (t[0], -t[1])
    tab, win = _dilated_table(*_soft_tiles(S))
    cb = lambda col: col // LANE

    sh = dict(w_in=lay.pad(w_in).astype(BF16),
              **{n: w.astype(BF16) for n, w in (("w_uq", w_uq), ("w_ukv", w_ukv), ("w_out", w_out),
                                                  ("w_gate", w_gate), ("w_up", w_up), ("w_down", w_down))})
    first3 = ["w_in", "w_uq", "w_ukv"]

    def first_weights(g):
        wuq = jnp.transpose(g[1], (1, 0, 2)).reshape(Q_LORA, H, QKA)
        wuq = jnp.pad(wuq, ((0, 0), (0, 0), (0, 2 * LANE - QKA))).reshape(Q_LORA, H * 2 * LANE)
        return dict(win=g[0].reshape(D, PW), wuq=wuq,
                    wukv=jnp.transpose(g[2], (1, 0, 2)).reshape(KV_LORA, H * 2 * LANE))

    def row(a):
        return a.reshape(1, -1)

    def forward(l, x0, W):
        A = dict(x0=x0)
        A["bias"] = jnp.pad(row(fox_forget_bias[l]), ((0, 0), (0, LANE - H)))
        h1 = A["h1"] = _rms_fwd(x0, row(attn_norm[l]), D, 0, BF16, "attn_norm")
        (proj, pb), (W["wg"],) = _mm(h1, W["win"], "in_proj", out_dtype=(F32, BF16),
                                     comm=("gather", [sh["w_gate"][l]]))
        A["proj"], A["pb"] = proj, pb
        qln = A["qln"] = _rms_fwd(proj, row(mla_q_norm[l]), Q_LORA, cb(lay.QL) // 4, BF16, "q_norm")
        kvln = A["kvln"] = _rms_fwd(proj, row(mla_kv_norm[l]), KV_LORA, cb(lay.KVL) // 4, BF16, "kv_norm")
        qa, qab = _mm(qln, W["wuq"], "q_up", out_dtype=(F32, BF16))
        A["qab"] = qab
        kv = A["kv"] = _mm(kvln, W["wukv"], "kv_up", out_dtype=BF16)
        q_pe = A["q_pe"] = _rope(qa, (1, H, 2), *rope_half, True, BF16, "rope_q_mla")
        k_pe = A["k_pe"] = _rope(proj, (cb(lay.KR), 1, 1), *rope_half, True, BF16, "rope_k_mla")
        (A["o_a"], A["lse_a"]), (W["wu"],) = _attn_fwd(
            S, H, qab, lambda h: 2 * h, kv, lambda h: 2 * h, kv, lambda h: 2 * h + 1, QKA ** -0.5, "mla_fwd",
            q2=q_pe, q2cb=lambda h: h, k2=k_pe, k2cb=lambda h: 0, comm=("gather", [sh["w_up"][l]]))
        qk_b = A["qk_b"] = _rope(proj, (cb(lay.QB), 2 * H, 1), *rope_full, False, BF16, "rope_qk_dil")
        (A["o_b"], A["lse_b"]), (g_down,) = _attn_fwd(
            S, H, qk_b, lambda h: h, qk_b, lambda h: H + h, pb, lambda h: cb(lay.VB) + h, HEAD ** -0.5,
            "dilated_fwd", tab=tab, win=win, comm=("gather", [sh["w_down"][l]]))
        W["wd"] = g_down.reshape(NDEV * FB, D)
        ccol, crow = A["ccol"], A["crow"] = _fox_prep(S, H, proj, cb(lay.FC), A["bias"], "fox_prep")
        (A["o_c"], A["lse_c"]), (g_out,) = _attn_fwd(
            S, H, pb, lambda h: cb(lay.QC) + h, pb, lambda h: cb(lay.KC) + h, pb, lambda h: cb(lay.VC) + h,
            HEAD ** -0.5, "fox_fwd", ccol=ccol, crow=crow, comm=("gather", [sh["w_out"][l]]))
        W["wout"] = g_out.reshape(4 * GW, D)
        A["o_d"], A["tot_d"] = _stick_fwd(
            S, H, pb, lambda h: cb(lay.QD) + h, lambda h: cb(lay.KD) + h, lambda h: cb(lay.VD) + h,
            HEAD ** -0.5, "stick_fwd")
        mix = A["mix"] = _gn_fwd([A["o_a"], A["o_b"], A["o_c"], A["o_d"]], row(group_norm[l]), "group_norm")
        x1 = A["x1"] = _mm(mix, W["wout"], "out_proj", res=x0)
        h2 = A["h2"] = _rms_fwd(x1, row(ffn_norm[l]), D, 0, BF16, "ffn_norm")
        nxt = None
        if l + 1 < L:
            (A["g"], A["u"], A["act"]), nxt = _ffn_up(
                h2, W["wg"], W["wu"], "ffn_up", comm=("gather", [sh[n][l + 1] for n in first3]))
        else:
            A["g"], A["u"], A["act"] = _ffn_up(h2, W["wg"], W["wu"], "ffn_up_last")
        return _mm_down(A["act"], W["wd"], x1, "ffn_down"), A, nxt

    def backward(l, dx2, W, A, late):
        proj, pb = A["proj"], A["pb"]
        G, small, got = {}, {}, {}
        dgate, dup = _ffn_dact(dx2, W["wd"], A["g"], A["u"], "ffn_dact")
        G["w_down"] = _mm_dwdown(A["act"], dx2, "dw_down").reshape(NDEV, FB, D)
        dh2, (got[l, "w_down"],) = _mm_dh2(dgate, W["wg"], dup, W["wu"], "ffn_dh", comm=("exchange", [G["w_down"]]))
        G["w_gate"] = _mm_dwgate(A["h2"], dgate, "dw_gate")
        G["w_up"] = _mm_dwgate(A["h2"], dup, "dw_up")
        dx1, small["ffn_norm"] = _rms_bwd(A["x1"], row(ffn_norm[l]), dh2, D, 0, "ffn_norm_bwd", res=dx2)
        dmix = _mm(dx1, W["wout"], "out_proj_dx", tb=True)
        G["w_out"] = _mm(A["mix"], dx1, "dw_out", ta=True, out_dtype=BF16).reshape(NDEV, 4 * GW // NDEV, D)
        do_a, do_b, do_c, do_d, small["group_norm"] = _gn_bwd(
            [A["o_a"], A["o_b"], A["o_c"], A["o_d"]], row(group_norm[l]), dmix, "group_norm_bwd")
        (dq_d, dk_d, dv_d), (got[l, "w_up"],) = _stick_bwd(
            S, H, pb, lambda h: cb(lay.QD) + h, lambda h: cb(lay.KD) + h, lambda h: cb(lay.VD) + h,
            do_d, A["tot_d"], HEAD ** -0.5, "stick_bwd", comm=("exchange", [G["w_up"]]))
        (dq_c, dk_c, dv_c, dcc), moved = _attn_bwd(
            S, H, pb, lambda h: cb(lay.QC) + h, pb, lambda h: cb(lay.KC) + h, pb, lambda h: cb(lay.VC) + h,
            A["o_c"], do_c, A["lse_c"], HEAD ** -0.5, "fox_bwd" if late else "fox_bwd_top",
            ccol=A["ccol"], crow=A["crow"], comm=("exchange", [G["w_out"]] + (late or [])))
        got[l, "w_out"] = moved[0]
        got.update({(l + 1, n): s for n, s in zip(first3, moved[1:])})
        dfc, dbias = _fox_bwd(S, H, proj, cb(lay.FC), A["bias"], dcc, "fox_gate_bwd")
        small["fox_forget_bias"] = dbias[0, :H]
        qk_b = A["qk_b"]
        dq_b, dk_b, dv_b = _attn_bwd(
            S, H, qk_b, lambda h: h, qk_b, lambda h: H + h, pb, lambda h: cb(lay.VB) + h,
            A["o_b"], do_b, A["lse_b"], HEAD ** -0.5, "dilated_bwd", tab=tab, win=win)
        dqk_b = _rope(jnp.concatenate([dq_b, dk_b], 1), (0, 2 * H, 1), *neg(rope_full), False, BF16, "rope_qk_dil_bwd")
        qab, kv = A["qab"], A["kv"]
        (dq1, dk1, dv_a, dq2, dk2), (got[l, "w_gate"],) = _attn_bwd(
            S, H, qab, lambda h: 2 * h, kv, lambda h: 2 * h, kv, lambda h: 2 * h + 1,
            A["o_a"], do_a, A["lse_a"], QKA ** -0.5, "mla_bwd",
            q2=A["q_pe"], q2cb=lambda h: h, k2=A["k_pe"], k2cb=lambda h: 0, comm=("exchange", [G["w_gate"]]))
        dq2 = _rope(dq2, (0, H, 1), *neg(rope_half), True, F32, "rope_q_mla_bwd")
        dk_pe = _rope(dk2.reshape(S, H, LANE).sum(1), (0, 1, 1), *neg(rope_half), True, BF16, "rope_k_mla_bwd")
        dqa = jnp.stack([dq1.reshape(S, H, LANE), dq2.reshape(S, H, LANE)], 2).reshape(S, H * 2 * LANE).astype(BF16)
        dkv = jnp.stack([dk1.reshape(S, H, LANE), dv_a.reshape(S, H, LANE)], 2).reshape(S, H * 2 * LANE).astype(BF16)
        dwuq = _mm(A["qln"], dqa, "dw_uq", ta=True, out_dtype=BF16)
        dwuq = dwuq.reshape(Q_LORA, H, 2 * LANE)[:, :, :QKA].reshape(Q_LORA, NDEV, H * QKA // NDEV)
        G["w_uq"] = jnp.transpose(dwuq, (1, 0, 2))
        dwukv = _mm(A["kvln"], dkv, "dw_ukv", ta=True, out_dtype=BF16).reshape(KV_LORA, NDEV, H * 2 * LANE // NDEV)
        G["w_ukv"] = jnp.transpose(dwukv, (1, 0, 2))
        dqln = _mm(dqa, W["wuq"], "q_up_dx", tb=True)
        dkvln = _mm(dkv, W["wukv"], "kv_up_dx", tb=True)
        dql, small["mla_q_norm"] = _rms_bwd(proj, row(mla_q_norm[l]), dqln, Q_LORA, cb(lay.QL) // 4, "q_norm_bwd")
        dkvl, small["mla_kv_norm"] = _rms_bwd(proj, row(mla_kv_norm[l]), dkvln, KV_LORA, cb(lay.KVL) // 4, "kv_norm_bwd")
        bf = lambda t: t.astype(BF16)
        dproj = jnp.concatenate([
            bf(dql), bf(dkvl), dqk_b, bf(dv_b), bf(dq_c), bf(dk_c), bf(dv_c), bf(dq_d), bf(dk_d), bf(dv_d),
            dk_pe, bf(dfc), jnp.zeros((S, PW - lay.FC - LANE), BF16)], axis=1)
        G["w_in"] = _mm(A["h1"], dproj, "dw_in", ta=True, out_dtype=BF16).reshape(NDEV, D // NDEV, PW)
        dh1 = _mm(dproj, W["win"], "in_proj_dx", tb=True)
        dx0, small["attn_norm"] = _rms_bwd(A["x0"], row(attn_norm[l]), dh1, D, 0, "attn_norm_bwd", res=dx1)
        return dx0, [G[n] for n in first3], got, small

    big = first3 + ["w_out", "w_gate", "w_up", "w_down"]
    Ws, As = [], []
    xc = x
    nxt = _comm_alone("gather", [sh[n][0] for n in first3], "gather_first")
    for l in range(L):
        W = first_weights(nxt)
        xc, A, nxt = forward(l, xc, W)
        Ws.append(W)
        As.append(A)
    dx, loss_part = _final_loss(xc, row(final_norm), target, "final_loss")
    dx, dfinal = _rms_bwd(xc, row(final_norm), dx, D, 0, "final_norm_bwd")
    slots = {}
    smalls = [None] * L
    late = None
    for l in reversed(range(L)):
        dx, late, got, smalls[l] = backward(l, dx, Ws[l], As[l], late)
        slots.update(got)
    slots.update({(0, n): s for n, s in zip(first3, _comm_alone("exchange", late, "exchange_last"))})

    names_small = ["attn_norm", "mla_q_norm", "mla_kv_norm", "fox_forget_bias", "group_norm", "ffn_norm"]
    params = dict(attn_norm=attn_norm, mla_q_norm=mla_q_norm, mla_kv_norm=mla_kv_norm, fox_forget_bias=fox_forget_bias,
                  group_norm=group_norm, ffn_norm=ffn_norm, final_norm=final_norm, w_in=w_in, w_uq=w_uq, w_ukv=w_ukv,
                  w_out=w_out, w_gate=w_gate, w_up=w_up, w_down=w_down)
    moms = dict(attn_norm=(m_attn_norm, v_attn_norm), mla_q_norm=(m_mla_q_norm, v_mla_q_norm),
                mla_kv_norm=(m_mla_kv_norm, v_mla_kv_norm), fox_forget_bias=(m_fox_forget_bias, v_fox_forget_bias),
                group_norm=(m_group_norm, v_group_norm), ffn_norm=(m_ffn_norm, v_ffn_norm),
                final_norm=(m_final_norm, v_final_norm), w_in=(m_w_in, v_w_in), w_uq=(m_w_uq, v_w_uq),
                w_ukv=(m_w_ukv, v_w_ukv), w_out=(m_w_out, v_w_out), w_gate=(m_w_gate, v_w_gate),
                w_up=(m_w_up, v_w_up), w_down=(m_w_down, v_w_down))
    small_list = names_small + ["final_norm"]
    small_grads = [jnp.stack([smalls[l][n].reshape(params[n].shape[1:]) for l in range(L)]) for n in names_small]
    small_grads.append(dfinal.reshape(final_norm.shape))
    shapes = [params[n].shape for n in small_list] + [(LANE,)]
    packed_g = _comm_alone("gather", [_pack(small_grads + [loss_part.reshape(LANE)])], "gather_small")[0]
    zero = jnp.zeros((LANE,), F32)
    res_small = _adamw(_pack([params[n] for n in small_list] + [zero]), packed_g,
                       _pack([moms[n][0] for n in small_list] + [zero]),
                       _pack([moms[n][1] for n in small_list] + [zero]), "adamw_small")
    unp = [_unpack(r, shapes) for r in res_small]
    out = {n: tuple(unp[k][i] for k in range(4)) for i, n in enumerate(small_list)}
    loss = unp[0][-1][0]

    for i, n in enumerate(big):
        st = jnp.stack([slots[l, n] for l in range(L)], axis=1)
        if n == "w_in":
            st = lay.unpad(st)
        C = st.shape[-1]
        st = st.reshape(NDEV, -1, C)
        w2 = params[n].reshape(-1, C)
        res = _adamw(w2, st, moms[n][0].reshape(-1, C), moms[n][1].reshape(-1, C), "adamw_" + n)
        out[n] = tuple(r.reshape(params[n].shape) for r in res)

    order = ["attn_norm", "w_in", "mla_q_norm", "w_uq", "mla_kv_norm", "w_ukv", "fox_forget_bias", "group_norm",
             "w_out", "ffn_norm", "w_gate", "w_up", "w_down", "final_norm"]
    return (loss, dx[None], *[out[n][0] for n in order], *[out[n][1] for n in order],
            *[out[n][2] for n in order], *[out[n][3] for n in order])
```

```python
import functools
import math

import numpy as np
import jax
import jax.numpy as jnp
from jax import lax
from jax.experimental import pallas as pl
from jax.experimental.pallas import tpu as pltpu

F32 = jnp.float32
BF16 = jnp.bfloat16
NDEV = 8
LANE = 128
HEAD = 128
Q_LORA = 512
KV_LORA = 512
QK_ROPE = 64
DILATED_PAIRS = ((128, 1), (512, 4), (2048, 16))
ROPE_THETA = 10000.0
EPS = 1e-6
NEG = -1e30
TQ = 256
TK = 128
TKS = 512
VMEM_LIMIT = 48 * 1024 * 1024
MM_OPERAND_BYTES = 20 * 1024 * 1024
ADAM_LR, ADAM_B1, ADAM_B2, ADAM_EPS, ADAM_WD, ADAM_STEP = 0.001, 0.9, 0.999, 1e-08, 0.01, 10
MESH = pl.DeviceIdType.MESH
ANY = pl.BlockSpec(memory_space=pl.ANY)


def _cp(*sem):
    return pltpu.CompilerParams(dimension_semantics=sem, vmem_limit_bytes=VMEM_LIMIT)


def _dot(a, b, ca, cb):
    return lax.dot_general(a, b, (((ca,), (cb,)), ((), ())), preferred_element_type=F32)


def _dot_nn(a, b):
    return _dot(a, b, 1, 0)


def _dot_nt(a, b):
    return _dot(a, b, 1, 1)


def _dot_tn(a, b):
    return _dot(a, b, 0, 0)


def _tile(n, t):
    if n <= t:
        return n
    t -= t % LANE
    while n % t:
        t -= LANE
    return t


def _direct_copies(ins, outs, send_sems, recv_sems, local_sems, want_recvs=True):
    x, y, c = lax.axis_index("x"), lax.axis_index("y"), lax.axis_index("c")
    my_id = 4 * x + 2 * y + c
    local, sends, recvs = [], [], []
    for a in range(len(ins)):
        mine = ins[a].at[my_id]
        local.append(pltpu.make_async_copy(mine, outs[a].at[my_id], local_sems.at[a]))
        for k in range(1, NDEV):
            peer = (1 - x if k & 4 else x, 1 - y if k & 2 else y, 1 - c if k & 1 else c)
            pid = 4 * peer[0] + 2 * peer[1] + peer[2]
            sems = dict(send_sem=send_sems.at[a, k - 1], recv_sem=recv_sems.at[a, k - 1],
                        device_id=peer, device_id_type=MESH)
            sends.append(pltpu.make_async_remote_copy(src_ref=ins[a].at[pid], dst_ref=outs[a].at[my_id], **sems))
            if want_recvs:
                recvs.append(pltpu.make_async_remote_copy(src_ref=mine, dst_ref=outs[a].at[pid], **sems))
    return local, sends, recvs


def _comm_start(kind, ins, outs, send_sems, recv_sems, local_sems):
    if kind == "exchange":
        local, sends, _ = _direct_copies(ins, outs, send_sems, recv_sems, local_sems, want_recvs=False)
        for cp in local + sends:
            cp.start()
        return
    x, y, c = lax.axis_index("x"), lax.axis_index("y"), lax.axis_index("c")
    for a in range(len(ins)):
        mine = outs[a].at[4 * x + 2 * y + c]
        pltpu.make_async_copy(ins[a], mine, local_sems.at[a]).start()
        for k, to in enumerate([(x, y, 1 - c), (1 - x, y, c), (x, 1 - y, c), (1 - x, 1 - y, c)]):
            pltpu.make_async_remote_copy(src_ref=ins[a], dst_ref=mine, send_sem=send_sems.at[a, k],
                                         recv_sem=recv_sems.at[a, k], device_id=to, device_id_type=MESH).start()


def _comm_finish(kind, ins, outs, send_sems, recv_sems, local_sems):
    if kind == "exchange":
        local, sends, recvs = _direct_copies(ins, outs, send_sems, recv_sems, local_sems)
        for cp in recvs:
            cp.wait_recv()
        for cp in sends:
            cp.wait_send()
        for cp in local:
            cp.wait()
        return
    x, y, c = lax.axis_index("x"), lax.axis_index("y"), lax.axis_index("c")
    sibling = (x, y, 1 - c)
    chips = [(1 - x, y), (x, 1 - y), (1 - x, 1 - y)]
    for a in range(len(ins)):
        def copy(k, block, to):
            rows = outs[a].at[4 * block[0] + 2 * block[1] + block[2]]
            return pltpu.make_async_remote_copy(src_ref=rows, dst_ref=rows, send_sem=send_sems.at[a, k],
                                                recv_sem=recv_sems.at[a, k], device_id=to, device_id_type=MESH)

        passed = []
        for j, chip in enumerate(chips):
            copy(1 + j, (*chip, c), (x, y, c)).wait_recv()
            passed.append(copy(4 + j, (*chip, c), sibling))
            passed[-1].start()
        copy(0, sibling, (x, y, c)).wait_recv()
        for j, chip in enumerate(chips):
            copy(4 + j, (*chip, 1 - c), (x, y, c)).wait_recv()
        for k in range(4):
            copy(k, (x, y, c), sibling).wait_send()
        for cp in passed:
            cp.wait_send()
        pltpu.make_async_copy(ins[a], outs[a].at[4 * x + 2 * y + c], local_sems.at[a]).wait()


def _comm_shapes(kind, arrs):
    out_shape = [jax.ShapeDtypeStruct(((NDEV,) if kind == "gather" else ()) + a.shape, a.dtype) for a in arrs]
    n = len(arrs)
    sems = [pltpu.SemaphoreType.DMA((n, 7)), pltpu.SemaphoreType.DMA((n, 7)), pltpu.SemaphoreType.DMA((n,))]
    return out_shape, sems


def _comm_alone(kind, arrs, name):
    n = len(arrs)

    def body(*refs):
        _comm_start(kind, refs[:n], refs[n:2 * n], *refs[2 * n:])
        _comm_finish(kind, refs[:n], refs[n:2 * n], *refs[2 * n:])

    out_shape, sems = _comm_shapes(kind, arrs)
    return pl.pallas_call(body, name=name, out_shape=out_shape, in_specs=[ANY] * n, out_specs=[ANY] * n,
                          scratch_shapes=sems)(*arrs)


def _pcall(body, *, name, grid, in_specs, out_specs, out_shape, operands, sem, scratch_shapes=(), comm=None):
    in_specs, out_specs, out_shape = list(in_specs), list(out_specs), list(out_shape)
    scratch_shapes = list(scratch_shapes)
    if comm is None:
        res = pl.pallas_call(body, name=name, grid=grid, in_specs=in_specs, out_specs=out_specs, out_shape=out_shape,
                             scratch_shapes=scratch_shapes, compiler_params=_cp(*sem))(*operands)
        return list(res), []
    kind, arrs = comm
    nc, n_in, n_out, n_scr = len(arrs), len(operands), len(out_shape), len(scratch_shapes)
    c_shape, c_sems = _comm_shapes(kind, arrs)

    def carrier(*refs):
        ins, cin = refs[:n_in], refs[n_in:n_in + nc]
        outs = refs[n_in + nc:n_in + nc + n_out]
        cout = refs[n_in + nc + n_out:n_in + 2 * nc + n_out]
        scr = refs[n_in + 2 * nc + n_out:n_in + 2 * nc + n_out + n_scr]
        sems = refs[n_in + 2 * nc + n_out + n_scr:]
        pids = [pl.program_id(d) for d in range(len(grid))]
        first = functools.reduce(jnp.logical_and, [p == 0 for p in pids])
        last = functools.reduce(jnp.logical_and, [p == g - 1 for p, g in zip(pids, grid)])

        @pl.when(first)
        def _():
            _comm_start(kind, cin, cout, *sems)

        body(*ins, *outs, *scr)

        @pl.when(last)
        def _():
            _comm_finish(kind, cin, cout, *sems)

    res = pl.pallas_call(
        carrier, name=name, grid=grid, in_specs=in_specs + [ANY] * nc, out_specs=out_specs + [ANY] * nc,
        out_shape=out_shape + c_shape, scratch_shapes=scratch_shapes + c_sems,
        compiler_params=_cp(*["arbitrary"] * len(grid)))(*operands, *arrs)
    return list(res[:n_out]), list(res[n_out:])


def _mm_call(pairs, grid, a_spec, b_spec, o_spec, out_shape, acc_shape, nk, ca, cb, name,
             res=None, res_spec=None, comm=None):
    npairs = len(pairs)
    multi = isinstance(out_shape, (list, tuple))
    nout = len(out_shape) if multi else 1

    def body(*refs):
        ab = refs[:2 * npairs]
        r_ref = refs[2 * npairs] if res is not None else None
        o_refs, acc = refs[-1 - nout:-1], refs[-1]
        k = pl.program_id(2)

        @pl.when(k == 0)
        def _():
            acc[...] = jnp.zeros_like(acc)

        tot = None
        for p in range(npairs):
            d = _dot(ab[2 * p][...].astype(BF16), ab[2 * p + 1][...].astype(BF16), ca, cb)
            tot = d if tot is None else tot + d
        acc[...] += tot

        @pl.when(k == nk - 1)
        def _():
            r = acc[...]
            if r_ref is not None:
                r = r + r_ref[...]
            for o_ref in o_refs:
                o_ref[...] = r.astype(o_ref.dtype)

    ops, specs = [], []
    for a, b in pairs:
        ops += [a, b]
        specs += [a_spec, b_spec]
    if res is not None:
        ops.append(res)
        specs.append(res_spec)
    outs, moved = _pcall(
        body, name=name, grid=grid, in_specs=specs, out_specs=[o_spec] * nout,
        out_shape=out_shape if multi else [out_shape], operands=ops,
        scratch_shapes=[pltpu.VMEM(acc_shape, F32)], sem=("parallel", "parallel", "arbitrary"), comm=comm)
    outs = outs if multi else outs[0]
    return outs if comm is None else (outs, moved)


def _k_tile(K, row_bytes, tk=2048):
    tk = _tile(K, tk)
    while 2 * tk * row_bytes > MM_OPERAND_BYTES and tk % 256 == 0:
        tk //= 2
    return tk


def _mm(a, b, name, ta=False, tb=False, out_dtype=F32, res=None, tm=1024, tn=1024, comm=None):
    M, K = (a.shape[1], a.shape[0]) if ta else a.shape
    N = b.shape[0] if tb else b.shape[1]
    tm, tn = _tile(M, tm), _tile(N, tn)
    tk = _k_tile(K, tm * a.dtype.itemsize + tn * b.dtype.itemsize)
    a_spec = pl.BlockSpec((tk, tm), lambda i, j, k: (k, i)) if ta else pl.BlockSpec((tm, tk), lambda i, j, k: (i, k))
    b_spec = pl.BlockSpec((tn, tk), lambda i, j, k: (j, k)) if tb else pl.BlockSpec((tk, tn), lambda i, j, k: (k, j))
    o_spec = pl.BlockSpec((tm, tn), lambda i, j, k: (i, j))
    if isinstance(out_dtype, tuple):
        out_shape = [jax.ShapeDtypeStruct((M, N), d) for d in out_dtype]
    else:
        out_shape = jax.ShapeDtypeStruct((M, N), out_dtype)
    return _mm_call([(a, b)], (M // tm, N // tn, K // tk), a_spec, b_spec, o_spec,
                    out_shape, (tm, tn), K // tk,
                    0 if ta else 1, 1 if tb else 0, name, res=res, res_spec=o_spec, comm=comm)


def _mm_down(act, wd, res, name, tm=1024, tn=1024):
    _, S, FB = act.shape
    D = wd.shape[1]
    tm, tn = _tile(S, tm), _tile(D, tn)
    o_spec = pl.BlockSpec((tm, tn), lambda i, j, k: (i, j))
    return _mm_call([(act, wd)], (S // tm, D // tn, NDEV),
                    pl.BlockSpec((None, tm, FB), lambda i, j, k: (k, i, 0)),
                    pl.BlockSpec((FB, tn), lambda i, j, k: (k, j)), o_spec,
                    jax.ShapeDtypeStruct((S, D), F32), (tm, tn), NDEV, 1, 0, name, res=res, res_spec=o_spec)


def _mm_dwdown(act, dy, name, tn=1024):
    _, S, FB = act.shape
    D = dy.shape[1]
    tn = _tile(D, tn)
    tk = _k_tile(S, FB * act.dtype.itemsize + tn * dy.dtype.itemsize)
    return _mm_call([(act, dy)], (NDEV, D // tn, S // tk),
                    pl.BlockSpec((None, tk, FB), lambda i, j, k: (i, k, 0)),
                    pl.BlockSpec((tk, tn), lambda i, j, k: (k, j)),
                    pl.BlockSpec((FB, tn), lambda i, j, k: (i, j)),
                    jax.ShapeDtypeStruct((NDEV * FB, D), BF16), (FB, tn), S // tk, 0, 0, name)


def _mm_dh2(dg, wg, du, wu, name, tm=1024, tn=1024, comm=None):
    _, S, FB = dg.shape
    D = wg.shape[1]
    tm, tn = _tile(S, tm), _tile(D, tn)
    return _mm_call([(dg, wg), (du, wu)], (S // tm, D // tn, NDEV),
                    pl.BlockSpec((None, tm, FB), lambda i, j, k: (k, i, 0)),
                    pl.BlockSpec((None, tn, FB), lambda i, j, k: (k, j, 0)),
                    pl.BlockSpec((tm, tn), lambda i, j, k: (i, j)),
                    jax.ShapeDtypeStruct((S, D), F32), (tm, tn), NDEV, 1, 1, name, comm=comm)


def _mm_dwgate(h2, dg, name, tm=1024):
    _, S, FB = dg.shape
    D = h2.shape[1]
    tm = _tile(D, tm)
    tk = _k_tile(S, tm * h2.dtype.itemsize + FB * dg.dtype.itemsize)
    return _mm_call([(h2, dg)], (NDEV, D // tm, S // tk),
                    pl.BlockSpec((tk, tm), lambda p, i, k: (k, i)),
                    pl.BlockSpec((None, tk, FB), lambda p, i, k: (p, k, 0)),
                    pl.BlockSpec((None, tm, FB), lambda p, i, k: (p, i, 0)),
                    jax.ShapeDtypeStruct((NDEV, D, FB), BF16), (tm, FB), S // tk, 0, 0, name)


def _ffn_up(h2, wg, wu, name, tm=512, comm=None):
    S, D = h2.shape
    FB = wg.shape[2]
    tm = _tile(S, tm)

    def body(h_ref, wg_ref, wu_ref, g_ref, u_ref, act_ref):
        h = h_ref[...]
        g = _dot_nn(h, wg_ref[...])
        u = _dot_nn(h, wu_ref[...])
        g_ref[...] = g.astype(BF16)
        u_ref[...] = u.astype(BF16)
        act_ref[...] = (g / (1.0 + jnp.exp(-g)) * u).astype(BF16)

    w_spec = pl.BlockSpec((None, D, FB), lambda p, i: (p, 0, 0))
    o_spec = pl.BlockSpec((None, tm, FB), lambda p, i: (p, i, 0))
    shp = (NDEV, S, FB)
    outs, moved = _pcall(
        body, name=name, grid=(NDEV, S // tm),
        in_specs=[pl.BlockSpec((tm, D), lambda p, i: (i, 0)), w_spec, w_spec],
        out_specs=[o_spec, o_spec, o_spec],
        out_shape=[jax.ShapeDtypeStruct(shp, BF16)] * 3,
        operands=[h2, wg, wu], sem=("parallel", "parallel"), comm=comm)
    return outs if comm is None else (outs, moved)


def _ffn_dact(dy, wd, g, u, name, tm=512):
    S, D = dy.shape
    FB = g.shape[2]
    tm = _tile(S, tm)

    def body(dy_ref, wd_ref, g_ref, u_ref, dg_ref, du_ref):
        dact = _dot_nt(dy_ref[...], wd_ref[...])
        gv = g_ref[...].astype(F32)
        sg = 1.0 / (1.0 + jnp.exp(-gv))
        dg_ref[...] = (dact * u_ref[...].astype(F32) * (sg * (1.0 + gv * (1.0 - sg)))).astype(BF16)
        du_ref[...] = (dact * (gv * sg)).astype(BF16)

    t_spec = pl.BlockSpec((None, tm, FB), lambda p, i: (p, i, 0))
    shp = jax.ShapeDtypeStruct((NDEV, S, FB), BF16)
    return pl.pallas_call(
        body, name=name, grid=(NDEV, S // tm),
        in_specs=[pl.BlockSpec((tm, D), lambda p, i: (i, 0)), pl.BlockSpec((FB, D), lambda p, i: (p, 0)),
                  t_spec, t_spec],
        out_specs=[t_spec, t_spec], out_shape=[shp, shp],
        compiler_params=_cp("parallel", "parallel"),
    )(dy, wd, g, u)


def _rms_fwd(x, gain, width, cb, out_dtype, name, ts=512):
    S = x.shape[0]
    ts = _tile(S, ts)

    def body(x_ref, g_ref, o_ref):
        xv = x_ref[...]
        r = lax.rsqrt(jnp.mean(xv * xv, axis=1, keepdims=True) + EPS)
        o_ref[...] = (xv * r * g_ref[...]).astype(o_ref.dtype)

    return pl.pallas_call(
        body, name=name, grid=(S // ts,),
        in_specs=[pl.BlockSpec((ts, width), lambda i: (i, cb)), pl.BlockSpec((1, width), lambda i: (0, 0))],
        out_specs=pl.BlockSpec((ts, width), lambda i: (i, 0)),
        out_shape=jax.ShapeDtypeStruct((S, width), out_dtype),
        compiler_params=_cp("parallel"),
    )(x, gain)


def _rms_bwd(x, gain, dy, width, cb, name, res=None, ts=256):
    S = x.shape[0]
    ts = _tile(S, ts)
    has_res = res is not None

    def body(*refs):
        x_ref, g_ref, dy_ref = refs[:3]
        r_ref = refs[3] if has_res else None
        dx_ref, dxb_ref, dg_ref = refs[-3], refs[-2], refs[-1]

        @pl.when(pl.program_id(0) == 0)
        def _():
            dg_ref[...] = jnp.zeros_like(dg_ref)

        xv = x_ref[...]
        r = lax.rsqrt(jnp.mean(xv * xv, axis=1, keepdims=True) + EPS)
        xh = xv * r
        dyv = dy_ref[...]
        dyg = dyv * g_ref[...]
        dx = r * (dyg - xh * jnp.mean(dyg * xh, axis=1, keepdims=True))
        if has_res:
            dx = dx + r_ref[...]
        dx_ref[...] = dx
        dxb_ref[...] = dx.astype(BF16)
        dg_ref[...] += jnp.sum(dyv * xh, axis=0, keepdims=True)

    row = pl.BlockSpec((ts, width), lambda i: (i, 0))
    vec = pl.BlockSpec((1, width), lambda i: (0, 0))
    ops = [x, gain, dy] + ([res] if has_res else [])
    specs = [pl.BlockSpec((ts, width), lambda i: (i, cb)), vec, row] + ([row] if has_res else [])
    return pl.pallas_call(
        body, name=name, grid=(S // ts,), in_specs=specs, out_specs=[row, row, vec],
        out_shape=[jax.ShapeDtypeStruct((S, width), F32), jax.ShapeDtypeStruct((S, width), BF16),
                   jax.ShapeDtypeStruct((1, width), F32)],
        compiler_params=_cp("arbitrary"),
    )(*ops)


def _gn_fwd(outs, gain, name, ts=512):
    S, GW = outs[0].shape
    ts = _tile(S, ts)

    def body(a_ref, b_ref, c_ref, d_ref, g_ref, o_ref):
        for g, r_ref in enumerate((a_ref, b_ref, c_ref, d_ref)):
            xv = r_ref[...]
            r = lax.rsqrt(jnp.mean(xv * xv, axis=1, keepdims=True) + EPS)
            o_ref[:, g * GW:(g + 1) * GW] = (xv * r * g_ref[:, g * GW:(g + 1) * GW]).astype(BF16)

    row = pl.BlockSpec((ts, GW), lambda i: (i, 0))
    return pl.pallas_call(
        body, name=name, grid=(S // ts,),
        in_specs=[row] * 4 + [pl.BlockSpec((1, 4 * GW), lambda i: (0, 0))],
        out_specs=pl.BlockSpec((ts, 4 * GW), lambda i: (i, 0)),
        out_shape=jax.ShapeDtypeStruct((S, 4 * GW), BF16),
        compiler_params=_cp("parallel"),
    )(*outs, gain)


def _gn_bwd(outs, gain, dmix, name, ts=256):
    S, GW = outs[0].shape
    ts = _tile(S, ts)

    def body(a_ref, b_ref, c_ref, d_ref, g_ref, dm_ref, da_ref, db_ref, dc_ref, dd_ref, dg_ref):
        @pl.when(pl.program_id(0) == 0)
        def _():
            dg_ref[...] = jnp.zeros_like(dg_ref)

        for g, (r_ref, o_ref) in enumerate(zip((a_ref, b_ref, c_ref, d_ref), (da_ref, db_ref, dc_ref, dd_ref))):
            sl = slice(g * GW, (g + 1) * GW)
            xv = r_ref[...]
            r = lax.rsqrt(jnp.mean(xv * xv, axis=1, keepdims=True) + EPS)
            xh = xv * r
            dyv = dm_ref[:, sl]
            dyg = dyv * g_ref[:, sl]
            o_ref[...] = (r * (dyg - xh * jnp.mean(dyg * xh, axis=1, keepdims=True))).astype(BF16)
            dg_ref[:, sl] += jnp.sum(dyv * xh, axis=0, keepdims=True)

    row = pl.BlockSpec((ts, GW), lambda i: (i, 0))
    vec = pl.BlockSpec((1, 4 * GW), lambda i: (0, 0))
    return pl.pallas_call(
        body, name=name, grid=(S // ts,),
        in_specs=[row] * 4 + [vec, pl.BlockSpec((ts, 4 * GW), lambda i: (i, 0))],
        out_specs=[row] * 4 + [vec],
        out_shape=[jax.ShapeDtypeStruct((S, GW), BF16)] * 4 + [jax.ShapeDtypeStruct((1, 4 * GW), F32)],
        compiler_params=_cp("arbitrary"),
    )(*outs, gain, dmix)


def _rope(x, cbs, cos, sin, half, out_dtype, name, ts=512):
    S = x.shape[0]
    cb0, nb, stride = cbs
    ts = _tile(S, ts)

    def body(x_ref, c_ref, s_ref, o_ref):
        xv = x_ref[...].astype(F32)
        if half:
            lane = lax.broadcasted_iota(jnp.int32, xv.shape, 1)
            partner = jnp.where(lane % 64 < 32, pltpu.roll(xv, LANE - 32, 1), pltpu.roll(xv, 32, 1))
        else:
            partner = pltpu.roll(xv, 64, 1)
        o_ref[...] = (xv * c_ref[...] + partner * s_ref[...]).astype(o_ref.dtype)

    tab = pl.BlockSpec((ts, LANE), lambda i, j: (i, 0))
    return pl.pallas_call(
        body, name=name, grid=(S // ts, nb),
        in_specs=[pl.BlockSpec((ts, LANE), lambda i, j: (i, cb0 + stride * j)), tab, tab],
        out_specs=pl.BlockSpec((ts, LANE), lambda i, j: (i, j)),
        out_shape=jax.ShapeDtypeStruct((S, nb * LANE), out_dtype),
        compiler_params=_cp("parallel", "parallel"),
    )(x, cos, sin)


def _final_loss(x, gain, target, name, ts=256):
    S, D = x.shape
    ts = _tile(S, ts)

    def body(x_ref, g_ref, t_ref, dy_ref, l_ref):
        @pl.when(pl.program_id(0) == 0)
        def _():
            l_ref[...] = jnp.zeros_like(l_ref)

        xv = x_ref[...]
        r = lax.rsqrt(jnp.mean(xv * xv, axis=1, keepdims=True) + EPS)
        err = xv * r * g_ref[...] - t_ref[...]
        dy_ref[...] = err * (1.0 / D)
        part = jnp.sum(jnp.mean(err * err, axis=1, keepdims=True), axis=0, keepdims=True)
        l_ref[...] += jnp.broadcast_to(0.5 * part, (1, LANE))

    row = pl.BlockSpec((ts, D), lambda i: (i, 0))
    return pl.pallas_call(
        body, name=name, grid=(S // ts,),
        in_specs=[row, pl.BlockSpec((1, D), lambda i: (0, 0)), row],
        out_specs=[row, pl.BlockSpec((1, LANE), lambda i: (0, 0))],
        out_shape=[jax.ShapeDtypeStruct((S, D), F32), jax.ShapeDtypeStruct((1, LANE), F32)],
        compiler_params=_cp("arbitrary"),
    )(x, gain, target)


def _colspec(rows, f):
    return pl.BlockSpec((rows, LANE), f)


def _soft_tiles(S):
    tq = _tile(S, TQ)
    tk = _tile(S, TKS)
    assert tk % tq == 0
    return tq, tk


def _key_row(crow_ref, j, tk):
    n = tk // TK
    return jnp.concatenate([crow_ref[j * n + c] for c in range(n)], axis=1)


def _attn_fwd(S, H, q1, q1cb, k1, k1cb, v, vcb, scale, name, q2=None, q2cb=None, k2=None, k2cb=None,
              tab=None, win=None, ccol=None, crow=None, comm=None):
    tq, tk = _soft_tiles(S)
    has2, hastab, hasc = q2 is not None, tab is not None, ccol is not None

    def body(*refs):
        it = iter(refs)
        q1r, k1r, vr = next(it), next(it), next(it)
        q2r, k2r = (next(it), next(it)) if has2 else (None, None)
        tabr = next(it) if hastab else None
        ccolr, crowr = (next(it), next(it)) if hasc else (None, None)
        o_ref, lse_ref = next(it), next(it)
        i = pl.program_id(1)
        q = q1r[...]
        qb2 = q2r[...] if has2 else None
        cq = ccolr[:, 0:1] if hasc else None
        qpos = i * tq + lax.broadcasted_iota(jnp.int32, (tq, tk), 0)
        kio = lax.broadcasted_iota(jnp.int32, (tq, tk), 1)
        j_diag = (i * tq) // tk
        j_lo = jnp.maximum((i * tq - win) // tk, 0) if win else 0

        def step(j, carry, masked):
            m, l, acc = carry
            off = pl.multiple_of(j * tk, tk)
            s = _dot_nt(q, k1r[pl.ds(off, tk), :])
            if has2:
                s = s + _dot_nt(qb2, k2r[pl.ds(off, tk), :])
            s = s * scale
            if hastab:
                s = s + tabr[i - j * (tk // tq)]
            else:
                if hasc:
                    s = s + (cq - _key_row(crowr, j, tk))
                if masked:
                    s = jnp.where(kio + j * tk <= qpos, s, NEG)
            mn = jnp.maximum(m, jnp.max(s, axis=1, keepdims=True))
            p = jnp.exp(s - mn)
            al = jnp.exp(m - mn)
            l = al * l + jnp.sum(p, axis=1, keepdims=True)
            acc = al * acc + _dot_nn(p.astype(BF16), vr[pl.ds(off, tk), :])
            return mn, l, acc

        carry = (jnp.full((tq, 1), NEG, F32), jnp.zeros((tq, 1), F32), jnp.zeros((tq, LANE), F32))
        if hastab:
            carry = lax.fori_loop(j_lo, j_diag + 1, functools.partial(step, masked=False), carry)
        else:
            carry = lax.fori_loop(j_lo, j_diag, functools.partial(step, masked=False), carry)
            carry = step(j_diag, carry, True)
        m, l, acc = carry
        o_ref[...] = acc / l
        lse_ref[...] = jnp.broadcast_to(m + jnp.log(l), (tq, LANE))

    ops = [q1, k1, v]
    specs = [_colspec(tq, lambda h, i: (i, q1cb(h))), _colspec(S, lambda h, i: (0, k1cb(h))),
             _colspec(S, lambda h, i: (0, vcb(h)))]
    if has2:
        ops += [q2, k2]
        specs += [_colspec(tq, lambda h, i: (i, q2cb(h))), _colspec(S, lambda h, i: (0, k2cb(h)))]
    if hastab:
        ops.append(tab)
        specs.append(pl.BlockSpec(tab.shape, lambda h, i: (0, 0, 0)))
    if hasc:
        ops += [ccol, crow]
        specs += [_colspec(tq, lambda h, i: (i, h)),
                  pl.BlockSpec((None, S // TK, 1, TK), lambda h, i: (h, 0, 0, 0))]
    o_spec = _colspec(tq, lambda h, i: (i, h))
    shp = jax.ShapeDtypeStruct((S, H * LANE), F32)
    outs, moved = _pcall(
        body, name=name, grid=(H, S // tq), in_specs=specs, out_specs=[o_spec, o_spec], out_shape=[shp, shp],
        operands=ops, sem=("parallel", "arbitrary"), comm=comm)
    return outs if comm is None else (outs, moved)


def _attn_bwd(S, H, q1, q1cb, k1, k1cb, v, vcb, o, do, lse, scale, name, q2=None, q2cb=None, k2=None, k2cb=None,
              tab=None, win=None, ccol=None, crow=None, comm=None):
    tq, tk = _soft_tiles(S)
    has2, hastab, hasc = q2 is not None, tab is not None, ccol is not None

    def body(*refs):
        it = iter(refs)
        q1r, k1r, vr, o_r, do_r, lse_r = (next(it) for _ in range(6))
        q2r, k2r = (next(it), next(it)) if has2 else (None, None)
        tabr = next(it) if hastab else None
        ccolr, crowr = (next(it), next(it)) if hasc else (None, None)
        dq1_r, dk1_r, dv_r = next(it), next(it), next(it)
        dq2_r, dk2_r = (next(it), next(it)) if has2 else (None, None)
        dcr_r = next(it) if hasc else None
        i = pl.program_id(1)

        @pl.when(i == 0)
        def _():
            dk1_r[...] = jnp.zeros_like(dk1_r)
            dv_r[...] = jnp.zeros_like(dv_r)
            if has2:
                dk2_r[...] = jnp.zeros_like(dk2_r)
            if hasc:
                dcr_r[...] = jnp.zeros_like(dcr_r)

        q = q1r[...]
        qb2 = q2r[...] if has2 else None
        dob = do_r[...]
        delta = jnp.sum(dob.astype(F32) * o_r[...], axis=1, keepdims=True)
        lse_c = lse_r[:, 0:1]
        cq = ccolr[:, 0:1] if hasc else None
        qpos = i * tq + lax.broadcasted_iota(jnp.int32, (tq, tk), 0)
        kio = lax.broadcasted_iota(jnp.int32, (tq, tk), 1)
        j_diag = (i * tq) // tk
        j_lo = jnp.maximum((i * tq - win) // tk, 0) if win else 0

        def probs(j, masked):
            off = pl.multiple_of(j * tk, tk)
            kb = k1r[pl.ds(off, tk), :]
            s = _dot_nt(q, kb)
            kb2 = None
            if has2:
                kb2 = k2r[pl.ds(off, tk), :]
                s = s + _dot_nt(qb2, kb2)
            s = s * scale
            if hastab:
                s = s + tabr[i - j * (tk // tq)]
            else:
                if hasc:
                    s = s + (cq - _key_row(crowr, j, tk))
                if masked:
                    s = jnp.where(kio + j * tk <= qpos, s, NEG)
            p = jnp.exp(s - lse_c)
            dp = _dot_nt(dob, vr[pl.ds(off, tk), :])
            return off, kb, kb2, p, dp

        def sweep(fn, carry):
            if hastab:
                return lax.fori_loop(j_lo, j_diag + 1, functools.partial(fn, masked=False), carry)
            carry = lax.fori_loop(j_lo, j_diag, functools.partial(fn, masked=False), carry)
            return fn(j_diag, carry, True)

        if hasc:
            def dstep(j, acc, masked):
                _, _, _, p, dp = probs(j, masked)
                return acc + jnp.sum(p * dp, axis=1, keepdims=True)

            delta = sweep(dstep, jnp.zeros((tq, 1), F32))

        def step(j, carry, masked):
            dq, dq2 = carry
            off, kb, kb2, p, dp = probs(j, masked)
            ds = p * (dp - delta)
            dsb = ds.astype(BF16)
            dq = dq + _dot_nn(dsb, kb)
            dk1_r[pl.ds(off, tk), :] += _dot_tn(dsb, q) * scale
            dv_r[pl.ds(off, tk), :] += _dot_tn(p.astype(BF16), dob)
            if has2:
                dq2 = dq2 + _dot_nn(dsb, kb2)
                dk2_r[pl.ds(off, tk), :] += _dot_tn(dsb, qb2) * scale
            if hasc:
                cs = -jnp.sum(ds, axis=0, keepdims=True)
                for c in range(tk // TK):
                    dcr_r[j * (tk // TK) + c] += cs[:, c * TK:(c + 1) * TK]
            return dq, dq2

        z = jnp.zeros((tq, LANE), F32)
        dq, dq2 = sweep(step, (z, z))
        dq1_r[...] = dq * scale
        if has2:
            dq2_r[...] = dq2 * scale

    qspec = _colspec(tq, lambda h, i: (i, h))
    kspec = _colspec(S, lambda h, i: (0, h))
    ops = [q1, k1, v, o, do, lse]
    specs = [_colspec(tq, lambda h, i: (i, q1cb(h))), _colspec(S, lambda h, i: (0, k1cb(h))),
             _colspec(S, lambda h, i: (0, vcb(h))), qspec, qspec, qspec]
    if has2:
        ops += [q2, k2]
        specs += [_colspec(tq, lambda h, i: (i, q2cb(h))), _colspec(S, lambda h, i: (0, k2cb(h)))]
    if hastab:
        ops.append(tab)
        specs.append(pl.BlockSpec(tab.shape, lambda h, i: (0, 0, 0)))
    if hasc:
        ops += [ccol, crow]
        specs += [qspec, pl.BlockSpec((None, S // TK, 1, TK), lambda h, i: (h, 0, 0, 0))]
    assert all(t.dtype == BF16 for t in ops[:3] + [do] + ([q2, k2] if has2 else []))
    out_specs = [qspec, kspec, kspec] + ([qspec, kspec] if has2 else [])
    shp = jax.ShapeDtypeStruct((S, H * LANE), F32)
    out_shape = [shp] * len(out_specs)
    if hasc:
        out_specs.append(pl.BlockSpec((None, S // TK, 1, TK), lambda h, i: (h, 0, 0, 0)))
        out_shape.append(jax.ShapeDtypeStruct((H, S // TK, 1, TK), F32))
    outs, moved = _pcall(
        body, name=name, grid=(H, S // tq), in_specs=specs, out_specs=out_specs, out_shape=out_shape,
        operands=ops, sem=("parallel", "arbitrary"), comm=comm)
    return outs if comm is None else (outs, moved)


def _scan_matrix(kind):
    j = np.arange(TK)[:, None]
    s = np.arange(TK)[None, :]
    tri = {"suffix_ex": j > s, "prefix_in": j <= s, "prefix_ex": j < s}[kind].astype(np.float32)
    half = np.concatenate([tri, np.ones((TK, TK), np.float32)], axis=1)
    return jnp.asarray(np.concatenate([half, half], axis=0), BF16)


def _scan_mxu(x, mat, carry, reverse):
    n = x.shape[1] // TK
    hi = x.astype(BF16)
    lo = (x - hi.astype(F32)).astype(BF16)
    parts = [None] * n
    for b in (reversed(range(n)) if reverse else range(n)):
        sl = slice(b * TK, (b + 1) * TK)
        r = _dot_nn(jnp.concatenate([hi[:, sl], lo[:, sl]], axis=1), mat)
        parts[b] = r[:, :TK] + carry
        carry = carry + r[:, TK:]
    return jnp.concatenate(parts, axis=1), carry


def _stick_logs(z):
    e = jnp.exp(-jnp.abs(z))
    return e, -jnp.maximum(z, 0.0) - jnp.log(1.0 + e)


def _stick_fwd(S, H, x, qcb, kcb, vcb, scale, name, comm=None):
    tq, tk = _soft_tiles(S)
    assert x.dtype == BF16

    def body(q_r, k_r, v_r, mat_r, o_ref, t_ref):
        i = pl.program_id(1)
        q = q_r[...]
        qpos = i * tq + lax.broadcasted_iota(jnp.int32, (tq, tk), 0)
        lane = lax.broadcasted_iota(jnp.int32, (tq, tk), 1)
        j_diag = (i * tq) // tk

        def step(j, carry, masked):
            c, acc = carry
            off = pl.multiple_of(j * tk, tk)
            z = _dot_nt(q, k_r[pl.ds(off, tk), :]) * scale
            _, lk = _stick_logs(z)
            if masked:
                past = lane + j * tk < qpos
                lk = jnp.where(past, lk, 0.0)
            suf, c = _scan_mxu(lk, mat_r[...], c, True)
            a = jnp.exp(z + lk + suf)
            if masked:
                a = jnp.where(past, a, 0.0)
            acc = acc + _dot_nn(a.astype(BF16), v_r[pl.ds(off, tk), :])
            return c, acc

        carry = step(j_diag, (jnp.zeros((tq, TK), F32), jnp.zeros((tq, LANE), F32)), True)
        c, acc = lax.fori_loop(0, j_diag, lambda jj, cr: step(j_diag - 1 - jj, cr, False), carry)
        o_ref[...] = acc
        t_ref[...] = c

    o_spec = _colspec(tq, lambda h, i: (i, h))
    shp = jax.ShapeDtypeStruct((S, H * LANE), F32)
    mat = _scan_matrix("suffix_ex")
    outs, moved = _pcall(
        body, name=name, grid=(H, S // tq),
        in_specs=[_colspec(tq, lambda h, i: (i, qcb(h))), _colspec(S, lambda h, i: (0, kcb(h))),
                  _colspec(S, lambda h, i: (0, vcb(h))), pl.BlockSpec(mat.shape, lambda h, i: (0, 0))],
        out_specs=[o_spec, o_spec], out_shape=[shp, shp],
        operands=[x, x, x, mat], sem=("parallel", "arbitrary"), comm=comm)
    return outs if comm is None else (outs, moved)


def _stick_bwd(S, H, x, qcb, kcb, vcb, do, tot, scale, name, comm=None):
    tq, tk = _soft_tiles(S)
    assert x.dtype == BF16 and do.dtype == BF16

    def body(q_r, k_r, v_r, do_r, t_r, pin_r, pex_r, dq_r, dk_r, dv_r):
        i = pl.program_id(1)

        @pl.when(i == 0)
        def _():
            dk_r[...] = jnp.zeros_like(dk_r)
            dv_r[...] = jnp.zeros_like(dv_r)

        q = q_r[...]
        dob = do_r[...]
        total = jnp.concatenate([t_r[...]] * (tk // TK), axis=1)
        qpos = i * tq + lax.broadcasted_iota(jnp.int32, (tq, tk), 0)
        lane = lax.broadcasted_iota(jnp.int32, (tq, tk), 1)
        j_diag = (i * tq) // tk

        def step(j, carry, masked):
            cl, cg, dq = carry
            off = pl.multiple_of(j * tk, tk)
            kb = k_r[pl.ds(off, tk), :]
            z = _dot_nt(q, kb) * scale
            e, lk = _stick_logs(z)
            if masked:
                past = lane + j * tk < qpos
                lk = jnp.where(past, lk, 0.0)
            pre, cl = _scan_mxu(lk, pin_r[...], cl, False)
            a = jnp.exp(z + lk + (total - pre))
            if masked:
                a = jnp.where(past, a, 0.0)
            g = _dot_nt(dob, v_r[pl.ds(off, tk), :]) * a
            gpre, cg = _scan_mxu(g, pex_r[...], cg, False)
            inv = 1.0 / (1.0 + e)
            small = e * inv
            pos = z >= 0
            dz = g * jnp.where(pos, small, inv) - jnp.where(pos, inv, small) * gpre
            if masked:
                dz = jnp.where(past, dz, 0.0)
            dzb = dz.astype(BF16)
            dk_r[pl.ds(off, tk), :] += _dot_tn(dzb, q) * scale
            dv_r[pl.ds(off, tk), :] += _dot_tn(a.astype(BF16), dob)
            return cl, cg, dq + _dot_nn(dzb, kb)

        zt = jnp.zeros((tq, TK), F32)
        carry = lax.fori_loop(0, j_diag, functools.partial(step, masked=False), (zt, zt, jnp.zeros((tq, LANE), F32)))
        dq_r[...] = step(j_diag, carry, True)[2] * scale

    qspec = _colspec(tq, lambda h, i: (i, h))
    kspec = _colspec(S, lambda h, i: (0, h))
    shp = jax.ShapeDtypeStruct((S, H * LANE), F32)
    pin, pex = _scan_matrix("prefix_in"), _scan_matrix("prefix_ex")
    mspec = pl.BlockSpec(pin.shape, lambda h, i: (0, 0))
    outs, moved = _pcall(
        body, name=name, grid=(H, S // tq),
        in_specs=[_colspec(tq, lambda h, i: (i, qcb(h))), _colspec(S, lambda h, i: (0, kcb(h))),
                  _colspec(S, lambda h, i: (0, vcb(h))), qspec, qspec, mspec, mspec],
        out_specs=[qspec, kspec, kspec], out_shape=[shp] * 3,
        operands=[x, x, x, do, tot, pin, pex], sem=("parallel", "arbitrary"), comm=comm)
    return outs if comm is None else (outs, moved)


def _scan8(x, rows, reverse):
    for sh in (1, 2, 4):
        if reverse:
            x = x + jnp.where(rows + sh < 8, pltpu.roll(x, 8 - sh, 0), 0.0)
        else:
            x = x + jnp.where(rows >= sh, pltpu.roll(x, sh, 0), 0.0)
    return x


def _fox_prep(S, H, proj, fcb, bias, name):
    tk = TK

    def body(f_ref, b_ref, ccol_ref, crow_ref, scr):
        rows = lax.broadcasted_iota(jnp.int32, (8, LANE), 0)

        def step(t, carry):
            off = pl.multiple_of(t * 8, 8)
            xb = f_ref[pl.ds(off, 8), :] + b_ref[...]
            lf = jnp.minimum(xb, 0.0) - jnp.log(1.0 + jnp.exp(-jnp.abs(xb)))
            lf = _scan8(lf, rows, False) + carry
            scr[pl.ds(off, 8), :] = lf
            return lf[7:8, :]

        lax.fori_loop(0, S // 8, step, jnp.zeros((1, LANE), F32))
        for h in range(H):
            ccol_ref[:, h * LANE:(h + 1) * LANE] = jnp.broadcast_to(scr[:, h:h + 1], (S, LANE))

            def tr(t, _):
                off = pl.multiple_of(t * tk, tk)
                blk = ccol_ref[pl.ds(off, tk), h * LANE:(h + 1) * LANE]
                crow_ref[h, t] = blk.T[0:1, :]
                return 0

            lax.fori_loop(0, S // tk, tr, 0)

    return pl.pallas_call(
        body, name=name, grid=(1,),
        in_specs=[_colspec(S, lambda i: (0, fcb)), pl.BlockSpec((1, LANE), lambda i: (0, 0))],
        out_specs=[pl.BlockSpec((S, H * LANE), lambda i: (0, 0)),
                   pl.BlockSpec((H, S // tk, 1, tk), lambda i: (0, 0, 0, 0))],
        out_shape=[jax.ShapeDtypeStruct((S, H * LANE), F32), jax.ShapeDtypeStruct((H, S // tk, 1, tk), F32)],
        scratch_shapes=[pltpu.VMEM((S, LANE), F32)],
        compiler_params=_cp("arbitrary"),
    )(proj, bias)


def _fox_bwd(S, H, proj, fcb, bias, dcr, name):
    tk = TK

    def body(f_ref, b_ref, dcr_ref, df_ref, db_ref, scr):
        rows = lax.broadcasted_iota(jnp.int32, (8, LANE), 0)
        lane_t = lax.broadcasted_iota(jnp.int32, (tk, LANE), 1)
        nb = S // 8

        def tr(t, _):
            off = pl.multiple_of(t * tk, tk)
            d = jnp.zeros((tk, LANE), F32)
            for h in range(H):
                d = d + jnp.where(lane_t == h, jnp.broadcast_to(dcr_ref[h, t], (LANE, tk)).T, 0.0)
            scr[pl.ds(off, tk), :] = d
            return 0

        lax.fori_loop(0, S // tk, tr, 0)

        def step(tt, carry):
            suffix, db = carry
            off = pl.multiple_of((nb - 1 - tt) * 8, 8)
            d = _scan8(scr[pl.ds(off, 8), :], rows, True) + suffix
            xb = f_ref[pl.ds(off, 8), :] + b_ref[...]
            e = jnp.exp(-jnp.abs(xb))
            dx = d * jnp.where(xb >= 0, e, 1.0) / (1.0 + e)
            df_ref[pl.ds(off, 8), :] = dx
            return d[0:1, :], db + jnp.sum(dx, axis=0, keepdims=True)

        z = jnp.zeros((1, LANE), F32)
        _, db = lax.fori_loop(0, nb, step, (z, z))
        db_ref[...] = db

    return pl.pallas_call(
        body, name=name, grid=(1,),
        in_specs=[_colspec(S, lambda i: (0, fcb)), pl.BlockSpec((1, LANE), lambda i: (0, 0)),
                  pl.BlockSpec((H, S // tk, 1, tk), lambda i: (0, 0, 0, 0))],
        out_specs=[pl.BlockSpec((S, LANE), lambda i: (0, 0)), pl.BlockSpec((1, LANE), lambda i: (0, 0))],
        out_shape=[jax.ShapeDtypeStruct((S, LANE), F32), jax.ShapeDtypeStruct((1, LANE), F32)],
        scratch_shapes=[pltpu.VMEM((S, LANE), F32)],
        compiler_params=_cp("arbitrary"),
    )(proj, bias, dcr)


def _adamw(w, slots, m, v, name, block_bytes=1 << 20):
    R, C = w.shape
    tr = R
    while tr * C * 4 > block_bytes and tr % 16 == 0:
        tr //= 2
    c1 = 1.0 - ADAM_B1 ** ADAM_STEP
    c2 = 1.0 - ADAM_B2 ** ADAM_STEP

    def body(w_ref, s_ref, m_ref, v_ref, g_ref, d_ref, nm_ref, nv_ref):
        g = s_ref[0].astype(F32)
        for s in range(1, NDEV):
            g = g + s_ref[s].astype(F32)
        mn = ADAM_B1 * m_ref[...] + (1.0 - ADAM_B1) * g
        vn = ADAM_B2 * v_ref[...] + (1.0 - ADAM_B2) * (g * g)
        g_ref[...] = g
        nm_ref[...] = mn
        nv_ref[...] = vn
        d_ref[...] = -ADAM_LR * ((mn / c1) / (jnp.sqrt(vn / c2) + ADAM_EPS) + ADAM_WD * w_ref[...])

    row = pl.BlockSpec((tr, C), lambda i: (i, 0))
    return pl.pallas_call(
        body, name=name, grid=(R // tr,),
        in_specs=[row, pl.BlockSpec((NDEV, tr, C), lambda i: (0, i, 0)), row, row],
        out_specs=[row] * 4, out_shape=[jax.ShapeDtypeStruct((R, C), F32)] * 4,
        compiler_params=_cp("parallel"),
    )(w, slots, m, v)


class _Layout:
    def __init__(self, D):
        self.GW = GW = D // 4
        self.H = H = GW // HEAD
        self.QL, self.KVL = 0, Q_LORA
        base = Q_LORA + KV_LORA
        (self.QB, self.KB, self.VB, self.QC, self.KC, self.VC, self.QD, self.KD, self.VD) = (
            base + k * GW for k in range(9))
        self.KR = base + 9 * GW
        self.FC = self.KR + LANE
        self.PW = -(-(self.FC + LANE) // 512) * 512
        self.o_kr = base
        self.o_bc = base + QK_ROPE
        self.o_fc = self.o_bc + 6 * GW
        self.o_d = self.o_fc + H
        self.IN = self.o_d + 3 * GW

    def pad(self, w):
        z = lambda n: jnp.zeros(w.shape[:-1] + (n,), w.dtype)
        return jnp.concatenate([
            w[..., :self.o_kr], w[..., self.o_bc:self.o_fc], w[..., self.o_d:self.IN],
            w[..., self.o_kr:self.o_bc], z(LANE - QK_ROPE), w[..., self.o_fc:self.o_d], z(LANE - self.H),
            z(self.PW - self.FC - LANE)], axis=-1)

    def unpad(self, g):
        return jnp.concatenate([
            g[..., :self.KR - 9 * self.GW], g[..., self.KR:self.KR + QK_ROPE], g[..., self.QB:self.QD],
            g[..., self.FC:self.FC + self.H], g[..., self.QD:self.KR]], axis=-1)


def _rope_tables(S):
    pos = jnp.arange(S, dtype=F32)

    def cs(dim):
        inv = ROPE_THETA ** (-jnp.arange(0, dim, 2, dtype=F32) / dim)
        ang = pos[:, None] * inv[None, :]
        return jnp.cos(ang), jnp.sin(ang)

    c, s = cs(HEAD)
    full = (jnp.concatenate([c, c], 1), jnp.concatenate([-s, s], 1))
    c, s = cs(QK_ROPE)
    z = jnp.zeros((S, LANE - QK_ROPE), F32)
    half = (jnp.concatenate([c, c, z], 1), jnp.concatenate([-s, s, z], 1))
    return full, half


def _dilated_table(tq, tk):
    win = max(w for w, _ in DILATED_PAIRS)
    nd = (win + tk) // tq + 1
    d = np.arange(nd)[:, None, None] * tq + np.arange(tq)[None, :, None] - np.arange(tk)[None, None, :]
    mult = np.zeros(d.shape, np.float64)
    for w, dil in DILATED_PAIRS:
        mult += (d >= 0) & (d <= w) & (d % dil == 0)
    return jnp.asarray(np.where(mult > 0, np.log(np.maximum(mult, 1.0)), NEG), F32), win


def _pack(arrs):
    rows = []
    for a in arrs:
        f = a.reshape(-1).astype(F32)
        f = jnp.pad(f, (0, (-f.shape[0]) % LANE))
        rows.append(f.reshape(-1, LANE))
    p = jnp.concatenate(rows, 0)
    return jnp.pad(p, ((0, (-p.shape[0]) % 8), (0, 0)))


def _unpack(p, shapes):
    out, r = [], 0
    for shp in shapes:
        n = int(np.prod(shp))
        nr = -(-n // LANE)
        out.append(p[r:r + nr].reshape(-1)[:n].reshape(shp))
        r += nr
    return out


def kernel(x, attn_norm, w_in, mla_q_norm, w_uq, mla_kv_norm, w_ukv, fox_forget_bias, group_norm, w_out, ffn_norm, w_gate, w_up, w_down, final_norm, loss_target, m_attn_norm, m_w_in, m_mla_q_norm, m_w_uq, m_mla_kv_norm, m_w_ukv, m_fox_forget_bias, m_group_norm, m_w_out, m_ffn_norm, m_w_gate, m_w_up, m_w_down, m_final_norm, v_attn_norm, v_w_in, v_mla_q_norm, v_w_uq, v_mla_kv_norm, v_w_ukv, v_fox_forget_bias, v_group_norm, v_w_out, v_ffn_norm, v_w_gate, v_w_up, v_w_down, v_final_norm):
    _, S, D = x.shape
    L = attn_norm.shape[0]
    lay = _Layout(D)
    H, GW, PW = lay.H, lay.GW, lay.PW
    FB = w_gate.shape[2]
    QKA = HEAD + QK_ROPE
    x = x[0]
    target = loss_target[0]
    rope_full, rope_half = _rope_tables(S)
    neg = lambda t: (t[0], -t[1])
    tab, win = _dilated_table(*_soft_tiles(S))
    cb = lambda col: col // LANE

    sh = dict(w_in=lay.pad(w_in).astype(BF16),
              **{n: w.astype(BF16) for n, w in (("w_uq", w_uq), ("w_ukv", w_ukv), ("w_out", w_out),
                                                  ("w_gate", w_gate), ("w_up", w_up), ("w_down", w_down))})
    first3 = ["w_in", "w_uq", "w_ukv"]

    def first_weights(g):
        wuq = jnp.transpose(g[1], (1, 0, 2)).reshape(Q_LORA, H, QKA)
        wuq = jnp.pad(wuq, ((0, 0), (0, 0), (0, 2 * LANE - QKA))).reshape(Q_LORA, H * 2 * LANE)
        return dict(win=g[0].reshape(D, PW), wuq=wuq,
                    wukv=jnp.transpose(g[2], (1, 0, 2)).reshape(KV_LORA, H * 2 * LANE))

    def row(a):
        return a.reshape(1, -1)

    def forward(l, x0, W):
        A = dict(x0=x0)
        A["bias"] = jnp.pad(row(fox_forget_bias[l]), ((0, 0), (0, LANE - H)))
        h1 = A["h1"] = _rms_fwd(x0, row(attn_norm[l]), D, 0, BF16, "attn_norm")
        (proj, pb), (W["wg"],) = _mm(h1, W["win"], "in_proj", out_dtype=(F32, BF16),
                                     comm=("gather", [sh["w_gate"][l]]))
        A["proj"], A["pb"] = proj, pb
        qln = A["qln"] = _rms_fwd(proj, row(mla_q_norm[l]), Q_LORA, cb(lay.QL) // 4, BF16, "q_norm")
        kvln = A["kvln"] = _rms_fwd(proj, row(mla_kv_norm[l]), KV_LORA, cb(lay.KVL) // 4, BF16, "kv_norm")
        qa, qab = _mm(qln, W["wuq"], "q_up", out_dtype=(F32, BF16))
        A["qab"] = qab
        kv = A["kv"] = _mm(kvln, W["wukv"], "kv_up", out_dtype=BF16)
        q_pe = A["q_pe"] = _rope(qa, (1, H, 2), *rope_half, True, BF16, "rope_q_mla")
        k_pe = A["k_pe"] = _rope(proj, (cb(lay.KR), 1, 1), *rope_half, True, BF16, "rope_k_mla")
        (A["o_a"], A["lse_a"]), (W["wu"],) = _attn_fwd(
            S, H, qab, lambda h: 2 * h, kv, lambda h: 2 * h, kv, lambda h: 2 * h + 1, QKA ** -0.5, "mla_fwd",
            q2=q_pe, q2cb=lambda h: h, k2=k_pe, k2cb=lambda h: 0, comm=("gather", [sh["w_up"][l]]))
        qk_b = A["qk_b"] = _rope(proj, (cb(lay.QB), 2 * H, 1), *rope_full, False, BF16, "rope_qk_dil")
        (A["o_b"], A["lse_b"]), (g_down,) = _attn_fwd(
            S, H, qk_b, lambda h: h, qk_b, lambda h: H + h, pb, lambda h: cb(lay.VB) + h, HEAD ** -0.5,
            "dilated_fwd", tab=tab, win=win, comm=("gather", [sh["w_down"][l]]))
        W["wd"] = g_down.reshape(NDEV * FB, D)
        ccol, crow = A["ccol"], A["crow"] = _fox_prep(S, H, proj, cb(lay.FC), A["bias"], "fox_prep")
        (A["o_c"], A["lse_c"]), (g_out,) = _attn_fwd(
            S, H, pb, lambda h: cb(lay.QC) + h, pb, lambda h: cb(lay.KC) + h, pb, lambda h: cb(lay.VC) + h,
            HEAD ** -0.5, "fox_fwd", ccol=ccol, crow=crow, comm=("gather", [sh["w_out"][l]]))
        W["wout"] = g_out.reshape(4 * GW, D)
        A["o_d"], A["tot_d"] = _stick_fwd(
            S, H, pb, lambda h: cb(lay.QD) + h, lambda h: cb(lay.KD) + h, lambda h: cb(lay.VD) + h,
            HEAD ** -0.5, "stick_fwd")
        mix = A["mix"] = _gn_fwd([A["o_a"], A["o_b"], A["o_c"], A["o_d"]], row(group_norm[l]), "group_norm")
        x1 = A["x1"] = _mm(mix, W["wout"], "out_proj", res=x0)
        h2 = A["h2"] = _rms_fwd(x1, row(ffn_norm[l]), D, 0, BF16, "ffn_norm")
        nxt = None
        if l + 1 < L:
            (A["g"], A["u"], A["act"]), nxt = _ffn_up(
                h2, W["wg"], W["wu"], "ffn_up", comm=("gather", [sh[n][l + 1] for n in first3]))
        else:
            A["g"], A["u"], A["act"] = _ffn_up(h2, W["wg"], W["wu"], "ffn_up_last")
        return _mm_down(A["act"], W["wd"], x1, "ffn_down"), A, nxt

    def backward(l, dx2, dx2b, W, A, late):
        proj, pb = A["proj"], A["pb"]
        G, small, got = {}, {}, {}
        dgate, dup = _ffn_dact(dx2b, W["wd"], A["g"], A["u"], "ffn_dact")
        G["w_down"] = _mm_dwdown(A["act"], dx2b, "dw_down").reshape(NDEV, FB, D)
        dh2, (got[l, "w_down"],) = _mm_dh2(dgate, W["wg"], dup, W["wu"], "ffn_dh", comm=("exchange", [G["w_down"]]))
        G["w_gate"] = _mm_dwgate(A["h2"], dgate, "dw_gate")
        G["w_up"] = _mm_dwgate(A["h2"], dup, "dw_up")
        dx1, dx1b, small["ffn_norm"] = _rms_bwd(A["x1"], row(ffn_norm[l]), dh2, D, 0, "ffn_norm_bwd", res=dx2)
        dmix = _mm(dx1b, W["wout"], "out_proj_dx", tb=True)
        G["w_out"] = _mm(A["mix"], dx1b, "dw_out", ta=True, out_dtype=BF16).reshape(NDEV, 4 * GW // NDEV, D)
        do_a, do_b, do_c, do_d, small["group_norm"] = _gn_bwd(
            [A["o_a"], A["o_b"], A["o_c"], A["o_d"]], row(group_norm[l]), dmix, "group_norm_bwd")
        (dq_d, dk_d, dv_d), (got[l, "w_up"],) = _stick_bwd(
            S, H, pb, lambda h: cb(lay.QD) + h, lambda h: cb(lay.KD) + h, lambda h: cb(lay.VD) + h,
            do_d, A["tot_d"], HEAD ** -0.5, "stick_bwd", comm=("exchange", [G["w_up"]]))
        (dq_c, dk_c, dv_c, dcc), moved = _attn_bwd(
            S, H, pb, lambda h: cb(lay.QC) + h, pb, lambda h: cb(lay.KC) + h, pb, lambda h: cb(lay.VC) + h,
            A["o_c"], do_c, A["lse_c"], HEAD ** -0.5, "fox_bwd" if late else "fox_bwd_top",
            ccol=A["ccol"], crow=A["crow"], comm=("exchange", [G["w_out"]] + (late or [])))
        got[l, "w_out"] = moved[0]
        got.update({(l + 1, n): s for n, s in zip(first3, moved[1:])})
        dfc, dbias = _fox_bwd(S, H, proj, cb(lay.FC), A["bias"], dcc, "fox_gate_bwd")
        small["fox_forget_bias"] = dbias[0, :H]
        qk_b = A["qk_b"]
        dq_b, dk_b, dv_b = _attn_bwd(
            S, H, qk_b, lambda h: h, qk_b, lambda h: H + h, pb, lambda h: cb(lay.VB) + h,
            A["o_b"], do_b, A["lse_b"], HEAD ** -0.5, "dilated_bwd", tab=tab, win=win)
        dqk_b = _rope(jnp.concatenate([dq_b, dk_b], 1), (0, 2 * H, 1), *neg(rope_full), False, BF16, "rope_qk_dil_bwd")
        qab, kv = A["qab"], A["kv"]
        (dq1, dk1, dv_a, dq2, dk2), (got[l, "w_gate"],) = _attn_bwd(
            S, H, qab, lambda h: 2 * h, kv, lambda h: 2 * h, kv, lambda h: 2 * h + 1,
            A["o_a"], do_a, A["lse_a"], QKA ** -0.5, "mla_bwd",
            q2=A["q_pe"], q2cb=lambda h: h, k2=A["k_pe"], k2cb=lambda h: 0, comm=("exchange", [G["w_gate"]]))
        dq2 = _rope(dq2, (0, H, 1), *neg(rope_half), True, F32, "rope_q_mla_bwd")
        dk_pe = _rope(dk2.reshape(S, H, LANE).sum(1), (0, 1, 1), *neg(rope_half), True, BF16, "rope_k_mla_bwd")
        dqa = jnp.stack([dq1.reshape(S, H, LANE), dq2.reshape(S, H, LANE)], 2).reshape(S, H * 2 * LANE).astype(BF16)
        dkv = jnp.stack([dk1.reshape(S, H, LANE), dv_a.reshape(S, H, LANE)], 2).reshape(S, H * 2 * LANE).astype(BF16)
        dwuq = _mm(A["qln"], dqa, "dw_uq", ta=True, out_dtype=BF16)
        dwuq = dwuq.reshape(Q_LORA, H, 2 * LANE)[:, :, :QKA].reshape(Q_LORA, NDEV, H * QKA // NDEV)
        G["w_uq"] = jnp.transpose(dwuq, (1, 0, 2))
        dwukv = _mm(A["kvln"], dkv, "dw_ukv", ta=True, out_dtype=BF16).reshape(KV_LORA, NDEV, H * 2 * LANE // NDEV)
        G["w_ukv"] = jnp.transpose(dwukv, (1, 0, 2))
        dqln = _mm(dqa, W["wuq"], "q_up_dx", tb=True)
        dkvln = _mm(dkv, W["wukv"], "kv_up_dx", tb=True)
        _, dql, small["mla_q_norm"] = _rms_bwd(proj, row(mla_q_norm[l]), dqln, Q_LORA, cb(lay.QL) // 4, "q_norm_bwd")
        _, dkvl, small["mla_kv_norm"] = _rms_bwd(
            proj, row(mla_kv_norm[l]), dkvln, KV_LORA, cb(lay.KVL) // 4, "kv_norm_bwd")
        bf = lambda t: t.astype(BF16)
        dproj = jnp.concatenate([
            dql, dkvl, dqk_b, bf(dv_b), bf(dq_c), bf(dk_c), bf(dv_c), bf(dq_d), bf(dk_d), bf(dv_d),
            dk_pe, bf(dfc), jnp.zeros((S, PW - lay.FC - LANE), BF16)], axis=1)
        G["w_in"] = _mm(A["h1"], dproj, "dw_in", ta=True, out_dtype=BF16).reshape(NDEV, D // NDEV, PW)
        late = [G[n] for n in first3]
        if l == 0:
            dh1, moved = _mm(dproj, W["win"], "in_proj_dx_last", tb=True, comm=("exchange", late))
            got.update({(0, n): s for n, s in zip(first3, moved)})
        else:
            dh1 = _mm(dproj, W["win"], "in_proj_dx", tb=True)
        dx0, dx0b, small["attn_norm"] = _rms_bwd(A["x0"], row(attn_norm[l]), dh1, D, 0, "attn_norm_bwd", res=dx1)
        return dx0, dx0b, late, got, small

    big = first3 + ["w_out", "w_gate", "w_up", "w_down"]
    Ws, As = [], []
    xc = x
    nxt = _comm_alone("gather", [sh[n][0] for n in first3], "gather_first")
    for l in range(L):
        W = first_weights(nxt)
        xc, A, nxt = forward(l, xc, W)
        Ws.append(W)
        As.append(A)
    dx, loss_part = _final_loss(xc, row(final_norm), target, "final_loss")
    dx, dxb, dfinal = _rms_bwd(xc, row(final_norm), dx, D, 0, "final_norm_bwd")
    slots = {}
    smalls = [None] * L
    late = None
    for l in reversed(range(L)):
        dx, dxb, late, got, smalls[l] = backward(l, dx, dxb, Ws[l], As[l], late)
        slots.update(got)

    names_small = ["attn_norm", "mla_q_norm", "mla_kv_norm", "fox_forget_bias", "group_norm", "ffn_norm"]
    params = dict(attn_norm=attn_norm, mla_q_norm=mla_q_norm, mla_kv_norm=mla_kv_norm, fox_forget_bias=fox_forget_bias,
                  group_norm=group_norm, ffn_norm=ffn_norm, final_norm=final_norm, w_in=w_in, w_uq=w_uq, w_ukv=w_ukv,
                  w_out=w_out, w_gate=w_gate, w_up=w_up, w_down=w_down)
    moms = dict(attn_norm=(m_attn_norm, v_attn_norm), mla_q_norm=(m_mla_q_norm, v_mla_q_norm),
                mla_kv_norm=(m_mla_kv_norm, v_mla_kv_norm), fox_forget_bias=(m_fox_forget_bias, v_fox_forget_bias),
                group_norm=(m_group_norm, v_group_norm), ffn_norm=(m_ffn_norm, v_ffn_norm),
                final_norm=(m_final_norm, v_final_norm), w_in=(m_w_in, v_w_in), w_uq=(m_w_uq, v_w_uq),
                w_ukv=(m_w_ukv, v_w_ukv), w_out=(m_w_out, v_w_out), w_gate=(m_w_gate, v_w_gate),
                w_up=(m_w_up, v_w_up), w_down=(m_w_down, v_w_down))
    small_list = names_small + ["final_norm"]
    small_grads = [jnp.stack([smalls[l][n].reshape(params[n].shape[1:]) for l in range(L)]) for n in names_small]
    small_grads.append(dfinal.reshape(final_norm.shape))
    shapes = [params[n].shape for n in small_list] + [(LANE,)]
    packed_g = _comm_alone("gather", [_pack(small_grads + [loss_part.reshape(LANE)])], "gather_small")[0]
    zero = jnp.zeros((LANE,), F32)
    res_small = _adamw(_pack([params[n] for n in small_list] + [zero]), packed_g,
                       _pack([moms[n][0] for n in small_list] + [zero]),
                       _pack([moms[n][1] for n in small_list] + [zero]), "adamw_small")
    unp = [_unpack(r, shapes) for r in res_small]
    out = {n: tuple(unp[k][i] for k in range(4)) for i, n in enumerate(small_list)}
    loss = unp[0][-1][0]

    for i, n in enumerate(big):
        st = jnp.stack([slots[l, n] for l in range(L)], axis=1)
        if n == "w_in":
            st = lay.unpad(st)
        C = st.shape[-1]
        st = st.reshape(NDEV, -1, C)
        w2 = params[n].reshape(-1, C)
        res = _adamw(w2, st, moms[n][0].reshape(-1, C), moms[n][1].reshape(-1, C), "adamw_" + n)
        out[n] = tuple(r.reshape(params[n].shape) for r in res)

    order = ["attn_norm", "w_in", "mla_q_norm", "w_uq", "mla_kv_norm", "w_ukv", "fox_forget_bias", "group_norm",
             "w_out", "ffn_norm", "w_gate", "w_up", "w_down", "final_norm"]
    return (loss, dx[None], *[out[n][0] for n in order], *[out[n][1] for n in order],
            *[out[n][2] for n in order], *[out[n][3] for n in order])
```

```python
import functools
import math

import numpy as np
import jax
import jax.numpy as jnp
from jax import lax
from jax.experimental import pallas as pl
from jax.experimental.pallas import tpu as pltpu

F32 = jnp.float32
BF16 = jnp.bfloat16
NDEV = 8
LANE = 128
HEAD = 128
Q_LORA = 512
KV_LORA = 512
QK_ROPE = 64
DILATED_PAIRS = ((128, 1), (512, 4), (2048, 16))
ROPE_THETA = 10000.0
EPS = 1e-6
NEG = -1e30
TQ = 256
TK = 128
TKS = 512
VMEM_LIMIT = 48 * 1024 * 1024
MM_OPERAND_BYTES = 20 * 1024 * 1024
FFN_BLOCKS_PER_STEP = 2
ADAM_LR, ADAM_B1, ADAM_B2, ADAM_EPS, ADAM_WD, ADAM_STEP = 0.001, 0.9, 0.999, 1e-08, 0.01, 10
MESH = pl.DeviceIdType.MESH
ANY = pl.BlockSpec(memory_space=pl.ANY)


def _cp(*sem):
    return pltpu.CompilerParams(dimension_semantics=sem, vmem_limit_bytes=VMEM_LIMIT)


def _dot(a, b, ca, cb):
    return lax.dot_general(a, b, (((ca,), (cb,)), ((), ())), preferred_element_type=F32)


def _dot_nn(a, b):
    return _dot(a, b, 1, 0)


def _dot_nt(a, b):
    return _dot(a, b, 1, 1)


def _dot_tn(a, b):
    return _dot(a, b, 0, 0)


def _tile(n, t):
    if n <= t:
        return n
    t -= t % LANE
    while n % t:
        t -= LANE
    return t


def _direct_copies(ins, outs, send_sems, recv_sems, local_sems, want_recvs=True):
    x, y, c = lax.axis_index("x"), lax.axis_index("y"), lax.axis_index("c")
    my_id = 4 * x + 2 * y + c
    local, sends, recvs = [], [], []
    for a in range(len(ins)):
        mine = ins[a].at[my_id]
        local.append(pltpu.make_async_copy(mine, outs[a].at[my_id], local_sems.at[a]))
        for k in range(1, NDEV):
            peer = (1 - x if k & 4 else x, 1 - y if k & 2 else y, 1 - c if k & 1 else c)
            pid = 4 * peer[0] + 2 * peer[1] + peer[2]
            sems = dict(send_sem=send_sems.at[a, k - 1], recv_sem=recv_sems.at[a, k - 1],
                        device_id=peer, device_id_type=MESH)
            sends.append(pltpu.make_async_remote_copy(src_ref=ins[a].at[pid], dst_ref=outs[a].at[my_id], **sems))
            if want_recvs:
                recvs.append(pltpu.make_async_remote_copy(src_ref=mine, dst_ref=outs[a].at[pid], **sems))
    return local, sends, recvs


def _comm_start(kind, ins, outs, send_sems, recv_sems, local_sems):
    if kind == "exchange":
        local, sends, _ = _direct_copies(ins, outs, send_sems, recv_sems, local_sems, want_recvs=False)
        for cp in local + sends:
            cp.start()
        return
    x, y, c = lax.axis_index("x"), lax.axis_index("y"), lax.axis_index("c")
    for a in range(len(ins)):
        mine = outs[a].at[4 * x + 2 * y + c]
        pltpu.make_async_copy(ins[a], mine, local_sems.at[a]).start()
        for k, to in enumerate([(x, y, 1 - c), (1 - x, y, c), (x, 1 - y, c), (1 - x, 1 - y, c)]):
            pltpu.make_async_remote_copy(src_ref=ins[a], dst_ref=mine, send_sem=send_sems.at[a, k],
                                         recv_sem=recv_sems.at[a, k], device_id=to, device_id_type=MESH).start()


def _comm_finish(kind, ins, outs, send_sems, recv_sems, local_sems):
    if kind == "exchange":
        local, sends, recvs = _direct_copies(ins, outs, send_sems, recv_sems, local_sems)
        for cp in recvs:
            cp.wait_recv()
        for cp in sends:
            cp.wait_send()
        for cp in local:
            cp.wait()
        return
    x, y, c = lax.axis_index("x"), lax.axis_index("y"), lax.axis_index("c")
    sibling = (x, y, 1 - c)
    chips = [(1 - x, y), (x, 1 - y), (1 - x, 1 - y)]
    for a in range(len(ins)):
        def copy(k, block, to):
            rows = outs[a].at[4 * block[0] + 2 * block[1] + block[2]]
            return pltpu.make_async_remote_copy(src_ref=rows, dst_ref=rows, send_sem=send_sems.at[a, k],
                                                recv_sem=recv_sems.at[a, k], device_id=to, device_id_type=MESH)

        passed = []
        for j, chip in enumerate(chips):
            copy(1 + j, (*chip, c), (x, y, c)).wait_recv()
            passed.append(copy(4 + j, (*chip, c), sibling))
            passed[-1].start()
        copy(0, sibling, (x, y, c)).wait_recv()
        for j, chip in enumerate(chips):
            copy(4 + j, (*chip, 1 - c), (x, y, c)).wait_recv()
        for k in range(4):
            copy(k, (x, y, c), sibling).wait_send()
        for cp in passed:
            cp.wait_send()
        pltpu.make_async_copy(ins[a], outs[a].at[4 * x + 2 * y + c], local_sems.at[a]).wait()


def _comm_shapes(kind, arrs):
    out_shape = [jax.ShapeDtypeStruct(((NDEV,) if kind == "gather" else ()) + a.shape, a.dtype) for a in arrs]
    n = len(arrs)
    sems = [pltpu.SemaphoreType.DMA((n, 7)), pltpu.SemaphoreType.DMA((n, 7)), pltpu.SemaphoreType.DMA((n,))]
    return out_shape, sems


def _comm_alone(kind, arrs, name):
    n = len(arrs)

    def body(*refs):
        _comm_start(kind, refs[:n], refs[n:2 * n], *refs[2 * n:])
        _comm_finish(kind, refs[:n], refs[n:2 * n], *refs[2 * n:])

    out_shape, sems = _comm_shapes(kind, arrs)
    return pl.pallas_call(body, name=name, out_shape=out_shape, in_specs=[ANY] * n, out_specs=[ANY] * n,
                          scratch_shapes=sems)(*arrs)


def _pcall(body, *, name, grid, in_specs, out_specs, out_shape, operands, sem, scratch_shapes=(), comm=None):
    in_specs, out_specs, out_shape = list(in_specs), list(out_specs), list(out_shape)
    scratch_shapes = list(scratch_shapes)
    if comm is None:
        res = pl.pallas_call(body, name=name, grid=grid, in_specs=in_specs, out_specs=out_specs, out_shape=out_shape,
                             scratch_shapes=scratch_shapes, compiler_params=_cp(*sem))(*operands)
        return list(res), []
    kind, arrs = comm
    nc, n_in, n_out, n_scr = len(arrs), len(operands), len(out_shape), len(scratch_shapes)
    c_shape, c_sems = _comm_shapes(kind, arrs)

    def carrier(*refs):
        ins, cin = refs[:n_in], refs[n_in:n_in + nc]
        outs = refs[n_in + nc:n_in + nc + n_out]
        cout = refs[n_in + nc + n_out:n_in + 2 * nc + n_out]
        scr = refs[n_in + 2 * nc + n_out:n_in + 2 * nc + n_out + n_scr]
        sems = refs[n_in + 2 * nc + n_out + n_scr:]
        pids = [pl.program_id(d) for d in range(len(grid))]
        first = functools.reduce(jnp.logical_and, [p == 0 for p in pids])
        last = functools.reduce(jnp.logical_and, [p == g - 1 for p, g in zip(pids, grid)])

        @pl.when(first)
        def _():
            _comm_start(kind, cin, cout, *sems)

        body(*ins, *outs, *scr)

        @pl.when(last)
        def _():
            _comm_finish(kind, cin, cout, *sems)

    res = pl.pallas_call(
        carrier, name=name, grid=grid, in_specs=in_specs + [ANY] * nc, out_specs=out_specs + [ANY] * nc,
        out_shape=out_shape + c_shape, scratch_shapes=scratch_shapes + c_sems,
        compiler_params=_cp(*["arbitrary"] * len(grid)))(*operands, *arrs)
    return list(res[:n_out]), list(res[n_out:])


def _mm_call(pairs, grid, a_spec, b_spec, o_spec, out_shape, acc_shape, nk, ca, cb, name,
             res=None, res_spec=None, comm=None):
    npairs = len(pairs)
    multi = isinstance(out_shape, (list, tuple))
    nout = len(out_shape) if multi else 1

    def body(*refs):
        ab = refs[:2 * npairs]
        r_ref = refs[2 * npairs] if res is not None else None
        o_refs, acc = refs[-1 - nout:-1], refs[-1]
        k = pl.program_id(2)

        @pl.when(k == 0)
        def _():
            acc[...] = jnp.zeros_like(acc)

        tot = None
        for p in range(npairs):
            av, bv = ab[2 * p][...].astype(BF16), ab[2 * p + 1][...].astype(BF16)
            if av.ndim == 2:
                terms = [(av, bv)]
            else:
                rows = bv.shape[0] // av.shape[0]
                terms = [(av[q], bv[q] if bv.ndim == 3 else bv[q * rows:(q + 1) * rows]) for q in range(av.shape[0])]
            for at, bt in terms:
                d = _dot(at, bt, ca, cb)
                tot = d if tot is None else tot + d
        acc[...] += tot

        @pl.when(k == nk - 1)
        def _():
            r = acc[...]
            if r_ref is not None:
                r = r + r_ref[...]
            for o_ref in o_refs:
                o_ref[...] = r.astype(o_ref.dtype)

    ops, specs = [], []
    for a, b in pairs:
        ops += [a, b]
        specs += [a_spec, b_spec]
    if res is not None:
        ops.append(res)
        specs.append(res_spec)
    outs, moved = _pcall(
        body, name=name, grid=grid, in_specs=specs, out_specs=[o_spec] * nout,
        out_shape=out_shape if multi else [out_shape], operands=ops,
        scratch_shapes=[pltpu.VMEM(acc_shape, F32)], sem=("parallel", "parallel", "arbitrary"), comm=comm)
    outs = outs if multi else outs[0]
    return outs if comm is None else (outs, moved)


def _k_tile(K, row_bytes, tk=2048):
    tk = _tile(K, tk)
    while 2 * tk * row_bytes > MM_OPERAND_BYTES and tk % 256 == 0:
        tk //= 2
    return tk


def _mm(a, b, name, ta=False, tb=False, out_dtype=F32, res=None, tm=1024, tn=1024, comm=None):
    M, K = (a.shape[1], a.shape[0]) if ta else a.shape
    N = b.shape[0] if tb else b.shape[1]
    tm, tn = _tile(M, tm), _tile(N, tn)
    tk = _k_tile(K, tm * a.dtype.itemsize + tn * b.dtype.itemsize)
    a_spec = pl.BlockSpec((tk, tm), lambda i, j, k: (k, i)) if ta else pl.BlockSpec((tm, tk), lambda i, j, k: (i, k))
    b_spec = pl.BlockSpec((tn, tk), lambda i, j, k: (j, k)) if tb else pl.BlockSpec((tk, tn), lambda i, j, k: (k, j))
    o_spec = pl.BlockSpec((tm, tn), lambda i, j, k: (i, j))
    if isinstance(out_dtype, tuple):
        out_shape = [jax.ShapeDtypeStruct((M, N), d) for d in out_dtype]
    else:
        out_shape = jax.ShapeDtypeStruct((M, N), out_dtype)
    return _mm_call([(a, b)], (M // tm, N // tn, K // tk), a_spec, b_spec, o_spec,
                    out_shape, (tm, tn), K // tk,
                    0 if ta else 1, 1 if tb else 0, name, res=res, res_spec=o_spec, comm=comm)


def _mm_down(act, wd, res, name, tm=1024, tn=1024):
    _, S, FB = act.shape
    D = wd.shape[1]
    tm, tn = _tile(S, tm), _tile(D, tn)
    o_spec = pl.BlockSpec((tm, tn), lambda i, j, k: (i, j))
    nb = FFN_BLOCKS_PER_STEP
    return _mm_call([(act, wd)], (S // tm, D // tn, NDEV // nb),
                    pl.BlockSpec((nb, tm, FB), lambda i, j, k: (k, i, 0)),
                    pl.BlockSpec((nb * FB, tn), lambda i, j, k: (k, j)), o_spec,
                    jax.ShapeDtypeStruct((S, D), F32), (tm, tn), NDEV // nb, 1, 0, name, res=res, res_spec=o_spec)


def _mm_dwdown(act, dy, name, tn=1024):
    _, S, FB = act.shape
    D = dy.shape[1]
    tn = _tile(D, tn)
    tk = _k_tile(S, FB * act.dtype.itemsize + tn * dy.dtype.itemsize)
    return _mm_call([(act, dy)], (NDEV, D // tn, S // tk),
                    pl.BlockSpec((None, tk, FB), lambda i, j, k: (i, k, 0)),
                    pl.BlockSpec((tk, tn), lambda i, j, k: (k, j)),
                    pl.BlockSpec((FB, tn), lambda i, j, k: (i, j)),
                    jax.ShapeDtypeStruct((NDEV * FB, D), BF16), (FB, tn), S // tk, 0, 0, name)


def _mm_dh2(dg, wg, du, wu, name, tm=1024, tn=1024, comm=None):
    _, S, FB = dg.shape
    D = wg.shape[1]
    tm, tn = _tile(S, tm), _tile(D, tn)
    nb = FFN_BLOCKS_PER_STEP
    return _mm_call([(dg, wg), (du, wu)], (S // tm, D // tn, NDEV // nb),
                    pl.BlockSpec((nb, tm, FB), lambda i, j, k: (k, i, 0)),
                    pl.BlockSpec((nb, tn, FB), lambda i, j, k: (k, j, 0)),
                    pl.BlockSpec((tm, tn), lambda i, j, k: (i, j)),
                    jax.ShapeDtypeStruct((S, D), F32), (tm, tn), NDEV // nb, 1, 1, name, comm=comm)


def _mm_dwgate(h2, dg, name, tm=1024):
    _, S, FB = dg.shape
    D = h2.shape[1]
    tm = _tile(D, tm)
    tk = _k_tile(S, tm * h2.dtype.itemsize + FB * dg.dtype.itemsize)
    return _mm_call([(h2, dg)], (NDEV, D // tm, S // tk),
                    pl.BlockSpec((tk, tm), lambda p, i, k: (k, i)),
                    pl.BlockSpec((None, tk, FB), lambda p, i, k: (p, k, 0)),
                    pl.BlockSpec((None, tm, FB), lambda p, i, k: (p, i, 0)),
                    jax.ShapeDtypeStruct((NDEV, D, FB), BF16), (tm, FB), S // tk, 0, 0, name)


def _ffn_up(h2, wg, wu, name, tm=512, comm=None):
    S, D = h2.shape
    FB = wg.shape[2]
    tm = _tile(S, tm)

    def body(h_ref, wg_ref, wu_ref, g_ref, u_ref, act_ref):
        h = h_ref[...]
        g = _dot_nn(h, wg_ref[...])
        u = _dot_nn(h, wu_ref[...])
        g_ref[...] = g.astype(BF16)
        u_ref[...] = u.astype(BF16)
        act_ref[...] = (g / (1.0 + jnp.exp(-g)) * u).astype(BF16)

    w_spec = pl.BlockSpec((None, D, FB), lambda p, i: (p, 0, 0))
    o_spec = pl.BlockSpec((None, tm, FB), lambda p, i: (p, i, 0))
    shp = (NDEV, S, FB)
    outs, moved = _pcall(
        body, name=name, grid=(NDEV, S // tm),
        in_specs=[pl.BlockSpec((tm, D), lambda p, i: (i, 0)), w_spec, w_spec],
        out_specs=[o_spec, o_spec, o_spec],
        out_shape=[jax.ShapeDtypeStruct(shp, BF16)] * 3,
        operands=[h2, wg, wu], sem=("parallel", "parallel"), comm=comm)
    return outs if comm is None else (outs, moved)


def _ffn_dact(dy, wd, g, u, name, tm=512):
    S, D = dy.shape
    FB = g.shape[2]
    tm = _tile(S, tm)

    def body(dy_ref, wd_ref, g_ref, u_ref, dg_ref, du_ref):
        dact = _dot_nt(dy_ref[...], wd_ref[...])
        gv = g_ref[...].astype(F32)
        sg = 1.0 / (1.0 + jnp.exp(-gv))
        dg_ref[...] = (dact * u_ref[...].astype(F32) * (sg * (1.0 + gv * (1.0 - sg)))).astype(BF16)
        du_ref[...] = (dact * (gv * sg)).astype(BF16)

    t_spec = pl.BlockSpec((None, tm, FB), lambda p, i: (p, i, 0))
    shp = jax.ShapeDtypeStruct((NDEV, S, FB), BF16)
    return pl.pallas_call(
        body, name=name, grid=(NDEV, S // tm),
        in_specs=[pl.BlockSpec((tm, D), lambda p, i: (i, 0)), pl.BlockSpec((FB, D), lambda p, i: (p, 0)),
                  t_spec, t_spec],
        out_specs=[t_spec, t_spec], out_shape=[shp, shp],
        compiler_params=_cp("parallel", "parallel"),
    )(dy, wd, g, u)


def _rms_fwd(x, gain, width, cb, out_dtype, name, ts=512):
    S = x.shape[0]
    ts = _tile(S, ts)

    def body(x_ref, g_ref, o_ref):
        xv = x_ref[...]
        r = lax.rsqrt(jnp.mean(xv * xv, axis=1, keepdims=True) + EPS)
        o_ref[...] = (xv * r * g_ref[...]).astype(o_ref.dtype)

    return pl.pallas_call(
        body, name=name, grid=(S // ts,),
        in_specs=[pl.BlockSpec((ts, width), lambda i: (i, cb)), pl.BlockSpec((1, width), lambda i: (0, 0))],
        out_specs=pl.BlockSpec((ts, width), lambda i: (i, 0)),
        out_shape=jax.ShapeDtypeStruct((S, width), out_dtype),
        compiler_params=_cp("parallel"),
    )(x, gain)


def _rms_bwd(x, gain, dy, width, cb, name, res=None, ts=256):
    S = x.shape[0]
    ts = _tile(S, ts)
    has_res = res is not None

    def body(*refs):
        x_ref, g_ref, dy_ref = refs[:3]
        r_ref = refs[3] if has_res else None
        dx_ref, dxb_ref, dg_ref = refs[-3], refs[-2], refs[-1]

        @pl.when(pl.program_id(0) == 0)
        def _():
            dg_ref[...] = jnp.zeros_like(dg_ref)

        xv = x_ref[...]
        r = lax.rsqrt(jnp.mean(xv * xv, axis=1, keepdims=True) + EPS)
        xh = xv * r
        dyv = dy_ref[...]
        dyg = dyv * g_ref[...]
        dx = r * (dyg - xh * jnp.mean(dyg * xh, axis=1, keepdims=True))
        if has_res:
            dx = dx + r_ref[...]
        dx_ref[...] = dx
        dxb_ref[...] = dx.astype(BF16)
        dg_ref[...] += jnp.sum(dyv * xh, axis=0, keepdims=True)

    row = pl.BlockSpec((ts, width), lambda i: (i, 0))
    vec = pl.BlockSpec((1, width), lambda i: (0, 0))
    ops = [x, gain, dy] + ([res] if has_res else [])
    specs = [pl.BlockSpec((ts, width), lambda i: (i, cb)), vec, row] + ([row] if has_res else [])
    return pl.pallas_call(
        body, name=name, grid=(S // ts,), in_specs=specs, out_specs=[row, row, vec],
        out_shape=[jax.ShapeDtypeStruct((S, width), F32), jax.ShapeDtypeStruct((S, width), BF16),
                   jax.ShapeDtypeStruct((1, width), F32)],
        compiler_params=_cp("arbitrary"),
    )(*ops)


def _gn_fwd(outs, gain, name, ts=512):
    S, GW = outs[0].shape
    ts = _tile(S, ts)

    def body(a_ref, b_ref, c_ref, d_ref, g_ref, o_ref):
        for g, r_ref in enumerate((a_ref, b_ref, c_ref, d_ref)):
            xv = r_ref[...]
            r = lax.rsqrt(jnp.mean(xv * xv, axis=1, keepdims=True) + EPS)
            o_ref[:, g * GW:(g + 1) * GW] = (xv * r * g_ref[:, g * GW:(g + 1) * GW]).astype(BF16)

    row = pl.BlockSpec((ts, GW), lambda i: (i, 0))
    return pl.pallas_call(
        body, name=name, grid=(S // ts,),
        in_specs=[row] * 4 + [pl.BlockSpec((1, 4 * GW), lambda i: (0, 0))],
        out_specs=pl.BlockSpec((ts, 4 * GW), lambda i: (i, 0)),
        out_shape=jax.ShapeDtypeStruct((S, 4 * GW), BF16),
        compiler_params=_cp("parallel"),
    )(*outs, gain)


def _gn_bwd(outs, gain, dmix, name, ts=256):
    S, GW = outs[0].shape
    ts = _tile(S, ts)

    def body(a_ref, b_ref, c_ref, d_ref, g_ref, dm_ref, da_ref, db_ref, dc_ref, dd_ref, dg_ref):
        @pl.when(pl.program_id(0) == 0)
        def _():
            dg_ref[...] = jnp.zeros_like(dg_ref)

        for g, (r_ref, o_ref) in enumerate(zip((a_ref, b_ref, c_ref, d_ref), (da_ref, db_ref, dc_ref, dd_ref))):
            sl = slice(g * GW, (g + 1) * GW)
            xv = r_ref[...]
            r = lax.rsqrt(jnp.mean(xv * xv, axis=1, keepdims=True) + EPS)
            xh = xv * r
            dyv = dm_ref[:, sl]
            dyg = dyv * g_ref[:, sl]
            o_ref[...] = (r * (dyg - xh * jnp.mean(dyg * xh, axis=1, keepdims=True))).astype(BF16)
            dg_ref[:, sl] += jnp.sum(dyv * xh, axis=0, keepdims=True)

    row = pl.BlockSpec((ts, GW), lambda i: (i, 0))
    vec = pl.BlockSpec((1, 4 * GW), lambda i: (0, 0))
    return pl.pallas_call(
        body, name=name, grid=(S // ts,),
        in_specs=[row] * 4 + [vec, pl.BlockSpec((ts, 4 * GW), lambda i: (i, 0))],
        out_specs=[row] * 4 + [vec],
        out_shape=[jax.ShapeDtypeStruct((S, GW), BF16)] * 4 + [jax.ShapeDtypeStruct((1, 4 * GW), F32)],
        compiler_params=_cp("arbitrary"),
    )(*outs, gain, dmix)


def _rope(x, cbs, cos, sin, half, out_dtype, name, ts=512):
    S = x.shape[0]
    cb0, nb, stride = cbs
    ts = _tile(S, ts)

    def body(x_ref, c_ref, s_ref, o_ref):
        xv = x_ref[...].astype(F32)
        if half:
            lane = lax.broadcasted_iota(jnp.int32, xv.shape, 1)
            partner = jnp.where(lane % 64 < 32, pltpu.roll(xv, LANE - 32, 1), pltpu.roll(xv, 32, 1))
        else:
            partner = pltpu.roll(xv, 64, 1)
        o_ref[...] = (xv * c_ref[...] + partner * s_ref[...]).astype(o_ref.dtype)

    tab = pl.BlockSpec((ts, LANE), lambda i, j: (i, 0))
    return pl.pallas_call(
        body, name=name, grid=(S // ts, nb),
        in_specs=[pl.BlockSpec((ts, LANE), lambda i, j: (i, cb0 + stride * j)), tab, tab],
        out_specs=pl.BlockSpec((ts, LANE), lambda i, j: (i, j)),
        out_shape=jax.ShapeDtypeStruct((S, nb * LANE), out_dtype),
        compiler_params=_cp("parallel", "parallel"),
    )(x, cos, sin)


def _final_loss(x, gain, target, name, ts=256):
    S, D = x.shape
    ts = _tile(S, ts)

    def body(x_ref, g_ref, t_ref, dy_ref, l_ref):
        @pl.when(pl.program_id(0) == 0)
        def _():
            l_ref[...] = jnp.zeros_like(l_ref)

        xv = x_ref[...]
        r = lax.rsqrt(jnp.mean(xv * xv, axis=1, keepdims=True) + EPS)
        err = xv * r * g_ref[...] - t_ref[...]
        dy_ref[...] = err * (1.0 / D)
        part = jnp.sum(jnp.mean(err * err, axis=1, keepdims=True), axis=0, keepdims=True)
        l_ref[...] += jnp.broadcast_to(0.5 * part, (1, LANE))

    row = pl.BlockSpec((ts, D), lambda i: (i, 0))
    return pl.pallas_call(
        body, name=name, grid=(S // ts,),
        in_specs=[row, pl.BlockSpec((1, D), lambda i: (0, 0)), row],
        out_specs=[row, pl.BlockSpec((1, LANE), lambda i: (0, 0))],
        out_shape=[jax.ShapeDtypeStruct((S, D), F32), jax.ShapeDtypeStruct((1, LANE), F32)],
        compiler_params=_cp("arbitrary"),
    )(x, gain, target)


def _colspec(rows, f):
    return pl.BlockSpec((rows, LANE), f)


def _soft_tiles(S):
    tq = _tile(S, TQ)
    tk = _tile(S, TKS)
    assert tk % tq == 0
    return tq, tk


def _key_row(crow_ref, j, tk):
    n = tk // TK
    return jnp.concatenate([crow_ref[j * n + c] for c in range(n)], axis=1)


def _attn_fwd(S, H, q1, q1cb, k1, k1cb, v, vcb, scale, name, q2=None, q2cb=None, k2=None, k2cb=None,
              tab=None, win=None, ccol=None, crow=None, comm=None):
    tq, tk = _soft_tiles(S)
    has2, hastab, hasc = q2 is not None, tab is not None, ccol is not None

    def body(*refs):
        it = iter(refs)
        q1r, k1r, vr = next(it), next(it), next(it)
        q2r, k2r = (next(it), next(it)) if has2 else (None, None)
        tabr = next(it) if hastab else None
        ccolr, crowr = (next(it), next(it)) if hasc else (None, None)
        o_ref, lse_ref = next(it), next(it)
        i = pl.program_id(1)
        q = q1r[...]
        qb2 = q2r[...] if has2 else None
        cq = ccolr[:, 0:1] if hasc else None
        qpos = i * tq + lax.broadcasted_iota(jnp.int32, (tq, tk), 0)
        kio = lax.broadcasted_iota(jnp.int32, (tq, tk), 1)
        j_diag = (i * tq) // tk
        j_lo = jnp.maximum((i * tq - win) // tk, 0) if win else 0

        def step(j, carry, masked):
            m, l, acc = carry
            off = pl.multiple_of(j * tk, tk)
            s = _dot_nt(q, k1r[pl.ds(off, tk), :])
            if has2:
                s = s + _dot_nt(qb2, k2r[pl.ds(off, tk), :])
            s = s * scale
            if hastab:
                s = s + tabr[i - j * (tk // tq)]
            else:
                if hasc:
                    s = s + (cq - _key_row(crowr, j, tk))
                if masked:
                    s = jnp.where(kio + j * tk <= qpos, s, NEG)
            mn = jnp.maximum(m, jnp.max(s, axis=1, keepdims=True))
            p = jnp.exp(s - mn)
            al = jnp.exp(m - mn)
            l = al * l + jnp.sum(p, axis=1, keepdims=True)
            acc = al * acc + _dot_nn(p.astype(BF16), vr[pl.ds(off, tk), :])
            return mn, l, acc

        carry = (jnp.full((tq, 1), NEG, F32), jnp.zeros((tq, 1), F32), jnp.zeros((tq, LANE), F32))
        if hastab:
            carry = lax.fori_loop(j_lo, j_diag + 1, functools.partial(step, masked=False), carry)
        else:
            carry = lax.fori_loop(j_lo, j_diag, functools.partial(step, masked=False), carry)
            carry = step(j_diag, carry, True)
        m, l, acc = carry
        o_ref[...] = acc / l
        lse_ref[...] = jnp.broadcast_to(m + jnp.log(l), (tq, LANE))

    ops = [q1, k1, v]
    specs = [_colspec(tq, lambda h, i: (i, q1cb(h))), _colspec(S, lambda h, i: (0, k1cb(h))),
             _colspec(S, lambda h, i: (0, vcb(h)))]
    if has2:
        ops += [q2, k2]
        specs += [_colspec(tq, lambda h, i: (i, q2cb(h))), _colspec(S, lambda h, i: (0, k2cb(h)))]
    if hastab:
        ops.append(tab)
        specs.append(pl.BlockSpec(tab.shape, lambda h, i: (0, 0, 0)))
    if hasc:
        ops += [ccol, crow]
        specs += [_colspec(tq, lambda h, i: (i, h)),
                  pl.BlockSpec((None, S // TK, 1, TK), lambda h, i: (h, 0, 0, 0))]
    o_spec = _colspec(tq, lambda h, i: (i, h))
    shp = jax.ShapeDtypeStruct((S, H * LANE), F32)
    outs, moved = _pcall(
        body, name=name, grid=(H, S // tq), in_specs=specs, out_specs=[o_spec, o_spec], out_shape=[shp, shp],
        operands=ops, sem=("parallel", "arbitrary"), comm=comm)
    return outs if comm is None else (outs, moved)


def _attn_bwd(S, H, q1, q1cb, k1, k1cb, v, vcb, o, do, lse, scale, name, q2=None, q2cb=None, k2=None, k2cb=None,
              tab=None, win=None, ccol=None, crow=None, comm=None):
    tq, tk = _soft_tiles(S)
    has2, hastab, hasc = q2 is not None, tab is not None, ccol is not None

    def body(*refs):
        it = iter(refs)
        q1r, k1r, vr, o_r, do_r, lse_r = (next(it) for _ in range(6))
        q2r, k2r = (next(it), next(it)) if has2 else (None, None)
        tabr = next(it) if hastab else None
        ccolr, crowr = (next(it), next(it)) if hasc else (None, None)
        dq1_r, dk1_o, dv_o = next(it), next(it), next(it)
        dq2_r, dk2_o = (next(it), next(it)) if has2 else (None, None)
        dcr_r = next(it) if hasc else None
        dk1_r, dv_r = next(it), next(it)
        dk2_r = next(it) if has2 else None
        i = pl.program_id(1)

        @pl.when(i == 0)
        def _():
            dk1_r[...] = jnp.zeros_like(dk1_r)
            dv_r[...] = jnp.zeros_like(dv_r)
            if has2:
                dk2_r[...] = jnp.zeros_like(dk2_r)
            if hasc:
                dcr_r[...] = jnp.zeros_like(dcr_r)

        q = q1r[...]
        qb2 = q2r[...] if has2 else None
        dob = do_r[...]
        delta = jnp.sum(dob.astype(F32) * o_r[...], axis=1, keepdims=True)
        lse_c = lse_r[:, 0:1]
        cq = ccolr[:, 0:1] if hasc else None
        qpos = i * tq + lax.broadcasted_iota(jnp.int32, (tq, tk), 0)
        kio = lax.broadcasted_iota(jnp.int32, (tq, tk), 1)
        j_diag = (i * tq) // tk
        j_lo = jnp.maximum((i * tq - win) // tk, 0) if win else 0

        def probs(j, masked):
            off = pl.multiple_of(j * tk, tk)
            kb = k1r[pl.ds(off, tk), :]
            s = _dot_nt(q, kb)
            kb2 = None
            if has2:
                kb2 = k2r[pl.ds(off, tk), :]
                s = s + _dot_nt(qb2, kb2)
            s = s * scale
            if hastab:
                s = s + tabr[i - j * (tk // tq)]
            else:
                if hasc:
                    s = s + (cq - _key_row(crowr, j, tk))
                if masked:
                    s = jnp.where(kio + j * tk <= qpos, s, NEG)
            p = jnp.exp(s - lse_c)
            dp = _dot_nt(dob, vr[pl.ds(off, tk), :])
            return off, kb, kb2, p, dp

        def sweep(fn, carry):
            if hastab:
                return lax.fori_loop(j_lo, j_diag + 1, functools.partial(fn, masked=False), carry)
            carry = lax.fori_loop(j_lo, j_diag, functools.partial(fn, masked=False), carry)
            return fn(j_diag, carry, True)

        if hasc:
            def dstep(j, acc, masked):
                _, _, _, p, dp = probs(j, masked)
                return acc + jnp.sum(p * dp, axis=1, keepdims=True)

            delta = sweep(dstep, jnp.zeros((tq, 1), F32))

        def step(j, carry, masked):
            dq, dq2 = carry
            off, kb, kb2, p, dp = probs(j, masked)
            ds = p * (dp - delta)
            dsb = ds.astype(BF16)
            dq = dq + _dot_nn(dsb, kb)
            dk1_r[pl.ds(off, tk), :] += _dot_tn(dsb, q) * scale
            dv_r[pl.ds(off, tk), :] += _dot_tn(p.astype(BF16), dob)
            if has2:
                dq2 = dq2 + _dot_nn(dsb, kb2)
                dk2_r[pl.ds(off, tk), :] += _dot_tn(dsb, qb2) * scale
            if hasc:
                cs = -jnp.sum(ds, axis=0, keepdims=True)
                for c in range(tk // TK):
                    dcr_r[j * (tk // TK) + c] += cs[:, c * TK:(c + 1) * TK]
            return dq, dq2

        z = jnp.zeros((tq, LANE), F32)
        dq, dq2 = sweep(step, (z, z))
        dq1_r[...] = (dq * scale).astype(BF16)
        if has2:
            dq2_r[...] = (dq2 * scale).astype(BF16)

        @pl.when(i == S // tq - 1)
        def _():
            dk1_o[...] = dk1_r[...].astype(BF16)
            dv_o[...] = dv_r[...].astype(BF16)
            if has2:
                dk2_o[...] = dk2_r[...].astype(BF16)

    qspec = _colspec(tq, lambda h, i: (i, h))
    kspec = _colspec(S, lambda h, i: (0, h))
    ops = [q1, k1, v, o, do, lse]
    specs = [_colspec(tq, lambda h, i: (i, q1cb(h))), _colspec(S, lambda h, i: (0, k1cb(h))),
             _colspec(S, lambda h, i: (0, vcb(h))), qspec, qspec, qspec]
    if has2:
        ops += [q2, k2]
        specs += [_colspec(tq, lambda h, i: (i, q2cb(h))), _colspec(S, lambda h, i: (0, k2cb(h)))]
    if hastab:
        ops.append(tab)
        specs.append(pl.BlockSpec(tab.shape, lambda h, i: (0, 0, 0)))
    if hasc:
        ops += [ccol, crow]
        specs += [qspec, pl.BlockSpec((None, S // TK, 1, TK), lambda h, i: (h, 0, 0, 0))]
    assert all(t.dtype == BF16 for t in ops[:3] + [do] + ([q2, k2] if has2 else []))
    out_specs = [qspec, kspec, kspec] + ([qspec, kspec] if has2 else [])
    shp = jax.ShapeDtypeStruct((S, H * LANE), BF16)
    out_shape = [shp] * len(out_specs)
    if hasc:
        out_specs.append(pl.BlockSpec((None, S // TK, 1, TK), lambda h, i: (h, 0, 0, 0)))
        out_shape.append(jax.ShapeDtypeStruct((H, S // TK, 1, TK), F32))
    outs, moved = _pcall(
        body, name=name, grid=(H, S // tq), in_specs=specs, out_specs=out_specs, out_shape=out_shape,
        operands=ops, scratch_shapes=[pltpu.VMEM((S, LANE), F32)] * (3 if has2 else 2),
        sem=("parallel", "arbitrary"), comm=comm)
    return outs if comm is None else (outs, moved)


def _scan_matrix(kind):
    j = np.arange(TK)[:, None]
    s = np.arange(TK)[None, :]
    tri = {"suffix_ex": j > s, "prefix_in": j <= s, "prefix_ex": j < s}[kind].astype(np.float32)
    half = np.concatenate([tri, np.ones((TK, TK), np.float32)], axis=1)
    return jnp.asarray(np.concatenate([half, half], axis=0), BF16)


def _scan_mxu(x, mat, carry, reverse):
    n = x.shape[1] // TK
    hi = x.astype(BF16)
    lo = (x - hi.astype(F32)).astype(BF16)
    parts = [None] * n
    for b in (reversed(range(n)) if reverse else range(n)):
        sl = slice(b * TK, (b + 1) * TK)
        r = _dot_nn(jnp.concatenate([hi[:, sl], lo[:, sl]], axis=1), mat)
        parts[b] = r[:, :TK] + carry
        carry = carry + r[:, TK:]
    return jnp.concatenate(parts, axis=1), carry


def _stick_logs(z):
    e = jnp.exp(-jnp.abs(z))
    return e, -jnp.maximum(z, 0.0) - jnp.log(1.0 + e)


def _stick_fwd(S, H, x, qcb, kcb, vcb, scale, name, comm=None):
    tq, tk = _soft_tiles(S)
    assert x.dtype == BF16

    def body(q_r, k_r, v_r, mat_r, o_ref, t_ref):
        i = pl.program_id(1)
        q = q_r[...]
        qpos = i * tq + lax.broadcasted_iota(jnp.int32, (tq, tk), 0)
        lane = lax.broadcasted_iota(jnp.int32, (tq, tk), 1)
        j_diag = (i * tq) // tk

        def step(j, carry, masked):
            c, acc = carry
            off = pl.multiple_of(j * tk, tk)
            z = _dot_nt(q, k_r[pl.ds(off, tk), :]) * scale
            _, lk = _stick_logs(z)
            if masked:
                past = lane + j * tk < qpos
                lk = jnp.where(past, lk, 0.0)
            suf, c = _scan_mxu(lk, mat_r[...], c, True)
            a = jnp.exp(z + lk + suf)
            if masked:
                a = jnp.where(past, a, 0.0)
            acc = acc + _dot_nn(a.astype(BF16), v_r[pl.ds(off, tk), :])
            return c, acc

        carry = step(j_diag, (jnp.zeros((tq, TK), F32), jnp.zeros((tq, LANE), F32)), True)
        c, acc = lax.fori_loop(0, j_diag, lambda jj, cr: step(j_diag - 1 - jj, cr, False), carry)
        o_ref[...] = acc
        t_ref[...] = c

    o_spec = _colspec(tq, lambda h, i: (i, h))
    shp = jax.ShapeDtypeStruct((S, H * LANE), F32)
    mat = _scan_matrix("suffix_ex")
    outs, moved = _pcall(
        body, name=name, grid=(H, S // tq),
        in_specs=[_colspec(tq, lambda h, i: (i, qcb(h))), _colspec(S, lambda h, i: (0, kcb(h))),
                  _colspec(S, lambda h, i: (0, vcb(h))), pl.BlockSpec(mat.shape, lambda h, i: (0, 0))],
        out_specs=[o_spec, o_spec], out_shape=[shp, shp],
        operands=[x, x, x, mat], sem=("parallel", "arbitrary"), comm=comm)
    return outs if comm is None else (outs, moved)


def _stick_bwd(S, H, x, qcb, kcb, vcb, do, tot, scale, name, comm=None):
    tq, tk = _soft_tiles(S)
    assert x.dtype == BF16 and do.dtype == BF16

    def body(q_r, k_r, v_r, do_r, t_r, pin_r, pex_r, dq_r, dk_o, dv_o, dk_r, dv_r):
        i = pl.program_id(1)

        @pl.when(i == 0)
        def _():
            dk_r[...] = jnp.zeros_like(dk_r)
            dv_r[...] = jnp.zeros_like(dv_r)

        q = q_r[...]
        dob = do_r[...]
        total = jnp.concatenate([t_r[...]] * (tk // TK), axis=1)
        qpos = i * tq + lax.broadcasted_iota(jnp.int32, (tq, tk), 0)
        lane = lax.broadcasted_iota(jnp.int32, (tq, tk), 1)
        j_diag = (i * tq) // tk

        def step(j, carry, masked):
            cl, cg, dq = carry
            off = pl.multiple_of(j * tk, tk)
            kb = k_r[pl.ds(off, tk), :]
            z = _dot_nt(q, kb) * scale
            e, lk = _stick_logs(z)
            if masked:
                past = lane + j * tk < qpos
                lk = jnp.where(past, lk, 0.0)
            pre, cl = _scan_mxu(lk, pin_r[...], cl, False)
            a = jnp.exp(z + lk + (total - pre))
            if masked:
                a = jnp.where(past, a, 0.0)
            g = _dot_nt(dob, v_r[pl.ds(off, tk), :]) * a
            gpre, cg = _scan_mxu(g, pex_r[...], cg, False)
            inv = 1.0 / (1.0 + e)
            small = e * inv
            pos = z >= 0
            dz = g * jnp.where(pos, small, inv) - jnp.where(pos, inv, small) * gpre
            if masked:
                dz = jnp.where(past, dz, 0.0)
            dzb = dz.astype(BF16)
            dk_r[pl.ds(off, tk), :] += _dot_tn(dzb, q) * scale
            dv_r[pl.ds(off, tk), :] += _dot_tn(a.astype(BF16), dob)
            return cl, cg, dq + _dot_nn(dzb, kb)

        zt = jnp.zeros((tq, TK), F32)
        carry = lax.fori_loop(0, j_diag, functools.partial(step, masked=False), (zt, zt, jnp.zeros((tq, LANE), F32)))
        dq_r[...] = (step(j_diag, carry, True)[2] * scale).astype(BF16)

        @pl.when(i == S // tq - 1)
        def _():
            dk_o[...] = dk_r[...].astype(BF16)
            dv_o[...] = dv_r[...].astype(BF16)

    qspec = _colspec(tq, lambda h, i: (i, h))
    kspec = _colspec(S, lambda h, i: (0, h))
    shp = jax.ShapeDtypeStruct((S, H * LANE), BF16)
    pin, pex = _scan_matrix("prefix_in"), _scan_matrix("prefix_ex")
    mspec = pl.BlockSpec(pin.shape, lambda h, i: (0, 0))
    outs, moved = _pcall(
        body, name=name, grid=(H, S // tq),
        in_specs=[_colspec(tq, lambda h, i: (i, qcb(h))), _colspec(S, lambda h, i: (0, kcb(h))),
                  _colspec(S, lambda h, i: (0, vcb(h))), qspec, qspec, mspec, mspec],
        out_specs=[qspec, kspec, kspec], out_shape=[shp] * 3,
        operands=[x, x, x, do, tot, pin, pex], scratch_shapes=[pltpu.VMEM((S, LANE), F32)] * 2,
        sem=("parallel", "arbitrary"), comm=comm)
    return outs if comm is None else (outs, moved)


def _scan8(x, rows, reverse):
    for sh in (1, 2, 4):
        if reverse:
            x = x + jnp.where(rows + sh < 8, pltpu.roll(x, 8 - sh, 0), 0.0)
        else:
            x = x + jnp.where(rows >= sh, pltpu.roll(x, sh, 0), 0.0)
    return x


def _fox_prep(S, H, proj, fcb, bias, name):
    tk = TK

    def body(f_ref, b_ref, ccol_ref, crow_ref, scr):
        rows = lax.broadcasted_iota(jnp.int32, (8, LANE), 0)

        def step(t, carry):
            off = pl.multiple_of(t * 8, 8)
            xb = f_ref[pl.ds(off, 8), :] + b_ref[...]
            lf = jnp.minimum(xb, 0.0) - jnp.log(1.0 + jnp.exp(-jnp.abs(xb)))
            lf = _scan8(lf, rows, False) + carry
            scr[pl.ds(off, 8), :] = lf
            return lf[7:8, :]

        lax.fori_loop(0, S // 8, step, jnp.zeros((1, LANE), F32))
        for h in range(H):
            ccol_ref[:, h * LANE:(h + 1) * LANE] = jnp.broadcast_to(scr[:, h:h + 1], (S, LANE))

            def tr(t, _):
                off = pl.multiple_of(t * tk, tk)
                blk = ccol_ref[pl.ds(off, tk), h * LANE:(h + 1) * LANE]
                crow_ref[h, t] = blk.T[0:1, :]
                return 0

            lax.fori_loop(0, S // tk, tr, 0)

    return pl.pallas_call(
        body, name=name, grid=(1,),
        in_specs=[_colspec(S, lambda i: (0, fcb)), pl.BlockSpec((1, LANE), lambda i: (0, 0))],
        out_specs=[pl.BlockSpec((S, H * LANE), lambda i: (0, 0)),
                   pl.BlockSpec((H, S // tk, 1, tk), lambda i: (0, 0, 0, 0))],
        out_shape=[jax.ShapeDtypeStruct((S, H * LANE), F32), jax.ShapeDtypeStruct((H, S // tk, 1, tk), F32)],
        scratch_shapes=[pltpu.VMEM((S, LANE), F32)],
        compiler_params=_cp("arbitrary"),
    )(proj, bias)


def _fox_bwd(S, H, proj, fcb, bias, dcr, name):
    tk = TK

    def body(f_ref, b_ref, dcr_ref, df_ref, db_ref, scr):
        rows = lax.broadcasted_iota(jnp.int32, (8, LANE), 0)
        lane_t = lax.broadcasted_iota(jnp.int32, (tk, LANE), 1)
        nb = S // 8

        def tr(t, _):
            off = pl.multiple_of(t * tk, tk)
            d = jnp.zeros((tk, LANE), F32)
            for h in range(H):
                d = d + jnp.where(lane_t == h, jnp.broadcast_to(dcr_ref[h, t], (LANE, tk)).T, 0.0)
            scr[pl.ds(off, tk), :] = d
            return 0

        lax.fori_loop(0, S // tk, tr, 0)

        def step(tt, carry):
            suffix, db = carry
            off = pl.multiple_of((nb - 1 - tt) * 8, 8)
            d = _scan8(scr[pl.ds(off, 8), :], rows, True) + suffix
            xb = f_ref[pl.ds(off, 8), :] + b_ref[...]
            e = jnp.exp(-jnp.abs(xb))
            dx = d * jnp.where(xb >= 0, e, 1.0) / (1.0 + e)
            df_ref[pl.ds(off, 8), :] = dx
            return d[0:1, :], db + jnp.sum(dx, axis=0, keepdims=True)

        z = jnp.zeros((1, LANE), F32)
        _, db = lax.fori_loop(0, nb, step, (z, z))
        db_ref[...] = db

    return pl.pallas_call(
        body, name=name, grid=(1,),
        in_specs=[_colspec(S, lambda i: (0, fcb)), pl.BlockSpec((1, LANE), lambda i: (0, 0)),
                  pl.BlockSpec((H, S // tk, 1, tk), lambda i: (0, 0, 0, 0))],
        out_specs=[pl.BlockSpec((S, LANE), lambda i: (0, 0)), pl.BlockSpec((1, LANE), lambda i: (0, 0))],
        out_shape=[jax.ShapeDtypeStruct((S, LANE), F32), jax.ShapeDtypeStruct((1, LANE), F32)],
        scratch_shapes=[pltpu.VMEM((S, LANE), F32)],
        compiler_params=_cp("arbitrary"),
    )(proj, bias, dcr)


def _adamw(w, slots, m, v, name, block_bytes=1 << 20):
    R, C = w.shape
    tr = R
    while tr * C * 4 > block_bytes and tr % 16 == 0:
        tr //= 2
    c1 = 1.0 - ADAM_B1 ** ADAM_STEP
    c2 = 1.0 - ADAM_B2 ** ADAM_STEP

    def body(w_ref, s_ref, m_ref, v_ref, g_ref, d_ref, nm_ref, nv_ref):
        g = s_ref[0].astype(F32)
        for s in range(1, NDEV):
            g = g + s_ref[s].astype(F32)
        mn = ADAM_B1 * m_ref[...] + (1.0 - ADAM_B1) * g
        vn = ADAM_B2 * v_ref[...] + (1.0 - ADAM_B2) * (g * g)
        g_ref[...] = g
        nm_ref[...] = mn
        nv_ref[...] = vn
        d_ref[...] = -ADAM_LR * ((mn / c1) / (jnp.sqrt(vn / c2) + ADAM_EPS) + ADAM_WD * w_ref[...])

    row = pl.BlockSpec((tr, C), lambda i: (i, 0))
    return pl.pallas_call(
        body, name=name, grid=(R // tr,),
        in_specs=[row, pl.BlockSpec((NDEV, tr, C), lambda i: (0, i, 0)), row, row],
        out_specs=[row] * 4, out_shape=[jax.ShapeDtypeStruct((R, C), F32)] * 4,
        compiler_params=_cp("parallel"),
    )(w, slots, m, v)


class _Layout:
    def __init__(self, D):
        self.GW = GW = D // 4
        self.H = H = GW // HEAD
        self.QL, self.KVL = 0, Q_LORA
        base = Q_LORA + KV_LORA
        (self.QB, self.KB, self.VB, self.QC, self.KC, self.VC, self.QD, self.KD, self.VD) = (
            base + k * GW for k in range(9))
        self.KR = base + 9 * GW
        self.FC = self.KR + LANE
        self.PW = -(-(self.FC + LANE) // 512) * 512
        self.o_kr = base
        self.o_bc = base + QK_ROPE
        self.o_fc = self.o_bc + 6 * GW
        self.o_d = self.o_fc + H
        self.IN = self.o_d + 3 * GW

    def pad(self, w):
        z = lambda n: jnp.zeros(w.shape[:-1] + (n,), w.dtype)
        return jnp.concatenate([
            w[..., :self.o_kr], w[..., self.o_bc:self.o_fc], w[..., self.o_d:self.IN],
            w[..., self.o_kr:self.o_bc], z(LANE - QK_ROPE), w[..., self.o_fc:self.o_d], z(LANE - self.H),
            z(self.PW - self.FC - LANE)], axis=-1)

    def unpad(self, g):
        return jnp.concatenate([
            g[..., :self.KR - 9 * self.GW], g[..., self.KR:self.KR + QK_ROPE], g[..., self.QB:self.QD],
            g[..., self.FC:self.FC + self.H], g[..., self.QD:self.KR]], axis=-1)


def _rope_tables(S):
    pos = jnp.arange(S, dtype=F32)

    def cs(dim):
        inv = ROPE_THETA ** (-jnp.arange(0, dim, 2, dtype=F32) / dim)
        ang = pos[:, None] * inv[None, :]
        return jnp.cos(ang), jnp.sin(ang)

    c, s = cs(HEAD)
    full = (jnp.concatenate([c, c], 1), jnp.concatenate([-s, s], 1))
    c, s = cs(QK_ROPE)
    z = jnp.zeros((S, LANE - QK_ROPE), F32)
    half = (jnp.concatenate([c, c, z], 1), jnp.concatenate([-s, s, z], 1))
    return full, half


def _dilated_table(tq, tk):
    win = max(w for w, _ in DILATED_PAIRS)
    nd = (win + tk) // tq + 1
    d = np.arange(nd)[:, None, None] * tq + np.arange(tq)[None, :, None] - np.arange(tk)[None, None, :]
    mult = np.zeros(d.shape, np.float64)
    for w, dil in DILATED_PAIRS:
        mult += (d >= 0) & (d <= w) & (d % dil == 0)
    return jnp.asarray(np.where(mult > 0, np.log(np.maximum(mult, 1.0)), NEG), F32), win


def _pack(arrs):
    rows = []
    for a in arrs:
        f = a.reshape(-1).astype(F32)
        f = jnp.pad(f, (0, (-f.shape[0]) % LANE))
        rows.append(f.reshape(-1, LANE))
    p = jnp.concatenate(rows, 0)
    return jnp.pad(p, ((0, (-p.shape[0]) % 8), (0, 0)))


def _unpack(p, shapes):
    out, r = [], 0
    for shp in shapes:
        n = int(np.prod(shp))
        nr = -(-n // LANE)
        out.append(p[r:r + nr].reshape(-1)[:n].reshape(shp))
        r += nr
    return out


def kernel(x, attn_norm, w_in, mla_q_norm, w_uq, mla_kv_norm, w_ukv, fox_forget_bias, group_norm, w_out, ffn_norm, w_gate, w_up, w_down, final_norm, loss_target, m_attn_norm, m_w_in, m_mla_q_norm, m_w_uq, m_mla_kv_norm, m_w_ukv, m_fox_forget_bias, m_group_norm, m_w_out, m_ffn_norm, m_w_gate, m_w_up, m_w_down, m_final_norm, v_attn_norm, v_w_in, v_mla_q_norm, v_w_uq, v_mla_kv_norm, v_w_ukv, v_fox_forget_bias, v_group_norm, v_w_out, v_ffn_norm, v_w_gate, v_w_up, v_w_down, v_final_norm):
    _, S, D = x.shape
    L = attn_norm.shape[0]
    lay = _Layout(D)
    H, GW, PW = lay.H, lay.GW, lay.PW
    FB = w_gate.shape[2]
    QKA = HEAD + QK_ROPE
    x = x[0]
    target = loss_target[0]
    rope_full, rope_half = _rope_tables(S)
    neg = lambda t: (t[0], -t[1])
    tab, win = _dilated_table(*_soft_tiles(S))
    cb = lambda col: col // LANE

    sh = dict(w_in=lay.pad(w_in).astype(BF16),
              **{n: w.astype(BF16) for n, w in (("w_uq", w_uq), ("w_ukv", w_ukv), ("w_out", w_out),
                                                  ("w_gate", w_gate), ("w_up", w_up), ("w_down", w_down))})
    first3 = ["w_in", "w_uq", "w_ukv"]

    def first_weights(g):
        wuq = jnp.transpose(g[1], (1, 0, 2)).reshape(Q_LORA, H, QKA)
        wuq = jnp.pad(wuq, ((0, 0), (0, 0), (0, 2 * LANE - QKA))).reshape(Q_LORA, H * 2 * LANE)
        return dict(win=g[0].reshape(D, PW), wuq=wuq,
                    wukv=jnp.transpose(g[2], (1, 0, 2)).reshape(KV_LORA, H * 2 * LANE))

    def row(a):
        return a.reshape(1, -1)

    def forward(l, x0, W):
        A = dict(x0=x0)
        A["bias"] = jnp.pad(row(fox_forget_bias[l]), ((0, 0), (0, LANE - H)))
        h1 = A["h1"] = _rms_fwd(x0, row(attn_norm[l]), D, 0, BF16, "attn_norm")
        (proj, pb), (W["wg"],) = _mm(h1, W["win"], "in_proj", out_dtype=(F32, BF16),
                                     comm=("gather", [sh["w_gate"][l]]))
        A["proj"], A["pb"] = proj, pb
        qln = A["qln"] = _rms_fwd(proj, row(mla_q_norm[l]), Q_LORA, cb(lay.QL) // 4, BF16, "q_norm")
        kvln = A["kvln"] = _rms_fwd(proj, row(mla_kv_norm[l]), KV_LORA, cb(lay.KVL) // 4, BF16, "kv_norm")
        qa, qab = _mm(qln, W["wuq"], "q_up", out_dtype=(F32, BF16))
        A["qab"] = qab
        kv = A["kv"] = _mm(kvln, W["wukv"], "kv_up", out_dtype=BF16)
        q_pe = A["q_pe"] = _rope(qa, (1, H, 2), *rope_half, True, BF16, "rope_q_mla")
        k_pe = A["k_pe"] = _rope(proj, (cb(lay.KR), 1, 1), *rope_half, True, BF16, "rope_k_mla")
        (A["o_a"], A["lse_a"]), (W["wu"],) = _attn_fwd(
            S, H, qab, lambda h: 2 * h, kv, lambda h: 2 * h, kv, lambda h: 2 * h + 1, QKA ** -0.5, "mla_fwd",
            q2=q_pe, q2cb=lambda h: h, k2=k_pe, k2cb=lambda h: 0, comm=("gather", [sh["w_up"][l]]))
        qk_b = A["qk_b"] = _rope(proj, (cb(lay.QB), 2 * H, 1), *rope_full, False, BF16, "rope_qk_dil")
        (A["o_b"], A["lse_b"]), (g_down,) = _attn_fwd(
            S, H, qk_b, lambda h: h, qk_b, lambda h: H + h, pb, lambda h: cb(lay.VB) + h, HEAD ** -0.5,
            "dilated_fwd", tab=tab, win=win, comm=("gather", [sh["w_down"][l]]))
        W["wd"] = g_down.reshape(NDEV * FB, D)
        ccol, crow = A["ccol"], A["crow"] = _fox_prep(S, H, proj, cb(lay.FC), A["bias"], "fox_prep")
        (A["o_c"], A["lse_c"]), (g_out,) = _attn_fwd(
            S, H, pb, lambda h: cb(lay.QC) + h, pb, lambda h: cb(lay.KC) + h, pb, lambda h: cb(lay.VC) + h,
            HEAD ** -0.5, "fox_fwd", ccol=ccol, crow=crow, comm=("gather", [sh["w_out"][l]]))
        W["wout"] = g_out.reshape(4 * GW, D)
        A["o_d"], A["tot_d"] = _stick_fwd(
            S, H, pb, lambda h: cb(lay.QD) + h, lambda h: cb(lay.KD) + h, lambda h: cb(lay.VD) + h,
            HEAD ** -0.5, "stick_fwd")
        mix = A["mix"] = _gn_fwd([A["o_a"], A["o_b"], A["o_c"], A["o_d"]], row(group_norm[l]), "group_norm")
        x1 = A["x1"] = _mm(mix, W["wout"], "out_proj", res=x0)
        h2 = A["h2"] = _rms_fwd(x1, row(ffn_norm[l]), D, 0, BF16, "ffn_norm")
        nxt = None
        if l + 1 < L:
            (A["g"], A["u"], A["act"]), nxt = _ffn_up(
                h2, W["wg"], W["wu"], "ffn_up", comm=("gather", [sh[n][l + 1] for n in first3]))
        else:
            A["g"], A["u"], A["act"] = _ffn_up(h2, W["wg"], W["wu"], "ffn_up_last")
        return _mm_down(A["act"], W["wd"], x1, "ffn_down"), A, nxt

    def backward(l, dx2, dx2b, W, A, late):
        proj, pb = A["proj"], A["pb"]
        G, small, got = {}, {}, {}
        dgate, dup = _ffn_dact(dx2b, W["wd"], A["g"], A["u"], "ffn_dact")
        G["w_down"] = _mm_dwdown(A["act"], dx2b, "dw_down").reshape(NDEV, FB, D)
        dh2, (got[l, "w_down"],) = _mm_dh2(dgate, W["wg"], dup, W["wu"], "ffn_dh", comm=("exchange", [G["w_down"]]))
        G["w_gate"] = _mm_dwgate(A["h2"], dgate, "dw_gate")
        G["w_up"] = _mm_dwgate(A["h2"], dup, "dw_up")
        dx1, dx1b, small["ffn_norm"] = _rms_bwd(A["x1"], row(ffn_norm[l]), dh2, D, 0, "ffn_norm_bwd", res=dx2)
        dmix = _mm(dx1b, W["wout"], "out_proj_dx", tb=True)
        G["w_out"] = _mm(A["mix"], dx1b, "dw_out", ta=True, out_dtype=BF16).reshape(NDEV, 4 * GW // NDEV, D)
        do_a, do_b, do_c, do_d, small["group_norm"] = _gn_bwd(
            [A["o_a"], A["o_b"], A["o_c"], A["o_d"]], row(group_norm[l]), dmix, "group_norm_bwd")
        (dq_d, dk_d, dv_d), (got[l, "w_up"],) = _stick_bwd(
            S, H, pb, lambda h: cb(lay.QD) + h, lambda h: cb(lay.KD) + h, lambda h: cb(lay.VD) + h,
            do_d, A["tot_d"], HEAD ** -0.5, "stick_bwd", comm=("exchange", [G["w_up"]]))
        (dq_c, dk_c, dv_c, dcc), moved = _attn_bwd(
            S, H, pb, lambda h: cb(lay.QC) + h, pb, lambda h: cb(lay.KC) + h, pb, lambda h: cb(lay.VC) + h,
            A["o_c"], do_c, A["lse_c"], HEAD ** -0.5, "fox_bwd" if late else "fox_bwd_top",
            ccol=A["ccol"], crow=A["crow"], comm=("exchange", [G["w_out"]] + (late or [])))
        got[l, "w_out"] = moved[0]
        got.update({(l + 1, n): s for n, s in zip(first3, moved[1:])})
        dfc, dbias = _fox_bwd(S, H, proj, cb(lay.FC), A["bias"], dcc, "fox_gate_bwd")
        small["fox_forget_bias"] = dbias[0, :H]
        qk_b = A["qk_b"]
        dq_b, dk_b, dv_b = _attn_bwd(
            S, H, qk_b, lambda h: h, qk_b, lambda h: H + h, pb, lambda h: cb(lay.VB) + h,
            A["o_b"], do_b, A["lse_b"], HEAD ** -0.5, "dilated_bwd", tab=tab, win=win)
        dqk_b = _rope(jnp.concatenate([dq_b, dk_b], 1), (0, 2 * H, 1), *neg(rope_full), False, BF16, "rope_qk_dil_bwd")
        qab, kv = A["qab"], A["kv"]
        (dq1, dk1, dv_a, dq2, dk2), (got[l, "w_gate"],) = _attn_bwd(
            S, H, qab, lambda h: 2 * h, kv, lambda h: 2 * h, kv, lambda h: 2 * h + 1,
            A["o_a"], do_a, A["lse_a"], QKA ** -0.5, "mla_bwd",
            q2=A["q_pe"], q2cb=lambda h: h, k2=A["k_pe"], k2cb=lambda h: 0, comm=("exchange", [G["w_gate"]]))
        dq2 = _rope(dq2, (0, H, 1), *neg(rope_half), True, BF16, "rope_q_mla_bwd")
        dk_pe = _rope(dk2.astype(F32).reshape(S, H, LANE).sum(1), (0, 1, 1), *neg(rope_half), True, BF16,
                      "rope_k_mla_bwd")
        dqa = jnp.stack([dq1.reshape(S, H, LANE), dq2.reshape(S, H, LANE)], 2).reshape(S, H * 2 * LANE)
        dkv = jnp.stack([dk1.reshape(S, H, LANE), dv_a.reshape(S, H, LANE)], 2).reshape(S, H * 2 * LANE)
        dwuq = _mm(A["qln"], dqa, "dw_uq", ta=True, out_dtype=BF16)
        dwuq = dwuq.reshape(Q_LORA, H, 2 * LANE)[:, :, :QKA].reshape(Q_LORA, NDEV, H * QKA // NDEV)
        G["w_uq"] = jnp.transpose(dwuq, (1, 0, 2))
        dwukv = _mm(A["kvln"], dkv, "dw_ukv", ta=True, out_dtype=BF16).reshape(KV_LORA, NDEV, H * 2 * LANE // NDEV)
        G["w_ukv"] = jnp.transpose(dwukv, (1, 0, 2))
        dqln = _mm(dqa, W["wuq"], "q_up_dx", tb=True)
        dkvln = _mm(dkv, W["wukv"], "kv_up_dx", tb=True)
        _, dql, small["mla_q_norm"] = _rms_bwd(proj, row(mla_q_norm[l]), dqln, Q_LORA, cb(lay.QL) // 4, "q_norm_bwd")
        _, dkvl, small["mla_kv_norm"] = _rms_bwd(
            proj, row(mla_kv_norm[l]), dkvln, KV_LORA, cb(lay.KVL) // 4, "kv_norm_bwd")
        dproj = jnp.concatenate([
            dql, dkvl, dqk_b, dv_b, dq_c, dk_c, dv_c, dq_d, dk_d, dv_d,
            dk_pe, dfc.astype(BF16), jnp.zeros((S, PW - lay.FC - LANE), BF16)], axis=1)
        G["w_in"] = _mm(A["h1"], dproj, "dw_in", ta=True, out_dtype=BF16).reshape(NDEV, D // NDEV, PW)
        late = [G[n] for n in first3]
        if l == 0:
            dh1, moved = _mm(dproj, W["win"], "in_proj_dx_last", tb=True, comm=("exchange", late))
            got.update({(0, n): s for n, s in zip(first3, moved)})
        else:
            dh1 = _mm(dproj, W["win"], "in_proj_dx", tb=True)
        dx0, dx0b, small["attn_norm"] = _rms_bwd(A["x0"], row(attn_norm[l]), dh1, D, 0, "attn_norm_bwd", res=dx1)
        return dx0, dx0b, late, got, small

    big = first3 + ["w_out", "w_gate", "w_up", "w_down"]
    Ws, As = [], []
    xc = x
    nxt = _comm_alone("gather", [sh[n][0] for n in first3], "gather_first")
    for l in range(L):
        W = first_weights(nxt)
        xc, A, nxt = forward(l, xc, W)
        Ws.append(W)
        As.append(A)
    dx, loss_part = _final_loss(xc, row(final_norm), target, "final_loss")
    dx, dxb, dfinal = _rms_bwd(xc, row(final_norm), dx, D, 0, "final_norm_bwd")
    slots = {}
    smalls = [None] * L
    late = None
    for l in reversed(range(L)):
        dx, dxb, late, got, smalls[l] = backward(l, dx, dxb, Ws[l], As[l], late)
        slots.update(got)

    names_small = ["attn_norm", "mla_q_norm", "mla_kv_norm", "fox_forget_bias", "group_norm", "ffn_norm"]
    params = dict(attn_norm=attn_norm, mla_q_norm=mla_q_norm, mla_kv_norm=mla_kv_norm, fox_forget_bias=fox_forget_bias,
                  group_norm=group_norm, ffn_norm=ffn_norm, final_norm=final_norm, w_in=w_in, w_uq=w_uq, w_ukv=w_ukv,
                  w_out=w_out, w_gate=w_gate, w_up=w_up, w_down=w_down)
    moms = dict(attn_norm=(m_attn_norm, v_attn_norm), mla_q_norm=(m_mla_q_norm, v_mla_q_norm),
                mla_kv_norm=(m_mla_kv_norm, v_mla_kv_norm), fox_forget_bias=(m_fox_forget_bias, v_fox_forget_bias),
                group_norm=(m_group_norm, v_group_norm), ffn_norm=(m_ffn_norm, v_ffn_norm),
                final_norm=(m_final_norm, v_final_norm), w_in=(m_w_in, v_w_in), w_uq=(m_w_uq, v_w_uq),
                w_ukv=(m_w_ukv, v_w_ukv), w_out=(m_w_out, v_w_out), w_gate=(m_w_gate, v_w_gate),
                w_up=(m_w_up, v_w_up), w_down=(m_w_down, v_w_down))
    small_list = names_small + ["final_norm"]
    small_grads = [jnp.stack([smalls[l][n].reshape(params[n].shape[1:]) for l in range(L)]) for n in names_small]
    small_grads.append(dfinal.reshape(final_norm.shape))
    shapes = [params[n].shape for n in small_list] + [(LANE,)]
    packed_g = _comm_alone("gather", [_pack(small_grads + [loss_part.reshape(LANE)])], "gather_small")[0]
    zero = jnp.zeros((LANE,), F32)
    res_small = _adamw(_pack([params[n] for n in small_list] + [zero]), packed_g,
                       _pack([moms[n][0] for n in small_list] + [zero]),
                       _pack([moms[n][1] for n in small_list] + [zero]), "adamw_small")
    unp = [_unpack(r, shapes) for r in res_small]
    out = {n: tuple(unp[k][i] for k in range(4)) for i, n in enumerate(small_list)}
    loss = unp[0][-1][0]

    for i, n in enumerate(big):
        st = jnp.stack([slots[l, n] for l in range(L)], axis=1)
        if n == "w_in":
            st = lay.unpad(st)
        C = st.shape[-1]
        st = st.reshape(NDEV, -1, C)
        w2 = params[n].reshape(-1, C)
        res = _adamw(w2, st, moms[n][0].reshape(-1, C), moms[n][1].reshape(-1, C), "adamw_" + n)
        out[n] = tuple(r.reshape(params[n].shape) for r in res)

    order = ["attn_norm", "w_in", "mla_q_norm", "w_uq", "mla_kv_norm", "w_ukv", "fox_forget_bias", "group_norm",
             "w_out", "ffn_norm", "w_gate", "w_up", "w_down", "final_norm"]
    return (loss, dx[None], *[out[n][0] for n in order], *[out[n][1] for n in order],
            *[out[n][2] for n in order], *[out[n][3] for n in order])
```

```python
import functools
import math

import numpy as np
import jax
import jax.numpy as jnp
from jax import lax
from jax.experimental import pallas as pl
from jax.experimental.pallas import tpu as pltpu

F32 = jnp.float32
BF16 = jnp.bfloat16
NDEV = 8
LANE = 128
HEAD = 128
Q_LORA = 512
KV_LORA = 512
QK_ROPE = 64
DILATED_PAIRS = ((128, 1), (512, 4), (2048, 16))
ROPE_THETA = 10000.0
EPS = 1e-6
NEG = -1e30
TQ = 256
TK = 128
TKS = 512
VMEM_LIMIT = 48 * 1024 * 1024
MM_OPERAND_BYTES = 20 * 1024 * 1024
FFN_BLOCKS_PER_STEP = 2
ADAM_LR, ADAM_B1, ADAM_B2, ADAM_EPS, ADAM_WD, ADAM_STEP = 0.001, 0.9, 0.999, 1e-08, 0.01, 10
MESH = pl.DeviceIdType.MESH
ANY = pl.BlockSpec(memory_space=pl.ANY)


def _cp(*sem):
    return pltpu.CompilerParams(dimension_semantics=sem, vmem_limit_bytes=VMEM_LIMIT)


def _dot(a, b, ca, cb):
    return lax.dot_general(a, b, (((ca,), (cb,)), ((), ())), preferred_element_type=F32)


def _dot_nn(a, b):
    return _dot(a, b, 1, 0)


def _dot_nt(a, b):
    return _dot(a, b, 1, 1)


def _dot_tn(a, b):
    return _dot(a, b, 0, 0)


def _tile(n, t):
    if n <= t:
        return n
    t -= t % LANE
    while n % t:
        t -= LANE
    return t


def _direct_copies(ins, outs, send_sems, recv_sems, local_sems, want_recvs=True):
    x, y, c = lax.axis_index("x"), lax.axis_index("y"), lax.axis_index("c")
    my_id = 4 * x + 2 * y + c
    local, sends, recvs = [], [], []
    for a in range(len(ins)):
        mine = ins[a].at[my_id]
        local.append(pltpu.make_async_copy(mine, outs[a].at[my_id], local_sems.at[a]))
        for k in range(1, NDEV):
            peer = (1 - x if k & 4 else x, 1 - y if k & 2 else y, 1 - c if k & 1 else c)
            pid = 4 * peer[0] + 2 * peer[1] + peer[2]
            sems = dict(send_sem=send_sems.at[a, k - 1], recv_sem=recv_sems.at[a, k - 1],
                        device_id=peer, device_id_type=MESH)
            sends.append(pltpu.make_async_remote_copy(src_ref=ins[a].at[pid], dst_ref=outs[a].at[my_id], **sems))
            if want_recvs:
                recvs.append(pltpu.make_async_remote_copy(src_ref=mine, dst_ref=outs[a].at[pid], **sems))
    return local, sends, recvs


def _comm_start(kind, ins, outs, send_sems, recv_sems, local_sems):
    if kind == "exchange":
        local, sends, _ = _direct_copies(ins, outs, send_sems, recv_sems, local_sems, want_recvs=False)
        for cp in local + sends:
            cp.start()
        return
    x, y, c = lax.axis_index("x"), lax.axis_index("y"), lax.axis_index("c")
    for a in range(len(ins)):
        mine = outs[a].at[4 * x + 2 * y + c]
        pltpu.make_async_copy(ins[a], mine, local_sems.at[a]).start()
        for k, to in enumerate([(x, y, 1 - c), (1 - x, y, c), (x, 1 - y, c), (1 - x, 1 - y, c)]):
            pltpu.make_async_remote_copy(src_ref=ins[a], dst_ref=mine, send_sem=send_sems.at[a, k],
                                         recv_sem=recv_sems.at[a, k], device_id=to, device_id_type=MESH).start()


def _comm_finish(kind, ins, outs, send_sems, recv_sems, local_sems):
    if kind == "exchange":
        local, sends, recvs = _direct_copies(ins, outs, send_sems, recv_sems, local_sems)
        for cp in recvs:
            cp.wait_recv()
        for cp in sends:
            cp.wait_send()
        for cp in local:
            cp.wait()
        return
    x, y, c = lax.axis_index("x"), lax.axis_index("y"), lax.axis_index("c")
    sibling = (x, y, 1 - c)
    chips = [(1 - x, y), (x, 1 - y), (1 - x, 1 - y)]
    for a in range(len(ins)):
        def copy(k, block, to):
            rows = outs[a].at[4 * block[0] + 2 * block[1] + block[2]]
            return pltpu.make_async_remote_copy(src_ref=rows, dst_ref=rows, send_sem=send_sems.at[a, k],
                                                recv_sem=recv_sems.at[a, k], device_id=to, device_id_type=MESH)

        passed = []
        for j, chip in enumerate(chips):
            copy(1 + j, (*chip, c), (x, y, c)).wait_recv()
            passed.append(copy(4 + j, (*chip, c), sibling))
            passed[-1].start()
        copy(0, sibling, (x, y, c)).wait_recv()
        for j, chip in enumerate(chips):
            copy(4 + j, (*chip, 1 - c), (x, y, c)).wait_recv()
        for k in range(4):
            copy(k, (x, y, c), sibling).wait_send()
        for cp in passed:
            cp.wait_send()
        pltpu.make_async_copy(ins[a], outs[a].at[4 * x + 2 * y + c], local_sems.at[a]).wait()


def _comm_shapes(kind, arrs):
    out_shape = [jax.ShapeDtypeStruct(((NDEV,) if kind == "gather" else ()) + a.shape, a.dtype) for a in arrs]
    n = len(arrs)
    sems = [pltpu.SemaphoreType.DMA((n, 7)), pltpu.SemaphoreType.DMA((n, 7)), pltpu.SemaphoreType.DMA((n,))]
    return out_shape, sems


def _comm_alone(kind, arrs, name):
    n = len(arrs)

    def body(*refs):
        _comm_start(kind, refs[:n], refs[n:2 * n], *refs[2 * n:])
        _comm_finish(kind, refs[:n], refs[n:2 * n], *refs[2 * n:])

    out_shape, sems = _comm_shapes(kind, arrs)
    return pl.pallas_call(body, name=name, out_shape=out_shape, in_specs=[ANY] * n, out_specs=[ANY] * n,
                          scratch_shapes=sems)(*arrs)


def _pcall(body, *, name, grid, in_specs, out_specs, out_shape, operands, sem, scratch_shapes=(), comm=None):
    in_specs, out_specs, out_shape = list(in_specs), list(out_specs), list(out_shape)
    scratch_shapes = list(scratch_shapes)
    if comm is None:
        res = pl.pallas_call(body, name=name, grid=grid, in_specs=in_specs, out_specs=out_specs, out_shape=out_shape,
                             scratch_shapes=scratch_shapes, compiler_params=_cp(*sem))(*operands)
        return list(res), []
    kind, arrs = comm
    nc, n_in, n_out, n_scr = len(arrs), len(operands), len(out_shape), len(scratch_shapes)
    c_shape, c_sems = _comm_shapes(kind, arrs)

    def carrier(*refs):
        ins, cin = refs[:n_in], refs[n_in:n_in + nc]
        outs = refs[n_in + nc:n_in + nc + n_out]
        cout = refs[n_in + nc + n_out:n_in + 2 * nc + n_out]
        scr = refs[n_in + 2 * nc + n_out:n_in + 2 * nc + n_out + n_scr]
        sems = refs[n_in + 2 * nc + n_out + n_scr:]
        pids = [pl.program_id(d) for d in range(len(grid))]
        first = functools.reduce(jnp.logical_and, [p == 0 for p in pids])
        last = functools.reduce(jnp.logical_and, [p == g - 1 for p, g in zip(pids, grid)])

        @pl.when(first)
        def _():
            _comm_start(kind, cin, cout, *sems)

        body(*ins, *outs, *scr)

        @pl.when(last)
        def _():
            _comm_finish(kind, cin, cout, *sems)

    res = pl.pallas_call(
        carrier, name=name, grid=grid, in_specs=in_specs + [ANY] * nc, out_specs=out_specs + [ANY] * nc,
        out_shape=out_shape + c_shape, scratch_shapes=scratch_shapes + c_sems,
        compiler_params=_cp(*["arbitrary"] * len(grid)))(*operands, *arrs)
    return list(res[:n_out]), list(res[n_out:])


def _mm_call(pairs, grid, a_spec, b_spec, o_spec, out_shape, acc_shape, nk, ca, cb, name,
             res=None, res_spec=None, comm=None):
    npairs = len(pairs)
    multi = isinstance(out_shape, (list, tuple))
    nout = len(out_shape) if multi else 1

    def body(*refs):
        ab = refs[:2 * npairs]
        r_ref = refs[2 * npairs] if res is not None else None
        o_refs, acc = refs[-1 - nout:-1], refs[-1]
        k = pl.program_id(2)

        @pl.when(k == 0)
        def _():
            acc[...] = jnp.zeros_like(acc)

        tot = None
        for p in range(npairs):
            av, bv = ab[2 * p][...].astype(BF16), ab[2 * p + 1][...].astype(BF16)
            if av.ndim == 2:
                terms = [(av, bv)]
            else:
                rows = bv.shape[0] // av.shape[0]
                terms = [(av[q], bv[q] if bv.ndim == 3 else bv[q * rows:(q + 1) * rows]) for q in range(av.shape[0])]
            for at, bt in terms:
                d = _dot(at, bt, ca, cb)
                tot = d if tot is None else tot + d
        acc[...] += tot

        @pl.when(k == nk - 1)
        def _():
            r = acc[...]
            if r_ref is not None:
                r = r + r_ref[...]
            for o_ref in o_refs:
                o_ref[...] = r.astype(o_ref.dtype)

    ops, specs = [], []
    for a, b in pairs:
        ops += [a, b]
        specs += [a_spec, b_spec]
    if res is not None:
        ops.append(res)
        specs.append(res_spec)
    outs, moved = _pcall(
        body, name=name, grid=grid, in_specs=specs, out_specs=[o_spec] * nout,
        out_shape=out_shape if multi else [out_shape], operands=ops,
        scratch_shapes=[pltpu.VMEM(acc_shape, F32)], sem=("parallel", "parallel", "arbitrary"), comm=comm)
    outs = outs if multi else outs[0]
    return outs if comm is None else (outs, moved)


def _k_tile(K, row_bytes, tk=2048):
    tk = _tile(K, tk)
    while 2 * tk * row_bytes > MM_OPERAND_BYTES and tk % 256 == 0:
        tk //= 2
    return tk


def _mm(a, b, name, ta=False, tb=False, out_dtype=F32, res=None, tm=1024, tn=1024, comm=None):
    M, K = (a.shape[1], a.shape[0]) if ta else a.shape
    N = b.shape[0] if tb else b.shape[1]
    tm, tn = _tile(M, tm), _tile(N, tn)
    tk = _k_tile(K, tm * a.dtype.itemsize + tn * b.dtype.itemsize)
    a_spec = pl.BlockSpec((tk, tm), lambda i, j, k: (k, i)) if ta else pl.BlockSpec((tm, tk), lambda i, j, k: (i, k))
    b_spec = pl.BlockSpec((tn, tk), lambda i, j, k: (j, k)) if tb else pl.BlockSpec((tk, tn), lambda i, j, k: (k, j))
    o_spec = pl.BlockSpec((tm, tn), lambda i, j, k: (i, j))
    if isinstance(out_dtype, tuple):
        out_shape = [jax.ShapeDtypeStruct((M, N), d) for d in out_dtype]
    else:
        out_shape = jax.ShapeDtypeStruct((M, N), out_dtype)
    return _mm_call([(a, b)], (M // tm, N // tn, K // tk), a_spec, b_spec, o_spec,
                    out_shape, (tm, tn), K // tk,
                    0 if ta else 1, 1 if tb else 0, name, res=res, res_spec=o_spec, comm=comm)


def _mm_down(act, wd, res, name, tm=1024, tn=1024):
    _, S, FB = act.shape
    D = wd.shape[1]
    tm, tn = _tile(S, tm), _tile(D, tn)
    o_spec = pl.BlockSpec((tm, tn), lambda i, j, k: (i, j))
    nb = FFN_BLOCKS_PER_STEP
    return _mm_call([(act, wd)], (S // tm, D // tn, NDEV // nb),
                    pl.BlockSpec((nb, tm, FB), lambda i, j, k: (k, i, 0)),
                    pl.BlockSpec((nb * FB, tn), lambda i, j, k: (k, j)), o_spec,
                    jax.ShapeDtypeStruct((S, D), F32), (tm, tn), NDEV // nb, 1, 0, name, res=res, res_spec=o_spec)


def _mm_dwdown(act, dy, name, tn=1024):
    _, S, FB = act.shape
    D = dy.shape[1]
    tn = _tile(D, tn)
    tk = _k_tile(S, FB * act.dtype.itemsize + tn * dy.dtype.itemsize)
    return _mm_call([(act, dy)], (NDEV, D // tn, S // tk),
                    pl.BlockSpec((None, tk, FB), lambda i, j, k: (i, k, 0)),
                    pl.BlockSpec((tk, tn), lambda i, j, k: (k, j)),
                    pl.BlockSpec((FB, tn), lambda i, j, k: (i, j)),
                    jax.ShapeDtypeStruct((NDEV * FB, D), BF16), (FB, tn), S // tk, 0, 0, name)


def _mm_dh2(dg, wg, du, wu, name, tm=1024, tn=1024, comm=None):
    _, S, FB = dg.shape
    D = wg.shape[1]
    tm, tn = _tile(S, tm), _tile(D, tn)
    nb = FFN_BLOCKS_PER_STEP
    return _mm_call([(dg, wg), (du, wu)], (S // tm, D // tn, NDEV // nb),
                    pl.BlockSpec((nb, tm, FB), lambda i, j, k: (k, i, 0)),
                    pl.BlockSpec((nb, tn, FB), lambda i, j, k: (k, j, 0)),
                    pl.BlockSpec((tm, tn), lambda i, j, k: (i, j)),
                    jax.ShapeDtypeStruct((S, D), F32), (tm, tn), NDEV // nb, 1, 1, name, comm=comm)


def _mm_dwgate(h2, dg, name, tm=1024):
    _, S, FB = dg.shape
    D = h2.shape[1]
    tm = _tile(D, tm)
    tk = _k_tile(S, tm * h2.dtype.itemsize + FB * dg.dtype.itemsize)
    return _mm_call([(h2, dg)], (NDEV, D // tm, S // tk),
                    pl.BlockSpec((tk, tm), lambda p, i, k: (k, i)),
                    pl.BlockSpec((None, tk, FB), lambda p, i, k: (p, k, 0)),
                    pl.BlockSpec((None, tm, FB), lambda p, i, k: (p, i, 0)),
                    jax.ShapeDtypeStruct((NDEV, D, FB), BF16), (tm, FB), S // tk, 0, 0, name)


def _ffn_up(h2, wg, wu, name, tm=512, comm=None):
    S, D = h2.shape
    FB = wg.shape[2]
    tm = _tile(S, tm)

    def body(h_ref, wg_ref, wu_ref, g_ref, u_ref, act_ref):
        h = h_ref[...]
        g = _dot_nn(h, wg_ref[...])
        u = _dot_nn(h, wu_ref[...])
        g_ref[...] = g.astype(BF16)
        u_ref[...] = u.astype(BF16)
        act_ref[...] = (g / (1.0 + jnp.exp(-g)) * u).astype(BF16)

    w_spec = pl.BlockSpec((None, D, FB), lambda p, i: (p, 0, 0))
    o_spec = pl.BlockSpec((None, tm, FB), lambda p, i: (p, i, 0))
    shp = (NDEV, S, FB)
    outs, moved = _pcall(
        body, name=name, grid=(NDEV, S // tm),
        in_specs=[pl.BlockSpec((tm, D), lambda p, i: (i, 0)), w_spec, w_spec],
        out_specs=[o_spec, o_spec, o_spec],
        out_shape=[jax.ShapeDtypeStruct(shp, BF16)] * 3,
        operands=[h2, wg, wu], sem=("parallel", "parallel"), comm=comm)
    return outs if comm is None else (outs, moved)


def _ffn_dact(dy, wd, g, u, name, tm=512):
    S, D = dy.shape
    FB = g.shape[2]
    tm = _tile(S, tm)

    def body(dy_ref, wd_ref, g_ref, u_ref, dg_ref, du_ref):
        dact = _dot_nt(dy_ref[...], wd_ref[...])
        gv = g_ref[...].astype(F32)
        sg = 1.0 / (1.0 + jnp.exp(-gv))
        dg_ref[...] = (dact * u_ref[...].astype(F32) * (sg * (1.0 + gv * (1.0 - sg)))).astype(BF16)
        du_ref[...] = (dact * (gv * sg)).astype(BF16)

    t_spec = pl.BlockSpec((None, tm, FB), lambda p, i: (p, i, 0))
    shp = jax.ShapeDtypeStruct((NDEV, S, FB), BF16)
    return pl.pallas_call(
        body, name=name, grid=(NDEV, S // tm),
        in_specs=[pl.BlockSpec((tm, D), lambda p, i: (i, 0)), pl.BlockSpec((FB, D), lambda p, i: (p, 0)),
                  t_spec, t_spec],
        out_specs=[t_spec, t_spec], out_shape=[shp, shp],
        compiler_params=_cp("parallel", "parallel"),
    )(dy, wd, g, u)


def _rms_fwd(x, gain, width, cb, out_dtype, name, ts=512):
    S = x.shape[0]
    ts = _tile(S, ts)

    def body(x_ref, g_ref, o_ref):
        xv = x_ref[...]
        r = lax.rsqrt(jnp.mean(xv * xv, axis=1, keepdims=True) + EPS)
        o_ref[...] = (xv * r * g_ref[...]).astype(o_ref.dtype)

    return pl.pallas_call(
        body, name=name, grid=(S // ts,),
        in_specs=[pl.BlockSpec((ts, width), lambda i: (i, cb)), pl.BlockSpec((1, width), lambda i: (0, 0))],
        out_specs=pl.BlockSpec((ts, width), lambda i: (i, 0)),
        out_shape=jax.ShapeDtypeStruct((S, width), out_dtype),
        compiler_params=_cp("parallel"),
    )(x, gain)


def _rms_bwd(x, gain, dy, width, cb, name, res=None, ts=256):
    S = x.shape[0]
    ts = _tile(S, ts)
    has_res = res is not None

    def body(*refs):
        x_ref, g_ref, dy_ref = refs[:3]
        r_ref = refs[3] if has_res else None
        dx_ref, dxb_ref, dg_ref = refs[-3], refs[-2], refs[-1]

        @pl.when(pl.program_id(0) == 0)
        def _():
            dg_ref[...] = jnp.zeros_like(dg_ref)

        xv = x_ref[...]
        r = lax.rsqrt(jnp.mean(xv * xv, axis=1, keepdims=True) + EPS)
        xh = xv * r
        dyv = dy_ref[...]
        dyg = dyv * g_ref[...]
        dx = r * (dyg - xh * jnp.mean(dyg * xh, axis=1, keepdims=True))
        if has_res:
            dx = dx + r_ref[...]
        dx_ref[...] = dx
        dxb_ref[...] = dx.astype(BF16)
        dg_ref[...] += jnp.sum(dyv * xh, axis=0, keepdims=True)

    row = pl.BlockSpec((ts, width), lambda i: (i, 0))
    vec = pl.BlockSpec((1, width), lambda i: (0, 0))
    ops = [x, gain, dy] + ([res] if has_res else [])
    specs = [pl.BlockSpec((ts, width), lambda i: (i, cb)), vec, row] + ([row] if has_res else [])
    return pl.pallas_call(
        body, name=name, grid=(S // ts,), in_specs=specs, out_specs=[row, row, vec],
        out_shape=[jax.ShapeDtypeStruct((S, width), F32), jax.ShapeDtypeStruct((S, width), BF16),
                   jax.ShapeDtypeStruct((1, width), F32)],
        compiler_params=_cp("arbitrary"),
    )(*ops)


def _gn_fwd(outs, gain, name, ts=512):
    S, GW = outs[0].shape
    ts = _tile(S, ts)

    def body(a_ref, b_ref, c_ref, d_ref, g_ref, o_ref):
        for g, r_ref in enumerate((a_ref, b_ref, c_ref, d_ref)):
            xv = r_ref[...]
            r = lax.rsqrt(jnp.mean(xv * xv, axis=1, keepdims=True) + EPS)
            o_ref[:, g * GW:(g + 1) * GW] = (xv * r * g_ref[:, g * GW:(g + 1) * GW]).astype(BF16)

    row = pl.BlockSpec((ts, GW), lambda i: (i, 0))
    return pl.pallas_call(
        body, name=name, grid=(S // ts,),
        in_specs=[row] * 4 + [pl.BlockSpec((1, 4 * GW), lambda i: (0, 0))],
        out_specs=pl.BlockSpec((ts, 4 * GW), lambda i: (i, 0)),
        out_shape=jax.ShapeDtypeStruct((S, 4 * GW), BF16),
        compiler_params=_cp("parallel"),
    )(*outs, gain)


def _gn_bwd(outs, gain, dmix, name, ts=256):
    S, GW = outs[0].shape
    ts = _tile(S, ts)

    def body(a_ref, b_ref, c_ref, d_ref, g_ref, dm_ref, da_ref, db_ref, dc_ref, dd_ref, dg_ref):
        @pl.when(pl.program_id(0) == 0)
        def _():
            dg_ref[...] = jnp.zeros_like(dg_ref)

        for g, (r_ref, o_ref) in enumerate(zip((a_ref, b_ref, c_ref, d_ref), (da_ref, db_ref, dc_ref, dd_ref))):
            sl = slice(g * GW, (g + 1) * GW)
            xv = r_ref[...]
            r = lax.rsqrt(jnp.mean(xv * xv, axis=1, keepdims=True) + EPS)
            xh = xv * r
            dyv = dm_ref[:, sl]
            dyg = dyv * g_ref[:, sl]
            o_ref[...] = (r * (dyg - xh * jnp.mean(dyg * xh, axis=1, keepdims=True))).astype(BF16)
            dg_ref[:, sl] += jnp.sum(dyv * xh, axis=0, keepdims=True)

    row = pl.BlockSpec((ts, GW), lambda i: (i, 0))
    vec = pl.BlockSpec((1, 4 * GW), lambda i: (0, 0))
    return pl.pallas_call(
        body, name=name, grid=(S // ts,),
        in_specs=[row] * 4 + [vec, pl.BlockSpec((ts, 4 * GW), lambda i: (i, 0))],
        out_specs=[row] * 4 + [vec],
        out_shape=[jax.ShapeDtypeStruct((S, GW), BF16)] * 4 + [jax.ShapeDtypeStruct((1, 4 * GW), F32)],
        compiler_params=_cp("arbitrary"),
    )(*outs, gain, dmix)


def _rope(x, cbs, cos, sin, half, out_dtype, name, ts=512):
    S = x.shape[0]
    cb0, nb, stride = cbs
    ts = _tile(S, ts)

    def body(x_ref, c_ref, s_ref, o_ref):
        xv = x_ref[...].astype(F32)
        if half:
            lane = lax.broadcasted_iota(jnp.int32, xv.shape, 1)
            partner = jnp.where(lane % 64 < 32, pltpu.roll(xv, LANE - 32, 1), pltpu.roll(xv, 32, 1))
        else:
            partner = pltpu.roll(xv, 64, 1)
        o_ref[...] = (xv * c_ref[...] + partner * s_ref[...]).astype(o_ref.dtype)

    tab = pl.BlockSpec((ts, LANE), lambda i, j: (i, 0))
    return pl.pallas_call(
        body, name=name, grid=(S // ts, nb),
        in_specs=[pl.BlockSpec((ts, LANE), lambda i, j: (i, cb0 + stride * j)), tab, tab],
        out_specs=pl.BlockSpec((ts, LANE), lambda i, j: (i, j)),
        out_shape=jax.ShapeDtypeStruct((S, nb * LANE), out_dtype),
        compiler_params=_cp("parallel", "parallel"),
    )(x, cos, sin)


def _final_loss(x, gain, target, name, ts=256):
    S, D = x.shape
    ts = _tile(S, ts)

    def body(x_ref, g_ref, t_ref, dy_ref, l_ref):
        @pl.when(pl.program_id(0) == 0)
        def _():
            l_ref[...] = jnp.zeros_like(l_ref)

        xv = x_ref[...]
        r = lax.rsqrt(jnp.mean(xv * xv, axis=1, keepdims=True) + EPS)
        err = xv * r * g_ref[...] - t_ref[...]
        dy_ref[...] = err * (1.0 / D)
        part = jnp.sum(jnp.mean(err * err, axis=1, keepdims=True), axis=0, keepdims=True)
        l_ref[...] += jnp.broadcast_to(0.5 * part, (1, LANE))

    row = pl.BlockSpec((ts, D), lambda i: (i, 0))
    return pl.pallas_call(
        body, name=name, grid=(S // ts,),
        in_specs=[row, pl.BlockSpec((1, D), lambda i: (0, 0)), row],
        out_specs=[row, pl.BlockSpec((1, LANE), lambda i: (0, 0))],
        out_shape=[jax.ShapeDtypeStruct((S, D), F32), jax.ShapeDtypeStruct((1, LANE), F32)],
        compiler_params=_cp("arbitrary"),
    )(x, gain, target)


def _colspec(rows, f):
    return pl.BlockSpec((rows, LANE), f)


def _soft_tiles(S):
    tq = _tile(S, TQ)
    tk = _tile(S, TKS)
    assert tk % tq == 0
    return tq, tk


def _key_row(crow_ref, j, tk):
    n = tk // TK
    return jnp.concatenate([crow_ref[j * n + c] for c in range(n)], axis=1)


def _attn_fwd(S, H, q1, q1cb, k1, k1cb, v, vcb, scale, name, q2=None, q2cb=None, k2=None, k2cb=None,
              tab=None, win=None, ccol=None, crow=None, comm=None):
    tq, tk = _soft_tiles(S)
    has2, hastab, hasc = q2 is not None, tab is not None, ccol is not None

    def body(*refs):
        it = iter(refs)
        q1r, k1r, vr = next(it), next(it), next(it)
        q2r, k2r = (next(it), next(it)) if has2 else (None, None)
        tabr = next(it) if hastab else None
        ccolr, crowr = (next(it), next(it)) if hasc else (None, None)
        o_ref, lse_ref = next(it), next(it)
        i = pl.program_id(1)
        q = q1r[...]
        qb2 = q2r[...] if has2 else None
        cq = ccolr[:, 0:1] if hasc else None
        qpos = i * tq + lax.broadcasted_iota(jnp.int32, (tq, tk), 0)
        kio = lax.broadcasted_iota(jnp.int32, (tq, tk), 1)
        j_diag = (i * tq) // tk
        j_lo = jnp.maximum((i * tq - win) // tk, 0) if win else 0

        if has2:
            q = jnp.concatenate([q, qb2], axis=1)

        def step(j, carry, masked):
            m, l, acc = carry
            off = pl.multiple_of(j * tk, tk)
            kb = k1r[pl.ds(off, tk), :]
            if has2:
                kb = jnp.concatenate([kb, k2r[pl.ds(off, tk), :]], axis=1)
            s = _dot_nt(q, kb) * scale
            if hastab:
                s = s + tabr[i - j * (tk // tq)]
            else:
                if hasc:
                    s = s + (cq - _key_row(crowr, j, tk))
                if masked:
                    s = jnp.where(kio + j * tk <= qpos, s, NEG)
            mn = jnp.maximum(m, jnp.max(s, axis=1, keepdims=True))
            p = jnp.exp(s - mn)
            al = jnp.exp(m - mn)
            l = al * l + jnp.sum(p, axis=1, keepdims=True)
            acc = al * acc + _dot_nn(p.astype(BF16), vr[pl.ds(off, tk), :])
            return mn, l, acc

        carry = (jnp.full((tq, 1), NEG, F32), jnp.zeros((tq, 1), F32), jnp.zeros((tq, LANE), F32))
        if hastab:
            carry = lax.fori_loop(j_lo, j_diag + 1, functools.partial(step, masked=False), carry)
        else:
            carry = lax.fori_loop(j_lo, j_diag, functools.partial(step, masked=False), carry)
            carry = step(j_diag, carry, True)
        m, l, acc = carry
        o_ref[...] = acc / l
        lse_ref[...] = jnp.broadcast_to(m + jnp.log(l), (tq, LANE))

    ops = [q1, k1, v]
    specs = [_colspec(tq, lambda h, i: (i, q1cb(h))), _colspec(S, lambda h, i: (0, k1cb(h))),
             _colspec(S, lambda h, i: (0, vcb(h)))]
    if has2:
        ops += [q2, k2]
        specs += [_colspec(tq, lambda h, i: (i, q2cb(h))), _colspec(S, lambda h, i: (0, k2cb(h)))]
    if hastab:
        ops.append(tab)
        specs.append(pl.BlockSpec(tab.shape, lambda h, i: (0, 0, 0)))
    if hasc:
        ops += [ccol, crow]
        specs += [_colspec(tq, lambda h, i: (i, h)),
                  pl.BlockSpec((None, S // TK, 1, TK), lambda h, i: (h, 0, 0, 0))]
    o_spec = _colspec(tq, lambda h, i: (i, h))
    shp = jax.ShapeDtypeStruct((S, H * LANE), F32)
    outs, moved = _pcall(
        body, name=name, grid=(H, S // tq), in_specs=specs, out_specs=[o_spec, o_spec], out_shape=[shp, shp],
        operands=ops, sem=("parallel", "arbitrary"), comm=comm)
    return outs if comm is None else (outs, moved)


def _attn_bwd(S, H, q1, q1cb, k1, k1cb, v, vcb, o, do, lse, scale, name, q2=None, q2cb=None, k2=None, k2cb=None,
              tab=None, win=None, ccol=None, crow=None, comm=None):
    tq, tk = _soft_tiles(S)
    has2, hastab, hasc = q2 is not None, tab is not None, ccol is not None

    def body(*refs):
        it = iter(refs)
        q1r, k1r, vr, o_r, do_r, lse_r = (next(it) for _ in range(6))
        q2r, k2r = (next(it), next(it)) if has2 else (None, None)
        tabr = next(it) if hastab else None
        ccolr, crowr = (next(it), next(it)) if hasc else (None, None)
        dq1_r, dk1_o, dv_o = next(it), next(it), next(it)
        dq2_r, dk2_o = (next(it), next(it)) if has2 else (None, None)
        dcr_r = next(it) if hasc else None
        dk1_r, dv_r = next(it), next(it)
        dk2_r = next(it) if has2 else None
        i = pl.program_id(1)

        @pl.when(i == 0)
        def _():
            dk1_r[...] = jnp.zeros_like(dk1_r)
            dv_r[...] = jnp.zeros_like(dv_r)
            if has2:
                dk2_r[...] = jnp.zeros_like(dk2_r)
            if hasc:
                dcr_r[...] = jnp.zeros_like(dcr_r)

        q = q1r[...]
        qb2 = q2r[...] if has2 else None
        dob = do_r[...]
        delta = jnp.sum(dob.astype(F32) * o_r[...], axis=1, keepdims=True)
        lse_c = lse_r[:, 0:1]
        cq = ccolr[:, 0:1] if hasc else None
        qpos = i * tq + lax.broadcasted_iota(jnp.int32, (tq, tk), 0)
        kio = lax.broadcasted_iota(jnp.int32, (tq, tk), 1)
        j_diag = (i * tq) // tk
        j_lo = jnp.maximum((i * tq - win) // tk, 0) if win else 0

        if has2:
            q = jnp.concatenate([q, qb2], axis=1)

        def probs(j, masked):
            off = pl.multiple_of(j * tk, tk)
            kb = k1r[pl.ds(off, tk), :]
            if has2:
                kb = jnp.concatenate([kb, k2r[pl.ds(off, tk), :]], axis=1)
            s = _dot_nt(q, kb) * scale
            if hastab:
                s = s + tabr[i - j * (tk // tq)]
            else:
                if hasc:
                    s = s + (cq - _key_row(crowr, j, tk))
                if masked:
                    s = jnp.where(kio + j * tk <= qpos, s, NEG)
            p = jnp.exp(s - lse_c)
            dp = _dot_nt(dob, vr[pl.ds(off, tk), :])
            return off, kb, p, dp

        def sweep(fn, carry):
            if hastab:
                return lax.fori_loop(j_lo, j_diag + 1, functools.partial(fn, masked=False), carry)
            carry = lax.fori_loop(j_lo, j_diag, functools.partial(fn, masked=False), carry)
            return fn(j_diag, carry, True)

        if hasc:
            def dstep(j, acc, masked):
                _, _, p, dp = probs(j, masked)
                return acc + jnp.sum(p * dp, axis=1, keepdims=True)

            delta = sweep(dstep, jnp.zeros((tq, 1), F32))

        def step(j, dq, masked):
            off, kb, p, dp = probs(j, masked)
            ds = p * (dp - delta)
            dsb = ds.astype(BF16)
            dk = _dot_tn(dsb, q) * scale
            dk1_r[pl.ds(off, tk), :] += dk[:, :LANE]
            if has2:
                dk2_r[pl.ds(off, tk), :] += dk[:, LANE:]
            dv_r[pl.ds(off, tk), :] += _dot_tn(p.astype(BF16), dob)
            if hasc:
                cs = -jnp.sum(ds, axis=0, keepdims=True)
                for c in range(tk // TK):
                    dcr_r[j * (tk // TK) + c] += cs[:, c * TK:(c + 1) * TK]
            return dq + _dot_nn(dsb, kb)

        dq = sweep(step, jnp.zeros(q.shape, F32)) * scale
        dq1_r[...] = dq[:, :LANE].astype(BF16)
        if has2:
            dq2_r[...] = dq[:, LANE:].astype(BF16)

        @pl.when(i == S // tq - 1)
        def _():
            dk1_o[...] = dk1_r[...].astype(BF16)
            dv_o[...] = dv_r[...].astype(BF16)
            if has2:
                dk2_o[...] = dk2_r[...].astype(BF16)

    qspec = _colspec(tq, lambda h, i: (i, h))
    kspec = _colspec(S, lambda h, i: (0, h))
    ops = [q1, k1, v, o, do, lse]
    specs = [_colspec(tq, lambda h, i: (i, q1cb(h))), _colspec(S, lambda h, i: (0, k1cb(h))),
             _colspec(S, lambda h, i: (0, vcb(h))), qspec, qspec, qspec]
    if has2:
        ops += [q2, k2]
        specs += [_colspec(tq, lambda h, i: (i, q2cb(h))), _colspec(S, lambda h, i: (0, k2cb(h)))]
    if hastab:
        ops.append(tab)
        specs.append(pl.BlockSpec(tab.shape, lambda h, i: (0, 0, 0)))
    if hasc:
        ops += [ccol, crow]
        specs += [qspec, pl.BlockSpec((None, S // TK, 1, TK), lambda h, i: (h, 0, 0, 0))]
    assert all(t.dtype == BF16 for t in ops[:3] + [do] + ([q2, k2] if has2 else []))
    out_specs = [qspec, kspec, kspec] + ([qspec, kspec] if has2 else [])
    shp = jax.ShapeDtypeStruct((S, H * LANE), BF16)
    out_shape = [shp] * len(out_specs)
    if hasc:
        out_specs.append(pl.BlockSpec((None, S // TK, 1, TK), lambda h, i: (h, 0, 0, 0)))
        out_shape.append(jax.ShapeDtypeStruct((H, S // TK, 1, TK), F32))
    outs, moved = _pcall(
        body, name=name, grid=(H, S // tq), in_specs=specs, out_specs=out_specs, out_shape=out_shape,
        operands=ops, scratch_shapes=[pltpu.VMEM((S, LANE), F32)] * (3 if has2 else 2),
        sem=("parallel", "arbitrary"), comm=comm)
    return outs if comm is None else (outs, moved)


def _scan_matrix(kind):
    j = np.arange(TK)[:, None]
    s = np.arange(TK)[None, :]
    tri = {"suffix_ex": j > s, "prefix_in": j <= s, "prefix_ex": j < s}[kind].astype(np.float32)
    half = np.concatenate([tri, np.ones((TK, TK), np.float32)], axis=1)
    return jnp.asarray(np.concatenate([half, half], axis=0), BF16)


def _scan_mxu(x, mat, carry, reverse):
    n = x.shape[1] // TK
    hi = x.astype(BF16)
    lo = (x - hi.astype(F32)).astype(BF16)
    parts = [None] * n
    for b in (reversed(range(n)) if reverse else range(n)):
        sl = slice(b * TK, (b + 1) * TK)
        r = _dot_nn(jnp.concatenate([hi[:, sl], lo[:, sl]], axis=1), mat)
        parts[b] = r[:, :TK] + carry
        carry = carry + r[:, TK:]
    return jnp.concatenate(parts, axis=1), carry


def _stick_logs(z):
    e = jnp.exp(-jnp.abs(z))
    return e, -jnp.maximum(z, 0.0) - jnp.log(1.0 + e)


def _stick_fwd(S, H, x, qcb, kcb, vcb, scale, name, comm=None):
    tq, tk = _soft_tiles(S)
    assert x.dtype == BF16

    def body(q_r, k_r, v_r, mat_r, o_ref, t_ref):
        i = pl.program_id(1)
        q = q_r[...]
        qpos = i * tq + lax.broadcasted_iota(jnp.int32, (tq, tk), 0)
        lane = lax.broadcasted_iota(jnp.int32, (tq, tk), 1)
        j_diag = (i * tq) // tk

        def step(j, carry, masked):
            c, acc = carry
            off = pl.multiple_of(j * tk, tk)
            z = _dot_nt(q, k_r[pl.ds(off, tk), :]) * scale
            _, lk = _stick_logs(z)
            if masked:
                past = lane + j * tk < qpos
                lk = jnp.where(past, lk, 0.0)
            suf, c = _scan_mxu(lk, mat_r[...], c, True)
            a = jnp.exp(z + lk + suf)
            if masked:
                a = jnp.where(past, a, 0.0)
            acc = acc + _dot_nn(a.astype(BF16), v_r[pl.ds(off, tk), :])
            return c, acc

        carry = step(j_diag, (jnp.zeros((tq, TK), F32), jnp.zeros((tq, LANE), F32)), True)
        c, acc = lax.fori_loop(0, j_diag, lambda jj, cr: step(j_diag - 1 - jj, cr, False), carry)
        o_ref[...] = acc
        t_ref[...] = c

    o_spec = _colspec(tq, lambda h, i: (i, h))
    shp = jax.ShapeDtypeStruct((S, H * LANE), F32)
    mat = _scan_matrix("suffix_ex")
    outs, moved = _pcall(
        body, name=name, grid=(H, S // tq),
        in_specs=[_colspec(tq, lambda h, i: (i, qcb(h))), _colspec(S, lambda h, i: (0, kcb(h))),
                  _colspec(S, lambda h, i: (0, vcb(h))), pl.BlockSpec(mat.shape, lambda h, i: (0, 0))],
        out_specs=[o_spec, o_spec], out_shape=[shp, shp],
        operands=[x, x, x, mat], sem=("parallel", "arbitrary"), comm=comm)
    return outs if comm is None else (outs, moved)


def _stick_bwd(S, H, x, qcb, kcb, vcb, do, tot, scale, name, comm=None):
    tq, tk = _soft_tiles(S)
    assert x.dtype == BF16 and do.dtype == BF16

    def body(q_r, k_r, v_r, do_r, t_r, pin_r, pex_r, dq_r, dk_o, dv_o, dk_r, dv_r):
        i = pl.program_id(1)

        @pl.when(i == 0)
        def _():
            dk_r[...] = jnp.zeros_like(dk_r)
            dv_r[...] = jnp.zeros_like(dv_r)

        q = q_r[...]
        dob = do_r[...]
        total = jnp.concatenate([t_r[...]] * (tk // TK), axis=1)
        qpos = i * tq + lax.broadcasted_iota(jnp.int32, (tq, tk), 0)
        lane = lax.broadcasted_iota(jnp.int32, (tq, tk), 1)
        j_diag = (i * tq) // tk

        def step(j, carry, masked):
            cl, cg, dq = carry
            off = pl.multiple_of(j * tk, tk)
            kb = k_r[pl.ds(off, tk), :]
            z = _dot_nt(q, kb) * scale
            e, lk = _stick_logs(z)
            if masked:
                past = lane + j * tk < qpos
                lk = jnp.where(past, lk, 0.0)
            pre, cl = _scan_mxu(lk, pin_r[...], cl, False)
            a = jnp.exp(z + lk + (total - pre))
            if masked:
                a = jnp.where(past, a, 0.0)
            g = _dot_nt(dob, v_r[pl.ds(off, tk), :]) * a
            gpre, cg = _scan_mxu(g, pex_r[...], cg, False)
            inv = 1.0 / (1.0 + e)
            small = e * inv
            pos = z >= 0
            dz = g * jnp.where(pos, small, inv) - jnp.where(pos, inv, small) * gpre
            if masked:
                dz = jnp.where(past, dz, 0.0)
            dzb = dz.astype(BF16)
            dk_r[pl.ds(off, tk), :] += _dot_tn(dzb, q) * scale
            dv_r[pl.ds(off, tk), :] += _dot_tn(a.astype(BF16), dob)
            return cl, cg, dq + _dot_nn(dzb, kb)

        zt = jnp.zeros((tq, TK), F32)
        carry = lax.fori_loop(0, j_diag, functools.partial(step, masked=False), (zt, zt, jnp.zeros((tq, LANE), F32)))
        dq_r[...] = (step(j_diag, carry, True)[2] * scale).astype(BF16)

        @pl.when(i == S // tq - 1)
        def _():
            dk_o[...] = dk_r[...].astype(BF16)
            dv_o[...] = dv_r[...].astype(BF16)

    qspec = _colspec(tq, lambda h, i: (i, h))
    kspec = _colspec(S, lambda h, i: (0, h))
    shp = jax.ShapeDtypeStruct((S, H * LANE), BF16)
    pin, pex = _scan_matrix("prefix_in"), _scan_matrix("prefix_ex")
    mspec = pl.BlockSpec(pin.shape, lambda h, i: (0, 0))
    outs, moved = _pcall(
        body, name=name, grid=(H, S // tq),
        in_specs=[_colspec(tq, lambda h, i: (i, qcb(h))), _colspec(S, lambda h, i: (0, kcb(h))),
                  _colspec(S, lambda h, i: (0, vcb(h))), qspec, qspec, mspec, mspec],
        out_specs=[qspec, kspec, kspec], out_shape=[shp] * 3,
        operands=[x, x, x, do, tot, pin, pex], scratch_shapes=[pltpu.VMEM((S, LANE), F32)] * 2,
        sem=("parallel", "arbitrary"), comm=comm)
    return outs if comm is None else (outs, moved)


def _scan8(x, rows, reverse):
    for sh in (1, 2, 4):
        if reverse:
            x = x + jnp.where(rows + sh < 8, pltpu.roll(x, 8 - sh, 0), 0.0)
        else:
            x = x + jnp.where(rows >= sh, pltpu.roll(x, sh, 0), 0.0)
    return x


def _fox_prep(S, H, proj, fcb, bias, name):
    tk = TK

    def body(f_ref, b_ref, ccol_ref, crow_ref, scr):
        rows = lax.broadcasted_iota(jnp.int32, (8, LANE), 0)

        def step(t, carry):
            off = pl.multiple_of(t * 8, 8)
            xb = f_ref[pl.ds(off, 8), :] + b_ref[...]
            lf = jnp.minimum(xb, 0.0) - jnp.log(1.0 + jnp.exp(-jnp.abs(xb)))
            lf = _scan8(lf, rows, False) + carry
            scr[pl.ds(off, 8), :] = lf
            return lf[7:8, :]

        lax.fori_loop(0, S // 8, step, jnp.zeros((1, LANE), F32))
        for h in range(H):
            ccol_ref[:, h * LANE:(h + 1) * LANE] = jnp.broadcast_to(scr[:, h:h + 1], (S, LANE))

            def tr(t, _):
                off = pl.multiple_of(t * tk, tk)
                blk = ccol_ref[pl.ds(off, tk), h * LANE:(h + 1) * LANE]
                crow_ref[h, t] = blk.T[0:1, :]
                return 0

            lax.fori_loop(0, S // tk, tr, 0)

    return pl.pallas_call(
        body, name=name, grid=(1,),
        in_specs=[_colspec(S, lambda i: (0, fcb)), pl.BlockSpec((1, LANE), lambda i: (0, 0))],
        out_specs=[pl.BlockSpec((S, H * LANE), lambda i: (0, 0)),
                   pl.BlockSpec((H, S // tk, 1, tk), lambda i: (0, 0, 0, 0))],
        out_shape=[jax.ShapeDtypeStruct((S, H * LANE), F32), jax.ShapeDtypeStruct((H, S // tk, 1, tk), F32)],
        scratch_shapes=[pltpu.VMEM((S, LANE), F32)],
        compiler_params=_cp("arbitrary"),
    )(proj, bias)


def _fox_bwd(S, H, proj, fcb, bias, dcr, name):
    tk = TK

    def body(f_ref, b_ref, dcr_ref, df_ref, db_ref, scr):
        rows = lax.broadcasted_iota(jnp.int32, (8, LANE), 0)
        lane_t = lax.broadcasted_iota(jnp.int32, (tk, LANE), 1)
        nb = S // 8

        def tr(t, _):
            off = pl.multiple_of(t * tk, tk)
            d = jnp.zeros((tk, LANE), F32)
            for h in range(H):
                d = d + jnp.where(lane_t == h, jnp.broadcast_to(dcr_ref[h, t], (LANE, tk)).T, 0.0)
            scr[pl.ds(off, tk), :] = d
            return 0

        lax.fori_loop(0, S // tk, tr, 0)

        def step(tt, carry):
            suffix, db = carry
            off = pl.multiple_of((nb - 1 - tt) * 8, 8)
            d = _scan8(scr[pl.ds(off, 8), :], rows, True) + suffix
            xb = f_ref[pl.ds(off, 8), :] + b_ref[...]
            e = jnp.exp(-jnp.abs(xb))
            dx = d * jnp.where(xb >= 0, e, 1.0) / (1.0 + e)
            df_ref[pl.ds(off, 8), :] = dx
            return d[0:1, :], db + jnp.sum(dx, axis=0, keepdims=True)

        z = jnp.zeros((1, LANE), F32)
        _, db = lax.fori_loop(0, nb, step, (z, z))
        db_ref[...] = db

    return pl.pallas_call(
        body, name=name, grid=(1,),
        in_specs=[_colspec(S, lambda i: (0, fcb)), pl.BlockSpec((1, LANE), lambda i: (0, 0)),
                  pl.BlockSpec((H, S // tk, 1, tk), lambda i: (0, 0, 0, 0))],
        out_specs=[pl.BlockSpec((S, LANE), lambda i: (0, 0)), pl.BlockSpec((1, LANE), lambda i: (0, 0))],
        out_shape=[jax.ShapeDtypeStruct((S, LANE), F32), jax.ShapeDtypeStruct((1, LANE), F32)],
        scratch_shapes=[pltpu.VMEM((S, LANE), F32)],
        compiler_params=_cp("arbitrary"),
    )(proj, bias, dcr)


def _adamw(w, slots, m, v, name, block_bytes=1 << 20):
    R, C = w.shape
    parts = len(slots)
    tr = R // parts
    while tr * C * 4 > block_bytes and tr % 16 == 0:
        tr //= 2
    per = R // parts // tr
    c1 = 1.0 - ADAM_B1 ** ADAM_STEP
    c2 = 1.0 - ADAM_B2 ** ADAM_STEP

    def body(*refs):
        w_ref, m_ref, v_ref = refs[:3]
        s_refs = refs[3:3 + parts]
        g_ref, d_ref, nm_ref, nv_ref = refs[3 + parts:]
        part = pl.program_id(0) // per
        for a, s_ref in enumerate(s_refs):
            @pl.when(part == a)
            def _():
                g = s_ref[0].astype(F32)
                for s in range(1, NDEV):
                    g = g + s_ref[s].astype(F32)
                g_ref[...] = g

        g = g_ref[...]
        mn = ADAM_B1 * m_ref[...] + (1.0 - ADAM_B1) * g
        vn = ADAM_B2 * v_ref[...] + (1.0 - ADAM_B2) * (g * g)
        nm_ref[...] = mn
        nv_ref[...] = vn
        d_ref[...] = -ADAM_LR * ((mn / c1) / (jnp.sqrt(vn / c2) + ADAM_EPS) + ADAM_WD * w_ref[...])

    row = pl.BlockSpec((tr, C), lambda i: (i, 0))
    s_specs = [pl.BlockSpec((NDEV, tr, C), lambda i, a=a: (0, jnp.clip(i - a * per, 0, per - 1), 0))
               for a in range(parts)]
    return pl.pallas_call(
        body, name=name, grid=(R // tr,),
        in_specs=[row, row, row] + s_specs,
        out_specs=[row] * 4, out_shape=[jax.ShapeDtypeStruct((R, C), F32)] * 4,
        compiler_params=_cp("arbitrary"),
    )(w, m, v, *slots)


class _Layout:
    def __init__(self, D):
        self.GW = GW = D // 4
        self.H = H = GW // HEAD
        self.QL, self.KVL = 0, Q_LORA
        base = Q_LORA + KV_LORA
        (self.QB, self.KB, self.VB, self.QC, self.KC, self.VC, self.QD, self.KD, self.VD) = (
            base + k * GW for k in range(9))
        self.KR = base + 9 * GW
        self.FC = self.KR + LANE
        self.PW = -(-(self.FC + LANE) // 512) * 512
        self.o_kr = base
        self.o_bc = base + QK_ROPE
        self.o_fc = self.o_bc + 6 * GW
        self.o_d = self.o_fc + H
        self.IN = self.o_d + 3 * GW

    def pad(self, w):
        z = lambda n: jnp.zeros(w.shape[:-1] + (n,), w.dtype)
        return jnp.concatenate([
            w[..., :self.o_kr], w[..., self.o_bc:self.o_fc], w[..., self.o_d:self.IN],
            w[..., self.o_kr:self.o_bc], z(LANE - QK_ROPE), w[..., self.o_fc:self.o_d], z(LANE - self.H),
            z(self.PW - self.FC - LANE)], axis=-1)

    def unpad(self, g):
        return jnp.concatenate([
            g[..., :self.KR - 9 * self.GW], g[..., self.KR:self.KR + QK_ROPE], g[..., self.QB:self.QD],
            g[..., self.FC:self.FC + self.H], g[..., self.QD:self.KR]], axis=-1)


def _rope_tables(S):
    pos = jnp.arange(S, dtype=F32)

    def cs(dim):
        inv = ROPE_THETA ** (-jnp.arange(0, dim, 2, dtype=F32) / dim)
        ang = pos[:, None] * inv[None, :]
        return jnp.cos(ang), jnp.sin(ang)

    c, s = cs(HEAD)
    full = (jnp.concatenate([c, c], 1), jnp.concatenate([-s, s], 1))
    c, s = cs(QK_ROPE)
    z = jnp.zeros((S, LANE - QK_ROPE), F32)
    half = (jnp.concatenate([c, c, z], 1), jnp.concatenate([-s, s, z], 1))
    return full, half


def _dilated_table(tq, tk):
    win = max(w for w, _ in DILATED_PAIRS)
    nd = (win + tk) // tq + 1
    d = np.arange(nd)[:, None, None] * tq + np.arange(tq)[None, :, None] - np.arange(tk)[None, None, :]
    mult = np.zeros(d.shape, np.float64)
    for w, dil in DILATED_PAIRS:
        mult += (d >= 0) & (d <= w) & (d % dil == 0)
    return jnp.asarray(np.where(mult > 0, np.log(np.maximum(mult, 1.0)), NEG), F32), win


def _pack(arrs):
    rows = []
    for a in arrs:
        f = a.reshape(-1).astype(F32)
        f = jnp.pad(f, (0, (-f.shape[0]) % LANE))
        rows.append(f.reshape(-1, LANE))
    p = jnp.concatenate(rows, 0)
    return jnp.pad(p, ((0, (-p.shape[0]) % 8), (0, 0)))


def _unpack(p, shapes):
    out, r = [], 0
    for shp in shapes:
        n = int(np.prod(shp))
        nr = -(-n // LANE)
        out.append(p[r:r + nr].reshape(-1)[:n].reshape(shp))
        r += nr
    return out


def kernel(x, attn_norm, w_in, mla_q_norm, w_uq, mla_kv_norm, w_ukv, fox_forget_bias, group_norm, w_out, ffn_norm, w_gate, w_up, w_down, final_norm, loss_target, m_attn_norm, m_w_in, m_mla_q_norm, m_w_uq, m_mla_kv_norm, m_w_ukv, m_fox_forget_bias, m_group_norm, m_w_out, m_ffn_norm, m_w_gate, m_w_up, m_w_down, m_final_norm, v_attn_norm, v_w_in, v_mla_q_norm, v_w_uq, v_mla_kv_norm, v_w_ukv, v_fox_forget_bias, v_group_norm, v_w_out, v_ffn_norm, v_w_gate, v_w_up, v_w_down, v_final_norm):
    _, S, D = x.shape
    L = attn_norm.shape[0]
    lay = _Layout(D)
    H, GW, PW = lay.H, lay.GW, lay.PW
    FB = w_gate.shape[2]
    QKA = HEAD + QK_ROPE
    x = x[0]
    target = loss_target[0]
    rope_full, rope_half = _rope_tables(S)
    neg = lambda t: (t[0], -t[1])
    tab, win = _dilated_table(*_soft_tiles(S))
    cb = lambda col: col // LANE

    sh = dict(w_in=lay.pad(w_in).astype(BF16),
              **{n: w.astype(BF16) for n, w in (("w_uq", w_uq), ("w_ukv", w_ukv), ("w_out", w_out),
                                                  ("w_gate", w_gate), ("w_up", w_up), ("w_down", w_down))})
    first3 = ["w_in", "w_uq", "w_ukv"]

    def first_weights(g):
        wuq = jnp.transpose(g[1], (1, 0, 2)).reshape(Q_LORA, H, QKA)
        wuq = jnp.pad(wuq, ((0, 0), (0, 0), (0, 2 * LANE - QKA))).reshape(Q_LORA, H * 2 * LANE)
        return dict(win=g[0].reshape(D, PW), wuq=wuq,
                    wukv=jnp.transpose(g[2], (1, 0, 2)).reshape(KV_LORA, H * 2 * LANE))

    def row(a):
        return a.reshape(1, -1)

    def forward(l, x0, W):
        A = dict(x0=x0)
        A["bias"] = jnp.pad(row(fox_forget_bias[l]), ((0, 0), (0, LANE - H)))
        h1 = A["h1"] = _rms_fwd(x0, row(attn_norm[l]), D, 0, BF16, "attn_norm")
        (proj, pb), (W["wg"],) = _mm(h1, W["win"], "in_proj", out_dtype=(F32, BF16),
                                     comm=("gather", [sh["w_gate"][l]]))
        A["proj"], A["pb"] = proj, pb
        qln = A["qln"] = _rms_fwd(proj, row(mla_q_norm[l]), Q_LORA, cb(lay.QL) // 4, BF16, "q_norm")
        kvln = A["kvln"] = _rms_fwd(proj, row(mla_kv_norm[l]), KV_LORA, cb(lay.KVL) // 4, BF16, "kv_norm")
        qa, qab = _mm(qln, W["wuq"], "q_up", out_dtype=(F32, BF16))
        A["qab"] = qab
        kv = A["kv"] = _mm(kvln, W["wukv"], "kv_up", out_dtype=BF16)
        q_pe = A["q_pe"] = _rope(qa, (1, H, 2), *rope_half, True, BF16, "rope_q_mla")
        k_pe = A["k_pe"] = _rope(proj, (cb(lay.KR), 1, 1), *rope_half, True, BF16, "rope_k_mla")
        (A["o_a"], A["lse_a"]), (W["wu"],) = _attn_fwd(
            S, H, qab, lambda h: 2 * h, kv, lambda h: 2 * h, kv, lambda h: 2 * h + 1, QKA ** -0.5, "mla_fwd",
            q2=q_pe, q2cb=lambda h: h, k2=k_pe, k2cb=lambda h: 0, comm=("gather", [sh["w_up"][l]]))
        qk_b = A["qk_b"] = _rope(proj, (cb(lay.QB), 2 * H, 1), *rope_full, False, BF16, "rope_qk_dil")
        (A["o_b"], A["lse_b"]), (g_down,) = _attn_fwd(
            S, H, qk_b, lambda h: h, qk_b, lambda h: H + h, pb, lambda h: cb(lay.VB) + h, HEAD ** -0.5,
            "dilated_fwd", tab=tab, win=win, comm=("gather", [sh["w_down"][l]]))
        W["wd"] = g_down.reshape(NDEV * FB, D)
        ccol, crow = A["ccol"], A["crow"] = _fox_prep(S, H, proj, cb(lay.FC), A["bias"], "fox_prep")
        (A["o_c"], A["lse_c"]), (g_out,) = _attn_fwd(
            S, H, pb, lambda h: cb(lay.QC) + h, pb, lambda h: cb(lay.KC) + h, pb, lambda h: cb(lay.VC) + h,
            HEAD ** -0.5, "fox_fwd", ccol=ccol, crow=crow, comm=("gather", [sh["w_out"][l]]))
        W["wout"] = g_out.reshape(4 * GW, D)
        A["o_d"], A["tot_d"] = _stick_fwd(
            S, H, pb, lambda h: cb(lay.QD) + h, lambda h: cb(lay.KD) + h, lambda h: cb(lay.VD) + h,
            HEAD ** -0.5, "stick_fwd")
        mix = A["mix"] = _gn_fwd([A["o_a"], A["o_b"], A["o_c"], A["o_d"]], row(group_norm[l]), "group_norm")
        x1 = A["x1"] = _mm(mix, W["wout"], "out_proj", res=x0)
        h2 = A["h2"] = _rms_fwd(x1, row(ffn_norm[l]), D, 0, BF16, "ffn_norm")
        nxt = None
        if l + 1 < L:
            (A["g"], A["u"], A["act"]), nxt = _ffn_up(
                h2, W["wg"], W["wu"], "ffn_up", comm=("gather", [sh[n][l + 1] for n in first3]))
        else:
            A["g"], A["u"], A["act"] = _ffn_up(h2, W["wg"], W["wu"], "ffn_up_last")
        return _mm_down(A["act"], W["wd"], x1, "ffn_down"), A, nxt

    def backward(l, dx2, dx2b, W, A, late):
        proj, pb = A["proj"], A["pb"]
        G, small, got = {}, {}, {}
        dgate, dup = _ffn_dact(dx2b, W["wd"], A["g"], A["u"], "ffn_dact")
        G["w_down"] = _mm_dwdown(A["act"], dx2b, "dw_down").reshape(NDEV, FB, D)
        dh2, (got[l, "w_down"],) = _mm_dh2(dgate, W["wg"], dup, W["wu"], "ffn_dh", comm=("exchange", [G["w_down"]]))
        G["w_gate"] = _mm_dwgate(A["h2"], dgate, "dw_gate")
        G["w_up"] = _mm_dwgate(A["h2"], dup, "dw_up")
        dx1, dx1b, small["ffn_norm"] = _rms_bwd(A["x1"], row(ffn_norm[l]), dh2, D, 0, "ffn_norm_bwd", res=dx2)
        dmix = _mm(dx1b, W["wout"], "out_proj_dx", tb=True)
        G["w_out"] = _mm(A["mix"], dx1b, "dw_out", ta=True, out_dtype=BF16).reshape(NDEV, 4 * GW // NDEV, D)
        do_a, do_b, do_c, do_d, small["group_norm"] = _gn_bwd(
            [A["o_a"], A["o_b"], A["o_c"], A["o_d"]], row(group_norm[l]), dmix, "group_norm_bwd")
        (dq_d, dk_d, dv_d), (got[l, "w_up"],) = _stick_bwd(
            S, H, pb, lambda h: cb(lay.QD) + h, lambda h: cb(lay.KD) + h, lambda h: cb(lay.VD) + h,
            do_d, A["tot_d"], HEAD ** -0.5, "stick_bwd", comm=("exchange", [G["w_up"]]))
        (dq_c, dk_c, dv_c, dcc), moved = _attn_bwd(
            S, H, pb, lambda h: cb(lay.QC) + h, pb, lambda h: cb(lay.KC) + h, pb, lambda h: cb(lay.VC) + h,
            A["o_c"], do_c, A["lse_c"], HEAD ** -0.5, "fox_bwd" if late else "fox_bwd_top",
            ccol=A["ccol"], crow=A["crow"], comm=("exchange", [G["w_out"]] + (late or [])))
        got[l, "w_out"] = moved[0]
        got.update({(l + 1, n): s for n, s in zip(first3, moved[1:])})
        dfc, dbias = _fox_bwd(S, H, proj, cb(lay.FC), A["bias"], dcc, "fox_gate_bwd")
        small["fox_forget_bias"] = dbias[0, :H]
        qk_b = A["qk_b"]
        dq_b, dk_b, dv_b = _attn_bwd(
            S, H, qk_b, lambda h: h, qk_b, lambda h: H + h, pb, lambda h: cb(lay.VB) + h,
            A["o_b"], do_b, A["lse_b"], HEAD ** -0.5, "dilated_bwd", tab=tab, win=win)
        dqk_b = _rope(jnp.concatenate([dq_b, dk_b], 1), (0, 2 * H, 1), *neg(rope_full), False, BF16, "rope_qk_dil_bwd")
        qab, kv = A["qab"], A["kv"]
        (dq1, dk1, dv_a, dq2, dk2), (got[l, "w_gate"],) = _attn_bwd(
            S, H, qab, lambda h: 2 * h, kv, lambda h: 2 * h, kv, lambda h: 2 * h + 1,
            A["o_a"], do_a, A["lse_a"], QKA ** -0.5, "mla_bwd",
            q2=A["q_pe"], q2cb=lambda h: h, k2=A["k_pe"], k2cb=lambda h: 0, comm=("exchange", [G["w_gate"]]))
        dq2 = _rope(dq2, (0, H, 1), *neg(rope_half), True, BF16, "rope_q_mla_bwd")
        dk_pe = _rope(dk2.astype(F32).reshape(S, H, LANE).sum(1), (0, 1, 1), *neg(rope_half), True, BF16,
                      "rope_k_mla_bwd")
        dqa = jnp.stack([dq1.reshape(S, H, LANE), dq2.reshape(S, H, LANE)], 2).reshape(S, H * 2 * LANE)
        dkv = jnp.stack([dk1.reshape(S, H, LANE), dv_a.reshape(S, H, LANE)], 2).reshape(S, H * 2 * LANE)
        dwuq = _mm(A["qln"], dqa, "dw_uq", ta=True, out_dtype=BF16)
        dwuq = dwuq.reshape(Q_LORA, H, 2 * LANE)[:, :, :QKA].reshape(Q_LORA, NDEV, H * QKA // NDEV)
        G["w_uq"] = jnp.transpose(dwuq, (1, 0, 2))
        dwukv = _mm(A["kvln"], dkv, "dw_ukv", ta=True, out_dtype=BF16).reshape(KV_LORA, NDEV, H * 2 * LANE // NDEV)
        G["w_ukv"] = jnp.transpose(dwukv, (1, 0, 2))
        dqln = _mm(dqa, W["wuq"], "q_up_dx", tb=True)
        dkvln = _mm(dkv, W["wukv"], "kv_up_dx", tb=True)
        _, dql, small["mla_q_norm"] = _rms_bwd(proj, row(mla_q_norm[l]), dqln, Q_LORA, cb(lay.QL) // 4, "q_norm_bwd")
        _, dkvl, small["mla_kv_norm"] = _rms_bwd(
            proj, row(mla_kv_norm[l]), dkvln, KV_LORA, cb(lay.KVL) // 4, "kv_norm_bwd")
        dproj = jnp.concatenate([
            dql, dkvl, dqk_b, dv_b, dq_c, dk_c, dv_c, dq_d, dk_d, dv_d,
            dk_pe, dfc.astype(BF16), jnp.zeros((S, PW - lay.FC - LANE), BF16)], axis=1)
        G["w_in"] = _mm(A["h1"], dproj, "dw_in", ta=True, out_dtype=BF16).reshape(NDEV, D // NDEV, PW)
        late = [G[n] for n in first3]
        if l == 0:
            dh1, moved = _mm(dproj, W["win"], "in_proj_dx_last", tb=True, comm=("exchange", late))
            got.update({(0, n): s for n, s in zip(first3, moved)})
        else:
            dh1 = _mm(dproj, W["win"], "in_proj_dx", tb=True)
        dx0, dx0b, small["attn_norm"] = _rms_bwd(A["x0"], row(attn_norm[l]), dh1, D, 0, "attn_norm_bwd", res=dx1)
        return dx0, dx0b, late, got, small

    big = first3 + ["w_out", "w_gate", "w_up", "w_down"]
    Ws, As = [], []
    xc = x
    nxt = _comm_alone("gather", [sh[n][0] for n in first3], "gather_first")
    for l in range(L):
        W = first_weights(nxt)
        xc, A, nxt = forward(l, xc, W)
        Ws.append(W)
        As.append(A)
    dx, loss_part = _final_loss(xc, row(final_norm), target, "final_loss")
    dx, dxb, dfinal = _rms_bwd(xc, row(final_norm), dx, D, 0, "final_norm_bwd")
    slots = {}
    smalls = [None] * L
    late = None
    for l in reversed(range(L)):
        dx, dxb, late, got, smalls[l] = backward(l, dx, dxb, Ws[l], As[l], late)
        slots.update(got)

    names_small = ["attn_norm", "mla_q_norm", "mla_kv_norm", "fox_forget_bias", "group_norm", "ffn_norm"]
    params = dict(attn_norm=attn_norm, mla_q_norm=mla_q_norm, mla_kv_norm=mla_kv_norm, fox_forget_bias=fox_forget_bias,
                  group_norm=group_norm, ffn_norm=ffn_norm, final_norm=final_norm, w_in=w_in, w_uq=w_uq, w_ukv=w_ukv,
                  w_out=w_out, w_gate=w_gate, w_up=w_up, w_down=w_down)
    moms = dict(attn_norm=(m_attn_norm, v_attn_norm), mla_q_norm=(m_mla_q_norm, v_mla_q_norm),
                mla_kv_norm=(m_mla_kv_norm, v_mla_kv_norm), fox_forget_bias=(m_fox_forget_bias, v_fox_forget_bias),
                group_norm=(m_group_norm, v_group_norm), ffn_norm=(m_ffn_norm, v_ffn_norm),
                final_norm=(m_final_norm, v_final_norm), w_in=(m_w_in, v_w_in), w_uq=(m_w_uq, v_w_uq),
                w_ukv=(m_w_ukv, v_w_ukv), w_out=(m_w_out, v_w_out), w_gate=(m_w_gate, v_w_gate),
                w_up=(m_w_up, v_w_up), w_down=(m_w_down, v_w_down))
    small_list = names_small + ["final_norm"]
    small_grads = [jnp.stack([smalls[l][n].reshape(params[n].shape[1:]) for l in range(L)]) for n in names_small]
    small_grads.append(dfinal.reshape(final_norm.shape))
    shapes = [params[n].shape for n in small_list] + [(LANE,)]
    packed_g = _comm_alone("gather", [_pack(small_grads + [loss_part.reshape(LANE)])], "gather_small")[0]
    zero = jnp.zeros((LANE,), F32)
    res_small = _adamw(_pack([params[n] for n in small_list] + [zero]), [packed_g],
                       _pack([moms[n][0] for n in small_list] + [zero]),
                       _pack([moms[n][1] for n in small_list] + [zero]), "adamw_small")
    unp = [_unpack(r, shapes) for r in res_small]
    out = {n: tuple(unp[k][i] for k in range(4)) for i, n in enumerate(small_list)}
    loss = unp[0][-1][0]

    for i, n in enumerate(big):
        st = [lay.unpad(slots[l, n]) if n == "w_in" else slots[l, n] for l in range(L)]
        C = st[0].shape[-1]
        st = [s.reshape(NDEV, -1, C) for s in st]
        w2 = params[n].reshape(-1, C)
        res = _adamw(w2, st, moms[n][0].reshape(-1, C), moms[n][1].reshape(-1, C), "adamw_" + n)
        out[n] = tuple(r.reshape(params[n].shape) for r in res)

    order = ["attn_norm", "w_in", "mla_q_norm", "w_uq", "mla_kv_norm", "w_ukv", "fox_forget_bias", "group_norm",
             "w_out", "ffn_norm", "w_gate", "w_up", "w_down", "final_norm"]
    return (loss, dx[None], *[out[n][0] for n in order], *[out[n][1] for n in order],
            *[out[n][2] for n in order], *[out[n][3] for n in order])
```

```python
import functools
import math

import numpy as np
import jax
import jax.numpy as jnp
from jax import lax
from jax.experimental import pallas as pl
from jax.experimental.pallas import tpu as pltpu

F32 = jnp.float32
BF16 = jnp.bfloat16
NDEV = 8
LANE = 128
HEAD = 128
Q_LORA = 512
KV_LORA = 512
QK_ROPE = 64
DILATED_PAIRS = ((128, 1), (512, 4), (2048, 16))
ROPE_THETA = 10000.0
EPS = 1e-6
NEG = -1e30
TQ = 256
TK = 128
TKS = 512
VMEM_LIMIT = 48 * 1024 * 1024
MM_OPERAND_BYTES = 20 * 1024 * 1024
FFN_BLOCKS_PER_STEP = 2
ADAMW_BLOCK_BYTES = 24 * 1024 * 1024
ADAM_LR, ADAM_B1, ADAM_B2, ADAM_EPS, ADAM_WD, ADAM_STEP = 0.001, 0.9, 0.999, 1e-08, 0.01, 10
MESH = pl.DeviceIdType.MESH
ANY = pl.BlockSpec(memory_space=pl.ANY)


def _cp(*sem):
    return pltpu.CompilerParams(dimension_semantics=sem, vmem_limit_bytes=VMEM_LIMIT)


def _dot(a, b, ca, cb):
    return lax.dot_general(a, b, (((ca,), (cb,)), ((), ())), preferred_element_type=F32)


def _dot_nn(a, b):
    return _dot(a, b, 1, 0)


def _dot_nt(a, b):
    return _dot(a, b, 1, 1)


def _dot_tn(a, b):
    return _dot(a, b, 0, 0)


def _tile(n, t):
    if n <= t:
        return n
    t -= t % LANE
    while n % t:
        t -= LANE
    return t


def _direct_copies(ins, outs, send_sems, recv_sems, local_sems, want_recvs=True):
    x, y, c = lax.axis_index("x"), lax.axis_index("y"), lax.axis_index("c")
    my_id = 4 * x + 2 * y + c
    local, sends, recvs = [], [], []
    for a in range(len(ins)):
        mine = ins[a].at[my_id]
        local.append(pltpu.make_async_copy(mine, outs[a].at[my_id], local_sems.at[a]))
        for k in range(1, NDEV):
            peer = (1 - x if k & 4 else x, 1 - y if k & 2 else y, 1 - c if k & 1 else c)
            pid = 4 * peer[0] + 2 * peer[1] + peer[2]
            sems = dict(send_sem=send_sems.at[a, k - 1], recv_sem=recv_sems.at[a, k - 1],
                        device_id=peer, device_id_type=MESH)
            sends.append(pltpu.make_async_remote_copy(src_ref=ins[a].at[pid], dst_ref=outs[a].at[my_id], **sems))
            if want_recvs:
                recvs.append(pltpu.make_async_remote_copy(src_ref=mine, dst_ref=outs[a].at[pid], **sems))
    return local, sends, recvs


def _comm_start(kind, ins, outs, send_sems, recv_sems, local_sems):
    if kind == "exchange":
        local, sends, _ = _direct_copies(ins, outs, send_sems, recv_sems, local_sems, want_recvs=False)
        for cp in local + sends:
            cp.start()
        return
    x, y, c = lax.axis_index("x"), lax.axis_index("y"), lax.axis_index("c")
    for a in range(len(ins)):
        mine = outs[a].at[4 * x + 2 * y + c]
        pltpu.make_async_copy(ins[a], mine, local_sems.at[a]).start()
        for k, to in enumerate([(x, y, 1 - c), (1 - x, y, c), (x, 1 - y, c), (1 - x, 1 - y, c)]):
            pltpu.make_async_remote_copy(src_ref=ins[a], dst_ref=mine, send_sem=send_sems.at[a, k],
                                         recv_sem=recv_sems.at[a, k], device_id=to, device_id_type=MESH).start()


def _comm_finish(kind, ins, outs, send_sems, recv_sems, local_sems):
    if kind == "exchange":
        local, sends, recvs = _direct_copies(ins, outs, send_sems, recv_sems, local_sems)
        for cp in recvs:
            cp.wait_recv()
        for cp in sends:
            cp.wait_send()
        for cp in local:
            cp.wait()
        return
    x, y, c = lax.axis_index("x"), lax.axis_index("y"), lax.axis_index("c")
    sibling = (x, y, 1 - c)
    chips = [(1 - x, y), (x, 1 - y), (1 - x, 1 - y)]
    for a in range(len(ins)):
        def copy(k, block, to):
            rows = outs[a].at[4 * block[0] + 2 * block[1] + block[2]]
            return pltpu.make_async_remote_copy(src_ref=rows, dst_ref=rows, send_sem=send_sems.at[a, k],
                                                recv_sem=recv_sems.at[a, k], device_id=to, device_id_type=MESH)

        passed = []
        for j, chip in enumerate(chips):
            copy(1 + j, (*chip, c), (x, y, c)).wait_recv()
            passed.append(copy(4 + j, (*chip, c), sibling))
            passed[-1].start()
        copy(0, sibling, (x, y, c)).wait_recv()
        for j, chip in enumerate(chips):
            copy(4 + j, (*chip, 1 - c), (x, y, c)).wait_recv()
        for k in range(4):
            copy(k, (x, y, c), sibling).wait_send()
        for cp in passed:
            cp.wait_send()
        pltpu.make_async_copy(ins[a], outs[a].at[4 * x + 2 * y + c], local_sems.at[a]).wait()


def _comm_shapes(kind, arrs):
    out_shape = [jax.ShapeDtypeStruct(((NDEV,) if kind == "gather" else ()) + a.shape, a.dtype) for a in arrs]
    n = len(arrs)
    sems = [pltpu.SemaphoreType.DMA((n, 7)), pltpu.SemaphoreType.DMA((n, 7)), pltpu.SemaphoreType.DMA((n,))]
    return out_shape, sems


def _comm_alone(kind, arrs, name):
    n = len(arrs)

    def body(*refs):
        _comm_start(kind, refs[:n], refs[n:2 * n], *refs[2 * n:])
        _comm_finish(kind, refs[:n], refs[n:2 * n], *refs[2 * n:])

    out_shape, sems = _comm_shapes(kind, arrs)
    return pl.pallas_call(body, name=name, out_shape=out_shape, in_specs=[ANY] * n, out_specs=[ANY] * n,
                          scratch_shapes=sems)(*arrs)


def _pcall(body, *, name, grid, in_specs, out_specs, out_shape, operands, sem, scratch_shapes=(), comm=None):
    in_specs, out_specs, out_shape = list(in_specs), list(out_specs), list(out_shape)
    scratch_shapes = list(scratch_shapes)
    if comm is None:
        res = pl.pallas_call(body, name=name, grid=grid, in_specs=in_specs, out_specs=out_specs, out_shape=out_shape,
                             scratch_shapes=scratch_shapes, compiler_params=_cp(*sem))(*operands)
        return list(res), []
    kind, arrs = comm
    nc, n_in, n_out, n_scr = len(arrs), len(operands), len(out_shape), len(scratch_shapes)
    c_shape, c_sems = _comm_shapes(kind, arrs)

    def carrier(*refs):
        ins, cin = refs[:n_in], refs[n_in:n_in + nc]
        outs = refs[n_in + nc:n_in + nc + n_out]
        cout = refs[n_in + nc + n_out:n_in + 2 * nc + n_out]
        scr = refs[n_in + 2 * nc + n_out:n_in + 2 * nc + n_out + n_scr]
        sems = refs[n_in + 2 * nc + n_out + n_scr:]
        pids = [pl.program_id(d) for d in range(len(grid))]
        first = functools.reduce(jnp.logical_and, [p == 0 for p in pids])
        last = functools.reduce(jnp.logical_and, [p == g - 1 for p, g in zip(pids, grid)])

        @pl.when(first)
        def _():
            _comm_start(kind, cin, cout, *sems)

        body(*ins, *outs, *scr)

        @pl.when(last)
        def _():
            _comm_finish(kind, cin, cout, *sems)

    res = pl.pallas_call(
        carrier, name=name, grid=grid, in_specs=in_specs + [ANY] * nc, out_specs=out_specs + [ANY] * nc,
        out_shape=out_shape + c_shape, scratch_shapes=scratch_shapes + c_sems,
        compiler_params=_cp(*["arbitrary"] * len(grid)))(*operands, *arrs)
    return list(res[:n_out]), list(res[n_out:])


def _mm_call(pairs, grid, a_spec, b_spec, o_spec, out_shape, acc_shape, nk, ca, cb, name,
             res=None, res_spec=None, comm=None):
    npairs = len(pairs)
    multi = isinstance(out_shape, (list, tuple))
    nout = len(out_shape) if multi else 1

    def body(*refs):
        ab = refs[:2 * npairs]
        r_ref = refs[2 * npairs] if res is not None else None
        o_refs, acc = refs[-1 - nout:-1], refs[-1]
        k = pl.program_id(2)

        @pl.when(k == 0)
        def _():
            acc[...] = jnp.zeros_like(acc)

        tot = None
        for p in range(npairs):
            av, bv = ab[2 * p][...].astype(BF16), ab[2 * p + 1][...].astype(BF16)
            if av.ndim == 2:
                terms = [(av, bv)]
            else:
                rows = bv.shape[0] // av.shape[0]
                terms = [(av[q], bv[q] if bv.ndim == 3 else bv[q * rows:(q + 1) * rows]) for q in range(av.shape[0])]
            for at, bt in terms:
                d = _dot(at, bt, ca, cb)
                tot = d if tot is None else tot + d
        acc[...] += tot

        @pl.when(k == nk - 1)
        def _():
            r = acc[...]
            if r_ref is not None:
                r = r + r_ref[...]
            for o_ref in o_refs:
                o_ref[...] = r.astype(o_ref.dtype)

    ops, specs = [], []
    for a, b in pairs:
        ops += [a, b]
        specs += [a_spec, b_spec]
    if res is not None:
        ops.append(res)
        specs.append(res_spec)
    outs, moved = _pcall(
        body, name=name, grid=grid, in_specs=specs, out_specs=[o_spec] * nout,
        out_shape=out_shape if multi else [out_shape], operands=ops,
        scratch_shapes=[pltpu.VMEM(acc_shape, F32)], sem=("parallel", "parallel", "arbitrary"), comm=comm)
    outs = outs if multi else outs[0]
    return outs if comm is None else (outs, moved)


def _k_tile(K, row_bytes, tk=2048):
    tk = _tile(K, tk)
    while 2 * tk * row_bytes > MM_OPERAND_BYTES and tk % 256 == 0:
        tk //= 2
    return tk


def _mm(a, b, name, ta=False, tb=False, out_dtype=F32, res=None, tm=1024, tn=1024, comm=None):
    M, K = (a.shape[1], a.shape[0]) if ta else a.shape
    N = b.shape[0] if tb else b.shape[1]
    tm, tn = _tile(M, tm), _tile(N, tn)
    tk = _k_tile(K, tm * a.dtype.itemsize + tn * b.dtype.itemsize)
    a_spec = pl.BlockSpec((tk, tm), lambda i, j, k: (k, i)) if ta else pl.BlockSpec((tm, tk), lambda i, j, k: (i, k))
    b_spec = pl.BlockSpec((tn, tk), lambda i, j, k: (j, k)) if tb else pl.BlockSpec((tk, tn), lambda i, j, k: (k, j))
    o_spec = pl.BlockSpec((tm, tn), lambda i, j, k: (i, j))
    if isinstance(out_dtype, tuple):
        out_shape = [jax.ShapeDtypeStruct((M, N), d) for d in out_dtype]
    else:
        out_shape = jax.ShapeDtypeStruct((M, N), out_dtype)
    return _mm_call([(a, b)], (M // tm, N // tn, K // tk), a_spec, b_spec, o_spec,
                    out_shape, (tm, tn), K // tk,
                    0 if ta else 1, 1 if tb else 0, name, res=res, res_spec=o_spec, comm=comm)


def _mm_down(act, wd, res, name, tm=1024, tn=1024):
    _, S, FB = act.shape
    D = wd.shape[1]
    tm, tn = _tile(S, tm), _tile(D, tn)
    o_spec = pl.BlockSpec((tm, tn), lambda i, j, k: (i, j))
    nb = FFN_BLOCKS_PER_STEP
    return _mm_call([(act, wd)], (S // tm, D // tn, NDEV // nb),
                    pl.BlockSpec((nb, tm, FB), lambda i, j, k: (k, i, 0)),
                    pl.BlockSpec((nb * FB, tn), lambda i, j, k: (k, j)), o_spec,
                    jax.ShapeDtypeStruct((S, D), F32), (tm, tn), NDEV // nb, 1, 0, name, res=res, res_spec=o_spec)


def _mm_dwdown(act, dy, name, tn=1024):
    _, S, FB = act.shape
    D = dy.shape[1]
    tn = _tile(D, tn)
    tk = _k_tile(S, FB * act.dtype.itemsize + tn * dy.dtype.itemsize)
    return _mm_call([(act, dy)], (NDEV, D // tn, S // tk),
                    pl.BlockSpec((None, tk, FB), lambda i, j, k: (i, k, 0)),
                    pl.BlockSpec((tk, tn), lambda i, j, k: (k, j)),
                    pl.BlockSpec((FB, tn), lambda i, j, k: (i, j)),
                    jax.ShapeDtypeStruct((NDEV * FB, D), BF16), (FB, tn), S // tk, 0, 0, name)


def _mm_dh2(dg, wg, du, wu, name, tm=1024, tn=1024, comm=None):
    _, S, FB = dg.shape
    D = wg.shape[1]
    tm, tn = _tile(S, tm), _tile(D, tn)
    nb = FFN_BLOCKS_PER_STEP
    return _mm_call([(dg, wg), (du, wu)], (S // tm, D // tn, NDEV // nb),
                    pl.BlockSpec((nb, tm, FB), lambda i, j, k: (k, i, 0)),
                    pl.BlockSpec((nb, tn, FB), lambda i, j, k: (k, j, 0)),
                    pl.BlockSpec((tm, tn), lambda i, j, k: (i, j)),
                    jax.ShapeDtypeStruct((S, D), F32), (tm, tn), NDEV // nb, 1, 1, name, comm=comm)


def _mm_dwgate(h2, dg, name, tm=1024, comm=None):
    _, S, FB = dg.shape
    D = h2.shape[1]
    tm = _tile(D, tm)
    tk = _k_tile(S, tm * h2.dtype.itemsize + FB * dg.dtype.itemsize)
    return _mm_call([(h2, dg)], (NDEV, D // tm, S // tk),
                    pl.BlockSpec((tk, tm), lambda p, i, k: (k, i)),
                    pl.BlockSpec((None, tk, FB), lambda p, i, k: (p, k, 0)),
                    pl.BlockSpec((None, tm, FB), lambda p, i, k: (p, i, 0)),
                    jax.ShapeDtypeStruct((NDEV, D, FB), BF16), (tm, FB), S // tk, 0, 0, name, comm=comm)


def _ffn_up(h2, wg, wu, name, tm=512, comm=None):
    S, D = h2.shape
    FB = wg.shape[2]
    tm = _tile(S, tm)

    def body(h_ref, wg_ref, wu_ref, g_ref, u_ref, act_ref):
        h = h_ref[...]
        g = _dot_nn(h, wg_ref[...])
        u = _dot_nn(h, wu_ref[...])
        g_ref[...] = g.astype(BF16)
        u_ref[...] = u.astype(BF16)
        act_ref[...] = (g / (1.0 + jnp.exp(-g)) * u).astype(BF16)

    w_spec = pl.BlockSpec((None, D, FB), lambda p, i: (p, 0, 0))
    o_spec = pl.BlockSpec((None, tm, FB), lambda p, i: (p, i, 0))
    shp = (NDEV, S, FB)
    outs, moved = _pcall(
        body, name=name, grid=(NDEV, S // tm),
        in_specs=[pl.BlockSpec((tm, D), lambda p, i: (i, 0)), w_spec, w_spec],
        out_specs=[o_spec, o_spec, o_spec],
        out_shape=[jax.ShapeDtypeStruct(shp, BF16)] * 3,
        operands=[h2, wg, wu], sem=("parallel", "parallel"), comm=comm)
    return outs if comm is None else (outs, moved)


def _ffn_dact(dy, wd, g, u, name, tm=512):
    S, D = dy.shape
    FB = g.shape[2]
    tm = _tile(S, tm)

    def body(dy_ref, wd_ref, g_ref, u_ref, dg_ref, du_ref):
        dact = _dot_nt(dy_ref[...], wd_ref[...])
        gv = g_ref[...].astype(F32)
        sg = 1.0 / (1.0 + jnp.exp(-gv))
        dg_ref[...] = (dact * u_ref[...].astype(F32) * (sg * (1.0 + gv * (1.0 - sg)))).astype(BF16)
        du_ref[...] = (dact * (gv * sg)).astype(BF16)

    t_spec = pl.BlockSpec((None, tm, FB), lambda p, i: (p, i, 0))
    shp = jax.ShapeDtypeStruct((NDEV, S, FB), BF16)
    return pl.pallas_call(
        body, name=name, grid=(NDEV, S // tm),
        in_specs=[pl.BlockSpec((tm, D), lambda p, i: (i, 0)), pl.BlockSpec((FB, D), lambda p, i: (p, 0)),
                  t_spec, t_spec],
        out_specs=[t_spec, t_spec], out_shape=[shp, shp],
        compiler_params=_cp("parallel", "parallel"),
    )(dy, wd, g, u)


def _rms_fwd(x, gain, width, cb, out_dtype, name, ts=512):
    S = x.shape[0]
    ts = _tile(S, ts)

    def body(x_ref, g_ref, o_ref):
        xv = x_ref[...]
        r = lax.rsqrt(jnp.mean(xv * xv, axis=1, keepdims=True) + EPS)
        o_ref[...] = (xv * r * g_ref[...]).astype(o_ref.dtype)

    return pl.pallas_call(
        body, name=name, grid=(S // ts,),
        in_specs=[pl.BlockSpec((ts, width), lambda i: (i, cb)), pl.BlockSpec((1, width), lambda i: (0, 0))],
        out_specs=pl.BlockSpec((ts, width), lambda i: (i, 0)),
        out_shape=jax.ShapeDtypeStruct((S, width), out_dtype),
        compiler_params=_cp("parallel"),
    )(x, gain)


def _rms_bwd(x, gain, dy, width, cb, name, res=None, ts=256):
    S = x.shape[0]
    ts = _tile(S, ts)
    has_res = res is not None

    def body(*refs):
        x_ref, g_ref, dy_ref = refs[:3]
        r_ref = refs[3] if has_res else None
        dx_ref, dxb_ref, dg_ref = refs[-3], refs[-2], refs[-1]

        @pl.when(pl.program_id(0) == 0)
        def _():
            dg_ref[...] = jnp.zeros_like(dg_ref)

        xv = x_ref[...]
        r = lax.rsqrt(jnp.mean(xv * xv, axis=1, keepdims=True) + EPS)
        xh = xv * r
        dyv = dy_ref[...]
        dyg = dyv * g_ref[...]
        dx = r * (dyg - xh * jnp.mean(dyg * xh, axis=1, keepdims=True))
        if has_res:
            dx = dx + r_ref[...]
        dx_ref[...] = dx
        dxb_ref[...] = dx.astype(BF16)
        dg_ref[...] += jnp.sum(dyv * xh, axis=0, keepdims=True)

    row = pl.BlockSpec((ts, width), lambda i: (i, 0))
    vec = pl.BlockSpec((1, width), lambda i: (0, 0))
    ops = [x, gain, dy] + ([res] if has_res else [])
    specs = [pl.BlockSpec((ts, width), lambda i: (i, cb)), vec, row] + ([row] if has_res else [])
    return pl.pallas_call(
        body, name=name, grid=(S // ts,), in_specs=specs, out_specs=[row, row, vec],
        out_shape=[jax.ShapeDtypeStruct((S, width), F32), jax.ShapeDtypeStruct((S, width), BF16),
                   jax.ShapeDtypeStruct((1, width), F32)],
        compiler_params=_cp("arbitrary"),
    )(*ops)


def _gn_fwd(outs, gain, name, ts=512):
    S, GW = outs[0].shape
    ts = _tile(S, ts)

    def body(a_ref, b_ref, c_ref, d_ref, g_ref, o_ref):
        for g, r_ref in enumerate((a_ref, b_ref, c_ref, d_ref)):
            xv = r_ref[...]
            r = lax.rsqrt(jnp.mean(xv * xv, axis=1, keepdims=True) + EPS)
            o_ref[:, g * GW:(g + 1) * GW] = (xv * r * g_ref[:, g * GW:(g + 1) * GW]).astype(BF16)

    row = pl.BlockSpec((ts, GW), lambda i: (i, 0))
    return pl.pallas_call(
        body, name=name, grid=(S // ts,),
        in_specs=[row] * 4 + [pl.BlockSpec((1, 4 * GW), lambda i: (0, 0))],
        out_specs=pl.BlockSpec((ts, 4 * GW), lambda i: (i, 0)),
        out_shape=jax.ShapeDtypeStruct((S, 4 * GW), BF16),
        compiler_params=_cp("parallel"),
    )(*outs, gain)


def _gn_bwd(outs, gain, dmix, name, ts=256):
    S, GW = outs[0].shape
    ts = _tile(S, ts)

    def body(a_ref, b_ref, c_ref, d_ref, g_ref, dm_ref, da_ref, db_ref, dc_ref, dd_ref, dg_ref):
        @pl.when(pl.program_id(0) == 0)
        def _():
            dg_ref[...] = jnp.zeros_like(dg_ref)

        for g, (r_ref, o_ref) in enumerate(zip((a_ref, b_ref, c_ref, d_ref), (da_ref, db_ref, dc_ref, dd_ref))):
            sl = slice(g * GW, (g + 1) * GW)
            xv = r_ref[...]
            r = lax.rsqrt(jnp.mean(xv * xv, axis=1, keepdims=True) + EPS)
            xh = xv * r
            dyv = dm_ref[:, sl]
            dyg = dyv * g_ref[:, sl]
            o_ref[...] = (r * (dyg - xh * jnp.mean(dyg * xh, axis=1, keepdims=True))).astype(BF16)
            dg_ref[:, sl] += jnp.sum(dyv * xh, axis=0, keepdims=True)

    row = pl.BlockSpec((ts, GW), lambda i: (i, 0))
    vec = pl.BlockSpec((1, 4 * GW), lambda i: (0, 0))
    return pl.pallas_call(
        body, name=name, grid=(S // ts,),
        in_specs=[row] * 4 + [vec, pl.BlockSpec((ts, 4 * GW), lambda i: (i, 0))],
        out_specs=[row] * 4 + [vec],
        out_shape=[jax.ShapeDtypeStruct((S, GW), BF16)] * 4 + [jax.ShapeDtypeStruct((1, 4 * GW), F32)],
        compiler_params=_cp("arbitrary"),
    )(*outs, gain, dmix)


def _rope_partner(x, half):
    if not half:
        return pltpu.roll(x, 64, 1)
    lane = lax.broadcasted_iota(jnp.int32, x.shape, 1)
    return jnp.where(lane % 64 < 32, pltpu.roll(x, LANE - 32, 1), pltpu.roll(x, 32, 1))


def _rope(x, cbs, cos, sin, half, out_dtype, name, ts=512):
    S = x.shape[0]
    cb0, nb, stride = cbs
    ts = _tile(S, ts)

    def body(x_ref, c_ref, s_ref, o_ref):
        xv = x_ref[...].astype(F32)
        o_ref[...] = (xv * c_ref[...] + _rope_partner(xv, half) * s_ref[...]).astype(o_ref.dtype)

    tab = pl.BlockSpec((ts, LANE), lambda i, j: (i, 0))
    return pl.pallas_call(
        body, name=name, grid=(S // ts, nb),
        in_specs=[pl.BlockSpec((ts, LANE), lambda i, j: (i, cb0 + stride * j)), tab, tab],
        out_specs=pl.BlockSpec((ts, LANE), lambda i, j: (i, j)),
        out_shape=jax.ShapeDtypeStruct((S, nb * LANE), out_dtype),
        compiler_params=_cp("parallel", "parallel"),
    )(x, cos, sin)


def _final_loss(x, gain, target, name, ts=256):
    S, D = x.shape
    ts = _tile(S, ts)

    def body(x_ref, g_ref, t_ref, dy_ref, l_ref):
        @pl.when(pl.program_id(0) == 0)
        def _():
            l_ref[...] = jnp.zeros_like(l_ref)

        xv = x_ref[...]
        r = lax.rsqrt(jnp.mean(xv * xv, axis=1, keepdims=True) + EPS)
        err = xv * r * g_ref[...] - t_ref[...]
        dy_ref[...] = err * (1.0 / D)
        part = jnp.sum(jnp.mean(err * err, axis=1, keepdims=True), axis=0, keepdims=True)
        l_ref[...] += jnp.broadcast_to(0.5 * part, (1, LANE))

    row = pl.BlockSpec((ts, D), lambda i: (i, 0))
    return pl.pallas_call(
        body, name=name, grid=(S // ts,),
        in_specs=[row, pl.BlockSpec((1, D), lambda i: (0, 0)), row],
        out_specs=[row, pl.BlockSpec((1, LANE), lambda i: (0, 0))],
        out_shape=[jax.ShapeDtypeStruct((S, D), F32), jax.ShapeDtypeStruct((1, LANE), F32)],
        compiler_params=_cp("arbitrary"),
    )(x, gain, target)


def _colspec(rows, f):
    return pl.BlockSpec((rows, LANE), f)


def _soft_tiles(S):
    tq = _tile(S, TQ)
    tk = _tile(S, TKS)
    assert tk % tq == 0
    return tq, tk


def _key_row(crow_ref, j, tk):
    n = tk // TK
    return jnp.concatenate([crow_ref[j * n + c] for c in range(n)], axis=1)


def _attn_fwd(S, H, q1, q1cb, k1, k1cb, v, vcb, scale, name, q2=None, q2cb=None, k2=None, k2cb=None,
              tab=None, win=None, ccol=None, crow=None, comm=None):
    tq, tk = _soft_tiles(S)
    has2, hastab, hasc = q2 is not None, tab is not None, ccol is not None

    def body(*refs):
        it = iter(refs)
        q1r, k1r, vr = next(it), next(it), next(it)
        q2r, k2r = (next(it), next(it)) if has2 else (None, None)
        tabr = next(it) if hastab else None
        ccolr, crowr = (next(it), next(it)) if hasc else (None, None)
        o_ref, lse_ref = next(it), next(it)
        i = pl.program_id(1)
        q = q1r[...]
        qb2 = q2r[...] if has2 else None
        cq = ccolr[:, 0:1] if hasc else None
        qpos = i * tq + lax.broadcasted_iota(jnp.int32, (tq, tk), 0)
        kio = lax.broadcasted_iota(jnp.int32, (tq, tk), 1)
        j_diag = (i * tq) // tk
        j_lo = jnp.maximum((i * tq - win) // tk, 0) if win else 0

        if has2:
            q = jnp.concatenate([q, qb2], axis=1)

        def step(j, carry, masked):
            m, l, acc = carry
            off = pl.multiple_of(j * tk, tk)
            kb = k1r[pl.ds(off, tk), :]
            if has2:
                kb = jnp.concatenate([kb, k2r[pl.ds(off, tk), :]], axis=1)
            s = _dot_nt(q, kb) * scale
            if hastab:
                s = s + tabr[i - j * (tk // tq)]
            else:
                if hasc:
                    s = s + (cq - _key_row(crowr, j, tk))
                if masked:
                    s = jnp.where(kio + j * tk <= qpos, s, NEG)
            mn = jnp.maximum(m, jnp.max(s, axis=1, keepdims=True))
            p = jnp.exp(s - mn)
            al = jnp.exp(m - mn)
            l = al * l + jnp.sum(p, axis=1, keepdims=True)
            acc = al * acc + _dot_nn(p.astype(BF16), vr[pl.ds(off, tk), :])
            return mn, l, acc

        carry = (jnp.full((tq, 1), NEG, F32), jnp.zeros((tq, 1), F32), jnp.zeros((tq, LANE), F32))
        if hastab:
            carry = lax.fori_loop(j_lo, j_diag + 1, functools.partial(step, masked=False), carry)
        else:
            carry = lax.fori_loop(j_lo, j_diag, functools.partial(step, masked=False), carry)
            carry = step(j_diag, carry, True)
        m, l, acc = carry
        o_ref[...] = acc / l
        lse_ref[...] = jnp.broadcast_to(m + jnp.log(l), (tq, LANE))

    ops = [q1, k1, v]
    specs = [_colspec(tq, lambda h, i: (i, q1cb(h))), _colspec(S, lambda h, i: (0, k1cb(h))),
             _colspec(S, lambda h, i: (0, vcb(h)))]
    if has2:
        ops += [q2, k2]
        specs += [_colspec(tq, lambda h, i: (i, q2cb(h))), _colspec(S, lambda h, i: (0, k2cb(h)))]
    if hastab:
        ops.append(tab)
        specs.append(pl.BlockSpec(tab.shape, lambda h, i: (0, 0, 0)))
    if hasc:
        ops += [ccol, crow]
        specs += [_colspec(tq, lambda h, i: (i, h)),
                  pl.BlockSpec((None, S // TK, 1, TK), lambda h, i: (h, 0, 0, 0))]
    o_spec = _colspec(tq, lambda h, i: (i, h))
    shp = jax.ShapeDtypeStruct((S, H * LANE), F32)
    outs, moved = _pcall(
        body, name=name, grid=(H, S // tq), in_specs=specs, out_specs=[o_spec, o_spec], out_shape=[shp, shp],
        operands=ops, sem=("parallel", "arbitrary"), comm=comm)
    return outs if comm is None else (outs, moved)


def _attn_bwd(S, H, q1, q1cb, k1, k1cb, v, vcb, o, do, lse, scale, name, q2=None, q2cb=None, k2=None, k2cb=None,
              rope2=None, tab=None, win=None, ccol=None, crow=None, comm=None):
    tq, tk = _soft_tiles(S)
    has2, hastab, hasc = q2 is not None, tab is not None, ccol is not None

    def body(*refs):
        it = iter(refs)
        q1r, k1r, vr, o_r, do_r, lse_r = (next(it) for _ in range(6))
        q2r, k2r, cos2_r, sin2_r = (next(it), next(it), next(it), next(it)) if has2 else (None,) * 4
        tabr = next(it) if hastab else None
        ccolr, crowr = (next(it), next(it)) if hasc else (None, None)
        dq1_r, dk1_o, dv_o = next(it), next(it), next(it)
        dk2_o = dv_o
        dcr_r = next(it) if hasc else None
        dk1_r, dv_r = next(it), next(it)
        dk2_r = next(it) if has2 else None
        i = pl.program_id(1)

        @pl.when(i == 0)
        def _():
            dk1_r[...] = jnp.zeros_like(dk1_r)
            dv_r[...] = jnp.zeros_like(dv_r)
            if has2:
                dk2_r[...] = jnp.zeros_like(dk2_r)
            if hasc:
                dcr_r[...] = jnp.zeros_like(dcr_r)

        q = q1r[...]
        qb2 = q2r[...] if has2 else None
        dob = do_r[...]
        delta = jnp.sum(dob.astype(F32) * o_r[...], axis=1, keepdims=True)
        lse_c = lse_r[:, 0:1]
        cq = ccolr[:, 0:1] if hasc else None
        qpos = i * tq + lax.broadcasted_iota(jnp.int32, (tq, tk), 0)
        kio = lax.broadcasted_iota(jnp.int32, (tq, tk), 1)
        j_diag = (i * tq) // tk
        j_lo = jnp.maximum((i * tq - win) // tk, 0) if win else 0

        if has2:
            q = jnp.concatenate([q, qb2], axis=1)

        def probs(j, masked):
            off = pl.multiple_of(j * tk, tk)
            kb = k1r[pl.ds(off, tk), :]
            if has2:
                kb = jnp.concatenate([kb, k2r[pl.ds(off, tk), :]], axis=1)
            s = _dot_nt(q, kb) * scale
            if hastab:
                s = s + tabr[i - j * (tk // tq)]
            else:
                if hasc:
                    s = s + (cq - _key_row(crowr, j, tk))
                if masked:
                    s = jnp.where(kio + j * tk <= qpos, s, NEG)
            p = jnp.exp(s - lse_c)
            dp = _dot_nt(dob, vr[pl.ds(off, tk), :])
            return off, kb, p, dp

        def sweep(fn, carry):
            if hastab:
                return lax.fori_loop(j_lo, j_diag + 1, functools.partial(fn, masked=False), carry)
            carry = lax.fori_loop(j_lo, j_diag, functools.partial(fn, masked=False), carry)
            return fn(j_diag, carry, True)

        if hasc:
            def dstep(j, acc, masked):
                _, _, p, dp = probs(j, masked)
                return acc + jnp.sum(p * dp, axis=1, keepdims=True)

            delta = sweep(dstep, jnp.zeros((tq, 1), F32))

        def step(j, dq, masked):
            off, kb, p, dp = probs(j, masked)
            ds = p * (dp - delta)
            dsb = ds.astype(BF16)
            dk = _dot_tn(dsb, q) * scale
            dk1_r[pl.ds(off, tk), :] += dk[:, :LANE]
            if has2:
                dk2_r[pl.ds(off, tk), :] += dk[:, LANE:]
            dv_r[pl.ds(off, tk), :] += _dot_tn(p.astype(BF16), dob)
            if hasc:
                cs = -jnp.sum(ds, axis=0, keepdims=True)
                for c in range(tk // TK):
                    dcr_r[j * (tk // TK) + c] += cs[:, c * TK:(c + 1) * TK]
            return dq + _dot_nn(dsb, kb)

        dq = sweep(step, jnp.zeros(q.shape, F32)) * scale
        dq1_r[:, :LANE] = dq[:, :LANE].astype(BF16)
        if has2:
            x2 = dq[:, LANE:]
            dq1_r[:, LANE:] = (x2 * cos2_r[...] + _rope_partner(x2, True) * sin2_r[...]).astype(BF16)

        @pl.when(i == S // tq - 1)
        def _():
            dk1_o[:, :LANE] = dk1_r[...].astype(BF16)
            if has2:
                dk1_o[:, LANE:] = dv_r[...].astype(BF16)
                dk2_o[...] = dk2_r[...].astype(BF16)
            else:
                dv_o[...] = dv_r[...].astype(BF16)

    qspec = _colspec(tq, lambda h, i: (i, h))
    kspec = _colspec(S, lambda h, i: (0, h))
    ops = [q1, k1, v, o, do, lse]
    specs = [_colspec(tq, lambda h, i: (i, q1cb(h))), _colspec(S, lambda h, i: (0, k1cb(h))),
             _colspec(S, lambda h, i: (0, vcb(h))), qspec, qspec, qspec]
    if has2:
        ops += [q2, k2, *rope2]
        tab2 = _colspec(tq, lambda h, i: (i, 0))
        specs += [_colspec(tq, lambda h, i: (i, q2cb(h))), _colspec(S, lambda h, i: (0, k2cb(h))), tab2, tab2]
    if hastab:
        ops.append(tab)
        specs.append(pl.BlockSpec(tab.shape, lambda h, i: (0, 0, 0)))
    if hasc:
        ops += [ccol, crow]
        specs += [qspec, pl.BlockSpec((None, S // TK, 1, TK), lambda h, i: (h, 0, 0, 0))]
    assert all(t.dtype == BF16 for t in ops[:3] + [do] + ([q2, k2] if has2 else []))
    shp = jax.ShapeDtypeStruct((S, H * LANE), BF16)
    if has2:
        out_specs = [pl.BlockSpec((tq, 2 * LANE), lambda h, i: (i, h)), pl.BlockSpec((S, 2 * LANE), lambda h, i: (0, h)),
                     kspec]
        wide = jax.ShapeDtypeStruct((S, H * 2 * LANE), BF16)
        out_shape = [wide, wide, shp]
    else:
        out_specs = [qspec, kspec, kspec]
        out_shape = [shp] * 3
    if hasc:
        out_specs.append(pl.BlockSpec((None, S // TK, 1, TK), lambda h, i: (h, 0, 0, 0)))
        out_shape.append(jax.ShapeDtypeStruct((H, S // TK, 1, TK), F32))
    outs, moved = _pcall(
        body, name=name, grid=(H, S // tq), in_specs=specs, out_specs=out_specs, out_shape=out_shape,
        operands=ops, scratch_shapes=[pltpu.VMEM((S, LANE), F32)] * (3 if has2 else 2),
        sem=("parallel", "arbitrary"), comm=comm)
    return outs if comm is None else (outs, moved)


def _scan_matrix(kind):
    j = np.arange(TK)[:, None]
    s = np.arange(TK)[None, :]
    tri = {"suffix_ex": j > s, "prefix_in": j <= s, "prefix_ex": j < s}[kind].astype(np.float32)
    half = np.concatenate([tri, np.ones((TK, TK), np.float32)], axis=1)
    return jnp.asarray(np.concatenate([half, half], axis=0), BF16)


def _scan_mxu(x, mat, carry, reverse):
    n = x.shape[1] // TK
    hi = x.astype(BF16)
    lo = (x - hi.astype(F32)).astype(BF16)
    parts = [None] * n
    for b in (reversed(range(n)) if reverse else range(n)):
        sl = slice(b * TK, (b + 1) * TK)
        r = _dot_nn(jnp.concatenate([hi[:, sl], lo[:, sl]], axis=1), mat)
        parts[b] = r[:, :TK] + carry
        carry = carry + r[:, TK:]
    return jnp.concatenate(parts, axis=1), carry


def _stick_logs(z):
    e = jnp.exp(-jnp.abs(z))
    return e, -jnp.maximum(z, 0.0) - jnp.log(1.0 + e)


def _stick_fwd(S, H, x, qcb, kcb, vcb, scale, name, comm=None):
    tq, tk = _soft_tiles(S)
    assert x.dtype == BF16

    def body(q_r, k_r, v_r, mat_r, o_ref, t_ref):
        i = pl.program_id(1)
        q = q_r[...]
        qpos = i * tq + lax.broadcasted_iota(jnp.int32, (tq, tk), 0)
        lane = lax.broadcasted_iota(jnp.int32, (tq, tk), 1)
        j_diag = (i * tq) // tk

        def step(j, carry, masked):
            c, acc = carry
            off = pl.multiple_of(j * tk, tk)
            z = _dot_nt(q, k_r[pl.ds(off, tk), :]) * scale
            _, lk = _stick_logs(z)
            if masked:
                past = lane + j * tk < qpos
                lk = jnp.where(past, lk, 0.0)
            suf, c = _scan_mxu(lk, mat_r[...], c, True)
            a = jnp.exp(z + lk + suf)
            if masked:
                a = jnp.where(past, a, 0.0)
            acc = acc + _dot_nn(a.astype(BF16), v_r[pl.ds(off, tk), :])
            return c, acc

        carry = step(j_diag, (jnp.zeros((tq, TK), F32), jnp.zeros((tq, LANE), F32)), True)
        c, acc = lax.fori_loop(0, j_diag, lambda jj, cr: step(j_diag - 1 - jj, cr, False), carry)
        o_ref[...] = acc
        t_ref[...] = c

    o_spec = _colspec(tq, lambda h, i: (i, h))
    shp = jax.ShapeDtypeStruct((S, H * LANE), F32)
    mat = _scan_matrix("suffix_ex")
    outs, moved = _pcall(
        body, name=name, grid=(H, S // tq),
        in_specs=[_colspec(tq, lambda h, i: (i, qcb(h))), _colspec(S, lambda h, i: (0, kcb(h))),
                  _colspec(S, lambda h, i: (0, vcb(h))), pl.BlockSpec(mat.shape, lambda h, i: (0, 0))],
        out_specs=[o_spec, o_spec], out_shape=[shp, shp],
        operands=[x, x, x, mat], sem=("parallel", "arbitrary"), comm=comm)
    return outs if comm is None else (outs, moved)


def _stick_bwd(S, H, x, qcb, kcb, vcb, do, tot, scale, name, comm=None):
    tq, tk = _soft_tiles(S)
    assert x.dtype == BF16 and do.dtype == BF16

    def body(q_r, k_r, v_r, do_r, t_r, pin_r, pex_r, dq_r, dk_o, dv_o, dk_r, dv_r):
        i = pl.program_id(1)

        @pl.when(i == 0)
        def _():
            dk_r[...] = jnp.zeros_like(dk_r)
            dv_r[...] = jnp.zeros_like(dv_r)

        q = q_r[...]
        dob = do_r[...]
        total = jnp.concatenate([t_r[...]] * (tk // TK), axis=1)
        qpos = i * tq + lax.broadcasted_iota(jnp.int32, (tq, tk), 0)
        lane = lax.broadcasted_iota(jnp.int32, (tq, tk), 1)
        j_diag = (i * tq) // tk

        def step(j, carry, masked):
            cl, cg, dq = carry
            off = pl.multiple_of(j * tk, tk)
            kb = k_r[pl.ds(off, tk), :]
            z = _dot_nt(q, kb) * scale
            e, lk = _stick_logs(z)
            if masked:
                past = lane + j * tk < qpos
                lk = jnp.where(past, lk, 0.0)
            pre, cl = _scan_mxu(lk, pin_r[...], cl, False)
            a = jnp.exp(z + lk + (total - pre))
            if masked:
                a = jnp.where(past, a, 0.0)
            g = _dot_nt(dob, v_r[pl.ds(off, tk), :]) * a
            gpre, cg = _scan_mxu(g, pex_r[...], cg, False)
            inv = 1.0 / (1.0 + e)
            small = e * inv
            pos = z >= 0
            dz = g * jnp.where(pos, small, inv) - jnp.where(pos, inv, small) * gpre
            if masked:
                dz = jnp.where(past, dz, 0.0)
            dzb = dz.astype(BF16)
            dk_r[pl.ds(off, tk), :] += _dot_tn(dzb, q) * scale
            dv_r[pl.ds(off, tk), :] += _dot_tn(a.astype(BF16), dob)
            return cl, cg, dq + _dot_nn(dzb, kb)

        zt = jnp.zeros((tq, TK), F32)
        carry = lax.fori_loop(0, j_diag, functools.partial(step, masked=False), (zt, zt, jnp.zeros((tq, LANE), F32)))
        dq_r[...] = (step(j_diag, carry, True)[2] * scale).astype(BF16)

        @pl.when(i == S // tq - 1)
        def _():
            dk_o[...] = dk_r[...].astype(BF16)
            dv_o[...] = dv_r[...].astype(BF16)

    qspec = _colspec(tq, lambda h, i: (i, h))
    kspec = _colspec(S, lambda h, i: (0, h))
    shp = jax.ShapeDtypeStruct((S, H * LANE), BF16)
    pin, pex = _scan_matrix("prefix_in"), _scan_matrix("prefix_ex")
    mspec = pl.BlockSpec(pin.shape, lambda h, i: (0, 0))
    outs, moved = _pcall(
        body, name=name, grid=(H, S // tq),
        in_specs=[_colspec(tq, lambda h, i: (i, qcb(h))), _colspec(S, lambda h, i: (0, kcb(h))),
                  _colspec(S, lambda h, i: (0, vcb(h))), qspec, qspec, mspec, mspec],
        out_specs=[qspec, kspec, kspec], out_shape=[shp] * 3,
        operands=[x, x, x, do, tot, pin, pex], scratch_shapes=[pltpu.VMEM((S, LANE), F32)] * 2,
        sem=("parallel", "arbitrary"), comm=comm)
    return outs if comm is None else (outs, moved)


def _scan8(x, rows, reverse):
    for sh in (1, 2, 4):
        if reverse:
            x = x + jnp.where(rows + sh < 8, pltpu.roll(x, 8 - sh, 0), 0.0)
        else:
            x = x + jnp.where(rows >= sh, pltpu.roll(x, sh, 0), 0.0)
    return x


def _fox_prep(S, H, proj, fcb, bias, name):
    tk = TK

    def body(f_ref, b_ref, ccol_ref, crow_ref, scr):
        rows = lax.broadcasted_iota(jnp.int32, (8, LANE), 0)

        def step(t, carry):
            off = pl.multiple_of(t * 8, 8)
            xb = f_ref[pl.ds(off, 8), :] + b_ref[...]
            lf = jnp.minimum(xb, 0.0) - jnp.log(1.0 + jnp.exp(-jnp.abs(xb)))
            lf = _scan8(lf, rows, False) + carry
            scr[pl.ds(off, 8), :] = lf
            return lf[7:8, :]

        lax.fori_loop(0, S // 8, step, jnp.zeros((1, LANE), F32))
        for h in range(H):
            ccol_ref[:, h * LANE:(h + 1) * LANE] = jnp.broadcast_to(scr[:, h:h + 1], (S, LANE))

            def tr(t, _):
                off = pl.multiple_of(t * tk, tk)
                blk = ccol_ref[pl.ds(off, tk), h * LANE:(h + 1) * LANE]
                crow_ref[h, t] = blk.T[0:1, :]
                return 0

            lax.fori_loop(0, S // tk, tr, 0)

    return pl.pallas_call(
        body, name=name, grid=(1,),
        in_specs=[_colspec(S, lambda i: (0, fcb)), pl.BlockSpec((1, LANE), lambda i: (0, 0))],
        out_specs=[pl.BlockSpec((S, H * LANE), lambda i: (0, 0)),
                   pl.BlockSpec((H, S // tk, 1, tk), lambda i: (0, 0, 0, 0))],
        out_shape=[jax.ShapeDtypeStruct((S, H * LANE), F32), jax.ShapeDtypeStruct((H, S // tk, 1, tk), F32)],
        scratch_shapes=[pltpu.VMEM((S, LANE), F32)],
        compiler_params=_cp("arbitrary"),
    )(proj, bias)


def _fox_bwd(S, H, proj, fcb, bias, dcr, name):
    tk = TK

    def body(f_ref, b_ref, dcr_ref, df_ref, db_ref, scr):
        rows = lax.broadcasted_iota(jnp.int32, (8, LANE), 0)
        lane_t = lax.broadcasted_iota(jnp.int32, (tk, LANE), 1)
        nb = S // 8

        def tr(t, _):
            off = pl.multiple_of(t * tk, tk)
            d = jnp.zeros((tk, LANE), F32)
            for h in range(H):
                d = d + jnp.where(lane_t == h, jnp.broadcast_to(dcr_ref[h, t], (LANE, tk)).T, 0.0)
            scr[pl.ds(off, tk), :] = d
            return 0

        lax.fori_loop(0, S // tk, tr, 0)

        def step(tt, carry):
            suffix, db = carry
            off = pl.multiple_of((nb - 1 - tt) * 8, 8)
            d = _scan8(scr[pl.ds(off, 8), :], rows, True) + suffix
            xb = f_ref[pl.ds(off, 8), :] + b_ref[...]
            e = jnp.exp(-jnp.abs(xb))
            dx = d * jnp.where(xb >= 0, e, 1.0) / (1.0 + e)
            df_ref[pl.ds(off, 8), :] = dx
            return d[0:1, :], db + jnp.sum(dx, axis=0, keepdims=True)

        z = jnp.zeros((1, LANE), F32)
        _, db = lax.fori_loop(0, nb, step, (z, z))
        db_ref[...] = db

    return pl.pallas_call(
        body, name=name, grid=(1,),
        in_specs=[_colspec(S, lambda i: (0, fcb)), pl.BlockSpec((1, LANE), lambda i: (0, 0)),
                  pl.BlockSpec((H, S // tk, 1, tk), lambda i: (0, 0, 0, 0))],
        out_specs=[pl.BlockSpec((S, LANE), lambda i: (0, 0)), pl.BlockSpec((1, LANE), lambda i: (0, 0))],
        out_shape=[jax.ShapeDtypeStruct((S, LANE), F32), jax.ShapeDtypeStruct((1, LANE), F32)],
        scratch_shapes=[pltpu.VMEM((S, LANE), F32)],
        compiler_params=_cp("arbitrary"),
    )(proj, bias, dcr)


def _adamw(w, slots, m, v, name, comm=None):
    R, C = w.shape
    parts = len(slots)
    rows = R // parts
    row_bytes = 2 * C * (parts * NDEV * slots[0].dtype.itemsize + 7 * 4)
    sub = 32 // slots[0].dtype.itemsize
    tiles = [t for t in range(sub, rows + 1, sub) if rows % t == 0] or [rows]
    tr = max([t for t in tiles if t * row_bytes <= ADAMW_BLOCK_BYTES] or tiles[:1])
    per = R // parts // tr
    c1 = 1.0 - ADAM_B1 ** ADAM_STEP
    c2 = 1.0 - ADAM_B2 ** ADAM_STEP

    def body(*refs):
        w_ref, m_ref, v_ref = refs[:3]
        s_refs = refs[3:3 + parts]
        g_ref, d_ref, nm_ref, nv_ref = refs[3 + parts:]
        part = pl.program_id(0) // per
        for a, s_ref in enumerate(s_refs):
            @pl.when(part == a)
            def _():
                g = s_ref[0].astype(F32)
                for s in range(1, NDEV):
                    g = g + s_ref[s].astype(F32)
                g_ref[...] = g

        g = g_ref[...]
        mn = ADAM_B1 * m_ref[...] + (1.0 - ADAM_B1) * g
        vn = ADAM_B2 * v_ref[...] + (1.0 - ADAM_B2) * (g * g)
        nm_ref[...] = mn
        nv_ref[...] = vn
        d_ref[...] = -ADAM_LR * ((mn / c1) / (jnp.sqrt(vn / c2) + ADAM_EPS) + ADAM_WD * w_ref[...])

    row = pl.BlockSpec((tr, C), lambda i: (i, 0))
    s_specs = [pl.BlockSpec((NDEV, tr, C), lambda i, a=a: (0, jnp.clip(i - a * per, 0, per - 1), 0))
               for a in range(parts)]
    outs, moved = _pcall(
        body, name=name, grid=(R // tr,), in_specs=[row, row, row] + s_specs,
        out_specs=[row] * 4, out_shape=[jax.ShapeDtypeStruct((R, C), F32)] * 4,
        operands=[w, m, v, *slots], sem=("arbitrary",), comm=comm)
    return outs if comm is None else (outs, moved)


class _Layout:
    def __init__(self, D):
        self.GW = GW = D // 4
        self.H = H = GW // HEAD
        self.QL, self.KVL = 0, Q_LORA
        base = Q_LORA + KV_LORA
        (self.QB, self.KB, self.VB, self.QC, self.KC, self.VC, self.QD, self.KD, self.VD) = (
            base + k * GW for k in range(9))
        self.KR = base + 9 * GW
        self.FC = self.KR + LANE
        self.PW = -(-(self.FC + LANE) // 512) * 512
        self.o_kr = base
        self.o_bc = base + QK_ROPE
        self.o_fc = self.o_bc + 6 * GW
        self.o_d = self.o_fc + H
        self.IN = self.o_d + 3 * GW

    def pad(self, w):
        z = lambda n: jnp.zeros(w.shape[:-1] + (n,), w.dtype)
        return jnp.concatenate([
            w[..., :self.o_kr], w[..., self.o_bc:self.o_fc], w[..., self.o_d:self.IN],
            w[..., self.o_kr:self.o_bc], z(LANE - QK_ROPE), w[..., self.o_fc:self.o_d], z(LANE - self.H),
            z(self.PW - self.FC - LANE)], axis=-1)

    def unpad(self, g):
        return jnp.concatenate([
            g[..., :self.KR - 9 * self.GW], g[..., self.KR:self.KR + QK_ROPE], g[..., self.QB:self.QD],
            g[..., self.FC:self.FC + self.H], g[..., self.QD:self.KR]], axis=-1)


def _rope_tables(S):
    pos = jnp.arange(S, dtype=F32)

    def cs(dim):
        inv = ROPE_THETA ** (-jnp.arange(0, dim, 2, dtype=F32) / dim)
        ang = pos[:, None] * inv[None, :]
        return jnp.cos(ang), jnp.sin(ang)

    c, s = cs(HEAD)
    full = (jnp.concatenate([c, c], 1), jnp.concatenate([-s, s], 1))
    c, s = cs(QK_ROPE)
    z = jnp.zeros((S, LANE - QK_ROPE), F32)
    half = (jnp.concatenate([c, c, z], 1), jnp.concatenate([-s, s, z], 1))
    return full, half


def _dilated_table(tq, tk):
    win = max(w for w, _ in DILATED_PAIRS)
    nd = (win + tk) // tq + 1
    d = np.arange(nd)[:, None, None] * tq + np.arange(tq)[None, :, None] - np.arange(tk)[None, None, :]
    mult = np.zeros(d.shape, np.float64)
    for w, dil in DILATED_PAIRS:
        mult += (d >= 0) & (d <= w) & (d % dil == 0)
    return jnp.asarray(np.where(mult > 0, np.log(np.maximum(mult, 1.0)), NEG), F32), win


def _pack(arrs):
    rows = []
    for a in arrs:
        f = a.reshape(-1).astype(F32)
        f = jnp.pad(f, (0, (-f.shape[0]) % LANE))
        rows.append(f.reshape(-1, LANE))
    p = jnp.concatenate(rows, 0)
    return jnp.pad(p, ((0, (-p.shape[0]) % 8), (0, 0)))


def _unpack(p, shapes):
    out, r = [], 0
    for shp in shapes:
        n = int(np.prod(shp))
        nr = -(-n // LANE)
        out.append(p[r:r + nr].reshape(-1)[:n].reshape(shp))
        r += nr
    return out


def kernel(x, attn_norm, w_in, mla_q_norm, w_uq, mla_kv_norm, w_ukv, fox_forget_bias, group_norm, w_out, ffn_norm, w_gate, w_up, w_down, final_norm, loss_target, m_attn_norm, m_w_in, m_mla_q_norm, m_w_uq, m_mla_kv_norm, m_w_ukv, m_fox_forget_bias, m_group_norm, m_w_out, m_ffn_norm, m_w_gate, m_w_up, m_w_down, m_final_norm, v_attn_norm, v_w_in, v_mla_q_norm, v_w_uq, v_mla_kv_norm, v_w_ukv, v_fox_forget_bias, v_group_norm, v_w_out, v_ffn_norm, v_w_gate, v_w_up, v_w_down, v_final_norm):
    _, S, D = x.shape
    L = attn_norm.shape[0]
    lay = _Layout(D)
    H, GW, PW = lay.H, lay.GW, lay.PW
    FB = w_gate.shape[2]
    QKA = HEAD + QK_ROPE
    x = x[0]
    target = loss_target[0]
    rope_full, rope_half = _rope_tables(S)
    neg = lambda t: (t[0], -t[1])
    tab, win = _dilated_table(*_soft_tiles(S))
    cb = lambda col: col // LANE

    sh = dict(w_in=lay.pad(w_in).astype(BF16),
              **{n: w.astype(BF16) for n, w in (("w_uq", w_uq), ("w_ukv", w_ukv), ("w_out", w_out),
                                                  ("w_gate", w_gate), ("w_up", w_up), ("w_down", w_down))})
    first3 = ["w_in", "w_uq", "w_ukv"]

    def first_weights(g):
        wuq = jnp.transpose(g[1], (1, 0, 2)).reshape(Q_LORA, H, QKA)
        wuq = jnp.pad(wuq, ((0, 0), (0, 0), (0, 2 * LANE - QKA))).reshape(Q_LORA, H * 2 * LANE)
        return dict(win=g[0].reshape(D, PW), wuq=wuq,
                    wukv=jnp.transpose(g[2], (1, 0, 2)).reshape(KV_LORA, H * 2 * LANE))

    def row(a):
        return a.reshape(1, -1)

    def forward(l, x0, W):
        A = dict(x0=x0)
        A["bias"] = jnp.pad(row(fox_forget_bias[l]), ((0, 0), (0, LANE - H)))
        h1 = A["h1"] = _rms_fwd(x0, row(attn_norm[l]), D, 0, BF16, "attn_norm")
        (proj, pb), (W["wg"],) = _mm(h1, W["win"], "in_proj", out_dtype=(F32, BF16),
                                     comm=("gather", [sh["w_gate"][l]]))
        A["proj"], A["pb"] = proj, pb
        qln = A["qln"] = _rms_fwd(proj, row(mla_q_norm[l]), Q_LORA, cb(lay.QL) // 4, BF16, "q_norm")
        kvln = A["kvln"] = _rms_fwd(proj, row(mla_kv_norm[l]), KV_LORA, cb(lay.KVL) // 4, BF16, "kv_norm")
        qa, qab = _mm(qln, W["wuq"], "q_up", out_dtype=(F32, BF16))
        A["qab"] = qab
        kv = A["kv"] = _mm(kvln, W["wukv"], "kv_up", out_dtype=BF16)
        q_pe = A["q_pe"] = _rope(qa, (1, H, 2), *rope_half, True, BF16, "rope_q_mla")
        k_pe = A["k_pe"] = _rope(proj, (cb(lay.KR), 1, 1), *rope_half, True, BF16, "rope_k_mla")
        (A["o_a"], A["lse_a"]), (W["wu"],) = _attn_fwd(
            S, H, qab, lambda h: 2 * h, kv, lambda h: 2 * h, kv, lambda h: 2 * h + 1, QKA ** -0.5, "mla_fwd",
            q2=q_pe, q2cb=lambda h: h, k2=k_pe, k2cb=lambda h: 0, comm=("gather", [sh["w_up"][l]]))
        qk_b = A["qk_b"] = _rope(proj, (cb(lay.QB), 2 * H, 1), *rope_full, False, BF16, "rope_qk_dil")
        (A["o_b"], A["lse_b"]), (g_down,) = _attn_fwd(
            S, H, qk_b, lambda h: h, qk_b, lambda h: H + h, pb, lambda h: cb(lay.VB) + h, HEAD ** -0.5,
            "dilated_fwd", tab=tab, win=win, comm=("gather", [sh["w_down"][l]]))
        W["wd"] = g_down.reshape(NDEV * FB, D)
        ccol, crow = A["ccol"], A["crow"] = _fox_prep(S, H, proj, cb(lay.FC), A["bias"], "fox_prep")
        (A["o_c"], A["lse_c"]), (g_out,) = _attn_fwd(
            S, H, pb, lambda h: cb(lay.QC) + h, pb, lambda h: cb(lay.KC) + h, pb, lambda h: cb(lay.VC) + h,
            HEAD ** -0.5, "fox_fwd", ccol=ccol, crow=crow, comm=("gather", [sh["w_out"][l]]))
        W["wout"] = g_out.reshape(4 * GW, D)
        A["o_d"], A["tot_d"] = _stick_fwd(
            S, H, pb, lambda h: cb(lay.QD) + h, lambda h: cb(lay.KD) + h, lambda h: cb(lay.VD) + h,
            HEAD ** -0.5, "stick_fwd")
        mix = A["mix"] = _gn_fwd([A["o_a"], A["o_b"], A["o_c"], A["o_d"]], row(group_norm[l]), "group_norm")
        x1 = A["x1"] = _mm(mix, W["wout"], "out_proj", res=x0)
        h2 = A["h2"] = _rms_fwd(x1, row(ffn_norm[l]), D, 0, BF16, "ffn_norm")
        nxt = None
        if l + 1 < L:
            (A["g"], A["u"], A["act"]), nxt = _ffn_up(
                h2, W["wg"], W["wu"], "ffn_up", comm=("gather", [sh[n][l + 1] for n in first3]))
        else:
            A["g"], A["u"], A["act"] = _ffn_up(h2, W["wg"], W["wu"], "ffn_up_last")
        return _mm_down(A["act"], W["wd"], x1, "ffn_down"), A, nxt

    def backward(l, dx2, dx2b, W, A, late):
        proj, pb = A["proj"], A["pb"]
        G, small, got = {}, {}, {}
        dgate, dup = _ffn_dact(dx2b, W["wd"], A["g"], A["u"], "ffn_dact")
        G["w_down"] = _mm_dwdown(A["act"], dx2b, "dw_down").reshape(NDEV, FB, D)
        dh2 = _mm_dh2(dgate, W["wg"], dup, W["wu"], "ffn_dh")
        if late is None:
            G["w_gate"] = _mm_dwgate(A["h2"], dgate, "dw_gate_top")
            G["w_up"] = _mm_dwgate(A["h2"], dup, "dw_up_top")
        else:
            G["w_gate"], (got[l + 1, "w_in_a"],) = _mm_dwgate(
                A["h2"], dgate, "dw_gate", comm=("exchange", [late["w_in_a"]]))
            G["w_up"], (got[l + 1, "w_in_b"],) = _mm_dwgate(
                A["h2"], dup, "dw_up", comm=("exchange", [late["w_in_b"]]))
        dx1, dx1b, small["ffn_norm"] = _rms_bwd(A["x1"], row(ffn_norm[l]), dh2, D, 0, "ffn_norm_bwd", res=dx2)
        dmix = _mm(dx1b, W["wout"], "out_proj_dx", tb=True)
        G["w_out"] = _mm(A["mix"], dx1b, "dw_out", ta=True, out_dtype=BF16).reshape(NDEV, 4 * GW // NDEV, D)
        do_a, do_b, do_c, do_d, small["group_norm"] = _gn_bwd(
            [A["o_a"], A["o_b"], A["o_c"], A["o_d"]], row(group_norm[l]), dmix, "group_norm_bwd")
        (dq_d, dk_d, dv_d), (got[l, "w_up"],) = _stick_bwd(
            S, H, pb, lambda h: cb(lay.QD) + h, lambda h: cb(lay.KD) + h, lambda h: cb(lay.VD) + h,
            do_d, A["tot_d"], HEAD ** -0.5, "stick_bwd", comm=("exchange", [G["w_up"]]))
        (dq_c, dk_c, dv_c, dcc), (got[l, "w_down"],) = _attn_bwd(
            S, H, pb, lambda h: cb(lay.QC) + h, pb, lambda h: cb(lay.KC) + h, pb, lambda h: cb(lay.VC) + h,
            A["o_c"], do_c, A["lse_c"], HEAD ** -0.5, "fox_bwd",
            ccol=A["ccol"], crow=A["crow"], comm=("exchange", [G["w_down"]]))
        dfc, dbias = _fox_bwd(S, H, proj, cb(lay.FC), A["bias"], dcc, "fox_gate_bwd")
        small["fox_forget_bias"] = dbias[0, :H]
        qk_b = A["qk_b"]
        (dq_b, dk_b, dv_b), moved = _attn_bwd(
            S, H, qk_b, lambda h: h, qk_b, lambda h: H + h, pb, lambda h: cb(lay.VB) + h,
            A["o_b"], do_b, A["lse_b"], HEAD ** -0.5, "dilated_bwd" if late else "dilated_bwd_top", tab=tab, win=win,
            comm=("exchange", [G["w_out"]] + ([late["w_uq"], late["w_ukv"]] if late else [])))
        got[l, "w_out"] = moved[0]
        if late:
            got[l + 1, "w_uq"], got[l + 1, "w_ukv"] = moved[1:]
        dqk_b = _rope(jnp.concatenate([dq_b, dk_b], 1), (0, 2 * H, 1), *neg(rope_full), False, BF16, "rope_qk_dil_bwd")
        qab, kv = A["qab"], A["kv"]
        (dqa, dkv, dk2), (got[l, "w_gate"],) = _attn_bwd(
            S, H, qab, lambda h: 2 * h, kv, lambda h: 2 * h, kv, lambda h: 2 * h + 1,
            A["o_a"], do_a, A["lse_a"], QKA ** -0.5, "mla_bwd", q2=A["q_pe"], q2cb=lambda h: h, k2=A["k_pe"],
            k2cb=lambda h: 0, rope2=neg(rope_half), comm=("exchange", [G["w_gate"]]))
        dk_pe = _rope(dk2.astype(F32).reshape(S, H, LANE).sum(1), (0, 1, 1), *neg(rope_half), True, BF16,
                      "rope_k_mla_bwd")
        dwuq = _mm(A["qln"], dqa, "dw_uq", ta=True, out_dtype=BF16)
        dwuq = dwuq.reshape(Q_LORA, H, 2 * LANE)[:, :, :QKA].reshape(Q_LORA, NDEV, H * QKA // NDEV)
        G["w_uq"] = jnp.transpose(dwuq, (1, 0, 2))
        dwukv = _mm(A["kvln"], dkv, "dw_ukv", ta=True, out_dtype=BF16).reshape(KV_LORA, NDEV, H * 2 * LANE // NDEV)
        G["w_ukv"] = jnp.transpose(dwukv, (1, 0, 2))
        dqln = _mm(dqa, W["wuq"], "q_up_dx", tb=True)
        dkvln = _mm(dkv, W["wukv"], "kv_up_dx", tb=True)
        _, dql, small["mla_q_norm"] = _rms_bwd(proj, row(mla_q_norm[l]), dqln, Q_LORA, cb(lay.QL) // 4, "q_norm_bwd")
        _, dkvl, small["mla_kv_norm"] = _rms_bwd(
            proj, row(mla_kv_norm[l]), dkvln, KV_LORA, cb(lay.KVL) // 4, "kv_norm_bwd")
        dproj = jnp.concatenate([
            dql, dkvl, dqk_b, dv_b, dq_c, dk_c, dv_c, dq_d, dk_d, dv_d,
            dk_pe, dfc.astype(BF16), jnp.zeros((S, PW - lay.FC - LANE), BF16)], axis=1)
        g_in = _mm(A["h1"], dproj, "dw_in", ta=True, out_dtype=BF16).reshape(NDEV, D // NDEV, PW)
        half = D // NDEV // 2
        late = dict(w_in_a=g_in[:, :half], w_in_b=g_in[:, half:], w_uq=G["w_uq"], w_ukv=G["w_ukv"])
        if l == 0:
            names = ["w_in_a", "w_uq", "w_ukv"]
            dh1, moved = _mm(dproj, W["win"], "in_proj_dx_last", tb=True,
                             comm=("exchange", [late[n] for n in names]))
            got.update({(0, n): s for n, s in zip(names, moved)})
        else:
            dh1 = _mm(dproj, W["win"], "in_proj_dx", tb=True)
        dx0, dx0b, small["attn_norm"] = _rms_bwd(A["x0"], row(attn_norm[l]), dh1, D, 0, "attn_norm_bwd", res=dx1)
        return dx0, dx0b, late, got, small

    big = first3 + ["w_out", "w_gate", "w_up", "w_down"]
    Ws, As = [], []
    xc = x
    nxt = _comm_alone("gather", [sh[n][0] for n in first3], "gather_first")
    for l in range(L):
        W = first_weights(nxt)
        xc, A, nxt = forward(l, xc, W)
        Ws.append(W)
        As.append(A)
    dx, loss_part = _final_loss(xc, row(final_norm), target, "final_loss")
    dx, dxb, dfinal = _rms_bwd(xc, row(final_norm), dx, D, 0, "final_norm_bwd")
    slots = {}
    smalls = [None] * L
    late = None
    for l in reversed(range(L)):
        dx, dxb, late, got, smalls[l] = backward(l, dx, dxb, Ws[l], As[l], late)
        slots.update(got)

    names_small = ["attn_norm", "mla_q_norm", "mla_kv_norm", "fox_forget_bias", "group_norm", "ffn_norm"]
    params = dict(attn_norm=attn_norm, mla_q_norm=mla_q_norm, mla_kv_norm=mla_kv_norm, fox_forget_bias=fox_forget_bias,
                  group_norm=group_norm, ffn_norm=ffn_norm, final_norm=final_norm, w_in=w_in, w_uq=w_uq, w_ukv=w_ukv,
                  w_out=w_out, w_gate=w_gate, w_up=w_up, w_down=w_down)
    moms = dict(attn_norm=(m_attn_norm, v_attn_norm), mla_q_norm=(m_mla_q_norm, v_mla_q_norm),
                mla_kv_norm=(m_mla_kv_norm, v_mla_kv_norm), fox_forget_bias=(m_fox_forget_bias, v_fox_forget_bias),
                group_norm=(m_group_norm, v_group_norm), ffn_norm=(m_ffn_norm, v_ffn_norm),
                final_norm=(m_final_norm, v_final_norm), w_in=(m_w_in, v_w_in), w_uq=(m_w_uq, v_w_uq),
                w_ukv=(m_w_ukv, v_w_ukv), w_out=(m_w_out, v_w_out), w_gate=(m_w_gate, v_w_gate),
                w_up=(m_w_up, v_w_up), w_down=(m_w_down, v_w_down))
    small_list = names_small + ["final_norm"]
    small_grads = [jnp.stack([smalls[l][n].reshape(params[n].shape[1:]) for l in range(L)]) for n in names_small]
    small_grads.append(dfinal.reshape(final_norm.shape))
    shapes = [params[n].shape for n in small_list] + [(LANE,)]
    packed_g = _comm_alone("gather", [_pack(small_grads + [loss_part.reshape(LANE)])], "gather_small")[0]
    zero = jnp.zeros((LANE,), F32)
    res_small = _adamw(_pack([params[n] for n in small_list] + [zero]), [packed_g],
                       _pack([moms[n][0] for n in small_list] + [zero]),
                       _pack([moms[n][1] for n in small_list] + [zero]), "adamw_small")
    unp = [_unpack(r, shapes) for r in res_small]
    out = {n: tuple(unp[k][i] for k in range(4)) for i, n in enumerate(small_list)}
    loss = unp[0][-1][0]

    for n in ["w_gate", "w_up", "w_down", "w_out", "w_uq", "w_ukv", "w_in"]:
        if n == "w_in":
            st = [lay.unpad(slots[l, n + h]) for l in range(L) for h in ("_a", "_b")]
        else:
            st = [slots[l, n] for l in range(L)]
        C = st[0].shape[-1]
        st = [s.reshape(NDEV, -1, C) for s in st]
        args = (params[n].reshape(-1, C), st, moms[n][0].reshape(-1, C), moms[n][1].reshape(-1, C), "adamw_" + n)
        if n == "w_gate":
            res, (slots[0, "w_in_b"],) = _adamw(*args, comm=("exchange", [late["w_in_b"]]))
        else:
            res = _adamw(*args)
        out[n] = tuple(r.reshape(params[n].shape) for r in res)

    order = ["attn_norm", "w_in", "mla_q_norm", "w_uq", "mla_kv_norm", "w_ukv", "fox_forget_bias", "group_norm",
             "w_out", "ffn_norm", "w_gate", "w_up", "w_down", "final_norm"]
    return (loss, dx[None], *[out[n][0] for n in order], *[out[n][1] for n in order],
            *[out[n][2] for n in order], *[out[n][3] for n in order])
```

```python
import functools
import math

import numpy as np
import jax
import jax.numpy as jnp
from jax import lax
from jax.experimental import pallas as pl
from jax.experimental.pallas import tpu as pltpu

F32 = jnp.float32
BF16 = jnp.bfloat16
NDEV = 8
LANE = 128
HEAD = 128
Q_LORA = 512
KV_LORA = 512
QK_ROPE = 64
DILATED_PAIRS = ((128, 1), (512, 4), (2048, 16))
ROPE_THETA = 10000.0
EPS = 1e-6
NEG = -1e30
TQ = 256
TK = 128
TKS = 512
VMEM_LIMIT = 48 * 1024 * 1024
MM_OPERAND_BYTES = 20 * 1024 * 1024
FFN_BLOCKS_PER_STEP = 2
ADAMW_BLOCK_BYTES = 24 * 1024 * 1024
ADAM_LR, ADAM_B1, ADAM_B2, ADAM_EPS, ADAM_WD, ADAM_STEP = 0.001, 0.9, 0.999, 1e-08, 0.01, 10
MESH = pl.DeviceIdType.MESH
ANY = pl.BlockSpec(memory_space=pl.ANY)


def _cp(*sem):
    return pltpu.CompilerParams(dimension_semantics=sem, vmem_limit_bytes=VMEM_LIMIT)


def _dot(a, b, ca, cb):
    return lax.dot_general(a, b, (((ca,), (cb,)), ((), ())), preferred_element_type=F32)


def _dot_nn(a, b):
    return _dot(a, b, 1, 0)


def _dot_nt(a, b):
    return _dot(a, b, 1, 1)


def _dot_tn(a, b):
    return _dot(a, b, 0, 0)


def _tile(n, t):
    if n <= t:
        return n
    t -= t % LANE
    while n % t:
        t -= LANE
    return t


def _direct_copies(ins, outs, send_sems, recv_sems, local_sems, want_recvs=True):
    x, y, c = lax.axis_index("x"), lax.axis_index("y"), lax.axis_index("c")
    my_id = 4 * x + 2 * y + c
    local, sends, recvs = [], [], []
    for a in range(len(ins)):
        mine = ins[a].at[my_id]
        local.append(pltpu.make_async_copy(mine, outs[a].at[my_id], local_sems.at[a]))
        for k in range(1, NDEV):
            peer = (1 - x if k & 4 else x, 1 - y if k & 2 else y, 1 - c if k & 1 else c)
            pid = 4 * peer[0] + 2 * peer[1] + peer[2]
            sems = dict(send_sem=send_sems.at[a, k - 1], recv_sem=recv_sems.at[a, k - 1],
                        device_id=peer, device_id_type=MESH)
            sends.append(pltpu.make_async_remote_copy(src_ref=ins[a].at[pid], dst_ref=outs[a].at[my_id], **sems))
            if want_recvs:
                recvs.append(pltpu.make_async_remote_copy(src_ref=mine, dst_ref=outs[a].at[pid], **sems))
    return local, sends, recvs


def _comm_start(kind, ins, outs, send_sems, recv_sems, local_sems):
    if kind == "exchange":
        local, sends, _ = _direct_copies(ins, outs, send_sems, recv_sems, local_sems, want_recvs=False)
        for cp in local + sends:
            cp.start()
        return
    x, y, c = lax.axis_index("x"), lax.axis_index("y"), lax.axis_index("c")
    for a in range(len(ins)):
        mine = outs[a].at[4 * x + 2 * y + c]
        pltpu.make_async_copy(ins[a], mine, local_sems.at[a]).start()
        for k, to in enumerate([(x, y, 1 - c), (1 - x, y, c), (x, 1 - y, c), (1 - x, 1 - y, c)]):
            pltpu.make_async_remote_copy(src_ref=ins[a], dst_ref=mine, send_sem=send_sems.at[a, k],
                                         recv_sem=recv_sems.at[a, k], device_id=to, device_id_type=MESH).start()


def _comm_finish(kind, ins, outs, send_sems, recv_sems, local_sems):
    if kind == "exchange":
        local, sends, recvs = _direct_copies(ins, outs, send_sems, recv_sems, local_sems)
        for cp in recvs:
            cp.wait_recv()
        for cp in sends:
            cp.wait_send()
        for cp in local:
            cp.wait()
        return
    x, y, c = lax.axis_index("x"), lax.axis_index("y"), lax.axis_index("c")
    sibling = (x, y, 1 - c)
    chips = [(1 - x, y), (x, 1 - y), (1 - x, 1 - y)]
    for a in range(len(ins)):
        def copy(k, block, to):
            rows = outs[a].at[4 * block[0] + 2 * block[1] + block[2]]
            return pltpu.make_async_remote_copy(src_ref=rows, dst_ref=rows, send_sem=send_sems.at[a, k],
                                                recv_sem=recv_sems.at[a, k], device_id=to, device_id_type=MESH)

        passed = []
        for j, chip in enumerate(chips):
            copy(1 + j, (*chip, c), (x, y, c)).wait_recv()
            passed.append(copy(4 + j, (*chip, c), sibling))
            passed[-1].start()
        copy(0, sibling, (x, y, c)).wait_recv()
        for j, chip in enumerate(chips):
            copy(4 + j, (*chip, 1 - c), (x, y, c)).wait_recv()
        for k in range(4):
            copy(k, (x, y, c), sibling).wait_send()
        for cp in passed:
            cp.wait_send()
        pltpu.make_async_copy(ins[a], outs[a].at[4 * x + 2 * y + c], local_sems.at[a]).wait()


def _comm_shapes(kind, arrs):
    out_shape = [jax.ShapeDtypeStruct(((NDEV,) if kind == "gather" else ()) + a.shape, a.dtype) for a in arrs]
    n = len(arrs)
    sems = [pltpu.SemaphoreType.DMA((n, 7)), pltpu.SemaphoreType.DMA((n, 7)), pltpu.SemaphoreType.DMA((n,))]
    return out_shape, sems


def _comm_alone(kind, arrs, name):
    n = len(arrs)

    def body(*refs):
        _comm_start(kind, refs[:n], refs[n:2 * n], *refs[2 * n:])
        _comm_finish(kind, refs[:n], refs[n:2 * n], *refs[2 * n:])

    out_shape, sems = _comm_shapes(kind, arrs)
    return pl.pallas_call(body, name=name, out_shape=out_shape, in_specs=[ANY] * n, out_specs=[ANY] * n,
                          scratch_shapes=sems)(*arrs)


def _pcall(body, *, name, grid, in_specs, out_specs, out_shape, operands, sem, scratch_shapes=(), comm=None):
    in_specs, out_specs, out_shape = list(in_specs), list(out_specs), list(out_shape)
    scratch_shapes = list(scratch_shapes)
    if comm is None:
        res = pl.pallas_call(body, name=name, grid=grid, in_specs=in_specs, out_specs=out_specs, out_shape=out_shape,
                             scratch_shapes=scratch_shapes, compiler_params=_cp(*sem))(*operands)
        return list(res), []
    kind, arrs = comm
    nc, n_in, n_out, n_scr = len(arrs), len(operands), len(out_shape), len(scratch_shapes)
    c_shape, c_sems = _comm_shapes(kind, arrs)

    def carrier(*refs):
        ins, cin = refs[:n_in], refs[n_in:n_in + nc]
        outs = refs[n_in + nc:n_in + nc + n_out]
        cout = refs[n_in + nc + n_out:n_in + 2 * nc + n_out]
        scr = refs[n_in + 2 * nc + n_out:n_in + 2 * nc + n_out + n_scr]
        sems = refs[n_in + 2 * nc + n_out + n_scr:]
        pids = [pl.program_id(d) for d in range(len(grid))]
        first = functools.reduce(jnp.logical_and, [p == 0 for p in pids])
        last = functools.reduce(jnp.logical_and, [p == g - 1 for p, g in zip(pids, grid)])

        @pl.when(first)
        def _():
            _comm_start(kind, cin, cout, *sems)

        body(*ins, *outs, *scr)

        @pl.when(last)
        def _():
            _comm_finish(kind, cin, cout, *sems)

    res = pl.pallas_call(
        carrier, name=name, grid=grid, in_specs=in_specs + [ANY] * nc, out_specs=out_specs + [ANY] * nc,
        out_shape=out_shape + c_shape, scratch_shapes=scratch_shapes + c_sems,
        compiler_params=_cp(*["arbitrary"] * len(grid)))(*operands, *arrs)
    return list(res[:n_out]), list(res[n_out:])


def _mm_call(pairs, grid, a_spec, b_spec, o_spec, out_shape, acc_shape, nk, ca, cb, name,
             res=None, res_spec=None, comm=None):
    npairs = len(pairs)
    multi = isinstance(out_shape, (list, tuple))
    nout = len(out_shape) if multi else 1

    def body(*refs):
        ab = refs[:2 * npairs]
        r_ref = refs[2 * npairs] if res is not None else None
        o_refs, acc = refs[-1 - nout:-1], refs[-1]
        k = pl.program_id(2)

        @pl.when(k == 0)
        def _():
            acc[...] = jnp.zeros_like(acc)

        tot = None
        for p in range(npairs):
            av, bv = ab[2 * p][...].astype(BF16), ab[2 * p + 1][...].astype(BF16)
            if av.ndim == 2:
                terms = [(av, bv)]
            else:
                rows = bv.shape[0] // av.shape[0]
                terms = [(av[q], bv[q] if bv.ndim == 3 else bv[q * rows:(q + 1) * rows]) for q in range(av.shape[0])]
            for at, bt in terms:
                d = _dot(at, bt, ca, cb)
                tot = d if tot is None else tot + d
        acc[...] += tot

        @pl.when(k == nk - 1)
        def _():
            r = acc[...]
            if r_ref is not None:
                r = r + r_ref[...]
            for o_ref in o_refs:
                o_ref[...] = r.astype(o_ref.dtype)

    ops, specs = [], []
    for a, b in pairs:
        ops += [a, b]
        specs += [a_spec, b_spec]
    if res is not None:
        ops.append(res)
        specs.append(res_spec)
    outs, moved = _pcall(
        body, name=name, grid=grid, in_specs=specs, out_specs=[o_spec] * nout,
        out_shape=out_shape if multi else [out_shape], operands=ops,
        scratch_shapes=[pltpu.VMEM(acc_shape, F32)], sem=("parallel", "parallel", "arbitrary"), comm=comm)
    outs = outs if multi else outs[0]
    return outs if comm is None else (outs, moved)


def _k_tile(K, row_bytes, tk=2048):
    tk = _tile(K, tk)
    while 2 * tk * row_bytes > MM_OPERAND_BYTES and tk % 256 == 0:
        tk //= 2
    return tk


def _mm(a, b, name, ta=False, tb=False, out_dtype=F32, res=None, tm=1024, tn=1024, comm=None):
    M, K = (a.shape[1], a.shape[0]) if ta else a.shape
    N = b.shape[0] if tb else b.shape[1]
    tm, tn = _tile(M, tm), _tile(N, tn)
    tk = _k_tile(K, tm * a.dtype.itemsize + tn * b.dtype.itemsize)
    a_spec = pl.BlockSpec((tk, tm), lambda i, j, k: (k, i)) if ta else pl.BlockSpec((tm, tk), lambda i, j, k: (i, k))
    b_spec = pl.BlockSpec((tn, tk), lambda i, j, k: (j, k)) if tb else pl.BlockSpec((tk, tn), lambda i, j, k: (k, j))
    o_spec = pl.BlockSpec((tm, tn), lambda i, j, k: (i, j))
    if isinstance(out_dtype, tuple):
        out_shape = [jax.ShapeDtypeStruct((M, N), d) for d in out_dtype]
    else:
        out_shape = jax.ShapeDtypeStruct((M, N), out_dtype)
    return _mm_call([(a, b)], (M // tm, N // tn, K // tk), a_spec, b_spec, o_spec,
                    out_shape, (tm, tn), K // tk,
                    0 if ta else 1, 1 if tb else 0, name, res=res, res_spec=o_spec, comm=comm)


def _mm_down(act, wd, res, name, tm=1024, tn=1024):
    _, S, FB = act.shape
    D = wd.shape[1]
    tm, tn = _tile(S, tm), _tile(D, tn)
    o_spec = pl.BlockSpec((tm, tn), lambda i, j, k: (i, j))
    nb = FFN_BLOCKS_PER_STEP
    return _mm_call([(act, wd)], (S // tm, D // tn, NDEV // nb),
                    pl.BlockSpec((nb, tm, FB), lambda i, j, k: (k, i, 0)),
                    pl.BlockSpec((nb * FB, tn), lambda i, j, k: (k, j)), o_spec,
                    jax.ShapeDtypeStruct((S, D), F32), (tm, tn), NDEV // nb, 1, 0, name, res=res, res_spec=o_spec)


def _mm_dwdown(act, dy, name, tn=1024):
    _, S, FB = act.shape
    D = dy.shape[1]
    tn = _tile(D, tn)
    tk = _k_tile(S, FB * act.dtype.itemsize + tn * dy.dtype.itemsize)
    return _mm_call([(act, dy)], (NDEV, D // tn, S // tk),
                    pl.BlockSpec((None, tk, FB), lambda i, j, k: (i, k, 0)),
                    pl.BlockSpec((tk, tn), lambda i, j, k: (k, j)),
                    pl.BlockSpec((FB, tn), lambda i, j, k: (i, j)),
                    jax.ShapeDtypeStruct((NDEV * FB, D), BF16), (FB, tn), S // tk, 0, 0, name)


def _mm_dh2(dg, wg, du, wu, name, tm=1024, tn=1024, comm=None):
    _, S, FB = dg.shape
    D = wg.shape[1]
    tm, tn = _tile(S, tm), _tile(D, tn)
    nb = FFN_BLOCKS_PER_STEP
    return _mm_call([(dg, wg), (du, wu)], (S // tm, D // tn, NDEV // nb),
                    pl.BlockSpec((nb, tm, FB), lambda i, j, k: (k, i, 0)),
                    pl.BlockSpec((nb, tn, FB), lambda i, j, k: (k, j, 0)),
                    pl.BlockSpec((tm, tn), lambda i, j, k: (i, j)),
                    jax.ShapeDtypeStruct((S, D), F32), (tm, tn), NDEV // nb, 1, 1, name, comm=comm)


def _mm_dwgate(h2, dg, name, tm=1024, comm=None):
    _, S, FB = dg.shape
    D = h2.shape[1]
    tm = _tile(D, tm)
    tk = _k_tile(S, tm * h2.dtype.itemsize + FB * dg.dtype.itemsize)
    return _mm_call([(h2, dg)], (NDEV, D // tm, S // tk),
                    pl.BlockSpec((tk, tm), lambda p, i, k: (k, i)),
                    pl.BlockSpec((None, tk, FB), lambda p, i, k: (p, k, 0)),
                    pl.BlockSpec((None, tm, FB), lambda p, i, k: (p, i, 0)),
                    jax.ShapeDtypeStruct((NDEV, D, FB), BF16), (tm, FB), S // tk, 0, 0, name, comm=comm)


def _ffn_up(h2, wg, wu, name, tm=512, comm=None):
    S, D = h2.shape
    FB = wg.shape[2]
    tm = _tile(S, tm)

    def body(h_ref, wg_ref, wu_ref, g_ref, u_ref, act_ref):
        h = h_ref[...]
        g = _dot_nn(h, wg_ref[...])
        u = _dot_nn(h, wu_ref[...])
        g_ref[...] = g.astype(BF16)
        u_ref[...] = u.astype(BF16)
        act_ref[...] = (g / (1.0 + jnp.exp(-g)) * u).astype(BF16)

    w_spec = pl.BlockSpec((None, D, FB), lambda p, i: (p, 0, 0))
    o_spec = pl.BlockSpec((None, tm, FB), lambda p, i: (p, i, 0))
    shp = (NDEV, S, FB)
    outs, moved = _pcall(
        body, name=name, grid=(NDEV, S // tm),
        in_specs=[pl.BlockSpec((tm, D), lambda p, i: (i, 0)), w_spec, w_spec],
        out_specs=[o_spec, o_spec, o_spec],
        out_shape=[jax.ShapeDtypeStruct(shp, BF16)] * 3,
        operands=[h2, wg, wu], sem=("parallel", "parallel"), comm=comm)
    return outs if comm is None else (outs, moved)


def _ffn_dact(dy, wd, g, u, name, tm=512):
    S, D = dy.shape
    FB = g.shape[2]
    tm = _tile(S, tm)

    def body(dy_ref, wd_ref, g_ref, u_ref, dg_ref, du_ref):
        dact = _dot_nt(dy_ref[...], wd_ref[...])
        gv = g_ref[...].astype(F32)
        sg = pl.reciprocal(1.0 + jnp.exp(-gv), approx=True)
        dg_ref[...] = (dact * u_ref[...].astype(F32) * (sg * (1.0 + gv * (1.0 - sg)))).astype(BF16)
        du_ref[...] = (dact * (gv * sg)).astype(BF16)

    t_spec = pl.BlockSpec((None, tm, FB), lambda p, i: (p, i, 0))
    shp = jax.ShapeDtypeStruct((NDEV, S, FB), BF16)
    return pl.pallas_call(
        body, name=name, grid=(NDEV, S // tm),
        in_specs=[pl.BlockSpec((tm, D), lambda p, i: (i, 0)), pl.BlockSpec((FB, D), lambda p, i: (p, 0)),
                  t_spec, t_spec],
        out_specs=[t_spec, t_spec], out_shape=[shp, shp],
        compiler_params=_cp("parallel", "parallel"),
    )(dy, wd, g, u)


def _rms_fwd(x, gain, width, cb, out_dtype, name, ts=512):
    S = x.shape[0]
    ts = _tile(S, ts)

    def body(x_ref, g_ref, o_ref):
        xv = x_ref[...]
        r = lax.rsqrt(jnp.mean(xv * xv, axis=1, keepdims=True) + EPS)
        o_ref[...] = (xv * r * g_ref[...]).astype(o_ref.dtype)

    return pl.pallas_call(
        body, name=name, grid=(S // ts,),
        in_specs=[pl.BlockSpec((ts, width), lambda i: (i, cb)), pl.BlockSpec((1, width), lambda i: (0, 0))],
        out_specs=pl.BlockSpec((ts, width), lambda i: (i, 0)),
        out_shape=jax.ShapeDtypeStruct((S, width), out_dtype),
        compiler_params=_cp("parallel"),
    )(x, gain)


def _rms_bwd(x, gain, dy, width, cb, name, res=None, ts=256):
    S = x.shape[0]
    ts = _tile(S, ts)
    has_res = res is not None

    def body(*refs):
        x_ref, g_ref, dy_ref = refs[:3]
        r_ref = refs[3] if has_res else None
        dx_ref, dxb_ref, dg_ref = refs[-3], refs[-2], refs[-1]

        @pl.when(pl.program_id(0) == 0)
        def _():
            dg_ref[...] = jnp.zeros_like(dg_ref)

        xv = x_ref[...]
        r = lax.rsqrt(jnp.mean(xv * xv, axis=1, keepdims=True) + EPS)
        xh = xv * r
        dyv = dy_ref[...]
        dyg = dyv * g_ref[...]
        dx = r * (dyg - xh * jnp.mean(dyg * xh, axis=1, keepdims=True))
        if has_res:
            dx = dx + r_ref[...]
        dx_ref[...] = dx
        dxb_ref[...] = dx.astype(BF16)
        dg_ref[...] += jnp.sum(dyv * xh, axis=0, keepdims=True)

    row = pl.BlockSpec((ts, width), lambda i: (i, 0))
    vec = pl.BlockSpec((1, width), lambda i: (0, 0))
    ops = [x, gain, dy] + ([res] if has_res else [])
    specs = [pl.BlockSpec((ts, width), lambda i: (i, cb)), vec, row] + ([row] if has_res else [])
    return pl.pallas_call(
        body, name=name, grid=(S // ts,), in_specs=specs, out_specs=[row, row, vec],
        out_shape=[jax.ShapeDtypeStruct((S, width), F32), jax.ShapeDtypeStruct((S, width), BF16),
                   jax.ShapeDtypeStruct((1, width), F32)],
        compiler_params=_cp("arbitrary"),
    )(*ops)


def _gn_fwd(outs, gain, name, ts=512):
    S, GW = outs[0].shape
    ts = _tile(S, ts)

    def body(a_ref, b_ref, c_ref, d_ref, g_ref, o_ref):
        for g, r_ref in enumerate((a_ref, b_ref, c_ref, d_ref)):
            xv = r_ref[...]
            r = lax.rsqrt(jnp.mean(xv * xv, axis=1, keepdims=True) + EPS)
            o_ref[:, g * GW:(g + 1) * GW] = (xv * r * g_ref[:, g * GW:(g + 1) * GW]).astype(BF16)

    row = pl.BlockSpec((ts, GW), lambda i: (i, 0))
    return pl.pallas_call(
        body, name=name, grid=(S // ts,),
        in_specs=[row] * 4 + [pl.BlockSpec((1, 4 * GW), lambda i: (0, 0))],
        out_specs=pl.BlockSpec((ts, 4 * GW), lambda i: (i, 0)),
        out_shape=jax.ShapeDtypeStruct((S, 4 * GW), BF16),
        compiler_params=_cp("parallel"),
    )(*outs, gain)


def _gn_bwd(outs, gain, dmix, name, ts=256):
    S, GW = outs[0].shape
    ts = _tile(S, ts)

    def body(a_ref, b_ref, c_ref, d_ref, g_ref, dm_ref, da_ref, db_ref, dc_ref, dd_ref, dg_ref):
        @pl.when(pl.program_id(0) == 0)
        def _():
            dg_ref[...] = jnp.zeros_like(dg_ref)

        for g, (r_ref, o_ref) in enumerate(zip((a_ref, b_ref, c_ref, d_ref), (da_ref, db_ref, dc_ref, dd_ref))):
            sl = slice(g * GW, (g + 1) * GW)
            xv = r_ref[...]
            r = lax.rsqrt(jnp.mean(xv * xv, axis=1, keepdims=True) + EPS)
            xh = xv * r
            dyv = dm_ref[:, sl]
            dyg = dyv * g_ref[:, sl]
            o_ref[...] = (r * (dyg - xh * jnp.mean(dyg * xh, axis=1, keepdims=True))).astype(BF16)
            dg_ref[:, sl] += jnp.sum(dyv * xh, axis=0, keepdims=True)

    row = pl.BlockSpec((ts, GW), lambda i: (i, 0))
    vec = pl.BlockSpec((1, 4 * GW), lambda i: (0, 0))
    return pl.pallas_call(
        body, name=name, grid=(S // ts,),
        in_specs=[row] * 4 + [vec, pl.BlockSpec((ts, 4 * GW), lambda i: (i, 0))],
        out_specs=[row] * 4 + [vec],
        out_shape=[jax.ShapeDtypeStruct((S, GW), BF16)] * 4 + [jax.ShapeDtypeStruct((1, 4 * GW), F32)],
        compiler_params=_cp("arbitrary"),
    )(*outs, gain, dmix)


def _rope_partner(x, half):
    if not half:
        return pltpu.roll(x, 64, 1)
    lane = lax.broadcasted_iota(jnp.int32, x.shape, 1)
    return jnp.where(lane % 64 < 32, pltpu.roll(x, LANE - 32, 1), pltpu.roll(x, 32, 1))


def _rope(x, cbs, cos, sin, half, out_dtype, name, ts=512):
    S = x.shape[0]
    cb0, nb, stride = cbs
    ts = _tile(S, ts)

    def body(x_ref, c_ref, s_ref, o_ref):
        xv = x_ref[...].astype(F32)
        o_ref[...] = (xv * c_ref[...] + _rope_partner(xv, half) * s_ref[...]).astype(o_ref.dtype)

    tab = pl.BlockSpec((ts, LANE), lambda i, j: (i, 0))
    return pl.pallas_call(
        body, name=name, grid=(S // ts, nb),
        in_specs=[pl.BlockSpec((ts, LANE), lambda i, j: (i, cb0 + stride * j)), tab, tab],
        out_specs=pl.BlockSpec((ts, LANE), lambda i, j: (i, j)),
        out_shape=jax.ShapeDtypeStruct((S, nb * LANE), out_dtype),
        compiler_params=_cp("parallel", "parallel"),
    )(x, cos, sin)


def _final_loss(x, gain, target, name, ts=256):
    S, D = x.shape
    ts = _tile(S, ts)

    def body(x_ref, g_ref, t_ref, dy_ref, l_ref):
        @pl.when(pl.program_id(0) == 0)
        def _():
            l_ref[...] = jnp.zeros_like(l_ref)

        xv = x_ref[...]
        r = lax.rsqrt(jnp.mean(xv * xv, axis=1, keepdims=True) + EPS)
        err = xv * r * g_ref[...] - t_ref[...]
        dy_ref[...] = err * (1.0 / D)
        part = jnp.sum(jnp.mean(err * err, axis=1, keepdims=True), axis=0, keepdims=True)
        l_ref[...] += jnp.broadcast_to(0.5 * part, (1, LANE))

    row = pl.BlockSpec((ts, D), lambda i: (i, 0))
    return pl.pallas_call(
        body, name=name, grid=(S // ts,),
        in_specs=[row, pl.BlockSpec((1, D), lambda i: (0, 0)), row],
        out_specs=[row, pl.BlockSpec((1, LANE), lambda i: (0, 0))],
        out_shape=[jax.ShapeDtypeStruct((S, D), F32), jax.ShapeDtypeStruct((1, LANE), F32)],
        compiler_params=_cp("arbitrary"),
    )(x, gain, target)


def _colspec(rows, f):
    return pl.BlockSpec((rows, LANE), f)


def _soft_tiles(S):
    tq = _tile(S, TQ)
    tk = _tile(S, TKS)
    assert tk % tq == 0
    return tq, tk


def _key_row(crow_ref, j, tk):
    n = tk // TK
    return jnp.concatenate([crow_ref[j * n + c] for c in range(n)], axis=1)


def _attn_fwd(S, H, q1, q1cb, k1, k1cb, v, vcb, scale, name, q2=None, q2cb=None, k2=None, k2cb=None,
              tab=None, win=None, ccol=None, crow=None, comm=None):
    tq, tk = _soft_tiles(S)
    has2, hastab, hasc = q2 is not None, tab is not None, ccol is not None

    def body(*refs):
        it = iter(refs)
        q1r, k1r, vr = next(it), next(it), next(it)
        q2r, k2r = (next(it), next(it)) if has2 else (None, None)
        tabr = next(it) if hastab else None
        ccolr, crowr = (next(it), next(it)) if hasc else (None, None)
        o_ref, lse_ref = next(it), next(it)
        i = pl.program_id(1)
        q = q1r[...]
        qb2 = q2r[...] if has2 else None
        cq = ccolr[:, 0:1] if hasc else None
        qpos = i * tq + lax.broadcasted_iota(jnp.int32, (tq, tk), 0)
        kio = lax.broadcasted_iota(jnp.int32, (tq, tk), 1)
        j_diag = (i * tq) // tk
        j_lo = jnp.maximum((i * tq - win) // tk, 0) if win else 0

        if has2:
            q = jnp.concatenate([q, qb2], axis=1)

        def step(j, carry, masked):
            m, l, acc = carry
            off = pl.multiple_of(j * tk, tk)
            kb = k1r[pl.ds(off, tk), :]
            if has2:
                kb = jnp.concatenate([kb, k2r[pl.ds(off, tk), :]], axis=1)
            s = _dot_nt(q, kb) * scale
            if hastab:
                s = s + tabr[i - j * (tk // tq)]
            else:
                if hasc:
                    s = s + (cq - _key_row(crowr, j, tk))
                if masked:
                    s = jnp.where(kio + j * tk <= qpos, s, NEG)
            mn = jnp.maximum(m, jnp.max(s, axis=1, keepdims=True))
            p = jnp.exp(s - mn)
            al = jnp.exp(m - mn)
            l = al * l + jnp.sum(p, axis=1, keepdims=True)
            acc = al * acc + _dot_nn(p.astype(BF16), vr[pl.ds(off, tk), :])
            return mn, l, acc

        carry = (jnp.full((tq, 1), NEG, F32), jnp.zeros((tq, 1), F32), jnp.zeros((tq, LANE), F32))
        if hastab:
            carry = lax.fori_loop(j_lo, j_diag + 1, functools.partial(step, masked=False), carry)
        else:
            carry = lax.fori_loop(j_lo, j_diag, functools.partial(step, masked=False), carry)
            carry = step(j_diag, carry, True)
        m, l, acc = carry
        o_ref[...] = acc / l
        lse_ref[...] = jnp.broadcast_to(m + jnp.log(l), (tq, LANE))

    ops = [q1, k1, v]
    specs = [_colspec(tq, lambda h, i: (i, q1cb(h))), _colspec(S, lambda h, i: (0, k1cb(h))),
             _colspec(S, lambda h, i: (0, vcb(h)))]
    if has2:
        ops += [q2, k2]
        specs += [_colspec(tq, lambda h, i: (i, q2cb(h))), _colspec(S, lambda h, i: (0, k2cb(h)))]
    if hastab:
        ops.append(tab)
        specs.append(pl.BlockSpec(tab.shape, lambda h, i: (0, 0, 0)))
    if hasc:
        ops += [ccol, crow]
        specs += [_colspec(tq, lambda h, i: (i, h)),
                  pl.BlockSpec((None, S // TK, 1, TK), lambda h, i: (h, 0, 0, 0))]
    o_spec = _colspec(tq, lambda h, i: (i, h))
    shp = jax.ShapeDtypeStruct((S, H * LANE), F32)
    outs, moved = _pcall(
        body, name=name, grid=(H, S // tq), in_specs=specs, out_specs=[o_spec, o_spec], out_shape=[shp, shp],
        operands=ops, sem=("parallel", "arbitrary"), comm=comm)
    return outs if comm is None else (outs, moved)


def _attn_bwd(S, H, q1, q1cb, k1, k1cb, v, vcb, o, do, lse, scale, name, q2=None, q2cb=None, k2=None, k2cb=None,
              rope2=None, tab=None, win=None, ccol=None, crow=None, comm=None):
    tq, tk = _soft_tiles(S)
    has2, hastab, hasc = q2 is not None, tab is not None, ccol is not None

    def body(*refs):
        it = iter(refs)
        q1r, k1r, vr, o_r, do_r, lse_r = (next(it) for _ in range(6))
        q2r, k2r, cos2_r, sin2_r = (next(it), next(it), next(it), next(it)) if has2 else (None,) * 4
        tabr = next(it) if hastab else None
        ccolr, crowr = (next(it), next(it)) if hasc else (None, None)
        dq1_r, dk1_o, dv_o = next(it), next(it), next(it)
        dk2_o = dv_o
        dcr_r = next(it) if hasc else None
        dk1_r, dv_r = next(it), next(it)
        dk2_r = next(it) if has2 else None
        i = pl.program_id(1)

        @pl.when(i == 0)
        def _():
            dk1_r[...] = jnp.zeros_like(dk1_r)
            dv_r[...] = jnp.zeros_like(dv_r)
            if has2:
                dk2_r[...] = jnp.zeros_like(dk2_r)
            if hasc:
                dcr_r[...] = jnp.zeros_like(dcr_r)

        q = q1r[...]
        qb2 = q2r[...] if has2 else None
        dob = do_r[...]
        delta = jnp.sum(dob.astype(F32) * o_r[...], axis=1, keepdims=True)
        lse_c = lse_r[:, 0:1]
        cq = ccolr[:, 0:1] if hasc else None
        qpos = i * tq + lax.broadcasted_iota(jnp.int32, (tq, tk), 0)
        kio = lax.broadcasted_iota(jnp.int32, (tq, tk), 1)
        j_diag = (i * tq) // tk
        j_lo = jnp.maximum((i * tq - win) // tk, 0) if win else 0

        if has2:
            q = jnp.concatenate([q, qb2], axis=1)

        def probs(j, masked):
            off = pl.multiple_of(j * tk, tk)
            kb = k1r[pl.ds(off, tk), :]
            if has2:
                kb = jnp.concatenate([kb, k2r[pl.ds(off, tk), :]], axis=1)
            s = _dot_nt(q, kb) * scale
            if hastab:
                s = s + tabr[i - j * (tk // tq)]
            else:
                if hasc:
                    s = s + (cq - _key_row(crowr, j, tk))
                if masked:
                    s = jnp.where(kio + j * tk <= qpos, s, NEG)
            p = jnp.exp(s - lse_c)
            dp = _dot_nt(dob, vr[pl.ds(off, tk), :])
            return off, kb, p, dp

        def sweep(fn, carry):
            if hastab:
                return lax.fori_loop(j_lo, j_diag + 1, functools.partial(fn, masked=False), carry)
            carry = lax.fori_loop(j_lo, j_diag, functools.partial(fn, masked=False), carry)
            return fn(j_diag, carry, True)

        if hasc:
            def dstep(j, acc, masked):
                _, _, p, dp = probs(j, masked)
                return acc + jnp.sum(p * dp, axis=1, keepdims=True)

            delta = sweep(dstep, jnp.zeros((tq, 1), F32))

        def step(j, dq, masked):
            off, kb, p, dp = probs(j, masked)
            ds = p * (dp - delta)
            dsb = ds.astype(BF16)
            dk = _dot_tn(dsb, q) * scale
            dk1_r[pl.ds(off, tk), :] += dk[:, :LANE]
            if has2:
                dk2_r[pl.ds(off, tk), :] += dk[:, LANE:]
            dv_r[pl.ds(off, tk), :] += _dot_tn(p.astype(BF16), dob)
            if hasc:
                cs = -jnp.sum(ds, axis=0, keepdims=True)
                for c in range(tk // TK):
                    dcr_r[j * (tk // TK) + c] += cs[:, c * TK:(c + 1) * TK]
            return dq + _dot_nn(dsb, kb)

        dq = sweep(step, jnp.zeros(q.shape, F32)) * scale
        dq1_r[:, :LANE] = dq[:, :LANE].astype(BF16)
        if has2:
            x2 = dq[:, LANE:]
            dq1_r[:, LANE:] = (x2 * cos2_r[...] + _rope_partner(x2, True) * sin2_r[...]).astype(BF16)

        @pl.when(i == S // tq - 1)
        def _():
            dk1_o[:, :LANE] = dk1_r[...].astype(BF16)
            if has2:
                dk1_o[:, LANE:] = dv_r[...].astype(BF16)
                dk2_o[...] = dk2_r[...].astype(BF16)
            else:
                dv_o[...] = dv_r[...].astype(BF16)

    qspec = _colspec(tq, lambda h, i: (i, h))
    kspec = _colspec(S, lambda h, i: (0, h))
    ops = [q1, k1, v, o, do, lse]
    specs = [_colspec(tq, lambda h, i: (i, q1cb(h))), _colspec(S, lambda h, i: (0, k1cb(h))),
             _colspec(S, lambda h, i: (0, vcb(h))), qspec, qspec, qspec]
    if has2:
        ops += [q2, k2, *rope2]
        tab2 = _colspec(tq, lambda h, i: (i, 0))
        specs += [_colspec(tq, lambda h, i: (i, q2cb(h))), _colspec(S, lambda h, i: (0, k2cb(h))), tab2, tab2]
    if hastab:
        ops.append(tab)
        specs.append(pl.BlockSpec(tab.shape, lambda h, i: (0, 0, 0)))
    if hasc:
        ops += [ccol, crow]
        specs += [qspec, pl.BlockSpec((None, S // TK, 1, TK), lambda h, i: (h, 0, 0, 0))]
    assert all(t.dtype == BF16 for t in ops[:3] + [do] + ([q2, k2] if has2 else []))
    shp = jax.ShapeDtypeStruct((S, H * LANE), BF16)
    if has2:
        out_specs = [pl.BlockSpec((tq, 2 * LANE), lambda h, i: (i, h)), pl.BlockSpec((S, 2 * LANE), lambda h, i: (0, h)),
                     kspec]
        wide = jax.ShapeDtypeStruct((S, H * 2 * LANE), BF16)
        out_shape = [wide, wide, shp]
    else:
        out_specs = [qspec, kspec, kspec]
        out_shape = [shp] * 3
    if hasc:
        out_specs.append(pl.BlockSpec((None, S // TK, 1, TK), lambda h, i: (h, 0, 0, 0)))
        out_shape.append(jax.ShapeDtypeStruct((H, S // TK, 1, TK), F32))
    outs, moved = _pcall(
        body, name=name, grid=(H, S // tq), in_specs=specs, out_specs=out_specs, out_shape=out_shape,
        operands=ops, scratch_shapes=[pltpu.VMEM((S, LANE), F32)] * (3 if has2 else 2),
        sem=("parallel", "arbitrary"), comm=comm)
    return outs if comm is None else (outs, moved)


def _scan_matrix(kind):
    j = np.arange(TK)[:, None]
    s = np.arange(TK)[None, :]
    tri = {"suffix_ex": j > s, "prefix_in": j <= s, "prefix_ex": j < s}[kind].astype(np.float32)
    half = np.concatenate([tri, np.ones((TK, TK), np.float32)], axis=1)
    return jnp.asarray(np.concatenate([half, half], axis=0), BF16)


def _scan_mxu(x, mat, carry, reverse, split=True):
    n = x.shape[1] // TK
    hi = x.astype(BF16)
    if split:
        lo = (x - hi.astype(F32)).astype(BF16)
    else:
        mat = mat[:TK]
    parts = [None] * n
    for b in (reversed(range(n)) if reverse else range(n)):
        sl = slice(b * TK, (b + 1) * TK)
        r = _dot_nn(jnp.concatenate([hi[:, sl], lo[:, sl]], axis=1) if split else hi[:, sl], mat)
        parts[b] = r[:, :TK] + carry
        carry = carry + r[:, TK:]
    return jnp.concatenate(parts, axis=1), carry


def _stick_logs(z):
    e = jnp.exp(-jnp.abs(z))
    return e, -jnp.maximum(z, 0.0) - jnp.log(1.0 + e)


def _stick_fwd(S, H, x, qcb, kcb, vcb, scale, name, comm=None):
    tq, tk = _soft_tiles(S)
    assert x.dtype == BF16

    def body(q_r, k_r, v_r, mat_r, o_ref, t_ref):
        i = pl.program_id(1)
        q = q_r[...]
        qpos = i * tq + lax.broadcasted_iota(jnp.int32, (tq, tk), 0)
        lane = lax.broadcasted_iota(jnp.int32, (tq, tk), 1)
        j_diag = (i * tq) // tk

        def step(j, carry, masked):
            c, acc = carry
            off = pl.multiple_of(j * tk, tk)
            z = _dot_nt(q, k_r[pl.ds(off, tk), :]) * scale
            _, lk = _stick_logs(z)
            if masked:
                past = lane + j * tk < qpos
                lk = jnp.where(past, lk, 0.0)
            suf, c = _scan_mxu(lk, mat_r[...], c, True)
            a = jnp.exp(z + lk + suf)
            if masked:
                a = jnp.where(past, a, 0.0)
            acc = acc + _dot_nn(a.astype(BF16), v_r[pl.ds(off, tk), :])
            return c, acc

        carry = step(j_diag, (jnp.zeros((tq, TK), F32), jnp.zeros((tq, LANE), F32)), True)
        c, acc = lax.fori_loop(0, j_diag, lambda jj, cr: step(j_diag - 1 - jj, cr, False), carry)
        o_ref[...] = acc
        t_ref[...] = c

    o_spec = _colspec(tq, lambda h, i: (i, h))
    shp = jax.ShapeDtypeStruct((S, H * LANE), F32)
    mat = _scan_matrix("suffix_ex")
    outs, moved = _pcall(
        body, name=name, grid=(H, S // tq),
        in_specs=[_colspec(tq, lambda h, i: (i, qcb(h))), _colspec(S, lambda h, i: (0, kcb(h))),
                  _colspec(S, lambda h, i: (0, vcb(h))), pl.BlockSpec(mat.shape, lambda h, i: (0, 0))],
        out_specs=[o_spec, o_spec], out_shape=[shp, shp],
        operands=[x, x, x, mat], sem=("parallel", "arbitrary"), comm=comm)
    return outs if comm is None else (outs, moved)


def _stick_bwd(S, H, x, qcb, kcb, vcb, do, tot, scale, name, comm=None):
    tq, tk = _soft_tiles(S)
    assert x.dtype == BF16 and do.dtype == BF16

    def body(q_r, k_r, v_r, do_r, t_r, pin_r, pex_r, dq_r, dk_o, dv_o, dk_r, dv_r):
        i = pl.program_id(1)

        @pl.when(i == 0)
        def _():
            dk_r[...] = jnp.zeros_like(dk_r)
            dv_r[...] = jnp.zeros_like(dv_r)

        q = q_r[...]
        dob = do_r[...]
        total = jnp.concatenate([t_r[...]] * (tk // TK), axis=1)
        qpos = i * tq + lax.broadcasted_iota(jnp.int32, (tq, tk), 0)
        lane = lax.broadcasted_iota(jnp.int32, (tq, tk), 1)
        j_diag = (i * tq) // tk

        def step(j, carry, masked):
            cl, cg, dq = carry
            off = pl.multiple_of(j * tk, tk)
            kb = k_r[pl.ds(off, tk), :]
            z = _dot_nt(q, kb) * scale
            e, lk = _stick_logs(z)
            if masked:
                past = lane + j * tk < qpos
                lk = jnp.where(past, lk, 0.0)
            pre, cl = _scan_mxu(lk, pin_r[...], cl, False)
            a = jnp.exp(z + lk + (total - pre))
            if masked:
                a = jnp.where(past, a, 0.0)
            g = _dot_nt(dob, v_r[pl.ds(off, tk), :]) * a
            gpre, cg = _scan_mxu(g, pex_r[...], cg, False, split=False)
            inv = pl.reciprocal(1.0 + e, approx=True)
            small = e * inv
            pos = z >= 0
            dz = g * jnp.where(pos, small, inv) - jnp.where(pos, inv, small) * gpre
            if masked:
                dz = jnp.where(past, dz, 0.0)
            dzb = dz.astype(BF16)
            dk_r[pl.ds(off, tk), :] += _dot_tn(dzb, q) * scale
            dv_r[pl.ds(off, tk), :] += _dot_tn(a.astype(BF16), dob)
            return cl, cg, dq + _dot_nn(dzb, kb)

        zt = jnp.zeros((tq, TK), F32)
        carry = lax.fori_loop(0, j_diag, functools.partial(step, masked=False), (zt, zt, jnp.zeros((tq, LANE), F32)))
        dq_r[...] = (step(j_diag, carry, True)[2] * scale).astype(BF16)

        @pl.when(i == S // tq - 1)
        def _():
            dk_o[...] = dk_r[...].astype(BF16)
            dv_o[...] = dv_r[...].astype(BF16)

    qspec = _colspec(tq, lambda h, i: (i, h))
    kspec = _colspec(S, lambda h, i: (0, h))
    shp = jax.ShapeDtypeStruct((S, H * LANE), BF16)
    pin, pex = _scan_matrix("prefix_in"), _scan_matrix("prefix_ex")
    mspec = pl.BlockSpec(pin.shape, lambda h, i: (0, 0))
    outs, moved = _pcall(
        body, name=name, grid=(H, S // tq),
        in_specs=[_colspec(tq, lambda h, i: (i, qcb(h))), _colspec(S, lambda h, i: (0, kcb(h))),
                  _colspec(S, lambda h, i: (0, vcb(h))), qspec, qspec, mspec, mspec],
        out_specs=[qspec, kspec, kspec], out_shape=[shp] * 3,
        operands=[x, x, x, do, tot, pin, pex], scratch_shapes=[pltpu.VMEM((S, LANE), F32)] * 2,
        sem=("parallel", "arbitrary"), comm=comm)
    return outs if comm is None else (outs, moved)


def _scan8(x, rows, reverse):
    for sh in (1, 2, 4):
        if reverse:
            x = x + jnp.where(rows + sh < 8, pltpu.roll(x, 8 - sh, 0), 0.0)
        else:
            x = x + jnp.where(rows >= sh, pltpu.roll(x, sh, 0), 0.0)
    return x


def _fox_prep(S, H, proj, fcb, bias, name):
    tk = TK

    def body(f_ref, b_ref, ccol_ref, crow_ref, scr):
        rows = lax.broadcasted_iota(jnp.int32, (8, LANE), 0)

        def step(t, carry):
            off = pl.multiple_of(t * 8, 8)
            xb = f_ref[pl.ds(off, 8), :] + b_ref[...]
            lf = jnp.minimum(xb, 0.0) - jnp.log(1.0 + jnp.exp(-jnp.abs(xb)))
            lf = _scan8(lf, rows, False) + carry
            scr[pl.ds(off, 8), :] = lf
            return lf[7:8, :]

        lax.fori_loop(0, S // 8, step, jnp.zeros((1, LANE), F32))
        for h in range(H):
            ccol_ref[:, h * LANE:(h + 1) * LANE] = jnp.broadcast_to(scr[:, h:h + 1], (S, LANE))

            def tr(t, _):
                off = pl.multiple_of(t * tk, tk)
                blk = ccol_ref[pl.ds(off, tk), h * LANE:(h + 1) * LANE]
                crow_ref[h, t] = blk.T[0:1, :]
                return 0

            lax.fori_loop(0, S // tk, tr, 0)

    return pl.pallas_call(
        body, name=name, grid=(1,),
        in_specs=[_colspec(S, lambda i: (0, fcb)), pl.BlockSpec((1, LANE), lambda i: (0, 0))],
        out_specs=[pl.BlockSpec((S, H * LANE), lambda i: (0, 0)),
                   pl.BlockSpec((H, S // tk, 1, tk), lambda i: (0, 0, 0, 0))],
        out_shape=[jax.ShapeDtypeStruct((S, H * LANE), F32), jax.ShapeDtypeStruct((H, S // tk, 1, tk), F32)],
        scratch_shapes=[pltpu.VMEM((S, LANE), F32)],
        compiler_params=_cp("arbitrary"),
    )(proj, bias)


def _fox_bwd(S, H, proj, fcb, bias, dcr, name):
    tk = TK

    def body(f_ref, b_ref, dcr_ref, df_ref, db_ref, scr):
        rows = lax.broadcasted_iota(jnp.int32, (8, LANE), 0)
        lane_t = lax.broadcasted_iota(jnp.int32, (tk, LANE), 1)
        nb = S // 8

        def tr(t, _):
            off = pl.multiple_of(t * tk, tk)
            d = jnp.zeros((tk, LANE), F32)
            for h in range(H):
                d = d + jnp.where(lane_t == h, jnp.broadcast_to(dcr_ref[h, t], (LANE, tk)).T, 0.0)
            scr[pl.ds(off, tk), :] = d
            return 0

        lax.fori_loop(0, S // tk, tr, 0)

        def step(tt, carry):
            suffix, db = carry
            off = pl.multiple_of((nb - 1 - tt) * 8, 8)
            d = _scan8(scr[pl.ds(off, 8), :], rows, True) + suffix
            xb = f_ref[pl.ds(off, 8), :] + b_ref[...]
            e = jnp.exp(-jnp.abs(xb))
            dx = d * jnp.where(xb >= 0, e, 1.0) / (1.0 + e)
            df_ref[pl.ds(off, 8), :] = dx
            return d[0:1, :], db + jnp.sum(dx, axis=0, keepdims=True)

        z = jnp.zeros((1, LANE), F32)
        _, db = lax.fori_loop(0, nb, step, (z, z))
        db_ref[...] = db

    return pl.pallas_call(
        body, name=name, grid=(1,),
        in_specs=[_colspec(S, lambda i: (0, fcb)), pl.BlockSpec((1, LANE), lambda i: (0, 0)),
                  pl.BlockSpec((H, S // tk, 1, tk), lambda i: (0, 0, 0, 0))],
        out_specs=[pl.BlockSpec((S, LANE), lambda i: (0, 0)), pl.BlockSpec((1, LANE), lambda i: (0, 0))],
        out_shape=[jax.ShapeDtypeStruct((S, LANE), F32), jax.ShapeDtypeStruct((1, LANE), F32)],
        scratch_shapes=[pltpu.VMEM((S, LANE), F32)],
        compiler_params=_cp("arbitrary"),
    )(proj, bias, dcr)


def _adamw(w, slots, m, v, name, comm=None):
    R, C = w.shape
    parts = len(slots)
    rows = R // parts
    row_bytes = 2 * C * (parts * NDEV * slots[0].dtype.itemsize + 7 * 4)
    sub = 32 // slots[0].dtype.itemsize
    tiles = [t for t in range(sub, rows + 1, sub) if rows % t == 0] or [rows]
    tr = max([t for t in tiles if t * row_bytes <= ADAMW_BLOCK_BYTES] or tiles[:1])
    per = R // parts // tr
    c1 = 1.0 - ADAM_B1 ** ADAM_STEP
    c2 = 1.0 - ADAM_B2 ** ADAM_STEP

    def body(*refs):
        w_ref, m_ref, v_ref = refs[:3]
        s_refs = refs[3:3 + parts]
        g_ref, d_ref, nm_ref, nv_ref = refs[3 + parts:]
        part = pl.program_id(0) // per
        for a, s_ref in enumerate(s_refs):
            @pl.when(part == a)
            def _():
                g = s_ref[0].astype(F32)
                for s in range(1, NDEV):
                    g = g + s_ref[s].astype(F32)
                g_ref[...] = g

        g = g_ref[...]
        mn = ADAM_B1 * m_ref[...] + (1.0 - ADAM_B1) * g
        vn = ADAM_B2 * v_ref[...] + (1.0 - ADAM_B2) * (g * g)
        nm_ref[...] = mn
        nv_ref[...] = vn
        d_ref[...] = -ADAM_LR * ((mn / c1) / (jnp.sqrt(vn / c2) + ADAM_EPS) + ADAM_WD * w_ref[...])

    row = pl.BlockSpec((tr, C), lambda i: (i, 0))
    s_specs = [pl.BlockSpec((NDEV, tr, C), lambda i, a=a: (0, jnp.clip(i - a * per, 0, per - 1), 0))
               for a in range(parts)]
    outs, moved = _pcall(
        body, name=name, grid=(R // tr,), in_specs=[row, row, row] + s_specs,
        out_specs=[row] * 4, out_shape=[jax.ShapeDtypeStruct((R, C), F32)] * 4,
        operands=[w, m, v, *slots], sem=("arbitrary",), comm=comm)
    return outs if comm is None else (outs, moved)


class _Layout:
    def __init__(self, D):
        self.GW = GW = D // 4
        self.H = H = GW // HEAD
        self.QL, self.KVL = 0, Q_LORA
        base = Q_LORA + KV_LORA
        (self.QB, self.KB, self.VB, self.QC, self.KC, self.VC, self.QD, self.KD, self.VD) = (
            base + k * GW for k in range(9))
        self.KR = base + 9 * GW
        self.FC = self.KR + LANE
        self.PW = -(-(self.FC + LANE) // 512) * 512
        self.o_kr = base
        self.o_bc = base + QK_ROPE
        self.o_fc = self.o_bc + 6 * GW
        self.o_d = self.o_fc + H
        self.IN = self.o_d + 3 * GW

    def pad(self, w):
        z = lambda n: jnp.zeros(w.shape[:-1] + (n,), w.dtype)
        return jnp.concatenate([
            w[..., :self.o_kr], w[..., self.o_bc:self.o_fc], w[..., self.o_d:self.IN],
            w[..., self.o_kr:self.o_bc], z(LANE - QK_ROPE), w[..., self.o_fc:self.o_d], z(LANE - self.H),
            z(self.PW - self.FC - LANE)], axis=-1)

    def unpad(self, g):
        return jnp.concatenate([
            g[..., :self.KR - 9 * self.GW], g[..., self.KR:self.KR + QK_ROPE], g[..., self.QB:self.QD],
            g[..., self.FC:self.FC + self.H], g[..., self.QD:self.KR]], axis=-1)


def _rope_tables(S):
    pos = jnp.arange(S, dtype=F32)

    def cs(dim):
        inv = ROPE_THETA ** (-jnp.arange(0, dim, 2, dtype=F32) / dim)
        ang = pos[:, None] * inv[None, :]
        return jnp.cos(ang), jnp.sin(ang)

    c, s = cs(HEAD)
    full = (jnp.concatenate([c, c], 1), jnp.concatenate([-s, s], 1))
    c, s = cs(QK_ROPE)
    z = jnp.zeros((S, LANE - QK_ROPE), F32)
    half = (jnp.concatenate([c, c, z], 1), jnp.concatenate([-s, s, z], 1))
    return full, half


def _dilated_table(tq, tk):
    win = max(w for w, _ in DILATED_PAIRS)
    nd = (win + tk) // tq + 1
    d = np.arange(nd)[:, None, None] * tq + np.arange(tq)[None, :, None] - np.arange(tk)[None, None, :]
    mult = np.zeros(d.shape, np.float64)
    for w, dil in DILATED_PAIRS:
        mult += (d >= 0) & (d <= w) & (d % dil == 0)
    return jnp.asarray(np.where(mult > 0, np.log(np.maximum(mult, 1.0)), NEG), F32), win


def _pack(arrs):
    rows = []
    for a in arrs:
        f = a.reshape(-1).astype(F32)
        f = jnp.pad(f, (0, (-f.shape[0]) % LANE))
        rows.append(f.reshape(-1, LANE))
    p = jnp.concatenate(rows, 0)
    return jnp.pad(p, ((0, (-p.shape[0]) % 8), (0, 0)))


def _unpack(p, shapes):
    out, r = [], 0
    for shp in shapes:
        n = int(np.prod(shp))
        nr = -(-n // LANE)
        out.append(p[r:r + nr].reshape(-1)[:n].reshape(shp))
        r += nr
    return out


def kernel(x, attn_norm, w_in, mla_q_norm, w_uq, mla_kv_norm, w_ukv, fox_forget_bias, group_norm, w_out, ffn_norm, w_gate, w_up, w_down, final_norm, loss_target, m_attn_norm, m_w_in, m_mla_q_norm, m_w_uq, m_mla_kv_norm, m_w_ukv, m_fox_forget_bias, m_group_norm, m_w_out, m_ffn_norm, m_w_gate, m_w_up, m_w_down, m_final_norm, v_attn_norm, v_w_in, v_mla_q_norm, v_w_uq, v_mla_kv_norm, v_w_ukv, v_fox_forget_bias, v_group_norm, v_w_out, v_ffn_norm, v_w_gate, v_w_up, v_w_down, v_final_norm):
    _, S, D = x.shape
    L = attn_norm.shape[0]
    lay = _Layout(D)
    H, GW, PW = lay.H, lay.GW, lay.PW
    FB = w_gate.shape[2]
    QKA = HEAD + QK_ROPE
    x = x[0]
    target = loss_target[0]
    rope_full, rope_half = _rope_tables(S)
    neg = lambda t: (t[0], -t[1])
    tab, win = _dilated_table(*_soft_tiles(S))
    cb = lambda col: col // LANE

    sh = dict(w_in=lay.pad(w_in).astype(BF16),
              **{n: w.astype(BF16) for n, w in (("w_uq", w_uq), ("w_ukv", w_ukv), ("w_out", w_out),
                                                  ("w_gate", w_gate), ("w_up", w_up), ("w_down", w_down))})
    first3 = ["w_in", "w_uq", "w_ukv"]

    def first_weights(g):
        wuq = jnp.transpose(g[1], (1, 0, 2)).reshape(Q_LORA, H, QKA)
        wuq = jnp.pad(wuq, ((0, 0), (0, 0), (0, 2 * LANE - QKA))).reshape(Q_LORA, H * 2 * LANE)
        return dict(win=g[0].reshape(D, PW), wuq=wuq,
                    wukv=jnp.transpose(g[2], (1, 0, 2)).reshape(KV_LORA, H * 2 * LANE))

    def row(a):
        return a.reshape(1, -1)

    def forward(l, x0, W):
        A = dict(x0=x0)
        A["bias"] = jnp.pad(row(fox_forget_bias[l]), ((0, 0), (0, LANE - H)))
        h1 = A["h1"] = _rms_fwd(x0, row(attn_norm[l]), D, 0, BF16, "attn_norm")
        (proj, pb), (W["wg"],) = _mm(h1, W["win"], "in_proj", out_dtype=(F32, BF16),
                                     comm=("gather", [sh["w_gate"][l]]))
        A["proj"], A["pb"] = proj, pb
        qln = A["qln"] = _rms_fwd(proj, row(mla_q_norm[l]), Q_LORA, cb(lay.QL) // 4, BF16, "q_norm")
        kvln = A["kvln"] = _rms_fwd(proj, row(mla_kv_norm[l]), KV_LORA, cb(lay.KVL) // 4, BF16, "kv_norm")
        qa, qab = _mm(qln, W["wuq"], "q_up", out_dtype=(F32, BF16))
        A["qab"] = qab
        kv = A["kv"] = _mm(kvln, W["wukv"], "kv_up", out_dtype=BF16)
        q_pe = A["q_pe"] = _rope(qa, (1, H, 2), *rope_half, True, BF16, "rope_q_mla")
        k_pe = A["k_pe"] = _rope(proj, (cb(lay.KR), 1, 1), *rope_half, True, BF16, "rope_k_mla")
        (A["o_a"], A["lse_a"]), (W["wu"],) = _attn_fwd(
            S, H, qab, lambda h: 2 * h, kv, lambda h: 2 * h, kv, lambda h: 2 * h + 1, QKA ** -0.5, "mla_fwd",
            q2=q_pe, q2cb=lambda h: h, k2=k_pe, k2cb=lambda h: 0, comm=("gather", [sh["w_up"][l]]))
        qk_b = A["qk_b"] = _rope(proj, (cb(lay.QB), 2 * H, 1), *rope_full, False, BF16, "rope_qk_dil")
        (A["o_b"], A["lse_b"]), (g_down,) = _attn_fwd(
            S, H, qk_b, lambda h: h, qk_b, lambda h: H + h, pb, lambda h: cb(lay.VB) + h, HEAD ** -0.5,
            "dilated_fwd", tab=tab, win=win, comm=("gather", [sh["w_down"][l]]))
        W["wd"] = g_down.reshape(NDEV * FB, D)
        ccol, crow = A["ccol"], A["crow"] = _fox_prep(S, H, proj, cb(lay.FC), A["bias"], "fox_prep")
        (A["o_c"], A["lse_c"]), (g_out,) = _attn_fwd(
            S, H, pb, lambda h: cb(lay.QC) + h, pb, lambda h: cb(lay.KC) + h, pb, lambda h: cb(lay.VC) + h,
            HEAD ** -0.5, "fox_fwd", ccol=ccol, crow=crow, comm=("gather", [sh["w_out"][l]]))
        W["wout"] = g_out.reshape(4 * GW, D)
        A["o_d"], A["tot_d"] = _stick_fwd(
            S, H, pb, lambda h: cb(lay.QD) + h, lambda h: cb(lay.KD) + h, lambda h: cb(lay.VD) + h,
            HEAD ** -0.5, "stick_fwd")
        mix = A["mix"] = _gn_fwd([A["o_a"], A["o_b"], A["o_c"], A["o_d"]], row(group_norm[l]), "group_norm")
        x1 = A["x1"] = _mm(mix, W["wout"], "out_proj", res=x0)
        h2 = A["h2"] = _rms_fwd(x1, row(ffn_norm[l]), D, 0, BF16, "ffn_norm")
        nxt = None
        if l + 1 < L:
            (A["g"], A["u"], A["act"]), nxt = _ffn_up(
                h2, W["wg"], W["wu"], "ffn_up", comm=("gather", [sh[n][l + 1] for n in first3]))
        else:
            A["g"], A["u"], A["act"] = _ffn_up(h2, W["wg"], W["wu"], "ffn_up_last")
        return _mm_down(A["act"], W["wd"], x1, "ffn_down"), A, nxt

    def backward(l, dx2, dx2b, W, A, late):
        proj, pb = A["proj"], A["pb"]
        G, small, got = {}, {}, {}
        dgate, dup = _ffn_dact(dx2b, W["wd"], A["g"], A["u"], "ffn_dact")
        G["w_down"] = _mm_dwdown(A["act"], dx2b, "dw_down").reshape(NDEV, FB, D)
        dh2 = _mm_dh2(dgate, W["wg"], dup, W["wu"], "ffn_dh")
        if late is None:
            G["w_gate"] = _mm_dwgate(A["h2"], dgate, "dw_gate_top")
            G["w_up"] = _mm_dwgate(A["h2"], dup, "dw_up_top")
        else:
            G["w_gate"], (got[l + 1, "w_in_a"],) = _mm_dwgate(
                A["h2"], dgate, "dw_gate", comm=("exchange", [late["w_in_a"]]))
            G["w_up"], (got[l + 1, "w_in_b"],) = _mm_dwgate(
                A["h2"], dup, "dw_up", comm=("exchange", [late["w_in_b"]]))
        dx1, dx1b, small["ffn_norm"] = _rms_bwd(A["x1"], row(ffn_norm[l]), dh2, D, 0, "ffn_norm_bwd", res=dx2)
        dmix = _mm(dx1b, W["wout"], "out_proj_dx", tb=True)
        G["w_out"] = _mm(A["mix"], dx1b, "dw_out", ta=True, out_dtype=BF16).reshape(NDEV, 4 * GW // NDEV, D)
        do_a, do_b, do_c, do_d, small["group_norm"] = _gn_bwd(
            [A["o_a"], A["o_b"], A["o_c"], A["o_d"]], row(group_norm[l]), dmix, "group_norm_bwd")
        (dq_d, dk_d, dv_d), (got[l, "w_up"],) = _stick_bwd(
            S, H, pb, lambda h: cb(lay.QD) + h, lambda h: cb(lay.KD) + h, lambda h: cb(lay.VD) + h,
            do_d, A["tot_d"], HEAD ** -0.5, "stick_bwd", comm=("exchange", [G["w_up"]]))
        (dq_c, dk_c, dv_c, dcc), (got[l, "w_down"],) = _attn_bwd(
            S, H, pb, lambda h: cb(lay.QC) + h, pb, lambda h: cb(lay.KC) + h, pb, lambda h: cb(lay.VC) + h,
            A["o_c"], do_c, A["lse_c"], HEAD ** -0.5, "fox_bwd",
            ccol=A["ccol"], crow=A["crow"], comm=("exchange", [G["w_down"]]))
        dfc, dbias = _fox_bwd(S, H, proj, cb(lay.FC), A["bias"], dcc, "fox_gate_bwd")
        small["fox_forget_bias"] = dbias[0, :H]
        qk_b = A["qk_b"]
        (dq_b, dk_b, dv_b), moved = _attn_bwd(
            S, H, qk_b, lambda h: h, qk_b, lambda h: H + h, pb, lambda h: cb(lay.VB) + h,
            A["o_b"], do_b, A["lse_b"], HEAD ** -0.5, "dilated_bwd" if late else "dilated_bwd_top", tab=tab, win=win,
            comm=("exchange", [G["w_out"]] + ([late["w_uq"], late["w_ukv"]] if late else [])))
        got[l, "w_out"] = moved[0]
        if late:
            got[l + 1, "w_uq"], got[l + 1, "w_ukv"] = moved[1:]
        dqk_b = _rope(jnp.concatenate([dq_b, dk_b], 1), (0, 2 * H, 1), *neg(rope_full), False, BF16, "rope_qk_dil_bwd")
        qab, kv = A["qab"], A["kv"]
        (dqa, dkv, dk2), (got[l, "w_gate"],) = _attn_bwd(
            S, H, qab, lambda h: 2 * h, kv, lambda h: 2 * h, kv, lambda h: 2 * h + 1,
            A["o_a"], do_a, A["lse_a"], QKA ** -0.5, "mla_bwd", q2=A["q_pe"], q2cb=lambda h: h, k2=A["k_pe"],
            k2cb=lambda h: 0, rope2=neg(rope_half), comm=("exchange", [G["w_gate"]]))
        dk_pe = _rope(dk2.astype(F32).reshape(S, H, LANE).sum(1), (0, 1, 1), *neg(rope_half), True, BF16,
                      "rope_k_mla_bwd")
        dwuq = _mm(A["qln"], dqa, "dw_uq", ta=True, out_dtype=BF16)
        dwuq = dwuq.reshape(Q_LORA, H, 2 * LANE)[:, :, :QKA].reshape(Q_LORA, NDEV, H * QKA // NDEV)
        G["w_uq"] = jnp.transpose(dwuq, (1, 0, 2))
        dwukv = _mm(A["kvln"], dkv, "dw_ukv", ta=True, out_dtype=BF16).reshape(KV_LORA, NDEV, H * 2 * LANE // NDEV)
        G["w_ukv"] = jnp.transpose(dwukv, (1, 0, 2))
        dqln = _mm(dqa, W["wuq"], "q_up_dx", tb=True)
        dkvln = _mm(dkv, W["wukv"], "kv_up_dx", tb=True)
        _, dql, small["mla_q_norm"] = _rms_bwd(proj, row(mla_q_norm[l]), dqln, Q_LORA, cb(lay.QL) // 4, "q_norm_bwd")
        _, dkvl, small["mla_kv_norm"] = _rms_bwd(
            proj, row(mla_kv_norm[l]), dkvln, KV_LORA, cb(lay.KVL) // 4, "kv_norm_bwd")
        dproj = jnp.concatenate([
            dql, dkvl, dqk_b, dv_b, dq_c, dk_c, dv_c, dq_d, dk_d, dv_d,
            dk_pe, dfc.astype(BF16), jnp.zeros((S, PW - lay.FC - LANE), BF16)], axis=1)
        g_in = _mm(A["h1"], dproj, "dw_in", ta=True, out_dtype=BF16).reshape(NDEV, D // NDEV, PW)
        half = D // NDEV // 2
        late = dict(w_in_a=g_in[:, :half], w_in_b=g_in[:, half:], w_uq=G["w_uq"], w_ukv=G["w_ukv"])
        if l == 0:
            names = ["w_in_a", "w_uq", "w_ukv"]
            dh1, moved = _mm(dproj, W["win"], "in_proj_dx_last", tb=True,
                             comm=("exchange", [late[n] for n in names]))
            got.update({(0, n): s for n, s in zip(names, moved)})
        else:
            dh1 = _mm(dproj, W["win"], "in_proj_dx", tb=True)
        dx0, dx0b, small["attn_norm"] = _rms_bwd(A["x0"], row(attn_norm[l]), dh1, D, 0, "attn_norm_bwd", res=dx1)
        return dx0, dx0b, late, got, small

    big = first3 + ["w_out", "w_gate", "w_up", "w_down"]
    Ws, As = [], []
    xc = x
    nxt = _comm_alone("gather", [sh[n][0] for n in first3], "gather_first")
    for l in range(L):
        W = first_weights(nxt)
        xc, A, nxt = forward(l, xc, W)
        Ws.append(W)
        As.append(A)
    dx, loss_part = _final_loss(xc, row(final_norm), target, "final_loss")
    dx, dxb, dfinal = _rms_bwd(xc, row(final_norm), dx, D, 0, "final_norm_bwd")
    slots = {}
    smalls = [None] * L
    late = None
    for l in reversed(range(L)):
        dx, dxb, late, got, smalls[l] = backward(l, dx, dxb, Ws[l], As[l], late)
        slots.update(got)

    names_small = ["attn_norm", "mla_q_norm", "mla_kv_norm", "fox_forget_bias", "group_norm", "ffn_norm"]
    params = dict(attn_norm=attn_norm, mla_q_norm=mla_q_norm, mla_kv_norm=mla_kv_norm, fox_forget_bias=fox_forget_bias,
                  group_norm=group_norm, ffn_norm=ffn_norm, final_norm=final_norm, w_in=w_in, w_uq=w_uq, w_ukv=w_ukv,
                  w_out=w_out, w_gate=w_gate, w_up=w_up, w_down=w_down)
    moms = dict(attn_norm=(m_attn_norm, v_attn_norm), mla_q_norm=(m_mla_q_norm, v_mla_q_norm),
                mla_kv_norm=(m_mla_kv_norm, v_mla_kv_norm), fox_forget_bias=(m_fox_forget_bias, v_fox_forget_bias),
                group_norm=(m_group_norm, v_group_norm), ffn_norm=(m_ffn_norm, v_ffn_norm),
                final_norm=(m_final_norm, v_final_norm), w_in=(m_w_in, v_w_in), w_uq=(m_w_uq, v_w_uq),
                w_ukv=(m_w_ukv, v_w_ukv), w_out=(m_w_out, v_w_out), w_gate=(m_w_gate, v_w_gate),
                w_up=(m_w_up, v_w_up), w_down=(m_w_down, v_w_down))
    small_list = names_small + ["final_norm"]
    small_grads = [jnp.stack([smalls[l][n].reshape(params[n].shape[1:]) for l in range(L)]) for n in names_small]
    small_grads.append(dfinal.reshape(final_norm.shape))
    shapes = [params[n].shape for n in small_list] + [(LANE,)]
    packed_g = _comm_alone("gather", [_pack(small_grads + [loss_part.reshape(LANE)])], "gather_small")[0]
    zero = jnp.zeros((LANE,), F32)
    res_small = _adamw(_pack([params[n] for n in small_list] + [zero]), [packed_g],
                       _pack([moms[n][0] for n in small_list] + [zero]),
                       _pack([moms[n][1] for n in small_list] + [zero]), "adamw_small")
    unp = [_unpack(r, shapes) for r in res_small]
    out = {n: tuple(unp[k][i] for k in range(4)) for i, n in enumerate(small_list)}
    loss = unp[0][-1][0]

    for n in ["w_gate", "w_up", "w_down", "w_out", "w_uq", "w_ukv", "w_in"]:
        if n == "w_in":
            st = [lay.unpad(slots[l, n + h]) for l in range(L) for h in ("_a", "_b")]
        else:
            st = [slots[l, n] for l in range(L)]
        C = st[0].shape[-1]
        st = [s.reshape(NDEV, -1, C) for s in st]
        args = (params[n].reshape(-1, C), st, moms[n][0].reshape(-1, C), moms[n][1].reshape(-1, C), "adamw_" + n)
        if n == "w_gate":
            res, (slots[0, "w_in_b"],) = _adamw(*args, comm=("exchange", [late["w_in_b"]]))
        else:
            res = _adamw(*args)
        out[n] = tuple(r.reshape(params[n].shape) for r in res)

    order = ["attn_norm", "w_in", "mla_q_norm", "w_uq", "mla_kv_norm", "w_ukv", "fox_forget_bias", "group_norm",
             "w_out", "ffn_norm", "w_gate", "w_up", "w_down", "final_norm"]
    return (loss, dx[None], *[out[n][0] for n in order], *[out[n][1] for n in order],
            *[out[n][2] for n in order], *[out[n][3] for n in order])
```

```python
import functools
import math

import numpy as np
import jax
import jax.numpy as jnp
from jax import lax
from jax.experimental import pallas as pl
from jax.experimental.pallas import tpu as pltpu

F32 = jnp.float32
BF16 = jnp.bfloat16
NDEV = 8
LANE = 128
HEAD = 128
Q_LORA = 512
KV_LORA = 512
QK_ROPE = 64
DILATED_PAIRS = ((128, 1), (512, 4), (2048, 16))
ROPE_THETA = 10000.0
EPS = 1e-6
NEG = -1e30
TQ = 256
TK = 128
TKS = 512
VMEM_LIMIT = 48 * 1024 * 1024
MM_OPERAND_BYTES = 20 * 1024 * 1024
FFN_BLOCKS_PER_STEP = 2
ADAMW_BLOCK_BYTES = 24 * 1024 * 1024
ADAM_LR, ADAM_B1, ADAM_B2, ADAM_EPS, ADAM_WD, ADAM_STEP = 0.001, 0.9, 0.999, 1e-08, 0.01, 10
MESH = pl.DeviceIdType.MESH
ANY = pl.BlockSpec(memory_space=pl.ANY)


def _cp(*sem):
    return pltpu.CompilerParams(dimension_semantics=sem, vmem_limit_bytes=VMEM_LIMIT)


def _dot(a, b, ca, cb):
    return lax.dot_general(a, b, (((ca,), (cb,)), ((), ())), preferred_element_type=F32)


def _dot_nn(a, b):
    return _dot(a, b, 1, 0)


def _dot_nt(a, b):
    return _dot(a, b, 1, 1)


def _dot_tn(a, b):
    return _dot(a, b, 0, 0)


def _tile(n, t):
    if n <= t:
        return n
    t -= t % LANE
    while n % t:
        t -= LANE
    return t


def _direct_copies(ins, outs, send_sems, recv_sems, local_sems, want_recvs=True):
    x, y, c = lax.axis_index("x"), lax.axis_index("y"), lax.axis_index("c")
    my_id = 4 * x + 2 * y + c
    local, sends, recvs = [], [], []
    for a in range(len(ins)):
        mine = ins[a].at[my_id]
        local.append(pltpu.make_async_copy(mine, outs[a].at[my_id], local_sems.at[a]))
        for k in range(1, NDEV):
            peer = (1 - x if k & 4 else x, 1 - y if k & 2 else y, 1 - c if k & 1 else c)
            pid = 4 * peer[0] + 2 * peer[1] + peer[2]
            sems = dict(send_sem=send_sems.at[a, k - 1], recv_sem=recv_sems.at[a, k - 1],
                        device_id=peer, device_id_type=MESH)
            sends.append(pltpu.make_async_remote_copy(src_ref=ins[a].at[pid], dst_ref=outs[a].at[my_id], **sems))
            if want_recvs:
                recvs.append(pltpu.make_async_remote_copy(src_ref=mine, dst_ref=outs[a].at[pid], **sems))
    return local, sends, recvs


def _comm_start(kind, ins, outs, send_sems, recv_sems, local_sems):
    if kind == "exchange":
        local, sends, _ = _direct_copies(ins, outs, send_sems, recv_sems, local_sems, want_recvs=False)
        for cp in local + sends:
            cp.start()
        return
    x, y, c = lax.axis_index("x"), lax.axis_index("y"), lax.axis_index("c")
    for a in range(len(ins)):
        mine = outs[a].at[4 * x + 2 * y + c]
        pltpu.make_async_copy(ins[a], mine, local_sems.at[a]).start()
        for k, to in enumerate([(x, y, 1 - c), (1 - x, y, c), (x, 1 - y, c), (1 - x, 1 - y, c)]):
            pltpu.make_async_remote_copy(src_ref=ins[a], dst_ref=mine, send_sem=send_sems.at[a, k],
                                         recv_sem=recv_sems.at[a, k], device_id=to, device_id_type=MESH).start()


def _comm_finish(kind, ins, outs, send_sems, recv_sems, local_sems):
    if kind == "exchange":
        local, sends, recvs = _direct_copies(ins, outs, send_sems, recv_sems, local_sems)
        for cp in recvs:
            cp.wait_recv()
        for cp in sends:
            cp.wait_send()
        for cp in local:
            cp.wait()
        return
    x, y, c = lax.axis_index("x"), lax.axis_index("y"), lax.axis_index("c")
    sibling = (x, y, 1 - c)
    chips = [(1 - x, y), (x, 1 - y), (1 - x, 1 - y)]
    for a in range(len(ins)):
        def copy(k, block, to):
            rows = outs[a].at[4 * block[0] + 2 * block[1] + block[2]]
            return pltpu.make_async_remote_copy(src_ref=rows, dst_ref=rows, send_sem=send_sems.at[a, k],
                                                recv_sem=recv_sems.at[a, k], device_id=to, device_id_type=MESH)

        passed = []
        for j, chip in enumerate(chips):
            copy(1 + j, (*chip, c), (x, y, c)).wait_recv()
            passed.append(copy(4 + j, (*chip, c), sibling))
            passed[-1].start()
        copy(0, sibling, (x, y, c)).wait_recv()
        for j, chip in enumerate(chips):
            copy(4 + j, (*chip, 1 - c), (x, y, c)).wait_recv()
        for k in range(4):
            copy(k, (x, y, c), sibling).wait_send()
        for cp in passed:
            cp.wait_send()
        pltpu.make_async_copy(ins[a], outs[a].at[4 * x + 2 * y + c], local_sems.at[a]).wait()


def _comm_shapes(kind, arrs):
    out_shape = [jax.ShapeDtypeStruct(((NDEV,) if kind == "gather" else ()) + a.shape, a.dtype) for a in arrs]
    n = len(arrs)
    sems = [pltpu.SemaphoreType.DMA((n, 7)), pltpu.SemaphoreType.DMA((n, 7)), pltpu.SemaphoreType.DMA((n,))]
    return out_shape, sems


def _comm_alone(kind, arrs, name):
    n = len(arrs)

    def body(*refs):
        _comm_start(kind, refs[:n], refs[n:2 * n], *refs[2 * n:])
        _comm_finish(kind, refs[:n], refs[n:2 * n], *refs[2 * n:])

    out_shape, sems = _comm_shapes(kind, arrs)
    return pl.pallas_call(body, name=name, out_shape=out_shape, in_specs=[ANY] * n, out_specs=[ANY] * n,
                          scratch_shapes=sems)(*arrs)


def _pcall(body, *, name, grid, in_specs, out_specs, out_shape, operands, sem, scratch_shapes=(), comm=None):
    in_specs, out_specs, out_shape = list(in_specs), list(out_specs), list(out_shape)
    scratch_shapes = list(scratch_shapes)
    if comm is None:
        res = pl.pallas_call(body, name=name, grid=grid, in_specs=in_specs, out_specs=out_specs, out_shape=out_shape,
                             scratch_shapes=scratch_shapes, compiler_params=_cp(*sem))(*operands)
        return list(res), []
    kind, arrs = comm
    nc, n_in, n_out, n_scr = len(arrs), len(operands), len(out_shape), len(scratch_shapes)
    c_shape, c_sems = _comm_shapes(kind, arrs)

    def carrier(*refs):
        ins, cin = refs[:n_in], refs[n_in:n_in + nc]
        outs = refs[n_in + nc:n_in + nc + n_out]
        cout = refs[n_in + nc + n_out:n_in + 2 * nc + n_out]
        scr = refs[n_in + 2 * nc + n_out:n_in + 2 * nc + n_out + n_scr]
        sems = refs[n_in + 2 * nc + n_out + n_scr:]
        pids = [pl.program_id(d) for d in range(len(grid))]
        first = functools.reduce(jnp.logical_and, [p == 0 for p in pids])
        last = functools.reduce(jnp.logical_and, [p == g - 1 for p, g in zip(pids, grid)])

        @pl.when(first)
        def _():
            _comm_start(kind, cin, cout, *sems)

        body(*ins, *outs, *scr)

        @pl.when(last)
        def _():
            _comm_finish(kind, cin, cout, *sems)

    res = pl.pallas_call(
        carrier, name=name, grid=grid, in_specs=in_specs + [ANY] * nc, out_specs=out_specs + [ANY] * nc,
        out_shape=out_shape + c_shape, scratch_shapes=scratch_shapes + c_sems,
        compiler_params=_cp(*["arbitrary"] * len(grid)))(*operands, *arrs)
    return list(res[:n_out]), list(res[n_out:])


def _mm_call(pairs, grid, a_spec, b_spec, o_spec, out_shape, acc_shape, nk, ca, cb, name,
             res=None, res_spec=None, comm=None):
    npairs = len(pairs)
    multi = isinstance(out_shape, (list, tuple))
    nout = len(out_shape) if multi else 1

    def body(*refs):
        ab = refs[:2 * npairs]
        r_ref = refs[2 * npairs] if res is not None else None
        o_refs, acc = refs[-1 - nout:-1], refs[-1]
        k = pl.program_id(2)

        @pl.when(k == 0)
        def _():
            acc[...] = jnp.zeros_like(acc)

        tot = None
        for p in range(npairs):
            av, bv = ab[2 * p][...].astype(BF16), ab[2 * p + 1][...].astype(BF16)
            if av.ndim == 2:
                terms = [(av, bv)]
            else:
                rows = bv.shape[0] // av.shape[0]
                terms = [(av[q], bv[q] if bv.ndim == 3 else bv[q * rows:(q + 1) * rows]) for q in range(av.shape[0])]
            for at, bt in terms:
                d = _dot(at, bt, ca, cb)
                tot = d if tot is None else tot + d
        acc[...] += tot

        @pl.when(k == nk - 1)
        def _():
            r = acc[...]
            if r_ref is not None:
                r = r + r_ref[...]
            for o_ref in o_refs:
                o_ref[...] = r.astype(o_ref.dtype)

    ops, specs = [], []
    for a, b in pairs:
        ops += [a, b]
        specs += [a_spec, b_spec]
    if res is not None:
        ops.append(res)
        specs.append(res_spec)
    outs, moved = _pcall(
        body, name=name, grid=grid, in_specs=specs, out_specs=[o_spec] * nout,
        out_shape=out_shape if multi else [out_shape], operands=ops,
        scratch_shapes=[pltpu.VMEM(acc_shape, F32)], sem=("parallel", "parallel", "arbitrary"), comm=comm)
    outs = outs if multi else outs[0]
    return outs if comm is None else (outs, moved)


def _k_tile(K, row_bytes, tk=2048):
    tk = _tile(K, tk)
    while 2 * tk * row_bytes > MM_OPERAND_BYTES and tk % 256 == 0:
        tk //= 2
    return tk


def _mm(a, b, name, ta=False, tb=False, out_dtype=F32, res=None, tm=1024, tn=1024, comm=None):
    M, K = (a.shape[1], a.shape[0]) if ta else a.shape
    N = b.shape[0] if tb else b.shape[1]
    tm, tn = _tile(M, tm), _tile(N, tn)
    tk = _k_tile(K, tm * a.dtype.itemsize + tn * b.dtype.itemsize)
    a_spec = pl.BlockSpec((tk, tm), lambda i, j, k: (k, i)) if ta else pl.BlockSpec((tm, tk), lambda i, j, k: (i, k))
    b_spec = pl.BlockSpec((tn, tk), lambda i, j, k: (j, k)) if tb else pl.BlockSpec((tk, tn), lambda i, j, k: (k, j))
    o_spec = pl.BlockSpec((tm, tn), lambda i, j, k: (i, j))
    if isinstance(out_dtype, tuple):
        out_shape = [jax.ShapeDtypeStruct((M, N), d) for d in out_dtype]
    else:
        out_shape = jax.ShapeDtypeStruct((M, N), out_dtype)
    return _mm_call([(a, b)], (M // tm, N // tn, K // tk), a_spec, b_spec, o_spec,
                    out_shape, (tm, tn), K // tk,
                    0 if ta else 1, 1 if tb else 0, name, res=res, res_spec=o_spec, comm=comm)


def _mm_down(act, wd, res, name, tm=1024, tn=1024):
    _, S, FB = act.shape
    D = wd.shape[1]
    tm, tn = _tile(S, tm), _tile(D, tn)
    o_spec = pl.BlockSpec((tm, tn), lambda i, j, k: (i, j))
    nb = FFN_BLOCKS_PER_STEP
    return _mm_call([(act, wd)], (S // tm, D // tn, NDEV // nb),
                    pl.BlockSpec((nb, tm, FB), lambda i, j, k: (k, i, 0)),
                    pl.BlockSpec((nb * FB, tn), lambda i, j, k: (k, j)), o_spec,
                    jax.ShapeDtypeStruct((S, D), F32), (tm, tn), NDEV // nb, 1, 0, name, res=res, res_spec=o_spec)


def _mm_dwdown(act, dy, name, tn=1024):
    _, S, FB = act.shape
    D = dy.shape[1]
    tn = _tile(D, tn)
    tk = _k_tile(S, FB * act.dtype.itemsize + tn * dy.dtype.itemsize)
    return _mm_call([(act, dy)], (NDEV, D // tn, S // tk),
                    pl.BlockSpec((None, tk, FB), lambda i, j, k: (i, k, 0)),
                    pl.BlockSpec((tk, tn), lambda i, j, k: (k, j)),
                    pl.BlockSpec((FB, tn), lambda i, j, k: (i, j)),
                    jax.ShapeDtypeStruct((NDEV * FB, D), BF16), (FB, tn), S // tk, 0, 0, name)


def _mm_dh2(dg, wg, du, wu, name, tm=1024, tn=1024, comm=None):
    _, S, FB = dg.shape
    D = wg.shape[1]
    tm, tn = _tile(S, tm), _tile(D, tn)
    nb = FFN_BLOCKS_PER_STEP
    return _mm_call([(dg, wg), (du, wu)], (S // tm, D // tn, NDEV // nb),
                    pl.BlockSpec((nb, tm, FB), lambda i, j, k: (k, i, 0)),
                    pl.BlockSpec((nb, tn, FB), lambda i, j, k: (k, j, 0)),
                    pl.BlockSpec((tm, tn), lambda i, j, k: (i, j)),
                    jax.ShapeDtypeStruct((S, D), F32), (tm, tn), NDEV // nb, 1, 1, name, comm=comm)


def _mm_dwgate(h2, dg, name, tm=1024, comm=None):
    _, S, FB = dg.shape
    D = h2.shape[1]
    tm = _tile(D, tm)
    tk = _k_tile(S, tm * h2.dtype.itemsize + FB * dg.dtype.itemsize)
    return _mm_call([(h2, dg)], (NDEV, D // tm, S // tk),
                    pl.BlockSpec((tk, tm), lambda p, i, k: (k, i)),
                    pl.BlockSpec((None, tk, FB), lambda p, i, k: (p, k, 0)),
                    pl.BlockSpec((None, tm, FB), lambda p, i, k: (p, i, 0)),
                    jax.ShapeDtypeStruct((NDEV, D, FB), BF16), (tm, FB), S // tk, 0, 0, name, comm=comm)


def _ffn_up(h2, wg, wu, name, tm=512, comm=None):
    S, D = h2.shape
    FB = wg.shape[2]
    tm = _tile(S, tm)

    def body(h_ref, wg_ref, wu_ref, g_ref, u_ref, act_ref):
        h = h_ref[...]
        g = _dot_nn(h, wg_ref[...])
        u = _dot_nn(h, wu_ref[...])
        g_ref[...] = g.astype(BF16)
        u_ref[...] = u.astype(BF16)
        act_ref[...] = (g / (1.0 + jnp.exp(-g)) * u).astype(BF16)

    w_spec = pl.BlockSpec((None, D, FB), lambda p, i: (p, 0, 0))
    o_spec = pl.BlockSpec((None, tm, FB), lambda p, i: (p, i, 0))
    shp = (NDEV, S, FB)
    outs, moved = _pcall(
        body, name=name, grid=(NDEV, S // tm),
        in_specs=[pl.BlockSpec((tm, D), lambda p, i: (i, 0)), w_spec, w_spec],
        out_specs=[o_spec, o_spec, o_spec],
        out_shape=[jax.ShapeDtypeStruct(shp, BF16)] * 3,
        operands=[h2, wg, wu], sem=("parallel", "parallel"), comm=comm)
    return outs if comm is None else (outs, moved)


def _ffn_dact(dy, wd, g, u, name, tm=512):
    S, D = dy.shape
    FB = g.shape[2]
    tm = _tile(S, tm)

    def body(dy_ref, wd_ref, g_ref, u_ref, dg_ref, du_ref):
        dact = _dot_nt(dy_ref[...], wd_ref[...])
        gv = g_ref[...].astype(F32)
        sg = pl.reciprocal(1.0 + jnp.exp(-gv), approx=True)
        dg_ref[...] = (dact * u_ref[...].astype(F32) * (sg * (1.0 + gv * (1.0 - sg)))).astype(BF16)
        du_ref[...] = (dact * (gv * sg)).astype(BF16)

    t_spec = pl.BlockSpec((None, tm, FB), lambda p, i: (p, i, 0))
    shp = jax.ShapeDtypeStruct((NDEV, S, FB), BF16)
    return pl.pallas_call(
        body, name=name, grid=(NDEV, S // tm),
        in_specs=[pl.BlockSpec((tm, D), lambda p, i: (i, 0)), pl.BlockSpec((FB, D), lambda p, i: (p, 0)),
                  t_spec, t_spec],
        out_specs=[t_spec, t_spec], out_shape=[shp, shp],
        compiler_params=_cp("parallel", "parallel"),
    )(dy, wd, g, u)


def _rms_fwd(x, gain, width, cb, out_dtype, name, ts=512):
    S = x.shape[0]
    ts = _tile(S, ts)

    def body(x_ref, g_ref, o_ref):
        xv = x_ref[...]
        r = lax.rsqrt(jnp.mean(xv * xv, axis=1, keepdims=True) + EPS)
        o_ref[...] = (xv * r * g_ref[...]).astype(o_ref.dtype)

    return pl.pallas_call(
        body, name=name, grid=(S // ts,),
        in_specs=[pl.BlockSpec((ts, width), lambda i: (i, cb)), pl.BlockSpec((1, width), lambda i: (0, 0))],
        out_specs=pl.BlockSpec((ts, width), lambda i: (i, 0)),
        out_shape=jax.ShapeDtypeStruct((S, width), out_dtype),
        compiler_params=_cp("parallel"),
    )(x, gain)


def _rms_bwd(x, gain, dy, width, cb, name, res=None, ts=256):
    S = x.shape[0]
    ts = _tile(S, ts)
    has_res = res is not None

    def body(*refs):
        x_ref, g_ref, dy_ref = refs[:3]
        r_ref = refs[3] if has_res else None
        dx_ref, dxb_ref, dg_ref = refs[-3], refs[-2], refs[-1]

        @pl.when(pl.program_id(0) == 0)
        def _():
            dg_ref[...] = jnp.zeros_like(dg_ref)

        xv = x_ref[...]
        r = lax.rsqrt(jnp.mean(xv * xv, axis=1, keepdims=True) + EPS)
        xh = xv * r
        dyv = dy_ref[...]
        dyg = dyv * g_ref[...]
        dx = r * (dyg - xh * jnp.mean(dyg * xh, axis=1, keepdims=True))
        if has_res:
            dx = dx + r_ref[...]
        dx_ref[...] = dx
        dxb_ref[...] = dx.astype(BF16)
        dg_ref[...] += jnp.sum(dyv * xh, axis=0, keepdims=True)

    row = pl.BlockSpec((ts, width), lambda i: (i, 0))
    vec = pl.BlockSpec((1, width), lambda i: (0, 0))
    ops = [x, gain, dy] + ([res] if has_res else [])
    specs = [pl.BlockSpec((ts, width), lambda i: (i, cb)), vec, row] + ([row] if has_res else [])
    return pl.pallas_call(
        body, name=name, grid=(S // ts,), in_specs=specs, out_specs=[row, row, vec],
        out_shape=[jax.ShapeDtypeStruct((S, width), F32), jax.ShapeDtypeStruct((S, width), BF16),
                   jax.ShapeDtypeStruct((1, width), F32)],
        compiler_params=_cp("arbitrary"),
    )(*ops)


def _gn_fwd(outs, gain, name, ts=512):
    S, GW = outs[0].shape
    ts = _tile(S, ts)

    def body(a_ref, b_ref, c_ref, d_ref, g_ref, o_ref):
        for g, r_ref in enumerate((a_ref, b_ref, c_ref, d_ref)):
            xv = r_ref[...]
            r = lax.rsqrt(jnp.mean(xv * xv, axis=1, keepdims=True) + EPS)
            o_ref[:, g * GW:(g + 1) * GW] = (xv * r * g_ref[:, g * GW:(g + 1) * GW]).astype(BF16)

    row = pl.BlockSpec((ts, GW), lambda i: (i, 0))
    return pl.pallas_call(
        body, name=name, grid=(S // ts,),
        in_specs=[row] * 4 + [pl.BlockSpec((1, 4 * GW), lambda i: (0, 0))],
        out_specs=pl.BlockSpec((ts, 4 * GW), lambda i: (i, 0)),
        out_shape=jax.ShapeDtypeStruct((S, 4 * GW), BF16),
        compiler_params=_cp("parallel"),
    )(*outs, gain)


def _gn_bwd(outs, gain, dmix, name, ts=256):
    S, GW = outs[0].shape
    ts = _tile(S, ts)

    def body(a_ref, b_ref, c_ref, d_ref, g_ref, dm_ref, da_ref, db_ref, dc_ref, dd_ref, dg_ref):
        @pl.when(pl.program_id(0) == 0)
        def _():
            dg_ref[...] = jnp.zeros_like(dg_ref)

        for g, (r_ref, o_ref) in enumerate(zip((a_ref, b_ref, c_ref, d_ref), (da_ref, db_ref, dc_ref, dd_ref))):
            sl = slice(g * GW, (g + 1) * GW)
            xv = r_ref[...]
            r = lax.rsqrt(jnp.mean(xv * xv, axis=1, keepdims=True) + EPS)
            xh = xv * r
            dyv = dm_ref[:, sl]
            dyg = dyv * g_ref[:, sl]
            o_ref[...] = (r * (dyg - xh * jnp.mean(dyg * xh, axis=1, keepdims=True))).astype(BF16)
            dg_ref[:, sl] += jnp.sum(dyv * xh, axis=0, keepdims=True)

    row = pl.BlockSpec((ts, GW), lambda i: (i, 0))
    vec = pl.BlockSpec((1, 4 * GW), lambda i: (0, 0))
    return pl.pallas_call(
        body, name=name, grid=(S // ts,),
        in_specs=[row] * 4 + [vec, pl.BlockSpec((ts, 4 * GW), lambda i: (i, 0))],
        out_specs=[row] * 4 + [vec],
        out_shape=[jax.ShapeDtypeStruct((S, GW), BF16)] * 4 + [jax.ShapeDtypeStruct((1, 4 * GW), F32)],
        compiler_params=_cp("arbitrary"),
    )(*outs, gain, dmix)


def _rope_partner(x, half):
    if not half:
        return pltpu.roll(x, 64, 1)
    lane = lax.broadcasted_iota(jnp.int32, x.shape, 1)
    return jnp.where(lane % 64 < 32, pltpu.roll(x, LANE - 32, 1), pltpu.roll(x, 32, 1))


def _rope(x, cbs, cos, sin, half, out_dtype, name, ts=512):
    S = x.shape[0]
    cb0, nb, stride = cbs
    ts = _tile(S, ts)

    def body(x_ref, c_ref, s_ref, o_ref):
        xv = x_ref[...].astype(F32)
        o_ref[...] = (xv * c_ref[...] + _rope_partner(xv, half) * s_ref[...]).astype(o_ref.dtype)

    tab = pl.BlockSpec((ts, LANE), lambda i, j: (i, 0))
    return pl.pallas_call(
        body, name=name, grid=(S // ts, nb),
        in_specs=[pl.BlockSpec((ts, LANE), lambda i, j: (i, cb0 + stride * j)), tab, tab],
        out_specs=pl.BlockSpec((ts, LANE), lambda i, j: (i, j)),
        out_shape=jax.ShapeDtypeStruct((S, nb * LANE), out_dtype),
        compiler_params=_cp("parallel", "parallel"),
    )(x, cos, sin)


def _final_loss(x, gain, target, name, ts=256):
    S, D = x.shape
    ts = _tile(S, ts)

    def body(x_ref, g_ref, t_ref, dy_ref, l_ref):
        @pl.when(pl.program_id(0) == 0)
        def _():
            l_ref[...] = jnp.zeros_like(l_ref)

        xv = x_ref[...]
        r = lax.rsqrt(jnp.mean(xv * xv, axis=1, keepdims=True) + EPS)
        err = xv * r * g_ref[...] - t_ref[...]
        dy_ref[...] = err * (1.0 / D)
        part = jnp.sum(jnp.mean(err * err, axis=1, keepdims=True), axis=0, keepdims=True)
        l_ref[...] += jnp.broadcast_to(0.5 * part, (1, LANE))

    row = pl.BlockSpec((ts, D), lambda i: (i, 0))
    return pl.pallas_call(
        body, name=name, grid=(S // ts,),
        in_specs=[row, pl.BlockSpec((1, D), lambda i: (0, 0)), row],
        out_specs=[row, pl.BlockSpec((1, LANE), lambda i: (0, 0))],
        out_shape=[jax.ShapeDtypeStruct((S, D), F32), jax.ShapeDtypeStruct((1, LANE), F32)],
        compiler_params=_cp("arbitrary"),
    )(x, gain, target)


def _colspec(rows, f):
    return pl.BlockSpec((rows, LANE), f)


def _soft_tiles(S):
    tq = _tile(S, TQ)
    tk = _tile(S, TKS)
    assert tk % tq == 0
    return tq, tk


def _key_row(crow_ref, j, tk):
    n = tk // TK
    return jnp.concatenate([crow_ref[j * n + c] for c in range(n)], axis=1)


def _attn_fwd(S, H, q1, q1cb, k1, k1cb, v, vcb, scale, name, q2=None, q2cb=None, k2=None, k2cb=None,
              tab=None, win=None, ccol=None, crow=None, comm=None):
    tq, tk = _soft_tiles(S)
    has2, hastab, hasc = q2 is not None, tab is not None, ccol is not None

    def body(*refs):
        it = iter(refs)
        q1r, k1r, vr = next(it), next(it), next(it)
        q2r, k2r = (next(it), next(it)) if has2 else (None, None)
        tabr = next(it) if hastab else None
        ccolr, crowr = (next(it), next(it)) if hasc else (None, None)
        o_ref, lse_ref = next(it), next(it)
        i = pl.program_id(1)
        q = q1r[...]
        qb2 = q2r[...] if has2 else None
        cq = ccolr[:, 0:1] if hasc else None
        qpos = i * tq + lax.broadcasted_iota(jnp.int32, (tq, tk), 0)
        kio = lax.broadcasted_iota(jnp.int32, (tq, tk), 1)
        j_diag = (i * tq) // tk
        j_lo = jnp.maximum((i * tq - win) // tk, 0) if win else 0

        if has2:
            q = jnp.concatenate([q, qb2], axis=1)

        def step(j, carry, masked):
            m, l, acc = carry
            off = pl.multiple_of(j * tk, tk)
            kb = k1r[pl.ds(off, tk), :]
            if has2:
                kb = jnp.concatenate([kb, k2r[pl.ds(off, tk), :]], axis=1)
            s = _dot_nt(q, kb) * scale
            if hastab:
                s = s + tabr[i - j * (tk // tq)]
            else:
                if hasc:
                    s = s + (cq - _key_row(crowr, j, tk))
                if masked:
                    s = jnp.where(kio + j * tk <= qpos, s, NEG)
            mn = jnp.maximum(m, jnp.max(s, axis=1, keepdims=True))
            p = jnp.exp(s - mn)
            al = jnp.exp(m - mn)
            l = al * l + jnp.sum(p, axis=1, keepdims=True)
            vb = vr[pl.ds(off, tk), :]
            ph = p.astype(BF16)
            pv = _dot_nn(ph, vb)
            if hasc:
                pv = pv + _dot_nn((p - ph.astype(F32)).astype(BF16), vb)
            acc = al * acc + pv
            return mn, l, acc

        carry = (jnp.full((tq, 1), NEG, F32), jnp.zeros((tq, 1), F32), jnp.zeros((tq, LANE), F32))
        if hastab:
            carry = lax.fori_loop(j_lo, j_diag + 1, functools.partial(step, masked=False), carry)
        else:
            carry = lax.fori_loop(j_lo, j_diag, functools.partial(step, masked=False), carry)
            carry = step(j_diag, carry, True)
        m, l, acc = carry
        o_ref[...] = acc / l
        lse_ref[...] = jnp.broadcast_to(m + jnp.log(l), (tq, LANE))

    ops = [q1, k1, v]
    specs = [_colspec(tq, lambda h, i: (i, q1cb(h))), _colspec(S, lambda h, i: (0, k1cb(h))),
             _colspec(S, lambda h, i: (0, vcb(h)))]
    if has2:
        ops += [q2, k2]
        specs += [_colspec(tq, lambda h, i: (i, q2cb(h))), _colspec(S, lambda h, i: (0, k2cb(h)))]
    if hastab:
        ops.append(tab)
        specs.append(pl.BlockSpec(tab.shape, lambda h, i: (0, 0, 0)))
    if hasc:
        ops += [ccol, crow]
        specs += [_colspec(tq, lambda h, i: (i, h)),
                  pl.BlockSpec((None, S // TK, 1, TK), lambda h, i: (h, 0, 0, 0))]
    o_spec = _colspec(tq, lambda h, i: (i, h))
    shp = jax.ShapeDtypeStruct((S, H * LANE), F32)
    outs, moved = _pcall(
        body, name=name, grid=(H, S // tq), in_specs=specs, out_specs=[o_spec, o_spec], out_shape=[shp, shp],
        operands=ops, sem=("parallel", "arbitrary"), comm=comm)
    return outs if comm is None else (outs, moved)


def _attn_bwd(S, H, q1, q1cb, k1, k1cb, v, vcb, o, do, lse, scale, name, q2=None, q2cb=None, k2=None, k2cb=None,
              rope2=None, tab=None, win=None, ccol=None, crow=None, comm=None):
    tq, tk = _soft_tiles(S)
    has2, hastab, hasc = q2 is not None, tab is not None, ccol is not None

    def body(*refs):
        it = iter(refs)
        q1r, k1r, vr, o_r, do_r, lse_r = (next(it) for _ in range(6))
        q2r, k2r, cos2_r, sin2_r = (next(it), next(it), next(it), next(it)) if has2 else (None,) * 4
        tabr = next(it) if hastab else None
        ccolr, crowr = (next(it), next(it)) if hasc else (None, None)
        dq1_r, dk1_o, dv_o = next(it), next(it), next(it)
        dk2_o = dv_o
        dcr_r = next(it) if hasc else None
        dk1_r, dv_r = next(it), next(it)
        dk2_r = next(it) if has2 else None
        i = pl.program_id(1)

        @pl.when(i == 0)
        def _():
            dk1_r[...] = jnp.zeros_like(dk1_r)
            dv_r[...] = jnp.zeros_like(dv_r)
            if has2:
                dk2_r[...] = jnp.zeros_like(dk2_r)
            if hasc:
                dcr_r[...] = jnp.zeros_like(dcr_r)

        q = q1r[...]
        qb2 = q2r[...] if has2 else None
        dob = do_r[...]
        delta = jnp.sum(dob.astype(F32) * o_r[...], axis=1, keepdims=True)
        lse_c = lse_r[:, 0:1]
        cq = ccolr[:, 0:1] if hasc else None
        qpos = i * tq + lax.broadcasted_iota(jnp.int32, (tq, tk), 0)
        kio = lax.broadcasted_iota(jnp.int32, (tq, tk), 1)
        j_diag = (i * tq) // tk
        j_lo = jnp.maximum((i * tq - win) // tk, 0) if win else 0

        if has2:
            q = jnp.concatenate([q, qb2], axis=1)

        def probs(j, masked):
            off = pl.multiple_of(j * tk, tk)
            kb = k1r[pl.ds(off, tk), :]
            if has2:
                kb = jnp.concatenate([kb, k2r[pl.ds(off, tk), :]], axis=1)
            s = _dot_nt(q, kb) * scale
            if hastab:
                s = s + tabr[i - j * (tk // tq)]
            else:
                if hasc:
                    s = s + (cq - _key_row(crowr, j, tk))
                if masked:
                    s = jnp.where(kio + j * tk <= qpos, s, NEG)
            p = jnp.exp(s - lse_c)
            dp = _dot_nt(dob, vr[pl.ds(off, tk), :])
            return off, kb, p, dp

        def sweep(fn, carry):
            if hastab:
                return lax.fori_loop(j_lo, j_diag + 1, functools.partial(fn, masked=False), carry)
            carry = lax.fori_loop(j_lo, j_diag, functools.partial(fn, masked=False), carry)
            return fn(j_diag, carry, True)

        def step(j, dq, masked):
            off, kb, p, dp = probs(j, masked)
            ds = p * (dp - delta)
            dsb = ds.astype(BF16)
            dk = _dot_tn(dsb, q) * scale
            dk1_r[pl.ds(off, tk), :] += dk[:, :LANE]
            if has2:
                dk2_r[pl.ds(off, tk), :] += dk[:, LANE:]
            dv_r[pl.ds(off, tk), :] += _dot_tn(p.astype(BF16), dob)
            if hasc:
                cs = -jnp.sum(ds, axis=0, keepdims=True)
                for c in range(tk // TK):
                    dcr_r[j * (tk // TK) + c] += cs[:, c * TK:(c + 1) * TK]
            return dq + _dot_nn(dsb, kb)

        dq = sweep(step, jnp.zeros(q.shape, F32)) * scale
        dq1_r[:, :LANE] = dq[:, :LANE].astype(BF16)
        if has2:
            x2 = dq[:, LANE:]
            dq1_r[:, LANE:] = (x2 * cos2_r[...] + _rope_partner(x2, True) * sin2_r[...]).astype(BF16)

        @pl.when(i == S // tq - 1)
        def _():
            dk1_o[:, :LANE] = dk1_r[...].astype(BF16)
            if has2:
                dk1_o[:, LANE:] = dv_r[...].astype(BF16)
                dk2_o[...] = dk2_r[...].astype(BF16)
            else:
                dv_o[...] = dv_r[...].astype(BF16)

    qspec = _colspec(tq, lambda h, i: (i, h))
    kspec = _colspec(S, lambda h, i: (0, h))
    ops = [q1, k1, v, o, do, lse]
    specs = [_colspec(tq, lambda h, i: (i, q1cb(h))), _colspec(S, lambda h, i: (0, k1cb(h))),
             _colspec(S, lambda h, i: (0, vcb(h))), qspec, qspec, qspec]
    if has2:
        ops += [q2, k2, *rope2]
        tab2 = _colspec(tq, lambda h, i: (i, 0))
        specs += [_colspec(tq, lambda h, i: (i, q2cb(h))), _colspec(S, lambda h, i: (0, k2cb(h))), tab2, tab2]
    if hastab:
        ops.append(tab)
        specs.append(pl.BlockSpec(tab.shape, lambda h, i: (0, 0, 0)))
    if hasc:
        ops += [ccol, crow]
        specs += [qspec, pl.BlockSpec((None, S // TK, 1, TK), lambda h, i: (h, 0, 0, 0))]
    assert all(t.dtype == BF16 for t in ops[:3] + [do] + ([q2, k2] if has2 else []))
    shp = jax.ShapeDtypeStruct((S, H * LANE), BF16)
    if has2:
        out_specs = [pl.BlockSpec((tq, 2 * LANE), lambda h, i: (i, h)), pl.BlockSpec((S, 2 * LANE), lambda h, i: (0, h)),
                     kspec]
        wide = jax.ShapeDtypeStruct((S, H * 2 * LANE), BF16)
        out_shape = [wide, wide, shp]
    else:
        out_specs = [qspec, kspec, kspec]
        out_shape = [shp] * 3
    if hasc:
        out_specs.append(pl.BlockSpec((None, S // TK, 1, TK), lambda h, i: (h, 0, 0, 0)))
        out_shape.append(jax.ShapeDtypeStruct((H, S // TK, 1, TK), F32))
    outs, moved = _pcall(
        body, name=name, grid=(H, S // tq), in_specs=specs, out_specs=out_specs, out_shape=out_shape,
        operands=ops, scratch_shapes=[pltpu.VMEM((S, LANE), F32)] * (3 if has2 else 2),
        sem=("parallel", "arbitrary"), comm=comm)
    return outs if comm is None else (outs, moved)


def _scan_matrix(kind):
    j = np.arange(TK)[:, None]
    s = np.arange(TK)[None, :]
    tri = {"suffix_ex": j > s, "prefix_in": j <= s, "prefix_ex": j < s}[kind].astype(np.float32)
    half = np.concatenate([tri, np.ones((TK, TK), np.float32)], axis=1)
    return jnp.asarray(np.concatenate([half, half], axis=0), BF16)


def _scan_mxu(x, mat, carry, reverse, split=True):
    n = x.shape[1] // TK
    hi = x.astype(BF16)
    if split:
        lo = (x - hi.astype(F32)).astype(BF16)
    else:
        mat = mat[:TK]
    parts = [None] * n
    for b in (reversed(range(n)) if reverse else range(n)):
        sl = slice(b * TK, (b + 1) * TK)
        r = _dot_nn(jnp.concatenate([hi[:, sl], lo[:, sl]], axis=1) if split else hi[:, sl], mat)
        parts[b] = r[:, :TK] + carry
        carry = carry + r[:, TK:]
    return jnp.concatenate(parts, axis=1), carry


def _stick_logs(z):
    e = jnp.exp(-jnp.abs(z))
    return e, -jnp.maximum(z, 0.0) - jnp.log(1.0 + e)


def _stick_fwd(S, H, x, qcb, kcb, vcb, scale, name, comm=None):
    tq, tk = _soft_tiles(S)
    assert x.dtype == BF16

    def body(q_r, k_r, v_r, mat_r, o_ref, t_ref):
        i = pl.program_id(1)
        q = q_r[...]
        qpos = i * tq + lax.broadcasted_iota(jnp.int32, (tq, tk), 0)
        lane = lax.broadcasted_iota(jnp.int32, (tq, tk), 1)
        j_diag = (i * tq) // tk

        def step(j, carry, masked):
            c, acc = carry
            off = pl.multiple_of(j * tk, tk)
            z = _dot_nt(q, k_r[pl.ds(off, tk), :]) * scale
            _, lk = _stick_logs(z)
            if masked:
                past = lane + j * tk < qpos
                lk = jnp.where(past, lk, 0.0)
            suf, c = _scan_mxu(lk, mat_r[...], c, True)
            a = jnp.exp(z + lk + suf)
            if masked:
                a = jnp.where(past, a, 0.0)
            acc = acc + _dot_nn(a.astype(BF16), v_r[pl.ds(off, tk), :])
            return c, acc

        carry = step(j_diag, (jnp.zeros((tq, TK), F32), jnp.zeros((tq, LANE), F32)), True)
        c, acc = lax.fori_loop(0, j_diag, lambda jj, cr: step(j_diag - 1 - jj, cr, False), carry)
        o_ref[...] = acc
        t_ref[...] = c

    o_spec = _colspec(tq, lambda h, i: (i, h))
    shp = jax.ShapeDtypeStruct((S, H * LANE), F32)
    mat = _scan_matrix("suffix_ex")
    outs, moved = _pcall(
        body, name=name, grid=(H, S // tq),
        in_specs=[_colspec(tq, lambda h, i: (i, qcb(h))), _colspec(S, lambda h, i: (0, kcb(h))),
                  _colspec(S, lambda h, i: (0, vcb(h))), pl.BlockSpec(mat.shape, lambda h, i: (0, 0))],
        out_specs=[o_spec, o_spec], out_shape=[shp, shp],
        operands=[x, x, x, mat], sem=("parallel", "arbitrary"), comm=comm)
    return outs if comm is None else (outs, moved)


def _stick_bwd(S, H, x, qcb, kcb, vcb, do, tot, scale, name, comm=None):
    tq, tk = _soft_tiles(S)
    assert x.dtype == BF16 and do.dtype == BF16

    def body(q_r, k_r, v_r, do_r, t_r, pin_r, pex_r, dq_r, dk_o, dv_o, dk_r, dv_r):
        i = pl.program_id(1)

        @pl.when(i == 0)
        def _():
            dk_r[...] = jnp.zeros_like(dk_r)
            dv_r[...] = jnp.zeros_like(dv_r)

        q = q_r[...]
        dob = do_r[...]
        total = jnp.concatenate([t_r[...]] * (tk // TK), axis=1)
        qpos = i * tq + lax.broadcasted_iota(jnp.int32, (tq, tk), 0)
        lane = lax.broadcasted_iota(jnp.int32, (tq, tk), 1)
        j_diag = (i * tq) // tk

        def step(j, carry, masked):
            cl, cg, dq = carry
            off = pl.multiple_of(j * tk, tk)
            kb = k_r[pl.ds(off, tk), :]
            z = _dot_nt(q, kb) * scale
            e, lk = _stick_logs(z)
            if masked:
                past = lane + j * tk < qpos
                lk = jnp.where(past, lk, 0.0)
            pre, cl = _scan_mxu(lk, pin_r[...], cl, False)
            a = jnp.exp(z + lk + (total - pre))
            if masked:
                a = jnp.where(past, a, 0.0)
            g = _dot_nt(dob, v_r[pl.ds(off, tk), :]) * a
            gpre, cg = _scan_mxu(g, pex_r[...], cg, False, split=False)
            inv = pl.reciprocal(1.0 + e, approx=True)
            small = e * inv
            pos = z >= 0
            dz = g * jnp.where(pos, small, inv) - jnp.where(pos, inv, small) * gpre
            if masked:
                dz = jnp.where(past, dz, 0.0)
            dzb = dz.astype(BF16)
            dk_r[pl.ds(off, tk), :] += _dot_tn(dzb, q) * scale
            dv_r[pl.ds(off, tk), :] += _dot_tn(a.astype(BF16), dob)
            return cl, cg, dq + _dot_nn(dzb, kb)

        zt = jnp.zeros((tq, TK), F32)
        carry = lax.fori_loop(0, j_diag, functools.partial(step, masked=False), (zt, zt, jnp.zeros((tq, LANE), F32)))
        dq_r[...] = (step(j_diag, carry, True)[2] * scale).astype(BF16)

        @pl.when(i == S // tq - 1)
        def _():
            dk_o[...] = dk_r[...].astype(BF16)
            dv_o[...] = dv_r[...].astype(BF16)

    qspec = _colspec(tq, lambda h, i: (i, h))
    kspec = _colspec(S, lambda h, i: (0, h))
    shp = jax.ShapeDtypeStruct((S, H * LANE), BF16)
    pin, pex = _scan_matrix("prefix_in"), _scan_matrix("prefix_ex")
    mspec = pl.BlockSpec(pin.shape, lambda h, i: (0, 0))
    outs, moved = _pcall(
        body, name=name, grid=(H, S // tq),
        in_specs=[_colspec(tq, lambda h, i: (i, qcb(h))), _colspec(S, lambda h, i: (0, kcb(h))),
                  _colspec(S, lambda h, i: (0, vcb(h))), qspec, qspec, mspec, mspec],
        out_specs=[qspec, kspec, kspec], out_shape=[shp] * 3,
        operands=[x, x, x, do, tot, pin, pex], scratch_shapes=[pltpu.VMEM((S, LANE), F32)] * 2,
        sem=("parallel", "arbitrary"), comm=comm)
    return outs if comm is None else (outs, moved)


def _scan8(x, rows, reverse):
    for sh in (1, 2, 4):
        if reverse:
            x = x + jnp.where(rows + sh < 8, pltpu.roll(x, 8 - sh, 0), 0.0)
        else:
            x = x + jnp.where(rows >= sh, pltpu.roll(x, sh, 0), 0.0)
    return x


def _fox_prep(S, H, proj, fcb, bias, name):
    tk = TK

    def body(f_ref, b_ref, ccol_ref, crow_ref, scr):
        rows = lax.broadcasted_iota(jnp.int32, (8, LANE), 0)

        def step(t, carry):
            off = pl.multiple_of(t * 8, 8)
            xb = f_ref[pl.ds(off, 8), :] + b_ref[...]
            lf = jnp.minimum(xb, 0.0) - jnp.log(1.0 + jnp.exp(-jnp.abs(xb)))
            lf = _scan8(lf, rows, False) + carry
            scr[pl.ds(off, 8), :] = lf
            return lf[7:8, :]

        lax.fori_loop(0, S // 8, step, jnp.zeros((1, LANE), F32))
        for h in range(H):
            ccol_ref[:, h * LANE:(h + 1) * LANE] = jnp.broadcast_to(scr[:, h:h + 1], (S, LANE))

            def tr(t, _):
                off = pl.multiple_of(t * tk, tk)
                blk = ccol_ref[pl.ds(off, tk), h * LANE:(h + 1) * LANE]
                crow_ref[h, t] = blk.T[0:1, :]
                return 0

            lax.fori_loop(0, S // tk, tr, 0)

    return pl.pallas_call(
        body, name=name, grid=(1,),
        in_specs=[_colspec(S, lambda i: (0, fcb)), pl.BlockSpec((1, LANE), lambda i: (0, 0))],
        out_specs=[pl.BlockSpec((S, H * LANE), lambda i: (0, 0)),
                   pl.BlockSpec((H, S // tk, 1, tk), lambda i: (0, 0, 0, 0))],
        out_shape=[jax.ShapeDtypeStruct((S, H * LANE), F32), jax.ShapeDtypeStruct((H, S // tk, 1, tk), F32)],
        scratch_shapes=[pltpu.VMEM((S, LANE), F32)],
        compiler_params=_cp("arbitrary"),
    )(proj, bias)


def _fox_bwd(S, H, proj, fcb, bias, dcr, name):
    tk = TK

    def body(f_ref, b_ref, dcr_ref, df_ref, db_ref, scr):
        rows = lax.broadcasted_iota(jnp.int32, (8, LANE), 0)
        lane_t = lax.broadcasted_iota(jnp.int32, (tk, LANE), 1)
        nb = S // 8

        def tr(t, _):
            off = pl.multiple_of(t * tk, tk)
            d = jnp.zeros((tk, LANE), F32)
            for h in range(H):
                d = d + jnp.where(lane_t == h, jnp.broadcast_to(dcr_ref[h, t], (LANE, tk)).T, 0.0)
            scr[pl.ds(off, tk), :] = d
            return 0

        lax.fori_loop(0, S // tk, tr, 0)

        def step(tt, carry):
            suffix, db = carry
            off = pl.multiple_of((nb - 1 - tt) * 8, 8)
            d = _scan8(scr[pl.ds(off, 8), :], rows, True) + suffix
            xb = f_ref[pl.ds(off, 8), :] + b_ref[...]
            e = jnp.exp(-jnp.abs(xb))
            dx = d * jnp.where(xb >= 0, e, 1.0) / (1.0 + e)
            df_ref[pl.ds(off, 8), :] = dx
            return d[0:1, :], db + jnp.sum(dx, axis=0, keepdims=True)

        z = jnp.zeros((1, LANE), F32)
        _, db = lax.fori_loop(0, nb, step, (z, z))
        db_ref[...] = db

    return pl.pallas_call(
        body, name=name, grid=(1,),
        in_specs=[_colspec(S, lambda i: (0, fcb)), pl.BlockSpec((1, LANE), lambda i: (0, 0)),
                  pl.BlockSpec((H, S // tk, 1, tk), lambda i: (0, 0, 0, 0))],
        out_specs=[pl.BlockSpec((S, LANE), lambda i: (0, 0)), pl.BlockSpec((1, LANE), lambda i: (0, 0))],
        out_shape=[jax.ShapeDtypeStruct((S, LANE), F32), jax.ShapeDtypeStruct((1, LANE), F32)],
        scratch_shapes=[pltpu.VMEM((S, LANE), F32)],
        compiler_params=_cp("arbitrary"),
    )(proj, bias, dcr)


def _adamw(w, slots, m, v, name, comm=None):
    R, C = w.shape
    parts = len(slots)
    rows = R // parts
    row_bytes = 2 * C * (parts * NDEV * slots[0].dtype.itemsize + 7 * 4)
    sub = 32 // slots[0].dtype.itemsize
    tiles = [t for t in range(sub, rows + 1, sub) if rows % t == 0] or [rows]
    tr = max([t for t in tiles if t * row_bytes <= ADAMW_BLOCK_BYTES] or tiles[:1])
    per = R // parts // tr
    c1 = 1.0 - ADAM_B1 ** ADAM_STEP
    c2 = 1.0 - ADAM_B2 ** ADAM_STEP

    def body(*refs):
        w_ref, m_ref, v_ref = refs[:3]
        s_refs = refs[3:3 + parts]
        g_ref, d_ref, nm_ref, nv_ref = refs[3 + parts:]
        part = pl.program_id(0) // per
        for a, s_ref in enumerate(s_refs):
            @pl.when(part == a)
            def _():
                g = s_ref[0].astype(F32)
                for s in range(1, NDEV):
                    g = g + s_ref[s].astype(F32)
                g_ref[...] = g

        g = g_ref[...]
        mn = ADAM_B1 * m_ref[...] + (1.0 - ADAM_B1) * g
        vn = ADAM_B2 * v_ref[...] + (1.0 - ADAM_B2) * (g * g)
        nm_ref[...] = mn
        nv_ref[...] = vn
        d_ref[...] = -ADAM_LR * ((mn / c1) / (jnp.sqrt(vn / c2) + ADAM_EPS) + ADAM_WD * w_ref[...])

    row = pl.BlockSpec((tr, C), lambda i: (i, 0))
    s_specs = [pl.BlockSpec((NDEV, tr, C), lambda i, a=a: (0, jnp.clip(i - a * per, 0, per - 1), 0))
               for a in range(parts)]
    outs, moved = _pcall(
        body, name=name, grid=(R // tr,), in_specs=[row, row, row] + s_specs,
        out_specs=[row] * 4, out_shape=[jax.ShapeDtypeStruct((R, C), F32)] * 4,
        operands=[w, m, v, *slots], sem=("arbitrary",), comm=comm)
    return outs if comm is None else (outs, moved)


class _Layout:
    def __init__(self, D):
        self.GW = GW = D // 4
        self.H = H = GW // HEAD
        self.QL, self.KVL = 0, Q_LORA
        base = Q_LORA + KV_LORA
        (self.QB, self.KB, self.VB, self.QC, self.KC, self.VC, self.QD, self.KD, self.VD) = (
            base + k * GW for k in range(9))
        self.KR = base + 9 * GW
        self.FC = self.KR + LANE
        self.PW = -(-(self.FC + LANE) // 512) * 512
        self.o_kr = base
        self.o_bc = base + QK_ROPE
        self.o_fc = self.o_bc + 6 * GW
        self.o_d = self.o_fc + H
        self.IN = self.o_d + 3 * GW

    def pad(self, w):
        z = lambda n: jnp.zeros(w.shape[:-1] + (n,), w.dtype)
        return jnp.concatenate([
            w[..., :self.o_kr], w[..., self.o_bc:self.o_fc], w[..., self.o_d:self.IN],
            w[..., self.o_kr:self.o_bc], z(LANE - QK_ROPE), w[..., self.o_fc:self.o_d], z(LANE - self.H),
            z(self.PW - self.FC - LANE)], axis=-1)

    def unpad(self, g):
        return jnp.concatenate([
            g[..., :self.KR - 9 * self.GW], g[..., self.KR:self.KR + QK_ROPE], g[..., self.QB:self.QD],
            g[..., self.FC:self.FC + self.H], g[..., self.QD:self.KR]], axis=-1)


def _rope_tables(S):
    pos = jnp.arange(S, dtype=F32)

    def cs(dim):
        inv = ROPE_THETA ** (-jnp.arange(0, dim, 2, dtype=F32) / dim)
        ang = pos[:, None] * inv[None, :]
        return jnp.cos(ang), jnp.sin(ang)

    c, s = cs(HEAD)
    full = (jnp.concatenate([c, c], 1), jnp.concatenate([-s, s], 1))
    c, s = cs(QK_ROPE)
    z = jnp.zeros((S, LANE - QK_ROPE), F32)
    half = (jnp.concatenate([c, c, z], 1), jnp.concatenate([-s, s, z], 1))
    return full, half


def _dilated_table(tq, tk):
    win = max(w for w, _ in DILATED_PAIRS)
    nd = (win + tk) // tq + 1
    d = np.arange(nd)[:, None, None] * tq + np.arange(tq)[None, :, None] - np.arange(tk)[None, None, :]
    mult = np.zeros(d.shape, np.float64)
    for w, dil in DILATED_PAIRS:
        mult += (d >= 0) & (d <= w) & (d % dil == 0)
    return jnp.asarray(np.where(mult > 0, np.log(np.maximum(mult, 1.0)), NEG), F32), win


def _pack(arrs):
    rows = []
    for a in arrs:
        f = a.reshape(-1).astype(F32)
        f = jnp.pad(f, (0, (-f.shape[0]) % LANE))
        rows.append(f.reshape(-1, LANE))
    p = jnp.concatenate(rows, 0)
    return jnp.pad(p, ((0, (-p.shape[0]) % 8), (0, 0)))


def _unpack(p, shapes):
    out, r = [], 0
    for shp in shapes:
        n = int(np.prod(shp))
        nr = -(-n // LANE)
        out.append(p[r:r + nr].reshape(-1)[:n].reshape(shp))
        r += nr
    return out


def kernel(x, attn_norm, w_in, mla_q_norm, w_uq, mla_kv_norm, w_ukv, fox_forget_bias, group_norm, w_out, ffn_norm, w_gate, w_up, w_down, final_norm, loss_target, m_attn_norm, m_w_in, m_mla_q_norm, m_w_uq, m_mla_kv_norm, m_w_ukv, m_fox_forget_bias, m_group_norm, m_w_out, m_ffn_norm, m_w_gate, m_w_up, m_w_down, m_final_norm, v_attn_norm, v_w_in, v_mla_q_norm, v_w_uq, v_mla_kv_norm, v_w_ukv, v_fox_forget_bias, v_group_norm, v_w_out, v_ffn_norm, v_w_gate, v_w_up, v_w_down, v_final_norm):
    _, S, D = x.shape
    L = attn_norm.shape[0]
    lay = _Layout(D)
    H, GW, PW = lay.H, lay.GW, lay.PW
    FB = w_gate.shape[2]
    QKA = HEAD + QK_ROPE
    x = x[0]
    target = loss_target[0]
    rope_full, rope_half = _rope_tables(S)
    neg = lambda t: (t[0], -t[1])
    tab, win = _dilated_table(*_soft_tiles(S))
    cb = lambda col: col // LANE

    sh = dict(w_in=lay.pad(w_in).astype(BF16),
              **{n: w.astype(BF16) for n, w in (("w_uq", w_uq), ("w_ukv", w_ukv), ("w_out", w_out),
                                                  ("w_gate", w_gate), ("w_up", w_up), ("w_down", w_down))})
    first3 = ["w_in", "w_uq", "w_ukv"]

    def first_weights(g):
        wuq = jnp.transpose(g[1], (1, 0, 2)).reshape(Q_LORA, H, QKA)
        wuq = jnp.pad(wuq, ((0, 0), (0, 0), (0, 2 * LANE - QKA))).reshape(Q_LORA, H * 2 * LANE)
        return dict(win=g[0].reshape(D, PW), wuq=wuq,
                    wukv=jnp.transpose(g[2], (1, 0, 2)).reshape(KV_LORA, H * 2 * LANE))

    def row(a):
        return a.reshape(1, -1)

    def forward(l, x0, W):
        A = dict(x0=x0)
        A["bias"] = jnp.pad(row(fox_forget_bias[l]), ((0, 0), (0, LANE - H)))
        h1 = A["h1"] = _rms_fwd(x0, row(attn_norm[l]), D, 0, BF16, "attn_norm")
        (proj, pb), (W["wg"],) = _mm(h1, W["win"], "in_proj", out_dtype=(F32, BF16),
                                     comm=("gather", [sh["w_gate"][l]]))
        A["proj"], A["pb"] = proj, pb
        qln = A["qln"] = _rms_fwd(proj, row(mla_q_norm[l]), Q_LORA, cb(lay.QL) // 4, BF16, "q_norm")
        kvln = A["kvln"] = _rms_fwd(proj, row(mla_kv_norm[l]), KV_LORA, cb(lay.KVL) // 4, BF16, "kv_norm")
        qa, qab = _mm(qln, W["wuq"], "q_up", out_dtype=(F32, BF16))
        A["qab"] = qab
        kv = A["kv"] = _mm(kvln, W["wukv"], "kv_up", out_dtype=BF16)
        q_pe = A["q_pe"] = _rope(qa, (1, H, 2), *rope_half, True, BF16, "rope_q_mla")
        k_pe = A["k_pe"] = _rope(proj, (cb(lay.KR), 1, 1), *rope_half, True, BF16, "rope_k_mla")
        (A["o_a"], A["lse_a"]), (W["wu"],) = _attn_fwd(
            S, H, qab, lambda h: 2 * h, kv, lambda h: 2 * h, kv, lambda h: 2 * h + 1, QKA ** -0.5, "mla_fwd",
            q2=q_pe, q2cb=lambda h: h, k2=k_pe, k2cb=lambda h: 0, comm=("gather", [sh["w_up"][l]]))
        qk_b = A["qk_b"] = _rope(proj, (cb(lay.QB), 2 * H, 1), *rope_full, False, BF16, "rope_qk_dil")
        (A["o_b"], A["lse_b"]), (g_down,) = _attn_fwd(
            S, H, qk_b, lambda h: h, qk_b, lambda h: H + h, pb, lambda h: cb(lay.VB) + h, HEAD ** -0.5,
            "dilated_fwd", tab=tab, win=win, comm=("gather", [sh["w_down"][l]]))
        W["wd"] = g_down.reshape(NDEV * FB, D)
        ccol, crow = A["ccol"], A["crow"] = _fox_prep(S, H, proj, cb(lay.FC), A["bias"], "fox_prep")
        (A["o_c"], A["lse_c"]), (g_out,) = _attn_fwd(
            S, H, pb, lambda h: cb(lay.QC) + h, pb, lambda h: cb(lay.KC) + h, pb, lambda h: cb(lay.VC) + h,
            HEAD ** -0.5, "fox_fwd", ccol=ccol, crow=crow, comm=("gather", [sh["w_out"][l]]))
        W["wout"] = g_out.reshape(4 * GW, D)
        A["o_d"], A["tot_d"] = _stick_fwd(
            S, H, pb, lambda h: cb(lay.QD) + h, lambda h: cb(lay.KD) + h, lambda h: cb(lay.VD) + h,
            HEAD ** -0.5, "stick_fwd")
        mix = A["mix"] = _gn_fwd([A["o_a"], A["o_b"], A["o_c"], A["o_d"]], row(group_norm[l]), "group_norm")
        x1 = A["x1"] = _mm(mix, W["wout"], "out_proj", res=x0)
        h2 = A["h2"] = _rms_fwd(x1, row(ffn_norm[l]), D, 0, BF16, "ffn_norm")
        nxt = None
        if l + 1 < L:
            (A["g"], A["u"], A["act"]), nxt = _ffn_up(
                h2, W["wg"], W["wu"], "ffn_up", comm=("gather", [sh[n][l + 1] for n in first3]))
        else:
            A["g"], A["u"], A["act"] = _ffn_up(h2, W["wg"], W["wu"], "ffn_up_last")
        return _mm_down(A["act"], W["wd"], x1, "ffn_down"), A, nxt

    def backward(l, dx2, dx2b, W, A, late):
        proj, pb = A["proj"], A["pb"]
        G, small, got = {}, {}, {}
        dgate, dup = _ffn_dact(dx2b, W["wd"], A["g"], A["u"], "ffn_dact")
        G["w_down"] = _mm_dwdown(A["act"], dx2b, "dw_down").reshape(NDEV, FB, D)
        dh2 = _mm_dh2(dgate, W["wg"], dup, W["wu"], "ffn_dh")
        if late is None:
            G["w_gate"] = _mm_dwgate(A["h2"], dgate, "dw_gate_top")
            G["w_up"] = _mm_dwgate(A["h2"], dup, "dw_up_top")
        else:
            G["w_gate"], (got[l + 1, "w_in_a"],) = _mm_dwgate(
                A["h2"], dgate, "dw_gate", comm=("exchange", [late["w_in_a"]]))
            G["w_up"], (got[l + 1, "w_in_b"],) = _mm_dwgate(
                A["h2"], dup, "dw_up", comm=("exchange", [late["w_in_b"]]))
        dx1, dx1b, small["ffn_norm"] = _rms_bwd(A["x1"], row(ffn_norm[l]), dh2, D, 0, "ffn_norm_bwd", res=dx2)
        dmix = _mm(dx1b, W["wout"], "out_proj_dx", tb=True)
        G["w_out"] = _mm(A["mix"], dx1b, "dw_out", ta=True, out_dtype=BF16).reshape(NDEV, 4 * GW // NDEV, D)
        do_a, do_b, do_c, do_d, small["group_norm"] = _gn_bwd(
            [A["o_a"], A["o_b"], A["o_c"], A["o_d"]], row(group_norm[l]), dmix, "group_norm_bwd")
        (dq_d, dk_d, dv_d), (got[l, "w_up"],) = _stick_bwd(
            S, H, pb, lambda h: cb(lay.QD) + h, lambda h: cb(lay.KD) + h, lambda h: cb(lay.VD) + h,
            do_d, A["tot_d"], HEAD ** -0.5, "stick_bwd", comm=("exchange", [G["w_up"]]))
        (dq_c, dk_c, dv_c, dcc), (got[l, "w_down"],) = _attn_bwd(
            S, H, pb, lambda h: cb(lay.QC) + h, pb, lambda h: cb(lay.KC) + h, pb, lambda h: cb(lay.VC) + h,
            A["o_c"], do_c, A["lse_c"], HEAD ** -0.5, "fox_bwd",
            ccol=A["ccol"], crow=A["crow"], comm=("exchange", [G["w_down"]]))
        dfc, dbias = _fox_bwd(S, H, proj, cb(lay.FC), A["bias"], dcc, "fox_gate_bwd")
        small["fox_forget_bias"] = dbias[0, :H]
        qk_b = A["qk_b"]
        (dq_b, dk_b, dv_b), moved = _attn_bwd(
            S, H, qk_b, lambda h: h, qk_b, lambda h: H + h, pb, lambda h: cb(lay.VB) + h,
            A["o_b"], do_b, A["lse_b"], HEAD ** -0.5, "dilated_bwd" if late else "dilated_bwd_top", tab=tab, win=win,
            comm=("exchange", [G["w_out"]] + ([late["w_uq"], late["w_ukv"]] if late else [])))
        got[l, "w_out"] = moved[0]
        if late:
            got[l + 1, "w_uq"], got[l + 1, "w_ukv"] = moved[1:]
        dqk_b = _rope(jnp.concatenate([dq_b, dk_b], 1), (0, 2 * H, 1), *neg(rope_full), False, BF16, "rope_qk_dil_bwd")
        qab, kv = A["qab"], A["kv"]
        (dqa, dkv, dk2), (got[l, "w_gate"],) = _attn_bwd(
            S, H, qab, lambda h: 2 * h, kv, lambda h: 2 * h, kv, lambda h: 2 * h + 1,
            A["o_a"], do_a, A["lse_a"], QKA ** -0.5, "mla_bwd", q2=A["q_pe"], q2cb=lambda h: h, k2=A["k_pe"],
            k2cb=lambda h: 0, rope2=neg(rope_half), comm=("exchange", [G["w_gate"]]))
        dk_pe = _rope(dk2.astype(F32).reshape(S, H, LANE).sum(1), (0, 1, 1), *neg(rope_half), True, BF16,
                      "rope_k_mla_bwd")
        dwuq = _mm(A["qln"], dqa, "dw_uq", ta=True, out_dtype=BF16)
        dwuq = dwuq.reshape(Q_LORA, H, 2 * LANE)[:, :, :QKA].reshape(Q_LORA, NDEV, H * QKA // NDEV)
        G["w_uq"] = jnp.transpose(dwuq, (1, 0, 2))
        dwukv = _mm(A["kvln"], dkv, "dw_ukv", ta=True, out_dtype=BF16).reshape(KV_LORA, NDEV, H * 2 * LANE // NDEV)
        G["w_ukv"] = jnp.transpose(dwukv, (1, 0, 2))
        dqln = _mm(dqa, W["wuq"], "q_up_dx", tb=True)
        dkvln = _mm(dkv, W["wukv"], "kv_up_dx", tb=True)
        _, dql, small["mla_q_norm"] = _rms_bwd(proj, row(mla_q_norm[l]), dqln, Q_LORA, cb(lay.QL) // 4, "q_norm_bwd")
        _, dkvl, small["mla_kv_norm"] = _rms_bwd(
            proj, row(mla_kv_norm[l]), dkvln, KV_LORA, cb(lay.KVL) // 4, "kv_norm_bwd")
        dproj = jnp.concatenate([
            dql, dkvl, dqk_b, dv_b, dq_c, dk_c, dv_c, dq_d, dk_d, dv_d,
            dk_pe, dfc.astype(BF16), jnp.zeros((S, PW - lay.FC - LANE), BF16)], axis=1)
        g_in = _mm(A["h1"], dproj, "dw_in", ta=True, out_dtype=BF16).reshape(NDEV, D // NDEV, PW)
        half = D // NDEV // 2
        late = dict(w_in_a=g_in[:, :half], w_in_b=g_in[:, half:], w_uq=G["w_uq"], w_ukv=G["w_ukv"])
        if l == 0:
            names = ["w_in_a", "w_uq", "w_ukv"]
            dh1, moved = _mm(dproj, W["win"], "in_proj_dx_last", tb=True,
                             comm=("exchange", [late[n] for n in names]))
            got.update({(0, n): s for n, s in zip(names, moved)})
        else:
            dh1 = _mm(dproj, W["win"], "in_proj_dx", tb=True)
        dx0, dx0b, small["attn_norm"] = _rms_bwd(A["x0"], row(attn_norm[l]), dh1, D, 0, "attn_norm_bwd", res=dx1)
        return dx0, dx0b, late, got, small

    big = first3 + ["w_out", "w_gate", "w_up", "w_down"]
    Ws, As = [], []
    xc = x
    nxt = _comm_alone("gather", [sh[n][0] for n in first3], "gather_first")
    for l in range(L):
        W = first_weights(nxt)
        xc, A, nxt = forward(l, xc, W)
        Ws.append(W)
        As.append(A)
    dx, loss_part = _final_loss(xc, row(final_norm), target, "final_loss")
    dx, dxb, dfinal = _rms_bwd(xc, row(final_norm), dx, D, 0, "final_norm_bwd")
    slots = {}
    smalls = [None] * L
    late = None
    for l in reversed(range(L)):
        dx, dxb, late, got, smalls[l] = backward(l, dx, dxb, Ws[l], As[l], late)
        slots.update(got)

    names_small = ["attn_norm", "mla_q_norm", "mla_kv_norm", "fox_forget_bias", "group_norm", "ffn_norm"]
    params = dict(attn_norm=attn_norm, mla_q_norm=mla_q_norm, mla_kv_norm=mla_kv_norm, fox_forget_bias=fox_forget_bias,
                  group_norm=group_norm, ffn_norm=ffn_norm, final_norm=final_norm, w_in=w_in, w_uq=w_uq, w_ukv=w_ukv,
                  w_out=w_out, w_gate=w_gate, w_up=w_up, w_down=w_down)
    moms = dict(attn_norm=(m_attn_norm, v_attn_norm), mla_q_norm=(m_mla_q_norm, v_mla_q_norm),
                mla_kv_norm=(m_mla_kv_norm, v_mla_kv_norm), fox_forget_bias=(m_fox_forget_bias, v_fox_forget_bias),
                group_norm=(m_group_norm, v_group_norm), ffn_norm=(m_ffn_norm, v_ffn_norm),
                final_norm=(m_final_norm, v_final_norm), w_in=(m_w_in, v_w_in), w_uq=(m_w_uq, v_w_uq),
                w_ukv=(m_w_ukv, v_w_ukv), w_out=(m_w_out, v_w_out), w_gate=(m_w_gate, v_w_gate),
                w_up=(m_w_up, v_w_up), w_down=(m_w_down, v_w_down))
    small_list = names_small + ["final_norm"]
    small_grads = [jnp.stack([smalls[l][n].reshape(params[n].shape[1:]) for l in range(L)]) for n in names_small]
    small_grads.append(dfinal.reshape(final_norm.shape))
    shapes = [params[n].shape for n in small_list] + [(LANE,)]
    packed_g = _comm_alone("gather", [_pack(small_grads + [loss_part.reshape(LANE)])], "gather_small")[0]
    zero = jnp.zeros((LANE,), F32)
    res_small = _adamw(_pack([params[n] for n in small_list] + [zero]), [packed_g],
                       _pack([moms[n][0] for n in small_list] + [zero]),
                       _pack([moms[n][1] for n in small_list] + [zero]), "adamw_small")
    unp = [_unpack(r, shapes) for r in res_small]
    out = {n: tuple(unp[k][i] for k in range(4)) for i, n in enumerate(small_list)}
    loss = unp[0][-1][0]

    for n in ["w_gate", "w_up", "w_down", "w_out", "w_uq", "w_ukv", "w_in"]:
        if n == "w_in":
            st = [lay.unpad(slots[l, n + h]) for l in range(L) for h in ("_a", "_b")]
        else:
            st = [slots[l, n] for l in range(L)]
        C = st[0].shape[-1]
        st = [s.reshape(NDEV, -1, C) for s in st]
        args = (params[n].reshape(-1, C), st, moms[n][0].reshape(-1, C), moms[n][1].reshape(-1, C), "adamw_" + n)
        if n == "w_gate":
            res, (slots[0, "w_in_b"],) = _adamw(*args, comm=("exchange", [late["w_in_b"]]))
        else:
            res = _adamw(*args)
        out[n] = tuple(r.reshape(params[n].shape) for r in res)

    order = ["attn_norm", "w_in", "mla_q_norm", "w_uq", "mla_kv_norm", "w_ukv", "fox_forget_bias", "group_norm",
             "w_out", "ffn_norm", "w_gate", "w_up", "w_down", "final_norm"]
    return (loss, dx[None], *[out[n][0] for n in order], *[out[n][1] for n in order],
            *[out[n][2] for n in order], *[out[n][3] for n in order])
```

```python
import functools
import math

import numpy as np
import jax
import jax.numpy as jnp
from jax import lax
from jax.experimental import pallas as pl
from jax.experimental.pallas import tpu as pltpu

F32 = jnp.float32
BF16 = jnp.bfloat16
NDEV = 8
LANE = 128
HEAD = 128
Q_LORA = 512
KV_LORA = 512
QK_ROPE = 64
DILATED_PAIRS = ((128, 1), (512, 4), (2048, 16))
ROPE_THETA = 10000.0
EPS = 1e-6
NEG = -1e30
TQ = 512
TK = 128
TKS = 512
VMEM_LIMIT = 48 * 1024 * 1024
MM_OPERAND_BYTES = 20 * 1024 * 1024
FFN_BLOCKS_PER_STEP = 2
ADAMW_BLOCK_BYTES = 24 * 1024 * 1024
ADAM_LR, ADAM_B1, ADAM_B2, ADAM_EPS, ADAM_WD, ADAM_STEP = 0.001, 0.9, 0.999, 1e-08, 0.01, 10
MESH = pl.DeviceIdType.MESH
ANY = pl.BlockSpec(memory_space=pl.ANY)


def _cp(*sem):
    return pltpu.CompilerParams(dimension_semantics=sem, vmem_limit_bytes=VMEM_LIMIT)


def _dot(a, b, ca, cb):
    return lax.dot_general(a, b, (((ca,), (cb,)), ((), ())), preferred_element_type=F32)


def _dot_nn(a, b):
    return _dot(a, b, 1, 0)


def _dot_nt(a, b):
    return _dot(a, b, 1, 1)


def _dot_tn(a, b):
    return _dot(a, b, 0, 0)


def _tile(n, t):
    if n <= t:
        return n
    t -= t % LANE
    while n % t:
        t -= LANE
    return t


def _direct_copies(ins, outs, send_sems, recv_sems, local_sems, want_recvs=True):
    x, y, c = lax.axis_index("x"), lax.axis_index("y"), lax.axis_index("c")
    my_id = 4 * x + 2 * y + c
    local, sends, recvs = [], [], []
    for a in range(len(ins)):
        mine = ins[a].at[my_id]
        local.append(pltpu.make_async_copy(mine, outs[a].at[my_id], local_sems.at[a]))
        for k in range(1, NDEV):
            peer = (1 - x if k & 4 else x, 1 - y if k & 2 else y, 1 - c if k & 1 else c)
            pid = 4 * peer[0] + 2 * peer[1] + peer[2]
            sems = dict(send_sem=send_sems.at[a, k - 1], recv_sem=recv_sems.at[a, k - 1],
                        device_id=peer, device_id_type=MESH)
            sends.append(pltpu.make_async_remote_copy(src_ref=ins[a].at[pid], dst_ref=outs[a].at[my_id], **sems))
            if want_recvs:
                recvs.append(pltpu.make_async_remote_copy(src_ref=mine, dst_ref=outs[a].at[pid], **sems))
    return local, sends, recvs


def _comm_start(kind, ins, outs, send_sems, recv_sems, local_sems):
    if kind == "exchange":
        local, sends, _ = _direct_copies(ins, outs, send_sems, recv_sems, local_sems, want_recvs=False)
        for cp in local + sends:
            cp.start()
        return
    x, y, c = lax.axis_index("x"), lax.axis_index("y"), lax.axis_index("c")
    for a in range(len(ins)):
        mine = outs[a].at[4 * x + 2 * y + c]
        pltpu.make_async_copy(ins[a], mine, local_sems.at[a]).start()
        for k, to in enumerate([(x, y, 1 - c), (1 - x, y, c), (x, 1 - y, c), (1 - x, 1 - y, c)]):
            pltpu.make_async_remote_copy(src_ref=ins[a], dst_ref=mine, send_sem=send_sems.at[a, k],
                                         recv_sem=recv_sems.at[a, k], device_id=to, device_id_type=MESH).start()


def _comm_finish(kind, ins, outs, send_sems, recv_sems, local_sems):
    if kind == "exchange":
        local, sends, recvs = _direct_copies(ins, outs, send_sems, recv_sems, local_sems)
        for cp in recvs:
            cp.wait_recv()
        for cp in sends:
            cp.wait_send()
        for cp in local:
            cp.wait()
        return
    x, y, c = lax.axis_index("x"), lax.axis_index("y"), lax.axis_index("c")
    sibling = (x, y, 1 - c)
    chips = [(1 - x, y), (x, 1 - y), (1 - x, 1 - y)]
    for a in range(len(ins)):
        def copy(k, block, to):
            rows = outs[a].at[4 * block[0] + 2 * block[1] + block[2]]
            return pltpu.make_async_remote_copy(src_ref=rows, dst_ref=rows, send_sem=send_sems.at[a, k],
                                                recv_sem=recv_sems.at[a, k], device_id=to, device_id_type=MESH)

        passed = []
        for j, chip in enumerate(chips):
            copy(1 + j, (*chip, c), (x, y, c)).wait_recv()
            passed.append(copy(4 + j, (*chip, c), sibling))
            passed[-1].start()
        copy(0, sibling, (x, y, c)).wait_recv()
        for j, chip in enumerate(chips):
            copy(4 + j, (*chip, 1 - c), (x, y, c)).wait_recv()
        for k in range(4):
            copy(k, (x, y, c), sibling).wait_send()
        for cp in passed:
            cp.wait_send()
        pltpu.make_async_copy(ins[a], outs[a].at[4 * x + 2 * y + c], local_sems.at[a]).wait()


def _comm_shapes(kind, arrs):
    out_shape = [jax.ShapeDtypeStruct(((NDEV,) if kind == "gather" else ()) + a.shape, a.dtype) for a in arrs]
    n = len(arrs)
    sems = [pltpu.SemaphoreType.DMA((n, 7)), pltpu.SemaphoreType.DMA((n, 7)), pltpu.SemaphoreType.DMA((n,))]
    return out_shape, sems


def _comm_alone(kind, arrs, name):
    n = len(arrs)

    def body(*refs):
        _comm_start(kind, refs[:n], refs[n:2 * n], *refs[2 * n:])
        _comm_finish(kind, refs[:n], refs[n:2 * n], *refs[2 * n:])

    out_shape, sems = _comm_shapes(kind, arrs)
    return pl.pallas_call(body, name=name, out_shape=out_shape, in_specs=[ANY] * n, out_specs=[ANY] * n,
                          scratch_shapes=sems)(*arrs)


def _pcall(body, *, name, grid, in_specs, out_specs, out_shape, operands, sem, scratch_shapes=(), comm=None):
    in_specs, out_specs, out_shape = list(in_specs), list(out_specs), list(out_shape)
    scratch_shapes = list(scratch_shapes)
    if comm is None:
        res = pl.pallas_call(body, name=name, grid=grid, in_specs=in_specs, out_specs=out_specs, out_shape=out_shape,
                             scratch_shapes=scratch_shapes, compiler_params=_cp(*sem))(*operands)
        return list(res), []
    kind, arrs = comm
    nc, n_in, n_out, n_scr = len(arrs), len(operands), len(out_shape), len(scratch_shapes)
    c_shape, c_sems = _comm_shapes(kind, arrs)

    def carrier(*refs):
        ins, cin = refs[:n_in], refs[n_in:n_in + nc]
        outs = refs[n_in + nc:n_in + nc + n_out]
        cout = refs[n_in + nc + n_out:n_in + 2 * nc + n_out]
        scr = refs[n_in + 2 * nc + n_out:n_in + 2 * nc + n_out + n_scr]
        sems = refs[n_in + 2 * nc + n_out + n_scr:]
        pids = [pl.program_id(d) for d in range(len(grid))]
        first = functools.reduce(jnp.logical_and, [p == 0 for p in pids])
        last = functools.reduce(jnp.logical_and, [p == g - 1 for p, g in zip(pids, grid)])

        @pl.when(first)
        def _():
            _comm_start(kind, cin, cout, *sems)

        body(*ins, *outs, *scr)

        @pl.when(last)
        def _():
            _comm_finish(kind, cin, cout, *sems)

    res = pl.pallas_call(
        carrier, name=name, grid=grid, in_specs=in_specs + [ANY] * nc, out_specs=out_specs + [ANY] * nc,
        out_shape=out_shape + c_shape, scratch_shapes=scratch_shapes + c_sems,
        compiler_params=_cp(*["arbitrary"] * len(grid)))(*operands, *arrs)
    return list(res[:n_out]), list(res[n_out:])


def _mm_call(pairs, grid, a_spec, b_spec, o_spec, out_shape, acc_shape, nk, ca, cb, name,
             res=None, res_spec=None, comm=None):
    npairs = len(pairs)
    multi = isinstance(out_shape, (list, tuple))
    nout = len(out_shape) if multi else 1

    def body(*refs):
        ab = refs[:2 * npairs]
        r_ref = refs[2 * npairs] if res is not None else None
        o_refs, acc = refs[-1 - nout:-1], refs[-1]
        k = pl.program_id(2)

        @pl.when(k == 0)
        def _():
            acc[...] = jnp.zeros_like(acc)

        tot = None
        for p in range(npairs):
            av, bv = ab[2 * p][...].astype(BF16), ab[2 * p + 1][...].astype(BF16)
            if av.ndim == 2:
                terms = [(av, bv)]
            else:
                rows = bv.shape[0] // av.shape[0]
                terms = [(av[q], bv[q] if bv.ndim == 3 else bv[q * rows:(q + 1) * rows]) for q in range(av.shape[0])]
            for at, bt in terms:
                d = _dot(at, bt, ca, cb)
                tot = d if tot is None else tot + d
        acc[...] += tot

        @pl.when(k == nk - 1)
        def _():
            r = acc[...]
            if r_ref is not None:
                r = r + r_ref[...]
            for o_ref in o_refs:
                o_ref[...] = r.astype(o_ref.dtype)

    ops, specs = [], []
    for a, b in pairs:
        ops += [a, b]
        specs += [a_spec, b_spec]
    if res is not None:
        ops.append(res)
        specs.append(res_spec)
    outs, moved = _pcall(
        body, name=name, grid=grid, in_specs=specs, out_specs=[o_spec] * nout,
        out_shape=out_shape if multi else [out_shape], operands=ops,
        scratch_shapes=[pltpu.VMEM(acc_shape, F32)], sem=("parallel", "parallel", "arbitrary"), comm=comm)
    outs = outs if multi else outs[0]
    return outs if comm is None else (outs, moved)


def _k_tile(K, row_bytes, tk=2048):
    tk = _tile(K, tk)
    while 2 * tk * row_bytes > MM_OPERAND_BYTES and tk % 256 == 0:
        tk //= 2
    return tk


def _mm(a, b, name, ta=False, tb=False, out_dtype=F32, res=None, tm=1024, tn=1024, comm=None):
    M, K = (a.shape[1], a.shape[0]) if ta else a.shape
    N = b.shape[0] if tb else b.shape[1]
    tm, tn = _tile(M, tm), _tile(N, tn)
    tk = _k_tile(K, tm * a.dtype.itemsize + tn * b.dtype.itemsize)
    a_spec = pl.BlockSpec((tk, tm), lambda i, j, k: (k, i)) if ta else pl.BlockSpec((tm, tk), lambda i, j, k: (i, k))
    b_spec = pl.BlockSpec((tn, tk), lambda i, j, k: (j, k)) if tb else pl.BlockSpec((tk, tn), lambda i, j, k: (k, j))
    o_spec = pl.BlockSpec((tm, tn), lambda i, j, k: (i, j))
    if isinstance(out_dtype, tuple):
        out_shape = [jax.ShapeDtypeStruct((M, N), d) for d in out_dtype]
    else:
        out_shape = jax.ShapeDtypeStruct((M, N), out_dtype)
    return _mm_call([(a, b)], (M // tm, N // tn, K // tk), a_spec, b_spec, o_spec,
                    out_shape, (tm, tn), K // tk,
                    0 if ta else 1, 1 if tb else 0, name, res=res, res_spec=o_spec, comm=comm)


def _mm_down(act, wd, res, name, tm=1024, tn=1024):
    _, S, FB = act.shape
    D = wd.shape[1]
    tm, tn = _tile(S, tm), _tile(D, tn)
    o_spec = pl.BlockSpec((tm, tn), lambda i, j, k: (i, j))
    nb = FFN_BLOCKS_PER_STEP
    return _mm_call([(act, wd)], (S // tm, D // tn, NDEV // nb),
                    pl.BlockSpec((nb, tm, FB), lambda i, j, k: (k, i, 0)),
                    pl.BlockSpec((nb * FB, tn), lambda i, j, k: (k, j)), o_spec,
                    jax.ShapeDtypeStruct((S, D), F32), (tm, tn), NDEV // nb, 1, 0, name, res=res, res_spec=o_spec)


def _mm_dwdown(act, dy, name, tn=1024):
    _, S, FB = act.shape
    D = dy.shape[1]
    tn = _tile(D, tn)
    tk = _k_tile(S, FB * act.dtype.itemsize + tn * dy.dtype.itemsize)
    return _mm_call([(act, dy)], (NDEV, D // tn, S // tk),
                    pl.BlockSpec((None, tk, FB), lambda i, j, k: (i, k, 0)),
                    pl.BlockSpec((tk, tn), lambda i, j, k: (k, j)),
                    pl.BlockSpec((FB, tn), lambda i, j, k: (i, j)),
                    jax.ShapeDtypeStruct((NDEV * FB, D), BF16), (FB, tn), S // tk, 0, 0, name)


def _mm_dh2(dg, wg, du, wu, name, tm=1024, tn=1024, comm=None):
    _, S, FB = dg.shape
    D = wg.shape[1]
    tm, tn = _tile(S, tm), _tile(D, tn)
    nb = FFN_BLOCKS_PER_STEP
    return _mm_call([(dg, wg), (du, wu)], (S // tm, D // tn, NDEV // nb),
                    pl.BlockSpec((nb, tm, FB), lambda i, j, k: (k, i, 0)),
                    pl.BlockSpec((nb, tn, FB), lambda i, j, k: (k, j, 0)),
                    pl.BlockSpec((tm, tn), lambda i, j, k: (i, j)),
                    jax.ShapeDtypeStruct((S, D), F32), (tm, tn), NDEV // nb, 1, 1, name, comm=comm)


def _mm_dwgate(h2, dg, name, tm=1024, comm=None):
    _, S, FB = dg.shape
    D = h2.shape[1]
    tm = _tile(D, tm)
    tk = _k_tile(S, tm * h2.dtype.itemsize + FB * dg.dtype.itemsize)
    return _mm_call([(h2, dg)], (NDEV, D // tm, S // tk),
                    pl.BlockSpec((tk, tm), lambda p, i, k: (k, i)),
                    pl.BlockSpec((None, tk, FB), lambda p, i, k: (p, k, 0)),
                    pl.BlockSpec((None, tm, FB), lambda p, i, k: (p, i, 0)),
                    jax.ShapeDtypeStruct((NDEV, D, FB), BF16), (tm, FB), S // tk, 0, 0, name, comm=comm)


def _ffn_up(h2, wg, wu, name, tm=512, comm=None):
    S, D = h2.shape
    FB = wg.shape[2]
    tm = _tile(S, tm)

    def body(h_ref, wg_ref, wu_ref, g_ref, u_ref, act_ref):
        h = h_ref[...]
        g = _dot_nn(h, wg_ref[...])
        u = _dot_nn(h, wu_ref[...])
        g_ref[...] = g.astype(BF16)
        u_ref[...] = u.astype(BF16)
        act_ref[...] = (g / (1.0 + jnp.exp(-g)) * u).astype(BF16)

    w_spec = pl.BlockSpec((None, D, FB), lambda p, i: (p, 0, 0))
    o_spec = pl.BlockSpec((None, tm, FB), lambda p, i: (p, i, 0))
    shp = (NDEV, S, FB)
    outs, moved = _pcall(
        body, name=name, grid=(NDEV, S // tm),
        in_specs=[pl.BlockSpec((tm, D), lambda p, i: (i, 0)), w_spec, w_spec],
        out_specs=[o_spec, o_spec, o_spec],
        out_shape=[jax.ShapeDtypeStruct(shp, BF16)] * 3,
        operands=[h2, wg, wu], sem=("parallel", "parallel"), comm=comm)
    return outs if comm is None else (outs, moved)


def _ffn_dact(dy, wd, g, u, name, tm=512):
    S, D = dy.shape
    FB = g.shape[2]
    tm = _tile(S, tm)

    def body(dy_ref, wd_ref, g_ref, u_ref, dg_ref, du_ref):
        dact = _dot_nt(dy_ref[...], wd_ref[...])
        gv = g_ref[...].astype(F32)
        sg = pl.reciprocal(1.0 + jnp.exp(-gv), approx=True)
        dg_ref[...] = (dact * u_ref[...].astype(F32) * (sg * (1.0 + gv * (1.0 - sg)))).astype(BF16)
        du_ref[...] = (dact * (gv * sg)).astype(BF16)

    t_spec = pl.BlockSpec((None, tm, FB), lambda p, i: (p, i, 0))
    shp = jax.ShapeDtypeStruct((NDEV, S, FB), BF16)
    return pl.pallas_call(
        body, name=name, grid=(NDEV, S // tm),
        in_specs=[pl.BlockSpec((tm, D), lambda p, i: (i, 0)), pl.BlockSpec((FB, D), lambda p, i: (p, 0)),
                  t_spec, t_spec],
        out_specs=[t_spec, t_spec], out_shape=[shp, shp],
        compiler_params=_cp("parallel", "parallel"),
    )(dy, wd, g, u)


def _rms_fwd(x, gain, width, cb, out_dtype, name, ts=512):
    S = x.shape[0]
    ts = _tile(S, ts)

    def body(x_ref, g_ref, o_ref):
        xv = x_ref[...]
        r = lax.rsqrt(jnp.mean(xv * xv, axis=1, keepdims=True) + EPS)
        o_ref[...] = (xv * r * g_ref[...]).astype(o_ref.dtype)

    return pl.pallas_call(
        body, name=name, grid=(S // ts,),
        in_specs=[pl.BlockSpec((ts, width), lambda i: (i, cb)), pl.BlockSpec((1, width), lambda i: (0, 0))],
        out_specs=pl.BlockSpec((ts, width), lambda i: (i, 0)),
        out_shape=jax.ShapeDtypeStruct((S, width), out_dtype),
        compiler_params=_cp("parallel"),
    )(x, gain)


def _rms_bwd(x, gain, dy, width, cb, name, res=None, ts=256):
    S = x.shape[0]
    ts = _tile(S, ts)
    has_res = res is not None

    def body(*refs):
        x_ref, g_ref, dy_ref = refs[:3]
        r_ref = refs[3] if has_res else None
        dx_ref, dxb_ref, dg_ref = refs[-3], refs[-2], refs[-1]

        @pl.when(pl.program_id(0) == 0)
        def _():
            dg_ref[...] = jnp.zeros_like(dg_ref)

        xv = x_ref[...]
        r = lax.rsqrt(jnp.mean(xv * xv, axis=1, keepdims=True) + EPS)
        xh = xv * r
        dyv = dy_ref[...]
        dyg = dyv * g_ref[...]
        dx = r * (dyg - xh * jnp.mean(dyg * xh, axis=1, keepdims=True))
        if has_res:
            dx = dx + r_ref[...]
        dx_ref[...] = dx
        dxb_ref[...] = dx.astype(BF16)
        dg_ref[...] += jnp.sum(dyv * xh, axis=0, keepdims=True)

    row = pl.BlockSpec((ts, width), lambda i: (i, 0))
    vec = pl.BlockSpec((1, width), lambda i: (0, 0))
    ops = [x, gain, dy] + ([res] if has_res else [])
    specs = [pl.BlockSpec((ts, width), lambda i: (i, cb)), vec, row] + ([row] if has_res else [])
    return pl.pallas_call(
        body, name=name, grid=(S // ts,), in_specs=specs, out_specs=[row, row, vec],
        out_shape=[jax.ShapeDtypeStruct((S, width), F32), jax.ShapeDtypeStruct((S, width), BF16),
                   jax.ShapeDtypeStruct((1, width), F32)],
        compiler_params=_cp("arbitrary"),
    )(*ops)


def _gn_fwd(outs, gain, name, ts=512):
    S, GW = outs[0].shape
    ts = _tile(S, ts)

    def body(a_ref, b_ref, c_ref, d_ref, g_ref, o_ref):
        for g, r_ref in enumerate((a_ref, b_ref, c_ref, d_ref)):
            xv = r_ref[...]
            r = lax.rsqrt(jnp.mean(xv * xv, axis=1, keepdims=True) + EPS)
            o_ref[:, g * GW:(g + 1) * GW] = (xv * r * g_ref[:, g * GW:(g + 1) * GW]).astype(BF16)

    row = pl.BlockSpec((ts, GW), lambda i: (i, 0))
    return pl.pallas_call(
        body, name=name, grid=(S // ts,),
        in_specs=[row] * 4 + [pl.BlockSpec((1, 4 * GW), lambda i: (0, 0))],
        out_specs=pl.BlockSpec((ts, 4 * GW), lambda i: (i, 0)),
        out_shape=jax.ShapeDtypeStruct((S, 4 * GW), BF16),
        compiler_params=_cp("parallel"),
    )(*outs, gain)


def _gn_bwd(outs, gain, dmix, name, ts=256):
    S, GW = outs[0].shape
    ts = _tile(S, ts)

    def body(a_ref, b_ref, c_ref, d_ref, g_ref, dm_ref, da_ref, db_ref, dc_ref, dd_ref, dg_ref):
        @pl.when(pl.program_id(0) == 0)
        def _():
            dg_ref[...] = jnp.zeros_like(dg_ref)

        for g, (r_ref, o_ref) in enumerate(zip((a_ref, b_ref, c_ref, d_ref), (da_ref, db_ref, dc_ref, dd_ref))):
            sl = slice(g * GW, (g + 1) * GW)
            xv = r_ref[...]
            r = lax.rsqrt(jnp.mean(xv * xv, axis=1, keepdims=True) + EPS)
            xh = xv * r
            dyv = dm_ref[:, sl]
            dyg = dyv * g_ref[:, sl]
            o_ref[...] = (r * (dyg - xh * jnp.mean(dyg * xh, axis=1, keepdims=True))).astype(BF16)
            dg_ref[:, sl] += jnp.sum(dyv * xh, axis=0, keepdims=True)

    row = pl.BlockSpec((ts, GW), lambda i: (i, 0))
    vec = pl.BlockSpec((1, 4 * GW), lambda i: (0, 0))
    return pl.pallas_call(
        body, name=name, grid=(S // ts,),
        in_specs=[row] * 4 + [vec, pl.BlockSpec((ts, 4 * GW), lambda i: (i, 0))],
        out_specs=[row] * 4 + [vec],
        out_shape=[jax.ShapeDtypeStruct((S, GW), BF16)] * 4 + [jax.ShapeDtypeStruct((1, 4 * GW), F32)],
        compiler_params=_cp("arbitrary"),
    )(*outs, gain, dmix)


def _rope_partner(x, half):
    if not half:
        return pltpu.roll(x, 64, 1)
    lane = lax.broadcasted_iota(jnp.int32, x.shape, 1)
    return jnp.where(lane % 64 < 32, pltpu.roll(x, LANE - 32, 1), pltpu.roll(x, 32, 1))


def _rope(x, cbs, cos, sin, half, out_dtype, name, ts=512):
    S = x.shape[0]
    cb0, nb, stride = cbs
    ts = _tile(S, ts)

    def body(x_ref, c_ref, s_ref, o_ref):
        xv = x_ref[...].astype(F32)
        o_ref[...] = (xv * c_ref[...] + _rope_partner(xv, half) * s_ref[...]).astype(o_ref.dtype)

    tab = pl.BlockSpec((ts, LANE), lambda i, j: (i, 0))
    return pl.pallas_call(
        body, name=name, grid=(S // ts, nb),
        in_specs=[pl.BlockSpec((ts, LANE), lambda i, j: (i, cb0 + stride * j)), tab, tab],
        out_specs=pl.BlockSpec((ts, LANE), lambda i, j: (i, j)),
        out_shape=jax.ShapeDtypeStruct((S, nb * LANE), out_dtype),
        compiler_params=_cp("parallel", "parallel"),
    )(x, cos, sin)


def _final_loss(x, gain, target, name, ts=256):
    S, D = x.shape
    ts = _tile(S, ts)

    def body(x_ref, g_ref, t_ref, dy_ref, l_ref):
        @pl.when(pl.program_id(0) == 0)
        def _():
            l_ref[...] = jnp.zeros_like(l_ref)

        xv = x_ref[...]
        r = lax.rsqrt(jnp.mean(xv * xv, axis=1, keepdims=True) + EPS)
        err = xv * r * g_ref[...] - t_ref[...]
        dy_ref[...] = err * (1.0 / D)
        part = jnp.sum(jnp.mean(err * err, axis=1, keepdims=True), axis=0, keepdims=True)
        l_ref[...] += jnp.broadcast_to(0.5 * part, (1, LANE))

    row = pl.BlockSpec((ts, D), lambda i: (i, 0))
    return pl.pallas_call(
        body, name=name, grid=(S // ts,),
        in_specs=[row, pl.BlockSpec((1, D), lambda i: (0, 0)), row],
        out_specs=[row, pl.BlockSpec((1, LANE), lambda i: (0, 0))],
        out_shape=[jax.ShapeDtypeStruct((S, D), F32), jax.ShapeDtypeStruct((1, LANE), F32)],
        compiler_params=_cp("arbitrary"),
    )(x, gain, target)


def _colspec(rows, f):
    return pl.BlockSpec((rows, LANE), f)


def _soft_tiles(S):
    tq = _tile(S, TQ)
    tk = _tile(S, TKS)
    assert tk % tq == 0
    return tq, tk


def _key_row(crow_ref, j, tk):
    n = tk // TK
    return jnp.concatenate([crow_ref[j * n + c] for c in range(n)], axis=1)


def _attn_fwd(S, H, q1, q1cb, k1, k1cb, v, vcb, scale, name, q2=None, q2cb=None, k2=None, k2cb=None,
              tab=None, win=None, ccol=None, crow=None, comm=None):
    tq, tk = _soft_tiles(S)
    has2, hastab, hasc = q2 is not None, tab is not None, ccol is not None

    def body(*refs):
        it = iter(refs)
        q1r, k1r, vr = next(it), next(it), next(it)
        q2r, k2r = (next(it), next(it)) if has2 else (None, None)
        tabr = next(it) if hastab else None
        ccolr, crowr = (next(it), next(it)) if hasc else (None, None)
        o_ref, lse_ref = next(it), next(it)
        i = pl.program_id(1)
        q = q1r[...]
        qb2 = q2r[...] if has2 else None
        cq = ccolr[:, 0:1] if hasc else None
        qpos = i * tq + lax.broadcasted_iota(jnp.int32, (tq, tk), 0)
        kio = lax.broadcasted_iota(jnp.int32, (tq, tk), 1)
        j_diag = (i * tq) // tk
        j_lo = jnp.maximum((i * tq - win) // tk, 0) if win else 0

        if has2:
            q = jnp.concatenate([q, qb2], axis=1)

        def step(j, carry, masked):
            m, l, acc = carry
            off = pl.multiple_of(j * tk, tk)
            kb = k1r[pl.ds(off, tk), :]
            if has2:
                kb = jnp.concatenate([kb, k2r[pl.ds(off, tk), :]], axis=1)
            s = _dot_nt(q, kb) * scale
            if hastab:
                s = s + tabr[i - j * (tk // tq)]
            else:
                if hasc:
                    s = s + (cq - _key_row(crowr, j, tk))
                if masked:
                    s = jnp.where(kio + j * tk <= qpos, s, NEG)
            mn = jnp.maximum(m, jnp.max(s, axis=1, keepdims=True))
            p = jnp.exp(s - mn)
            al = jnp.exp(m - mn)
            l = al * l + jnp.sum(p, axis=1, keepdims=True)
            vb = vr[pl.ds(off, tk), :]
            ph = p.astype(BF16)
            pv = _dot_nn(ph, vb)
            if hasc:
                pv = pv + _dot_nn((p - ph.astype(F32)).astype(BF16), vb)
            acc = al * acc + pv
            return mn, l, acc

        carry = (jnp.full((tq, 1), NEG, F32), jnp.zeros((tq, 1), F32), jnp.zeros((tq, LANE), F32))
        if hastab:
            carry = lax.fori_loop(j_lo, j_diag + 1, functools.partial(step, masked=False), carry)
        else:
            carry = lax.fori_loop(j_lo, j_diag, functools.partial(step, masked=False), carry)
            carry = step(j_diag, carry, True)
        m, l, acc = carry
        o_ref[...] = acc / l
        lse_ref[...] = jnp.broadcast_to(m + jnp.log(l), (tq, LANE))

    ops = [q1, k1, v]
    specs = [_colspec(tq, lambda h, i: (i, q1cb(h))), _colspec(S, lambda h, i: (0, k1cb(h))),
             _colspec(S, lambda h, i: (0, vcb(h)))]
    if has2:
        ops += [q2, k2]
        specs += [_colspec(tq, lambda h, i: (i, q2cb(h))), _colspec(S, lambda h, i: (0, k2cb(h)))]
    if hastab:
        ops.append(tab)
        specs.append(pl.BlockSpec(tab.shape, lambda h, i: (0, 0, 0)))
    if hasc:
        ops += [ccol, crow]
        specs += [_colspec(tq, lambda h, i: (i, h)),
                  pl.BlockSpec((None, S // TK, 1, TK), lambda h, i: (h, 0, 0, 0))]
    o_spec = _colspec(tq, lambda h, i: (i, h))
    shp = jax.ShapeDtypeStruct((S, H * LANE), F32)
    outs, moved = _pcall(
        body, name=name, grid=(H, S // tq), in_specs=specs, out_specs=[o_spec, o_spec], out_shape=[shp, shp],
        operands=ops, sem=("parallel", "arbitrary"), comm=comm)
    return outs if comm is None else (outs, moved)


def _attn_bwd(S, H, q1, q1cb, k1, k1cb, v, vcb, o, do, lse, scale, name, q2=None, q2cb=None, k2=None, k2cb=None,
              rope2=None, tab=None, win=None, ccol=None, crow=None, comm=None):
    tq, tk = _soft_tiles(S)
    has2, hastab, hasc = q2 is not None, tab is not None, ccol is not None

    def body(*refs):
        it = iter(refs)
        q1r, k1r, vr, o_r, do_r, lse_r = (next(it) for _ in range(6))
        q2r, k2r, cos2_r, sin2_r = (next(it), next(it), next(it), next(it)) if has2 else (None,) * 4
        tabr = next(it) if hastab else None
        ccolr, crowr = (next(it), next(it)) if hasc else (None, None)
        dq1_r, dk1_o, dv_o = next(it), next(it), next(it)
        dk2_o = dv_o
        dcr_r = next(it) if hasc else None
        dk1_r, dv_r = next(it), next(it)
        dk2_r = next(it) if has2 else None
        i = pl.program_id(1)

        @pl.when(i == 0)
        def _():
            dk1_r[...] = jnp.zeros_like(dk1_r)
            dv_r[...] = jnp.zeros_like(dv_r)
            if has2:
                dk2_r[...] = jnp.zeros_like(dk2_r)
            if hasc:
                dcr_r[...] = jnp.zeros_like(dcr_r)

        q = q1r[...]
        qb2 = q2r[...] if has2 else None
        dob = do_r[...]
        delta = jnp.sum(dob.astype(F32) * o_r[...], axis=1, keepdims=True)
        lse_c = lse_r[:, 0:1]
        cq = ccolr[:, 0:1] if hasc else None
        qpos = i * tq + lax.broadcasted_iota(jnp.int32, (tq, tk), 0)
        kio = lax.broadcasted_iota(jnp.int32, (tq, tk), 1)
        j_diag = (i * tq) // tk
        j_lo = jnp.maximum((i * tq - win) // tk, 0) if win else 0

        if has2:
            q = jnp.concatenate([q, qb2], axis=1)

        def probs(j, masked):
            off = pl.multiple_of(j * tk, tk)
            kb = k1r[pl.ds(off, tk), :]
            if has2:
                kb = jnp.concatenate([kb, k2r[pl.ds(off, tk), :]], axis=1)
            s = _dot_nt(q, kb) * scale
            if hastab:
                s = s + tabr[i - j * (tk // tq)]
            else:
                if hasc:
                    s = s + (cq - _key_row(crowr, j, tk))
                if masked:
                    s = jnp.where(kio + j * tk <= qpos, s, NEG)
            p = jnp.exp(s - lse_c)
            dp = _dot_nt(dob, vr[pl.ds(off, tk), :])
            return off, kb, p, dp

        def sweep(fn, carry):
            if hastab:
                return lax.fori_loop(j_lo, j_diag + 1, functools.partial(fn, masked=False), carry)
            carry = lax.fori_loop(j_lo, j_diag, functools.partial(fn, masked=False), carry)
            return fn(j_diag, carry, True)

        def step(j, dq, masked):
            off, kb, p, dp = probs(j, masked)
            ds = p * (dp - delta)
            dsb = ds.astype(BF16)
            dk = _dot_tn(dsb, q) * scale
            dk1_r[pl.ds(off, tk), :] += dk[:, :LANE]
            if has2:
                dk2_r[pl.ds(off, tk), :] += dk[:, LANE:]
            dv_r[pl.ds(off, tk), :] += _dot_tn(p.astype(BF16), dob)
            if hasc:
                cs = -jnp.sum(ds, axis=0, keepdims=True)
                for c in range(tk // TK):
                    dcr_r[j * (tk // TK) + c] += cs[:, c * TK:(c + 1) * TK]
            return dq + _dot_nn(dsb, kb)

        dq = sweep(step, jnp.zeros(q.shape, F32)) * scale
        dq1_r[:, :LANE] = dq[:, :LANE].astype(BF16)
        if has2:
            x2 = dq[:, LANE:]
            dq1_r[:, LANE:] = (x2 * cos2_r[...] + _rope_partner(x2, True) * sin2_r[...]).astype(BF16)

        @pl.when(i == S // tq - 1)
        def _():
            dk1_o[:, :LANE] = dk1_r[...].astype(BF16)
            if has2:
                dk1_o[:, LANE:] = dv_r[...].astype(BF16)
                dk2_o[...] = dk2_r[...].astype(BF16)
            else:
                dv_o[...] = dv_r[...].astype(BF16)

    qspec = _colspec(tq, lambda h, i: (i, h))
    kspec = _colspec(S, lambda h, i: (0, h))
    ops = [q1, k1, v, o, do, lse]
    specs = [_colspec(tq, lambda h, i: (i, q1cb(h))), _colspec(S, lambda h, i: (0, k1cb(h))),
             _colspec(S, lambda h, i: (0, vcb(h))), qspec, qspec, qspec]
    if has2:
        ops += [q2, k2, *rope2]
        tab2 = _colspec(tq, lambda h, i: (i, 0))
        specs += [_colspec(tq, lambda h, i: (i, q2cb(h))), _colspec(S, lambda h, i: (0, k2cb(h))), tab2, tab2]
    if hastab:
        ops.append(tab)
        specs.append(pl.BlockSpec(tab.shape, lambda h, i: (0, 0, 0)))
    if hasc:
        ops += [ccol, crow]
        specs += [qspec, pl.BlockSpec((None, S // TK, 1, TK), lambda h, i: (h, 0, 0, 0))]
    assert all(t.dtype == BF16 for t in ops[:3] + [do] + ([q2, k2] if has2 else []))
    shp = jax.ShapeDtypeStruct((S, H * LANE), BF16)
    if has2:
        out_specs = [pl.BlockSpec((tq, 2 * LANE), lambda h, i: (i, h)), pl.BlockSpec((S, 2 * LANE), lambda h, i: (0, h)),
                     kspec]
        wide = jax.ShapeDtypeStruct((S, H * 2 * LANE), BF16)
        out_shape = [wide, wide, shp]
    else:
        out_specs = [qspec, kspec, kspec]
        out_shape = [shp] * 3
    if hasc:
        out_specs.append(pl.BlockSpec((None, S // TK, 1, TK), lambda h, i: (h, 0, 0, 0)))
        out_shape.append(jax.ShapeDtypeStruct((H, S // TK, 1, TK), F32))
    outs, moved = _pcall(
        body, name=name, grid=(H, S // tq), in_specs=specs, out_specs=out_specs, out_shape=out_shape,
        operands=ops, scratch_shapes=[pltpu.VMEM((S, LANE), F32)] * (3 if has2 else 2),
        sem=("parallel", "arbitrary"), comm=comm)
    return outs if comm is None else (outs, moved)


def _scan_matrix(kind):
    j = np.arange(TK)[:, None]
    s = np.arange(TK)[None, :]
    tri = {"suffix_ex": j > s, "prefix_in": j <= s, "prefix_ex": j < s}[kind].astype(np.float32)
    half = np.concatenate([tri, np.ones((TK, TK), np.float32)], axis=1)
    return jnp.asarray(np.concatenate([half, half], axis=0), BF16)


def _scan_mxu(x, mat, carry, reverse, split=True):
    n = x.shape[1] // TK
    hi = x.astype(BF16)
    if split:
        lo = (x - hi.astype(F32)).astype(BF16)
    else:
        mat = mat[:TK]
    parts = [None] * n
    for b in (reversed(range(n)) if reverse else range(n)):
        sl = slice(b * TK, (b + 1) * TK)
        r = _dot_nn(jnp.concatenate([hi[:, sl], lo[:, sl]], axis=1) if split else hi[:, sl], mat)
        parts[b] = r[:, :TK] + carry
        carry = carry + r[:, TK:]
    return jnp.concatenate(parts, axis=1), carry


def _stick_logs(z):
    e = jnp.exp(-jnp.abs(z))
    return e, -jnp.maximum(z, 0.0) - jnp.log(1.0 + e)


def _stick_fwd(S, H, x, qcb, kcb, vcb, scale, name, comm=None):
    tq, tk = _soft_tiles(S)
    assert x.dtype == BF16

    def body(q_r, k_r, v_r, mat_r, o_ref, t_ref):
        i = pl.program_id(1)
        q = q_r[...]
        qpos = i * tq + lax.broadcasted_iota(jnp.int32, (tq, tk), 0)
        lane = lax.broadcasted_iota(jnp.int32, (tq, tk), 1)
        j_diag = (i * tq) // tk

        def step(j, carry, masked):
            c, acc = carry
            off = pl.multiple_of(j * tk, tk)
            z = _dot_nt(q, k_r[pl.ds(off, tk), :]) * scale
            _, lk = _stick_logs(z)
            if masked:
                past = lane + j * tk < qpos
                lk = jnp.where(past, lk, 0.0)
            suf, c = _scan_mxu(lk, mat_r[...], c, True)
            a = jnp.exp(z + lk + suf)
            if masked:
                a = jnp.where(past, a, 0.0)
            acc = acc + _dot_nn(a.astype(BF16), v_r[pl.ds(off, tk), :])
            return c, acc

        carry = step(j_diag, (jnp.zeros((tq, TK), F32), jnp.zeros((tq, LANE), F32)), True)
        c, acc = lax.fori_loop(0, j_diag, lambda jj, cr: step(j_diag - 1 - jj, cr, False), carry)
        o_ref[...] = acc
        t_ref[...] = c

    o_spec = _colspec(tq, lambda h, i: (i, h))
    shp = jax.ShapeDtypeStruct((S, H * LANE), F32)
    mat = _scan_matrix("suffix_ex")
    outs, moved = _pcall(
        body, name=name, grid=(H, S // tq),
        in_specs=[_colspec(tq, lambda h, i: (i, qcb(h))), _colspec(S, lambda h, i: (0, kcb(h))),
                  _colspec(S, lambda h, i: (0, vcb(h))), pl.BlockSpec(mat.shape, lambda h, i: (0, 0))],
        out_specs=[o_spec, o_spec], out_shape=[shp, shp],
        operands=[x, x, x, mat], sem=("parallel", "arbitrary"), comm=comm)
    return outs if comm is None else (outs, moved)


def _stick_bwd(S, H, x, qcb, kcb, vcb, do, tot, scale, name, comm=None):
    tq, tk = _soft_tiles(S)
    assert x.dtype == BF16 and do.dtype == BF16

    def body(q_r, k_r, v_r, do_r, t_r, pin_r, pex_r, dq_r, dk_o, dv_o, dk_r, dv_r):
        i = pl.program_id(1)

        @pl.when(i == 0)
        def _():
            dk_r[...] = jnp.zeros_like(dk_r)
            dv_r[...] = jnp.zeros_like(dv_r)

        q = q_r[...]
        dob = do_r[...]
        total = jnp.concatenate([t_r[...]] * (tk // TK), axis=1)
        qpos = i * tq + lax.broadcasted_iota(jnp.int32, (tq, tk), 0)
        lane = lax.broadcasted_iota(jnp.int32, (tq, tk), 1)
        j_diag = (i * tq) // tk

        def step(j, carry, masked):
            cl, cg, dq = carry
            off = pl.multiple_of(j * tk, tk)
            kb = k_r[pl.ds(off, tk), :]
            z = _dot_nt(q, kb) * scale
            e, lk = _stick_logs(z)
            if masked:
                past = lane + j * tk < qpos
                lk = jnp.where(past, lk, 0.0)
            pre, cl = _scan_mxu(lk, pin_r[...], cl, False)
            a = jnp.exp(z + lk + (total - pre))
            if masked:
                a = jnp.where(past, a, 0.0)
            g = _dot_nt(dob, v_r[pl.ds(off, tk), :]) * a
            gpre, cg = _scan_mxu(g, pex_r[...], cg, False, split=False)
            inv = pl.reciprocal(1.0 + e, approx=True)
            small = e * inv
            pos = z >= 0
            dz = g * jnp.where(pos, small, inv) - jnp.where(pos, inv, small) * gpre
            if masked:
                dz = jnp.where(past, dz, 0.0)
            dzb = dz.astype(BF16)
            dk_r[pl.ds(off, tk), :] += _dot_tn(dzb, q) * scale
            dv_r[pl.ds(off, tk), :] += _dot_tn(a.astype(BF16), dob)
            return cl, cg, dq + _dot_nn(dzb, kb)

        zt = jnp.zeros((tq, TK), F32)
        carry = lax.fori_loop(0, j_diag, functools.partial(step, masked=False), (zt, zt, jnp.zeros((tq, LANE), F32)))
        dq_r[...] = (step(j_diag, carry, True)[2] * scale).astype(BF16)

        @pl.when(i == S // tq - 1)
        def _():
            dk_o[...] = dk_r[...].astype(BF16)
            dv_o[...] = dv_r[...].astype(BF16)

    qspec = _colspec(tq, lambda h, i: (i, h))
    kspec = _colspec(S, lambda h, i: (0, h))
    shp = jax.ShapeDtypeStruct((S, H * LANE), BF16)
    pin, pex = _scan_matrix("prefix_in"), _scan_matrix("prefix_ex")
    mspec = pl.BlockSpec(pin.shape, lambda h, i: (0, 0))
    outs, moved = _pcall(
        body, name=name, grid=(H, S // tq),
        in_specs=[_colspec(tq, lambda h, i: (i, qcb(h))), _colspec(S, lambda h, i: (0, kcb(h))),
                  _colspec(S, lambda h, i: (0, vcb(h))), qspec, qspec, mspec, mspec],
        out_specs=[qspec, kspec, kspec], out_shape=[shp] * 3,
        operands=[x, x, x, do, tot, pin, pex], scratch_shapes=[pltpu.VMEM((S, LANE), F32)] * 2,
        sem=("parallel", "arbitrary"), comm=comm)
    return outs if comm is None else (outs, moved)


def _scan8(x, rows, reverse):
    for sh in (1, 2, 4):
        if reverse:
            x = x + jnp.where(rows + sh < 8, pltpu.roll(x, 8 - sh, 0), 0.0)
        else:
            x = x + jnp.where(rows >= sh, pltpu.roll(x, sh, 0), 0.0)
    return x


def _fox_prep(S, H, proj, fcb, bias, name):
    tk = TK

    def body(f_ref, b_ref, ccol_ref, crow_ref, scr):
        rows = lax.broadcasted_iota(jnp.int32, (8, LANE), 0)

        def step(t, carry):
            off = pl.multiple_of(t * 8, 8)
            xb = f_ref[pl.ds(off, 8), :] + b_ref[...]
            lf = jnp.minimum(xb, 0.0) - jnp.log(1.0 + jnp.exp(-jnp.abs(xb)))
            lf = _scan8(lf, rows, False) + carry
            scr[pl.ds(off, 8), :] = lf
            return lf[7:8, :]

        lax.fori_loop(0, S // 8, step, jnp.zeros((1, LANE), F32))
        for h in range(H):
            ccol_ref[:, h * LANE:(h + 1) * LANE] = jnp.broadcast_to(scr[:, h:h + 1], (S, LANE))

            def tr(t, _):
                off = pl.multiple_of(t * tk, tk)
                blk = ccol_ref[pl.ds(off, tk), h * LANE:(h + 1) * LANE]
                crow_ref[h, t] = blk.T[0:1, :]
                return 0

            lax.fori_loop(0, S // tk, tr, 0)

    return pl.pallas_call(
        body, name=name, grid=(1,),
        in_specs=[_colspec(S, lambda i: (0, fcb)), pl.BlockSpec((1, LANE), lambda i: (0, 0))],
        out_specs=[pl.BlockSpec((S, H * LANE), lambda i: (0, 0)),
                   pl.BlockSpec((H, S // tk, 1, tk), lambda i: (0, 0, 0, 0))],
        out_shape=[jax.ShapeDtypeStruct((S, H * LANE), F32), jax.ShapeDtypeStruct((H, S // tk, 1, tk), F32)],
        scratch_shapes=[pltpu.VMEM((S, LANE), F32)],
        compiler_params=_cp("arbitrary"),
    )(proj, bias)


def _fox_bwd(S, H, proj, fcb, bias, dcr, name):
    tk = TK

    def body(f_ref, b_ref, dcr_ref, df_ref, db_ref, scr):
        rows = lax.broadcasted_iota(jnp.int32, (8, LANE), 0)
        lane_t = lax.broadcasted_iota(jnp.int32, (tk, LANE), 1)
        nb = S // 8

        def tr(t, _):
            off = pl.multiple_of(t * tk, tk)
            d = jnp.zeros((tk, LANE), F32)
            for h in range(H):
                d = d + jnp.where(lane_t == h, jnp.broadcast_to(dcr_ref[h, t], (LANE, tk)).T, 0.0)
            scr[pl.ds(off, tk), :] = d
            return 0

        lax.fori_loop(0, S // tk, tr, 0)

        def step(tt, carry):
            suffix, db = carry
            off = pl.multiple_of((nb - 1 - tt) * 8, 8)
            d = _scan8(scr[pl.ds(off, 8), :], rows, True) + suffix
            xb = f_ref[pl.ds(off, 8), :] + b_ref[...]
            e = jnp.exp(-jnp.abs(xb))
            dx = d * jnp.where(xb >= 0, e, 1.0) / (1.0 + e)
            df_ref[pl.ds(off, 8), :] = dx
            return d[0:1, :], db + jnp.sum(dx, axis=0, keepdims=True)

        z = jnp.zeros((1, LANE), F32)
        _, db = lax.fori_loop(0, nb, step, (z, z))
        db_ref[...] = db

    return pl.pallas_call(
        body, name=name, grid=(1,),
        in_specs=[_colspec(S, lambda i: (0, fcb)), pl.BlockSpec((1, LANE), lambda i: (0, 0)),
                  pl.BlockSpec((H, S // tk, 1, tk), lambda i: (0, 0, 0, 0))],
        out_specs=[pl.BlockSpec((S, LANE), lambda i: (0, 0)), pl.BlockSpec((1, LANE), lambda i: (0, 0))],
        out_shape=[jax.ShapeDtypeStruct((S, LANE), F32), jax.ShapeDtypeStruct((1, LANE), F32)],
        scratch_shapes=[pltpu.VMEM((S, LANE), F32)],
        compiler_params=_cp("arbitrary"),
    )(proj, bias, dcr)


def _adamw(w, slots, m, v, name, comm=None):
    R, C = w.shape
    parts = len(slots)
    rows = R // parts
    row_bytes = 2 * C * (parts * NDEV * slots[0].dtype.itemsize + 7 * 4)
    sub = 32 // slots[0].dtype.itemsize
    tiles = [t for t in range(sub, rows + 1, sub) if rows % t == 0] or [rows]
    tr = max([t for t in tiles if t * row_bytes <= ADAMW_BLOCK_BYTES] or tiles[:1])
    per = R // parts // tr
    c1 = 1.0 - ADAM_B1 ** ADAM_STEP
    c2 = 1.0 - ADAM_B2 ** ADAM_STEP

    def body(*refs):
        w_ref, m_ref, v_ref = refs[:3]
        s_refs = refs[3:3 + parts]
        g_ref, d_ref, nm_ref, nv_ref = refs[3 + parts:]
        part = pl.program_id(0) // per
        for a, s_ref in enumerate(s_refs):
            @pl.when(part == a)
            def _():
                g = s_ref[0].astype(F32)
                for s in range(1, NDEV):
                    g = g + s_ref[s].astype(F32)
                g_ref[...] = g

        g = g_ref[...]
        mn = ADAM_B1 * m_ref[...] + (1.0 - ADAM_B1) * g
        vn = ADAM_B2 * v_ref[...] + (1.0 - ADAM_B2) * (g * g)
        nm_ref[...] = mn
        nv_ref[...] = vn
        d_ref[...] = -ADAM_LR * ((mn / c1) / (jnp.sqrt(vn / c2) + ADAM_EPS) + ADAM_WD * w_ref[...])

    row = pl.BlockSpec((tr, C), lambda i: (i, 0))
    s_specs = [pl.BlockSpec((NDEV, tr, C), lambda i, a=a: (0, jnp.clip(i - a * per, 0, per - 1), 0))
               for a in range(parts)]
    outs, moved = _pcall(
        body, name=name, grid=(R // tr,), in_specs=[row, row, row] + s_specs,
        out_specs=[row] * 4, out_shape=[jax.ShapeDtypeStruct((R, C), F32)] * 4,
        operands=[w, m, v, *slots], sem=("arbitrary",), comm=comm)
    return outs if comm is None else (outs, moved)


class _Layout:
    def __init__(self, D):
        self.GW = GW = D // 4
        self.H = H = GW // HEAD
        self.QL, self.KVL = 0, Q_LORA
        base = Q_LORA + KV_LORA
        (self.QB, self.KB, self.VB, self.QC, self.KC, self.VC, self.QD, self.KD, self.VD) = (
            base + k * GW for k in range(9))
        self.KR = base + 9 * GW
        self.FC = self.KR + LANE
        self.PW = -(-(self.FC + LANE) // 512) * 512
        self.o_kr = base
        self.o_bc = base + QK_ROPE
        self.o_fc = self.o_bc + 6 * GW
        self.o_d = self.o_fc + H
        self.IN = self.o_d + 3 * GW

    def pad(self, w):
        z = lambda n: jnp.zeros(w.shape[:-1] + (n,), w.dtype)
        return jnp.concatenate([
            w[..., :self.o_kr], w[..., self.o_bc:self.o_fc], w[..., self.o_d:self.IN],
            w[..., self.o_kr:self.o_bc], z(LANE - QK_ROPE), w[..., self.o_fc:self.o_d], z(LANE - self.H),
            z(self.PW - self.FC - LANE)], axis=-1)

    def unpad(self, g):
        return jnp.concatenate([
            g[..., :self.KR - 9 * self.GW], g[..., self.KR:self.KR + QK_ROPE], g[..., self.QB:self.QD],
            g[..., self.FC:self.FC + self.H], g[..., self.QD:self.KR]], axis=-1)


def _rope_tables(S):
    pos = jnp.arange(S, dtype=F32)

    def cs(dim):
        inv = ROPE_THETA ** (-jnp.arange(0, dim, 2, dtype=F32) / dim)
        ang = pos[:, None] * inv[None, :]
        return jnp.cos(ang), jnp.sin(ang)

    c, s = cs(HEAD)
    full = (jnp.concatenate([c, c], 1), jnp.concatenate([-s, s], 1))
    c, s = cs(QK_ROPE)
    z = jnp.zeros((S, LANE - QK_ROPE), F32)
    half = (jnp.concatenate([c, c, z], 1), jnp.concatenate([-s, s, z], 1))
    return full, half


def _dilated_table(tq, tk):
    win = max(w for w, _ in DILATED_PAIRS)
    nd = (win + tk) // tq + 1
    d = np.arange(nd)[:, None, None] * tq + np.arange(tq)[None, :, None] - np.arange(tk)[None, None, :]
    mult = np.zeros(d.shape, np.float64)
    for w, dil in DILATED_PAIRS:
        mult += (d >= 0) & (d <= w) & (d % dil == 0)
    return jnp.asarray(np.where(mult > 0, np.log(np.maximum(mult, 1.0)), NEG), F32), win


def _pack(arrs):
    rows = []
    for a in arrs:
        f = a.reshape(-1).astype(F32)
        f = jnp.pad(f, (0, (-f.shape[0]) % LANE))
        rows.append(f.reshape(-1, LANE))
    p = jnp.concatenate(rows, 0)
    return jnp.pad(p, ((0, (-p.shape[0]) % 8), (0, 0)))


def _unpack(p, shapes):
    out, r = [], 0
    for shp in shapes:
        n = int(np.prod(shp))
        nr = -(-n // LANE)
        out.append(p[r:r + nr].reshape(-1)[:n].reshape(shp))
        r += nr
    return out


def kernel(x, attn_norm, w_in, mla_q_norm, w_uq, mla_kv_norm, w_ukv, fox_forget_bias, group_norm, w_out, ffn_norm, w_gate, w_up, w_down, final_norm, loss_target, m_attn_norm, m_w_in, m_mla_q_norm, m_w_uq, m_mla_kv_norm, m_w_ukv, m_fox_forget_bias, m_group_norm, m_w_out, m_ffn_norm, m_w_gate, m_w_up, m_w_down, m_final_norm, v_attn_norm, v_w_in, v_mla_q_norm, v_w_uq, v_mla_kv_norm, v_w_ukv, v_fox_forget_bias, v_group_norm, v_w_out, v_ffn_norm, v_w_gate, v_w_up, v_w_down, v_final_norm):
    _, S, D = x.shape
    L = attn_norm.shape[0]
    lay = _Layout(D)
    H, GW, PW = lay.H, lay.GW, lay.PW
    FB = w_gate.shape[2]
    QKA = HEAD + QK_ROPE
    x = x[0]
    target = loss_target[0]
    rope_full, rope_half = _rope_tables(S)
    neg = lambda t: (t[0], -t[1])
    tab, win = _dilated_table(*_soft_tiles(S))
    cb = lambda col: col // LANE

    sh = dict(w_in=lay.pad(w_in).astype(BF16),
              **{n: w.astype(BF16) for n, w in (("w_uq", w_uq), ("w_ukv", w_ukv), ("w_out", w_out),
                                                  ("w_gate", w_gate), ("w_up", w_up), ("w_down", w_down))})
    first3 = ["w_in", "w_uq", "w_ukv"]

    def first_weights(g):
        wuq = jnp.transpose(g[1], (1, 0, 2)).reshape(Q_LORA, H, QKA)
        wuq = jnp.pad(wuq, ((0, 0), (0, 0), (0, 2 * LANE - QKA))).reshape(Q_LORA, H * 2 * LANE)
        return dict(win=g[0].reshape(D, PW), wuq=wuq,
                    wukv=jnp.transpose(g[2], (1, 0, 2)).reshape(KV_LORA, H * 2 * LANE))

    def row(a):
        return a.reshape(1, -1)

    def forward(l, x0, W):
        A = dict(x0=x0)
        A["bias"] = jnp.pad(row(fox_forget_bias[l]), ((0, 0), (0, LANE - H)))
        h1 = A["h1"] = _rms_fwd(x0, row(attn_norm[l]), D, 0, BF16, "attn_norm")
        (proj, pb), (W["wg"],) = _mm(h1, W["win"], "in_proj", out_dtype=(F32, BF16),
                                     comm=("gather", [sh["w_gate"][l]]))
        A["proj"], A["pb"] = proj, pb
        qln = A["qln"] = _rms_fwd(proj, row(mla_q_norm[l]), Q_LORA, cb(lay.QL) // 4, BF16, "q_norm")
        kvln = A["kvln"] = _rms_fwd(proj, row(mla_kv_norm[l]), KV_LORA, cb(lay.KVL) // 4, BF16, "kv_norm")
        qa, qab = _mm(qln, W["wuq"], "q_up", out_dtype=(F32, BF16))
        A["qab"] = qab
        kv = A["kv"] = _mm(kvln, W["wukv"], "kv_up", out_dtype=BF16)
        q_pe = A["q_pe"] = _rope(qa, (1, H, 2), *rope_half, True, BF16, "rope_q_mla")
        k_pe = A["k_pe"] = _rope(proj, (cb(lay.KR), 1, 1), *rope_half, True, BF16, "rope_k_mla")
        (A["o_a"], A["lse_a"]), (W["wu"],) = _attn_fwd(
            S, H, qab, lambda h: 2 * h, kv, lambda h: 2 * h, kv, lambda h: 2 * h + 1, QKA ** -0.5, "mla_fwd",
            q2=q_pe, q2cb=lambda h: h, k2=k_pe, k2cb=lambda h: 0, comm=("gather", [sh["w_up"][l]]))
        qk_b = A["qk_b"] = _rope(proj, (cb(lay.QB), 2 * H, 1), *rope_full, False, BF16, "rope_qk_dil")
        (A["o_b"], A["lse_b"]), (g_down,) = _attn_fwd(
            S, H, qk_b, lambda h: h, qk_b, lambda h: H + h, pb, lambda h: cb(lay.VB) + h, HEAD ** -0.5,
            "dilated_fwd", tab=tab, win=win, comm=("gather", [sh["w_down"][l]]))
        W["wd"] = g_down.reshape(NDEV * FB, D)
        ccol, crow = A["ccol"], A["crow"] = _fox_prep(S, H, proj, cb(lay.FC), A["bias"], "fox_prep")
        (A["o_c"], A["lse_c"]), (g_out,) = _attn_fwd(
            S, H, pb, lambda h: cb(lay.QC) + h, pb, lambda h: cb(lay.KC) + h, pb, lambda h: cb(lay.VC) + h,
            HEAD ** -0.5, "fox_fwd", ccol=ccol, crow=crow, comm=("gather", [sh["w_out"][l]]))
        W["wout"] = g_out.reshape(4 * GW, D)
        A["o_d"], A["tot_d"] = _stick_fwd(
            S, H, pb, lambda h: cb(lay.QD) + h, lambda h: cb(lay.KD) + h, lambda h: cb(lay.VD) + h,
            HEAD ** -0.5, "stick_fwd")
        mix = A["mix"] = _gn_fwd([A["o_a"], A["o_b"], A["o_c"], A["o_d"]], row(group_norm[l]), "group_norm")
        x1 = A["x1"] = _mm(mix, W["wout"], "out_proj", res=x0)
        h2 = A["h2"] = _rms_fwd(x1, row(ffn_norm[l]), D, 0, BF16, "ffn_norm")
        nxt = None
        if l + 1 < L:
            (A["g"], A["u"], A["act"]), nxt = _ffn_up(
                h2, W["wg"], W["wu"], "ffn_up", comm=("gather", [sh[n][l + 1] for n in first3]))
        else:
            A["g"], A["u"], A["act"] = _ffn_up(h2, W["wg"], W["wu"], "ffn_up_last")
        return _mm_down(A["act"], W["wd"], x1, "ffn_down"), A, nxt

    def backward(l, dx2, dx2b, W, A, late):
        proj, pb = A["proj"], A["pb"]
        G, small, got = {}, {}, {}
        dgate, dup = _ffn_dact(dx2b, W["wd"], A["g"], A["u"], "ffn_dact")
        G["w_down"] = _mm_dwdown(A["act"], dx2b, "dw_down").reshape(NDEV, FB, D)
        dh2 = _mm_dh2(dgate, W["wg"], dup, W["wu"], "ffn_dh")
        if late is None:
            G["w_gate"] = _mm_dwgate(A["h2"], dgate, "dw_gate_top")
            G["w_up"] = _mm_dwgate(A["h2"], dup, "dw_up_top")
        else:
            G["w_gate"], (got[l + 1, "w_in_a"],) = _mm_dwgate(
                A["h2"], dgate, "dw_gate", comm=("exchange", [late["w_in_a"]]))
            G["w_up"], (got[l + 1, "w_in_b"],) = _mm_dwgate(
                A["h2"], dup, "dw_up", comm=("exchange", [late["w_in_b"]]))
        dx1, dx1b, small["ffn_norm"] = _rms_bwd(A["x1"], row(ffn_norm[l]), dh2, D, 0, "ffn_norm_bwd", res=dx2)
        dmix = _mm(dx1b, W["wout"], "out_proj_dx", tb=True)
        G["w_out"] = _mm(A["mix"], dx1b, "dw_out", ta=True, out_dtype=BF16).reshape(NDEV, 4 * GW // NDEV, D)
        do_a, do_b, do_c, do_d, small["group_norm"] = _gn_bwd(
            [A["o_a"], A["o_b"], A["o_c"], A["o_d"]], row(group_norm[l]), dmix, "group_norm_bwd")
        (dq_d, dk_d, dv_d), (got[l, "w_up"],) = _stick_bwd(
            S, H, pb, lambda h: cb(lay.QD) + h, lambda h: cb(lay.KD) + h, lambda h: cb(lay.VD) + h,
            do_d, A["tot_d"], HEAD ** -0.5, "stick_bwd", comm=("exchange", [G["w_up"]]))
        (dq_c, dk_c, dv_c, dcc), (got[l, "w_down"],) = _attn_bwd(
            S, H, pb, lambda h: cb(lay.QC) + h, pb, lambda h: cb(lay.KC) + h, pb, lambda h: cb(lay.VC) + h,
            A["o_c"], do_c, A["lse_c"], HEAD ** -0.5, "fox_bwd",
            ccol=A["ccol"], crow=A["crow"], comm=("exchange", [G["w_down"]]))
        dfc, dbias = _fox_bwd(S, H, proj, cb(lay.FC), A["bias"], dcc, "fox_gate_bwd")
        small["fox_forget_bias"] = dbias[0, :H]
        qk_b = A["qk_b"]
        (dq_b, dk_b, dv_b), moved = _attn_bwd(
            S, H, qk_b, lambda h: h, qk_b, lambda h: H + h, pb, lambda h: cb(lay.VB) + h,
            A["o_b"], do_b, A["lse_b"], HEAD ** -0.5, "dilated_bwd" if late else "dilated_bwd_top", tab=tab, win=win,
            comm=("exchange", [G["w_out"]] + ([late["w_uq"], late["w_ukv"]] if late else [])))
        got[l, "w_out"] = moved[0]
        if late:
            got[l + 1, "w_uq"], got[l + 1, "w_ukv"] = moved[1:]
        dqk_b = _rope(jnp.concatenate([dq_b, dk_b], 1), (0, 2 * H, 1), *neg(rope_full), False, BF16, "rope_qk_dil_bwd")
        qab, kv = A["qab"], A["kv"]
        (dqa, dkv, dk2), (got[l, "w_gate"],) = _attn_bwd(
            S, H, qab, lambda h: 2 * h, kv, lambda h: 2 * h, kv, lambda h: 2 * h + 1,
            A["o_a"], do_a, A["lse_a"], QKA ** -0.5, "mla_bwd", q2=A["q_pe"], q2cb=lambda h: h, k2=A["k_pe"],
            k2cb=lambda h: 0, rope2=neg(rope_half), comm=("exchange", [G["w_gate"]]))
        dk_pe = _rope(dk2.astype(F32).reshape(S, H, LANE).sum(1), (0, 1, 1), *neg(rope_half), True, BF16,
                      "rope_k_mla_bwd")
        dwuq = _mm(A["qln"], dqa, "dw_uq", ta=True, out_dtype=BF16)
        dwuq = dwuq.reshape(Q_LORA, H, 2 * LANE)[:, :, :QKA].reshape(Q_LORA, NDEV, H * QKA // NDEV)
        G["w_uq"] = jnp.transpose(dwuq, (1, 0, 2))
        dwukv = _mm(A["kvln"], dkv, "dw_ukv", ta=True, out_dtype=BF16).reshape(KV_LORA, NDEV, H * 2 * LANE // NDEV)
        G["w_ukv"] = jnp.transpose(dwukv, (1, 0, 2))
        dqln = _mm(dqa, W["wuq"], "q_up_dx", tb=True)
        dkvln = _mm(dkv, W["wukv"], "kv_up_dx", tb=True)
        _, dql, small["mla_q_norm"] = _rms_bwd(proj, row(mla_q_norm[l]), dqln, Q_LORA, cb(lay.QL) // 4, "q_norm_bwd")
        _, dkvl, small["mla_kv_norm"] = _rms_bwd(
            proj, row(mla_kv_norm[l]), dkvln, KV_LORA, cb(lay.KVL) // 4, "kv_norm_bwd")
        dproj = jnp.concatenate([
            dql, dkvl, dqk_b, dv_b, dq_c, dk_c, dv_c, dq_d, dk_d, dv_d,
            dk_pe, dfc.astype(BF16), jnp.zeros((S, PW - lay.FC - LANE), BF16)], axis=1)
        g_in = _mm(A["h1"], dproj, "dw_in", ta=True, out_dtype=BF16).reshape(NDEV, D // NDEV, PW)
        half = D // NDEV // 2
        late = dict(w_in_a=g_in[:, :half], w_in_b=g_in[:, half:], w_uq=G["w_uq"], w_ukv=G["w_ukv"])
        if l == 0:
            names = ["w_in_a", "w_uq", "w_ukv"]
            dh1, moved = _mm(dproj, W["win"], "in_proj_dx_last", tb=True,
                             comm=("exchange", [late[n] for n in names]))
            got.update({(0, n): s for n, s in zip(names, moved)})
        else:
            dh1 = _mm(dproj, W["win"], "in_proj_dx", tb=True)
        dx0, dx0b, small["attn_norm"] = _rms_bwd(A["x0"], row(attn_norm[l]), dh1, D, 0, "attn_norm_bwd", res=dx1)
        return dx0, dx0b, late, got, small

    big = first3 + ["w_out", "w_gate", "w_up", "w_down"]
    Ws, As = [], []
    xc = x
    nxt = _comm_alone("gather", [sh[n][0] for n in first3], "gather_first")
    for l in range(L):
        W = first_weights(nxt)
        xc, A, nxt = forward(l, xc, W)
        Ws.append(W)
        As.append(A)
    dx, loss_part = _final_loss(xc, row(final_norm), target, "final_loss")
    dx, dxb, dfinal = _rms_bwd(xc, row(final_norm), dx, D, 0, "final_norm_bwd")
    slots = {}
    smalls = [None] * L
    late = None
    for l in reversed(range(L)):
        dx, dxb, late, got, smalls[l] = backward(l, dx, dxb, Ws[l], As[l], late)
        slots.update(got)

    names_small = ["attn_norm", "mla_q_norm", "mla_kv_norm", "fox_forget_bias", "group_norm", "ffn_norm"]
    params = dict(attn_norm=attn_norm, mla_q_norm=mla_q_norm, mla_kv_norm=mla_kv_norm, fox_forget_bias=fox_forget_bias,
                  group_norm=group_norm, ffn_norm=ffn_norm, final_norm=final_norm, w_in=w_in, w_uq=w_uq, w_ukv=w_ukv,
                  w_out=w_out, w_gate=w_gate, w_up=w_up, w_down=w_down)
    moms = dict(attn_norm=(m_attn_norm, v_attn_norm), mla_q_norm=(m_mla_q_norm, v_mla_q_norm),
                mla_kv_norm=(m_mla_kv_norm, v_mla_kv_norm), fox_forget_bias=(m_fox_forget_bias, v_fox_forget_bias),
                group_norm=(m_group_norm, v_group_norm), ffn_norm=(m_ffn_norm, v_ffn_norm),
                final_norm=(m_final_norm, v_final_norm), w_in=(m_w_in, v_w_in), w_uq=(m_w_uq, v_w_uq),
                w_ukv=(m_w_ukv, v_w_ukv), w_out=(m_w_out, v_w_out), w_gate=(m_w_gate, v_w_gate),
                w_up=(m_w_up, v_w_up), w_down=(m_w_down, v_w_down))
    small_list = names_small + ["final_norm"]
    small_grads = [jnp.stack([smalls[l][n].reshape(params[n].shape[1:]) for l in range(L)]) for n in names_small]
    small_grads.append(dfinal.reshape(final_norm.shape))
    shapes = [params[n].shape for n in small_list] + [(LANE,)]
    packed_g = _comm_alone("gather", [_pack(small_grads + [loss_part.reshape(LANE)])], "gather_small")[0]
    zero = jnp.zeros((LANE,), F32)
    res_small = _adamw(_pack([params[n] for n in small_list] + [zero]), [packed_g],
                       _pack([moms[n][0] for n in small_list] + [zero]),
                       _pack([moms[n][1] for n in small_list] + [zero]), "adamw_small")
    unp = [_unpack(r, shapes) for r in res_small]
    out = {n: tuple(unp[k][i] for k in range(4)) for i, n in enumerate(small_list)}
    loss = unp[0][-1][0]

    for n in ["w_gate", "w_up", "w_down", "w_out", "w_uq", "w_ukv", "w_in"]:
        if n == "w_in":
            st = [lay.unpad(slots[l, n + h]) for l in range(L) for h in ("_a", "_b")]
        else:
            st = [slots[l, n] for l in range(L)]
        C = st[0].shape[-1]
        st = [s.reshape(NDEV, -1, C) for s in st]
        args = (params[n].reshape(-1, C), st, moms[n][0].reshape(-1, C), moms[n][1].reshape(-1, C), "adamw_" + n)
        if n == "w_gate":
            res, (slots[0, "w_in_b"],) = _adamw(*args, comm=("exchange", [late["w_in_b"]]))
        else:
            res = _adamw(*args)
        out[n] = tuple(r.reshape(params[n].shape) for r in res)

    order = ["attn_norm", "w_in", "mla_q_norm", "w_uq", "mla_kv_norm", "w_ukv", "fox_forget_bias", "group_norm",
             "w_out", "ffn_norm", "w_gate", "w_up", "w_down", "final_norm"]
    return (loss, dx[None], *[out[n][0] for n in order], *[out[n][1] for n in order],
            *[out[n][2] for n in order], *[out[n][3] for n in order])
```

```python
import functools
import math

import numpy as np
import jax
import jax.numpy as jnp
from jax import lax
from jax.experimental import pallas as pl
from jax.experimental.pallas import tpu as pltpu

F32 = jnp.float32
BF16 = jnp.bfloat16
NDEV = 8
LANE = 128
HEAD = 128
Q_LORA = 512
KV_LORA = 512
QK_ROPE = 64
DILATED_PAIRS = ((128, 1), (512, 4), (2048, 16))
ROPE_THETA = 10000.0
EPS = 1e-6
NEG = -1e30
TQ = 512
TK = 128
TKS = 512
VMEM_LIMIT = 48 * 1024 * 1024
MM_OPERAND_BYTES = 20 * 1024 * 1024
FFN_BLOCKS_PER_STEP = 2
ADAMW_BLOCK_BYTES = 24 * 1024 * 1024
ADAM_LR, ADAM_B1, ADAM_B2, ADAM_EPS, ADAM_WD, ADAM_STEP = 0.001, 0.9, 0.999, 1e-08, 0.01, 10
MESH = pl.DeviceIdType.MESH
ANY = pl.BlockSpec(memory_space=pl.ANY)


def _cp(*sem):
    return pltpu.CompilerParams(dimension_semantics=sem, vmem_limit_bytes=VMEM_LIMIT)


def _dot(a, b, ca, cb):
    return lax.dot_general(a, b, (((ca,), (cb,)), ((), ())), preferred_element_type=F32)


def _dot_nn(a, b):
    return _dot(a, b, 1, 0)


def _dot_nt(a, b):
    return _dot(a, b, 1, 1)


def _dot_tn(a, b):
    return _dot(a, b, 0, 0)


def _tile(n, t):
    if n <= t:
        return n
    t -= t % LANE
    while n % t:
        t -= LANE
    return t


def _direct_copies(ins, outs, send_sems, recv_sems, local_sems, want_recvs=True):
    x, y, c = lax.axis_index("x"), lax.axis_index("y"), lax.axis_index("c")
    my_id = 4 * x + 2 * y + c
    local, sends, recvs = [], [], []
    for a in range(len(ins)):
        mine = ins[a].at[my_id]
        local.append(pltpu.make_async_copy(mine, outs[a].at[my_id], local_sems.at[a]))
        for k in range(1, NDEV):
            peer = (1 - x if k & 4 else x, 1 - y if k & 2 else y, 1 - c if k & 1 else c)
            pid = 4 * peer[0] + 2 * peer[1] + peer[2]
            sems = dict(send_sem=send_sems.at[a, k - 1], recv_sem=recv_sems.at[a, k - 1],
                        device_id=peer, device_id_type=MESH)
            sends.append(pltpu.make_async_remote_copy(src_ref=ins[a].at[pid], dst_ref=outs[a].at[my_id], **sems))
            if want_recvs:
                recvs.append(pltpu.make_async_remote_copy(src_ref=mine, dst_ref=outs[a].at[pid], **sems))
    return local, sends, recvs


def _comm_start(kind, ins, outs, send_sems, recv_sems, local_sems):
    if kind == "exchange":
        local, sends, _ = _direct_copies(ins, outs, send_sems, recv_sems, local_sems, want_recvs=False)
        for cp in local + sends:
            cp.start()
        return
    x, y, c = lax.axis_index("x"), lax.axis_index("y"), lax.axis_index("c")
    for a in range(len(ins)):
        mine = outs[a].at[4 * x + 2 * y + c]
        pltpu.make_async_copy(ins[a], mine, local_sems.at[a]).start()
        for k, to in enumerate([(x, y, 1 - c), (1 - x, y, c), (x, 1 - y, c), (1 - x, 1 - y, c)]):
            pltpu.make_async_remote_copy(src_ref=ins[a], dst_ref=mine, send_sem=send_sems.at[a, k],
                                         recv_sem=recv_sems.at[a, k], device_id=to, device_id_type=MESH).start()


def _comm_finish(kind, ins, outs, send_sems, recv_sems, local_sems):
    if kind == "exchange":
        local, sends, recvs = _direct_copies(ins, outs, send_sems, recv_sems, local_sems)
        for cp in recvs:
            cp.wait_recv()
        for cp in sends:
            cp.wait_send()
        for cp in local:
            cp.wait()
        return
    x, y, c = lax.axis_index("x"), lax.axis_index("y"), lax.axis_index("c")
    sibling = (x, y, 1 - c)
    chips = [(1 - x, y), (x, 1 - y), (1 - x, 1 - y)]
    for a in range(len(ins)):
        def copy(k, block, to):
            rows = outs[a].at[4 * block[0] + 2 * block[1] + block[2]]
            return pltpu.make_async_remote_copy(src_ref=rows, dst_ref=rows, send_sem=send_sems.at[a, k],
                                                recv_sem=recv_sems.at[a, k], device_id=to, device_id_type=MESH)

        passed = []
        for j, chip in enumerate(chips):
            copy(1 + j, (*chip, c), (x, y, c)).wait_recv()
            passed.append(copy(4 + j, (*chip, c), sibling))
            passed[-1].start()
        copy(0, sibling, (x, y, c)).wait_recv()
        for j, chip in enumerate(chips):
            copy(4 + j, (*chip, 1 - c), (x, y, c)).wait_recv()
        for k in range(4):
            copy(k, (x, y, c), sibling).wait_send()
        for cp in passed:
            cp.wait_send()
        pltpu.make_async_copy(ins[a], outs[a].at[4 * x + 2 * y + c], local_sems.at[a]).wait()


def _comm_shapes(kind, arrs):
    out_shape = [jax.ShapeDtypeStruct(((NDEV,) if kind == "gather" else ()) + a.shape, a.dtype) for a in arrs]
    n = len(arrs)
    sems = [pltpu.SemaphoreType.DMA((n, 7)), pltpu.SemaphoreType.DMA((n, 7)), pltpu.SemaphoreType.DMA((n,))]
    return out_shape, sems


def _comm_alone(kind, arrs, name):
    n = len(arrs)

    def body(*refs):
        _comm_start(kind, refs[:n], refs[n:2 * n], *refs[2 * n:])
        _comm_finish(kind, refs[:n], refs[n:2 * n], *refs[2 * n:])

    out_shape, sems = _comm_shapes(kind, arrs)
    return pl.pallas_call(body, name=name, out_shape=out_shape, in_specs=[ANY] * n, out_specs=[ANY] * n,
                          scratch_shapes=sems)(*arrs)


def _pcall(body, *, name, grid, in_specs, out_specs, out_shape, operands, sem, scratch_shapes=(), comm=None):
    in_specs, out_specs, out_shape = list(in_specs), list(out_specs), list(out_shape)
    scratch_shapes = list(scratch_shapes)
    if comm is None:
        res = pl.pallas_call(body, name=name, grid=grid, in_specs=in_specs, out_specs=out_specs, out_shape=out_shape,
                             scratch_shapes=scratch_shapes, compiler_params=_cp(*sem))(*operands)
        return list(res), []
    kind, arrs = comm
    nc, n_in, n_out, n_scr = len(arrs), len(operands), len(out_shape), len(scratch_shapes)
    c_shape, c_sems = _comm_shapes(kind, arrs)

    def carrier(*refs):
        ins, cin = refs[:n_in], refs[n_in:n_in + nc]
        outs = refs[n_in + nc:n_in + nc + n_out]
        cout = refs[n_in + nc + n_out:n_in + 2 * nc + n_out]
        scr = refs[n_in + 2 * nc + n_out:n_in + 2 * nc + n_out + n_scr]
        sems = refs[n_in + 2 * nc + n_out + n_scr:]
        pids = [pl.program_id(d) for d in range(len(grid))]
        first = functools.reduce(jnp.logical_and, [p == 0 for p in pids])
        last = functools.reduce(jnp.logical_and, [p == g - 1 for p, g in zip(pids, grid)])

        @pl.when(first)
        def _():
            _comm_start(kind, cin, cout, *sems)

        body(*ins, *outs, *scr)

        @pl.when(last)
        def _():
            _comm_finish(kind, cin, cout, *sems)

    res = pl.pallas_call(
        carrier, name=name, grid=grid, in_specs=in_specs + [ANY] * nc, out_specs=out_specs + [ANY] * nc,
        out_shape=out_shape + c_shape, scratch_shapes=scratch_shapes + c_sems,
        compiler_params=_cp(*["arbitrary"] * len(grid)))(*operands, *arrs)
    return list(res[:n_out]), list(res[n_out:])


def _mm_call(pairs, grid, a_spec, b_spec, o_spec, out_shape, acc_shape, nk, ca, cb, name,
             res=None, res_spec=None, comm=None):
    npairs = len(pairs)
    multi = isinstance(out_shape, (list, tuple))
    nout = len(out_shape) if multi else 1

    def body(*refs):
        ab = refs[:2 * npairs]
        r_ref = refs[2 * npairs] if res is not None else None
        o_refs, acc = refs[-1 - nout:-1], refs[-1]
        k = pl.program_id(2)

        @pl.when(k == 0)
        def _():
            acc[...] = jnp.zeros_like(acc)

        tot = None
        for p in range(npairs):
            av, bv = ab[2 * p][...].astype(BF16), ab[2 * p + 1][...].astype(BF16)
            if av.ndim == 2:
                terms = [(av, bv)]
            else:
                rows = bv.shape[0] // av.shape[0]
                terms = [(av[q], bv[q] if bv.ndim == 3 else bv[q * rows:(q + 1) * rows]) for q in range(av.shape[0])]
            for at, bt in terms:
                d = _dot(at, bt, ca, cb)
                tot = d if tot is None else tot + d
        acc[...] += tot

        @pl.when(k == nk - 1)
        def _():
            r = acc[...]
            if r_ref is not None:
                r = r + r_ref[...]
            for o_ref in o_refs:
                o_ref[...] = r.astype(o_ref.dtype)

    ops, specs = [], []
    for a, b in pairs:
        ops += [a, b]
        specs += [a_spec, b_spec]
    if res is not None:
        ops.append(res)
        specs.append(res_spec)
    outs, moved = _pcall(
        body, name=name, grid=grid, in_specs=specs, out_specs=[o_spec] * nout,
        out_shape=out_shape if multi else [out_shape], operands=ops,
        scratch_shapes=[pltpu.VMEM(acc_shape, F32)], sem=("parallel", "parallel", "arbitrary"), comm=comm)
    outs = outs if multi else outs[0]
    return outs if comm is None else (outs, moved)


def _k_tile(K, row_bytes, tk=2048):
    tk = _tile(K, tk)
    while 2 * tk * row_bytes > MM_OPERAND_BYTES and tk % 256 == 0:
        tk //= 2
    return tk


def _mm(a, b, name, ta=False, tb=False, out_dtype=F32, res=None, tm=1024, tn=1024, comm=None):
    M, K = (a.shape[1], a.shape[0]) if ta else a.shape
    N = b.shape[0] if tb else b.shape[1]
    tm, tn = _tile(M, tm), _tile(N, tn)
    tk = _k_tile(K, tm * a.dtype.itemsize + tn * b.dtype.itemsize)
    a_spec = pl.BlockSpec((tk, tm), lambda i, j, k: (k, i)) if ta else pl.BlockSpec((tm, tk), lambda i, j, k: (i, k))
    b_spec = pl.BlockSpec((tn, tk), lambda i, j, k: (j, k)) if tb else pl.BlockSpec((tk, tn), lambda i, j, k: (k, j))
    o_spec = pl.BlockSpec((tm, tn), lambda i, j, k: (i, j))
    if isinstance(out_dtype, tuple):
        out_shape = [jax.ShapeDtypeStruct((M, N), d) for d in out_dtype]
    else:
        out_shape = jax.ShapeDtypeStruct((M, N), out_dtype)
    return _mm_call([(a, b)], (M // tm, N // tn, K // tk), a_spec, b_spec, o_spec,
                    out_shape, (tm, tn), K // tk,
                    0 if ta else 1, 1 if tb else 0, name, res=res, res_spec=o_spec, comm=comm)


def _mm_down(act, wd, res, name, tm=1024, tn=1024):
    _, S, FB = act.shape
    D = wd.shape[1]
    tm, tn = _tile(S, tm), _tile(D, tn)
    o_spec = pl.BlockSpec((tm, tn), lambda i, j, k: (i, j))
    nb = FFN_BLOCKS_PER_STEP
    return _mm_call([(act, wd)], (S // tm, D // tn, NDEV // nb),
                    pl.BlockSpec((nb, tm, FB), lambda i, j, k: (k, i, 0)),
                    pl.BlockSpec((nb * FB, tn), lambda i, j, k: (k, j)), o_spec,
                    jax.ShapeDtypeStruct((S, D), F32), (tm, tn), NDEV // nb, 1, 0, name, res=res, res_spec=o_spec)


def _mm_dwdown(act, dy, name, tn=1024):
    _, S, FB = act.shape
    D = dy.shape[1]
    tn = _tile(D, tn)
    tk = _k_tile(S, FB * act.dtype.itemsize + tn * dy.dtype.itemsize)
    return _mm_call([(act, dy)], (NDEV, D // tn, S // tk),
                    pl.BlockSpec((None, tk, FB), lambda i, j, k: (i, k, 0)),
                    pl.BlockSpec((tk, tn), lambda i, j, k: (k, j)),
                    pl.BlockSpec((FB, tn), lambda i, j, k: (i, j)),
                    jax.ShapeDtypeStruct((NDEV * FB, D), BF16), (FB, tn), S // tk, 0, 0, name)


def _mm_dh2(dg, wg, du, wu, name, tm=1024, tn=1024, comm=None):
    _, S, FB = dg.shape
    D = wg.shape[1]
    tm, tn = _tile(S, tm), _tile(D, tn)
    nb = FFN_BLOCKS_PER_STEP
    return _mm_call([(dg, wg), (du, wu)], (S // tm, D // tn, NDEV // nb),
                    pl.BlockSpec((nb, tm, FB), lambda i, j, k: (k, i, 0)),
                    pl.BlockSpec((nb, tn, FB), lambda i, j, k: (k, j, 0)),
                    pl.BlockSpec((tm, tn), lambda i, j, k: (i, j)),
                    jax.ShapeDtypeStruct((S, D), F32), (tm, tn), NDEV // nb, 1, 1, name, comm=comm)


def _mm_dwgate(h2, dg, name, tm=1024, comm=None):
    _, S, FB = dg.shape
    D = h2.shape[1]
    tm = _tile(D, tm)
    tk = _k_tile(S, tm * h2.dtype.itemsize + FB * dg.dtype.itemsize)
    return _mm_call([(h2, dg)], (NDEV, D // tm, S // tk),
                    pl.BlockSpec((tk, tm), lambda p, i, k: (k, i)),
                    pl.BlockSpec((None, tk, FB), lambda p, i, k: (p, k, 0)),
                    pl.BlockSpec((None, tm, FB), lambda p, i, k: (p, i, 0)),
                    jax.ShapeDtypeStruct((NDEV, D, FB), BF16), (tm, FB), S // tk, 0, 0, name, comm=comm)


def _ffn_up(h2, wg, wu, name, tm=512, comm=None):
    S, D = h2.shape
    FB = wg.shape[2]
    tm = _tile(S, tm)

    def body(h_ref, wg_ref, wu_ref, g_ref, u_ref, act_ref):
        h = h_ref[...]
        g = _dot_nn(h, wg_ref[...])
        u = _dot_nn(h, wu_ref[...])
        g_ref[...] = g.astype(BF16)
        u_ref[...] = u.astype(BF16)
        act_ref[...] = (g / (1.0 + jnp.exp(-g)) * u).astype(BF16)

    w_spec = pl.BlockSpec((None, D, FB), lambda p, i: (p, 0, 0))
    o_spec = pl.BlockSpec((None, tm, FB), lambda p, i: (p, i, 0))
    shp = (NDEV, S, FB)
    outs, moved = _pcall(
        body, name=name, grid=(NDEV, S // tm),
        in_specs=[pl.BlockSpec((tm, D), lambda p, i: (i, 0)), w_spec, w_spec],
        out_specs=[o_spec, o_spec, o_spec],
        out_shape=[jax.ShapeDtypeStruct(shp, BF16)] * 3,
        operands=[h2, wg, wu], sem=("parallel", "parallel"), comm=comm)
    return outs if comm is None else (outs, moved)


def _ffn_dact(dy, wd, g, u, name, tm=512):
    S, D = dy.shape
    FB = g.shape[2]
    tm = _tile(S, tm)

    def body(dy_ref, wd_ref, g_ref, u_ref, dg_ref, du_ref):
        dact = _dot_nt(dy_ref[...], wd_ref[...])
        gv = g_ref[...].astype(F32)
        sg = pl.reciprocal(1.0 + jnp.exp(-gv), approx=True)
        dg_ref[...] = (dact * u_ref[...].astype(F32) * (sg * (1.0 + gv * (1.0 - sg)))).astype(BF16)
        du_ref[...] = (dact * (gv * sg)).astype(BF16)

    t_spec = pl.BlockSpec((None, tm, FB), lambda p, i: (p, i, 0))
    shp = jax.ShapeDtypeStruct((NDEV, S, FB), BF16)
    return pl.pallas_call(
        body, name=name, grid=(NDEV, S // tm),
        in_specs=[pl.BlockSpec((tm, D), lambda p, i: (i, 0)), pl.BlockSpec((FB, D), lambda p, i: (p, 0)),
                  t_spec, t_spec],
        out_specs=[t_spec, t_spec], out_shape=[shp, shp],
        compiler_params=_cp("parallel", "parallel"),
    )(dy, wd, g, u)


def _rms_fwd(x, gain, width, cb, out_dtype, name, ts=512):
    S = x.shape[0]
    ts = _tile(S, ts)

    def body(x_ref, g_ref, o_ref):
        xv = x_ref[...]
        r = lax.rsqrt(jnp.mean(xv * xv, axis=1, keepdims=True) + EPS)
        o_ref[...] = (xv * r * g_ref[...]).astype(o_ref.dtype)

    return pl.pallas_call(
        body, name=name, grid=(S // ts,),
        in_specs=[pl.BlockSpec((ts, width), lambda i: (i, cb)), pl.BlockSpec((1, width), lambda i: (0, 0))],
        out_specs=pl.BlockSpec((ts, width), lambda i: (i, 0)),
        out_shape=jax.ShapeDtypeStruct((S, width), out_dtype),
        compiler_params=_cp("parallel"),
    )(x, gain)


def _rms_bwd(x, gain, dy, width, cb, name, res=None, ts=256):
    S = x.shape[0]
    ts = _tile(S, ts)
    has_res = res is not None

    def body(*refs):
        x_ref, g_ref, dy_ref = refs[:3]
        r_ref = refs[3] if has_res else None
        dx_ref, dxb_ref, dg_ref = refs[-3], refs[-2], refs[-1]

        @pl.when(pl.program_id(0) == 0)
        def _():
            dg_ref[...] = jnp.zeros_like(dg_ref)

        xv = x_ref[...]
        r = lax.rsqrt(jnp.mean(xv * xv, axis=1, keepdims=True) + EPS)
        xh = xv * r
        dyv = dy_ref[...]
        dyg = dyv * g_ref[...]
        dx = r * (dyg - xh * jnp.mean(dyg * xh, axis=1, keepdims=True))
        if has_res:
            dx = dx + r_ref[...]
        dx_ref[...] = dx
        dxb_ref[...] = dx.astype(BF16)
        dg_ref[...] += jnp.sum(dyv * xh, axis=0, keepdims=True)

    row = pl.BlockSpec((ts, width), lambda i: (i, 0))
    vec = pl.BlockSpec((1, width), lambda i: (0, 0))
    ops = [x, gain, dy] + ([res] if has_res else [])
    specs = [pl.BlockSpec((ts, width), lambda i: (i, cb)), vec, row] + ([row] if has_res else [])
    return pl.pallas_call(
        body, name=name, grid=(S // ts,), in_specs=specs, out_specs=[row, row, vec],
        out_shape=[jax.ShapeDtypeStruct((S, width), F32), jax.ShapeDtypeStruct((S, width), BF16),
                   jax.ShapeDtypeStruct((1, width), F32)],
        compiler_params=_cp("arbitrary"),
    )(*ops)


def _gn_fwd(outs, gain, name, ts=512):
    S, GW = outs[0].shape
    ts = _tile(S, ts)

    def body(a_ref, b_ref, c_ref, d_ref, g_ref, o_ref):
        for g, r_ref in enumerate((a_ref, b_ref, c_ref, d_ref)):
            xv = r_ref[...]
            r = lax.rsqrt(jnp.mean(xv * xv, axis=1, keepdims=True) + EPS)
            o_ref[:, g * GW:(g + 1) * GW] = (xv * r * g_ref[:, g * GW:(g + 1) * GW]).astype(BF16)

    row = pl.BlockSpec((ts, GW), lambda i: (i, 0))
    return pl.pallas_call(
        body, name=name, grid=(S // ts,),
        in_specs=[row] * 4 + [pl.BlockSpec((1, 4 * GW), lambda i: (0, 0))],
        out_specs=pl.BlockSpec((ts, 4 * GW), lambda i: (i, 0)),
        out_shape=jax.ShapeDtypeStruct((S, 4 * GW), BF16),
        compiler_params=_cp("parallel"),
    )(*outs, gain)


def _gn_bwd(outs, gain, dmix, name, ts=256):
    S, GW = outs[0].shape
    ts = _tile(S, ts)

    def body(a_ref, b_ref, c_ref, d_ref, g_ref, dm_ref, da_ref, db_ref, dc_ref, dd_ref, dg_ref):
        @pl.when(pl.program_id(0) == 0)
        def _():
            dg_ref[...] = jnp.zeros_like(dg_ref)

        for g, (r_ref, o_ref) in enumerate(zip((a_ref, b_ref, c_ref, d_ref), (da_ref, db_ref, dc_ref, dd_ref))):
            sl = slice(g * GW, (g + 1) * GW)
            xv = r_ref[...]
            r = lax.rsqrt(jnp.mean(xv * xv, axis=1, keepdims=True) + EPS)
            xh = xv * r
            dyv = dm_ref[:, sl]
            dyg = dyv * g_ref[:, sl]
            o_ref[...] = (r * (dyg - xh * jnp.mean(dyg * xh, axis=1, keepdims=True))).astype(BF16)
            dg_ref[:, sl] += jnp.sum(dyv * xh, axis=0, keepdims=True)

    row = pl.BlockSpec((ts, GW), lambda i: (i, 0))
    vec = pl.BlockSpec((1, 4 * GW), lambda i: (0, 0))
    return pl.pallas_call(
        body, name=name, grid=(S // ts,),
        in_specs=[row] * 4 + [vec, pl.BlockSpec((ts, 4 * GW), lambda i: (i, 0))],
        out_specs=[row] * 4 + [vec],
        out_shape=[jax.ShapeDtypeStruct((S, GW), BF16)] * 4 + [jax.ShapeDtypeStruct((1, 4 * GW), F32)],
        compiler_params=_cp("arbitrary"),
    )(*outs, gain, dmix)


def _rope_partner(x, half):
    if not half:
        return pltpu.roll(x, 64, 1)
    lane = lax.broadcasted_iota(jnp.int32, x.shape, 1)
    return jnp.where(lane % 64 < 32, pltpu.roll(x, LANE - 32, 1), pltpu.roll(x, 32, 1))


def _rope(x, cbs, cos, sin, half, out_dtype, name, ts=512):
    S = x.shape[0]
    cb0, nb, stride = cbs
    ts = _tile(S, ts)

    def body(x_ref, c_ref, s_ref, o_ref):
        xv = x_ref[...].astype(F32)
        o_ref[...] = (xv * c_ref[...] + _rope_partner(xv, half) * s_ref[...]).astype(o_ref.dtype)

    tab = pl.BlockSpec((ts, LANE), lambda i, j: (i, 0))
    return pl.pallas_call(
        body, name=name, grid=(S // ts, nb),
        in_specs=[pl.BlockSpec((ts, LANE), lambda i, j: (i, cb0 + stride * j)), tab, tab],
        out_specs=pl.BlockSpec((ts, LANE), lambda i, j: (i, j)),
        out_shape=jax.ShapeDtypeStruct((S, nb * LANE), out_dtype),
        compiler_params=_cp("parallel", "parallel"),
    )(x, cos, sin)


def _final_loss(x, gain, target, name, ts=256):
    S, D = x.shape
    ts = _tile(S, ts)

    def body(x_ref, g_ref, t_ref, dy_ref, l_ref):
        @pl.when(pl.program_id(0) == 0)
        def _():
            l_ref[...] = jnp.zeros_like(l_ref)

        xv = x_ref[...]
        r = lax.rsqrt(jnp.mean(xv * xv, axis=1, keepdims=True) + EPS)
        err = xv * r * g_ref[...] - t_ref[...]
        dy_ref[...] = err * (1.0 / D)
        part = jnp.sum(jnp.mean(err * err, axis=1, keepdims=True), axis=0, keepdims=True)
        l_ref[...] += jnp.broadcast_to(0.5 * part, (1, LANE))

    row = pl.BlockSpec((ts, D), lambda i: (i, 0))
    return pl.pallas_call(
        body, name=name, grid=(S // ts,),
        in_specs=[row, pl.BlockSpec((1, D), lambda i: (0, 0)), row],
        out_specs=[row, pl.BlockSpec((1, LANE), lambda i: (0, 0))],
        out_shape=[jax.ShapeDtypeStruct((S, D), F32), jax.ShapeDtypeStruct((1, LANE), F32)],
        compiler_params=_cp("arbitrary"),
    )(x, gain, target)


def _colspec(rows, f):
    return pl.BlockSpec((rows, LANE), f)


def _soft_tiles(S):
    tq = _tile(S, TQ)
    tk = _tile(S, TKS)
    assert tk % tq == 0
    return tq, tk


def _key_row(crow_ref, j, tk):
    n = tk // TK
    return jnp.concatenate([crow_ref[j * n + c] for c in range(n)], axis=1)


def _attn_fwd(S, H, q1, q1cb, k1, k1cb, v, vcb, scale, name, q2=None, q2cb=None, k2=None, k2cb=None,
              tab=None, win=None, ccol=None, crow=None, comm=None):
    tq, tk = _soft_tiles(S)
    has2, hastab, hasc = q2 is not None, tab is not None, ccol is not None

    def body(*refs):
        it = iter(refs)
        q1r, k1r, vr = next(it), next(it), next(it)
        q2r, k2r = (next(it), next(it)) if has2 else (None, None)
        tabr = next(it) if hastab else None
        ccolr, crowr = (next(it), next(it)) if hasc else (None, None)
        o_ref, lse_ref = next(it), next(it)
        i = pl.program_id(1)
        q = q1r[...]
        qb2 = q2r[...] if has2 else None
        cq = ccolr[:, 0:1] if hasc else None
        qpos = i * tq + lax.broadcasted_iota(jnp.int32, (tq, tk), 0)
        kio = lax.broadcasted_iota(jnp.int32, (tq, tk), 1)
        j_diag = (i * tq) // tk
        j_lo = jnp.maximum((i * tq - win) // tk, 0) if win else 0

        if has2:
            q = jnp.concatenate([q, qb2], axis=1)

        def step(j, carry, masked):
            m, l, acc = carry
            off = pl.multiple_of(j * tk, tk)
            kb = k1r[pl.ds(off, tk), :]
            if has2:
                kb = jnp.concatenate([kb, k2r[pl.ds(off, tk), :]], axis=1)
            s = _dot_nt(q, kb) * scale
            if hastab:
                s = s + tabr[i - j * (tk // tq)]
            else:
                if hasc:
                    s = s + (cq - _key_row(crowr, j, tk))
                if masked:
                    s = jnp.where(kio + j * tk <= qpos, s, NEG)
            mn = jnp.maximum(m, jnp.max(s, axis=1, keepdims=True))
            p = jnp.exp(s - mn)
            al = jnp.exp(m - mn)
            l = al * l + jnp.sum(p, axis=1, keepdims=True)
            vb = vr[pl.ds(off, tk), :]
            ph = p.astype(BF16)
            pv = _dot_nn(ph, vb)
            if hasc:
                pv = pv + _dot_nn((p - ph.astype(F32)).astype(BF16), vb)
            acc = al * acc + pv
            return mn, l, acc

        carry = (jnp.full((tq, 1), NEG, F32), jnp.zeros((tq, 1), F32), jnp.zeros((tq, LANE), F32))
        if hastab:
            carry = lax.fori_loop(j_lo, j_diag + 1, functools.partial(step, masked=False), carry)
        else:
            carry = lax.fori_loop(j_lo, j_diag, functools.partial(step, masked=False), carry)
            carry = step(j_diag, carry, True)
        m, l, acc = carry
        o_ref[...] = acc / l
        lse_ref[...] = jnp.broadcast_to(m + jnp.log(l), (tq, LANE))

    ops = [q1, k1, v]
    specs = [_colspec(tq, lambda h, i: (i, q1cb(h))), _colspec(S, lambda h, i: (0, k1cb(h))),
             _colspec(S, lambda h, i: (0, vcb(h)))]
    if has2:
        ops += [q2, k2]
        specs += [_colspec(tq, lambda h, i: (i, q2cb(h))), _colspec(S, lambda h, i: (0, k2cb(h)))]
    if hastab:
        ops.append(tab)
        specs.append(pl.BlockSpec(tab.shape, lambda h, i: (0, 0, 0)))
    if hasc:
        ops += [ccol, crow]
        specs += [_colspec(tq, lambda h, i: (i, h)),
                  pl.BlockSpec((None, S // TK, 1, TK), lambda h, i: (h, 0, 0, 0))]
    o_spec = _colspec(tq, lambda h, i: (i, h))
    shp = jax.ShapeDtypeStruct((S, H * LANE), F32)
    outs, moved = _pcall(
        body, name=name, grid=(H, S // tq), in_specs=specs, out_specs=[o_spec, o_spec], out_shape=[shp, shp],
        operands=ops, sem=("parallel", "arbitrary"), comm=comm)
    return outs if comm is None else (outs, moved)


def _attn_bwd(S, H, q1, q1cb, k1, k1cb, v, vcb, o, do, lse, scale, name, q2=None, q2cb=None, k2=None, k2cb=None,
              rope2=None, tab=None, win=None, ccol=None, crow=None, comm=None):
    tq, tk = _soft_tiles(S)
    has2, hastab, hasc = q2 is not None, tab is not None, ccol is not None

    def body(*refs):
        it = iter(refs)
        q1r, k1r, vr, o_r, do_r, lse_r = (next(it) for _ in range(6))
        q2r, k2r, cos2_r, sin2_r = (next(it), next(it), next(it), next(it)) if has2 else (None,) * 4
        tabr = next(it) if hastab else None
        ccolr, crowr = (next(it), next(it)) if hasc else (None, None)
        dq1_r, dk1_o, dv_o = next(it), next(it), next(it)
        dk2_o = dv_o
        dcr_r = next(it) if hasc else None
        dk1_r, dv_r = next(it), next(it)
        dk2_r = next(it) if has2 else None
        i = pl.program_id(1)

        @pl.when(i == 0)
        def _():
            dk1_r[...] = jnp.zeros_like(dk1_r)
            dv_r[...] = jnp.zeros_like(dv_r)
            if has2:
                dk2_r[...] = jnp.zeros_like(dk2_r)
            if hasc:
                dcr_r[...] = jnp.zeros_like(dcr_r)

        q = q1r[...]
        qb2 = q2r[...] if has2 else None
        dob = do_r[...]
        delta = jnp.sum(dob.astype(F32) * o_r[...], axis=1, keepdims=True)
        lse_c = lse_r[:, 0:1]
        cq = ccolr[:, 0:1] if hasc else None
        qpos = i * tq + lax.broadcasted_iota(jnp.int32, (tq, tk), 0)
        kio = lax.broadcasted_iota(jnp.int32, (tq, tk), 1)
        j_diag = (i * tq) // tk
        j_lo = jnp.maximum((i * tq - win) // tk, 0) if win else 0

        if has2:
            q = jnp.concatenate([q, qb2], axis=1)

        def probs(j, masked):
            off = pl.multiple_of(j * tk, tk)
            kb = k1r[pl.ds(off, tk), :]
            if has2:
                kb = jnp.concatenate([kb, k2r[pl.ds(off, tk), :]], axis=1)
            s = _dot_nt(q, kb) * scale
            if hastab:
                s = s + tabr[i - j * (tk // tq)]
            else:
                if hasc:
                    s = s + (cq - _key_row(crowr, j, tk))
                if masked:
                    s = jnp.where(kio + j * tk <= qpos, s, NEG)
            p = jnp.exp(s - lse_c)
            dp = _dot_nt(dob, vr[pl.ds(off, tk), :])
            return off, kb, p, dp

        def sweep(fn, carry):
            if hastab:
                return lax.fori_loop(j_lo, j_diag + 1, functools.partial(fn, masked=False), carry)
            carry = lax.fori_loop(j_lo, j_diag, functools.partial(fn, masked=False), carry)
            return fn(j_diag, carry, True)

        def step(j, dq, masked):
            off, kb, p, dp = probs(j, masked)
            ds = p * (dp - delta)
            dsb = ds.astype(BF16)
            dk = _dot_tn(dsb, q) * scale
            dk1_r[pl.ds(off, tk), :] += dk[:, :LANE]
            if has2:
                dk2_r[pl.ds(off, tk), :] += dk[:, LANE:]
            dv_r[pl.ds(off, tk), :] += _dot_tn(p.astype(BF16), dob)
            if hasc:
                cs = -jnp.sum(ds, axis=0, keepdims=True)
                for c in range(tk // TK):
                    dcr_r[j * (tk // TK) + c] += cs[:, c * TK:(c + 1) * TK]
            return dq + _dot_nn(dsb, kb)

        dq = sweep(step, jnp.zeros(q.shape, F32)) * scale
        dq1_r[:, :LANE] = dq[:, :LANE].astype(BF16)
        if has2:
            x2 = dq[:, LANE:]
            dq1_r[:, LANE:] = (x2 * cos2_r[...] + _rope_partner(x2, True) * sin2_r[...]).astype(BF16)

        @pl.when(i == S // tq - 1)
        def _():
            dk1_o[:, :LANE] = dk1_r[...].astype(BF16)
            if has2:
                dk1_o[:, LANE:] = dv_r[...].astype(BF16)
                dk2_o[...] = dk2_r[...].astype(BF16)
            else:
                dv_o[...] = dv_r[...].astype(BF16)

    qspec = _colspec(tq, lambda h, i: (i, h))
    kspec = _colspec(S, lambda h, i: (0, h))
    ops = [q1, k1, v, o, do, lse]
    specs = [_colspec(tq, lambda h, i: (i, q1cb(h))), _colspec(S, lambda h, i: (0, k1cb(h))),
             _colspec(S, lambda h, i: (0, vcb(h))), qspec, qspec, qspec]
    if has2:
        ops += [q2, k2, *rope2]
        tab2 = _colspec(tq, lambda h, i: (i, 0))
        specs += [_colspec(tq, lambda h, i: (i, q2cb(h))), _colspec(S, lambda h, i: (0, k2cb(h))), tab2, tab2]
    if hastab:
        ops.append(tab)
        specs.append(pl.BlockSpec(tab.shape, lambda h, i: (0, 0, 0)))
    if hasc:
        ops += [ccol, crow]
        specs += [qspec, pl.BlockSpec((None, S // TK, 1, TK), lambda h, i: (h, 0, 0, 0))]
    assert all(t.dtype == BF16 for t in ops[:3] + [do] + ([q2, k2] if has2 else []))
    shp = jax.ShapeDtypeStruct((S, H * LANE), BF16)
    if has2:
        out_specs = [pl.BlockSpec((tq, 2 * LANE), lambda h, i: (i, h)), pl.BlockSpec((S, 2 * LANE), lambda h, i: (0, h)),
                     kspec]
        wide = jax.ShapeDtypeStruct((S, H * 2 * LANE), BF16)
        out_shape = [wide, wide, shp]
    else:
        out_specs = [qspec, kspec, kspec]
        out_shape = [shp] * 3
    if hasc:
        out_specs.append(pl.BlockSpec((None, S // TK, 1, TK), lambda h, i: (h, 0, 0, 0)))
        out_shape.append(jax.ShapeDtypeStruct((H, S // TK, 1, TK), F32))
    outs, moved = _pcall(
        body, name=name, grid=(H, S // tq), in_specs=specs, out_specs=out_specs, out_shape=out_shape,
        operands=ops, scratch_shapes=[pltpu.VMEM((S, LANE), F32)] * (3 if has2 else 2),
        sem=("parallel", "arbitrary"), comm=comm)
    return outs if comm is None else (outs, moved)


def _scan_matrix(kind):
    j = np.arange(TK)[:, None]
    s = np.arange(TK)[None, :]
    tri = {"suffix_ex": j > s, "prefix_in": j <= s, "prefix_ex": j < s}[kind].astype(np.float32)
    half = np.concatenate([tri, np.ones((TK, TK), np.float32)], axis=1)
    return jnp.asarray(np.concatenate([half, half], axis=0), BF16)


def _scan_mxu(x, mat, carry, reverse, split=True):
    n = x.shape[1] // TK
    hi = x.astype(BF16)
    if split:
        lo = (x - hi.astype(F32)).astype(BF16)
    else:
        mat = mat[:TK]
    parts = [None] * n
    for b in (reversed(range(n)) if reverse else range(n)):
        sl = slice(b * TK, (b + 1) * TK)
        r = _dot_nn(jnp.concatenate([hi[:, sl], lo[:, sl]], axis=1) if split else hi[:, sl], mat)
        parts[b] = r[:, :TK] + carry
        carry = carry + r[:, TK:]
    return jnp.concatenate(parts, axis=1), carry


def _stick_logs(z):
    e = jnp.exp(-jnp.abs(z))
    return e, -jnp.maximum(z, 0.0) - jnp.log(1.0 + e)


def _stick_fwd(S, H, x, qcb, kcb, vcb, scale, name, comm=None):
    tq, tk = _soft_tiles(S)
    assert x.dtype == BF16

    def body(q_r, k_r, v_r, mat_r, o_ref, t_ref):
        i = pl.program_id(1)
        q = q_r[...]
        qpos = i * tq + lax.broadcasted_iota(jnp.int32, (tq, tk), 0)
        lane = lax.broadcasted_iota(jnp.int32, (tq, tk), 1)
        j_diag = (i * tq) // tk

        def step(j, carry, masked):
            c, acc = carry
            off = pl.multiple_of(j * tk, tk)
            z = _dot_nt(q, k_r[pl.ds(off, tk), :]) * scale
            _, lk = _stick_logs(z)
            if masked:
                past = lane + j * tk < qpos
                lk = jnp.where(past, lk, 0.0)
            suf, c = _scan_mxu(lk, mat_r[...], c, True)
            a = jnp.exp(z + lk + suf)
            if masked:
                a = jnp.where(past, a, 0.0)
            acc = acc + _dot_nn(a.astype(BF16), v_r[pl.ds(off, tk), :])
            return c, acc

        carry = step(j_diag, (jnp.zeros((tq, TK), F32), jnp.zeros((tq, LANE), F32)), True)
        c, acc = lax.fori_loop(0, j_diag, lambda jj, cr: step(j_diag - 1 - jj, cr, False), carry)
        o_ref[...] = acc
        t_ref[...] = c

    o_spec = _colspec(tq, lambda h, i: (i, h))
    shp = jax.ShapeDtypeStruct((S, H * LANE), F32)
    mat = _scan_matrix("suffix_ex")
    outs, moved = _pcall(
        body, name=name, grid=(H, S // tq),
        in_specs=[_colspec(tq, lambda h, i: (i, qcb(h))), _colspec(S, lambda h, i: (0, kcb(h))),
                  _colspec(S, lambda h, i: (0, vcb(h))), pl.BlockSpec(mat.shape, lambda h, i: (0, 0))],
        out_specs=[o_spec, o_spec], out_shape=[shp, shp],
        operands=[x, x, x, mat], sem=("parallel", "arbitrary"), comm=comm)
    return outs if comm is None else (outs, moved)


def _stick_bwd(S, H, x, qcb, kcb, vcb, do, tot, scale, name, comm=None):
    tq, tk = _soft_tiles(S)
    assert x.dtype == BF16 and do.dtype == BF16

    def body(q_r, k_r, v_r, do_r, t_r, pin_r, pex_r, dq_r, dk_o, dv_o, dk_r, dv_r):
        i = pl.program_id(1)

        @pl.when(i == 0)
        def _():
            dk_r[...] = jnp.zeros_like(dk_r)
            dv_r[...] = jnp.zeros_like(dv_r)

        q = q_r[...]
        dob = do_r[...]
        total = jnp.concatenate([t_r[...]] * (tk // TK), axis=1)
        qpos = i * tq + lax.broadcasted_iota(jnp.int32, (tq, tk), 0)
        lane = lax.broadcasted_iota(jnp.int32, (tq, tk), 1)
        j_diag = (i * tq) // tk

        def step(j, carry, masked):
            cl, cg, dq = carry
            off = pl.multiple_of(j * tk, tk)
            kb = k_r[pl.ds(off, tk), :]
            z = _dot_nt(q, kb) * scale
            e, lk = _stick_logs(z)
            if masked:
                past = lane + j * tk < qpos
                lk = jnp.where(past, lk, 0.0)
            pre, cl = _scan_mxu(lk, pin_r[...], cl, False)
            a = jnp.exp(z + lk + (total - pre))
            if masked:
                a = jnp.where(past, a, 0.0)
            g = _dot_nt(dob, v_r[pl.ds(off, tk), :]) * a
            gpre, cg = _scan_mxu(g, pex_r[...], cg, False, split=False)
            inv = pl.reciprocal(1.0 + e, approx=True)
            small = e * inv
            pos = z >= 0
            dz = g * jnp.where(pos, small, inv) - jnp.where(pos, inv, small) * gpre
            if masked:
                dz = jnp.where(past, dz, 0.0)
            dzb = dz.astype(BF16)
            dk_r[pl.ds(off, tk), :] += _dot_tn(dzb, q) * scale
            dv_r[pl.ds(off, tk), :] += _dot_tn(a.astype(BF16), dob)
            return cl, cg, dq + _dot_nn(dzb, kb)

        zt = jnp.zeros((tq, TK), F32)
        carry = lax.fori_loop(0, j_diag, functools.partial(step, masked=False), (zt, zt, jnp.zeros((tq, LANE), F32)))
        dq_r[...] = (step(j_diag, carry, True)[2] * scale).astype(BF16)

        @pl.when(i == S // tq - 1)
        def _():
            dk_o[...] = dk_r[...].astype(BF16)
            dv_o[...] = dv_r[...].astype(BF16)

    qspec = _colspec(tq, lambda h, i: (i, h))
    kspec = _colspec(S, lambda h, i: (0, h))
    shp = jax.ShapeDtypeStruct((S, H * LANE), BF16)
    pin, pex = _scan_matrix("prefix_in"), _scan_matrix("prefix_ex")
    mspec = pl.BlockSpec(pin.shape, lambda h, i: (0, 0))
    outs, moved = _pcall(
        body, name=name, grid=(H, S // tq),
        in_specs=[_colspec(tq, lambda h, i: (i, qcb(h))), _colspec(S, lambda h, i: (0, kcb(h))),
                  _colspec(S, lambda h, i: (0, vcb(h))), qspec, qspec, mspec, mspec],
        out_specs=[qspec, kspec, kspec], out_shape=[shp] * 3,
        operands=[x, x, x, do, tot, pin, pex], scratch_shapes=[pltpu.VMEM((S, LANE), F32)] * 2,
        sem=("parallel", "arbitrary"), comm=comm)
    return outs if comm is None else (outs, moved)


def _scan8(x, rows, reverse):
    for sh in (1, 2, 4):
        if reverse:
            x = x + jnp.where(rows + sh < 8, pltpu.roll(x, 8 - sh, 0), 0.0)
        else:
            x = x + jnp.where(rows >= sh, pltpu.roll(x, sh, 0), 0.0)
    return x


def _fox_prep(S, H, proj, fcb, bias, name):
    tk = TK

    def body(f_ref, b_ref, ccol_ref, crow_ref, scr):
        rows = lax.broadcasted_iota(jnp.int32, (8, LANE), 0)

        def step(t, carry):
            off = pl.multiple_of(t * 8, 8)
            xb = f_ref[pl.ds(off, 8), :] + b_ref[...]
            lf = jnp.minimum(xb, 0.0) - jnp.log(1.0 + jnp.exp(-jnp.abs(xb)))
            lf = _scan8(lf, rows, False) + carry
            scr[pl.ds(off, 8), :] = lf
            return lf[7:8, :]

        lax.fori_loop(0, S // 8, step, jnp.zeros((1, LANE), F32))
        for h in range(H):
            ccol_ref[:, h * LANE:(h + 1) * LANE] = jnp.broadcast_to(scr[:, h:h + 1], (S, LANE))

            def tr(t, _):
                off = pl.multiple_of(t * tk, tk)
                blk = ccol_ref[pl.ds(off, tk), h * LANE:(h + 1) * LANE]
                crow_ref[h, t] = blk.T[0:1, :]
                return 0

            lax.fori_loop(0, S // tk, tr, 0)

    return pl.pallas_call(
        body, name=name, grid=(1,),
        in_specs=[_colspec(S, lambda i: (0, fcb)), pl.BlockSpec((1, LANE), lambda i: (0, 0))],
        out_specs=[pl.BlockSpec((S, H * LANE), lambda i: (0, 0)),
                   pl.BlockSpec((H, S // tk, 1, tk), lambda i: (0, 0, 0, 0))],
        out_shape=[jax.ShapeDtypeStruct((S, H * LANE), F32), jax.ShapeDtypeStruct((H, S // tk, 1, tk), F32)],
        scratch_shapes=[pltpu.VMEM((S, LANE), F32)],
        compiler_params=_cp("arbitrary"),
    )(proj, bias)


def _fox_bwd(S, H, proj, fcb, bias, dcr, name):
    tk = TK

    def body(f_ref, b_ref, dcr_ref, df_ref, db_ref, scr):
        rows = lax.broadcasted_iota(jnp.int32, (8, LANE), 0)
        lane_t = lax.broadcasted_iota(jnp.int32, (tk, LANE), 1)
        nb = S // 8

        def tr(t, _):
            off = pl.multiple_of(t * tk, tk)
            d = jnp.zeros((tk, LANE), F32)
            for h in range(H):
                d = d + jnp.where(lane_t == h, jnp.broadcast_to(dcr_ref[h, t], (LANE, tk)).T, 0.0)
            scr[pl.ds(off, tk), :] = d
            return 0

        lax.fori_loop(0, S // tk, tr, 0)

        def step(tt, carry):
            suffix, db = carry
            off = pl.multiple_of((nb - 1 - tt) * 8, 8)
            d = _scan8(scr[pl.ds(off, 8), :], rows, True) + suffix
            xb = f_ref[pl.ds(off, 8), :] + b_ref[...]
            e = jnp.exp(-jnp.abs(xb))
            dx = d * jnp.where(xb >= 0, e, 1.0) / (1.0 + e)
            df_ref[pl.ds(off, 8), :] = dx
            return d[0:1, :], db + jnp.sum(dx, axis=0, keepdims=True)

        z = jnp.zeros((1, LANE), F32)
        _, db = lax.fori_loop(0, nb, step, (z, z))
        db_ref[...] = db

    return pl.pallas_call(
        body, name=name, grid=(1,),
        in_specs=[_colspec(S, lambda i: (0, fcb)), pl.BlockSpec((1, LANE), lambda i: (0, 0)),
                  pl.BlockSpec((H, S // tk, 1, tk), lambda i: (0, 0, 0, 0))],
        out_specs=[pl.BlockSpec((S, LANE), lambda i: (0, 0)), pl.BlockSpec((1, LANE), lambda i: (0, 0))],
        out_shape=[jax.ShapeDtypeStruct((S, LANE), F32), jax.ShapeDtypeStruct((1, LANE), F32)],
        scratch_shapes=[pltpu.VMEM((S, LANE), F32)],
        compiler_params=_cp("arbitrary"),
    )(proj, bias, dcr)


def _adamw(w, slots, m, v, name, comm=None):
    R, C = w.shape
    parts = len(slots)
    rows = R // parts
    row_bytes = 2 * C * (parts * NDEV * slots[0].dtype.itemsize + 7 * 4)
    sub = 32 // slots[0].dtype.itemsize
    tiles = [t for t in range(sub, rows + 1, sub) if rows % t == 0] or [rows]
    tr = max([t for t in tiles if t * row_bytes <= ADAMW_BLOCK_BYTES] or tiles[:1])
    per = R // parts // tr
    c1 = 1.0 - ADAM_B1 ** ADAM_STEP
    c2 = 1.0 - ADAM_B2 ** ADAM_STEP

    def body(*refs):
        w_ref, m_ref, v_ref = refs[:3]
        s_refs = refs[3:3 + parts]
        g_ref, d_ref, nm_ref, nv_ref = refs[3 + parts:]
        part = pl.program_id(0) // per
        for a, s_ref in enumerate(s_refs):
            @pl.when(part == a)
            def _():
                g = s_ref[0].astype(F32)
                for s in range(1, NDEV):
                    g = g + s_ref[s].astype(F32)
                g_ref[...] = g

        g = g_ref[...]
        mn = ADAM_B1 * m_ref[...] + (1.0 - ADAM_B1) * g
        vn = ADAM_B2 * v_ref[...] + (1.0 - ADAM_B2) * (g * g)
        nm_ref[...] = mn
        nv_ref[...] = vn
        d_ref[...] = -ADAM_LR * ((mn / c1) / (jnp.sqrt(vn / c2) + ADAM_EPS) + ADAM_WD * w_ref[...])

    row = pl.BlockSpec((tr, C), lambda i: (i, 0))
    s_specs = [pl.BlockSpec((NDEV, tr, C), lambda i, a=a: (0, jnp.clip(i - a * per, 0, per - 1), 0))
               for a in range(parts)]
    outs, moved = _pcall(
        body, name=name, grid=(R // tr,), in_specs=[row, row, row] + s_specs,
        out_specs=[row] * 4, out_shape=[jax.ShapeDtypeStruct((R, C), F32)] * 4,
        operands=[w, m, v, *slots], sem=("arbitrary",), comm=comm)
    return outs if comm is None else (outs, moved)


class _Layout:
    def __init__(self, D):
        self.GW = GW = D // 4
        self.H = H = GW // HEAD
        self.QL, self.KVL = 0, Q_LORA
        base = Q_LORA + KV_LORA
        (self.QB, self.KB, self.VB, self.QC, self.KC, self.VC, self.QD, self.KD, self.VD) = (
            base + k * GW for k in range(9))
        self.KR = base + 9 * GW
        self.FC = self.KR + LANE
        self.PW = -(-(self.FC + LANE) // 512) * 512
        self.o_kr = base
        self.o_bc = base + QK_ROPE
        self.o_fc = self.o_bc + 6 * GW
        self.o_d = self.o_fc + H
        self.IN = self.o_d + 3 * GW

    def pad(self, w):
        z = lambda n: jnp.zeros(w.shape[:-1] + (n,), w.dtype)
        return jnp.concatenate([
            w[..., :self.o_kr], w[..., self.o_bc:self.o_fc], w[..., self.o_d:self.IN],
            w[..., self.o_kr:self.o_bc], z(LANE - QK_ROPE), w[..., self.o_fc:self.o_d], z(LANE - self.H),
            z(self.PW - self.FC - LANE)], axis=-1)

    def unpad(self, g):
        return jnp.concatenate([
            g[..., :self.KR - 9 * self.GW], g[..., self.KR:self.KR + QK_ROPE], g[..., self.QB:self.QD],
            g[..., self.FC:self.FC + self.H], g[..., self.QD:self.KR]], axis=-1)


def _rope_tables(S):
    pos = jnp.arange(S, dtype=F32)

    def cs(dim):
        inv = ROPE_THETA ** (-jnp.arange(0, dim, 2, dtype=F32) / dim)
        ang = pos[:, None] * inv[None, :]
        return jnp.cos(ang), jnp.sin(ang)

    c, s = cs(HEAD)
    full = (jnp.concatenate([c, c], 1), jnp.concatenate([-s, s], 1))
    c, s = cs(QK_ROPE)
    z = jnp.zeros((S, LANE - QK_ROPE), F32)
    half = (jnp.concatenate([c, c, z], 1), jnp.concatenate([-s, s, z], 1))
    return full, half


def _dilated_table(tq, tk):
    win = max(w for w, _ in DILATED_PAIRS)
    nd = (win + tk) // tq + 1
    d = np.arange(nd)[:, None, None] * tq + np.arange(tq)[None, :, None] - np.arange(tk)[None, None, :]
    mult = np.zeros(d.shape, np.float64)
    for w, dil in DILATED_PAIRS:
        mult += (d >= 0) & (d <= w) & (d % dil == 0)
    return jnp.asarray(np.where(mult > 0, np.log(np.maximum(mult, 1.0)), NEG), F32), win


def _pack(arrs):
    rows = []
    for a in arrs:
        f = a.reshape(-1).astype(F32)
        f = jnp.pad(f, (0, (-f.shape[0]) % LANE))
        rows.append(f.reshape(-1, LANE))
    p = jnp.concatenate(rows, 0)
    return jnp.pad(p, ((0, (-p.shape[0]) % 8), (0, 0)))


def _unpack(p, shapes):
    out, r = [], 0
    for shp in shapes:
        n = int(np.prod(shp))
        nr = -(-n // LANE)
        out.append(p[r:r + nr].reshape(-1)[:n].reshape(shp))
        r += nr
    return out


def kernel(x, attn_norm, w_in, mla_q_norm, w_uq, mla_kv_norm, w_ukv, fox_forget_bias, group_norm, w_out, ffn_norm, w_gate, w_up, w_down, final_norm, loss_target, m_attn_norm, m_w_in, m_mla_q_norm, m_w_uq, m_mla_kv_norm, m_w_ukv, m_fox_forget_bias, m_group_norm, m_w_out, m_ffn_norm, m_w_gate, m_w_up, m_w_down, m_final_norm, v_attn_norm, v_w_in, v_mla_q_norm, v_w_uq, v_mla_kv_norm, v_w_ukv, v_fox_forget_bias, v_group_norm, v_w_out, v_ffn_norm, v_w_gate, v_w_up, v_w_down, v_final_norm):
    _, S, D = x.shape
    L = attn_norm.shape[0]
    lay = _Layout(D)
    H, GW, PW = lay.H, lay.GW, lay.PW
    FB = w_gate.shape[2]
    QKA = HEAD + QK_ROPE
    x = x[0]
    target = loss_target[0]
    rope_full, rope_half = _rope_tables(S)
    neg = lambda t: (t[0], -t[1])
    tab, win = _dilated_table(*_soft_tiles(S))
    cb = lambda col: col // LANE

    sh = dict(w_in=lay.pad(w_in).astype(BF16),
              **{n: w.astype(BF16) for n, w in (("w_uq", w_uq), ("w_ukv", w_ukv), ("w_out", w_out),
                                                  ("w_gate", w_gate), ("w_up", w_up), ("w_down", w_down))})
    first3 = ["w_in", "w_uq", "w_ukv"]

    def first_weights(g):
        wuq = jnp.transpose(g[1], (1, 0, 2)).reshape(Q_LORA, H, QKA)
        wuq = jnp.pad(wuq, ((0, 0), (0, 0), (0, 2 * LANE - QKA))).reshape(Q_LORA, H * 2 * LANE)
        return dict(win=g[0].reshape(D, PW), wuq=wuq,
                    wukv=jnp.transpose(g[2], (1, 0, 2)).reshape(KV_LORA, H * 2 * LANE))

    def row(a):
        return a.reshape(1, -1)

    def forward(l, x0, W):
        A = dict(x0=x0)
        A["bias"] = jnp.pad(row(fox_forget_bias[l]), ((0, 0), (0, LANE - H)))
        h1 = A["h1"] = _rms_fwd(x0, row(attn_norm[l]), D, 0, BF16, "attn_norm")
        (proj, pb), (W["wg"],) = _mm(h1, W["win"], "in_proj", out_dtype=(F32, BF16),
                                     comm=("gather", [sh["w_gate"][l]]))
        A["proj"], A["pb"] = proj, pb
        qln = A["qln"] = _rms_fwd(proj, row(mla_q_norm[l]), Q_LORA, cb(lay.QL) // 4, BF16, "q_norm")
        kvln = A["kvln"] = _rms_fwd(proj, row(mla_kv_norm[l]), KV_LORA, cb(lay.KVL) // 4, BF16, "kv_norm")
        qa, qab = _mm(qln, W["wuq"], "q_up", out_dtype=(F32, BF16))
        A["qab"] = qab
        kv = A["kv"] = _mm(kvln, W["wukv"], "kv_up", out_dtype=BF16)
        q_pe = A["q_pe"] = _rope(qa, (1, H, 2), *rope_half, True, BF16, "rope_q_mla")
        k_pe = A["k_pe"] = _rope(proj, (cb(lay.KR), 1, 1), *rope_half, True, BF16, "rope_k_mla")
        (A["o_a"], A["lse_a"]), (W["wu"],) = _attn_fwd(
            S, H, qab, lambda h: 2 * h, kv, lambda h: 2 * h, kv, lambda h: 2 * h + 1, QKA ** -0.5, "mla_fwd",
            q2=q_pe, q2cb=lambda h: h, k2=k_pe, k2cb=lambda h: 0, comm=("gather", [sh["w_up"][l]]))
        qk_b = A["qk_b"] = _rope(proj, (cb(lay.QB), 2 * H, 1), *rope_full, False, BF16, "rope_qk_dil")
        (A["o_b"], A["lse_b"]), (g_down,) = _attn_fwd(
            S, H, qk_b, lambda h: h, qk_b, lambda h: H + h, pb, lambda h: cb(lay.VB) + h, HEAD ** -0.5,
            "dilated_fwd", tab=tab, win=win, comm=("gather", [sh["w_down"][l]]))
        W["wd"] = g_down.reshape(NDEV * FB, D)
        ccol, crow = A["ccol"], A["crow"] = _fox_prep(S, H, proj, cb(lay.FC), A["bias"], "fox_prep")
        (A["o_c"], A["lse_c"]), (g_out,) = _attn_fwd(
            S, H, pb, lambda h: cb(lay.QC) + h, pb, lambda h: cb(lay.KC) + h, pb, lambda h: cb(lay.VC) + h,
            HEAD ** -0.5, "fox_fwd", ccol=ccol, crow=crow, comm=("gather", [sh["w_out"][l]]))
        W["wout"] = g_out.reshape(4 * GW, D)
        A["o_d"], A["tot_d"] = _stick_fwd(
            S, H, pb, lambda h: cb(lay.QD) + h, lambda h: cb(lay.KD) + h, lambda h: cb(lay.VD) + h,
            HEAD ** -0.5, "stick_fwd")
        mix = A["mix"] = _gn_fwd([A["o_a"], A["o_b"], A["o_c"], A["o_d"]], row(group_norm[l]), "group_norm")
        x1 = A["x1"] = _mm(mix, W["wout"], "out_proj", res=x0)
        h2 = A["h2"] = _rms_fwd(x1, row(ffn_norm[l]), D, 0, BF16, "ffn_norm")
        nxt = None
        if l + 1 < L:
            (A["g"], A["u"], A["act"]), nxt = _ffn_up(
                h2, W["wg"], W["wu"], "ffn_up", comm=("gather", [sh[n][l + 1] for n in first3]))
        else:
            A["g"], A["u"], A["act"] = _ffn_up(h2, W["wg"], W["wu"], "ffn_up_last")
        return _mm_down(A["act"], W["wd"], x1, "ffn_down"), A, nxt

    def backward(l, dx2, dx2b, W, A, late):
        proj, pb = A["proj"], A["pb"]
        G, small, got = {}, {}, {}
        dgate, dup = _ffn_dact(dx2b, W["wd"], A["g"], A["u"], "ffn_dact")
        G["w_down"] = _mm_dwdown(A["act"], dx2b, "dw_down").reshape(NDEV, FB, D)
        dh2, (got[l, "w_down"],) = _mm_dh2(dgate, W["wg"], dup, W["wu"], "ffn_dh", comm=("exchange", [G["w_down"]]))
        if late is None:
            G["w_gate"] = _mm_dwgate(A["h2"], dgate, "dw_gate_top")
            G["w_up"] = _mm_dwgate(A["h2"], dup, "dw_up_top")
        else:
            G["w_gate"], (got[l + 1, "w_in_a"],) = _mm_dwgate(
                A["h2"], dgate, "dw_gate", comm=("exchange", [late["w_in_a"]]))
            G["w_up"], (got[l + 1, "w_in_b"],) = _mm_dwgate(
                A["h2"], dup, "dw_up", comm=("exchange", [late["w_in_b"]]))
        dx1, dx1b, small["ffn_norm"] = _rms_bwd(A["x1"], row(ffn_norm[l]), dh2, D, 0, "ffn_norm_bwd", res=dx2)
        dmix = _mm(dx1b, W["wout"], "out_proj_dx", tb=True)
        G["w_out"] = _mm(A["mix"], dx1b, "dw_out", ta=True, out_dtype=BF16).reshape(NDEV, 4 * GW // NDEV, D)
        do_a, do_b, do_c, do_d, small["group_norm"] = _gn_bwd(
            [A["o_a"], A["o_b"], A["o_c"], A["o_d"]], row(group_norm[l]), dmix, "group_norm_bwd")
        (dq_d, dk_d, dv_d), (got[l, "w_up"],) = _stick_bwd(
            S, H, pb, lambda h: cb(lay.QD) + h, lambda h: cb(lay.KD) + h, lambda h: cb(lay.VD) + h,
            do_d, A["tot_d"], HEAD ** -0.5, "stick_bwd", comm=("exchange", [G["w_up"]]))
        (dq_c, dk_c, dv_c, dcc), (got[l, "w_gate_a"],) = _attn_bwd(
            S, H, pb, lambda h: cb(lay.QC) + h, pb, lambda h: cb(lay.KC) + h, pb, lambda h: cb(lay.VC) + h,
            A["o_c"], do_c, A["lse_c"], HEAD ** -0.5, "fox_bwd",
            ccol=A["ccol"], crow=A["crow"], comm=("exchange", [G["w_gate"][:, :D // 2]]))
        dfc, dbias = _fox_bwd(S, H, proj, cb(lay.FC), A["bias"], dcc, "fox_gate_bwd")
        small["fox_forget_bias"] = dbias[0, :H]
        qk_b = A["qk_b"]
        (dq_b, dk_b, dv_b), moved = _attn_bwd(
            S, H, qk_b, lambda h: h, qk_b, lambda h: H + h, pb, lambda h: cb(lay.VB) + h,
            A["o_b"], do_b, A["lse_b"], HEAD ** -0.5, "dilated_bwd" if late else "dilated_bwd_top", tab=tab, win=win,
            comm=("exchange", [G["w_out"]] + ([late["w_uq"], late["w_ukv"]] if late else [])))
        got[l, "w_out"] = moved[0]
        if late:
            got[l + 1, "w_uq"], got[l + 1, "w_ukv"] = moved[1:]
        dqk_b = _rope(jnp.concatenate([dq_b, dk_b], 1), (0, 2 * H, 1), *neg(rope_full), False, BF16, "rope_qk_dil_bwd")
        qab, kv = A["qab"], A["kv"]
        (dqa, dkv, dk2), (got[l, "w_gate_b"],) = _attn_bwd(
            S, H, qab, lambda h: 2 * h, kv, lambda h: 2 * h, kv, lambda h: 2 * h + 1,
            A["o_a"], do_a, A["lse_a"], QKA ** -0.5, "mla_bwd", q2=A["q_pe"], q2cb=lambda h: h, k2=A["k_pe"],
            k2cb=lambda h: 0, rope2=neg(rope_half), comm=("exchange", [G["w_gate"][:, D // 2:]]))
        dk_pe = _rope(dk2.astype(F32).reshape(S, H, LANE).sum(1), (0, 1, 1), *neg(rope_half), True, BF16,
                      "rope_k_mla_bwd")
        dwuq = _mm(A["qln"], dqa, "dw_uq", ta=True, out_dtype=BF16)
        dwuq = dwuq.reshape(Q_LORA, H, 2 * LANE)[:, :, :QKA].reshape(Q_LORA, NDEV, H * QKA // NDEV)
        G["w_uq"] = jnp.transpose(dwuq, (1, 0, 2))
        dwukv = _mm(A["kvln"], dkv, "dw_ukv", ta=True, out_dtype=BF16).reshape(KV_LORA, NDEV, H * 2 * LANE // NDEV)
        G["w_ukv"] = jnp.transpose(dwukv, (1, 0, 2))
        dqln = _mm(dqa, W["wuq"], "q_up_dx", tb=True)
        dkvln = _mm(dkv, W["wukv"], "kv_up_dx", tb=True)
        _, dql, small["mla_q_norm"] = _rms_bwd(proj, row(mla_q_norm[l]), dqln, Q_LORA, cb(lay.QL) // 4, "q_norm_bwd")
        _, dkvl, small["mla_kv_norm"] = _rms_bwd(
            proj, row(mla_kv_norm[l]), dkvln, KV_LORA, cb(lay.KVL) // 4, "kv_norm_bwd")
        dproj = jnp.concatenate([
            dql, dkvl, dqk_b, dv_b, dq_c, dk_c, dv_c, dq_d, dk_d, dv_d,
            dk_pe, dfc.astype(BF16), jnp.zeros((S, PW - lay.FC - LANE), BF16)], axis=1)
        g_in = _mm(A["h1"], dproj, "dw_in", ta=True, out_dtype=BF16).reshape(NDEV, D // NDEV, PW)
        half = D // NDEV // 2
        late = dict(w_in_a=g_in[:, :half], w_in_b=g_in[:, half:], w_uq=G["w_uq"], w_ukv=G["w_ukv"])
        if l == 0:
            names = ["w_in_a", "w_uq", "w_ukv"]
            dh1, moved = _mm(dproj, W["win"], "in_proj_dx_last", tb=True,
                             comm=("exchange", [late[n] for n in names]))
            got.update({(0, n): s for n, s in zip(names, moved)})
        else:
            dh1 = _mm(dproj, W["win"], "in_proj_dx", tb=True)
        dx0, dx0b, small["attn_norm"] = _rms_bwd(A["x0"], row(attn_norm[l]), dh1, D, 0, "attn_norm_bwd", res=dx1)
        return dx0, dx0b, late, got, small

    big = first3 + ["w_out", "w_gate", "w_up", "w_down"]
    Ws, As = [], []
    xc = x
    nxt = _comm_alone("gather", [sh[n][0] for n in first3], "gather_first")
    for l in range(L):
        W = first_weights(nxt)
        xc, A, nxt = forward(l, xc, W)
        Ws.append(W)
        As.append(A)
    dx, loss_part = _final_loss(xc, row(final_norm), target, "final_loss")
    dx, dxb, dfinal = _rms_bwd(xc, row(final_norm), dx, D, 0, "final_norm_bwd")
    slots = {}
    smalls = [None] * L
    late = None
    for l in reversed(range(L)):
        dx, dxb, late, got, smalls[l] = backward(l, dx, dxb, Ws[l], As[l], late)
        slots.update(got)

    names_small = ["attn_norm", "mla_q_norm", "mla_kv_norm", "fox_forget_bias", "group_norm", "ffn_norm"]
    params = dict(attn_norm=attn_norm, mla_q_norm=mla_q_norm, mla_kv_norm=mla_kv_norm, fox_forget_bias=fox_forget_bias,
                  group_norm=group_norm, ffn_norm=ffn_norm, final_norm=final_norm, w_in=w_in, w_uq=w_uq, w_ukv=w_ukv,
                  w_out=w_out, w_gate=w_gate, w_up=w_up, w_down=w_down)
    moms = dict(attn_norm=(m_attn_norm, v_attn_norm), mla_q_norm=(m_mla_q_norm, v_mla_q_norm),
                mla_kv_norm=(m_mla_kv_norm, v_mla_kv_norm), fox_forget_bias=(m_fox_forget_bias, v_fox_forget_bias),
                group_norm=(m_group_norm, v_group_norm), ffn_norm=(m_ffn_norm, v_ffn_norm),
                final_norm=(m_final_norm, v_final_norm), w_in=(m_w_in, v_w_in), w_uq=(m_w_uq, v_w_uq),
                w_ukv=(m_w_ukv, v_w_ukv), w_out=(m_w_out, v_w_out), w_gate=(m_w_gate, v_w_gate),
                w_up=(m_w_up, v_w_up), w_down=(m_w_down, v_w_down))
    small_list = names_small + ["final_norm"]
    small_grads = [jnp.stack([smalls[l][n].reshape(params[n].shape[1:]) for l in range(L)]) for n in names_small]
    small_grads.append(dfinal.reshape(final_norm.shape))
    shapes = [params[n].shape for n in small_list] + [(LANE,)]
    packed_g = _comm_alone("gather", [_pack(small_grads + [loss_part.reshape(LANE)])], "gather_small")[0]
    zero = jnp.zeros((LANE,), F32)
    res_small = _adamw(_pack([params[n] for n in small_list] + [zero]), [packed_g],
                       _pack([moms[n][0] for n in small_list] + [zero]),
                       _pack([moms[n][1] for n in small_list] + [zero]), "adamw_small")
    unp = [_unpack(r, shapes) for r in res_small]
    out = {n: tuple(unp[k][i] for k in range(4)) for i, n in enumerate(small_list)}
    loss = unp[0][-1][0]

    for n in ["w_gate", "w_up", "w_down", "w_out", "w_uq", "w_ukv", "w_in"]:
        if n == "w_in":
            st = [lay.unpad(slots[l, n + h]) for l in range(L) for h in ("_a", "_b")]
        elif n == "w_gate":
            st = [slots[l, n + h] for l in range(L) for h in ("_a", "_b")]
        else:
            st = [slots[l, n] for l in range(L)]
        C = st[0].shape[-1]
        st = [s.reshape(NDEV, -1, C) for s in st]
        args = (params[n].reshape(-1, C), st, moms[n][0].reshape(-1, C), moms[n][1].reshape(-1, C), "adamw_" + n)
        if n == "w_gate":
            res, (slots[0, "w_in_b"],) = _adamw(*args, comm=("exchange", [late["w_in_b"]]))
        else:
            res = _adamw(*args)
        out[n] = tuple(r.reshape(params[n].shape) for r in res)

    order = ["attn_norm", "w_in", "mla_q_norm", "w_uq", "mla_kv_norm", "w_ukv", "fox_forget_bias", "group_norm",
             "w_out", "ffn_norm", "w_gate", "w_up", "w_down", "final_norm"]
    return (loss, dx[None], *[out[n][0] for n in order], *[out[n][1] for n in order],
            *[out[n][2] for n in order], *[out[n][3] for n in order])
```

```python
import functools
import math

import numpy as np
import jax
import jax.numpy as jnp
from jax import lax
from jax.experimental import pallas as pl
from jax.experimental.pallas import tpu as pltpu

F32 = jnp.float32
BF16 = jnp.bfloat16
NDEV = 8
LANE = 128
HEAD = 128
Q_LORA = 512
KV_LORA = 512
QK_ROPE = 64
DILATED_PAIRS = ((128, 1), (512, 4), (2048, 16))
ROPE_THETA = 10000.0
EPS = 1e-6
NEG = -1e30
TQ = 512
TK = 128
TKS = 512
VMEM_LIMIT = 48 * 1024 * 1024
MM_OPERAND_BYTES = 20 * 1024 * 1024
FFN_BLOCKS_PER_STEP = 2
ADAMW_BLOCK_BYTES = 24 * 1024 * 1024
ADAM_LR, ADAM_B1, ADAM_B2, ADAM_EPS, ADAM_WD, ADAM_STEP = 0.001, 0.9, 0.999, 1e-08, 0.01, 10
MESH = pl.DeviceIdType.MESH
ANY = pl.BlockSpec(memory_space=pl.ANY)


def _cp(*sem):
    return pltpu.CompilerParams(dimension_semantics=sem, vmem_limit_bytes=VMEM_LIMIT)


def _dot(a, b, ca, cb):
    return lax.dot_general(a, b, (((ca,), (cb,)), ((), ())), preferred_element_type=F32)


def _dot_nn(a, b):
    return _dot(a, b, 1, 0)


def _dot_nt(a, b):
    return _dot(a, b, 1, 1)


def _dot_tn(a, b):
    return _dot(a, b, 0, 0)


def _tile(n, t):
    if n <= t:
        return n
    t -= t % LANE
    while n % t:
        t -= LANE
    return t


def _direct_copies(ins, outs, send_sems, recv_sems, local_sems, want_recvs=True):
    x, y, c = lax.axis_index("x"), lax.axis_index("y"), lax.axis_index("c")
    my_id = 4 * x + 2 * y + c
    local, sends, recvs = [], [], []
    for a in range(len(ins)):
        mine = ins[a].at[my_id]
        local.append(pltpu.make_async_copy(mine, outs[a].at[my_id], local_sems.at[a]))
        for k in range(1, NDEV):
            peer = (1 - x if k & 4 else x, 1 - y if k & 2 else y, 1 - c if k & 1 else c)
            pid = 4 * peer[0] + 2 * peer[1] + peer[2]
            sems = dict(send_sem=send_sems.at[a, k - 1], recv_sem=recv_sems.at[a, k - 1],
                        device_id=peer, device_id_type=MESH)
            sends.append(pltpu.make_async_remote_copy(src_ref=ins[a].at[pid], dst_ref=outs[a].at[my_id], **sems))
            if want_recvs:
                recvs.append(pltpu.make_async_remote_copy(src_ref=mine, dst_ref=outs[a].at[pid], **sems))
    return local, sends, recvs


def _comm_start(kind, ins, outs, send_sems, recv_sems, local_sems):
    if kind == "exchange":
        local, sends, _ = _direct_copies(ins, outs, send_sems, recv_sems, local_sems, want_recvs=False)
        for cp in local + sends:
            cp.start()
        return
    x, y, c = lax.axis_index("x"), lax.axis_index("y"), lax.axis_index("c")
    for a in range(len(ins)):
        mine = outs[a].at[4 * x + 2 * y + c]
        pltpu.make_async_copy(ins[a], mine, local_sems.at[a]).start()
        for k, to in enumerate([(x, y, 1 - c), (1 - x, y, c), (x, 1 - y, c), (1 - x, 1 - y, c)]):
            pltpu.make_async_remote_copy(src_ref=ins[a], dst_ref=mine, send_sem=send_sems.at[a, k],
                                         recv_sem=recv_sems.at[a, k], device_id=to, device_id_type=MESH).start()


def _comm_finish(kind, ins, outs, send_sems, recv_sems, local_sems):
    if kind == "exchange":
        local, sends, recvs = _direct_copies(ins, outs, send_sems, recv_sems, local_sems)
        for cp in recvs:
            cp.wait_recv()
        for cp in sends:
            cp.wait_send()
        for cp in local:
            cp.wait()
        return
    x, y, c = lax.axis_index("x"), lax.axis_index("y"), lax.axis_index("c")
    sibling = (x, y, 1 - c)
    chips = [(1 - x, y), (x, 1 - y), (1 - x, 1 - y)]
    for a in range(len(ins)):
        def copy(k, block, to):
            rows = outs[a].at[4 * block[0] + 2 * block[1] + block[2]]
            return pltpu.make_async_remote_copy(src_ref=rows, dst_ref=rows, send_sem=send_sems.at[a, k],
                                                recv_sem=recv_sems.at[a, k], device_id=to, device_id_type=MESH)

        passed = []
        for j, chip in enumerate(chips):
            copy(1 + j, (*chip, c), (x, y, c)).wait_recv()
            passed.append(copy(4 + j, (*chip, c), sibling))
            passed[-1].start()
        copy(0, sibling, (x, y, c)).wait_recv()
        for j, chip in enumerate(chips):
            copy(4 + j, (*chip, 1 - c), (x, y, c)).wait_recv()
        for k in range(4):
            copy(k, (x, y, c), sibling).wait_send()
        for cp in passed:
            cp.wait_send()
        pltpu.make_async_copy(ins[a], outs[a].at[4 * x + 2 * y + c], local_sems.at[a]).wait()


def _comm_shapes(kind, arrs):
    out_shape = [jax.ShapeDtypeStruct(((NDEV,) if kind == "gather" else ()) + a.shape, a.dtype) for a in arrs]
    n = len(arrs)
    sems = [pltpu.SemaphoreType.DMA((n, 7)), pltpu.SemaphoreType.DMA((n, 7)), pltpu.SemaphoreType.DMA((n,))]
    return out_shape, sems


def _comm_alone(kind, arrs, name):
    n = len(arrs)

    def body(*refs):
        _comm_start(kind, refs[:n], refs[n:2 * n], *refs[2 * n:])
        _comm_finish(kind, refs[:n], refs[n:2 * n], *refs[2 * n:])

    out_shape, sems = _comm_shapes(kind, arrs)
    return pl.pallas_call(body, name=name, out_shape=out_shape, in_specs=[ANY] * n, out_specs=[ANY] * n,
                          scratch_shapes=sems)(*arrs)


def _pcall(body, *, name, grid, in_specs, out_specs, out_shape, operands, sem, scratch_shapes=(), comm=None):
    in_specs, out_specs, out_shape = list(in_specs), list(out_specs), list(out_shape)
    scratch_shapes = list(scratch_shapes)
    if comm is None:
        res = pl.pallas_call(body, name=name, grid=grid, in_specs=in_specs, out_specs=out_specs, out_shape=out_shape,
                             scratch_shapes=scratch_shapes, compiler_params=_cp(*sem))(*operands)
        return list(res), []
    kind, arrs = comm
    nc, n_in, n_out, n_scr = len(arrs), len(operands), len(out_shape), len(scratch_shapes)
    c_shape, c_sems = _comm_shapes(kind, arrs)

    def carrier(*refs):
        ins, cin = refs[:n_in], refs[n_in:n_in + nc]
        outs = refs[n_in + nc:n_in + nc + n_out]
        cout = refs[n_in + nc + n_out:n_in + 2 * nc + n_out]
        scr = refs[n_in + 2 * nc + n_out:n_in + 2 * nc + n_out + n_scr]
        sems = refs[n_in + 2 * nc + n_out + n_scr:]
        pids = [pl.program_id(d) for d in range(len(grid))]
        first = functools.reduce(jnp.logical_and, [p == 0 for p in pids])
        last = functools.reduce(jnp.logical_and, [p == g - 1 for p, g in zip(pids, grid)])

        @pl.when(first)
        def _():
            _comm_start(kind, cin, cout, *sems)

        body(*ins, *outs, *scr)

        @pl.when(last)
        def _():
            _comm_finish(kind, cin, cout, *sems)

    res = pl.pallas_call(
        carrier, name=name, grid=grid, in_specs=in_specs + [ANY] * nc, out_specs=out_specs + [ANY] * nc,
        out_shape=out_shape + c_shape, scratch_shapes=scratch_shapes + c_sems,
        compiler_params=_cp(*["arbitrary"] * len(grid)))(*operands, *arrs)
    return list(res[:n_out]), list(res[n_out:])


def _mm_call(pairs, grid, a_spec, b_spec, o_spec, out_shape, acc_shape, nk, ca, cb, name,
             res=None, res_spec=None, comm=None):
    npairs = len(pairs)
    multi = isinstance(out_shape, (list, tuple))
    nout = len(out_shape) if multi else 1

    def body(*refs):
        ab = refs[:2 * npairs]
        r_ref = refs[2 * npairs] if res is not None else None
        o_refs, acc = refs[-1 - nout:-1], refs[-1]
        k = pl.program_id(2)

        @pl.when(k == 0)
        def _():
            acc[...] = jnp.zeros_like(acc)

        tot = None
        for p in range(npairs):
            av, bv = ab[2 * p][...].astype(BF16), ab[2 * p + 1][...].astype(BF16)
            if av.ndim == 2:
                terms = [(av, bv)]
            else:
                rows = bv.shape[0] // av.shape[0]
                terms = [(av[q], bv[q] if bv.ndim == 3 else bv[q * rows:(q + 1) * rows]) for q in range(av.shape[0])]
            for at, bt in terms:
                d = _dot(at, bt, ca, cb)
                tot = d if tot is None else tot + d
        acc[...] += tot

        @pl.when(k == nk - 1)
        def _():
            r = acc[...]
            if r_ref is not None:
                r = r + r_ref[...]
            for o_ref in o_refs:
                o_ref[...] = r.astype(o_ref.dtype)

    ops, specs = [], []
    for a, b in pairs:
        ops += [a, b]
        specs += [a_spec, b_spec]
    if res is not None:
        ops.append(res)
        specs.append(res_spec)
    outs, moved = _pcall(
        body, name=name, grid=grid, in_specs=specs, out_specs=[o_spec] * nout,
        out_shape=out_shape if multi else [out_shape], operands=ops,
        scratch_shapes=[pltpu.VMEM(acc_shape, F32)], sem=("parallel", "parallel", "arbitrary"), comm=comm)
    outs = outs if multi else outs[0]
    return outs if comm is None else (outs, moved)


def _k_tile(K, row_bytes, tk=2048):
    tk = _tile(K, tk)
    while 2 * tk * row_bytes > MM_OPERAND_BYTES and tk % 256 == 0:
        tk //= 2
    return tk


def _mm(a, b, name, ta=False, tb=False, out_dtype=F32, res=None, tm=1024, tn=1024, comm=None):
    M, K = (a.shape[1], a.shape[0]) if ta else a.shape
    N = b.shape[0] if tb else b.shape[1]
    tm, tn = _tile(M, tm), _tile(N, tn)
    tk = _k_tile(K, tm * a.dtype.itemsize + tn * b.dtype.itemsize)
    a_spec = pl.BlockSpec((tk, tm), lambda i, j, k: (k, i)) if ta else pl.BlockSpec((tm, tk), lambda i, j, k: (i, k))
    b_spec = pl.BlockSpec((tn, tk), lambda i, j, k: (j, k)) if tb else pl.BlockSpec((tk, tn), lambda i, j, k: (k, j))
    o_spec = pl.BlockSpec((tm, tn), lambda i, j, k: (i, j))
    if isinstance(out_dtype, tuple):
        out_shape = [jax.ShapeDtypeStruct((M, N), d) for d in out_dtype]
    else:
        out_shape = jax.ShapeDtypeStruct((M, N), out_dtype)
    return _mm_call([(a, b)], (M // tm, N // tn, K // tk), a_spec, b_spec, o_spec,
                    out_shape, (tm, tn), K // tk,
                    0 if ta else 1, 1 if tb else 0, name, res=res, res_spec=o_spec, comm=comm)


def _mm_down(act, wd, res, name, tm=1024, tn=1024):
    _, S, FB = act.shape
    D = wd.shape[1]
    tm, tn = _tile(S, tm), _tile(D, tn)
    o_spec = pl.BlockSpec((tm, tn), lambda i, j, k: (i, j))
    nb = FFN_BLOCKS_PER_STEP
    return _mm_call([(act, wd)], (S // tm, D // tn, NDEV // nb),
                    pl.BlockSpec((nb, tm, FB), lambda i, j, k: (k, i, 0)),
                    pl.BlockSpec((nb * FB, tn), lambda i, j, k: (k, j)), o_spec,
                    jax.ShapeDtypeStruct((S, D), F32), (tm, tn), NDEV // nb, 1, 0, name, res=res, res_spec=o_spec)


def _mm_dwdown(act, dy, name, tn=1024):
    _, S, FB = act.shape
    D = dy.shape[1]
    tn = _tile(D, tn)
    tk = _k_tile(S, FB * act.dtype.itemsize + tn * dy.dtype.itemsize)
    return _mm_call([(act, dy)], (NDEV, D // tn, S // tk),
                    pl.BlockSpec((None, tk, FB), lambda i, j, k: (i, k, 0)),
                    pl.BlockSpec((tk, tn), lambda i, j, k: (k, j)),
                    pl.BlockSpec((FB, tn), lambda i, j, k: (i, j)),
                    jax.ShapeDtypeStruct((NDEV * FB, D), BF16), (FB, tn), S // tk, 0, 0, name)


def _mm_dh2(dg, wg, du, wu, name, tm=1024, tn=1024, comm=None):
    _, S, FB = dg.shape
    D = wg.shape[1]
    tm, tn = _tile(S, tm), _tile(D, tn)
    nb = FFN_BLOCKS_PER_STEP
    return _mm_call([(dg, wg), (du, wu)], (S // tm, D // tn, NDEV // nb),
                    pl.BlockSpec((nb, tm, FB), lambda i, j, k: (k, i, 0)),
                    pl.BlockSpec((nb, tn, FB), lambda i, j, k: (k, j, 0)),
                    pl.BlockSpec((tm, tn), lambda i, j, k: (i, j)),
                    jax.ShapeDtypeStruct((S, D), F32), (tm, tn), NDEV // nb, 1, 1, name, comm=comm)


def _mm_dwgate(h2, dg, name, tm=1024, comm=None):
    _, S, FB = dg.shape
    D = h2.shape[1]
    tm = _tile(D, tm)
    tk = _k_tile(S, tm * h2.dtype.itemsize + FB * dg.dtype.itemsize)
    return _mm_call([(h2, dg)], (NDEV, D // tm, S // tk),
                    pl.BlockSpec((tk, tm), lambda p, i, k: (k, i)),
                    pl.BlockSpec((None, tk, FB), lambda p, i, k: (p, k, 0)),
                    pl.BlockSpec((None, tm, FB), lambda p, i, k: (p, i, 0)),
                    jax.ShapeDtypeStruct((NDEV, D, FB), BF16), (tm, FB), S // tk, 0, 0, name, comm=comm)


def _ffn_up(h2, wg, wu, name, tm=512, comm=None):
    S, D = h2.shape
    FB = wg.shape[2]
    tm = _tile(S, tm)

    def body(h_ref, wg_ref, wu_ref, g_ref, u_ref, act_ref):
        h = h_ref[...]
        g = _dot_nn(h, wg_ref[...])
        u = _dot_nn(h, wu_ref[...])
        g_ref[...] = g.astype(BF16)
        u_ref[...] = u.astype(BF16)
        act_ref[...] = (g / (1.0 + jnp.exp(-g)) * u).astype(BF16)

    w_spec = pl.BlockSpec((None, D, FB), lambda p, i: (p, 0, 0))
    o_spec = pl.BlockSpec((None, tm, FB), lambda p, i: (p, i, 0))
    shp = (NDEV, S, FB)
    outs, moved = _pcall(
        body, name=name, grid=(NDEV, S // tm),
        in_specs=[pl.BlockSpec((tm, D), lambda p, i: (i, 0)), w_spec, w_spec],
        out_specs=[o_spec, o_spec, o_spec],
        out_shape=[jax.ShapeDtypeStruct(shp, BF16)] * 3,
        operands=[h2, wg, wu], sem=("parallel", "parallel"), comm=comm)
    return outs if comm is None else (outs, moved)


def _ffn_dact(dy, wd, g, u, name, tm=512, comm=None):
    S, D = dy.shape
    FB = g.shape[2]
    tm = _tile(S, tm)

    def body(dy_ref, wd_ref, g_ref, u_ref, dg_ref, du_ref):
        dact = _dot_nt(dy_ref[...], wd_ref[...])
        gv = g_ref[...].astype(F32)
        sg = pl.reciprocal(1.0 + jnp.exp(-gv), approx=True)
        dg_ref[...] = (dact * u_ref[...].astype(F32) * (sg * (1.0 + gv * (1.0 - sg)))).astype(BF16)
        du_ref[...] = (dact * (gv * sg)).astype(BF16)

    t_spec = pl.BlockSpec((None, tm, FB), lambda p, i: (p, i, 0))
    shp = jax.ShapeDtypeStruct((NDEV, S, FB), BF16)
    outs, moved = _pcall(
        body, name=name, grid=(NDEV, S // tm),
        in_specs=[pl.BlockSpec((tm, D), lambda p, i: (i, 0)), pl.BlockSpec((FB, D), lambda p, i: (p, 0)),
                  t_spec, t_spec],
        out_specs=[t_spec, t_spec], out_shape=[shp, shp],
        operands=[dy, wd, g, u], sem=("parallel", "parallel"), comm=comm)
    return outs if comm is None else (outs, moved)


def _rms_fwd(x, gain, width, cb, out_dtype, name, ts=512):
    S = x.shape[0]
    ts = _tile(S, ts)

    def body(x_ref, g_ref, o_ref):
        xv = x_ref[...]
        r = lax.rsqrt(jnp.mean(xv * xv, axis=1, keepdims=True) + EPS)
        o_ref[...] = (xv * r * g_ref[...]).astype(o_ref.dtype)

    return pl.pallas_call(
        body, name=name, grid=(S // ts,),
        in_specs=[pl.BlockSpec((ts, width), lambda i: (i, cb)), pl.BlockSpec((1, width), lambda i: (0, 0))],
        out_specs=pl.BlockSpec((ts, width), lambda i: (i, 0)),
        out_shape=jax.ShapeDtypeStruct((S, width), out_dtype),
        compiler_params=_cp("parallel"),
    )(x, gain)


def _rms_bwd(x, gain, dy, width, cb, name, res=None, ts=256):
    S = x.shape[0]
    ts = _tile(S, ts)
    has_res = res is not None

    def body(*refs):
        x_ref, g_ref, dy_ref = refs[:3]
        r_ref = refs[3] if has_res else None
        dx_ref, dxb_ref, dg_ref = refs[-3], refs[-2], refs[-1]

        @pl.when(pl.program_id(0) == 0)
        def _():
            dg_ref[...] = jnp.zeros_like(dg_ref)

        xv = x_ref[...]
        r = lax.rsqrt(jnp.mean(xv * xv, axis=1, keepdims=True) + EPS)
        xh = xv * r
        dyv = dy_ref[...]
        dyg = dyv * g_ref[...]
        dx = r * (dyg - xh * jnp.mean(dyg * xh, axis=1, keepdims=True))
        if has_res:
            dx = dx + r_ref[...]
        dx_ref[...] = dx
        dxb_ref[...] = dx.astype(BF16)
        dg_ref[...] += jnp.sum(dyv * xh, axis=0, keepdims=True)

    row = pl.BlockSpec((ts, width), lambda i: (i, 0))
    vec = pl.BlockSpec((1, width), lambda i: (0, 0))
    ops = [x, gain, dy] + ([res] if has_res else [])
    specs = [pl.BlockSpec((ts, width), lambda i: (i, cb)), vec, row] + ([row] if has_res else [])
    return pl.pallas_call(
        body, name=name, grid=(S // ts,), in_specs=specs, out_specs=[row, row, vec],
        out_shape=[jax.ShapeDtypeStruct((S, width), F32), jax.ShapeDtypeStruct((S, width), BF16),
                   jax.ShapeDtypeStruct((1, width), F32)],
        compiler_params=_cp("arbitrary"),
    )(*ops)


def _gn_fwd(outs, gain, name, ts=512):
    S, GW = outs[0].shape
    ts = _tile(S, ts)

    def body(a_ref, b_ref, c_ref, d_ref, g_ref, o_ref):
        for g, r_ref in enumerate((a_ref, b_ref, c_ref, d_ref)):
            xv = r_ref[...]
            r = lax.rsqrt(jnp.mean(xv * xv, axis=1, keepdims=True) + EPS)
            o_ref[:, g * GW:(g + 1) * GW] = (xv * r * g_ref[:, g * GW:(g + 1) * GW]).astype(BF16)

    row = pl.BlockSpec((ts, GW), lambda i: (i, 0))
    return pl.pallas_call(
        body, name=name, grid=(S // ts,),
        in_specs=[row] * 4 + [pl.BlockSpec((1, 4 * GW), lambda i: (0, 0))],
        out_specs=pl.BlockSpec((ts, 4 * GW), lambda i: (i, 0)),
        out_shape=jax.ShapeDtypeStruct((S, 4 * GW), BF16),
        compiler_params=_cp("parallel"),
    )(*outs, gain)


def _gn_bwd(outs, gain, dmix, name, ts=256):
    S, GW = outs[0].shape
    ts = _tile(S, ts)

    def body(a_ref, b_ref, c_ref, d_ref, g_ref, dm_ref, da_ref, db_ref, dc_ref, dd_ref, dg_ref):
        @pl.when(pl.program_id(0) == 0)
        def _():
            dg_ref[...] = jnp.zeros_like(dg_ref)

        for g, (r_ref, o_ref) in enumerate(zip((a_ref, b_ref, c_ref, d_ref), (da_ref, db_ref, dc_ref, dd_ref))):
            sl = slice(g * GW, (g + 1) * GW)
            xv = r_ref[...]
            r = lax.rsqrt(jnp.mean(xv * xv, axis=1, keepdims=True) + EPS)
            xh = xv * r
            dyv = dm_ref[:, sl]
            dyg = dyv * g_ref[:, sl]
            o_ref[...] = (r * (dyg - xh * jnp.mean(dyg * xh, axis=1, keepdims=True))).astype(BF16)
            dg_ref[:, sl] += jnp.sum(dyv * xh, axis=0, keepdims=True)

    row = pl.BlockSpec((ts, GW), lambda i: (i, 0))
    vec = pl.BlockSpec((1, 4 * GW), lambda i: (0, 0))
    return pl.pallas_call(
        body, name=name, grid=(S // ts,),
        in_specs=[row] * 4 + [vec, pl.BlockSpec((ts, 4 * GW), lambda i: (i, 0))],
        out_specs=[row] * 4 + [vec],
        out_shape=[jax.ShapeDtypeStruct((S, GW), BF16)] * 4 + [jax.ShapeDtypeStruct((1, 4 * GW), F32)],
        compiler_params=_cp("arbitrary"),
    )(*outs, gain, dmix)


def _rope_partner(x, half):
    if not half:
        return pltpu.roll(x, 64, 1)
    lane = lax.broadcasted_iota(jnp.int32, x.shape, 1)
    return jnp.where(lane % 64 < 32, pltpu.roll(x, LANE - 32, 1), pltpu.roll(x, 32, 1))


def _rope(x, cbs, cos, sin, half, out_dtype, name, ts=512):
    S = x.shape[0]
    cb0, nb, stride = cbs
    ts = _tile(S, ts)

    def body(x_ref, c_ref, s_ref, o_ref):
        xv = x_ref[...].astype(F32)
        o_ref[...] = (xv * c_ref[...] + _rope_partner(xv, half) * s_ref[...]).astype(o_ref.dtype)

    tab = pl.BlockSpec((ts, LANE), lambda i, j: (i, 0))
    return pl.pallas_call(
        body, name=name, grid=(S // ts, nb),
        in_specs=[pl.BlockSpec((ts, LANE), lambda i, j: (i, cb0 + stride * j)), tab, tab],
        out_specs=pl.BlockSpec((ts, LANE), lambda i, j: (i, j)),
        out_shape=jax.ShapeDtypeStruct((S, nb * LANE), out_dtype),
        compiler_params=_cp("parallel", "parallel"),
    )(x, cos, sin)


def _final_loss(x, gain, target, name, ts=256):
    S, D = x.shape
    ts = _tile(S, ts)

    def body(x_ref, g_ref, t_ref, dy_ref, l_ref):
        @pl.when(pl.program_id(0) == 0)
        def _():
            l_ref[...] = jnp.zeros_like(l_ref)

        xv = x_ref[...]
        r = lax.rsqrt(jnp.mean(xv * xv, axis=1, keepdims=True) + EPS)
        err = xv * r * g_ref[...] - t_ref[...]
        dy_ref[...] = err * (1.0 / D)
        part = jnp.sum(jnp.mean(err * err, axis=1, keepdims=True), axis=0, keepdims=True)
        l_ref[...] += jnp.broadcast_to(0.5 * part, (1, LANE))

    row = pl.BlockSpec((ts, D), lambda i: (i, 0))
    return pl.pallas_call(
        body, name=name, grid=(S // ts,),
        in_specs=[row, pl.BlockSpec((1, D), lambda i: (0, 0)), row],
        out_specs=[row, pl.BlockSpec((1, LANE), lambda i: (0, 0))],
        out_shape=[jax.ShapeDtypeStruct((S, D), F32), jax.ShapeDtypeStruct((1, LANE), F32)],
        compiler_params=_cp("arbitrary"),
    )(x, gain, target)


def _colspec(rows, f):
    return pl.BlockSpec((rows, LANE), f)


def _soft_tiles(S):
    tq = _tile(S, TQ)
    tk = _tile(S, TKS)
    assert tk % tq == 0
    return tq, tk


def _key_row(crow_ref, j, tk):
    n = tk // TK
    return jnp.concatenate([crow_ref[j * n + c] for c in range(n)], axis=1)


def _attn_fwd(S, H, q1, q1cb, k1, k1cb, v, vcb, scale, name, q2=None, q2cb=None, k2=None, k2cb=None,
              tab=None, win=None, ccol=None, crow=None, comm=None):
    tq, tk = _soft_tiles(S)
    has2, hastab, hasc = q2 is not None, tab is not None, ccol is not None

    def body(*refs):
        it = iter(refs)
        q1r, k1r, vr = next(it), next(it), next(it)
        q2r, k2r = (next(it), next(it)) if has2 else (None, None)
        tabr = next(it) if hastab else None
        ccolr, crowr = (next(it), next(it)) if hasc else (None, None)
        o_ref, lse_ref = next(it), next(it)
        i = pl.program_id(1)
        q = q1r[...]
        qb2 = q2r[...] if has2 else None
        cq = ccolr[:, 0:1] if hasc else None
        qpos = i * tq + lax.broadcasted_iota(jnp.int32, (tq, tk), 0)
        kio = lax.broadcasted_iota(jnp.int32, (tq, tk), 1)
        j_diag = (i * tq) // tk
        j_lo = jnp.maximum((i * tq - win) // tk, 0) if win else 0

        if has2:
            q = jnp.concatenate([q, qb2], axis=1)

        def step(j, carry, masked):
            m, l, acc = carry
            off = pl.multiple_of(j * tk, tk)
            kb = k1r[pl.ds(off, tk), :]
            if has2:
                kb = jnp.concatenate([kb, k2r[pl.ds(off, tk), :]], axis=1)
            s = _dot_nt(q, kb) * scale
            if hastab:
                s = s + tabr[i - j * (tk // tq)]
            else:
                if hasc:
                    s = s + (cq - _key_row(crowr, j, tk))
                if masked:
                    s = jnp.where(kio + j * tk <= qpos, s, NEG)
            mn = jnp.maximum(m, jnp.max(s, axis=1, keepdims=True))
            p = jnp.exp(s - mn)
            al = jnp.exp(m - mn)
            l = al * l + jnp.sum(p, axis=1, keepdims=True)
            vb = vr[pl.ds(off, tk), :]
            ph = p.astype(BF16)
            pv = _dot_nn(ph, vb)
            if hasc:
                pv = pv + _dot_nn((p - ph.astype(F32)).astype(BF16), vb)
            acc = al * acc + pv
            return mn, l, acc

        carry = (jnp.full((tq, 1), NEG, F32), jnp.zeros((tq, 1), F32), jnp.zeros((tq, LANE), F32))
        if hastab:
            carry = lax.fori_loop(j_lo, j_diag + 1, functools.partial(step, masked=False), carry)
        else:
            carry = lax.fori_loop(j_lo, j_diag, functools.partial(step, masked=False), carry)
            carry = step(j_diag, carry, True)
        m, l, acc = carry
        o_ref[...] = acc / l
        lse_ref[...] = jnp.broadcast_to(m + jnp.log(l), (tq, LANE))

    ops = [q1, k1, v]
    specs = [_colspec(tq, lambda h, i: (i, q1cb(h))), _colspec(S, lambda h, i: (0, k1cb(h))),
             _colspec(S, lambda h, i: (0, vcb(h)))]
    if has2:
        ops += [q2, k2]
        specs += [_colspec(tq, lambda h, i: (i, q2cb(h))), _colspec(S, lambda h, i: (0, k2cb(h)))]
    if hastab:
        ops.append(tab)
        specs.append(pl.BlockSpec(tab.shape, lambda h, i: (0, 0, 0)))
    if hasc:
        ops += [ccol, crow]
        specs += [_colspec(tq, lambda h, i: (i, h)),
                  pl.BlockSpec((None, S // TK, 1, TK), lambda h, i: (h, 0, 0, 0))]
    o_spec = _colspec(tq, lambda h, i: (i, h))
    shp = jax.ShapeDtypeStruct((S, H * LANE), F32)
    outs, moved = _pcall(
        body, name=name, grid=(H, S // tq), in_specs=specs, out_specs=[o_spec, o_spec], out_shape=[shp, shp],
        operands=ops, sem=("parallel", "arbitrary"), comm=comm)
    return outs if comm is None else (outs, moved)


def _attn_bwd(S, H, q1, q1cb, k1, k1cb, v, vcb, o, do, lse, scale, name, q2=None, q2cb=None, k2=None, k2cb=None,
              rope2=None, tab=None, win=None, ccol=None, crow=None, comm=None):
    tq, tk = _soft_tiles(S)
    has2, hastab, hasc = q2 is not None, tab is not None, ccol is not None

    def body(*refs):
        it = iter(refs)
        q1r, k1r, vr, o_r, do_r, lse_r = (next(it) for _ in range(6))
        q2r, k2r, cos2_r, sin2_r = (next(it), next(it), next(it), next(it)) if has2 else (None,) * 4
        tabr = next(it) if hastab else None
        ccolr, crowr = (next(it), next(it)) if hasc else (None, None)
        dq1_r, dk1_o, dv_o = next(it), next(it), next(it)
        dk2_o = dv_o
        dcr_r = next(it) if hasc else None
        dk1_r, dv_r = next(it), next(it)
        dk2_r = next(it) if has2 else None
        i = pl.program_id(1)

        @pl.when(i == 0)
        def _():
            dk1_r[...] = jnp.zeros_like(dk1_r)
            dv_r[...] = jnp.zeros_like(dv_r)
            if has2:
                dk2_r[...] = jnp.zeros_like(dk2_r)
            if hasc:
                dcr_r[...] = jnp.zeros_like(dcr_r)

        q = q1r[...]
        qb2 = q2r[...] if has2 else None
        dob = do_r[...]
        delta = jnp.sum(dob.astype(F32) * o_r[...], axis=1, keepdims=True)
        lse_c = lse_r[:, 0:1]
        cq = ccolr[:, 0:1] if hasc else None
        qpos = i * tq + lax.broadcasted_iota(jnp.int32, (tq, tk), 0)
        kio = lax.broadcasted_iota(jnp.int32, (tq, tk), 1)
        j_diag = (i * tq) // tk
        j_lo = jnp.maximum((i * tq - win) // tk, 0) if win else 0

        if has2:
            q = jnp.concatenate([q, qb2], axis=1)

        def probs(j, masked):
            off = pl.multiple_of(j * tk, tk)
            kb = k1r[pl.ds(off, tk), :]
            if has2:
                kb = jnp.concatenate([kb, k2r[pl.ds(off, tk), :]], axis=1)
            s = _dot_nt(q, kb) * scale
            if hastab:
                s = s + tabr[i - j * (tk // tq)]
            else:
                if hasc:
                    s = s + (cq - _key_row(crowr, j, tk))
                if masked:
                    s = jnp.where(kio + j * tk <= qpos, s, NEG)
            p = jnp.exp(s - lse_c)
            dp = _dot_nt(dob, vr[pl.ds(off, tk), :])
            return off, kb, p, dp

        def sweep(fn, carry):
            if hastab:
                return lax.fori_loop(j_lo, j_diag + 1, functools.partial(fn, masked=False), carry)
            carry = lax.fori_loop(j_lo, j_diag, functools.partial(fn, masked=False), carry)
            return fn(j_diag, carry, True)

        def step(j, dq, masked):
            off, kb, p, dp = probs(j, masked)
            ds = p * (dp - delta)
            dsb = ds.astype(BF16)
            dk = _dot_tn(dsb, q) * scale
            dk1_r[pl.ds(off, tk), :] += dk[:, :LANE]
            if has2:
                dk2_r[pl.ds(off, tk), :] += dk[:, LANE:]
            dv_r[pl.ds(off, tk), :] += _dot_tn(p.astype(BF16), dob)
            if hasc:
                cs = -jnp.sum(ds, axis=0, keepdims=True)
                for c in range(tk // TK):
                    dcr_r[j * (tk // TK) + c] += cs[:, c * TK:(c + 1) * TK]
            return dq + _dot_nn(dsb, kb)

        dq = sweep(step, jnp.zeros(q.shape, F32)) * scale
        dq1_r[:, :LANE] = dq[:, :LANE].astype(BF16)
        if has2:
            x2 = dq[:, LANE:]
            dq1_r[:, LANE:] = (x2 * cos2_r[...] + _rope_partner(x2, True) * sin2_r[...]).astype(BF16)

        @pl.when(i == S // tq - 1)
        def _():
            dk1_o[:, :LANE] = dk1_r[...].astype(BF16)
            if has2:
                dk1_o[:, LANE:] = dv_r[...].astype(BF16)
                dk2_o[...] = dk2_r[...].astype(BF16)
            else:
                dv_o[...] = dv_r[...].astype(BF16)

    qspec = _colspec(tq, lambda h, i: (i, h))
    kspec = _colspec(S, lambda h, i: (0, h))
    ops = [q1, k1, v, o, do, lse]
    specs = [_colspec(tq, lambda h, i: (i, q1cb(h))), _colspec(S, lambda h, i: (0, k1cb(h))),
             _colspec(S, lambda h, i: (0, vcb(h))), qspec, qspec, qspec]
    if has2:
        ops += [q2, k2, *rope2]
        tab2 = _colspec(tq, lambda h, i: (i, 0))
        specs += [_colspec(tq, lambda h, i: (i, q2cb(h))), _colspec(S, lambda h, i: (0, k2cb(h))), tab2, tab2]
    if hastab:
        ops.append(tab)
        specs.append(pl.BlockSpec(tab.shape, lambda h, i: (0, 0, 0)))
    if hasc:
        ops += [ccol, crow]
        specs += [qspec, pl.BlockSpec((None, S // TK, 1, TK), lambda h, i: (h, 0, 0, 0))]
    assert all(t.dtype == BF16 for t in ops[:3] + [do] + ([q2, k2] if has2 else []))
    shp = jax.ShapeDtypeStruct((S, H * LANE), BF16)
    if has2:
        out_specs = [pl.BlockSpec((tq, 2 * LANE), lambda h, i: (i, h)), pl.BlockSpec((S, 2 * LANE), lambda h, i: (0, h)),
                     kspec]
        wide = jax.ShapeDtypeStruct((S, H * 2 * LANE), BF16)
        out_shape = [wide, wide, shp]
    else:
        out_specs = [qspec, kspec, kspec]
        out_shape = [shp] * 3
    if hasc:
        out_specs.append(pl.BlockSpec((None, S // TK, 1, TK), lambda h, i: (h, 0, 0, 0)))
        out_shape.append(jax.ShapeDtypeStruct((H, S // TK, 1, TK), F32))
    outs, moved = _pcall(
        body, name=name, grid=(H, S // tq), in_specs=specs, out_specs=out_specs, out_shape=out_shape,
        operands=ops, scratch_shapes=[pltpu.VMEM((S, LANE), F32)] * (3 if has2 else 2),
        sem=("parallel", "arbitrary"), comm=comm)
    return outs if comm is None else (outs, moved)


def _scan_matrix(kind):
    j = np.arange(TK)[:, None]
    s = np.arange(TK)[None, :]
    tri = {"suffix_ex": j > s, "prefix_in": j <= s, "prefix_ex": j < s}[kind].astype(np.float32)
    half = np.concatenate([tri, np.ones((TK, TK), np.float32)], axis=1)
    return jnp.asarray(np.concatenate([half, half], axis=0), BF16)


def _scan_mxu(x, mat, carry, reverse, split=True):
    n = x.shape[1] // TK
    hi = x.astype(BF16)
    if split:
        lo = (x - hi.astype(F32)).astype(BF16)
    else:
        mat = mat[:TK]
    parts = [None] * n
    for b in (reversed(range(n)) if reverse else range(n)):
        sl = slice(b * TK, (b + 1) * TK)
        r = _dot_nn(jnp.concatenate([hi[:, sl], lo[:, sl]], axis=1) if split else hi[:, sl], mat)
        parts[b] = r[:, :TK] + carry
        carry = carry + r[:, TK:]
    return jnp.concatenate(parts, axis=1), carry


def _stick_logs(z):
    e = jnp.exp(-jnp.abs(z))
    return e, -jnp.maximum(z, 0.0) - jnp.log(1.0 + e)


def _stick_fwd(S, H, x, qcb, kcb, vcb, scale, name, comm=None):
    tq, tk = _soft_tiles(S)
    assert x.dtype == BF16

    def body(q_r, k_r, v_r, mat_r, o_ref, t_ref):
        i = pl.program_id(1)
        q = q_r[...]
        qpos = i * tq + lax.broadcasted_iota(jnp.int32, (tq, tk), 0)
        lane = lax.broadcasted_iota(jnp.int32, (tq, tk), 1)
        j_diag = (i * tq) // tk

        def step(j, carry, masked):
            c, acc = carry
            off = pl.multiple_of(j * tk, tk)
            z = _dot_nt(q, k_r[pl.ds(off, tk), :]) * scale
            _, lk = _stick_logs(z)
            if masked:
                past = lane + j * tk < qpos
                lk = jnp.where(past, lk, 0.0)
            suf, c = _scan_mxu(lk, mat_r[...], c, True)
            a = jnp.exp(z + lk + suf)
            if masked:
                a = jnp.where(past, a, 0.0)
            acc = acc + _dot_nn(a.astype(BF16), v_r[pl.ds(off, tk), :])
            return c, acc

        carry = step(j_diag, (jnp.zeros((tq, TK), F32), jnp.zeros((tq, LANE), F32)), True)
        c, acc = lax.fori_loop(0, j_diag, lambda jj, cr: step(j_diag - 1 - jj, cr, False), carry)
        o_ref[...] = acc
        t_ref[...] = c

    o_spec = _colspec(tq, lambda h, i: (i, h))
    shp = jax.ShapeDtypeStruct((S, H * LANE), F32)
    mat = _scan_matrix("suffix_ex")
    outs, moved = _pcall(
        body, name=name, grid=(H, S // tq),
        in_specs=[_colspec(tq, lambda h, i: (i, qcb(h))), _colspec(S, lambda h, i: (0, kcb(h))),
                  _colspec(S, lambda h, i: (0, vcb(h))), pl.BlockSpec(mat.shape, lambda h, i: (0, 0))],
        out_specs=[o_spec, o_spec], out_shape=[shp, shp],
        operands=[x, x, x, mat], sem=("parallel", "arbitrary"), comm=comm)
    return outs if comm is None else (outs, moved)


def _stick_bwd(S, H, x, qcb, kcb, vcb, do, tot, scale, name, comm=None):
    tq, tk = _soft_tiles(S)
    assert x.dtype == BF16 and do.dtype == BF16

    def body(q_r, k_r, v_r, do_r, t_r, pin_r, pex_r, dq_r, dk_o, dv_o, dk_r, dv_r):
        i = pl.program_id(1)

        @pl.when(i == 0)
        def _():
            dk_r[...] = jnp.zeros_like(dk_r)
            dv_r[...] = jnp.zeros_like(dv_r)

        q = q_r[...]
        dob = do_r[...]
        total = jnp.concatenate([t_r[...]] * (tk // TK), axis=1)
        qpos = i * tq + lax.broadcasted_iota(jnp.int32, (tq, tk), 0)
        lane = lax.broadcasted_iota(jnp.int32, (tq, tk), 1)
        j_diag = (i * tq) // tk

        def step(j, carry, masked):
            cl, cg, dq = carry
            off = pl.multiple_of(j * tk, tk)
            kb = k_r[pl.ds(off, tk), :]
            z = _dot_nt(q, kb) * scale
            e, lk = _stick_logs(z)
            if masked:
                past = lane + j * tk < qpos
                lk = jnp.where(past, lk, 0.0)
            pre, cl = _scan_mxu(lk, pin_r[...], cl, False)
            a = jnp.exp(z + lk + (total - pre))
            if masked:
                a = jnp.where(past, a, 0.0)
            g = _dot_nt(dob, v_r[pl.ds(off, tk), :]) * a
            gpre, cg = _scan_mxu(g, pex_r[...], cg, False, split=False)
            inv = pl.reciprocal(1.0 + e, approx=True)
            small = e * inv
            pos = z >= 0
            dz = g * jnp.where(pos, small, inv) - jnp.where(pos, inv, small) * gpre
            if masked:
                dz = jnp.where(past, dz, 0.0)
            dzb = dz.astype(BF16)
            dk_r[pl.ds(off, tk), :] += _dot_tn(dzb, q) * scale
            dv_r[pl.ds(off, tk), :] += _dot_tn(a.astype(BF16), dob)
            return cl, cg, dq + _dot_nn(dzb, kb)

        zt = jnp.zeros((tq, TK), F32)
        carry = lax.fori_loop(0, j_diag, functools.partial(step, masked=False), (zt, zt, jnp.zeros((tq, LANE), F32)))
        dq_r[...] = (step(j_diag, carry, True)[2] * scale).astype(BF16)

        @pl.when(i == S // tq - 1)
        def _():
            dk_o[...] = dk_r[...].astype(BF16)
            dv_o[...] = dv_r[...].astype(BF16)

    qspec = _colspec(tq, lambda h, i: (i, h))
    kspec = _colspec(S, lambda h, i: (0, h))
    shp = jax.ShapeDtypeStruct((S, H * LANE), BF16)
    pin, pex = _scan_matrix("prefix_in"), _scan_matrix("prefix_ex")
    mspec = pl.BlockSpec(pin.shape, lambda h, i: (0, 0))
    outs, moved = _pcall(
        body, name=name, grid=(H, S // tq),
        in_specs=[_colspec(tq, lambda h, i: (i, qcb(h))), _colspec(S, lambda h, i: (0, kcb(h))),
                  _colspec(S, lambda h, i: (0, vcb(h))), qspec, qspec, mspec, mspec],
        out_specs=[qspec, kspec, kspec], out_shape=[shp] * 3,
        operands=[x, x, x, do, tot, pin, pex], scratch_shapes=[pltpu.VMEM((S, LANE), F32)] * 2,
        sem=("parallel", "arbitrary"), comm=comm)
    return outs if comm is None else (outs, moved)


def _scan8(x, rows, reverse):
    for sh in (1, 2, 4):
        if reverse:
            x = x + jnp.where(rows + sh < 8, pltpu.roll(x, 8 - sh, 0), 0.0)
        else:
            x = x + jnp.where(rows >= sh, pltpu.roll(x, sh, 0), 0.0)
    return x


def _fox_prep(S, H, proj, fcb, bias, name):
    tk = TK

    def body(f_ref, b_ref, ccol_ref, crow_ref, scr):
        rows = lax.broadcasted_iota(jnp.int32, (8, LANE), 0)

        def step(t, carry):
            off = pl.multiple_of(t * 8, 8)
            xb = f_ref[pl.ds(off, 8), :] + b_ref[...]
            lf = jnp.minimum(xb, 0.0) - jnp.log(1.0 + jnp.exp(-jnp.abs(xb)))
            lf = _scan8(lf, rows, False) + carry
            scr[pl.ds(off, 8), :] = lf
            return lf[7:8, :]

        lax.fori_loop(0, S // 8, step, jnp.zeros((1, LANE), F32))
        for h in range(H):
            ccol_ref[:, h * LANE:(h + 1) * LANE] = jnp.broadcast_to(scr[:, h:h + 1], (S, LANE))

            def tr(t, _):
                off = pl.multiple_of(t * tk, tk)
                blk = ccol_ref[pl.ds(off, tk), h * LANE:(h + 1) * LANE]
                crow_ref[h, t] = blk.T[0:1, :]
                return 0

            lax.fori_loop(0, S // tk, tr, 0)

    return pl.pallas_call(
        body, name=name, grid=(1,),
        in_specs=[_colspec(S, lambda i: (0, fcb)), pl.BlockSpec((1, LANE), lambda i: (0, 0))],
        out_specs=[pl.BlockSpec((S, H * LANE), lambda i: (0, 0)),
                   pl.BlockSpec((H, S // tk, 1, tk), lambda i: (0, 0, 0, 0))],
        out_shape=[jax.ShapeDtypeStruct((S, H * LANE), F32), jax.ShapeDtypeStruct((H, S // tk, 1, tk), F32)],
        scratch_shapes=[pltpu.VMEM((S, LANE), F32)],
        compiler_params=_cp("arbitrary"),
    )(proj, bias)


def _fox_bwd(S, H, proj, fcb, bias, dcr, name):
    tk = TK

    def body(f_ref, b_ref, dcr_ref, df_ref, db_ref, scr):
        rows = lax.broadcasted_iota(jnp.int32, (8, LANE), 0)
        lane_t = lax.broadcasted_iota(jnp.int32, (tk, LANE), 1)
        nb = S // 8

        def tr(t, _):
            off = pl.multiple_of(t * tk, tk)
            d = jnp.zeros((tk, LANE), F32)
            for h in range(H):
                d = d + jnp.where(lane_t == h, jnp.broadcast_to(dcr_ref[h, t], (LANE, tk)).T, 0.0)
            scr[pl.ds(off, tk), :] = d
            return 0

        lax.fori_loop(0, S // tk, tr, 0)

        def step(tt, carry):
            suffix, db = carry
            off = pl.multiple_of((nb - 1 - tt) * 8, 8)
            d = _scan8(scr[pl.ds(off, 8), :], rows, True) + suffix
            xb = f_ref[pl.ds(off, 8), :] + b_ref[...]
            e = jnp.exp(-jnp.abs(xb))
            dx = d * jnp.where(xb >= 0, e, 1.0) / (1.0 + e)
            df_ref[pl.ds(off, 8), :] = dx
            return d[0:1, :], db + jnp.sum(dx, axis=0, keepdims=True)

        z = jnp.zeros((1, LANE), F32)
        _, db = lax.fori_loop(0, nb, step, (z, z))
        db_ref[...] = db

    return pl.pallas_call(
        body, name=name, grid=(1,),
        in_specs=[_colspec(S, lambda i: (0, fcb)), pl.BlockSpec((1, LANE), lambda i: (0, 0)),
                  pl.BlockSpec((H, S // tk, 1, tk), lambda i: (0, 0, 0, 0))],
        out_specs=[pl.BlockSpec((S, LANE), lambda i: (0, 0)), pl.BlockSpec((1, LANE), lambda i: (0, 0))],
        out_shape=[jax.ShapeDtypeStruct((S, LANE), F32), jax.ShapeDtypeStruct((1, LANE), F32)],
        scratch_shapes=[pltpu.VMEM((S, LANE), F32)],
        compiler_params=_cp("arbitrary"),
    )(proj, bias, dcr)


def _adamw(w, slots, m, v, name, comm=None):
    R, C = w.shape
    parts = len(slots)
    rows = R // parts
    row_bytes = 2 * C * (parts * NDEV * slots[0].dtype.itemsize + 7 * 4)
    sub = 32 // slots[0].dtype.itemsize
    tiles = [t for t in range(sub, rows + 1, sub) if rows % t == 0] or [rows]
    tr = max([t for t in tiles if t * row_bytes <= ADAMW_BLOCK_BYTES] or tiles[:1])
    per = R // parts // tr
    c1 = 1.0 - ADAM_B1 ** ADAM_STEP
    c2 = 1.0 - ADAM_B2 ** ADAM_STEP

    def body(*refs):
        w_ref, m_ref, v_ref = refs[:3]
        s_refs = refs[3:3 + parts]
        g_ref, d_ref, nm_ref, nv_ref = refs[3 + parts:]
        part = pl.program_id(0) // per
        for a, s_ref in enumerate(s_refs):
            @pl.when(part == a)
            def _():
                g = s_ref[0].astype(F32)
                for s in range(1, NDEV):
                    g = g + s_ref[s].astype(F32)
                g_ref[...] = g

        g = g_ref[...]
        mn = ADAM_B1 * m_ref[...] + (1.0 - ADAM_B1) * g
        vn = ADAM_B2 * v_ref[...] + (1.0 - ADAM_B2) * (g * g)
        nm_ref[...] = mn
        nv_ref[...] = vn
        d_ref[...] = -ADAM_LR * ((mn / c1) / (jnp.sqrt(vn / c2) + ADAM_EPS) + ADAM_WD * w_ref[...])

    row = pl.BlockSpec((tr, C), lambda i: (i, 0))
    s_specs = [pl.BlockSpec((NDEV, tr, C), lambda i, a=a: (0, jnp.clip(i - a * per, 0, per - 1), 0))
               for a in range(parts)]
    outs, moved = _pcall(
        body, name=name, grid=(R // tr,), in_specs=[row, row, row] + s_specs,
        out_specs=[row] * 4, out_shape=[jax.ShapeDtypeStruct((R, C), F32)] * 4,
        operands=[w, m, v, *slots], sem=("arbitrary",), comm=comm)
    return outs if comm is None else (outs, moved)


class _Layout:
    def __init__(self, D):
        self.GW = GW = D // 4
        self.H = H = GW // HEAD
        self.QL, self.KVL = 0, Q_LORA
        base = Q_LORA + KV_LORA
        (self.QB, self.KB, self.VB, self.QC, self.KC, self.VC, self.QD, self.KD, self.VD) = (
            base + k * GW for k in range(9))
        self.KR = base + 9 * GW
        self.FC = self.KR + LANE
        self.PW = -(-(self.FC + LANE) // 512) * 512
        self.o_kr = base
        self.o_bc = base + QK_ROPE
        self.o_fc = self.o_bc + 6 * GW
        self.o_d = self.o_fc + H
        self.IN = self.o_d + 3 * GW

    def pad(self, w):
        z = lambda n: jnp.zeros(w.shape[:-1] + (n,), w.dtype)
        return jnp.concatenate([
            w[..., :self.o_kr], w[..., self.o_bc:self.o_fc], w[..., self.o_d:self.IN],
            w[..., self.o_kr:self.o_bc], z(LANE - QK_ROPE), w[..., self.o_fc:self.o_d], z(LANE - self.H),
            z(self.PW - self.FC - LANE)], axis=-1)

    def unpad(self, g):
        return jnp.concatenate([
            g[..., :self.KR - 9 * self.GW], g[..., self.KR:self.KR + QK_ROPE], g[..., self.QB:self.QD],
            g[..., self.FC:self.FC + self.H], g[..., self.QD:self.KR]], axis=-1)


def _rope_tables(S):
    pos = jnp.arange(S, dtype=F32)

    def cs(dim):
        inv = ROPE_THETA ** (-jnp.arange(0, dim, 2, dtype=F32) / dim)
        ang = pos[:, None] * inv[None, :]
        return jnp.cos(ang), jnp.sin(ang)

    c, s = cs(HEAD)
    full = (jnp.concatenate([c, c], 1), jnp.concatenate([-s, s], 1))
    c, s = cs(QK_ROPE)
    z = jnp.zeros((S, LANE - QK_ROPE), F32)
    half = (jnp.concatenate([c, c, z], 1), jnp.concatenate([-s, s, z], 1))
    return full, half


def _dilated_table(tq, tk):
    win = max(w for w, _ in DILATED_PAIRS)
    nd = (win + tk) // tq + 1
    d = np.arange(nd)[:, None, None] * tq + np.arange(tq)[None, :, None] - np.arange(tk)[None, None, :]
    mult = np.zeros(d.shape, np.float64)
    for w, dil in DILATED_PAIRS:
        mult += (d >= 0) & (d <= w) & (d % dil == 0)
    return jnp.asarray(np.where(mult > 0, np.log(np.maximum(mult, 1.0)), NEG), F32), win


def _pack(arrs):
    rows = []
    for a in arrs:
        f = a.reshape(-1).astype(F32)
        f = jnp.pad(f, (0, (-f.shape[0]) % LANE))
        rows.append(f.reshape(-1, LANE))
    p = jnp.concatenate(rows, 0)
    return jnp.pad(p, ((0, (-p.shape[0]) % 8), (0, 0)))


def _unpack(p, shapes):
    out, r = [], 0
    for shp in shapes:
        n = int(np.prod(shp))
        nr = -(-n // LANE)
        out.append(p[r:r + nr].reshape(-1)[:n].reshape(shp))
        r += nr
    return out


def kernel(x, attn_norm, w_in, mla_q_norm, w_uq, mla_kv_norm, w_ukv, fox_forget_bias, group_norm, w_out, ffn_norm, w_gate, w_up, w_down, final_norm, loss_target, m_attn_norm, m_w_in, m_mla_q_norm, m_w_uq, m_mla_kv_norm, m_w_ukv, m_fox_forget_bias, m_group_norm, m_w_out, m_ffn_norm, m_w_gate, m_w_up, m_w_down, m_final_norm, v_attn_norm, v_w_in, v_mla_q_norm, v_w_uq, v_mla_kv_norm, v_w_ukv, v_fox_forget_bias, v_group_norm, v_w_out, v_ffn_norm, v_w_gate, v_w_up, v_w_down, v_final_norm):
    _, S, D = x.shape
    L = attn_norm.shape[0]
    lay = _Layout(D)
    H, GW, PW = lay.H, lay.GW, lay.PW
    FB = w_gate.shape[2]
    QKA = HEAD + QK_ROPE
    x = x[0]
    target = loss_target[0]
    rope_full, rope_half = _rope_tables(S)
    neg = lambda t: (t[0], -t[1])
    tab, win = _dilated_table(*_soft_tiles(S))
    cb = lambda col: col // LANE

    sh = dict(w_in=lay.pad(w_in).astype(BF16),
              **{n: w.astype(BF16) for n, w in (("w_uq", w_uq), ("w_ukv", w_ukv), ("w_out", w_out),
                                                  ("w_gate", w_gate), ("w_up", w_up), ("w_down", w_down))})
    first3 = ["w_in", "w_uq", "w_ukv"]

    def first_weights(g):
        wuq = jnp.transpose(g[1], (1, 0, 2)).reshape(Q_LORA, H, QKA)
        wuq = jnp.pad(wuq, ((0, 0), (0, 0), (0, 2 * LANE - QKA))).reshape(Q_LORA, H * 2 * LANE)
        return dict(win=g[0].reshape(D, PW), wuq=wuq,
                    wukv=jnp.transpose(g[2], (1, 0, 2)).reshape(KV_LORA, H * 2 * LANE))

    def row(a):
        return a.reshape(1, -1)

    def forward(l, x0, W):
        A = dict(x0=x0)
        A["bias"] = jnp.pad(row(fox_forget_bias[l]), ((0, 0), (0, LANE - H)))
        h1 = A["h1"] = _rms_fwd(x0, row(attn_norm[l]), D, 0, BF16, "attn_norm")
        (proj, pb), (W["wg"],) = _mm(h1, W["win"], "in_proj", out_dtype=(F32, BF16),
                                     comm=("gather", [sh["w_gate"][l]]))
        A["proj"], A["pb"] = proj, pb
        qln = A["qln"] = _rms_fwd(proj, row(mla_q_norm[l]), Q_LORA, cb(lay.QL) // 4, BF16, "q_norm")
        kvln = A["kvln"] = _rms_fwd(proj, row(mla_kv_norm[l]), KV_LORA, cb(lay.KVL) // 4, BF16, "kv_norm")
        qa, qab = _mm(qln, W["wuq"], "q_up", out_dtype=(F32, BF16))
        A["qab"] = qab
        kv = A["kv"] = _mm(kvln, W["wukv"], "kv_up", out_dtype=BF16)
        q_pe = A["q_pe"] = _rope(qa, (1, H, 2), *rope_half, True, BF16, "rope_q_mla")
        k_pe = A["k_pe"] = _rope(proj, (cb(lay.KR), 1, 1), *rope_half, True, BF16, "rope_k_mla")
        (A["o_a"], A["lse_a"]), (W["wu"],) = _attn_fwd(
            S, H, qab, lambda h: 2 * h, kv, lambda h: 2 * h, kv, lambda h: 2 * h + 1, QKA ** -0.5, "mla_fwd",
            q2=q_pe, q2cb=lambda h: h, k2=k_pe, k2cb=lambda h: 0, comm=("gather", [sh["w_up"][l]]))
        qk_b = A["qk_b"] = _rope(proj, (cb(lay.QB), 2 * H, 1), *rope_full, False, BF16, "rope_qk_dil")
        (A["o_b"], A["lse_b"]), (g_down,) = _attn_fwd(
            S, H, qk_b, lambda h: h, qk_b, lambda h: H + h, pb, lambda h: cb(lay.VB) + h, HEAD ** -0.5,
            "dilated_fwd", tab=tab, win=win, comm=("gather", [sh["w_down"][l]]))
        W["wd"] = g_down.reshape(NDEV * FB, D)
        ccol, crow = A["ccol"], A["crow"] = _fox_prep(S, H, proj, cb(lay.FC), A["bias"], "fox_prep")
        (A["o_c"], A["lse_c"]), (g_out,) = _attn_fwd(
            S, H, pb, lambda h: cb(lay.QC) + h, pb, lambda h: cb(lay.KC) + h, pb, lambda h: cb(lay.VC) + h,
            HEAD ** -0.5, "fox_fwd", ccol=ccol, crow=crow, comm=("gather", [sh["w_out"][l]]))
        W["wout"] = g_out.reshape(4 * GW, D)
        A["o_d"], A["tot_d"] = _stick_fwd(
            S, H, pb, lambda h: cb(lay.QD) + h, lambda h: cb(lay.KD) + h, lambda h: cb(lay.VD) + h,
            HEAD ** -0.5, "stick_fwd")
        mix = A["mix"] = _gn_fwd([A["o_a"], A["o_b"], A["o_c"], A["o_d"]], row(group_norm[l]), "group_norm")
        x1 = A["x1"] = _mm(mix, W["wout"], "out_proj", res=x0)
        h2 = A["h2"] = _rms_fwd(x1, row(ffn_norm[l]), D, 0, BF16, "ffn_norm")
        nxt = None
        if l + 1 < L:
            (A["g"], A["u"], A["act"]), nxt = _ffn_up(
                h2, W["wg"], W["wu"], "ffn_up", comm=("gather", [sh[n][l + 1] for n in first3]))
        else:
            A["g"], A["u"], A["act"] = _ffn_up(h2, W["wg"], W["wu"], "ffn_up_last")
        return _mm_down(A["act"], W["wd"], x1, "ffn_down"), A, nxt

    def backward(l, dx2, dx2b, W, A, late):
        proj, pb = A["proj"], A["pb"]
        G, small, got = {}, {}, {}
        g_down = _mm_dwdown(A["act"], dx2b, "dw_down").reshape(NDEV, FB, D)
        (dgate, dup), (got[l, "w_down_a"],) = _ffn_dact(
            dx2b, W["wd"], A["g"], A["u"], "ffn_dact", comm=("exchange", [g_down[:, :FB // 2]]))
        dh2, (got[l, "w_down_b"],) = _mm_dh2(
            dgate, W["wg"], dup, W["wu"], "ffn_dh", comm=("exchange", [g_down[:, FB // 2:]]))
        if late is None:
            G["w_gate"] = _mm_dwgate(A["h2"], dgate, "dw_gate_top")
            G["w_up"] = _mm_dwgate(A["h2"], dup, "dw_up_top")
        else:
            G["w_gate"], (got[l + 1, "w_in_a"],) = _mm_dwgate(
                A["h2"], dgate, "dw_gate", comm=("exchange", [late["w_in_a"]]))
            G["w_up"], (got[l + 1, "w_in_b"],) = _mm_dwgate(
                A["h2"], dup, "dw_up", comm=("exchange", [late["w_in_b"]]))
        dx1, dx1b, small["ffn_norm"] = _rms_bwd(A["x1"], row(ffn_norm[l]), dh2, D, 0, "ffn_norm_bwd", res=dx2)
        dmix = _mm(dx1b, W["wout"], "out_proj_dx", tb=True)
        G["w_out"] = _mm(A["mix"], dx1b, "dw_out", ta=True, out_dtype=BF16).reshape(NDEV, 4 * GW // NDEV, D)
        do_a, do_b, do_c, do_d, small["group_norm"] = _gn_bwd(
            [A["o_a"], A["o_b"], A["o_c"], A["o_d"]], row(group_norm[l]), dmix, "group_norm_bwd")
        (dq_d, dk_d, dv_d), (got[l, "w_up"],) = _stick_bwd(
            S, H, pb, lambda h: cb(lay.QD) + h, lambda h: cb(lay.KD) + h, lambda h: cb(lay.VD) + h,
            do_d, A["tot_d"], HEAD ** -0.5, "stick_bwd", comm=("exchange", [G["w_up"]]))
        (dq_c, dk_c, dv_c, dcc), (got[l, "w_gate_a"],) = _attn_bwd(
            S, H, pb, lambda h: cb(lay.QC) + h, pb, lambda h: cb(lay.KC) + h, pb, lambda h: cb(lay.VC) + h,
            A["o_c"], do_c, A["lse_c"], HEAD ** -0.5, "fox_bwd",
            ccol=A["ccol"], crow=A["crow"], comm=("exchange", [G["w_gate"][:, :D // 2]]))
        dfc, dbias = _fox_bwd(S, H, proj, cb(lay.FC), A["bias"], dcc, "fox_gate_bwd")
        small["fox_forget_bias"] = dbias[0, :H]
        qk_b = A["qk_b"]
        (dq_b, dk_b, dv_b), moved = _attn_bwd(
            S, H, qk_b, lambda h: h, qk_b, lambda h: H + h, pb, lambda h: cb(lay.VB) + h,
            A["o_b"], do_b, A["lse_b"], HEAD ** -0.5, "dilated_bwd" if late else "dilated_bwd_top", tab=tab, win=win,
            comm=("exchange", [G["w_out"]] + ([late["w_uq"], late["w_ukv"]] if late else [])))
        got[l, "w_out"] = moved[0]
        if late:
            got[l + 1, "w_uq"], got[l + 1, "w_ukv"] = moved[1:]
        dqk_b = _rope(jnp.concatenate([dq_b, dk_b], 1), (0, 2 * H, 1), *neg(rope_full), False, BF16, "rope_qk_dil_bwd")
        qab, kv = A["qab"], A["kv"]
        (dqa, dkv, dk2), (got[l, "w_gate_b"],) = _attn_bwd(
            S, H, qab, lambda h: 2 * h, kv, lambda h: 2 * h, kv, lambda h: 2 * h + 1,
            A["o_a"], do_a, A["lse_a"], QKA ** -0.5, "mla_bwd", q2=A["q_pe"], q2cb=lambda h: h, k2=A["k_pe"],
            k2cb=lambda h: 0, rope2=neg(rope_half), comm=("exchange", [G["w_gate"][:, D // 2:]]))
        dk_pe = _rope(dk2.astype(F32).reshape(S, H, LANE).sum(1), (0, 1, 1), *neg(rope_half), True, BF16,
                      "rope_k_mla_bwd")
        dwuq = _mm(A["qln"], dqa, "dw_uq", ta=True, out_dtype=BF16)
        dwuq = dwuq.reshape(Q_LORA, H, 2 * LANE)[:, :, :QKA].reshape(Q_LORA, NDEV, H * QKA // NDEV)
        G["w_uq"] = jnp.transpose(dwuq, (1, 0, 2))
        dwukv = _mm(A["kvln"], dkv, "dw_ukv", ta=True, out_dtype=BF16).reshape(KV_LORA, NDEV, H * 2 * LANE // NDEV)
        G["w_ukv"] = jnp.transpose(dwukv, (1, 0, 2))
        dqln = _mm(dqa, W["wuq"], "q_up_dx", tb=True)
        dkvln = _mm(dkv, W["wukv"], "kv_up_dx", tb=True)
        _, dql, small["mla_q_norm"] = _rms_bwd(proj, row(mla_q_norm[l]), dqln, Q_LORA, cb(lay.QL) // 4, "q_norm_bwd")
        _, dkvl, small["mla_kv_norm"] = _rms_bwd(
            proj, row(mla_kv_norm[l]), dkvln, KV_LORA, cb(lay.KVL) // 4, "kv_norm_bwd")
        dproj = jnp.concatenate([
            dql, dkvl, dqk_b, dv_b, dq_c, dk_c, dv_c, dq_d, dk_d, dv_d,
            dk_pe, dfc.astype(BF16), jnp.zeros((S, PW - lay.FC - LANE), BF16)], axis=1)
        g_in = _mm(A["h1"], dproj, "dw_in", ta=True, out_dtype=BF16).reshape(NDEV, D // NDEV, PW)
        half = D // NDEV // 2
        late = dict(w_in_a=g_in[:, :half], w_in_b=g_in[:, half:], w_uq=G["w_uq"], w_ukv=G["w_ukv"])
        if l == 0:
            names = ["w_in_a", "w_uq", "w_ukv"]
            dh1, moved = _mm(dproj, W["win"], "in_proj_dx_last", tb=True,
                             comm=("exchange", [late[n] for n in names]))
            got.update({(0, n): s for n, s in zip(names, moved)})
        else:
            dh1 = _mm(dproj, W["win"], "in_proj_dx", tb=True)
        dx0, dx0b, small["attn_norm"] = _rms_bwd(A["x0"], row(attn_norm[l]), dh1, D, 0, "attn_norm_bwd", res=dx1)
        return dx0, dx0b, late, got, small

    big = first3 + ["w_out", "w_gate", "w_up", "w_down"]
    Ws, As = [], []
    xc = x
    nxt = _comm_alone("gather", [sh[n][0] for n in first3], "gather_first")
    for l in range(L):
        W = first_weights(nxt)
        xc, A, nxt = forward(l, xc, W)
        Ws.append(W)
        As.append(A)
    dx, loss_part = _final_loss(xc, row(final_norm), target, "final_loss")
    dx, dxb, dfinal = _rms_bwd(xc, row(final_norm), dx, D, 0, "final_norm_bwd")
    slots = {}
    smalls = [None] * L
    late = None
    for l in reversed(range(L)):
        dx, dxb, late, got, smalls[l] = backward(l, dx, dxb, Ws[l], As[l], late)
        slots.update(got)

    names_small = ["attn_norm", "mla_q_norm", "mla_kv_norm", "fox_forget_bias", "group_norm", "ffn_norm"]
    params = dict(attn_norm=attn_norm, mla_q_norm=mla_q_norm, mla_kv_norm=mla_kv_norm, fox_forget_bias=fox_forget_bias,
                  group_norm=group_norm, ffn_norm=ffn_norm, final_norm=final_norm, w_in=w_in, w_uq=w_uq, w_ukv=w_ukv,
                  w_out=w_out, w_gate=w_gate, w_up=w_up, w_down=w_down)
    moms = dict(attn_norm=(m_attn_norm, v_attn_norm), mla_q_norm=(m_mla_q_norm, v_mla_q_norm),
                mla_kv_norm=(m_mla_kv_norm, v_mla_kv_norm), fox_forget_bias=(m_fox_forget_bias, v_fox_forget_bias),
                group_norm=(m_group_norm, v_group_norm), ffn_norm=(m_ffn_norm, v_ffn_norm),
                final_norm=(m_final_norm, v_final_norm), w_in=(m_w_in, v_w_in), w_uq=(m_w_uq, v_w_uq),
                w_ukv=(m_w_ukv, v_w_ukv), w_out=(m_w_out, v_w_out), w_gate=(m_w_gate, v_w_gate),
                w_up=(m_w_up, v_w_up), w_down=(m_w_down, v_w_down))
    small_list = names_small + ["final_norm"]
    small_grads = [jnp.stack([smalls[l][n].reshape(params[n].shape[1:]) for l in range(L)]) for n in names_small]
    small_grads.append(dfinal.reshape(final_norm.shape))
    shapes = [params[n].shape for n in small_list] + [(LANE,)]
    packed_g = _comm_alone("gather", [_pack(small_grads + [loss_part.reshape(LANE)])], "gather_small")[0]
    zero = jnp.zeros((LANE,), F32)
    res_small = _adamw(_pack([params[n] for n in small_list] + [zero]), [packed_g],
                       _pack([moms[n][0] for n in small_list] + [zero]),
                       _pack([moms[n][1] for n in small_list] + [zero]), "adamw_small")
    unp = [_unpack(r, shapes) for r in res_small]
    out = {n: tuple(unp[k][i] for k in range(4)) for i, n in enumerate(small_list)}
    loss = unp[0][-1][0]

    for n in ["w_gate", "w_up", "w_down", "w_out", "w_uq", "w_ukv", "w_in"]:
        if n == "w_in":
            st = [lay.unpad(slots[l, n + h]) for l in range(L) for h in ("_a", "_b")]
        elif n in ("w_gate", "w_down"):
            st = [slots[l, n + h] for l in range(L) for h in ("_a", "_b")]
        else:
            st = [slots[l, n] for l in range(L)]
        C = st[0].shape[-1]
        st = [s.reshape(NDEV, -1, C) for s in st]
        args = (params[n].reshape(-1, C), st, moms[n][0].reshape(-1, C), moms[n][1].reshape(-1, C), "adamw_" + n)
        if n == "w_gate":
            res, (slots[0, "w_in_b"],) = _adamw(*args, comm=("exchange", [late["w_in_b"]]))
        else:
            res = _adamw(*args)
        out[n] = tuple(r.reshape(params[n].shape) for r in res)

    order = ["attn_norm", "w_in", "mla_q_norm", "w_uq", "mla_kv_norm", "w_ukv", "fox_forget_bias", "group_norm",
             "w_out", "ffn_norm", "w_gate", "w_up", "w_down", "final_norm"]
    return (loss, dx[None], *[out[n][0] for n in order], *[out[n][1] for n in order],
            *[out[n][2] for n in order], *[out[n][3] for n in order])
```
